```python
import jax, jax.numpy as jnp
from jax import lax
import numpy as np

D_MODEL = 1024
BATCH = 8
SEQ = 8192
DEPTH = 1

CONV_W = D_MODEL // 2
N_HEADS = 8
HEAD_DIM = (D_MODEL - CONV_W) // N_HEADS
ATTN_W = N_HEADS * HEAD_DIM
IN_COLS = 3 * CONV_W + 3 * ATTN_W
CONV_K = 3
D_FF = 2816
PLE_DIM = 256
DILATED_PAIRS = ((128, 1), (512, 4), (2048, 16))
BLOCK = 128
EPS = 1e-6

kernel_name = 'hybrid_conv_dilated_attn_convffn_ple'


def rmsnorm(a, g):
    af = a.astype(jnp.float32)
    af = af * lax.rsqrt(jnp.mean(af * af, axis=-1, keepdims=True) + EPS)
    return (af * g.astype(jnp.float32)).astype(a.dtype)


def causal_dwconv3(u, w, b):
    up = jnp.pad(u, ((0, 0), (CONV_K - 1, 0), (0, 0)))
    t = u.shape[1]
    return up[:, 0:t] * w[0] + up[:, 1:t + 1] * w[1] + up[:, 2:t + 2] * w[2] + b


def alibi_slopes(n):
    return jnp.exp2(-8.0 * jnp.arange(1, n + 1, dtype=jnp.float32) / n)


def dilated_branch(q, k, v, slopes, window, dilation):
    b, t, h, hd = q.shape
    steps = window // dilation
    L = t // dilation
    nb = -(-L // BLOCK)
    lp = nb * BLOCK

    def to_blocks(a):
        a = a.reshape(b, L, dilation, h, hd).transpose(0, 2, 3, 1, 4)
        a = jnp.pad(a, ((0, 0), (0, 0), (0, 0), (0, lp - L), (0, 0)))
        return a.reshape(b, dilation, h, nb, BLOCK, hd)

    def with_prev(a):
        prev = jnp.pad(a[:, :, :, :-1], ((0, 0), (0, 0), (0, 0), (1, 0), (0, 0), (0, 0)))
        return jnp.concatenate([prev, a], axis=4)

    qb = to_blocks(q)
    kk = with_prev(to_blocks(k))
    vv = with_prev(to_blocks(v))

    s = jnp.einsum('brhnqd,brhnkd->brhnqk', qb, kk) * (hd ** -0.5)
    qi = jnp.arange(BLOCK)[:, None] + BLOCK
    kj = jnp.arange(2 * BLOCK)[None, :]
    step = qi - kj
    band = (step >= 0) & (step <= steps)
    first = (jnp.arange(nb)[:, None, None] == 0) & (kj < BLOCK)[None]
    mask = band[None] & ~first
    dist = (step * dilation).astype(jnp.float32)
    bias = -slopes[:, None, None] * dist[None]
    s = jnp.where(mask, s + bias[:, None], -jnp.inf)
    m = jnp.max(s, axis=-1, keepdims=True)
    e = jnp.exp(s - m)
    den = jnp.sum(e, axis=-1, keepdims=True)
    o = jnp.einsum('brhnqk,brhnkd->brhnqd', e, vv) / den
    lse = (m + jnp.log(den))[..., 0]

    o = o.reshape(b, dilation, h, lp, hd)[:, :, :, :L]
    o = o.transpose(0, 3, 1, 2, 4).reshape(b, t, h, hd)
    lse = lse.reshape(b, dilation, h, lp)[:, :, :, :L]
    lse = lse.transpose(0, 3, 1, 2).reshape(b, t, h)
    return o, lse


def dilated_attention(q, k, v):
    slopes = alibi_slopes(q.shape[2])
    outs, lses = [], []
    for window, dilation in DILATED_PAIRS:
        o, lse = dilated_branch(q, k, v, slopes, window, dilation)
        outs.append(o)
        lses.append(lse)
    wts = jax.nn.softmax(jnp.stack(lses, axis=0), axis=0)
    o = jnp.sum(wts[..., None] * jnp.stack(outs, axis=0), axis=0)
    return o


def _fwd_setup_inputs(seed: int = 0) -> dict:
    key = jax.random.key(seed)
    ks = jax.random.split(key, 24)
    f32 = jnp.float32

    def nrm(k, shape, fan):
        return jax.random.normal(k, shape, f32) * (fan ** -0.5)

    def gain(k, shape):
        return 1.0 + 0.05 * jax.random.normal(k, shape, f32)

    def bias(k, shape):
        return 0.01 * jax.random.normal(k, shape, f32)

    return {
        'x': jax.random.normal(ks[0], (BATCH, SEQ, D_MODEL), f32),
        'p': jax.random.normal(ks[1], (DEPTH, BATCH, SEQ, PLE_DIM), f32),
        'g_mix': gain(ks[2], (DEPTH, D_MODEL)),
        'w_in': nrm(ks[3], (DEPTH, D_MODEL, IN_COLS), D_MODEL),
        'conv_w': nrm(ks[4], (DEPTH, CONV_K, CONV_W), CONV_K),
        'conv_b': bias(ks[5], (DEPTH, CONV_W)),
        'q_norm_g': gain(ks[6], (DEPTH, HEAD_DIM)),
        'k_norm_g': gain(ks[7], (DEPTH, HEAD_DIM)),
        'g_out_conv': gain(ks[8], (DEPTH, CONV_W)),
        'g_out_attn': gain(ks[9], (DEPTH, ATTN_W)),
        'w_out': nrm(ks[10], (DEPTH, CONV_W + ATTN_W, D_MODEL), CONV_W + ATTN_W),
        'g_ffn': gain(ks[11], (DEPTH, D_MODEL)),
        'w_gate': nrm(ks[12], (DEPTH, D_MODEL, D_FF), D_MODEL),
        'w_up': nrm(ks[13], (DEPTH, D_MODEL, D_FF), D_MODEL),
        'ffn_conv_w': nrm(ks[14], (DEPTH, CONV_K, D_FF), CONV_K),
        'ffn_conv_b': bias(ks[15], (DEPTH, D_FF)),
        'w_down': nrm(ks[16], (DEPTH, D_FF, D_MODEL), D_FF),
        'g_ple': gain(ks[17], (DEPTH, D_MODEL)),
        'w_ple_gate': nrm(ks[18], (DEPTH, D_MODEL, D_MODEL), D_MODEL),
        'w_ple_proj': nrm(ks[19], (DEPTH, PLE_DIM, D_MODEL), PLE_DIM),
    }


def _fwd_reference(x, p, g_mix, w_in, conv_w, conv_b, q_norm_g, k_norm_g, g_out_conv,
              g_out_attn, w_out, g_ffn, w_gate, w_up, ffn_conv_w, ffn_conv_b, w_down,
              g_ple, w_ple_gate, w_ple_proj):
    b, t, _ = x.shape
    for i in range(DEPTH):
        h = rmsnorm(x, g_mix[i])
        z = h @ w_in[i]
        zb, zc, zx, zq, zk, zv = jnp.split(
            z, np.cumsum([CONV_W, CONV_W, CONV_W, ATTN_W, ATTN_W]), axis=-1)
        y_c = zb * causal_dwconv3(zc * zx, conv_w[i], conv_b[i])
        q = rmsnorm(zq.reshape(b, t, N_HEADS, HEAD_DIM), q_norm_g[i]).astype(jnp.float32)
        k = rmsnorm(zk.reshape(b, t, N_HEADS, HEAD_DIM), k_norm_g[i]).astype(jnp.float32)
        v = zv.reshape(b, t, N_HEADS, HEAD_DIM).astype(jnp.float32)
        y_a = dilated_attention(q, k, v).reshape(b, t, ATTN_W).astype(x.dtype)
        y = jnp.concatenate([rmsnorm(y_c, g_out_conv[i]), rmsnorm(y_a, g_out_attn[i])], axis=-1)
        x = x + y @ w_out[i]
        h = rmsnorm(x, g_ffn[i])
        gate = causal_dwconv3(h @ w_gate[i], ffn_conv_w[i], ffn_conv_b[i])
        x = x + (jax.nn.silu(gate) * (h @ w_up[i])) @ w_down[i]
        ple_gate = jax.nn.sigmoid(rmsnorm(x, g_ple[i]) @ w_ple_gate[i])
        x = x + ple_gate * (p[i].astype(x.dtype) @ w_ple_proj[i])
    return x


import jax as _jax
import jax.numpy as _jnp

TWIN_FORMAT = 'train_step'
FWD_PARAMS = ['x', 'p', 'g_mix', 'w_in', 'conv_w', 'conv_b', 'q_norm_g', 'k_norm_g', 'g_out_conv', 'g_out_attn', 'w_out', 'g_ffn', 'w_gate', 'w_up', 'ffn_conv_w', 'ffn_conv_b', 'w_down', 'g_ple', 'w_ple_gate', 'w_ple_proj']
TWIN_WEIGHTS = ['g_mix', 'w_in', 'conv_w', 'conv_b', 'q_norm_g', 'k_norm_g', 'g_out_conv', 'g_out_attn', 'w_out', 'g_ffn', 'w_gate', 'w_up', 'ffn_conv_w', 'ffn_conv_b', 'w_down', 'g_ple', 'w_ple_gate', 'w_ple_proj']
TWIN_DIFF_INPUT = 'x'
TWIN_INPUTS = ['x', 'p', 'g_mix', 'w_in', 'conv_w', 'conv_b', 'q_norm_g', 'k_norm_g', 'g_out_conv', 'g_out_attn', 'w_out', 'g_ffn', 'w_gate', 'w_up', 'ffn_conv_w', 'ffn_conv_b', 'w_down', 'g_ple', 'w_ple_gate', 'w_ple_proj', 'loss_target', 'm_g_mix', 'm_w_in', 'm_conv_w', 'm_conv_b', 'm_q_norm_g', 'm_k_norm_g', 'm_g_out_conv', 'm_g_out_attn', 'm_w_out', 'm_g_ffn', 'm_w_gate', 'm_w_up', 'm_ffn_conv_w', 'm_ffn_conv_b', 'm_w_down', 'm_g_ple', 'm_w_ple_gate', 'm_w_ple_proj', 'v_g_mix', 'v_w_in', 'v_conv_w', 'v_conv_b', 'v_q_norm_g', 'v_k_norm_g', 'v_g_out_conv', 'v_g_out_attn', 'v_w_out', 'v_g_ffn', 'v_w_gate', 'v_w_up', 'v_ffn_conv_w', 'v_ffn_conv_b', 'v_w_down', 'v_g_ple', 'v_w_ple_gate', 'v_w_ple_proj']
TWIN_OUTPUTS = ['loss', 'grad_x', 'grad_g_mix', 'grad_w_in', 'grad_conv_w', 'grad_conv_b', 'grad_q_norm_g', 'grad_k_norm_g', 'grad_g_out_conv', 'grad_g_out_attn', 'grad_w_out', 'grad_g_ffn', 'grad_w_gate', 'grad_w_up', 'grad_ffn_conv_w', 'grad_ffn_conv_b', 'grad_w_down', 'grad_g_ple', 'grad_w_ple_gate', 'grad_w_ple_proj', 'delta_g_mix', 'delta_w_in', 'delta_conv_w', 'delta_conv_b', 'delta_q_norm_g', 'delta_k_norm_g', 'delta_g_out_conv', 'delta_g_out_attn', 'delta_w_out', 'delta_g_ffn', 'delta_w_gate', 'delta_w_up', 'delta_ffn_conv_w', 'delta_ffn_conv_b', 'delta_w_down', 'delta_g_ple', 'delta_w_ple_gate', 'delta_w_ple_proj', 'new_m_g_mix', 'new_m_w_in', 'new_m_conv_w', 'new_m_conv_b', 'new_m_q_norm_g', 'new_m_k_norm_g', 'new_m_g_out_conv', 'new_m_g_out_attn', 'new_m_w_out', 'new_m_g_ffn', 'new_m_w_gate', 'new_m_w_up', 'new_m_ffn_conv_w', 'new_m_ffn_conv_b', 'new_m_w_down', 'new_m_g_ple', 'new_m_w_ple_gate', 'new_m_w_ple_proj', 'new_v_g_mix', 'new_v_w_in', 'new_v_conv_w', 'new_v_conv_b', 'new_v_q_norm_g', 'new_v_k_norm_g', 'new_v_g_out_conv', 'new_v_g_out_attn', 'new_v_w_out', 'new_v_g_ffn', 'new_v_w_gate', 'new_v_w_up', 'new_v_ffn_conv_w', 'new_v_ffn_conv_b', 'new_v_w_down', 'new_v_g_ple', 'new_v_w_ple_gate', 'new_v_w_ple_proj']
TWIN_LEAF_KINDS = {'loss': 'loss', 'grad_x': 'grad_x', 'grad_g_mix': 'grad_w', 'grad_w_in': 'grad_w', 'grad_conv_w': 'grad_w', 'grad_conv_b': 'grad_w', 'grad_q_norm_g': 'grad_w', 'grad_k_norm_g': 'grad_w', 'grad_g_out_conv': 'grad_w', 'grad_g_out_attn': 'grad_w', 'grad_w_out': 'grad_w', 'grad_g_ffn': 'grad_w', 'grad_w_gate': 'grad_w', 'grad_w_up': 'grad_w', 'grad_ffn_conv_w': 'grad_w', 'grad_ffn_conv_b': 'grad_w', 'grad_w_down': 'grad_w', 'grad_g_ple': 'grad_w', 'grad_w_ple_gate': 'grad_w', 'grad_w_ple_proj': 'grad_w', 'delta_g_mix': 'delta_w', 'delta_w_in': 'delta_w', 'delta_conv_w': 'delta_w', 'delta_conv_b': 'delta_w', 'delta_q_norm_g': 'delta_w', 'delta_k_norm_g': 'delta_w', 'delta_g_out_conv': 'delta_w', 'delta_g_out_attn': 'delta_w', 'delta_w_out': 'delta_w', 'delta_g_ffn': 'delta_w', 'delta_w_gate': 'delta_w', 'delta_w_up': 'delta_w', 'delta_ffn_conv_w': 'delta_w', 'delta_ffn_conv_b': 'delta_w', 'delta_w_down': 'delta_w', 'delta_g_ple': 'delta_w', 'delta_w_ple_gate': 'delta_w', 'delta_w_ple_proj': 'delta_w', 'new_m_g_mix': 'new_m', 'new_m_w_in': 'new_m', 'new_m_conv_w': 'new_m', 'new_m_conv_b': 'new_m', 'new_m_q_norm_g': 'new_m', 'new_m_k_norm_g': 'new_m', 'new_m_g_out_conv': 'new_m', 'new_m_g_out_attn': 'new_m', 'new_m_w_out': 'new_m', 'new_m_g_ffn': 'new_m', 'new_m_w_gate': 'new_m', 'new_m_w_up': 'new_m', 'new_m_ffn_conv_w': 'new_m', 'new_m_ffn_conv_b': 'new_m', 'new_m_w_down': 'new_m', 'new_m_g_ple': 'new_m', 'new_m_w_ple_gate': 'new_m', 'new_m_w_ple_proj': 'new_m', 'new_v_g_mix': 'new_v', 'new_v_w_in': 'new_v', 'new_v_conv_w': 'new_v', 'new_v_conv_b': 'new_v', 'new_v_q_norm_g': 'new_v', 'new_v_k_norm_g': 'new_v', 'new_v_g_out_conv': 'new_v', 'new_v_g_out_attn': 'new_v', 'new_v_w_out': 'new_v', 'new_v_g_ffn': 'new_v', 'new_v_w_gate': 'new_v', 'new_v_w_up': 'new_v', 'new_v_ffn_conv_w': 'new_v', 'new_v_ffn_conv_b': 'new_v', 'new_v_w_down': 'new_v', 'new_v_g_ple': 'new_v', 'new_v_w_ple_gate': 'new_v', 'new_v_w_ple_proj': 'new_v'}


def _forward(args):
    return _fwd_reference(*[args[k] for k in FWD_PARAMS])


def _output_shape():
    def fwd():
        inp = _fwd_setup_inputs(0)
        return _fwd_reference(*[inp[k] for k in FWD_PARAMS])
    out = _jax.eval_shape(fwd)
    return out.shape, out.dtype

N_MICROBATCH = 1
ADAM_LR = 0.001
ADAM_B1 = 0.9
ADAM_B2 = 0.999
ADAM_EPS = 1e-08
ADAM_WD = 0.01
ADAM_STEP = 10
PER_EXAMPLE_BATCH_AXIS = {'x': 0, 'p': 1, 'loss_target': 0}
SHARED_INPUTS = []
_WEIGHT_DTYPES = {'g_mix': _jnp.float32, 'w_in': _jnp.float32, 'conv_w': _jnp.float32, 'conv_b': _jnp.float32, 'q_norm_g': _jnp.float32, 'k_norm_g': _jnp.float32, 'g_out_conv': _jnp.float32, 'g_out_attn': _jnp.float32, 'w_out': _jnp.float32, 'g_ffn': _jnp.float32, 'w_gate': _jnp.float32, 'w_up': _jnp.float32, 'ffn_conv_w': _jnp.float32, 'ffn_conv_b': _jnp.float32, 'w_down': _jnp.float32, 'g_ple': _jnp.float32, 'w_ple_gate': _jnp.float32, 'w_ple_proj': _jnp.float32}
MOMENT_SCALE = {'g_mix': 1.375473e+00, 'w_in': 7.014482e-01, 'conv_w': 3.003342e+00, 'conv_b': 1.838959e+00, 'q_norm_g': 1.473537e+00, 'k_norm_g': 1.520620e+00, 'g_out_conv': 8.065953e+01, 'g_out_attn': 8.717284e+01, 'w_out': 2.419314e+00, 'g_ffn': 5.199352e+01, 'w_gate': 4.085374e-01, 'w_up': 5.037547e-01, 'ffn_conv_w': 5.948439e+00, 'ffn_conv_b': 7.169878e+00, 'w_down': 7.619331e-01, 'g_ple': 1.937421e+00, 'w_ple_gate': 1.236131e-01, 'w_ple_proj': 8.758699e-01}


def _to_microbatches(a, axis):
    t = _jnp.moveaxis(a, axis, 0)
    t = t.reshape((N_MICROBATCH, t.shape[0] // N_MICROBATCH) + t.shape[1:])
    return _jnp.moveaxis(t, 1, axis + 1)


def setup_inputs(seed: int = 0) -> dict:
    inp = _fwd_setup_inputs(seed)
    key = _jax.random.fold_in(_jax.random.key(seed), 7919)
    shape, _ = _output_shape()
    out = dict(inp)
    out["loss_target"] = _jax.random.normal(_jax.random.fold_in(key, 0), shape, _jnp.float32)
    for i, name in enumerate(TWIN_WEIGHTS):
        w = inp[name].astype(_jnp.float32)
        if MOMENT_SCALE is None:
            s = _jnp.sqrt(_jnp.mean(_jnp.square(w)) + 1e-30)
        else:
            s = MOMENT_SCALE[name]
        km, kv = _jax.random.split(_jax.random.fold_in(key, i + 1))
        out[name] = w
        out["m_" + name] = s * _jax.random.normal(km, w.shape, _jnp.float32)
        out["v_" + name] = (s * s) * _jax.random.uniform(kv, w.shape, _jnp.float32, 0.5, 1.5)
    if N_MICROBATCH > 1:
        for name, axis in PER_EXAMPLE_BATCH_AXIS.items():
            out[name] = _to_microbatches(out[name], axis)
    return {'x': out['x'], 'p': out['p'], 'g_mix': out['g_mix'], 'w_in': out['w_in'], 'conv_w': out['conv_w'], 'conv_b': out['conv_b'], 'q_norm_g': out['q_norm_g'], 'k_norm_g': out['k_norm_g'], 'g_out_conv': out['g_out_conv'], 'g_out_attn': out['g_out_attn'], 'w_out': out['w_out'], 'g_ffn': out['g_ffn'], 'w_gate': out['w_gate'], 'w_up': out['w_up'], 'ffn_conv_w': out['ffn_conv_w'], 'ffn_conv_b': out['ffn_conv_b'], 'w_down': out['w_down'], 'g_ple': out['g_ple'], 'w_ple_gate': out['w_ple_gate'], 'w_ple_proj': out['w_ple_proj'], 'loss_target': out['loss_target'], 'm_g_mix': out['m_g_mix'], 'm_w_in': out['m_w_in'], 'm_conv_w': out['m_conv_w'], 'm_conv_b': out['m_conv_b'], 'm_q_norm_g': out['m_q_norm_g'], 'm_k_norm_g': out['m_k_norm_g'], 'm_g_out_conv': out['m_g_out_conv'], 'm_g_out_attn': out['m_g_out_attn'], 'm_w_out': out['m_w_out'], 'm_g_ffn': out['m_g_ffn'], 'm_w_gate': out['m_w_gate'], 'm_w_up': out['m_w_up'], 'm_ffn_conv_w': out['m_ffn_conv_w'], 'm_ffn_conv_b': out['m_ffn_conv_b'], 'm_w_down': out['m_w_down'], 'm_g_ple': out['m_g_ple'], 'm_w_ple_gate': out['m_w_ple_gate'], 'm_w_ple_proj': out['m_w_ple_proj'], 'v_g_mix': out['v_g_mix'], 'v_w_in': out['v_w_in'], 'v_conv_w': out['v_conv_w'], 'v_conv_b': out['v_conv_b'], 'v_q_norm_g': out['v_q_norm_g'], 'v_k_norm_g': out['v_k_norm_g'], 'v_g_out_conv': out['v_g_out_conv'], 'v_g_out_attn': out['v_g_out_attn'], 'v_w_out': out['v_w_out'], 'v_g_ffn': out['v_g_ffn'], 'v_w_gate': out['v_w_gate'], 'v_w_up': out['v_w_up'], 'v_ffn_conv_w': out['v_ffn_conv_w'], 'v_ffn_conv_b': out['v_ffn_conv_b'], 'v_w_down': out['v_w_down'], 'v_g_ple': out['v_g_ple'], 'v_w_ple_gate': out['v_w_ple_gate'], 'v_w_ple_proj': out['v_w_ple_proj']}


def _loss(weights, diff, rest, loss_target):
    with _jax.named_scope("forward"):
        args = {**rest, TWIN_DIFF_INPUT: diff, **{k: w.astype(_WEIGHT_DTYPES[k]) for k, w in weights.items()}}
        y = _forward(args)
    with _jax.named_scope("loss_head"):
        err = _jnp.square(y.astype(_jnp.float32) - loss_target)
        return 0.5 * _jnp.sum(_jnp.mean(err, axis=-1)) if err.ndim else 0.5 * err


def _adamw(w, g, m, v):
    m = ADAM_B1 * m + (1.0 - ADAM_B1) * g
    v = ADAM_B2 * v + (1.0 - ADAM_B2) * _jnp.square(g)
    m_hat = m / (1.0 - ADAM_B1 ** ADAM_STEP)
    v_hat = v / (1.0 - ADAM_B2 ** ADAM_STEP)
    delta = -ADAM_LR * (m_hat / (_jnp.sqrt(v_hat) + ADAM_EPS) + ADAM_WD * w)
    return delta, m, v


def reference(x, p, g_mix, w_in, conv_w, conv_b, q_norm_g, k_norm_g, g_out_conv, g_out_attn, w_out, g_ffn, w_gate, w_up, ffn_conv_w, ffn_conv_b, w_down, g_ple, w_ple_gate, w_ple_proj, loss_target, m_g_mix, m_w_in, m_conv_w, m_conv_b, m_q_norm_g, m_k_norm_g, m_g_out_conv, m_g_out_attn, m_w_out, m_g_ffn, m_w_gate, m_w_up, m_ffn_conv_w, m_ffn_conv_b, m_w_down, m_g_ple, m_w_ple_gate, m_w_ple_proj, v_g_mix, v_w_in, v_conv_w, v_conv_b, v_q_norm_g, v_k_norm_g, v_g_out_conv, v_g_out_attn, v_w_out, v_g_ffn, v_w_gate, v_w_up, v_ffn_conv_w, v_ffn_conv_b, v_w_down, v_g_ple, v_w_ple_gate, v_w_ple_proj):
    given = dict(x=x, p=p, g_mix=g_mix, w_in=w_in, conv_w=conv_w, conv_b=conv_b, q_norm_g=q_norm_g, k_norm_g=k_norm_g, g_out_conv=g_out_conv, g_out_attn=g_out_attn, w_out=w_out, g_ffn=g_ffn, w_gate=w_gate, w_up=w_up, ffn_conv_w=ffn_conv_w, ffn_conv_b=ffn_conv_b, w_down=w_down, g_ple=g_ple, w_ple_gate=w_ple_gate, w_ple_proj=w_ple_proj, loss_target=loss_target, m_g_mix=m_g_mix, m_w_in=m_w_in, m_conv_w=m_conv_w, m_conv_b=m_conv_b, m_q_norm_g=m_q_norm_g, m_k_norm_g=m_k_norm_g, m_g_out_conv=m_g_out_conv, m_g_out_attn=m_g_out_attn, m_w_out=m_w_out, m_g_ffn=m_g_ffn, m_w_gate=m_w_gate, m_w_up=m_w_up, m_ffn_conv_w=m_ffn_conv_w, m_ffn_conv_b=m_ffn_conv_b, m_w_down=m_w_down, m_g_ple=m_g_ple, m_w_ple_gate=m_w_ple_gate, m_w_ple_proj=m_w_ple_proj, v_g_mix=v_g_mix, v_w_in=v_w_in, v_conv_w=v_conv_w, v_conv_b=v_conv_b, v_q_norm_g=v_q_norm_g, v_k_norm_g=v_k_norm_g, v_g_out_conv=v_g_out_conv, v_g_out_attn=v_g_out_attn, v_w_out=v_w_out, v_g_ffn=v_g_ffn, v_w_gate=v_w_gate, v_w_up=v_w_up, v_ffn_conv_w=v_ffn_conv_w, v_ffn_conv_b=v_ffn_conv_b, v_w_down=v_w_down, v_g_ple=v_g_ple, v_w_ple_gate=v_w_ple_gate, v_w_ple_proj=v_w_ple_proj)
    weights = {n: given[n] for n in TWIN_WEIGHTS}
    shared = {n: given[n] for n in SHARED_INPUTS}
    per_example = {n: given[n] for n in ['x', 'p']}
    grad_fn = _jax.value_and_grad(_loss, argnums=(0, 1))

    def one_microbatch(ex, loss_target):
        ex = dict(ex)
        diff = ex.pop(TWIN_DIFF_INPUT)
        return grad_fn(weights, diff, {**shared, **ex}, loss_target)

    if N_MICROBATCH == 1:
        loss, (grad_w, grad_x) = one_microbatch(per_example, given["loss_target"])
    else:
        def body(carry, xs):
            loss_sum, grad_sum = carry
            l_k, (gw_k, gx_k) = one_microbatch(xs[0], xs[1])
            with _jax.named_scope("update"):
                return (loss_sum + l_k, _jax.tree.map(_jnp.add, grad_sum, gw_k)), gx_k

        init = (_jnp.zeros((), _jnp.float32), _jax.tree.map(_jnp.zeros_like, weights))
        (loss, grad_w), grad_x = _jax.lax.scan(body, init, (per_example, given["loss_target"]))
    with _jax.named_scope("update"):
        delta_w, new_m, new_v = {}, {}, {}
        for n in TWIN_WEIGHTS:
            delta_w[n], new_m[n], new_v[n] = _adamw(weights[n], grad_w[n], given["m_" + n], given["v_" + n])
    return (loss, grad_x, *[grad_w[n] for n in TWIN_WEIGHTS], *[delta_w[n] for n in TWIN_WEIGHTS],
            *[new_m[n] for n in TWIN_WEIGHTS], *[new_v[n] for n in TWIN_WEIGHTS])
```

```python
import functools

import jax
import jax.numpy as jnp
from jax import lax
from jax.experimental import pallas as pl
from jax.experimental.pallas import tpu as pltpu

D_MODEL = 1024
CONV_W = 512
N_HEADS = 8
HEAD_DIM = 64
ATTN_W = 512
D_FF_SHARD = 704
D_FF_SLAB = 768
D_FF_PAD = 4 * D_FF_SLAB
IN_SLAB = 768
PLE_DIM = 256
N_CHIPS = 4
QBLK = 128
DILATIONS = (1, 4, 16)
EPS = 1e-6
NEG = -1e30
MESH = pl.DeviceIdType.MESH

ADAM_LR = 0.001
ADAM_B1 = 0.9
ADAM_B2 = 0.999
ADAM_EPS = 1e-08
ADAM_WD = 0.01
ADAM_STEP = 10

BF = jnp.bfloat16
F32 = jnp.float32
MIB = 1024 * 1024


def _mm(a, b):
    return jnp.dot(a, b, preferred_element_type=F32)


def _mm_nt(a, b):
    return lax.dot_general(a, b, (((1,), (1,)), ((), ())), preferred_element_type=F32)


def _mm_tn(a, b):
    return lax.dot_general(a, b, (((0,), (0,)), ((), ())), preferred_element_type=F32)


def _rstd(a):
    return lax.rsqrt(jnp.mean(a * a, axis=-1, keepdims=True) + EPS)


def _norm_bwd(dy, xh, r, g):
    dxh = dy * g
    return r * (dxh - xh * jnp.mean(dxh * xh, axis=-1, keepdims=True))


def _colsum(a):
    return jnp.sum(a, axis=0, keepdims=True)


def _params(vmem_mib, n_grid=1):
    return pltpu.CompilerParams(dimension_semantics=("arbitrary",) * n_grid, vmem_limit_bytes=vmem_mib * MIB)


def _const(shape):
    n = len(shape)
    return pl.BlockSpec(shape, lambda *_: (0,) * n, pipeline_mode=pl.Buffered(1))


def _rows(tm, width, rev_of=None):
    if rev_of is None:
        return pl.BlockSpec((tm, width), lambda i: (i, 0))
    return pl.BlockSpec((tm, width), lambda i: (rev_of - 1 - i, 0))


def _halo(tm, width, nt):
    return pl.BlockSpec((8, width), lambda i: (jnp.maximum((nt - 1 - i) * (tm // 8) - 1, 0), 0))


def _fwd_mix(x, g_mix, win4, conv_w, conv_b, g_oc, tm):
    t = x.shape[0]
    nt = t // tm

    def body(x_ref, g_ref, w_ref, cw_ref, cb_ref, goc_ref, zbcx_ref, qkv_ref, ycn_ref, ubuf):
        @pl.when(pl.program_id(0) == 0)
        def _():
            ubuf[0:8, :] = jnp.zeros((8, CONV_W), F32)

        xt = x_ref[...]
        h = ((xt * _rstd(xt)) * g_ref[...]).astype(BF)
        zbcx_ref[:, 0:IN_SLAB] = _mm(h, w_ref[0])
        zbcx_ref[:, IN_SLAB:2 * IN_SLAB] = _mm(h, w_ref[1])
        qkv_ref[:, 0:IN_SLAB] = _mm(h, w_ref[2])
        qkv_ref[:, IN_SLAB:2 * IN_SLAB] = _mm(h, w_ref[3])
        u = zbcx_ref[:, 512:1024] * zbcx_ref[:, 1024:1536]
        ubuf[8:8 + tm, :] = u
        cv = (cw_ref[0:1, :] * ubuf[6:6 + tm, :] + cw_ref[1:2, :] * ubuf[7:7 + tm, :]
              + cw_ref[2:3, :] * u + cb_ref[...])
        ubuf[0:8, :] = ubuf[tm:tm + 8, :]
        yc = zbcx_ref[:, 0:512] * cv
        ycn_ref[...] = ((yc * _rstd(yc)) * goc_ref[...]).astype(BF)

    return pl.pallas_call(
        body, name="fwd_mix", grid=(nt,),
        in_specs=[_rows(tm, D_MODEL), _const((1, D_MODEL)), _const((N_CHIPS, D_MODEL, IN_SLAB)),
                  _const((3, CONV_W)), _const((1, CONV_W)), _const((1, CONV_W))],
        out_specs=[_rows(tm, 1536), _rows(tm, 1536), _rows(tm, CONV_W)],
        out_shape=[jax.ShapeDtypeStruct((t, 1536), F32), jax.ShapeDtypeStruct((t, 1536), F32),
                   jax.ShapeDtypeStruct((t, CONV_W), BF)],
        scratch_shapes=[pltpu.VMEM((tm + 8, CONV_W), F32)],
        compiler_params=_params(48),
    )(x, g_mix, win4, conv_w, conv_b, g_oc)


def _alibi(h):
    return 2.0 ** (-(h + 1))


def _head_norm(z, g):
    r = _rstd(z)
    zh = z * r
    return zh, r, zh * g


def _attn_fwd(qkv, gq, gk, d, prev):
    t = qkv.shape[0]
    rows = t // d
    nb = rows // QBLK
    view = qkv.reshape(rows, d * 1536)
    merge = prev is not None

    def body(*refs):
        if merge:
            cur_ref, prv_ref, gq_ref, gk_ref, oin_ref, lin_ref, o_ref, l_ref = refs
        else:
            cur_ref, prv_ref, gq_ref, gk_ref, o_ref, l_ref = refs
        n = pl.program_id(1)
        qi = lax.broadcasted_iota(jnp.int32, (QBLK, 2 * QBLK), 0)
        kj = lax.broadcasted_iota(jnp.int32, (QBLK, 2 * QBLK), 1)
        step = qi + QBLK - kj
        valid = (step >= 0) & (step <= QBLK) & ((kj >= QBLK) | (n > 0))
        dist = step.astype(F32) * float(d)
        outs, lses = [], []
        for h in range(N_HEADS):
            lo = h * HEAD_DIM
            _, _, qn = _head_norm(cur_ref[:, lo:lo + HEAD_DIM], gq_ref[...])
            qs = (qn * (HEAD_DIM ** -0.5)).astype(BF)
            kk = jnp.concatenate([prv_ref[:, 512 + lo:512 + lo + HEAD_DIM],
                                  cur_ref[:, 512 + lo:512 + lo + HEAD_DIM]], axis=0)
            _, _, kn = _head_norm(kk, gk_ref[...])
            vv = jnp.concatenate([prv_ref[:, 1024 + lo:1024 + lo + HEAD_DIM],
                                  cur_ref[:, 1024 + lo:1024 + lo + HEAD_DIM]], axis=0).astype(BF)
            s = _mm_nt(qs, kn.astype(BF)) - _alibi(h) * dist
            s = jnp.where(valid, s, NEG)
            m = jnp.max(s, axis=-1, keepdims=True)
            e = jnp.exp(s - m)
            den = jnp.sum(e, axis=-1, keepdims=True)
            o = _mm(e.astype(BF), vv) / den
            lse = jnp.broadcast_to(m + jnp.log(den), (QBLK, HEAD_DIM))
            if merge:
                la = lin_ref[:, lo:lo + HEAD_DIM]
                mx = jnp.maximum(la, lse)
                wa = jnp.exp(la - mx)
                wb = jnp.exp(lse - mx)
                tot = wa + wb
                o = (oin_ref[:, lo:lo + HEAD_DIM] * wa + o * wb) / tot
                lse = mx + jnp.log(tot)
            outs.append(o)
            lses.append(lse)
        o_ref[...] = jnp.concatenate(outs, axis=1)
        l_ref[...] = jnp.concatenate(lses, axis=1)

    blk = pl.BlockSpec((QBLK, ATTN_W), lambda r, n: (n, r))
    in_specs = [pl.BlockSpec((QBLK, 1536), lambda r, n: (n, r)),
                pl.BlockSpec((QBLK, 1536), lambda r, n: (jnp.maximum(n - 1, 0), r)),
                pl.BlockSpec((1, HEAD_DIM), lambda r, n: (0, 0)), pl.BlockSpec((1, HEAD_DIM), lambda r, n: (0, 0))]
    args = [view, view, gq, gk]
    if merge:
        in_specs += [blk, blk]
        args += [prev[0].reshape(rows, d * ATTN_W), prev[1].reshape(rows, d * ATTN_W)]
    o, lse = pl.pallas_call(
        body, name=f"attn_fwd_d{d}", grid=(d, nb), in_specs=in_specs, out_specs=[blk, blk],
        out_shape=[jax.ShapeDtypeStruct((rows, d * ATTN_W), F32)] * 2,
        compiler_params=_params(32, 2),
    )(*args)
    return o.reshape(t, ATTN_W), lse.reshape(t, ATTN_W)


def _attn_bwd(qkv, o, lse, do, gq, gk, d, acc):
    t = qkv.shape[0]
    rows = t // d
    nb = rows // QBLK
    view = qkv.reshape(rows, d * 1536)
    have_acc = acc is not None

    def body(*refs):
        if have_acc:
            (prv_ref, cur_ref, nxt_ref, oc_ref, ox_ref, lc_ref, lx_ref, dc_ref, dx_ref, gq_ref, gk_ref,
             acc_ref, out_ref, dgq_ref, dgk_ref) = refs
        else:
            (prv_ref, cur_ref, nxt_ref, oc_ref, ox_ref, lc_ref, lx_ref, dc_ref, dx_ref, gq_ref, gk_ref,
             out_ref, dgq_ref, dgk_ref) = refs
        first = (pl.program_id(0) == 0) & (pl.program_id(1) == 0)

        @pl.when(first)
        def _():
            dgq_ref[...] = jnp.zeros_like(dgq_ref)
            dgk_ref[...] = jnp.zeros_like(dgk_ref)

        n = pl.program_id(1)
        has_prev = n > 0
        has_next = n < nb - 1
        qi = lax.broadcasted_iota(jnp.int32, (QBLK, 2 * QBLK), 0)
        kj = lax.broadcasted_iota(jnp.int32, (QBLK, 2 * QBLK), 1)
        step_q = qi + QBLK - kj
        valid_q = (step_q >= 0) & (step_q <= QBLK) & ((kj >= QBLK) | has_prev)
        dist_q = step_q.astype(F32) * float(d)
        ri = lax.broadcasted_iota(jnp.int32, (2 * QBLK, QBLK), 0)
        ci = lax.broadcasted_iota(jnp.int32, (2 * QBLK, QBLK), 1)
        step_k = ri - ci
        valid_k = (step_k >= 0) & (step_k <= QBLK) & ((ri < QBLK) | has_next)
        dist_k = step_k.astype(F32) * float(d)
        scale = HEAD_DIM ** -0.5
        dqs, dks, dvs = [], [], []
        dgq = jnp.zeros((1, HEAD_DIM), F32)
        dgk = jnp.zeros((1, HEAD_DIM), F32)
        for h in range(N_HEADS):
            lo = h * HEAD_DIM
            sq = slice(lo, lo + HEAD_DIM)
            sk = slice(512 + lo, 512 + lo + HEAD_DIM)
            sv = slice(1024 + lo, 1024 + lo + HEAD_DIM)
            qh_c, rq_c, qn_c = _head_norm(cur_ref[:, sq], gq_ref[...])
            _, _, qn_x = _head_norm(nxt_ref[:, sq], gq_ref[...])
            kh_c, rk_c, kn_c = _head_norm(cur_ref[:, sk], gk_ref[...])
            _, _, kn_p = _head_norm(prv_ref[:, sk], gk_ref[...])
            qs_c = (qn_c * scale).astype(BF)
            qs_x = (qn_x * scale).astype(BF)
            kn_cb = kn_c.astype(BF)
            kk = jnp.concatenate([kn_p.astype(BF), kn_cb], axis=0)
            v_cb = cur_ref[:, sv].astype(BF)
            vv = jnp.concatenate([prv_ref[:, sv].astype(BF), v_cb], axis=0)
            do_c = dc_ref[:, sq]
            do_x = dx_ref[:, sq]
            dd_c = jnp.sum(do_c * oc_ref[:, sq], axis=-1, keepdims=True)
            dd_x = jnp.sum(do_x * ox_ref[:, sq], axis=-1, keepdims=True)
            lse_c = lc_ref[:, lo:lo + 1]
            lse_x = lx_ref[:, lo:lo + 1]
            do_cb = do_c.astype(BF)
            s = jnp.where(valid_q, _mm_nt(qs_c, kk) - _alibi(h) * dist_q, NEG)
            p = jnp.exp(s - lse_c)
            dp = _mm_nt(do_cb, vv)
            ds = (p * (dp - dd_c)).astype(BF)
            dqn = _mm(ds, kk) * scale
            q2 = jnp.concatenate([qs_c, qs_x], axis=0)
            do2 = jnp.concatenate([do_cb, do_x.astype(BF)], axis=0)
            lse2 = jnp.concatenate([lse_c, lse_x], axis=0)
            dd2 = jnp.concatenate([dd_c, dd_x], axis=0)
            s2 = jnp.where(valid_k, _mm_nt(q2, kn_cb) - _alibi(h) * dist_k, NEG)
            p2 = jnp.exp(s2 - lse2)
            dv = _mm_tn(p2.astype(BF), do2)
            dp2 = _mm_nt(do2, v_cb)
            ds2 = (p2 * (dp2 - dd2)).astype(BF)
            dkn = _mm_tn(ds2, q2)
            dgq = dgq + _colsum(dqn * qh_c)
            dgk = dgk + _colsum(dkn * kh_c)
            dqs.append(_norm_bwd(dqn, qh_c, rq_c, gq_ref[...]))
            dks.append(_norm_bwd(dkn, kh_c, rk_c, gk_ref[...]))
            dvs.append(dv)
        new = jnp.concatenate(dqs + dks + dvs, axis=1)
        if have_acc:
            new = new + acc_ref[...]
        out_ref[...] = new
        dgq_ref[...] += dgq
        dgk_ref[...] += dgk

    def wide(shift):
        return pl.BlockSpec((QBLK, 1536), lambda r, n: (jnp.clip(n + shift, 0, nb - 1), r))

    def narrow(shift):
        return pl.BlockSpec((QBLK, ATTN_W), lambda r, n: (jnp.clip(n + shift, 0, nb - 1), r))

    gspec = pl.BlockSpec((1, HEAD_DIM), lambda r, n: (0, 0))
    ov = o.reshape(rows, d * ATTN_W)
    lv = lse.reshape(rows, d * ATTN_W)
    dv_ = do.reshape(rows, d * ATTN_W)
    in_specs = [wide(-1), wide(0), wide(1), narrow(0), narrow(1), narrow(0), narrow(1), narrow(0), narrow(1),
                gspec, gspec]
    args = [view, view, view, ov, ov, lv, lv, dv_, dv_, gq, gk]
    aliases = {}
    if have_acc:
        in_specs.append(wide(0))
        args.append(acc.reshape(rows, d * 1536))
        aliases = {len(args) - 1: 0}
    out, dgq, dgk = pl.pallas_call(
        body, name=f"attn_bwd_d{d}", grid=(d, nb), in_specs=in_specs,
        out_specs=[wide(0), gspec, gspec],
        out_shape=[jax.ShapeDtypeStruct((rows, d * 1536), F32), jax.ShapeDtypeStruct((1, HEAD_DIM), F32),
                   jax.ShapeDtypeStruct((1, HEAD_DIM), F32)],
        input_output_aliases=aliases,
        compiler_params=_params(32, 2),
    )(*args)
    return out.reshape(t, 1536), dgq, dgk


def _fwd_ffn(x, ycn, ya, wout, wg4, wu4, g_oa, g_ffn, fcw, fcb, tm):
    t = x.shape[0]
    nt = t // tm

    def body(x_ref, ycn_ref, ya_ref, wout_ref, wg_ref, wu_ref, goa_ref, gffn_ref, fcw_ref, fcb_ref,
             x1_ref, gp_ref, up_ref, act_ref, ycat_ref, cbuf):
        @pl.when(pl.program_id(0) == 0)
        def _():
            cbuf[0:8, :] = jnp.zeros((8, D_FF_PAD), F32)

        yat = ya_ref[...]
        yan = ((yat * _rstd(yat)) * goa_ref[...]).astype(BF)
        ycn = ycn_ref[...]
        ycat_ref[:, 0:CONV_W] = ycn
        ycat_ref[:, CONV_W:D_MODEL] = yan
        x1 = x_ref[...] + _mm(ycn, wout_ref[0:CONV_W, :]) + _mm(yan, wout_ref[CONV_W:D_MODEL, :])
        x1_ref[...] = x1
        h2 = ((x1 * _rstd(x1)) * gffn_ref[...]).astype(BF)
        for s in range(N_CHIPS):
            lo, hi = s * D_FF_SLAB, (s + 1) * D_FF_SLAB
            gps = _mm(h2, wg_ref[s])
            ups = _mm(h2, wu_ref[s])
            gp_ref[:, lo:hi] = gps
            up_ref[:, lo:hi] = ups
            cbuf[8:8 + tm, lo:hi] = gps
            gate = (fcw_ref[0:1, lo:hi] * cbuf[6:6 + tm, lo:hi] + fcw_ref[1:2, lo:hi] * cbuf[7:7 + tm, lo:hi]
                    + fcw_ref[2:3, lo:hi] * gps + fcb_ref[:, lo:hi])
            act_ref[:, lo:hi] = ((gate * jax.nn.sigmoid(gate)) * ups).astype(BF)
        cbuf[0:8, :] = cbuf[tm:tm + 8, :]

    return pl.pallas_call(
        body, name="fwd_ffn", grid=(nt,),
        in_specs=[_rows(tm, D_MODEL), _rows(tm, CONV_W), _rows(tm, ATTN_W), _const((D_MODEL, D_MODEL)),
                  _const((N_CHIPS, D_MODEL, D_FF_SLAB)), _const((N_CHIPS, D_MODEL, D_FF_SLAB)),
                  _const((1, ATTN_W)), _const((1, D_MODEL)), _const((3, D_FF_PAD)), _const((1, D_FF_PAD))],
        out_specs=[_rows(tm, D_MODEL), _rows(tm, D_FF_PAD), _rows(tm, D_FF_PAD), _rows(tm, D_FF_PAD),
                   _rows(tm, D_MODEL)],
        out_shape=[jax.ShapeDtypeStruct((t, D_MODEL), F32), jax.ShapeDtypeStruct((t, D_FF_PAD), F32),
                   jax.ShapeDtypeStruct((t, D_FF_PAD), F32), jax.ShapeDtypeStruct((t, D_FF_PAD), BF),
                   jax.ShapeDtypeStruct((t, D_MODEL), BF)],
        scratch_shapes=[pltpu.VMEM((tm + 8, D_FF_PAD), F32)],
        compiler_params=_params(56),
    )(x, ycn, ya, wout, wg4, wu4, g_oa, g_ffn, fcw, fcb)


def _fwd_tail(x1, act, p, target, wd4, wpg, wpp4, g_ple, tm):
    t = x1.shape[0]
    nt = t // tm

    def body(x1_ref, act_ref, p_ref, tgt_ref, wd_ref, wpg_ref, wpp_ref, g_ref,
             dx2_ref, h3_ref, ds_ref, dpp_ref, dg_ref, loss_ref, lacc):
        i = pl.program_id(0)

        @pl.when(i == 0)
        def _():
            dg_ref[...] = jnp.zeros_like(dg_ref)
            lacc[...] = jnp.zeros_like(lacc)

        x2 = x1_ref[...]
        for s in range(N_CHIPS):
            x2 = x2 + _mm(act_ref[:, s * D_FF_SLAB:(s + 1) * D_FF_SLAB], wd_ref[s])
        r3 = _rstd(x2)
        xh = x2 * r3
        h3 = (xh * g_ref[...]).astype(BF)
        h3_ref[...] = h3
        sg = jax.nn.sigmoid(_mm(h3, wpg_ref[...]))
        pb = p_ref[...].astype(BF)
        pp = jnp.concatenate([_mm(pb, wpp_ref[s]) for s in range(N_CHIPS)], axis=1)
        err = (x2 + sg * pp) - tgt_ref[...]
        lacc[...] += _colsum(err * err)
        dx3 = err * (1.0 / D_MODEL)
        dpp_ref[...] = (dx3 * sg).astype(BF)
        dsb = ((dx3 * pp) * (sg * (1.0 - sg))).astype(BF)
        ds_ref[...] = dsb
        dh3 = _mm_nt(dsb, wpg_ref[...])
        dg_ref[...] += _colsum(dh3 * xh)
        dx2_ref[...] = dx3 + _norm_bwd(dh3, xh, r3, g_ref[...])

        @pl.when(i == nt - 1)
        def _():
            loss_ref[...] = jnp.full((1, 128), jnp.sum(lacc[...]) * (0.5 / D_MODEL), F32)

    return pl.pallas_call(
        body, name="fwd_tail", grid=(nt,),
        in_specs=[_rows(tm, D_MODEL), _rows(tm, D_FF_PAD), _rows(tm, PLE_DIM), _rows(tm, D_MODEL),
                  _const((N_CHIPS, D_FF_SLAB, D_MODEL)), _const((D_MODEL, D_MODEL)),
                  _const((N_CHIPS, PLE_DIM, PLE_DIM)), _const((1, D_MODEL))],
        out_specs=[_rows(tm, D_MODEL), _rows(tm, D_MODEL), _rows(tm, D_MODEL), _rows(tm, D_MODEL),
                   pl.BlockSpec((1, D_MODEL), lambda i: (0, 0)), pl.BlockSpec((1, 128), lambda i: (0, 0))],
        out_shape=[jax.ShapeDtypeStruct((t, D_MODEL), F32), jax.ShapeDtypeStruct((t, D_MODEL), BF),
                   jax.ShapeDtypeStruct((t, D_MODEL), BF), jax.ShapeDtypeStruct((t, D_MODEL), BF),
                   jax.ShapeDtypeStruct((1, D_MODEL), F32), jax.ShapeDtypeStruct((1, 128), F32)],
        scratch_shapes=[pltpu.VMEM((1, D_MODEL), F32)],
        compiler_params=_params(48),
    )(x1, act, p, target, wd4, wpg, wpp4, g_ple)


def _bwd_ffn_a(dx2, gp, up, wd4, fcw, fcb, tm):
    t = dx2.shape[0]
    nt = t // tm

    def body(dx2_ref, gp_ref, gph_ref, up_ref, wd_ref, fcw_ref, fcb_ref,
             dgp_ref, dup_ref, dfcw_ref, dfcb_ref, cbuf, dbuf):
        i = pl.program_id(0)

        @pl.when(i == 0)
        def _():
            dbuf[tm:tm + 8, :] = jnp.zeros((8, D_FF_PAD), F32)
            dfcw_ref[...] = jnp.zeros_like(dfcw_ref)
            dfcb_ref[...] = jnp.zeros_like(dfcb_ref)

        not_first_tile = i < nt - 1
        dx2b = dx2_ref[...].astype(BF)
        for s in range(N_CHIPS):
            lo, hi = s * D_FF_SLAB, (s + 1) * D_FF_SLAB
            gps = gp_ref[:, lo:hi]
            cbuf[0:8, lo:hi] = jnp.where(not_first_tile, gph_ref[:, lo:hi], 0.0)
            cbuf[8:8 + tm, lo:hi] = gps
            g1 = cbuf[7:7 + tm, lo:hi]
            g2 = cbuf[6:6 + tm, lo:hi]
            w0, w1, w2 = fcw_ref[0:1, lo:hi], fcw_ref[1:2, lo:hi], fcw_ref[2:3, lo:hi]
            gate = w0 * g2 + w1 * g1 + w2 * gps + fcb_ref[:, lo:hi]
            sg = jax.nn.sigmoid(gate)
            dact = _mm_nt(dx2b, wd_ref[s])
            dup_ref[:, lo:hi] = (dact * (gate * sg)).astype(BF)
            dgate = (dact * up_ref[:, lo:hi]) * (sg * (1.0 + gate * (1.0 - sg)))
            dfcb_ref[:, lo:hi] += _colsum(dgate)
            dfcw_ref[0:1, lo:hi] += _colsum(dgate * g2)
            dfcw_ref[1:2, lo:hi] += _colsum(dgate * g1)
            dfcw_ref[2:3, lo:hi] += _colsum(dgate * gps)
            dbuf[0:tm, lo:hi] = dgate
            dgp = w2 * dgate + w1 * dbuf[1:1 + tm, lo:hi] + w0 * dbuf[2:2 + tm, lo:hi]
            dgp_ref[:, lo:hi] = dgp.astype(BF)
        dbuf[tm:tm + 8, :] = dbuf[0:8, :]

    return pl.pallas_call(
        body, name="bwd_ffn_a", grid=(nt,),
        in_specs=[_rows(tm, D_MODEL, nt), _rows(tm, D_FF_PAD, nt), _halo(tm, D_FF_PAD, nt), _rows(tm, D_FF_PAD, nt),
                  _const((N_CHIPS, D_FF_SLAB, D_MODEL)), _const((3, D_FF_PAD)), _const((1, D_FF_PAD))],
        out_specs=[_rows(tm, D_FF_PAD, nt), _rows(tm, D_FF_PAD, nt),
                   pl.BlockSpec((3, D_FF_PAD), lambda i: (0, 0)), pl.BlockSpec((1, D_FF_PAD), lambda i: (0, 0))],
        out_shape=[jax.ShapeDtypeStruct((t, D_FF_PAD), BF), jax.ShapeDtypeStruct((t, D_FF_PAD), BF),
                   jax.ShapeDtypeStruct((3, D_FF_PAD), F32), jax.ShapeDtypeStruct((1, D_FF_PAD), F32)],
        scratch_shapes=[pltpu.VMEM((tm + 8, D_FF_PAD), F32), pltpu.VMEM((tm + 8, D_FF_PAD), F32)],
        compiler_params=_params(56),
    )(dx2, gp, gp, up, wd4, fcw, fcb)


def _bwd_ffn_b(dgp, dup, dx2, x1, ya, wg4, wu4, wout, g_ffn, g_oa, tm):
    t = dx2.shape[0]
    nt = t // tm

    def body(dgp_ref, dup_ref, dx2_ref, x1_ref, ya_ref, wg_ref, wu_ref, wout_ref, gffn_ref, goa_ref,
             dx1_ref, dycn_ref, dya_ref, h2_ref, dgffn_ref, dgoa_ref):
        @pl.when(pl.program_id(0) == 0)
        def _():
            dgffn_ref[...] = jnp.zeros_like(dgffn_ref)
            dgoa_ref[...] = jnp.zeros_like(dgoa_ref)

        dh2 = jnp.zeros((tm, D_MODEL), F32)
        for s in range(N_CHIPS):
            lo, hi = s * D_FF_SLAB, (s + 1) * D_FF_SLAB
            dh2 = dh2 + _mm_nt(dgp_ref[:, lo:hi], wg_ref[s]) + _mm_nt(dup_ref[:, lo:hi], wu_ref[s])
        x1 = x1_ref[...]
        r2 = _rstd(x1)
        xh = x1 * r2
        h2_ref[...] = (xh * gffn_ref[...]).astype(BF)
        dgffn_ref[...] += _colsum(dh2 * xh)
        dx1 = dx2_ref[...] + _norm_bwd(dh2, xh, r2, gffn_ref[...])
        dx1_ref[...] = dx1
        dy = _mm_nt(dx1.astype(BF), wout_ref[...])
        dycn_ref[...] = dy[:, 0:CONV_W]
        dyan = dy[:, CONV_W:D_MODEL]
        yat = ya_ref[...]
        ra = _rstd(yat)
        yah = yat * ra
        dgoa_ref[...] += _colsum(dyan * yah)
        dya_ref[...] = _norm_bwd(dyan, yah, ra, goa_ref[...])

    return pl.pallas_call(
        body, name="bwd_ffn_b", grid=(nt,),
        in_specs=[_rows(tm, D_FF_PAD), _rows(tm, D_FF_PAD), _rows(tm, D_MODEL), _rows(tm, D_MODEL),
                  _rows(tm, ATTN_W), _const((N_CHIPS, D_MODEL, D_FF_SLAB)), _const((N_CHIPS, D_MODEL, D_FF_SLAB)),
                  _const((D_MODEL, D_MODEL)), _const((1, D_MODEL)), _const((1, ATTN_W))],
        out_specs=[_rows(tm, D_MODEL), _rows(tm, CONV_W), _rows(tm, ATTN_W), _rows(tm, D_MODEL),
                   pl.BlockSpec((1, D_MODEL), lambda i: (0, 0)), pl.BlockSpec((1, ATTN_W), lambda i: (0, 0))],
        out_shape=[jax.ShapeDtypeStruct((t, D_MODEL), F32), jax.ShapeDtypeStruct((t, CONV_W), F32),
                   jax.ShapeDtypeStruct((t, ATTN_W), F32), jax.ShapeDtypeStruct((t, D_MODEL), BF),
                   jax.ShapeDtypeStruct((1, D_MODEL), F32), jax.ShapeDtypeStruct((1, ATTN_W), F32)],
        compiler_params=_params(48),
    )(dgp, dup, dx2, x1, ya, wg4, wu4, wout, g_ffn, g_oa)


def _bwd_mix(x, dx1, zbcx, dycn, dqkv, win4, conv_w, conv_b, g_oc, g_mix, tm):
    t = x.shape[0]
    nt = t // tm

    def body(x_ref, dx1_ref, z_ref, zh_ref, dycn_ref, dqkv_ref, w_ref, cw_ref, cb_ref, goc_ref, g_ref,
             gx_ref, h1_ref, dz_ref, dcw_ref, dcb_ref, dgoc_ref, dg_ref, ubuf, dbuf):
        i = pl.program_id(0)

        @pl.when(i == 0)
        def _():
            dbuf[tm:tm + 8, :] = jnp.zeros((8, CONV_W), F32)
            dcw_ref[...] = jnp.zeros_like(dcw_ref)
            dcb_ref[...] = jnp.zeros_like(dcb_ref)
            dgoc_ref[...] = jnp.zeros_like(dgoc_ref)
            dg_ref[...] = jnp.zeros_like(dg_ref)

        not_first_tile = i < nt - 1
        zb = z_ref[:, 0:512]
        zc = z_ref[:, 512:1024]
        zx = z_ref[:, 1024:1536]
        u = zc * zx
        ubuf[0:8, :] = jnp.where(not_first_tile, zh_ref[:, 512:1024] * zh_ref[:, 1024:1536], 0.0)
        ubuf[8:8 + tm, :] = u
        u1 = ubuf[7:7 + tm, :]
        u2 = ubuf[6:6 + tm, :]
        w0, w1, w2 = cw_ref[0:1, :], cw_ref[1:2, :], cw_ref[2:3, :]
        cv = w0 * u2 + w1 * u1 + w2 * u + cb_ref[...]
        yc = zb * cv
        rc = _rstd(yc)
        ych = yc * rc
        dycn = dycn_ref[...]
        dgoc_ref[...] += _colsum(dycn * ych)
        dyc = _norm_bwd(dycn, ych, rc, goc_ref[...])
        dcv = dyc * zb
        dcb_ref[...] += _colsum(dcv)
        dcw_ref[0:1, :] += _colsum(dcv * u2)
        dcw_ref[1:2, :] += _colsum(dcv * u1)
        dcw_ref[2:3, :] += _colsum(dcv * u)
        dbuf[0:tm, :] = dcv
        du = w2 * dcv + w1 * dbuf[1:1 + tm, :] + w0 * dbuf[2:2 + tm, :]
        dbuf[tm:tm + 8, :] = dbuf[0:8, :]
        dz_ref[:, 0:512] = (dyc * cv).astype(BF)
        dz_ref[:, 512:1024] = (du * zx).astype(BF)
        dz_ref[:, 1024:1536] = (du * zc).astype(BF)
        dz_ref[:, 1536:3072] = dqkv_ref[...].astype(BF)
        dh1 = jnp.zeros((tm, D_MODEL), F32)
        for s in range(N_CHIPS):
            dh1 = dh1 + _mm_nt(dz_ref[:, s * IN_SLAB:(s + 1) * IN_SLAB], w_ref[s])
        xt = x_ref[...]
        r1 = _rstd(xt)
        xh = xt * r1
        h1_ref[...] = (xh * g_ref[...]).astype(BF)
        dg_ref[...] += _colsum(dh1 * xh)
        gx_ref[...] = dx1_ref[...] + _norm_bwd(dh1, xh, r1, g_ref[...])

    def acc(width, rows=1):
        return pl.BlockSpec((rows, width), lambda i: (0, 0))

    return pl.pallas_call(
        body, name="bwd_mix", grid=(nt,),
        in_specs=[_rows(tm, D_MODEL, nt), _rows(tm, D_MODEL, nt), _rows(tm, 1536, nt), _halo(tm, 1536, nt),
                  _rows(tm, CONV_W, nt), _rows(tm, 1536, nt), _const((N_CHIPS, D_MODEL, IN_SLAB)),
                  _const((3, CONV_W)), _const((1, CONV_W)), _const((1, CONV_W)), _const((1, D_MODEL))],
        out_specs=[_rows(tm, D_MODEL, nt), _rows(tm, D_MODEL, nt), _rows(tm, 3072, nt),
                   acc(CONV_W, 3), acc(CONV_W), acc(CONV_W), acc(D_MODEL)],
        out_shape=[jax.ShapeDtypeStruct((t, D_MODEL), F32), jax.ShapeDtypeStruct((t, D_MODEL), BF),
                   jax.ShapeDtypeStruct((t, 3072), BF), jax.ShapeDtypeStruct((3, CONV_W), F32),
                   jax.ShapeDtypeStruct((1, CONV_W), F32), jax.ShapeDtypeStruct((1, CONV_W), F32),
                   jax.ShapeDtypeStruct((1, D_MODEL), F32)],
        scratch_shapes=[pltpu.VMEM((tm + 8, CONV_W), F32), pltpu.VMEM((tm + 8, CONV_W), F32)],
        compiler_params=_params(48),
    )(x, dx1, zbcx, zbcx, dycn, dqkv, win4, conv_w, conv_b, g_oc, g_mix)


def _wgrad(a, b, tn, tt, name):
    t, k = a.shape
    n = b.shape[1]

    def body(a_ref, b_ref, o_ref):
        @pl.when(pl.program_id(1) == 0)
        def _():
            o_ref[...] = jnp.zeros_like(o_ref)

        o_ref[...] += _mm_tn(a_ref[...].astype(BF), b_ref[...].astype(BF))

    return pl.pallas_call(
        body, name=name, grid=(n // tn, t // tt),
        in_specs=[pl.BlockSpec((tt, k), lambda j, i: (i, 0)), pl.BlockSpec((tt, tn), lambda j, i: (i, j))],
        out_specs=pl.BlockSpec((k, tn), lambda j, i: (0, j)),
        out_shape=jax.ShapeDtypeStruct((k, n), F32),
        compiler_params=_params(48, 2),
    )(a, b)


def _place():
    x, y, c = lax.axis_index("x"), lax.axis_index("y"), lax.axis_index("c")
    return x, y, c


def _chip_peer(x, y, k):
    return x ^ (k >> 1), y ^ (k & 1)


def _gather_weights(shards, pack):
    nw = len(shards)

    def body(*refs):
        ins = refs[:nw]
        pack_ref = refs[nw]
        outs = refs[nw + 1:2 * nw + 1]
        pack_out = refs[2 * nw + 1]
        send_sems, recv_sems, local_sems = refs[2 * nw + 2:]
        x, y, c = _place()
        me = 2 * x + y
        local, remote = [], []

        def sem(w, j):
            return w * 6 + j

        def push(src, dst, w, j, to):
            return pltpu.make_async_remote_copy(src_ref=src, dst_ref=dst, send_sem=send_sems.at[sem(w, j)],
                                                recv_sem=recv_sems.at[sem(w, j)], device_id=to, device_id_type=MESH)

        def half_rows(w, h):
            half = ins[w].shape[0] // 2
            return pl.ds(pl.multiple_of(h * half, 16), half)

        for w in range(nw):
            local.append(pltpu.make_async_copy(ins[w], outs[w].at[me], local_sems.at[w]))
            for k in (1, 2, 3):
                px, py = _chip_peer(x, y, k)
                mine = half_rows(w, c)
                remote.append(push(ins[w].at[mine], outs[w].at[me, mine], w, k - 1, (px, py, c)))
        local.append(pltpu.make_async_copy(pack_ref, pack_out.at[me], local_sems.at[nw]))
        for k in (1, 2, 3):
            px, py = _chip_peer(x, y, k)
            remote.append(push(pack_ref, pack_out.at[me], nw, k - 1, (px, py, c)))
        for cp in local + remote:
            cp.start()
        for w in range(nw):
            for k in (1, 2, 3):
                landed = outs[w].at[me ^ k, half_rows(w, c)]
                push(landed, landed, w, k - 1, (x, y, c)).wait_recv()
                fw = push(landed, landed, w, 2 + k, (x, y, 1 - c))
                fw.start()
                remote.append(fw)
        for k in (1, 2, 3):
            landed = pack_out.at[me ^ k]
            push(landed, landed, nw, k - 1, (x, y, c)).wait_recv()
        for w in range(nw):
            for k in (1, 2, 3):
                landed = outs[w].at[me ^ k, half_rows(w, 1 - c)]
                push(landed, landed, w, 2 + k, (x, y, c)).wait_recv()
        for cp in remote:
            cp.wait_send()
        for cp in local:
            cp.wait()

    any_spec = pl.BlockSpec(memory_space=pl.ANY)
    out_shape = [jax.ShapeDtypeStruct((N_CHIPS,) + s.shape, s.dtype) for s in shards]
    out_shape.append(jax.ShapeDtypeStruct((N_CHIPS,) + pack.shape, pack.dtype))
    return pl.pallas_call(
        body, name="gather_weights",
        in_specs=[any_spec] * (nw + 1), out_specs=[any_spec] * (nw + 1), out_shape=out_shape,
        scratch_shapes=[pltpu.SemaphoreType.DMA(((nw + 1) * 6,)), pltpu.SemaphoreType.DMA(((nw + 1) * 6,)),
                        pltpu.SemaphoreType.DMA((nw + 1,))],
    )(*shards, pack)


def _adamw(w, g, m, v):
    m = ADAM_B1 * m + (1.0 - ADAM_B1) * g
    v = ADAM_B2 * v + (1.0 - ADAM_B2) * (g * g)
    m_hat = m / (1.0 - ADAM_B1 ** ADAM_STEP)
    v_hat = v / (1.0 - ADAM_B2 ** ADAM_STEP)
    delta = -ADAM_LR * (m_hat / (jnp.sqrt(v_hat) + ADAM_EPS) + ADAM_WD * w)
    return delta, m, v


def _reduce_adamw(grad, w, m, v, col_sharded, name):
    kk, nn = grad.shape
    if col_sharded:
        r, cw = kk // 2, nn // N_CHIPS
    else:
        r, cw = kk // (2 * N_CHIPS), nn
    vr, vc = w.shape
    chunk = 64
    assert r % chunk == 0 and vr % chunk == 0 and vr <= 2 * r and vc <= cw

    def window(s, h):
        if col_sharded:
            return (pl.ds(pl.multiple_of(h * r, 8), r), pl.ds(pl.multiple_of(s * cw, 128), cw))
        return (pl.ds(pl.multiple_of((2 * s + h) * r, 8), r), slice(None))

    def body(g_hbm, w_ref, m_ref, v_ref, go_ref, do_ref, mo_ref, vo_ref,
             own, bufa, bufb, full, lsem, a_send, a_recv, b_send, b_recv, c_send, c_recv):
        x, y, c = _place()
        me = 2 * x + y
        sib = (x, y, 1 - c)
        for s in range(N_CHIPS):
            pltpu.make_async_copy(g_hbm.at[window(s, c)], own.at[s], lsem.at[s]).start()
            pltpu.make_async_remote_copy(
                src_ref=g_hbm.at[window(s, 1 - c)], dst_ref=bufa.at[s], send_sem=a_send.at[s], recv_sem=a_recv.at[s],
                device_id=sib, device_id_type=MESH).start()
        for s in range(N_CHIPS):
            pltpu.make_async_copy(g_hbm.at[window(s, c)], own.at[s], lsem.at[s]).wait()
            pltpu.make_async_remote_copy(
                src_ref=g_hbm.at[window(s, 1 - c)], dst_ref=bufa.at[s], send_sem=a_send.at[s], recv_sem=a_recv.at[s],
                device_id=sib, device_id_type=MESH).wait()

        def add_a(j, carry):
            rows = pl.ds(pl.multiple_of(j * chunk, 8), chunk)
            for s in range(N_CHIPS):
                own[s, rows, :] = own[s, rows, :] + bufa[s, rows, :]
            return carry

        lax.fori_loop(0, r // chunk, add_a, 0)
        sends = []
        for k in (1, 2, 3):
            px, py = _chip_peer(x, y, k)
            cp = pltpu.make_async_remote_copy(
                src_ref=own.at[me ^ k], dst_ref=bufb.at[k - 1], send_sem=b_send.at[k - 1], recv_sem=b_recv.at[k - 1],
                device_id=(px, py, c), device_id_type=MESH)
            cp.start()
            sends.append(cp)
        for cp in sends:
            cp.wait()
        mine = pl.multiple_of(c * r, 8)

        def add_b(j, carry):
            rows = pl.ds(pl.multiple_of(j * chunk, 8), chunk)
            tot = (own[me, rows, :] + bufb[0, rows, :]) + (bufb[1, rows, :] + bufb[2, rows, :])
            full[pl.ds(mine + pl.multiple_of(j * chunk, 8), chunk), :] = tot
            return carry

        lax.fori_loop(0, r // chunk, add_b, 0)
        half = full.at[pl.ds(mine, r), :]
        cp = pltpu.make_async_remote_copy(src_ref=half, dst_ref=half, send_sem=c_send, recv_sem=c_recv,
                                          device_id=sib, device_id_type=MESH)
        cp.start()
        cp.wait()

        def update(j, carry):
            rows = pl.ds(pl.multiple_of(j * chunk, 8), chunk)
            g = full[rows, 0:vc]
            delta, mn, vn = _adamw(w_ref[rows, :], g, m_ref[rows, :], v_ref[rows, :])
            go_ref[rows, :] = g
            do_ref[rows, :] = delta
            mo_ref[rows, :] = mn
            vo_ref[rows, :] = vn
            return carry

        lax.fori_loop(0, vr // chunk, update, 0)

    any_spec = pl.BlockSpec(memory_space=pl.ANY)
    vmem = pl.BlockSpec(memory_space=pltpu.VMEM)
    shard = jax.ShapeDtypeStruct((vr, vc), F32)
    return pl.pallas_call(
        body, name=name,
        in_specs=[any_spec, vmem, vmem, vmem], out_specs=[vmem] * 4, out_shape=[shard] * 4,
        scratch_shapes=[pltpu.VMEM((N_CHIPS, r, cw), F32), pltpu.VMEM((N_CHIPS, r, cw), F32),
                        pltpu.VMEM((3, r, cw), F32), pltpu.VMEM((2 * r, cw), F32),
                        pltpu.SemaphoreType.DMA((N_CHIPS,)), pltpu.SemaphoreType.DMA((N_CHIPS,)),
                        pltpu.SemaphoreType.DMA((N_CHIPS,)), pltpu.SemaphoreType.DMA((3,)),
                        pltpu.SemaphoreType.DMA((3,)), pltpu.SemaphoreType.DMA, pltpu.SemaphoreType.DMA],
        compiler_params=pltpu.CompilerParams(vmem_limit_bytes=56 * MIB),
    )(grad, w, m, v)


def _allreduce_small(pack):
    rows = pack.shape[0]

    def body(p_ref, o_ref, slots, send_sems, recv_sems):
        x, y, c = _place()
        me = 4 * x + 2 * y + c
        slots[me] = p_ref[...]
        sends = []
        for k in range(1, 8):
            cp = pltpu.make_async_remote_copy(
                src_ref=p_ref, dst_ref=slots.at[me], send_sem=send_sems.at[k - 1], recv_sem=recv_sems.at[k - 1],
                device_id=(x ^ (k >> 2), y ^ ((k >> 1) & 1), c ^ (k & 1)), device_id_type=MESH)
            cp.start()
            sends.append(cp)
        for cp in sends:
            cp.wait()
        tot = slots[0]
        for j in range(1, 8):
            tot = tot + slots[j]
        o_ref[...] = tot

    vmem = pl.BlockSpec(memory_space=pltpu.VMEM)
    return pl.pallas_call(
        body, name="allreduce_small", in_specs=[vmem], out_specs=vmem,
        out_shape=jax.ShapeDtypeStruct(pack.shape, F32),
        scratch_shapes=[pltpu.VMEM((8, rows, D_MODEL), F32), pltpu.SemaphoreType.DMA((7,)),
                        pltpu.SemaphoreType.DMA((7,))],
    )(pack)


def _adamw_small(ws, gs, ms, vs):
    n = len(ws)

    def body(*refs):
        w_refs, g_refs, m_refs, v_refs = refs[0:n], refs[n:2 * n], refs[2 * n:3 * n], refs[3 * n:4 * n]
        d_refs, mo_refs, vo_refs = refs[4 * n:5 * n], refs[5 * n:6 * n], refs[6 * n:7 * n]
        for j in range(n):
            delta, mn, vn = _adamw(w_refs[j][...], g_refs[j][...], m_refs[j][...], v_refs[j][...])
            d_refs[j][...] = delta
            mo_refs[j][...] = mn
            vo_refs[j][...] = vn

    vmem = pl.BlockSpec(memory_space=pltpu.VMEM)
    shapes = [jax.ShapeDtypeStruct(w.shape, F32) for w in ws]
    outs = pl.pallas_call(
        body, name="adamw_small", in_specs=[vmem] * (4 * n), out_specs=[vmem] * (3 * n), out_shape=shapes * 3,
    )(*ws, *gs, *ms, *vs)
    return outs[0:n], outs[n:2 * n], outs[2 * n:3 * n]


def _pad_slab(a, axis):
    pad = [(0, 0)] * a.ndim
    pad[axis] = (0, D_FF_SLAB - D_FF_SHARD)
    return jnp.pad(a, pad)


def _local_step(x, p, target, wts):
    (win4, wout, wg4, wu4, wd4, wpg, wpp4, conv_w, fcw, g_mix, conv_b, gq, gk, g_oc, g_oa, g_ffn, fcb, g_ple) = wts
    zbcx, qkv, ycn = _fwd_mix(x, g_mix, win4, conv_w, conv_b, g_oc, 512)
    merged = None
    for d in DILATIONS:
        merged = _attn_fwd(qkv, gq, gk, d, merged)
    ya, lse = merged
    x1, gp, up, act, ycat = _fwd_ffn(x, ycn, ya, wout, wg4, wu4, g_oa, g_ffn, fcw, fcb, 256)
    dx2, h3, ds, dpp, dg_ple, loss = _fwd_tail(x1, act, p, target, wd4, wpg, wpp4, g_ple, 256)
    dgp, dup, dfcw, dfcb = _bwd_ffn_a(dx2, gp, up, wd4, fcw, fcb, 256)
    dx1, dycn, dya, h2, dg_ffn, dg_oa = _bwd_ffn_b(dgp, dup, dx2, x1, ya, wg4, wu4, wout, g_ffn, g_oa, 256)
    dqkv, dgq, dgk = None, 0.0, 0.0
    for d in DILATIONS:
        dqkv, a, b = _attn_bwd(qkv, ya, lse, dya, gq, gk, d, dqkv)
        dgq, dgk = dgq + a, dgk + b
    grad_x, h1, dz, dcw, dcb, dg_oc, dg_mix = _bwd_mix(x, dx1, zbcx, dycn, dqkv, win4, conv_w, conv_b, g_oc, g_mix, 256)
    big = dict(
        w_in=_wgrad(h1, dz, 1536, 512, "wgrad_in"),
        w_out=_wgrad(ycat, dx1, 1024, 512, "wgrad_out"),
        w_gate=_wgrad(h2, dgp, 1536, 512, "wgrad_gate"),
        w_up=_wgrad(h2, dup, 1536, 512, "wgrad_up"),
        w_down=_wgrad(act, dx2, 512, 512, "wgrad_down"),
        w_ple_gate=_wgrad(h3, ds, 1024, 512, "wgrad_ple_gate"),
        w_ple_proj=_wgrad(p, dpp, 1024, 512, "wgrad_ple_proj"),
    )
    small = dict(g_mix=dg_mix, conv_w=dcw, conv_b=dcb, q_norm_g=dgq, k_norm_g=dgk, g_out_conv=dg_oc,
                 g_out_attn=dg_oa, g_ffn=dg_ffn, ffn_conv_w=dfcw, ffn_conv_b=dfcb, g_ple=dg_ple)
    return loss[0, 0], grad_x, big, small


_SMALL_ROWS = 24


def _pack_small(s):
    z64 = jnp.zeros((1, 1024 - 512 - 128), F32)
    rows = [s["g_mix"], s["g_ffn"], s["g_ple"],
            jnp.concatenate([s["conv_b"], s["g_out_conv"]], axis=1),
            jnp.concatenate([s["g_out_attn"], s["q_norm_g"], s["k_norm_g"], z64], axis=1),
            jnp.pad(s["conv_w"], ((0, 0), (0, 512))),
            s["ffn_conv_b"].reshape(3, 1024),
            s["ffn_conv_w"].reshape(9, 1024),
            jnp.zeros((_SMALL_ROWS - 20, 1024), F32)]
    return jnp.concatenate(rows, axis=0)


def _unpack_small(t):
    return dict(g_mix=t[0:1], g_ffn=t[1:2], g_ple=t[2:3], conv_b=t[3:4, 0:512], g_out_conv=t[3:4, 512:1024],
                g_out_attn=t[4:5, 0:512], q_norm_g=t[4:5, 512:576], k_norm_g=t[4:5, 576:640],
                conv_w=t[5:8, 0:512], ffn_conv_b=t[8:11].reshape(1, D_FF_PAD), ffn_conv_w=t[11:20].reshape(3, D_FF_PAD))


def _unpad_ff(a):
    r = a.shape[0]
    return a.reshape(r, N_CHIPS, D_FF_SLAB)[:, :, :D_FF_SHARD].reshape(r, N_CHIPS * D_FF_SHARD)


_BIG = ("w_in", "w_out", "w_gate", "w_up", "w_down", "w_ple_gate", "w_ple_proj")
_COL_SHARDED = dict(w_in=True, w_out=False, w_gate=True, w_up=True, w_down=False, w_ple_gate=False, w_ple_proj=True)
_WEIGHTS = ("g_mix", "w_in", "conv_w", "conv_b", "q_norm_g", "k_norm_g", "g_out_conv", "g_out_attn", "w_out",
            "g_ffn", "w_gate", "w_up", "ffn_conv_w", "ffn_conv_b", "w_down", "g_ple", "w_ple_gate", "w_ple_proj")


def kernel(x, p, g_mix, w_in, conv_w, conv_b, q_norm_g, k_norm_g, g_out_conv, g_out_attn, w_out, g_ffn, w_gate, w_up, ffn_conv_w, ffn_conv_b, w_down, g_ple, w_ple_gate, w_ple_proj, loss_target, m_g_mix, m_w_in, m_conv_w, m_conv_b, m_q_norm_g, m_k_norm_g, m_g_out_conv, m_g_out_attn, m_w_out, m_g_ffn, m_w_gate, m_w_up, m_ffn_conv_w, m_ffn_conv_b, m_w_down, m_g_ple, m_w_ple_gate, m_w_ple_proj, v_g_mix, v_w_in, v_conv_w, v_conv_b, v_q_norm_g, v_k_norm_g, v_g_out_conv, v_g_out_attn, v_w_out, v_g_ffn, v_w_gate, v_w_up, v_ffn_conv_w, v_ffn_conv_b, v_w_down, v_g_ple, v_w_ple_gate, v_w_ple_proj):
    w = dict(g_mix=g_mix, w_in=w_in, conv_w=conv_w, conv_b=conv_b, q_norm_g=q_norm_g, k_norm_g=k_norm_g,
             g_out_conv=g_out_conv, g_out_attn=g_out_attn, w_out=w_out, g_ffn=g_ffn, w_gate=w_gate, w_up=w_up,
             ffn_conv_w=ffn_conv_w, ffn_conv_b=ffn_conv_b, w_down=w_down, g_ple=g_ple, w_ple_gate=w_ple_gate,
             w_ple_proj=w_ple_proj)
    m = dict(g_mix=m_g_mix, w_in=m_w_in, conv_w=m_conv_w, conv_b=m_conv_b, q_norm_g=m_q_norm_g, k_norm_g=m_k_norm_g,
             g_out_conv=m_g_out_conv, g_out_attn=m_g_out_attn, w_out=m_w_out, g_ffn=m_g_ffn, w_gate=m_w_gate,
             w_up=m_w_up, ffn_conv_w=m_ffn_conv_w, ffn_conv_b=m_ffn_conv_b, w_down=m_w_down, g_ple=m_g_ple,
             w_ple_gate=m_w_ple_gate, w_ple_proj=m_w_ple_proj)
    v = dict(g_mix=v_g_mix, w_in=v_w_in, conv_w=v_conv_w, conv_b=v_conv_b, q_norm_g=v_q_norm_g, k_norm_g=v_k_norm_g,
             g_out_conv=v_g_out_conv, g_out_attn=v_g_out_attn, w_out=v_w_out, g_ffn=v_g_ffn, w_gate=v_w_gate,
             w_up=v_w_up, ffn_conv_w=v_ffn_conv_w, ffn_conv_b=v_ffn_conv_b, w_down=v_w_down, g_ple=v_g_ple,
             w_ple_gate=v_w_ple_gate, w_ple_proj=v_w_ple_proj)
    mats = [k for k, a in w.items() if a.ndim == 3]
    w = {k: (a[0] if k in mats else a) for k, a in w.items()}
    m = {k: (a[0] if k in mats else a) for k, a in m.items()}
    v = {k: (a[0] if k in mats else a) for k, a in v.items()}
    chip = 2 * lax.axis_index("x") + lax.axis_index("y")

    shards = [w["w_in"].astype(BF), w["w_out"].astype(BF), _pad_slab(w["w_gate"], 1).astype(BF),
              _pad_slab(w["w_up"], 1).astype(BF), _pad_slab(w["w_down"], 0).astype(BF),
              w["w_ple_gate"].astype(BF), w["w_ple_proj"].astype(BF)]
    pack = jnp.pad(jnp.concatenate([w["conv_w"], _pad_slab(w["ffn_conv_w"], 1)], axis=1), ((0, 5), (0, 128)))
    win4, wout4, wg4, wu4, wd4, wpg4, wpp4, pack4 = _gather_weights(shards, pack)
    conv_w_full = pack4[:, 0:3, 0:128].transpose(1, 0, 2).reshape(3, CONV_W)
    fcw_full = pack4[:, 0:3, 128:128 + D_FF_SLAB].transpose(1, 0, 2).reshape(3, D_FF_PAD)
    fcb_pad = _pad_slab(w["ffn_conv_b"].reshape(N_CHIPS, D_FF_SHARD), 1).reshape(1, D_FF_PAD)
    wts = (win4, wout4.reshape(D_MODEL, D_MODEL), wg4, wu4, wd4, wpg4.reshape(D_MODEL, D_MODEL), wpp4,
           conv_w_full, fcw_full, w["g_mix"], w["conv_b"], w["q_norm_g"], w["k_norm_g"], w["g_out_conv"],
           w["g_out_attn"], w["g_ffn"], fcb_pad, w["g_ple"])

    loss, grad_x, big, small = _local_step(x[0], p[0, 0], loss_target[0], wts)
    loss = lax.psum(loss, ("x", "y", "c"))

    grads, deltas, new_m, new_v = {}, {}, {}, {}
    for name in _BIG:
        grads[name], deltas[name], new_m[name], new_v[name] = _reduce_adamw(
            big[name], w[name], m[name], v[name], _COL_SHARDED[name], "reduce_" + name)
    tot = _unpack_small(_allreduce_small(_pack_small(small)))
    tot["conv_w"] = lax.dynamic_slice_in_dim(tot["conv_w"], chip * 128, 128, axis=1)
    tot["ffn_conv_w"] = lax.dynamic_slice_in_dim(tot["ffn_conv_w"], chip * D_FF_SLAB, D_FF_SLAB, axis=1)[:, :D_FF_SHARD]
    tot["ffn_conv_b"] = _unpad_ff(tot["ffn_conv_b"])
    names = [n for n in _WEIGHTS if n not in _BIG]
    d_s, m_s, v_s = _adamw_small([w[n] for n in names], [tot[n] for n in names], [m[n] for n in names],
                                 [v[n] for n in names])
    for j, n in enumerate(names):
        grads[n], deltas[n], new_m[n], new_v[n] = tot[n], d_s[j], m_s[j], v_s[j]

    out = [loss, grad_x[None]]
    for group in (grads, deltas, new_m, new_v):
        out += [group[n][None] if n in mats else group[n] for n in _WEIGHTS]
    return tuple(out)
```

```python
import jax
import jax.numpy as jnp
from jax import lax
from jax.experimental import pallas as pl
from jax.experimental.pallas import tpu as pltpu

D_MODEL = 1024
CONV_W = 512
N_HEADS = 8
HEAD_DIM = 64
ATTN_W = 512
D_FF_SHARD = 704
D_FF_SLAB = 768
D_FF_PAD = 4 * D_FF_SLAB
IN_SLAB = 768
PLE_DIM = 256
N_CHIPS = 4
QBLK = 128
DILATIONS = (1, 4, 16)
EPS = 1e-6
NEG = -1e30
MESH = pl.DeviceIdType.MESH

ADAM_LR = 0.001
ADAM_B1 = 0.9
ADAM_B2 = 0.999
ADAM_EPS = 1e-08
ADAM_WD = 0.01
ADAM_STEP = 10

BF = jnp.bfloat16
F32 = jnp.float32
MIB = 1024 * 1024


def _mm(a, b):
    return jnp.dot(a, b, preferred_element_type=F32)


def _mm_nt(a, b):
    return lax.dot_general(a, b, (((1,), (1,)), ((), ())), preferred_element_type=F32)


def _mm_tn(a, b):
    return lax.dot_general(a, b, (((0,), (0,)), ((), ())), preferred_element_type=F32)


def _rstd(a):
    return lax.rsqrt(jnp.mean(a * a, axis=-1, keepdims=True) + EPS)


def _norm_bwd(dy, xh, r, g):
    dxh = dy * g
    return r * (dxh - xh * jnp.mean(dxh * xh, axis=-1, keepdims=True))


def _colsum(a):
    return jnp.sum(a, axis=0, keepdims=True)


def _head_mean(a, gm_ref):
    hi = a.astype(BF)
    lo = (a - hi.astype(F32)).astype(BF)
    return _mm(hi, gm_ref[...]) + _mm(lo, gm_ref[...])


def _params(vmem_mib, n_grid=1):
    return pltpu.CompilerParams(dimension_semantics=("arbitrary",) * n_grid, vmem_limit_bytes=vmem_mib * MIB)


def _const(shape):
    n = len(shape)
    return pl.BlockSpec(shape, lambda *_: (0,) * n, pipeline_mode=pl.Buffered(1))


def _rows(tm, width, rev_of=None):
    if rev_of is None:
        return pl.BlockSpec((tm, width), lambda i: (i, 0))
    return pl.BlockSpec((tm, width), lambda i: (rev_of - 1 - i, 0))


def _halo(tm, width, nt):
    return pl.BlockSpec((8, width), lambda i: (jnp.maximum((nt - 1 - i) * (tm // 8) - 1, 0), 0))


def _fwd_mix(x, g_mix, win4, conv_w, conv_b, g_oc, gm, gq8, gk8, tm):
    t = x.shape[0]
    nt = t // tm

    def body(x_ref, g_ref, w_ref, cw_ref, cb_ref, goc_ref, gm_ref, gq_ref, gk_ref,
             zbcx_ref, qkv_ref, ycn_ref, qkn_ref, ubuf):
        @pl.when(pl.program_id(0) == 0)
        def _():
            ubuf[0:8, :] = jnp.zeros((8, CONV_W), F32)

        xt = x_ref[...]
        h = ((xt * _rstd(xt)) * g_ref[...]).astype(BF)
        zbcx_ref[:, 0:IN_SLAB] = _mm(h, w_ref[0])
        zbcx_ref[:, IN_SLAB:2 * IN_SLAB] = _mm(h, w_ref[1])
        qkv_ref[:, 0:IN_SLAB] = _mm(h, w_ref[2])
        qkv_ref[:, IN_SLAB:2 * IN_SLAB] = _mm(h, w_ref[3])
        u = zbcx_ref[:, 512:1024] * zbcx_ref[:, 1024:1536]
        ubuf[8:8 + tm, :] = u
        cv = (cw_ref[0:1, :] * ubuf[6:6 + tm, :] + cw_ref[1:2, :] * ubuf[7:7 + tm, :]
              + cw_ref[2:3, :] * u + cb_ref[...])
        ubuf[0:8, :] = ubuf[tm:tm + 8, :]
        yc = zbcx_ref[:, 0:512] * cv
        ycn_ref[...] = ((yc * _rstd(yc)) * goc_ref[...]).astype(BF)
        zq = qkv_ref[:, 0:512]
        zk = qkv_ref[:, 512:1024]
        rq = lax.rsqrt(_head_mean(zq * zq, gm_ref) + EPS)
        rk = lax.rsqrt(_head_mean(zk * zk, gm_ref) + EPS)
        qkn_ref[:, 0:512] = ((zq * rq) * gq_ref[...]) * (HEAD_DIM ** -0.5)
        qkn_ref[:, 512:1024] = (zk * rk) * gk_ref[...]

    return pl.pallas_call(
        body, name="fwd_mix", grid=(nt,),
        in_specs=[_rows(tm, D_MODEL), _const((1, D_MODEL)), _const((N_CHIPS, D_MODEL, IN_SLAB)),
                  _const((3, CONV_W)), _const((1, CONV_W)), _const((1, CONV_W)), _const((ATTN_W, ATTN_W)),
                  _const((1, ATTN_W)), _const((1, ATTN_W))],
        out_specs=[_rows(tm, 1536), _rows(tm, 1536), _rows(tm, CONV_W), _rows(tm, 1024)],
        out_shape=[jax.ShapeDtypeStruct((t, 1536), F32), jax.ShapeDtypeStruct((t, 1536), F32),
                   jax.ShapeDtypeStruct((t, CONV_W), BF), jax.ShapeDtypeStruct((t, 1024), F32)],
        scratch_shapes=[pltpu.VMEM((tm + 8, CONV_W), F32)],
        compiler_params=_params(48),
    )(x, g_mix, win4, conv_w, conv_b, g_oc, gm, gq8, gk8)


def _alibi(h):
    return 2.0 ** (-(h + 1))


CHUNK = 2048


def _mask_tables():
    slopes = jnp.asarray([_alibi(h) for h in range(N_HEADS)], F32)[:, None, None]

    def table(step):
        valid = (step >= 0) & (step <= QBLK)
        return jnp.stack([jnp.where(valid[None], -slopes * (step * d)[None].astype(F32), NEG) for d in DILATIONS])

    i = jnp.arange(QBLK)[:, None]
    j2 = jnp.arange(2 * QBLK)[None, :]
    r2 = jnp.arange(2 * QBLK)[:, None]
    j = jnp.arange(QBLK)[None, :]
    return table(i + QBLK - j2), table(r2 - j)


def _attn_fwd(qkn, qkv, mb):
    t = qkn.shape[0]
    nc = t // CHUNK

    def body(qc_ref, kp_ref, kc_ref, vp_ref, vc_ref, mb_ref, o_ref, l_ref, ob0, ob1, ob2, lb0, lb1, lb2):
        first = pl.program_id(1) == 0
        lane = lax.broadcasted_iota(jnp.int32, (QBLK, 128), 1)
        lo_half = lane < HEAD_DIM
        kj = lax.broadcasted_iota(jnp.int32, (QBLK, 2 * QBLK), 1)
        no_prev = first & (kj < QBLK)
        obs, lbs = (ob0, ob1, ob2), (lb0, lb1, lb2)
        for di, d in enumerate(DILATIONS):
            span = d * QBLK
            for r in range(d):
                for b in range(CHUNK // span):
                    rows = pl.ds(r + span * b, QBLK, stride=d)
                    q2 = qc_ref[rows, :]
                    if b == 0:
                        tail = pl.ds(CHUNK - span + r, QBLK, stride=d)
                        k2 = jnp.concatenate([kp_ref[tail, :], kc_ref[rows, :]], axis=0)
                        v2 = jnp.concatenate([vp_ref[tail, :], vc_ref[rows, :]], axis=0)
                    else:
                        both = pl.ds(r + span * (b - 1), 2 * QBLK, stride=d)
                        k2 = kc_ref[both, :]
                        v2 = vc_ref[both, :]
                    k2 = k2.astype(BF)
                    v2 = v2.astype(BF)
                    oh, lh = [], []
                    for hh in range(2):
                        sel = lo_half if hh == 0 else jnp.logical_not(lo_half)
                        qh = jnp.where(sel, q2, 0.0).astype(BF)
                        s = _mm_nt(qh, k2) + mb_ref[di, hh]
                        if b == 0:
                            s = jnp.where(no_prev, NEG, s)
                        m = jnp.max(s, axis=-1, keepdims=True)
                        e = jnp.exp(s - m)
                        den = jnp.sum(e, axis=-1, keepdims=True)
                        oh.append(_mm(e.astype(BF), v2) / den)
                        lh.append(jnp.broadcast_to(m + jnp.log(den), (QBLK, 128)))
                    obs[di][rows, :] = jnp.where(lo_half, oh[0], oh[1])
                    lbs[di][rows, :] = jnp.where(lo_half, lh[0], lh[1])
        for c0 in range(0, CHUNK, 256):
            rs = slice(c0, c0 + 256)
            l0, l1, l2 = lb0[rs, :], lb1[rs, :], lb2[rs, :]
            mx = jnp.maximum(jnp.maximum(l0, l1), l2)
            w0, w1, w2 = jnp.exp(l0 - mx), jnp.exp(l1 - mx), jnp.exp(l2 - mx)
            tot = w0 + w1 + w2
            o_ref[rs, :] = (ob0[rs, :] * w0 + ob1[rs, :] * w1 + ob2[rs, :] * w2) / tot
            l_ref[rs, :] = mx + jnp.log(tot)

    def cur(col):
        return pl.BlockSpec((CHUNK, 128), lambda hp, n: (n, col + hp))

    def prv(col):
        return pl.BlockSpec((CHUNK, 128), lambda hp, n: (jnp.maximum(n - 1, 0), col + hp))

    out = pl.BlockSpec((CHUNK, 128), lambda hp, n: (n, hp))
    return pl.pallas_call(
        body, name="attn_fwd", grid=(N_HEADS // 2, nc),
        in_specs=[cur(0), prv(4), cur(4), prv(8), cur(8),
                  pl.BlockSpec((3, 2, QBLK, 2 * QBLK), lambda hp, n: (0, hp, 0, 0))],
        out_specs=[out, out],
        out_shape=[jax.ShapeDtypeStruct((t, ATTN_W), F32)] * 2,
        scratch_shapes=[pltpu.VMEM((CHUNK, 128), F32)] * 6,
        compiler_params=_params(48, 2),
    )(qkn, qkn, qkn, qkv, qkv, mb)


def _attn_bwd(qkn, qkv, o, lse, do, mb, mbk):
    t = qkn.shape[0]
    nc = t // CHUNK

    def body(qc_ref, qn_ref, kp_ref, kc_ref, vp_ref, vc_ref, oc_ref, on_ref, lc_ref, ln_ref, dc_ref, dn_ref,
             mb_ref, mbk_ref, dq_ref, dk_ref, dv_ref):
        first = pl.program_id(1) == 0
        last = pl.program_id(1) == nc - 1
        lane = lax.broadcasted_iota(jnp.int32, (QBLK, 128), 1)
        lo_half = lane < HEAD_DIM
        lane2 = lax.broadcasted_iota(jnp.int32, (2 * QBLK, 128), 1)
        lo_half2 = lane2 < HEAD_DIM
        kj = lax.broadcasted_iota(jnp.int32, (QBLK, 2 * QBLK), 1)
        no_prev = first & (kj < QBLK)
        ri = lax.broadcasted_iota(jnp.int32, (2 * QBLK, QBLK), 0)
        no_next = last & (ri >= QBLK)

        def head_cols(x, sel):
            return jnp.max(jnp.where(sel, x, NEG), axis=-1, keepdims=True)

        for di, d in enumerate(DILATIONS):
            span = d * QBLK
            nbk = CHUNK // span
            for r in range(d):
                for b in range(nbk):
                    rows = pl.ds(r + span * b, QBLK, stride=d)
                    q2 = qc_ref[rows, :]
                    do2 = dc_ref[rows, :]
                    dd2 = do2 * oc_ref[rows, :]
                    l2 = lc_ref[rows, :]
                    if b == 0:
                        tail = pl.ds(CHUNK - span + r, QBLK, stride=d)
                        k2 = jnp.concatenate([kp_ref[tail, :], kc_ref[rows, :]], axis=0)
                        v2 = jnp.concatenate([vp_ref[tail, :], vc_ref[rows, :]], axis=0)
                    else:
                        both = pl.ds(r + span * (b - 1), 2 * QBLK, stride=d)
                        k2 = kc_ref[both, :]
                        v2 = vc_ref[both, :]
                    k2 = k2.astype(BF)
                    v2 = v2.astype(BF)
                    dqh = []
                    for hh in range(2):
                        sel = lo_half if hh == 0 else jnp.logical_not(lo_half)
                        qh = jnp.where(sel, q2, 0.0).astype(BF)
                        doh = jnp.where(sel, do2, 0.0).astype(BF)
                        dcol = jnp.sum(jnp.where(sel, dd2, 0.0), axis=-1, keepdims=True)
                        lcol = head_cols(l2, sel)
                        s = _mm_nt(qh, k2) + mb_ref[di, hh]
                        if b == 0:
                            s = jnp.where(no_prev, NEG, s)
                        p = jnp.exp(s - lcol)
                        dp = _mm_nt(doh, v2)
                        ds = (p * (dp - dcol)).astype(BF)
                        dqh.append(_mm(ds, k2))
                    dq2 = jnp.where(lo_half, dqh[0], dqh[1])
                    if di == 0:
                        dq_ref[rows, :] = dq2
                    else:
                        dq_ref[rows, :] = dq_ref[rows, :] + dq2
                    k1 = kc_ref[rows, :].astype(BF)
                    v1 = vc_ref[rows, :].astype(BF)
                    if b == nbk - 1:
                        head = pl.ds(r, QBLK, stride=d)
                        q4 = jnp.concatenate([q2, qn_ref[head, :]], axis=0)
                        do4 = jnp.concatenate([do2, dn_ref[head, :]], axis=0)
                        dd4 = jnp.concatenate([dd2, dn_ref[head, :] * on_ref[head, :]], axis=0)
                        l4 = jnp.concatenate([l2, ln_ref[head, :]], axis=0)
                    else:
                        both = pl.ds(r + span * b, 2 * QBLK, stride=d)
                        q4 = qc_ref[both, :]
                        do4 = dc_ref[both, :]
                        dd4 = do4 * oc_ref[both, :]
                        l4 = lc_ref[both, :]
                    dkh, dvh = [], []
                    for hh in range(2):
                        sel = lo_half2 if hh == 0 else jnp.logical_not(lo_half2)
                        qh = jnp.where(sel, q4, 0.0).astype(BF)
                        doh = jnp.where(sel, do4, 0.0).astype(BF)
                        dcol = jnp.sum(jnp.where(sel, dd4, 0.0), axis=-1, keepdims=True)
                        lcol = head_cols(l4, sel)
                        s = _mm_nt(qh, k1) + mbk_ref[di, hh]
                        if b == nbk - 1:
                            s = jnp.where(no_next, NEG, s)
                        p = jnp.exp(s - lcol)
                        dvh.append(_mm_tn(p.astype(BF), doh))
                        dp = _mm_nt(doh, v1)
                        ds = (p * (dp - dcol)).astype(BF)
                        dkh.append(_mm_tn(ds, qh))
                    dk2 = jnp.where(lo_half, dkh[0], dkh[1])
                    dv2 = jnp.where(lo_half, dvh[0], dvh[1])
                    if di == 0:
                        dk_ref[rows, :] = dk2
                        dv_ref[rows, :] = dv2
                    else:
                        dk_ref[rows, :] = dk_ref[rows, :] + dk2
                        dv_ref[rows, :] = dv_ref[rows, :] + dv2

    def at(shift, col):
        return pl.BlockSpec((CHUNK, 128), lambda hp, n: (jnp.clip(n + shift, 0, nc - 1), col + hp))

    out = pl.BlockSpec((CHUNK, 128), lambda hp, n: (n, hp))
    return pl.pallas_call(
        body, name="attn_bwd", grid=(N_HEADS // 2, nc),
        in_specs=[at(0, 0), at(1, 0), at(-1, 4), at(0, 4), at(-1, 8), at(0, 8),
                  at(0, 0), at(1, 0), at(0, 0), at(1, 0), at(0, 0), at(1, 0),
                  pl.BlockSpec((3, 2, QBLK, 2 * QBLK), lambda hp, n: (0, hp, 0, 0)),
                  pl.BlockSpec((3, 2, 2 * QBLK, QBLK), lambda hp, n: (0, hp, 0, 0))],
        out_specs=[out, out, out],
        out_shape=[jax.ShapeDtypeStruct((t, ATTN_W), F32)] * 3,
        compiler_params=_params(56, 2),
    )(qkn, qkn, qkn, qkn, qkv, qkv, o, o, lse, lse, do, do, mb, mbk)


def _fwd_ffn(x, ycn, ya, wout, wg4, wu4, g_oa, g_ffn, fcw, fcb, tm):
    t = x.shape[0]
    nt = t // tm

    def body(x_ref, ycn_ref, ya_ref, wout_ref, wg_ref, wu_ref, goa_ref, gffn_ref, fcw_ref, fcb_ref,
             x1_ref, gp_ref, up_ref, act_ref, ycat_ref, cbuf):
        @pl.when(pl.program_id(0) == 0)
        def _():
            cbuf[0:8, :] = jnp.zeros((8, D_FF_PAD), F32)

        yat = ya_ref[...]
        yan = ((yat * _rstd(yat)) * goa_ref[...]).astype(BF)
        ycn = ycn_ref[...]
        ycat_ref[:, 0:CONV_W] = ycn
        ycat_ref[:, CONV_W:D_MODEL] = yan
        x1 = x_ref[...] + _mm(ycn, wout_ref[0:CONV_W, :]) + _mm(yan, wout_ref[CONV_W:D_MODEL, :])
        x1_ref[...] = x1
        h2 = ((x1 * _rstd(x1)) * gffn_ref[...]).astype(BF)
        for s in range(N_CHIPS):
            lo, hi = s * D_FF_SLAB, (s + 1) * D_FF_SLAB
            gps = _mm(h2, wg_ref[s])
            ups = _mm(h2, wu_ref[s])
            gp_ref[:, lo:hi] = gps
            up_ref[:, lo:hi] = ups
            cbuf[8:8 + tm, lo:hi] = gps
            gate = (fcw_ref[0:1, lo:hi] * cbuf[6:6 + tm, lo:hi] + fcw_ref[1:2, lo:hi] * cbuf[7:7 + tm, lo:hi]
                    + fcw_ref[2:3, lo:hi] * gps + fcb_ref[:, lo:hi])
            act_ref[:, lo:hi] = ((gate * jax.nn.sigmoid(gate)) * ups).astype(BF)
        cbuf[0:8, :] = cbuf[tm:tm + 8, :]

    return pl.pallas_call(
        body, name="fwd_ffn", grid=(nt,),
        in_specs=[_rows(tm, D_MODEL), _rows(tm, CONV_W), _rows(tm, ATTN_W), _const((D_MODEL, D_MODEL)),
                  _const((N_CHIPS, D_MODEL, D_FF_SLAB)), _const((N_CHIPS, D_MODEL, D_FF_SLAB)),
                  _const((1, ATTN_W)), _const((1, D_MODEL)), _const((3, D_FF_PAD)), _const((1, D_FF_PAD))],
        out_specs=[_rows(tm, D_MODEL), _rows(tm, D_FF_PAD), _rows(tm, D_FF_PAD), _rows(tm, D_FF_PAD),
                   _rows(tm, D_MODEL)],
        out_shape=[jax.ShapeDtypeStruct((t, D_MODEL), F32), jax.ShapeDtypeStruct((t, D_FF_PAD), F32),
                   jax.ShapeDtypeStruct((t, D_FF_PAD), F32), jax.ShapeDtypeStruct((t, D_FF_PAD), BF),
                   jax.ShapeDtypeStruct((t, D_MODEL), BF)],
        scratch_shapes=[pltpu.VMEM((tm + 8, D_FF_PAD), F32)],
        compiler_params=_params(56),
    )(x, ycn, ya, wout, wg4, wu4, g_oa, g_ffn, fcw, fcb)


def _fwd_tail(x1, act, p, target, wd4, wpg, wpp4, g_ple, tm):
    t = x1.shape[0]
    nt = t // tm

    def body(x1_ref, act_ref, p_ref, tgt_ref, wd_ref, wpg_ref, wpp_ref, g_ref,
             dx2_ref, h3_ref, ds_ref, dpp_ref, dg_ref, loss_ref, lacc):
        i = pl.program_id(0)

        @pl.when(i == 0)
        def _():
            dg_ref[...] = jnp.zeros_like(dg_ref)
            lacc[...] = jnp.zeros_like(lacc)

        x2 = x1_ref[...]
        for s in range(N_CHIPS):
            x2 = x2 + _mm(act_ref[:, s * D_FF_SLAB:(s + 1) * D_FF_SLAB], wd_ref[s])
        r3 = _rstd(x2)
        xh = x2 * r3
        h3 = (xh * g_ref[...]).astype(BF)
        h3_ref[...] = h3
        sg = jax.nn.sigmoid(_mm(h3, wpg_ref[...]))
        pb = p_ref[...].astype(BF)
        pp = jnp.concatenate([_mm(pb, wpp_ref[s]) for s in range(N_CHIPS)], axis=1)
        err = (x2 + sg * pp) - tgt_ref[...]
        lacc[...] += _colsum(err * err)
        dx3 = err * (1.0 / D_MODEL)
        dpp_ref[...] = (dx3 * sg).astype(BF)
        dsb = ((dx3 * pp) * (sg * (1.0 - sg))).astype(BF)
        ds_ref[...] = dsb
        dh3 = _mm_nt(dsb, wpg_ref[...])
        dg_ref[...] += _colsum(dh3 * xh)
        dx2_ref[...] = dx3 + _norm_bwd(dh3, xh, r3, g_ref[...])

        @pl.when(i == nt - 1)
        def _():
            loss_ref[...] = jnp.full((1, 128), jnp.sum(lacc[...]) * (0.5 / D_MODEL), F32)

    return pl.pallas_call(
        body, name="fwd_tail", grid=(nt,),
        in_specs=[_rows(tm, D_MODEL), _rows(tm, D_FF_PAD), _rows(tm, PLE_DIM), _rows(tm, D_MODEL),
                  _const((N_CHIPS, D_FF_SLAB, D_MODEL)), _const((D_MODEL, D_MODEL)),
                  _const((N_CHIPS, PLE_DIM, PLE_DIM)), _const((1, D_MODEL))],
        out_specs=[_rows(tm, D_MODEL), _rows(tm, D_MODEL), _rows(tm, D_MODEL), _rows(tm, D_MODEL),
                   pl.BlockSpec((1, D_MODEL), lambda i: (0, 0)), pl.BlockSpec((1, 128), lambda i: (0, 0))],
        out_shape=[jax.ShapeDtypeStruct((t, D_MODEL), F32), jax.ShapeDtypeStruct((t, D_MODEL), BF),
                   jax.ShapeDtypeStruct((t, D_MODEL), BF), jax.ShapeDtypeStruct((t, D_MODEL), BF),
                   jax.ShapeDtypeStruct((1, D_MODEL), F32), jax.ShapeDtypeStruct((1, 128), F32)],
        scratch_shapes=[pltpu.VMEM((1, D_MODEL), F32)],
        compiler_params=_params(48),
    )(x1, act, p, target, wd4, wpg, wpp4, g_ple)


def _bwd_ffn_a(dx2, gp, up, wd4, fcw, fcb, tm):
    t = dx2.shape[0]
    nt = t // tm

    def body(dx2_ref, gp_ref, gph_ref, up_ref, wd_ref, fcw_ref, fcb_ref,
             dgp_ref, dup_ref, dfcw_ref, dfcb_ref, cbuf, dbuf):
        i = pl.program_id(0)

        @pl.when(i == 0)
        def _():
            dbuf[tm:tm + 8, :] = jnp.zeros((8, D_FF_PAD), F32)
            dfcw_ref[...] = jnp.zeros_like(dfcw_ref)
            dfcb_ref[...] = jnp.zeros_like(dfcb_ref)

        not_first_tile = i < nt - 1
        dx2b = dx2_ref[...].astype(BF)
        for s in range(N_CHIPS):
            lo, hi = s * D_FF_SLAB, (s + 1) * D_FF_SLAB
            gps = gp_ref[:, lo:hi]
            cbuf[0:8, lo:hi] = jnp.where(not_first_tile, gph_ref[:, lo:hi], 0.0)
            cbuf[8:8 + tm, lo:hi] = gps
            g1 = cbuf[7:7 + tm, lo:hi]
            g2 = cbuf[6:6 + tm, lo:hi]
            w0, w1, w2 = fcw_ref[0:1, lo:hi], fcw_ref[1:2, lo:hi], fcw_ref[2:3, lo:hi]
            gate = w0 * g2 + w1 * g1 + w2 * gps + fcb_ref[:, lo:hi]
            sg = jax.nn.sigmoid(gate)
            dact = _mm_nt(dx2b, wd_ref[s])
            dup_ref[:, lo:hi] = (dact * (gate * sg)).astype(BF)
            dgate = (dact * up_ref[:, lo:hi]) * (sg * (1.0 + gate * (1.0 - sg)))
            dfcb_ref[:, lo:hi] += _colsum(dgate)
            dfcw_ref[0:1, lo:hi] += _colsum(dgate * g2)
            dfcw_ref[1:2, lo:hi] += _colsum(dgate * g1)
            dfcw_ref[2:3, lo:hi] += _colsum(dgate * gps)
            dbuf[0:tm, lo:hi] = dgate
            dgp = w2 * dgate + w1 * dbuf[1:1 + tm, lo:hi] + w0 * dbuf[2:2 + tm, lo:hi]
            dgp_ref[:, lo:hi] = dgp.astype(BF)
        dbuf[tm:tm + 8, :] = dbuf[0:8, :]

    return pl.pallas_call(
        body, name="bwd_ffn_a", grid=(nt,),
        in_specs=[_rows(tm, D_MODEL, nt), _rows(tm, D_FF_PAD, nt), _halo(tm, D_FF_PAD, nt), _rows(tm, D_FF_PAD, nt),
                  _const((N_CHIPS, D_FF_SLAB, D_MODEL)), _const((3, D_FF_PAD)), _const((1, D_FF_PAD))],
        out_specs=[_rows(tm, D_FF_PAD, nt), _rows(tm, D_FF_PAD, nt),
                   pl.BlockSpec((3, D_FF_PAD), lambda i: (0, 0)), pl.BlockSpec((1, D_FF_PAD), lambda i: (0, 0))],
        out_shape=[jax.ShapeDtypeStruct((t, D_FF_PAD), BF), jax.ShapeDtypeStruct((t, D_FF_PAD), BF),
                   jax.ShapeDtypeStruct((3, D_FF_PAD), F32), jax.ShapeDtypeStruct((1, D_FF_PAD), F32)],
        scratch_shapes=[pltpu.VMEM((tm + 8, D_FF_PAD), F32), pltpu.VMEM((tm + 8, D_FF_PAD), F32)],
        compiler_params=_params(56),
    )(dx2, gp, gp, up, wd4, fcw, fcb)


def _bwd_ffn_b(dgp, dup, dx2, x1, ya, wg4, wu4, wout, g_ffn, g_oa, tm):
    t = dx2.shape[0]
    nt = t // tm

    def body(dgp_ref, dup_ref, dx2_ref, x1_ref, ya_ref, wg_ref, wu_ref, wout_ref, gffn_ref, goa_ref,
             dx1_ref, dycn_ref, dya_ref, h2_ref, dgffn_ref, dgoa_ref):
        @pl.when(pl.program_id(0) == 0)
        def _():
            dgffn_ref[...] = jnp.zeros_like(dgffn_ref)
            dgoa_ref[...] = jnp.zeros_like(dgoa_ref)

        dh2 = jnp.zeros((tm, D_MODEL), F32)
        for s in range(N_CHIPS):
            lo, hi = s * D_FF_SLAB, (s + 1) * D_FF_SLAB
            dh2 = dh2 + _mm_nt(dgp_ref[:, lo:hi], wg_ref[s]) + _mm_nt(dup_ref[:, lo:hi], wu_ref[s])
        x1 = x1_ref[...]
        r2 = _rstd(x1)
        xh = x1 * r2
        h2_ref[...] = (xh * gffn_ref[...]).astype(BF)
        dgffn_ref[...] += _colsum(dh2 * xh)
        dx1 = dx2_ref[...] + _norm_bwd(dh2, xh, r2, gffn_ref[...])
        dx1_ref[...] = dx1
        dy = _mm_nt(dx1.astype(BF), wout_ref[...])
        dycn_ref[...] = dy[:, 0:CONV_W]
        dyan = dy[:, CONV_W:D_MODEL]
        yat = ya_ref[...]
        ra = _rstd(yat)
        yah = yat * ra
        dgoa_ref[...] += _colsum(dyan * yah)
        dya_ref[...] = _norm_bwd(dyan, yah, ra, goa_ref[...])

    return pl.pallas_call(
        body, name="bwd_ffn_b", grid=(nt,),
        in_specs=[_rows(tm, D_FF_PAD), _rows(tm, D_FF_PAD), _rows(tm, D_MODEL), _rows(tm, D_MODEL),
                  _rows(tm, ATTN_W), _const((N_CHIPS, D_MODEL, D_FF_SLAB)), _const((N_CHIPS, D_MODEL, D_FF_SLAB)),
                  _const((D_MODEL, D_MODEL)), _const((1, D_MODEL)), _const((1, ATTN_W))],
        out_specs=[_rows(tm, D_MODEL), _rows(tm, CONV_W), _rows(tm, ATTN_W), _rows(tm, D_MODEL),
                   pl.BlockSpec((1, D_MODEL), lambda i: (0, 0)), pl.BlockSpec((1, ATTN_W), lambda i: (0, 0))],
        out_shape=[jax.ShapeDtypeStruct((t, D_MODEL), F32), jax.ShapeDtypeStruct((t, CONV_W), F32),
                   jax.ShapeDtypeStruct((t, ATTN_W), F32), jax.ShapeDtypeStruct((t, D_MODEL), BF),
                   jax.ShapeDtypeStruct((1, D_MODEL), F32), jax.ShapeDtypeStruct((1, ATTN_W), F32)],
        compiler_params=_params(48),
    )(dgp, dup, dx2, x1, ya, wg4, wu4, wout, g_ffn, g_oa)


def _bwd_mix(x, dx1, zbcx, qkv, dycn, dq, dk, dv, win4, conv_w, conv_b, g_oc, g_mix, gm, gq8, gk8, tm):
    t = x.shape[0]
    nt = t // tm

    def body(x_ref, dx1_ref, z_ref, zh_ref, qkv_ref, dycn_ref, dq_ref, dk_ref, dv_ref, w_ref, cw_ref, cb_ref,
             goc_ref, g_ref, gm_ref, gq_ref, gk_ref,
             gx_ref, h1_ref, dz_ref, dcw_ref, dcb_ref, dgoc_ref, dg_ref, dgq_ref, dgk_ref, ubuf, dbuf):
        i = pl.program_id(0)

        @pl.when(i == 0)
        def _():
            dbuf[tm:tm + 8, :] = jnp.zeros((8, CONV_W), F32)
            dcw_ref[...] = jnp.zeros_like(dcw_ref)
            dcb_ref[...] = jnp.zeros_like(dcb_ref)
            dgoc_ref[...] = jnp.zeros_like(dgoc_ref)
            dg_ref[...] = jnp.zeros_like(dg_ref)
            dgq_ref[...] = jnp.zeros_like(dgq_ref)
            dgk_ref[...] = jnp.zeros_like(dgk_ref)

        not_first_tile = i < nt - 1
        zb = z_ref[:, 0:512]
        zc = z_ref[:, 512:1024]
        zx = z_ref[:, 1024:1536]
        u = zc * zx
        ubuf[0:8, :] = jnp.where(not_first_tile, zh_ref[:, 512:1024] * zh_ref[:, 1024:1536], 0.0)
        ubuf[8:8 + tm, :] = u
        u1 = ubuf[7:7 + tm, :]
        u2 = ubuf[6:6 + tm, :]
        w0, w1, w2 = cw_ref[0:1, :], cw_ref[1:2, :], cw_ref[2:3, :]
        cv = w0 * u2 + w1 * u1 + w2 * u + cb_ref[...]
        yc = zb * cv
        rc = _rstd(yc)
        ych = yc * rc
        dycn = dycn_ref[...]
        dgoc_ref[...] += _colsum(dycn * ych)
        dyc = _norm_bwd(dycn, ych, rc, goc_ref[...])
        dcv = dyc * zb
        dcb_ref[...] += _colsum(dcv)
        dcw_ref[0:1, :] += _colsum(dcv * u2)
        dcw_ref[1:2, :] += _colsum(dcv * u1)
        dcw_ref[2:3, :] += _colsum(dcv * u)
        dbuf[0:tm, :] = dcv
        du = w2 * dcv + w1 * dbuf[1:1 + tm, :] + w0 * dbuf[2:2 + tm, :]
        dbuf[tm:tm + 8, :] = dbuf[0:8, :]
        dz_ref[:, 0:512] = (dyc * cv).astype(BF)
        dz_ref[:, 512:1024] = (du * zx).astype(BF)
        dz_ref[:, 1024:1536] = (du * zc).astype(BF)
        for z0, d_ref, gg_ref, acc_ref, sc in ((0, dq_ref, gq_ref, dgq_ref, HEAD_DIM ** -0.5),
                                               (512, dk_ref, gk_ref, dgk_ref, 1.0)):
            z = qkv_ref[:, z0:z0 + 512]
            rr = lax.rsqrt(_head_mean(z * z, gm_ref) + EPS)
            zh = z * rr
            dn = d_ref[...] * sc
            acc_ref[...] += _colsum(dn * zh)
            dzh = dn * gg_ref[...]
            dz_ref[:, 1536 + z0:1536 + z0 + 512] = (rr * (dzh - zh * _head_mean(dzh * zh, gm_ref))).astype(BF)
        dz_ref[:, 2560:3072] = dv_ref[...].astype(BF)
        dh1 = jnp.zeros((tm, D_MODEL), F32)
        for s in range(N_CHIPS):
            dh1 = dh1 + _mm_nt(dz_ref[:, s * IN_SLAB:(s + 1) * IN_SLAB], w_ref[s])
        xt = x_ref[...]
        r1 = _rstd(xt)
        xh = xt * r1
        h1_ref[...] = (xh * g_ref[...]).astype(BF)
        dg_ref[...] += _colsum(dh1 * xh)
        gx_ref[...] = dx1_ref[...] + _norm_bwd(dh1, xh, r1, g_ref[...])

    def acc(width, rows=1):
        return pl.BlockSpec((rows, width), lambda i: (0, 0))

    return pl.pallas_call(
        body, name="bwd_mix", grid=(nt,),
        in_specs=[_rows(tm, D_MODEL, nt), _rows(tm, D_MODEL, nt), _rows(tm, 1536, nt), _halo(tm, 1536, nt),
                  _rows(tm, 1536, nt), _rows(tm, CONV_W, nt), _rows(tm, ATTN_W, nt), _rows(tm, ATTN_W, nt),
                  _rows(tm, ATTN_W, nt), _const((N_CHIPS, D_MODEL, IN_SLAB)),
                  _const((3, CONV_W)), _const((1, CONV_W)), _const((1, CONV_W)), _const((1, D_MODEL)),
                  _const((ATTN_W, ATTN_W)), _const((1, ATTN_W)), _const((1, ATTN_W))],
        out_specs=[_rows(tm, D_MODEL, nt), _rows(tm, D_MODEL, nt), _rows(tm, 3072, nt),
                   acc(CONV_W, 3), acc(CONV_W), acc(CONV_W), acc(D_MODEL), acc(ATTN_W), acc(ATTN_W)],
        out_shape=[jax.ShapeDtypeStruct((t, D_MODEL), F32), jax.ShapeDtypeStruct((t, D_MODEL), BF),
                   jax.ShapeDtypeStruct((t, 3072), BF), jax.ShapeDtypeStruct((3, CONV_W), F32),
                   jax.ShapeDtypeStruct((1, CONV_W), F32), jax.ShapeDtypeStruct((1, CONV_W), F32),
                   jax.ShapeDtypeStruct((1, D_MODEL), F32), jax.ShapeDtypeStruct((1, ATTN_W), F32),
                   jax.ShapeDtypeStruct((1, ATTN_W), F32)],
        scratch_shapes=[pltpu.VMEM((tm + 8, CONV_W), F32), pltpu.VMEM((tm + 8, CONV_W), F32)],
        compiler_params=_params(56),
    )(x, dx1, zbcx, zbcx, qkv, dycn, dq, dk, dv, win4, conv_w, conv_b, g_oc, g_mix, gm, gq8, gk8)


def _wgrad(a, b, tn, tt, name):
    t, k = a.shape
    n = b.shape[1]

    def body(a_ref, b_ref, o_ref):
        @pl.when(pl.program_id(1) == 0)
        def _():
            o_ref[...] = jnp.zeros_like(o_ref)

        o_ref[...] += _mm_tn(a_ref[...].astype(BF), b_ref[...].astype(BF))

    return pl.pallas_call(
        body, name=name, grid=(n // tn, t // tt),
        in_specs=[pl.BlockSpec((tt, k), lambda j, i: (i, 0)), pl.BlockSpec((tt, tn), lambda j, i: (i, j))],
        out_specs=pl.BlockSpec((k, tn), lambda j, i: (0, j)),
        out_shape=jax.ShapeDtypeStruct((k, n), F32),
        compiler_params=_params(48, 2),
    )(a, b)


def _place():
    x, y, c = lax.axis_index("x"), lax.axis_index("y"), lax.axis_index("c")
    return x, y, c


def _chip_peer(x, y, k):
    return x ^ (k >> 1), y ^ (k & 1)


def _gather_weights(shards, pack):
    nw = len(shards)

    def body(*refs):
        ins = refs[:nw]
        pack_ref = refs[nw]
        outs = refs[nw + 1:2 * nw + 1]
        pack_out = refs[2 * nw + 1]
        send_sems, recv_sems, local_sems = refs[2 * nw + 2:]
        x, y, c = _place()
        me = 2 * x + y
        local, remote = [], []

        def sem(w, j):
            return w * 6 + j

        def push(src, dst, w, j, to):
            return pltpu.make_async_remote_copy(src_ref=src, dst_ref=dst, send_sem=send_sems.at[sem(w, j)],
                                                recv_sem=recv_sems.at[sem(w, j)], device_id=to, device_id_type=MESH)

        def half_rows(w, h):
            half = ins[w].shape[0] // 2
            return pl.ds(pl.multiple_of(h * half, 16), half)

        for w in range(nw):
            local.append(pltpu.make_async_copy(ins[w], outs[w].at[me], local_sems.at[w]))
            for k in (1, 2, 3):
                px, py = _chip_peer(x, y, k)
                mine = half_rows(w, c)
                remote.append(push(ins[w].at[mine], outs[w].at[me, mine], w, k - 1, (px, py, c)))
        local.append(pltpu.make_async_copy(pack_ref, pack_out.at[me], local_sems.at[nw]))
        for k in (1, 2, 3):
            px, py = _chip_peer(x, y, k)
            remote.append(push(pack_ref, pack_out.at[me], nw, k - 1, (px, py, c)))
        for cp in local + remote:
            cp.start()
        for w in range(nw):
            for k in (1, 2, 3):
                landed = outs[w].at[me ^ k, half_rows(w, c)]
                push(landed, landed, w, k - 1, (x, y, c)).wait_recv()
                fw = push(landed, landed, w, 2 + k, (x, y, 1 - c))
                fw.start()
                remote.append(fw)
        for k in (1, 2, 3):
            landed = pack_out.at[me ^ k]
            push(landed, landed, nw, k - 1, (x, y, c)).wait_recv()
        for w in range(nw):
            for k in (1, 2, 3):
                landed = outs[w].at[me ^ k, half_rows(w, 1 - c)]
                push(landed, landed, w, 2 + k, (x, y, c)).wait_recv()
        for cp in remote:
            cp.wait_send()
        for cp in local:
            cp.wait()

    any_spec = pl.BlockSpec(memory_space=pl.ANY)
    out_shape = [jax.ShapeDtypeStruct((N_CHIPS,) + s.shape, s.dtype) for s in shards]
    out_shape.append(jax.ShapeDtypeStruct((N_CHIPS,) + pack.shape, pack.dtype))
    return pl.pallas_call(
        body, name="gather_weights",
        in_specs=[any_spec] * (nw + 1), out_specs=[any_spec] * (nw + 1), out_shape=out_shape,
        scratch_shapes=[pltpu.SemaphoreType.DMA(((nw + 1) * 6,)), pltpu.SemaphoreType.DMA(((nw + 1) * 6,)),
                        pltpu.SemaphoreType.DMA((nw + 1,))],
    )(*shards, pack)


def _adamw(w, g, m, v):
    m = ADAM_B1 * m + (1.0 - ADAM_B1) * g
    v = ADAM_B2 * v + (1.0 - ADAM_B2) * (g * g)
    m_hat = m / (1.0 - ADAM_B1 ** ADAM_STEP)
    v_hat = v / (1.0 - ADAM_B2 ** ADAM_STEP)
    delta = -ADAM_LR * (m_hat / (jnp.sqrt(v_hat) + ADAM_EPS) + ADAM_WD * w)
    return delta, m, v


def _reduce_adamw(grad, w, m, v, col_sharded, name):
    kk, nn = grad.shape
    if col_sharded:
        r, cw = kk // 2, nn // N_CHIPS
    else:
        r, cw = kk // (2 * N_CHIPS), nn
    vr, vc = w.shape
    chunk = 64
    assert r % chunk == 0 and vr % chunk == 0 and vr <= 2 * r and vc <= cw

    def window(s, h):
        if col_sharded:
            return (pl.ds(pl.multiple_of(h * r, 8), r), pl.ds(pl.multiple_of(s * cw, 128), cw))
        return (pl.ds(pl.multiple_of((2 * s + h) * r, 8), r), slice(None))

    def body(g_hbm, w_ref, m_ref, v_ref, go_ref, do_ref, mo_ref, vo_ref,
             own, bufa, stage, bufb, full, lsem, a_send, a_recv, b_send, b_recv, c_send, c_recv):
        x, y, c = _place()
        me = 2 * x + y
        sib = (x, y, 1 - c)
        for s in range(N_CHIPS):
            pltpu.make_async_copy(g_hbm.at[window(s, c)], own.at[s], lsem.at[s]).start()
            pltpu.make_async_remote_copy(
                src_ref=g_hbm.at[window(s, 1 - c)], dst_ref=bufa.at[s], send_sem=a_send.at[s], recv_sem=a_recv.at[s],
                device_id=sib, device_id_type=MESH).start()
        for s in range(N_CHIPS):
            pltpu.make_async_copy(g_hbm.at[window(s, c)], own.at[s], lsem.at[s]).wait()
            pltpu.make_async_remote_copy(
                src_ref=g_hbm.at[window(s, 1 - c)], dst_ref=bufa.at[s], send_sem=a_send.at[s], recv_sem=a_recv.at[s],
                device_id=sib, device_id_type=MESH).wait()

        def add_a(j, carry):
            rows = pl.ds(pl.multiple_of(j * chunk, 8), chunk)
            for s in range(N_CHIPS):
                own[s, rows, :] = own[s, rows, :] + bufa[s, rows, :]
            for k in (1, 2, 3):
                stage[k - 1, rows, :] = own[me ^ k, rows, :].astype(BF)
            return carry

        lax.fori_loop(0, r // chunk, add_a, 0)
        sends = []
        for k in (1, 2, 3):
            px, py = _chip_peer(x, y, k)
            cp = pltpu.make_async_remote_copy(
                src_ref=stage.at[k - 1], dst_ref=bufb.at[k - 1], send_sem=b_send.at[k - 1], recv_sem=b_recv.at[k - 1],
                device_id=(px, py, c), device_id_type=MESH)
            cp.start()
            sends.append(cp)
        for cp in sends:
            cp.wait()
        mine = pl.multiple_of(c * r, 8)

        def add_b(j, carry):
            rows = pl.ds(pl.multiple_of(j * chunk, 8), chunk)
            tot = ((own[me, rows, :] + bufb[0, rows, :].astype(F32))
                   + (bufb[1, rows, :].astype(F32) + bufb[2, rows, :].astype(F32)))
            full[pl.ds(mine + pl.multiple_of(j * chunk, 8), chunk), :] = tot
            return carry

        lax.fori_loop(0, r // chunk, add_b, 0)
        half = full.at[pl.ds(mine, r), :]
        cp = pltpu.make_async_remote_copy(src_ref=half, dst_ref=half, send_sem=c_send, recv_sem=c_recv,
                                          device_id=sib, device_id_type=MESH)
        cp.start()
        cp.wait()

        def update(j, carry):
            rows = pl.ds(pl.multiple_of(j * chunk, 8), chunk)
            g = full[rows, 0:vc]
            delta, mn, vn = _adamw(w_ref[rows, :], g, m_ref[rows, :], v_ref[rows, :])
            go_ref[rows, :] = g
            do_ref[rows, :] = delta
            mo_ref[rows, :] = mn
            vo_ref[rows, :] = vn
            return carry

        lax.fori_loop(0, vr // chunk, update, 0)

    any_spec = pl.BlockSpec(memory_space=pl.ANY)
    vmem = pl.BlockSpec(memory_space=pltpu.VMEM)
    shard = jax.ShapeDtypeStruct((vr, vc), F32)
    return pl.pallas_call(
        body, name=name,
        in_specs=[any_spec, vmem, vmem, vmem], out_specs=[vmem] * 4, out_shape=[shard] * 4,
        scratch_shapes=[pltpu.VMEM((N_CHIPS, r, cw), F32), pltpu.VMEM((N_CHIPS, r, cw), F32),
                        pltpu.VMEM((3, r, cw), BF), pltpu.VMEM((3, r, cw), BF), pltpu.VMEM((2 * r, cw), F32),
                        pltpu.SemaphoreType.DMA((N_CHIPS,)), pltpu.SemaphoreType.DMA((N_CHIPS,)),
                        pltpu.SemaphoreType.DMA((N_CHIPS,)), pltpu.SemaphoreType.DMA((3,)),
                        pltpu.SemaphoreType.DMA((3,)), pltpu.SemaphoreType.DMA, pltpu.SemaphoreType.DMA],
        compiler_params=pltpu.CompilerParams(vmem_limit_bytes=56 * MIB),
    )(grad, w, m, v)


def _allreduce_small(pack):
    rows = pack.shape[0]

    def body(p_ref, o_ref, slots, send_sems, recv_sems):
        x, y, c = _place()
        me = 4 * x + 2 * y + c
        slots[me] = p_ref[...]
        sends = []
        for k in range(1, 8):
            cp = pltpu.make_async_remote_copy(
                src_ref=p_ref, dst_ref=slots.at[me], send_sem=send_sems.at[k - 1], recv_sem=recv_sems.at[k - 1],
                device_id=(x ^ (k >> 2), y ^ ((k >> 1) & 1), c ^ (k & 1)), device_id_type=MESH)
            cp.start()
            sends.append(cp)
        for cp in sends:
            cp.wait()
        tot = slots[0]
        for j in range(1, 8):
            tot = tot + slots[j]
        o_ref[...] = tot

    vmem = pl.BlockSpec(memory_space=pltpu.VMEM)
    return pl.pallas_call(
        body, name="allreduce_small", in_specs=[vmem], out_specs=vmem,
        out_shape=jax.ShapeDtypeStruct(pack.shape, F32),
        scratch_shapes=[pltpu.VMEM((8, rows, D_MODEL), F32), pltpu.SemaphoreType.DMA((7,)),
                        pltpu.SemaphoreType.DMA((7,))],
    )(pack)


def _adamw_small(ws, gs, ms, vs):
    n = len(ws)

    def body(*refs):
        w_refs, g_refs, m_refs, v_refs = refs[0:n], refs[n:2 * n], refs[2 * n:3 * n], refs[3 * n:4 * n]
        d_refs, mo_refs, vo_refs = refs[4 * n:5 * n], refs[5 * n:6 * n], refs[6 * n:7 * n]
        for j in range(n):
            delta, mn, vn = _adamw(w_refs[j][...], g_refs[j][...], m_refs[j][...], v_refs[j][...])
            d_refs[j][...] = delta
            mo_refs[j][...] = mn
            vo_refs[j][...] = vn

    vmem = pl.BlockSpec(memory_space=pltpu.VMEM)
    shapes = [jax.ShapeDtypeStruct(w.shape, F32) for w in ws]
    outs = pl.pallas_call(
        body, name="adamw_small", in_specs=[vmem] * (4 * n), out_specs=[vmem] * (3 * n), out_shape=shapes * 3,
    )(*ws, *gs, *ms, *vs)
    return outs[0:n], outs[n:2 * n], outs[2 * n:3 * n]


def _pad_slab(a, axis):
    pad = [(0, 0)] * a.ndim
    pad[axis] = (0, D_FF_SLAB - D_FF_SHARD)
    return jnp.pad(a, pad)


def _local_step(x, p, target, wts):
    (win4, wout, wg4, wu4, wd4, wpg, wpp4, conv_w, fcw, g_mix, conv_b, gq, gk, g_oc, g_oa, g_ffn, fcb, g_ple) = wts
    gm = jnp.kron(jnp.eye(N_HEADS, dtype=F32), jnp.full((HEAD_DIM, HEAD_DIM), 1.0 / HEAD_DIM, F32)).astype(BF)
    gq8, gk8 = jnp.tile(gq, (1, N_HEADS)), jnp.tile(gk, (1, N_HEADS))
    mb, mbk = _mask_tables()
    zbcx, qkv, ycn, qkn = _fwd_mix(x, g_mix, win4, conv_w, conv_b, g_oc, gm, gq8, gk8, 512)
    ya, lse = _attn_fwd(qkn, qkv, mb)
    x1, gp, up, act, ycat = _fwd_ffn(x, ycn, ya, wout, wg4, wu4, g_oa, g_ffn, fcw, fcb, 256)
    dx2, h3, ds, dpp, dg_ple, loss = _fwd_tail(x1, act, p, target, wd4, wpg, wpp4, g_ple, 256)
    dgp, dup, dfcw, dfcb = _bwd_ffn_a(dx2, gp, up, wd4, fcw, fcb, 256)
    dx1, dycn, dya, h2, dg_ffn, dg_oa = _bwd_ffn_b(dgp, dup, dx2, x1, ya, wg4, wu4, wout, g_ffn, g_oa, 256)
    dq, dk, dv = _attn_bwd(qkn, qkv, ya, lse, dya, mb, mbk)
    grad_x, h1, dz, dcw, dcb, dg_oc, dg_mix, dgq8, dgk8 = _bwd_mix(
        x, dx1, zbcx, qkv, dycn, dq, dk, dv, win4, conv_w, conv_b, g_oc, g_mix, gm, gq8, gk8, 256)
    dgq = dgq8.reshape(N_HEADS, HEAD_DIM).sum(axis=0, keepdims=True)
    dgk = dgk8.reshape(N_HEADS, HEAD_DIM).sum(axis=0, keepdims=True)
    big = dict(
        w_in=_wgrad(h1, dz, 1536, 512, "wgrad_in"),
        w_out=_wgrad(ycat, dx1, 1024, 512, "wgrad_out"),
        w_gate=_wgrad(h2, dgp, 1536, 512, "wgrad_gate"),
        w_up=_wgrad(h2, dup, 1536, 512, "wgrad_up"),
        w_down=_wgrad(act, dx2, 512, 512, "wgrad_down"),
        w_ple_gate=_wgrad(h3, ds, 1024, 512, "wgrad_ple_gate"),
        w_ple_proj=_wgrad(p, dpp, 1024, 512, "wgrad_ple_proj"),
    )
    small = dict(g_mix=dg_mix, conv_w=dcw, conv_b=dcb, q_norm_g=dgq, k_norm_g=dgk, g_out_conv=dg_oc,
                 g_out_attn=dg_oa, g_ffn=dg_ffn, ffn_conv_w=dfcw, ffn_conv_b=dfcb, g_ple=dg_ple)
    return loss[0, 0], grad_x, big, small


_SMALL_ROWS = 24


def _pack_small(s):
    z64 = jnp.zeros((1, 1024 - 512 - 128), F32)
    rows = [s["g_mix"], s["g_ffn"], s["g_ple"],
            jnp.concatenate([s["conv_b"], s["g_out_conv"]], axis=1),
            jnp.concatenate([s["g_out_attn"], s["q_norm_g"], s["k_norm_g"], z64], axis=1),
            jnp.pad(s["conv_w"], ((0, 0), (0, 512))),
            s["ffn_conv_b"].reshape(3, 1024),
            s["ffn_conv_w"].reshape(9, 1024),
            jnp.zeros((_SMALL_ROWS - 20, 1024), F32)]
    return jnp.concatenate(rows, axis=0)


def _unpack_small(t):
    return dict(g_mix=t[0:1], g_ffn=t[1:2], g_ple=t[2:3], conv_b=t[3:4, 0:512], g_out_conv=t[3:4, 512:1024],
                g_out_attn=t[4:5, 0:512], q_norm_g=t[4:5, 512:576], k_norm_g=t[4:5, 576:640],
                conv_w=t[5:8, 0:512], ffn_conv_b=t[8:11].reshape(1, D_FF_PAD), ffn_conv_w=t[11:20].reshape(3, D_FF_PAD))


def _unpad_ff(a):
    r = a.shape[0]
    return a.reshape(r, N_CHIPS, D_FF_SLAB)[:, :, :D_FF_SHARD].reshape(r, N_CHIPS * D_FF_SHARD)


_BIG = ("w_in", "w_out", "w_gate", "w_up", "w_down", "w_ple_gate", "w_ple_proj")
_COL_SHARDED = dict(w_in=True, w_out=False, w_gate=True, w_up=True, w_down=False, w_ple_gate=False, w_ple_proj=True)
_WEIGHTS = ("g_mix", "w_in", "conv_w", "conv_b", "q_norm_g", "k_norm_g", "g_out_conv", "g_out_attn", "w_out",
            "g_ffn", "w_gate", "w_up", "ffn_conv_w", "ffn_conv_b", "w_down", "g_ple", "w_ple_gate", "w_ple_proj")


def kernel(x, p, g_mix, w_in, conv_w, conv_b, q_norm_g, k_norm_g, g_out_conv, g_out_attn, w_out, g_ffn, w_gate, w_up, ffn_conv_w, ffn_conv_b, w_down, g_ple, w_ple_gate, w_ple_proj, loss_target, m_g_mix, m_w_in, m_conv_w, m_conv_b, m_q_norm_g, m_k_norm_g, m_g_out_conv, m_g_out_attn, m_w_out, m_g_ffn, m_w_gate, m_w_up, m_ffn_conv_w, m_ffn_conv_b, m_w_down, m_g_ple, m_w_ple_gate, m_w_ple_proj, v_g_mix, v_w_in, v_conv_w, v_conv_b, v_q_norm_g, v_k_norm_g, v_g_out_conv, v_g_out_attn, v_w_out, v_g_ffn, v_w_gate, v_w_up, v_ffn_conv_w, v_ffn_conv_b, v_w_down, v_g_ple, v_w_ple_gate, v_w_ple_proj):
    w = dict(g_mix=g_mix, w_in=w_in, conv_w=conv_w, conv_b=conv_b, q_norm_g=q_norm_g, k_norm_g=k_norm_g,
             g_out_conv=g_out_conv, g_out_attn=g_out_attn, w_out=w_out, g_ffn=g_ffn, w_gate=w_gate, w_up=w_up,
             ffn_conv_w=ffn_conv_w, ffn_conv_b=ffn_conv_b, w_down=w_down, g_ple=g_ple, w_ple_gate=w_ple_gate,
             w_ple_proj=w_ple_proj)
    m = dict(g_mix=m_g_mix, w_in=m_w_in, conv_w=m_conv_w, conv_b=m_conv_b, q_norm_g=m_q_norm_g, k_norm_g=m_k_norm_g,
             g_out_conv=m_g_out_conv, g_out_attn=m_g_out_attn, w_out=m_w_out, g_ffn=m_g_ffn, w_gate=m_w_gate,
             w_up=m_w_up, ffn_conv_w=m_ffn_conv_w, ffn_conv_b=m_ffn_conv_b, w_down=m_w_down, g_ple=m_g_ple,
             w_ple_gate=m_w_ple_gate, w_ple_proj=m_w_ple_proj)
    v = dict(g_mix=v_g_mix, w_in=v_w_in, conv_w=v_conv_w, conv_b=v_conv_b, q_norm_g=v_q_norm_g, k_norm_g=v_k_norm_g,
             g_out_conv=v_g_out_conv, g_out_attn=v_g_out_attn, w_out=v_w_out, g_ffn=v_g_ffn, w_gate=v_w_gate,
             w_up=v_w_up, ffn_conv_w=v_ffn_conv_w, ffn_conv_b=v_ffn_conv_b, w_down=v_w_down, g_ple=v_g_ple,
             w_ple_gate=v_w_ple_gate, w_ple_proj=v_w_ple_proj)
    mats = [k for k, a in w.items() if a.ndim == 3]
    w = {k: (a[0] if k in mats else a) for k, a in w.items()}
    m = {k: (a[0] if k in mats else a) for k, a in m.items()}
    v = {k: (a[0] if k in mats else a) for k, a in v.items()}
    chip = 2 * lax.axis_index("x") + lax.axis_index("y")

    shards = [w["w_in"].astype(BF), w["w_out"].astype(BF), _pad_slab(w["w_gate"], 1).astype(BF),
              _pad_slab(w["w_up"], 1).astype(BF), _pad_slab(w["w_down"], 0).astype(BF),
              w["w_ple_gate"].astype(BF), w["w_ple_proj"].astype(BF)]
    pack = jnp.pad(jnp.concatenate([w["conv_w"], _pad_slab(w["ffn_conv_w"], 1)], axis=1), ((0, 5), (0, 128)))
    win4, wout4, wg4, wu4, wd4, wpg4, wpp4, pack4 = _gather_weights(shards, pack)
    conv_w_full = pack4[:, 0:3, 0:128].transpose(1, 0, 2).reshape(3, CONV_W)
    fcw_full = pack4[:, 0:3, 128:128 + D_FF_SLAB].transpose(1, 0, 2).reshape(3, D_FF_PAD)
    fcb_pad = _pad_slab(w["ffn_conv_b"].reshape(N_CHIPS, D_FF_SHARD), 1).reshape(1, D_FF_PAD)
    wts = (win4, wout4.reshape(D_MODEL, D_MODEL), wg4, wu4, wd4, wpg4.reshape(D_MODEL, D_MODEL), wpp4,
           conv_w_full, fcw_full, w["g_mix"], w["conv_b"], w["q_norm_g"], w["k_norm_g"], w["g_out_conv"],
           w["g_out_attn"], w["g_ffn"], fcb_pad, w["g_ple"])

    loss, grad_x, big, small = _local_step(x[0], p[0, 0], loss_target[0], wts)
    loss = lax.psum(loss, ("x", "y", "c"))

    grads, deltas, new_m, new_v = {}, {}, {}, {}
    for name in _BIG:
        grads[name], deltas[name], new_m[name], new_v[name] = _reduce_adamw(
            big[name], w[name], m[name], v[name], _COL_SHARDED[name], "reduce_" + name)
    tot = _unpack_small(_allreduce_small(_pack_small(small)))
    tot["conv_w"] = lax.dynamic_slice_in_dim(tot["conv_w"], chip * 128, 128, axis=1)
    tot["ffn_conv_w"] = lax.dynamic_slice_in_dim(tot["ffn_conv_w"], chip * D_FF_SLAB, D_FF_SLAB, axis=1)[:, :D_FF_SHARD]
    tot["ffn_conv_b"] = _unpad_ff(tot["ffn_conv_b"])
    names = [n for n in _WEIGHTS if n not in _BIG]
    d_s, m_s, v_s = _adamw_small([w[n] for n in names], [tot[n] for n in names], [m[n] for n in names],
                                 [v[n] for n in names])
    for j, n in enumerate(names):
        grads[n], deltas[n], new_m[n], new_v[n] = tot[n], d_s[j], m_s[j], v_s[j]

    out = [loss, grad_x[None]]
    for group in (grads, deltas, new_m, new_v):
        out += [group[n][None] if n in mats else group[n] for n in _WEIGHTS]
    return tuple(out)
```

```python
import jax
import jax.numpy as jnp
from jax import lax
from jax.experimental import pallas as pl
from jax.experimental.pallas import tpu as pltpu

D_MODEL = 1024
CONV_W = 512
N_HEADS = 8
HEAD_DIM = 64
ATTN_W = 512
D_FF_SHARD = 704
D_FF_SLAB = 768
D_FF_PAD = 4 * D_FF_SLAB
IN_SLAB = 768
PLE_DIM = 256
N_CHIPS = 4
QBLK = 128
DILATIONS = (1, 4, 16)
EPS = 1e-6
NEG = -1e30
MESH = pl.DeviceIdType.MESH

ADAM_LR = 0.001
ADAM_B1 = 0.9
ADAM_B2 = 0.999
ADAM_EPS = 1e-08
ADAM_WD = 0.01
ADAM_STEP = 10

BF = jnp.bfloat16
F32 = jnp.float32
MIB = 1024 * 1024


def _mm(a, b):
    return jnp.dot(a, b, preferred_element_type=F32)


def _mm_nt(a, b):
    return lax.dot_general(a, b, (((1,), (1,)), ((), ())), preferred_element_type=F32)


def _mm_tn(a, b):
    return lax.dot_general(a, b, (((0,), (0,)), ((), ())), preferred_element_type=F32)


def _rstd(a):
    return lax.rsqrt(jnp.mean(a * a, axis=-1, keepdims=True) + EPS)


def _norm_bwd(dy, xh, r, g):
    dxh = dy * g
    return r * (dxh - xh * jnp.mean(dxh * xh, axis=-1, keepdims=True))


def _colsum(a):
    return jnp.sum(a, axis=0, keepdims=True)


def _head_mean(a, gm_ref):
    hi = a.astype(BF)
    lo = (a - hi.astype(F32)).astype(BF)
    return _mm(hi, gm_ref[...]) + _mm(lo, gm_ref[...])


def _shift_down(buf, k, tm):
    return pltpu.roll(buf, k, axis=0)[8:8 + tm]


def _shift_up(buf, k, tm):
    return pltpu.roll(buf, tm + 8 - k, axis=0)[0:tm]


def _params(vmem_mib, n_grid=1):
    return pltpu.CompilerParams(dimension_semantics=("arbitrary",) * n_grid, vmem_limit_bytes=vmem_mib * MIB)


def _const(shape):
    n = len(shape)
    return pl.BlockSpec(shape, lambda *_: (0,) * n, pipeline_mode=pl.Buffered(1))


def _rows(tm, width, rev_of=None):
    if rev_of is None:
        return pl.BlockSpec((tm, width), lambda i: (i, 0))
    return pl.BlockSpec((tm, width), lambda i: (rev_of - 1 - i, 0))


def _halo(tm, width, nt):
    return pl.BlockSpec((8, width), lambda i: (jnp.maximum((nt - 1 - i) * (tm // 8) - 1, 0), 0))


def _fwd_mix(x, g_mix, win4, conv_w, conv_b, g_oc, gm, gq8, gk8, tm):
    t = x.shape[0]
    nt = t // tm

    def body(x_ref, g_ref, w_ref, cw_ref, cb_ref, goc_ref, gm_ref, gq_ref, gk_ref,
             zbcx_ref, qkv_ref, ycn_ref, qkn_ref, ubuf):
        @pl.when(pl.program_id(0) == 0)
        def _():
            ubuf[0:8, :] = jnp.zeros((8, CONV_W), F32)

        xt = x_ref[...]
        h = ((xt * _rstd(xt)) * g_ref[...]).astype(BF)
        zbcx_ref[:, 0:IN_SLAB] = _mm(h, w_ref[0])
        zbcx_ref[:, IN_SLAB:2 * IN_SLAB] = _mm(h, w_ref[1])
        qkv_ref[:, 0:IN_SLAB] = _mm(h, w_ref[2])
        qkv_ref[:, IN_SLAB:2 * IN_SLAB] = _mm(h, w_ref[3])
        u = zbcx_ref[:, 512:1024] * zbcx_ref[:, 1024:1536]
        ubuf[8:8 + tm, :] = u
        ub = ubuf[...]
        cv = (cw_ref[0:1, :] * _shift_down(ub, 2, tm) + cw_ref[1:2, :] * _shift_down(ub, 1, tm)
              + cw_ref[2:3, :] * u + cb_ref[...])
        ubuf[0:8, :] = ubuf[tm:tm + 8, :]
        yc = zbcx_ref[:, 0:512] * cv
        ycn_ref[...] = ((yc * _rstd(yc)) * goc_ref[...]).astype(BF)
        zq = qkv_ref[:, 0:512]
        zk = qkv_ref[:, 512:1024]
        rq = lax.rsqrt(_head_mean(zq * zq, gm_ref) + EPS)
        rk = lax.rsqrt(_head_mean(zk * zk, gm_ref) + EPS)
        qkn_ref[:, 0:512] = ((zq * rq) * gq_ref[...]) * (HEAD_DIM ** -0.5)
        qkn_ref[:, 512:1024] = (zk * rk) * gk_ref[...]

    return pl.pallas_call(
        body, name="fwd_mix", grid=(nt,),
        in_specs=[_rows(tm, D_MODEL), _const((1, D_MODEL)), _const((N_CHIPS, D_MODEL, IN_SLAB)),
                  _const((3, CONV_W)), _const((1, CONV_W)), _const((1, CONV_W)), _const((ATTN_W, ATTN_W)),
                  _const((1, ATTN_W)), _const((1, ATTN_W))],
        out_specs=[_rows(tm, 1536), _rows(tm, 1536), _rows(tm, CONV_W), _rows(tm, 1024)],
        out_shape=[jax.ShapeDtypeStruct((t, 1536), F32), jax.ShapeDtypeStruct((t, 1536), F32),
                   jax.ShapeDtypeStruct((t, CONV_W), BF), jax.ShapeDtypeStruct((t, 1024), F32)],
        scratch_shapes=[pltpu.VMEM((tm + 8, CONV_W), F32)],
        compiler_params=_params(48),
    )(x, g_mix, win4, conv_w, conv_b, g_oc, gm, gq8, gk8)


def _alibi(h):
    return 2.0 ** (-(h + 1))


CHUNK = 2048


def _mask_tables():
    slopes = jnp.asarray([_alibi(h) for h in range(N_HEADS)], F32)[:, None, None]

    def table(step):
        valid = (step >= 0) & (step <= QBLK)
        return jnp.stack([jnp.where(valid[None], -slopes * (step * d)[None].astype(F32), NEG) for d in DILATIONS])

    i = jnp.arange(QBLK)[:, None]
    j2 = jnp.arange(2 * QBLK)[None, :]
    r2 = jnp.arange(2 * QBLK)[:, None]
    j = jnp.arange(QBLK)[None, :]
    fwd, bwd = table(i + QBLK - j2), table(r2 - j)
    return fwd.reshape(3, N_HEADS // 2, 2 * QBLK, 2 * QBLK), bwd.reshape(3, N_HEADS // 2, 4 * QBLK, QBLK)


def _attn_fwd(qkn, qkv, mb):
    t = qkn.shape[0]
    nc = t // CHUNK

    def body(qc_ref, kp_ref, kc_ref, vp_ref, vc_ref, mb_ref, o_ref, l_ref, ob0, ob1, ob2, lb0, lb1, lb2):
        first = pl.program_id(1) == 0
        lane = lax.broadcasted_iota(jnp.int32, (QBLK, 128), 1)
        lo_half = lane < HEAD_DIM
        kj = lax.broadcasted_iota(jnp.int32, (2 * QBLK, 2 * QBLK), 1)
        no_prev = first & (kj < QBLK)
        obs, lbs = (ob0, ob1, ob2), (lb0, lb1, lb2)

        def by_head(a):
            return jnp.where(lo_half, a, 0.0).astype(BF), jnp.where(lo_half, 0.0, a).astype(BF)

        for di, d in enumerate(DILATIONS):
            span = d * QBLK
            for r in range(d):
                tail = pl.ds(CHUNK - span + r, QBLK, stride=d)
                k_prev = kp_ref[tail, :].astype(BF)
                v_prev = by_head(vp_ref[tail, :])
                for b in range(CHUNK // span):
                    rows = pl.ds(r + span * b, QBLK, stride=d)
                    q0, q1 = by_head(qc_ref[rows, :])
                    k_cur = kc_ref[rows, :].astype(BF)
                    v_cur = by_head(vc_ref[rows, :])
                    s = _mm_nt(jnp.concatenate([q0, q1], axis=0), jnp.concatenate([k_prev, k_cur], axis=0))
                    s = s + mb_ref[di, 0]
                    if b == 0:
                        s = jnp.where(no_prev, NEG, s)
                    m = jnp.max(s, axis=-1, keepdims=True)
                    e = jnp.exp(s - m)
                    den = jnp.sum(e, axis=-1, keepdims=True)
                    eb = e.astype(BF)
                    o = _mm(jnp.concatenate([eb[0:QBLK], eb[QBLK:2 * QBLK]], axis=1),
                            jnp.concatenate([v_prev[0], v_cur[0], v_prev[1], v_cur[1]], axis=0))
                    inv = 1.0 / den
                    lse = m + jnp.log(den)
                    obs[di][rows, :] = o * jnp.where(lo_half, inv[0:QBLK], inv[QBLK:2 * QBLK])
                    lbs[di][rows, :] = jnp.where(lo_half, lse[0:QBLK], lse[QBLK:2 * QBLK])
                    k_prev, v_prev = k_cur, v_cur
        for c0 in range(0, CHUNK, 256):
            rs = slice(c0, c0 + 256)
            l0, l1, l2 = lb0[rs, :], lb1[rs, :], lb2[rs, :]
            mx = jnp.maximum(jnp.maximum(l0, l1), l2)
            w0, w1, w2 = jnp.exp(l0 - mx), jnp.exp(l1 - mx), jnp.exp(l2 - mx)
            tot = w0 + w1 + w2
            o_ref[rs, :] = (ob0[rs, :] * w0 + ob1[rs, :] * w1 + ob2[rs, :] * w2) / tot
            l_ref[rs, :] = mx + jnp.log(tot)

    def cur(col):
        return pl.BlockSpec((CHUNK, 128), lambda hp, n: (n, col + hp))

    def prv(col):
        return pl.BlockSpec((CHUNK, 128), lambda hp, n: (jnp.maximum(n - 1, 0), col + hp))

    out = pl.BlockSpec((CHUNK, 128), lambda hp, n: (n, hp))
    return pl.pallas_call(
        body, name="attn_fwd", grid=(N_HEADS // 2, nc),
        in_specs=[cur(0), prv(4), cur(4), prv(8), cur(8),
                  pl.BlockSpec((3, 1, 2 * QBLK, 2 * QBLK), lambda hp, n: (0, hp, 0, 0))],
        out_specs=[out, out],
        out_shape=[jax.ShapeDtypeStruct((t, ATTN_W), F32)] * 2,
        scratch_shapes=[pltpu.VMEM((CHUNK, 128), F32)] * 6,
        compiler_params=_params(48, 2),
    )(qkn, qkn, qkn, qkv, qkv, mb)


def _attn_bwd(qkn, qkv, o, lse, do, mb, mbk):
    t = qkn.shape[0]
    nc = t // CHUNK

    def body(qc_ref, qn_ref, kp_ref, kc_ref, vp_ref, vc_ref, oc_ref, on_ref, lc_ref, ln_ref, dc_ref, dn_ref,
             mb_ref, mbk_ref, dq_ref, dk_ref, dv_ref):
        first = pl.program_id(1) == 0
        last = pl.program_id(1) == nc - 1
        lane = lax.broadcasted_iota(jnp.int32, (QBLK, 128), 1)
        lo_half = lane < HEAD_DIM
        kj = lax.broadcasted_iota(jnp.int32, (2 * QBLK, 2 * QBLK), 1)
        no_prev = first & (kj < QBLK)
        ri = lax.broadcasted_iota(jnp.int32, (4 * QBLK, QBLK), 0)
        no_next = last & ((ri & (2 * QBLK - 1)) >= QBLK)

        def by_head(a):
            return jnp.where(lo_half, a, 0.0).astype(BF), jnp.where(lo_half, 0.0, a).astype(BF)

        def query_side(q_ref, d_ref, o_ref_, l_ref_, rows):
            dvals = d_ref[rows, :]
            dd = dvals * o_ref_[rows, :]
            lv = l_ref_[rows, :]
            d0 = jnp.sum(jnp.where(lo_half, dd, 0.0), axis=-1, keepdims=True)
            d1 = jnp.sum(jnp.where(lo_half, 0.0, dd), axis=-1, keepdims=True)
            l0 = jnp.max(jnp.where(lo_half, lv, NEG), axis=-1, keepdims=True)
            l1 = jnp.max(jnp.where(lo_half, NEG, lv), axis=-1, keepdims=True)
            return by_head(q_ref[rows, :]), by_head(dvals), (l0, l1), (d0, d1)

        for di, d in enumerate(DILATIONS):
            span = d * QBLK
            nbk = CHUNK // span
            for r in range(d):
                tail = pl.ds(CHUNK - span + r, QBLK, stride=d)
                k_prev = kp_ref[tail, :]
                kb_prev, km_prev = k_prev.astype(BF), by_head(k_prev)
                vb_prev = vp_ref[tail, :].astype(BF)
                rows0 = pl.ds(r, QBLK, stride=d)
                cur = query_side(qc_ref, dc_ref, oc_ref, lc_ref, rows0)
                for b in range(nbk):
                    rows = pl.ds(r + span * b, QBLK, stride=d)
                    if b == nbk - 1:
                        nxt = query_side(qn_ref, dn_ref, on_ref, ln_ref, rows0)
                    else:
                        nxt = query_side(qc_ref, dc_ref, oc_ref, lc_ref, pl.ds(r + span * (b + 1), QBLK, stride=d))
                    (q0, q1), (do0, do1), (l0, l1), (d0, d1) = cur
                    (qx0, qx1), (dox0, dox1), (lx0, lx1), (dx0, dx1) = nxt
                    k_cur = kc_ref[rows, :]
                    kb_cur, km_cur = k_cur.astype(BF), by_head(k_cur)
                    vb_cur = vc_ref[rows, :].astype(BF)
                    k2 = jnp.concatenate([kb_prev, kb_cur], axis=0)
                    v2 = jnp.concatenate([vb_prev, vb_cur], axis=0)
                    s = _mm_nt(jnp.concatenate([q0, q1], axis=0), k2) + mb_ref[di, 0]
                    if b == 0:
                        s = jnp.where(no_prev, NEG, s)
                    p = jnp.exp(s - jnp.concatenate([l0, l1], axis=0))
                    dp = _mm_nt(jnp.concatenate([do0, do1], axis=0), v2)
                    ds = (p * (dp - jnp.concatenate([d0, d1], axis=0))).astype(BF)
                    dq2 = _mm(jnp.concatenate([ds[0:QBLK], ds[QBLK:2 * QBLK]], axis=1),
                              jnp.concatenate([km_prev[0], km_cur[0], km_prev[1], km_cur[1]], axis=0))
                    if di == 0:
                        dq_ref[rows, :] = dq2
                    else:
                        dq_ref[rows, :] = dq_ref[rows, :] + dq2
                    q4 = jnp.concatenate([q0, qx0, q1, qx1], axis=0)
                    do4 = jnp.concatenate([do0, dox0, do1, dox1], axis=0)
                    s = _mm_nt(q4, kb_cur) + mbk_ref[di, 0]
                    if b == nbk - 1:
                        s = jnp.where(no_next, NEG, s)
                    p = jnp.exp(s - jnp.concatenate([l0, lx0, l1, lx1], axis=0))
                    dv2 = _mm_tn(p.astype(BF), do4)
                    dp = _mm_nt(do4, vb_cur)
                    ds = (p * (dp - jnp.concatenate([d0, dx0, d1, dx1], axis=0))).astype(BF)
                    dk2 = _mm_tn(ds, q4)
                    if di == 0:
                        dk_ref[rows, :] = dk2
                        dv_ref[rows, :] = dv2
                    else:
                        dk_ref[rows, :] = dk_ref[rows, :] + dk2
                        dv_ref[rows, :] = dv_ref[rows, :] + dv2
                    cur = nxt
                    kb_prev, km_prev, vb_prev = kb_cur, km_cur, vb_cur

    def at(shift, col):
        return pl.BlockSpec((CHUNK, 128), lambda hp, n: (jnp.clip(n + shift, 0, nc - 1), col + hp))

    out = pl.BlockSpec((CHUNK, 128), lambda hp, n: (n, hp))
    return pl.pallas_call(
        body, name="attn_bwd", grid=(N_HEADS // 2, nc),
        in_specs=[at(0, 0), at(1, 0), at(-1, 4), at(0, 4), at(-1, 8), at(0, 8),
                  at(0, 0), at(1, 0), at(0, 0), at(1, 0), at(0, 0), at(1, 0),
                  pl.BlockSpec((3, 1, 2 * QBLK, 2 * QBLK), lambda hp, n: (0, hp, 0, 0)),
                  pl.BlockSpec((3, 1, 4 * QBLK, QBLK), lambda hp, n: (0, hp, 0, 0))],
        out_specs=[out, out, out],
        out_shape=[jax.ShapeDtypeStruct((t, ATTN_W), F32)] * 3,
        compiler_params=_params(56, 2),
    )(qkn, qkn, qkn, qkn, qkv, qkv, o, o, lse, lse, do, do, mb, mbk)


def _fwd_ffn(x, ycn, ya, wout, wg4, wu4, g_oa, g_ffn, fcw, fcb, tm):
    t = x.shape[0]
    nt = t // tm

    def body(x_ref, ycn_ref, ya_ref, wout_ref, wg_ref, wu_ref, goa_ref, gffn_ref, fcw_ref, fcb_ref,
             x1_ref, gp_ref, up_ref, act_ref, ycat_ref, cbuf):
        @pl.when(pl.program_id(0) == 0)
        def _():
            cbuf[0:8, :] = jnp.zeros((8, D_FF_PAD), F32)

        yat = ya_ref[...]
        yan = ((yat * _rstd(yat)) * goa_ref[...]).astype(BF)
        ycn = ycn_ref[...]
        ycat_ref[:, 0:CONV_W] = ycn
        ycat_ref[:, CONV_W:D_MODEL] = yan
        x1 = x_ref[...] + _mm(ycn, wout_ref[0:CONV_W, :]) + _mm(yan, wout_ref[CONV_W:D_MODEL, :])
        x1_ref[...] = x1
        h2 = ((x1 * _rstd(x1)) * gffn_ref[...]).astype(BF)
        for s in range(N_CHIPS):
            lo, hi = s * D_FF_SLAB, (s + 1) * D_FF_SLAB
            gps = _mm(h2, wg_ref[s])
            ups = _mm(h2, wu_ref[s])
            gp_ref[:, lo:hi] = gps
            up_ref[:, lo:hi] = ups
            cbuf[8:8 + tm, lo:hi] = gps
            cb = cbuf[:, lo:hi]
            gate = (fcw_ref[0:1, lo:hi] * _shift_down(cb, 2, tm) + fcw_ref[1:2, lo:hi] * _shift_down(cb, 1, tm)
                    + fcw_ref[2:3, lo:hi] * gps + fcb_ref[:, lo:hi])
            act_ref[:, lo:hi] = ((gate * jax.nn.sigmoid(gate)) * ups).astype(BF)
        cbuf[0:8, :] = cbuf[tm:tm + 8, :]

    return pl.pallas_call(
        body, name="fwd_ffn", grid=(nt,),
        in_specs=[_rows(tm, D_MODEL), _rows(tm, CONV_W), _rows(tm, ATTN_W), _const((D_MODEL, D_MODEL)),
                  _const((N_CHIPS, D_MODEL, D_FF_SLAB)), _const((N_CHIPS, D_MODEL, D_FF_SLAB)),
                  _const((1, ATTN_W)), _const((1, D_MODEL)), _const((3, D_FF_PAD)), _const((1, D_FF_PAD))],
        out_specs=[_rows(tm, D_MODEL), _rows(tm, D_FF_PAD), _rows(tm, D_FF_PAD), _rows(tm, D_FF_PAD),
                   _rows(tm, D_MODEL)],
        out_shape=[jax.ShapeDtypeStruct((t, D_MODEL), F32), jax.ShapeDtypeStruct((t, D_FF_PAD), F32),
                   jax.ShapeDtypeStruct((t, D_FF_PAD), F32), jax.ShapeDtypeStruct((t, D_FF_PAD), BF),
                   jax.ShapeDtypeStruct((t, D_MODEL), BF)],
        scratch_shapes=[pltpu.VMEM((tm + 8, D_FF_PAD), F32)],
        compiler_params=_params(56),
    )(x, ycn, ya, wout, wg4, wu4, g_oa, g_ffn, fcw, fcb)


def _fwd_tail(x1, act, p, target, wd4, wpg, wpp4, g_ple, tm):
    t = x1.shape[0]
    nt = t // tm

    def body(x1_ref, act_ref, p_ref, tgt_ref, wd_ref, wpg_ref, wpp_ref, g_ref,
             dx2_ref, h3_ref, ds_ref, dpp_ref, dg_ref, loss_ref, lacc):
        i = pl.program_id(0)

        @pl.when(i == 0)
        def _():
            dg_ref[...] = jnp.zeros_like(dg_ref)
            lacc[...] = jnp.zeros_like(lacc)

        x2 = x1_ref[...]
        for s in range(N_CHIPS):
            x2 = x2 + _mm(act_ref[:, s * D_FF_SLAB:(s + 1) * D_FF_SLAB], wd_ref[s])
        r3 = _rstd(x2)
        xh = x2 * r3
        h3 = (xh * g_ref[...]).astype(BF)
        h3_ref[...] = h3
        sg = jax.nn.sigmoid(_mm(h3, wpg_ref[...]))
        pb = p_ref[...].astype(BF)
        pp = jnp.concatenate([_mm(pb, wpp_ref[s]) for s in range(N_CHIPS)], axis=1)
        err = (x2 + sg * pp) - tgt_ref[...]
        lacc[...] += _colsum(err * err)
        dx3 = err * (1.0 / D_MODEL)
        dpp_ref[...] = (dx3 * sg).astype(BF)
        dsb = ((dx3 * pp) * (sg * (1.0 - sg))).astype(BF)
        ds_ref[...] = dsb
        dh3 = _mm_nt(dsb, wpg_ref[...])
        dg_ref[...] += _colsum(dh3 * xh)
        dx2_ref[...] = dx3 + _norm_bwd(dh3, xh, r3, g_ref[...])

        @pl.when(i == nt - 1)
        def _():
            loss_ref[...] = jnp.full((1, 128), jnp.sum(lacc[...]) * (0.5 / D_MODEL), F32)

    return pl.pallas_call(
        body, name="fwd_tail", grid=(nt,),
        in_specs=[_rows(tm, D_MODEL), _rows(tm, D_FF_PAD), _rows(tm, PLE_DIM), _rows(tm, D_MODEL),
                  _const((N_CHIPS, D_FF_SLAB, D_MODEL)), _const((D_MODEL, D_MODEL)),
                  _const((N_CHIPS, PLE_DIM, PLE_DIM)), _const((1, D_MODEL))],
        out_specs=[_rows(tm, D_MODEL), _rows(tm, D_MODEL), _rows(tm, D_MODEL), _rows(tm, D_MODEL),
                   pl.BlockSpec((1, D_MODEL), lambda i: (0, 0)), pl.BlockSpec((1, 128), lambda i: (0, 0))],
        out_shape=[jax.ShapeDtypeStruct((t, D_MODEL), F32), jax.ShapeDtypeStruct((t, D_MODEL), BF),
                   jax.ShapeDtypeStruct((t, D_MODEL), BF), jax.ShapeDtypeStruct((t, D_MODEL), BF),
                   jax.ShapeDtypeStruct((1, D_MODEL), F32), jax.ShapeDtypeStruct((1, 128), F32)],
        scratch_shapes=[pltpu.VMEM((1, D_MODEL), F32)],
        compiler_params=_params(48),
    )(x1, act, p, target, wd4, wpg, wpp4, g_ple)


def _bwd_ffn_a(dx2, gp, up, wd4, fcw, fcb, tm):
    t = dx2.shape[0]
    nt = t // tm

    def body(dx2_ref, gp_ref, gph_ref, up_ref, wd_ref, fcw_ref, fcb_ref,
             dgp_ref, dup_ref, dfcw_ref, dfcb_ref, cbuf, dbuf):
        i = pl.program_id(0)

        @pl.when(i == 0)
        def _():
            dbuf[tm:tm + 8, :] = jnp.zeros((8, D_FF_PAD), F32)
            dfcw_ref[...] = jnp.zeros_like(dfcw_ref)
            dfcb_ref[...] = jnp.zeros_like(dfcb_ref)

        not_first_tile = i < nt - 1
        dx2b = dx2_ref[...].astype(BF)
        for s in range(N_CHIPS):
            lo, hi = s * D_FF_SLAB, (s + 1) * D_FF_SLAB
            gps = gp_ref[:, lo:hi]
            cbuf[0:8, lo:hi] = jnp.where(not_first_tile, gph_ref[:, lo:hi], 0.0)
            cbuf[8:8 + tm, lo:hi] = gps
            cb = cbuf[:, lo:hi]
            g1 = _shift_down(cb, 1, tm)
            g2 = _shift_down(cb, 2, tm)
            w0, w1, w2 = fcw_ref[0:1, lo:hi], fcw_ref[1:2, lo:hi], fcw_ref[2:3, lo:hi]
            gate = w0 * g2 + w1 * g1 + w2 * gps + fcb_ref[:, lo:hi]
            sg = jax.nn.sigmoid(gate)
            dact = _mm_nt(dx2b, wd_ref[s])
            dup_ref[:, lo:hi] = (dact * (gate * sg)).astype(BF)
            dgate = (dact * up_ref[:, lo:hi]) * (sg * (1.0 + gate * (1.0 - sg)))
            dfcb_ref[:, lo:hi] += _colsum(dgate)
            dfcw_ref[0:1, lo:hi] += _colsum(dgate * g2)
            dfcw_ref[1:2, lo:hi] += _colsum(dgate * g1)
            dfcw_ref[2:3, lo:hi] += _colsum(dgate * gps)
            dbuf[0:tm, lo:hi] = dgate
            db = dbuf[:, lo:hi]
            dgp = w2 * dgate + w1 * _shift_up(db, 1, tm) + w0 * _shift_up(db, 2, tm)
            dgp_ref[:, lo:hi] = dgp.astype(BF)
        dbuf[tm:tm + 8, :] = dbuf[0:8, :]

    return pl.pallas_call(
        body, name="bwd_ffn_a", grid=(nt,),
        in_specs=[_rows(tm, D_MODEL, nt), _rows(tm, D_FF_PAD, nt), _halo(tm, D_FF_PAD, nt), _rows(tm, D_FF_PAD, nt),
                  _const((N_CHIPS, D_FF_SLAB, D_MODEL)), _const((3, D_FF_PAD)), _const((1, D_FF_PAD))],
        out_specs=[_rows(tm, D_FF_PAD, nt), _rows(tm, D_FF_PAD, nt),
                   pl.BlockSpec((3, D_FF_PAD), lambda i: (0, 0)), pl.BlockSpec((1, D_FF_PAD), lambda i: (0, 0))],
        out_shape=[jax.ShapeDtypeStruct((t, D_FF_PAD), BF), jax.ShapeDtypeStruct((t, D_FF_PAD), BF),
                   jax.ShapeDtypeStruct((3, D_FF_PAD), F32), jax.ShapeDtypeStruct((1, D_FF_PAD), F32)],
        scratch_shapes=[pltpu.VMEM((tm + 8, D_FF_PAD), F32), pltpu.VMEM((tm + 8, D_FF_PAD), F32)],
        compiler_params=_params(56),
    )(dx2, gp, gp, up, wd4, fcw, fcb)


def _bwd_ffn_b(dgp, dup, dx2, x1, ya, wg4, wu4, wout, g_ffn, g_oa, tm):
    t = dx2.shape[0]
    nt = t // tm

    def body(dgp_ref, dup_ref, dx2_ref, x1_ref, ya_ref, wg_ref, wu_ref, wout_ref, gffn_ref, goa_ref,
             dx1_ref, dycn_ref, dya_ref, h2_ref, dgffn_ref, dgoa_ref):
        @pl.when(pl.program_id(0) == 0)
        def _():
            dgffn_ref[...] = jnp.zeros_like(dgffn_ref)
            dgoa_ref[...] = jnp.zeros_like(dgoa_ref)

        dh2 = jnp.zeros((tm, D_MODEL), F32)
        for s in range(N_CHIPS):
            lo, hi = s * D_FF_SLAB, (s + 1) * D_FF_SLAB
            dh2 = dh2 + _mm_nt(dgp_ref[:, lo:hi], wg_ref[s]) + _mm_nt(dup_ref[:, lo:hi], wu_ref[s])
        x1 = x1_ref[...]
        r2 = _rstd(x1)
        xh = x1 * r2
        h2_ref[...] = (xh * gffn_ref[...]).astype(BF)
        dgffn_ref[...] += _colsum(dh2 * xh)
        dx1 = dx2_ref[...] + _norm_bwd(dh2, xh, r2, gffn_ref[...])
        dx1_ref[...] = dx1
        dy = _mm_nt(dx1.astype(BF), wout_ref[...])
        dycn_ref[...] = dy[:, 0:CONV_W]
        dyan = dy[:, CONV_W:D_MODEL]
        yat = ya_ref[...]
        ra = _rstd(yat)
        yah = yat * ra
        dgoa_ref[...] += _colsum(dyan * yah)
        dya_ref[...] = _norm_bwd(dyan, yah, ra, goa_ref[...])

    return pl.pallas_call(
        body, name="bwd_ffn_b", grid=(nt,),
        in_specs=[_rows(tm, D_FF_PAD), _rows(tm, D_FF_PAD), _rows(tm, D_MODEL), _rows(tm, D_MODEL),
                  _rows(tm, ATTN_W), _const((N_CHIPS, D_MODEL, D_FF_SLAB)), _const((N_CHIPS, D_MODEL, D_FF_SLAB)),
                  _const((D_MODEL, D_MODEL)), _const((1, D_MODEL)), _const((1, ATTN_W))],
        out_specs=[_rows(tm, D_MODEL), _rows(tm, CONV_W), _rows(tm, ATTN_W), _rows(tm, D_MODEL),
                   pl.BlockSpec((1, D_MODEL), lambda i: (0, 0)), pl.BlockSpec((1, ATTN_W), lambda i: (0, 0))],
        out_shape=[jax.ShapeDtypeStruct((t, D_MODEL), F32), jax.ShapeDtypeStruct((t, CONV_W), F32),
                   jax.ShapeDtypeStruct((t, ATTN_W), F32), jax.ShapeDtypeStruct((t, D_MODEL), BF),
                   jax.ShapeDtypeStruct((1, D_MODEL), F32), jax.ShapeDtypeStruct((1, ATTN_W), F32)],
        compiler_params=_params(48),
    )(dgp, dup, dx2, x1, ya, wg4, wu4, wout, g_ffn, g_oa)


def _bwd_mix(x, dx1, zbcx, qkv, dycn, dq, dk, dv, win4, conv_w, conv_b, g_oc, g_mix, gm, gq8, gk8, tm):
    t = x.shape[0]
    nt = t // tm

    def body(x_ref, dx1_ref, z_ref, zh_ref, qkv_ref, dycn_ref, dq_ref, dk_ref, dv_ref, w_ref, cw_ref, cb_ref,
             goc_ref, g_ref, gm_ref, gq_ref, gk_ref,
             gx_ref, h1_ref, dz_ref, dcw_ref, dcb_ref, dgoc_ref, dg_ref, dgq_ref, dgk_ref, ubuf, dbuf):
        i = pl.program_id(0)

        @pl.when(i == 0)
        def _():
            dbuf[tm:tm + 8, :] = jnp.zeros((8, CONV_W), F32)
            dcw_ref[...] = jnp.zeros_like(dcw_ref)
            dcb_ref[...] = jnp.zeros_like(dcb_ref)
            dgoc_ref[...] = jnp.zeros_like(dgoc_ref)
            dg_ref[...] = jnp.zeros_like(dg_ref)
            dgq_ref[...] = jnp.zeros_like(dgq_ref)
            dgk_ref[...] = jnp.zeros_like(dgk_ref)

        not_first_tile = i < nt - 1
        zb = z_ref[:, 0:512]
        zc = z_ref[:, 512:1024]
        zx = z_ref[:, 1024:1536]
        u = zc * zx
        ubuf[0:8, :] = jnp.where(not_first_tile, zh_ref[:, 512:1024] * zh_ref[:, 1024:1536], 0.0)
        ubuf[8:8 + tm, :] = u
        ub = ubuf[...]
        u1 = _shift_down(ub, 1, tm)
        u2 = _shift_down(ub, 2, tm)
        w0, w1, w2 = cw_ref[0:1, :], cw_ref[1:2, :], cw_ref[2:3, :]
        cv = w0 * u2 + w1 * u1 + w2 * u + cb_ref[...]
        yc = zb * cv
        rc = _rstd(yc)
        ych = yc * rc
        dycn = dycn_ref[...]
        dgoc_ref[...] += _colsum(dycn * ych)
        dyc = _norm_bwd(dycn, ych, rc, goc_ref[...])
        dcv = dyc * zb
        dcb_ref[...] += _colsum(dcv)
        dcw_ref[0:1, :] += _colsum(dcv * u2)
        dcw_ref[1:2, :] += _colsum(dcv * u1)
        dcw_ref[2:3, :] += _colsum(dcv * u)
        dbuf[0:tm, :] = dcv
        db = dbuf[...]
        du = w2 * dcv + w1 * _shift_up(db, 1, tm) + w0 * _shift_up(db, 2, tm)
        dbuf[tm:tm + 8, :] = dbuf[0:8, :]
        dz_ref[:, 0:512] = (dyc * cv).astype(BF)
        dz_ref[:, 512:1024] = (du * zx).astype(BF)
        dz_ref[:, 1024:1536] = (du * zc).astype(BF)
        for z0, d_ref, gg_ref, acc_ref, sc in ((0, dq_ref, gq_ref, dgq_ref, HEAD_DIM ** -0.5),
                                               (512, dk_ref, gk_ref, dgk_ref, 1.0)):
            z = qkv_ref[:, z0:z0 + 512]
            rr = lax.rsqrt(_head_mean(z * z, gm_ref) + EPS)
            zh = z * rr
            dn = d_ref[...] * sc
            acc_ref[...] += _colsum(dn * zh)
            dzh = dn * gg_ref[...]
            dz_ref[:, 1536 + z0:1536 + z0 + 512] = (rr * (dzh - zh * _head_mean(dzh * zh, gm_ref))).astype(BF)
        dz_ref[:, 2560:3072] = dv_ref[...].astype(BF)
        dh1 = jnp.zeros((tm, D_MODEL), F32)
        for s in range(N_CHIPS):
            dh1 = dh1 + _mm_nt(dz_ref[:, s * IN_SLAB:(s + 1) * IN_SLAB], w_ref[s])
        xt = x_ref[...]
        r1 = _rstd(xt)
        xh = xt * r1
        h1_ref[...] = (xh * g_ref[...]).astype(BF)
        dg_ref[...] += _colsum(dh1 * xh)
        gx_ref[...] = dx1_ref[...] + _norm_bwd(dh1, xh, r1, g_ref[...])

    def acc(width, rows=1):
        return pl.BlockSpec((rows, width), lambda i: (0, 0))

    return pl.pallas_call(
        body, name="bwd_mix", grid=(nt,),
        in_specs=[_rows(tm, D_MODEL, nt), _rows(tm, D_MODEL, nt), _rows(tm, 1536, nt), _halo(tm, 1536, nt),
                  _rows(tm, 1536, nt), _rows(tm, CONV_W, nt), _rows(tm, ATTN_W, nt), _rows(tm, ATTN_W, nt),
                  _rows(tm, ATTN_W, nt), _const((N_CHIPS, D_MODEL, IN_SLAB)),
                  _const((3, CONV_W)), _const((1, CONV_W)), _const((1, CONV_W)), _const((1, D_MODEL)),
                  _const((ATTN_W, ATTN_W)), _const((1, ATTN_W)), _const((1, ATTN_W))],
        out_specs=[_rows(tm, D_MODEL, nt), _rows(tm, D_MODEL, nt), _rows(tm, 3072, nt),
                   acc(CONV_W, 3), acc(CONV_W), acc(CONV_W), acc(D_MODEL), acc(ATTN_W), acc(ATTN_W)],
        out_shape=[jax.ShapeDtypeStruct((t, D_MODEL), F32), jax.ShapeDtypeStruct((t, D_MODEL), BF),
                   jax.ShapeDtypeStruct((t, 3072), BF), jax.ShapeDtypeStruct((3, CONV_W), F32),
                   jax.ShapeDtypeStruct((1, CONV_W), F32), jax.ShapeDtypeStruct((1, CONV_W), F32),
                   jax.ShapeDtypeStruct((1, D_MODEL), F32), jax.ShapeDtypeStruct((1, ATTN_W), F32),
                   jax.ShapeDtypeStruct((1, ATTN_W), F32)],
        scratch_shapes=[pltpu.VMEM((tm + 8, CONV_W), F32), pltpu.VMEM((tm + 8, CONV_W), F32)],
        compiler_params=_params(56),
    )(x, dx1, zbcx, zbcx, qkv, dycn, dq, dk, dv, win4, conv_w, conv_b, g_oc, g_mix, gm, gq8, gk8)


def _wgrad(a, b, tn, tt, name):
    t, k = a.shape
    n = b.shape[1]

    def body(a_ref, b_ref, o_ref):
        @pl.when(pl.program_id(1) == 0)
        def _():
            o_ref[...] = jnp.zeros_like(o_ref)

        o_ref[...] += _mm_tn(a_ref[...].astype(BF), b_ref[...].astype(BF))

    return pl.pallas_call(
        body, name=name, grid=(n // tn, t // tt),
        in_specs=[pl.BlockSpec((tt, k), lambda j, i: (i, 0)), pl.BlockSpec((tt, tn), lambda j, i: (i, j))],
        out_specs=pl.BlockSpec((k, tn), lambda j, i: (0, j)),
        out_shape=jax.ShapeDtypeStruct((k, n), F32),
        compiler_params=_params(48, 2),
    )(a, b)


def _place():
    x, y, c = lax.axis_index("x"), lax.axis_index("y"), lax.axis_index("c")
    return x, y, c


def _chip_peer(x, y, k):
    return x ^ (k >> 1), y ^ (k & 1)


def _gather_weights(shards, pack):
    nw = len(shards)

    def body(*refs):
        ins = refs[:nw]
        pack_ref = refs[nw]
        outs = refs[nw + 1:2 * nw + 1]
        pack_out = refs[2 * nw + 1]
        send_sems, recv_sems, local_sems = refs[2 * nw + 2:]
        x, y, c = _place()
        me = 2 * x + y
        local, remote = [], []

        def sem(w, j):
            return w * 6 + j

        def push(src, dst, w, j, to):
            return pltpu.make_async_remote_copy(src_ref=src, dst_ref=dst, send_sem=send_sems.at[sem(w, j)],
                                                recv_sem=recv_sems.at[sem(w, j)], device_id=to, device_id_type=MESH)

        def half_rows(w, h):
            half = ins[w].shape[0] // 2
            return pl.ds(pl.multiple_of(h * half, 16), half)

        for w in range(nw):
            local.append(pltpu.make_async_copy(ins[w], outs[w].at[me], local_sems.at[w]))
            for k in (1, 2, 3):
                px, py = _chip_peer(x, y, k)
                mine = half_rows(w, c)
                remote.append(push(ins[w].at[mine], outs[w].at[me, mine], w, k - 1, (px, py, c)))
        local.append(pltpu.make_async_copy(pack_ref, pack_out.at[me], local_sems.at[nw]))
        for k in (1, 2, 3):
            px, py = _chip_peer(x, y, k)
            remote.append(push(pack_ref, pack_out.at[me], nw, k - 1, (px, py, c)))
        for cp in local + remote:
            cp.start()
        for w in range(nw):
            for k in (1, 2, 3):
                landed = outs[w].at[me ^ k, half_rows(w, c)]
                push(landed, landed, w, k - 1, (x, y, c)).wait_recv()
                fw = push(landed, landed, w, 2 + k, (x, y, 1 - c))
                fw.start()
                remote.append(fw)
        for k in (1, 2, 3):
            landed = pack_out.at[me ^ k]
            push(landed, landed, nw, k - 1, (x, y, c)).wait_recv()
        for w in range(nw):
            for k in (1, 2, 3):
                landed = outs[w].at[me ^ k, half_rows(w, 1 - c)]
                push(landed, landed, w, 2 + k, (x, y, c)).wait_recv()
        for cp in remote:
            cp.wait_send()
        for cp in local:
            cp.wait()

    any_spec = pl.BlockSpec(memory_space=pl.ANY)
    out_shape = [jax.ShapeDtypeStruct((N_CHIPS,) + s.shape, s.dtype) for s in shards]
    out_shape.append(jax.ShapeDtypeStruct((N_CHIPS,) + pack.shape, pack.dtype))
    return pl.pallas_call(
        body, name="gather_weights",
        in_specs=[any_spec] * (nw + 1), out_specs=[any_spec] * (nw + 1), out_shape=out_shape,
        scratch_shapes=[pltpu.SemaphoreType.DMA(((nw + 1) * 6,)), pltpu.SemaphoreType.DMA(((nw + 1) * 6,)),
                        pltpu.SemaphoreType.DMA((nw + 1,))],
    )(*shards, pack)


def _adamw(w, g, m, v):
    m = ADAM_B1 * m + (1.0 - ADAM_B1) * g
    v = ADAM_B2 * v + (1.0 - ADAM_B2) * (g * g)
    m_hat = m / (1.0 - ADAM_B1 ** ADAM_STEP)
    v_hat = v / (1.0 - ADAM_B2 ** ADAM_STEP)
    delta = -ADAM_LR * (m_hat / (jnp.sqrt(v_hat) + ADAM_EPS) + ADAM_WD * w)
    return delta, m, v


def _reduce_adamw(grad, w, m, v, col_sharded, name):
    kk, nn = grad.shape
    if col_sharded:
        r, cw = kk // 2, nn // N_CHIPS
    else:
        r, cw = kk // (2 * N_CHIPS), nn
    vr, vc = w.shape
    chunk = 64
    assert r % chunk == 0 and vr % chunk == 0 and vr <= 2 * r and vc <= cw

    def window(s, h):
        if col_sharded:
            return (pl.ds(pl.multiple_of(h * r, 8), r), pl.ds(pl.multiple_of(s * cw, 128), cw))
        return (pl.ds(pl.multiple_of((2 * s + h) * r, 8), r), slice(None))

    def body(g_hbm, w_ref, m_ref, v_ref, go_ref, do_ref, mo_ref, vo_ref,
             own, bufa, stage, bufb, full, lsem, a_send, a_recv, b_send, b_recv, c_send, c_recv):
        x, y, c = _place()
        me = 2 * x + y
        sib = (x, y, 1 - c)
        for s in range(N_CHIPS):
            pltpu.make_async_copy(g_hbm.at[window(s, c)], own.at[s], lsem.at[s]).start()
            pltpu.make_async_remote_copy(
                src_ref=g_hbm.at[window(s, 1 - c)], dst_ref=bufa.at[s], send_sem=a_send.at[s], recv_sem=a_recv.at[s],
                device_id=sib, device_id_type=MESH).start()
        for s in range(N_CHIPS):
            pltpu.make_async_copy(g_hbm.at[window(s, c)], own.at[s], lsem.at[s]).wait()
            pltpu.make_async_remote_copy(
                src_ref=g_hbm.at[window(s, 1 - c)], dst_ref=bufa.at[s], send_sem=a_send.at[s], recv_sem=a_recv.at[s],
                device_id=sib, device_id_type=MESH).wait()

        def add_a(j, carry):
            rows = pl.ds(pl.multiple_of(j * chunk, 8), chunk)
            for s in range(N_CHIPS):
                own[s, rows, :] = own[s, rows, :] + bufa[s, rows, :]
            for k in (1, 2, 3):
                stage[k - 1, rows, :] = own[me ^ k, rows, :].astype(BF)
            return carry

        lax.fori_loop(0, r // chunk, add_a, 0)
        sends = []
        for k in (1, 2, 3):
            px, py = _chip_peer(x, y, k)
            cp = pltpu.make_async_remote_copy(
                src_ref=stage.at[k - 1], dst_ref=bufb.at[k - 1], send_sem=b_send.at[k - 1], recv_sem=b_recv.at[k - 1],
                device_id=(px, py, c), device_id_type=MESH)
            cp.start()
            sends.append(cp)
        for cp in sends:
            cp.wait()
        mine = pl.multiple_of(c * r, 8)

        def add_b(j, carry):
            rows = pl.ds(pl.multiple_of(j * chunk, 8), chunk)
            tot = ((own[me, rows, :] + bufb[0, rows, :].astype(F32))
                   + (bufb[1, rows, :].astype(F32) + bufb[2, rows, :].astype(F32)))
            full[pl.ds(mine + pl.multiple_of(j * chunk, 8), chunk), :] = tot
            return carry

        lax.fori_loop(0, r // chunk, add_b, 0)
        half = full.at[pl.ds(mine, r), :]
        cp = pltpu.make_async_remote_copy(src_ref=half, dst_ref=half, send_sem=c_send, recv_sem=c_recv,
                                          device_id=sib, device_id_type=MESH)
        cp.start()
        cp.wait()

        def update(j, carry):
            rows = pl.ds(pl.multiple_of(j * chunk, 8), chunk)
            g = full[rows, 0:vc]
            delta, mn, vn = _adamw(w_ref[rows, :], g, m_ref[rows, :], v_ref[rows, :])
            go_ref[rows, :] = g
            do_ref[rows, :] = delta
            mo_ref[rows, :] = mn
            vo_ref[rows, :] = vn
            return carry

        lax.fori_loop(0, vr // chunk, update, 0)

    any_spec = pl.BlockSpec(memory_space=pl.ANY)
    vmem = pl.BlockSpec(memory_space=pltpu.VMEM)
    shard = jax.ShapeDtypeStruct((vr, vc), F32)
    return pl.pallas_call(
        body, name=name,
        in_specs=[any_spec, vmem, vmem, vmem], out_specs=[vmem] * 4, out_shape=[shard] * 4,
        scratch_shapes=[pltpu.VMEM((N_CHIPS, r, cw), F32), pltpu.VMEM((N_CHIPS, r, cw), F32),
                        pltpu.VMEM((3, r, cw), BF), pltpu.VMEM((3, r, cw), BF), pltpu.VMEM((2 * r, cw), F32),
                        pltpu.SemaphoreType.DMA((N_CHIPS,)), pltpu.SemaphoreType.DMA((N_CHIPS,)),
                        pltpu.SemaphoreType.DMA((N_CHIPS,)), pltpu.SemaphoreType.DMA((3,)),
                        pltpu.SemaphoreType.DMA((3,)), pltpu.SemaphoreType.DMA, pltpu.SemaphoreType.DMA],
        compiler_params=pltpu.CompilerParams(vmem_limit_bytes=56 * MIB),
    )(grad, w, m, v)


def _allreduce_small(pack):
    rows = pack.shape[0]

    def body(p_ref, o_ref, slots, send_sems, recv_sems):
        x, y, c = _place()
        me = 4 * x + 2 * y + c
        slots[me] = p_ref[...]
        sends = []
        for k in range(1, 8):
            cp = pltpu.make_async_remote_copy(
                src_ref=p_ref, dst_ref=slots.at[me], send_sem=send_sems.at[k - 1], recv_sem=recv_sems.at[k - 1],
                device_id=(x ^ (k >> 2), y ^ ((k >> 1) & 1), c ^ (k & 1)), device_id_type=MESH)
            cp.start()
            sends.append(cp)
        for cp in sends:
            cp.wait()
        tot = slots[0]
        for j in range(1, 8):
            tot = tot + slots[j]
        o_ref[...] = tot

    vmem = pl.BlockSpec(memory_space=pltpu.VMEM)
    return pl.pallas_call(
        body, name="allreduce_small", in_specs=[vmem], out_specs=vmem,
        out_shape=jax.ShapeDtypeStruct(pack.shape, F32),
        scratch_shapes=[pltpu.VMEM((8, rows, D_MODEL), F32), pltpu.SemaphoreType.DMA((7,)),
                        pltpu.SemaphoreType.DMA((7,))],
    )(pack)


def _adamw_small(ws, gs, ms, vs):
    n = len(ws)

    def body(*refs):
        w_refs, g_refs, m_refs, v_refs = refs[0:n], refs[n:2 * n], refs[2 * n:3 * n], refs[3 * n:4 * n]
        d_refs, mo_refs, vo_refs = refs[4 * n:5 * n], refs[5 * n:6 * n], refs[6 * n:7 * n]
        for j in range(n):
            delta, mn, vn = _adamw(w_refs[j][...], g_refs[j][...], m_refs[j][...], v_refs[j][...])
            d_refs[j][...] = delta
            mo_refs[j][...] = mn
            vo_refs[j][...] = vn

    vmem = pl.BlockSpec(memory_space=pltpu.VMEM)
    shapes = [jax.ShapeDtypeStruct(w.shape, F32) for w in ws]
    outs = pl.pallas_call(
        body, name="adamw_small", in_specs=[vmem] * (4 * n), out_specs=[vmem] * (3 * n), out_shape=shapes * 3,
    )(*ws, *gs, *ms, *vs)
    return outs[0:n], outs[n:2 * n], outs[2 * n:3 * n]


def _pad_slab(a, axis):
    pad = [(0, 0)] * a.ndim
    pad[axis] = (0, D_FF_SLAB - D_FF_SHARD)
    return jnp.pad(a, pad)


def _local_step(x, p, target, wts):
    (win4, wout, wg4, wu4, wd4, wpg, wpp4, conv_w, fcw, g_mix, conv_b, gq, gk, g_oc, g_oa, g_ffn, fcb, g_ple) = wts
    gm = jnp.kron(jnp.eye(N_HEADS, dtype=F32), jnp.full((HEAD_DIM, HEAD_DIM), 1.0 / HEAD_DIM, F32)).astype(BF)
    gq8, gk8 = jnp.tile(gq, (1, N_HEADS)), jnp.tile(gk, (1, N_HEADS))
    mb, mbk = _mask_tables()
    zbcx, qkv, ycn, qkn = _fwd_mix(x, g_mix, win4, conv_w, conv_b, g_oc, gm, gq8, gk8, 512)
    ya, lse = _attn_fwd(qkn, qkv, mb)
    x1, gp, up, act, ycat = _fwd_ffn(x, ycn, ya, wout, wg4, wu4, g_oa, g_ffn, fcw, fcb, 256)
    dx2, h3, ds, dpp, dg_ple, loss = _fwd_tail(x1, act, p, target, wd4, wpg, wpp4, g_ple, 256)
    dgp, dup, dfcw, dfcb = _bwd_ffn_a(dx2, gp, up, wd4, fcw, fcb, 256)
    dx1, dycn, dya, h2, dg_ffn, dg_oa = _bwd_ffn_b(dgp, dup, dx2, x1, ya, wg4, wu4, wout, g_ffn, g_oa, 256)
    dq, dk, dv = _attn_bwd(qkn, qkv, ya, lse, dya, mb, mbk)
    grad_x, h1, dz, dcw, dcb, dg_oc, dg_mix, dgq8, dgk8 = _bwd_mix(
        x, dx1, zbcx, qkv, dycn, dq, dk, dv, win4, conv_w, conv_b, g_oc, g_mix, gm, gq8, gk8, 256)
    dgq = dgq8.reshape(N_HEADS, HEAD_DIM).sum(axis=0, keepdims=True)
    dgk = dgk8.reshape(N_HEADS, HEAD_DIM).sum(axis=0, keepdims=True)
    big = dict(
        w_in=_wgrad(h1, dz, 1536, 512, "wgrad_in"),
        w_out=_wgrad(ycat, dx1, 1024, 512, "wgrad_out"),
        w_gate=_wgrad(h2, dgp, 1536, 512, "wgrad_gate"),
        w_up=_wgrad(h2, dup, 1536, 512, "wgrad_up"),
        w_down=_wgrad(act, dx2, 512, 512, "wgrad_down"),
        w_ple_gate=_wgrad(h3, ds, 1024, 512, "wgrad_ple_gate"),
        w_ple_proj=_wgrad(p, dpp, 1024, 512, "wgrad_ple_proj"),
    )
    small = dict(g_mix=dg_mix, conv_w=dcw, conv_b=dcb, q_norm_g=dgq, k_norm_g=dgk, g_out_conv=dg_oc,
                 g_out_attn=dg_oa, g_ffn=dg_ffn, ffn_conv_w=dfcw, ffn_conv_b=dfcb, g_ple=dg_ple)
    return loss[0, 0], grad_x, big, small


_SMALL_ROWS = 24


def _pack_small(s):
    z64 = jnp.zeros((1, 1024 - 512 - 128), F32)
    rows = [s["g_mix"], s["g_ffn"], s["g_ple"],
            jnp.concatenate([s["conv_b"], s["g_out_conv"]], axis=1),
            jnp.concatenate([s["g_out_attn"], s["q_norm_g"], s["k_norm_g"], z64], axis=1),
            jnp.pad(s["conv_w"], ((0, 0), (0, 512))),
            s["ffn_conv_b"].reshape(3, 1024),
            s["ffn_conv_w"].reshape(9, 1024),
            jnp.zeros((_SMALL_ROWS - 20, 1024), F32)]
    return jnp.concatenate(rows, axis=0)


def _unpack_small(t):
    return dict(g_mix=t[0:1], g_ffn=t[1:2], g_ple=t[2:3], conv_b=t[3:4, 0:512], g_out_conv=t[3:4, 512:1024],
                g_out_attn=t[4:5, 0:512], q_norm_g=t[4:5, 512:576], k_norm_g=t[4:5, 576:640],
                conv_w=t[5:8, 0:512], ffn_conv_b=t[8:11].reshape(1, D_FF_PAD), ffn_conv_w=t[11:20].reshape(3, D_FF_PAD))


def _unpad_ff(a):
    r = a.shape[0]
    return a.reshape(r, N_CHIPS, D_FF_SLAB)[:, :, :D_FF_SHARD].reshape(r, N_CHIPS * D_FF_SHARD)


_BIG = ("w_in", "w_out", "w_gate", "w_up", "w_down", "w_ple_gate", "w_ple_proj")
_COL_SHARDED = dict(w_in=True, w_out=False, w_gate=True, w_up=True, w_down=False, w_ple_gate=False, w_ple_proj=True)
_WEIGHTS = ("g_mix", "w_in", "conv_w", "conv_b", "q_norm_g", "k_norm_g", "g_out_conv", "g_out_attn", "w_out",
            "g_ffn", "w_gate", "w_up", "ffn_conv_w", "ffn_conv_b", "w_down", "g_ple", "w_ple_gate", "w_ple_proj")


def kernel(x, p, g_mix, w_in, conv_w, conv_b, q_norm_g, k_norm_g, g_out_conv, g_out_attn, w_out, g_ffn, w_gate, w_up, ffn_conv_w, ffn_conv_b, w_down, g_ple, w_ple_gate, w_ple_proj, loss_target, m_g_mix, m_w_in, m_conv_w, m_conv_b, m_q_norm_g, m_k_norm_g, m_g_out_conv, m_g_out_attn, m_w_out, m_g_ffn, m_w_gate, m_w_up, m_ffn_conv_w, m_ffn_conv_b, m_w_down, m_g_ple, m_w_ple_gate, m_w_ple_proj, v_g_mix, v_w_in, v_conv_w, v_conv_b, v_q_norm_g, v_k_norm_g, v_g_out_conv, v_g_out_attn, v_w_out, v_g_ffn, v_w_gate, v_w_up, v_ffn_conv_w, v_ffn_conv_b, v_w_down, v_g_ple, v_w_ple_gate, v_w_ple_proj):
    w = dict(g_mix=g_mix, w_in=w_in, conv_w=conv_w, conv_b=conv_b, q_norm_g=q_norm_g, k_norm_g=k_norm_g,
             g_out_conv=g_out_conv, g_out_attn=g_out_attn, w_out=w_out, g_ffn=g_ffn, w_gate=w_gate, w_up=w_up,
             ffn_conv_w=ffn_conv_w, ffn_conv_b=ffn_conv_b, w_down=w_down, g_ple=g_ple, w_ple_gate=w_ple_gate,
             w_ple_proj=w_ple_proj)
    m = dict(g_mix=m_g_mix, w_in=m_w_in, conv_w=m_conv_w, conv_b=m_conv_b, q_norm_g=m_q_norm_g, k_norm_g=m_k_norm_g,
             g_out_conv=m_g_out_conv, g_out_attn=m_g_out_attn, w_out=m_w_out, g_ffn=m_g_ffn, w_gate=m_w_gate,
             w_up=m_w_up, ffn_conv_w=m_ffn_conv_w, ffn_conv_b=m_ffn_conv_b, w_down=m_w_down, g_ple=m_g_ple,
             w_ple_gate=m_w_ple_gate, w_ple_proj=m_w_ple_proj)
    v = dict(g_mix=v_g_mix, w_in=v_w_in, conv_w=v_conv_w, conv_b=v_conv_b, q_norm_g=v_q_norm_g, k_norm_g=v_k_norm_g,
             g_out_conv=v_g_out_conv, g_out_attn=v_g_out_attn, w_out=v_w_out, g_ffn=v_g_ffn, w_gate=v_w_gate,
             w_up=v_w_up, ffn_conv_w=v_ffn_conv_w, ffn_conv_b=v_ffn_conv_b, w_down=v_w_down, g_ple=v_g_ple,
             w_ple_gate=v_w_ple_gate, w_ple_proj=v_w_ple_proj)
    mats = [k for k, a in w.items() if a.ndim == 3]
    w = {k: (a[0] if k in mats else a) for k, a in w.items()}
    m = {k: (a[0] if k in mats else a) for k, a in m.items()}
    v = {k: (a[0] if k in mats else a) for k, a in v.items()}
    chip = 2 * lax.axis_index("x") + lax.axis_index("y")

    shards = [w["w_in"].astype(BF), w["w_out"].astype(BF), _pad_slab(w["w_gate"], 1).astype(BF),
              _pad_slab(w["w_up"], 1).astype(BF), _pad_slab(w["w_down"], 0).astype(BF),
              w["w_ple_gate"].astype(BF), w["w_ple_proj"].astype(BF)]
    pack = jnp.pad(jnp.concatenate([w["conv_w"], _pad_slab(w["ffn_conv_w"], 1)], axis=1), ((0, 5), (0, 128)))
    win4, wout4, wg4, wu4, wd4, wpg4, wpp4, pack4 = _gather_weights(shards, pack)
    conv_w_full = pack4[:, 0:3, 0:128].transpose(1, 0, 2).reshape(3, CONV_W)
    fcw_full = pack4[:, 0:3, 128:128 + D_FF_SLAB].transpose(1, 0, 2).reshape(3, D_FF_PAD)
    fcb_pad = _pad_slab(w["ffn_conv_b"].reshape(N_CHIPS, D_FF_SHARD), 1).reshape(1, D_FF_PAD)
    wts = (win4, wout4.reshape(D_MODEL, D_MODEL), wg4, wu4, wd4, wpg4.reshape(D_MODEL, D_MODEL), wpp4,
           conv_w_full, fcw_full, w["g_mix"], w["conv_b"], w["q_norm_g"], w["k_norm_g"], w["g_out_conv"],
           w["g_out_attn"], w["g_ffn"], fcb_pad, w["g_ple"])

    loss, grad_x, big, small = _local_step(x[0], p[0, 0], loss_target[0], wts)
    loss = lax.psum(loss, ("x", "y", "c"))

    grads, deltas, new_m, new_v = {}, {}, {}, {}
    for name in _BIG:
        grads[name], deltas[name], new_m[name], new_v[name] = _reduce_adamw(
            big[name], w[name], m[name], v[name], _COL_SHARDED[name], "reduce_" + name)
    tot = _unpack_small(_allreduce_small(_pack_small(small)))
    tot["conv_w"] = lax.dynamic_slice_in_dim(tot["conv_w"], chip * 128, 128, axis=1)
    tot["ffn_conv_w"] = lax.dynamic_slice_in_dim(tot["ffn_conv_w"], chip * D_FF_SLAB, D_FF_SLAB, axis=1)[:, :D_FF_SHARD]
    tot["ffn_conv_b"] = _unpad_ff(tot["ffn_conv_b"])
    names = [n for n in _WEIGHTS if n not in _BIG]
    d_s, m_s, v_s = _adamw_small([w[n] for n in names], [tot[n] for n in names], [m[n] for n in names],
                                 [v[n] for n in names])
    for j, n in enumerate(names):
        grads[n], deltas[n], new_m[n], new_v[n] = tot[n], d_s[j], m_s[j], v_s[j]

    out = [loss, grad_x[None]]
    for group in (grads, deltas, new_m, new_v):
        out += [group[n][None] if n in mats else group[n] for n in _WEIGHTS]
    return tuple(out)
```

```python
import jax
import jax.numpy as jnp
from jax import lax
from jax.experimental import pallas as pl
from jax.experimental.pallas import tpu as pltpu

D_MODEL = 1024
CONV_W = 512
N_HEADS = 8
HEAD_DIM = 64
ATTN_W = 512
D_FF_SHARD = 704
D_FF_SLAB = 768
D_FF_PAD = 4 * D_FF_SLAB
IN_SLAB = 768
PLE_DIM = 256
N_CHIPS = 4
QBLK = 128
DILATIONS = (1, 4, 16)
EPS = 1e-6
NEG = -1e30
MESH = pl.DeviceIdType.MESH

ADAM_LR = 0.001
ADAM_B1 = 0.9
ADAM_B2 = 0.999
ADAM_EPS = 1e-08
ADAM_WD = 0.01
ADAM_STEP = 10

BF = jnp.bfloat16
F32 = jnp.float32
MIB = 1024 * 1024


def _mm(a, b):
    return jnp.dot(a, b, preferred_element_type=F32)


def _mm_nt(a, b):
    return lax.dot_general(a, b, (((1,), (1,)), ((), ())), preferred_element_type=F32)


def _mm_tn(a, b):
    return lax.dot_general(a, b, (((0,), (0,)), ((), ())), preferred_element_type=F32)


def _rstd(a):
    return lax.rsqrt(jnp.mean(a * a, axis=-1, keepdims=True) + EPS)


def _norm_bwd(dy, xh, r, g):
    dxh = dy * g
    return r * (dxh - xh * jnp.mean(dxh * xh, axis=-1, keepdims=True))


def _colsum(a):
    return jnp.sum(a, axis=0, keepdims=True)


def _head_mean(a, gm_ref):
    hi = a.astype(BF)
    lo = (a - hi.astype(F32)).astype(BF)
    return _mm(hi, gm_ref[...]) + _mm(lo, gm_ref[...])


def _shift_down(buf, k, tm):
    return pltpu.roll(buf, k, axis=0)[8:8 + tm]


def _shift_up(buf, k, tm):
    return pltpu.roll(buf, tm + 8 - k, axis=0)[0:tm]


def _params(vmem_mib, n_grid=1):
    return pltpu.CompilerParams(dimension_semantics=("arbitrary",) * n_grid, vmem_limit_bytes=vmem_mib * MIB)


def _const(shape):
    n = len(shape)
    return pl.BlockSpec(shape, lambda *_: (0,) * n, pipeline_mode=pl.Buffered(1))


def _rows(tm, width, rev_of=None):
    if rev_of is None:
        return pl.BlockSpec((tm, width), lambda i: (i, 0))
    return pl.BlockSpec((tm, width), lambda i: (rev_of - 1 - i, 0))


def _halo(tm, width, nt):
    return pl.BlockSpec((8, width), lambda i: (jnp.maximum((nt - 1 - i) * (tm // 8) - 1, 0), 0))


def _fwd_mix(x, g_mix, win4, conv_w, conv_b, g_oc, gm, gq8, gk8, tm):
    t = x.shape[0]
    nt = t // tm

    def body(x_ref, g_ref, w_ref, cw_ref, cb_ref, goc_ref, gm_ref, gq_ref, gk_ref,
             zbcx_ref, qkv_ref, ycn_ref, qkn_ref, ubuf):
        @pl.when(pl.program_id(0) == 0)
        def _():
            ubuf[0:8, :] = jnp.zeros((8, CONV_W), F32)

        xt = x_ref[...]
        h = ((xt * _rstd(xt)) * g_ref[...]).astype(BF)
        zbcx_ref[:, 0:IN_SLAB] = _mm(h, w_ref[0])
        zbcx_ref[:, IN_SLAB:2 * IN_SLAB] = _mm(h, w_ref[1])
        qkv_ref[:, 0:IN_SLAB] = _mm(h, w_ref[2])
        qkv_ref[:, IN_SLAB:2 * IN_SLAB] = _mm(h, w_ref[3])
        u = zbcx_ref[:, 512:1024] * zbcx_ref[:, 1024:1536]
        ubuf[8:8 + tm, :] = u
        ub = ubuf[...]
        cv = (cw_ref[0:1, :] * _shift_down(ub, 2, tm) + cw_ref[1:2, :] * _shift_down(ub, 1, tm)
              + cw_ref[2:3, :] * u + cb_ref[...])
        ubuf[0:8, :] = ubuf[tm:tm + 8, :]
        yc = zbcx_ref[:, 0:512] * cv
        ycn_ref[...] = ((yc * _rstd(yc)) * goc_ref[...]).astype(BF)
        zq = qkv_ref[:, 0:512]
        zk = qkv_ref[:, 512:1024]
        rq = lax.rsqrt(_head_mean(zq * zq, gm_ref) + EPS)
        rk = lax.rsqrt(_head_mean(zk * zk, gm_ref) + EPS)
        qkn_ref[:, 0:512] = ((zq * rq) * gq_ref[...]) * (HEAD_DIM ** -0.5)
        qkn_ref[:, 512:1024] = (zk * rk) * gk_ref[...]

    return pl.pallas_call(
        body, name="fwd_mix", grid=(nt,),
        in_specs=[_rows(tm, D_MODEL), _const((1, D_MODEL)), _const((N_CHIPS, D_MODEL, IN_SLAB)),
                  _const((3, CONV_W)), _const((1, CONV_W)), _const((1, CONV_W)), _const((ATTN_W, ATTN_W)),
                  _const((1, ATTN_W)), _const((1, ATTN_W))],
        out_specs=[_rows(tm, 1536), _rows(tm, 1536), _rows(tm, CONV_W), _rows(tm, 1024)],
        out_shape=[jax.ShapeDtypeStruct((t, 1536), F32), jax.ShapeDtypeStruct((t, 1536), F32),
                   jax.ShapeDtypeStruct((t, CONV_W), BF), jax.ShapeDtypeStruct((t, 1024), F32)],
        scratch_shapes=[pltpu.VMEM((tm + 8, CONV_W), F32)],
        compiler_params=_params(48),
    )(x, g_mix, win4, conv_w, conv_b, g_oc, gm, gq8, gk8)


def _place():
    x, y, c = lax.axis_index("x"), lax.axis_index("y"), lax.axis_index("c")
    return x, y, c


def _chip_peer(x, y, k):
    return x ^ (k >> 1), y ^ (k & 1)


def _piece_shape(grad_shape, col_sharded):
    kk, nn = grad_shape
    return (kk // 2, nn // N_CHIPS) if col_sharded else (kk // (2 * N_CHIPS), nn)


def _piece_window(col_sharded, r, cw, s, h):
    if col_sharded:
        return (pl.ds(pl.multiple_of(h * r, 16), r), pl.ds(pl.multiple_of(s * cw, 128), cw))
    return (pl.ds(pl.multiple_of((2 * s + h) * r, 16), r), slice(None))


def _scatter_copies(g_ref, slots_ref, send_sems, recv_sems, base, col_sharded):
    x, y, c = _place()
    r, cw = slots_ref.shape[1:]
    copies = []
    for k in range(1, 8):
        tx, ty, tc = x ^ (k >> 2), y ^ ((k >> 1) & 1), c ^ (k & 1)
        copies.append(pltpu.make_async_remote_copy(
            src_ref=g_ref.at[_piece_window(col_sharded, r, cw, 2 * tx + ty, tc)], dst_ref=slots_ref.at[k - 1],
            send_sem=send_sems.at[base + k - 1], recv_sem=recv_sems.at[base + k - 1],
            device_id=(tx, ty, tc), device_id_type=MESH))
    return copies


def _ride_scatter(first, last, riders, g_refs, slot_refs, send_sems, recv_sems):
    def all_copies():
        out = []
        for j, (_, col_sharded) in enumerate(riders):
            out += _scatter_copies(g_refs[j], slot_refs[j], send_sems, recv_sems, 7 * j, col_sharded)
        return out

    @pl.when(first)
    def _():
        for cp in all_copies():
            cp.start()

    @pl.when(last)
    def _():
        for cp in all_copies():
            cp.wait()


def _rider_specs(riders):
    any_spec = pl.BlockSpec(memory_space=pl.ANY)
    shapes = [jax.ShapeDtypeStruct((7,) + _piece_shape(g.shape, cs), BF) for g, cs in riders]
    sems = [pltpu.SemaphoreType.DMA((7 * len(riders),)), pltpu.SemaphoreType.DMA((7 * len(riders),))] if riders else []
    return [any_spec] * len(riders), shapes, sems


class _Gather:
    def __init__(self, ins, outs, send_sems, recv_sems, local_sems):
        self.ins, self.outs = ins, outs
        self.send_sems, self.recv_sems, self.local_sems = send_sems, recv_sems, local_sems
        self.x, self.y, self.c = _place()
        self.me = 2 * self.x + self.y

    def _push(self, src, dst, w, j, to):
        return pltpu.make_async_remote_copy(src_ref=src, dst_ref=dst, send_sem=self.send_sems.at[6 * w + j],
                                            recv_sem=self.recv_sems.at[6 * w + j], device_id=to, device_id_type=MESH)

    def _half(self, w, h):
        half = self.ins[w].shape[0] // 2
        return pl.ds(pl.multiple_of(h * half, 16), half)

    def _local(self, w):
        return pltpu.make_async_copy(self.ins[w], self.outs[w].at[self.me], self.local_sems.at[w])

    def _ici(self, w, k):
        px, py = _chip_peer(self.x, self.y, k)
        mine = self._half(w, self.c)
        return self._push(self.ins[w].at[mine], self.outs[w].at[self.me, mine], w, k - 1, (px, py, self.c))

    def _landed(self, w, k, h):
        return self.outs[w].at[self.me ^ k, self._half(w, h)]

    def _fwd(self, w, k):
        landed = self._landed(w, k, self.c)
        return self._push(landed, landed, w, 2 + k, (self.x, self.y, 1 - self.c))

    def start(self):
        for w in range(len(self.ins)):
            self._local(w).start()
            for k in (1, 2, 3):
                self._ici(w, k).start()

    def forward(self):
        for w in range(len(self.ins)):
            for k in (1, 2, 3):
                landed = self._landed(w, k, self.c)
                self._push(landed, landed, w, k - 1, (self.x, self.y, self.c)).wait_recv()
                self._fwd(w, k).start()

    def finish(self):
        for w in range(len(self.ins)):
            for k in (1, 2, 3):
                landed = self._landed(w, k, 1 - self.c)
                self._push(landed, landed, w, 2 + k, (self.x, self.y, self.c)).wait_recv()
            for k in (1, 2, 3):
                self._ici(w, k).wait_send()
                self._fwd(w, k).wait_send()
            self._local(w).wait()


def _alibi(h):
    return 2.0 ** (-(h + 1))


CHUNK = 2048


def _mask_tables():
    slopes = jnp.asarray([_alibi(h) for h in range(N_HEADS)], F32)[:, None, None]

    def table(step):
        valid = (step >= 0) & (step <= QBLK)
        return jnp.stack([jnp.where(valid[None], -slopes * (step * d)[None].astype(F32), NEG) for d in DILATIONS])

    i = jnp.arange(QBLK)[:, None]
    j2 = jnp.arange(2 * QBLK)[None, :]
    r2 = jnp.arange(2 * QBLK)[:, None]
    j = jnp.arange(QBLK)[None, :]
    fwd, bwd = table(i + QBLK - j2), table(r2 - j)
    return fwd.reshape(3, N_HEADS // 2, 2 * QBLK, 2 * QBLK), bwd.reshape(3, N_HEADS // 2, 4 * QBLK, QBLK)


def _attn_fwd(qkn, qkv, mb, late=()):
    t = qkn.shape[0]
    nc = t // CHUNK
    nl = len(late)

    def body(*refs):
        qc_ref, kp_ref, kc_ref, vp_ref, vc_ref, mb_ref = refs[0:6]
        o_ref, l_ref = refs[6 + nl:8 + nl]
        ob0, ob1, ob2, lb0, lb1, lb2 = refs[8 + 2 * nl:14 + 2 * nl]
        if nl:
            gather = _Gather(refs[6:6 + nl], refs[8 + nl:8 + 2 * nl], *refs[14 + 2 * nl:17 + 2 * nl])
            step = pl.program_id(0) * nc + pl.program_id(1)
            pl.when(step == 0)(gather.start)
            pl.when(step == 2 * nc)(gather.forward)
            pl.when(step == (N_HEADS // 2) * nc - 1)(gather.finish)
        first = pl.program_id(1) == 0
        lane = lax.broadcasted_iota(jnp.int32, (QBLK, 128), 1)
        lo_half = lane < HEAD_DIM
        kj = lax.broadcasted_iota(jnp.int32, (2 * QBLK, 2 * QBLK), 1)
        no_prev = first & (kj < QBLK)
        obs, lbs = (ob0, ob1, ob2), (lb0, lb1, lb2)

        def by_head(a):
            return jnp.where(lo_half, a, 0.0).astype(BF), jnp.where(lo_half, 0.0, a).astype(BF)

        for di, d in enumerate(DILATIONS):
            span = d * QBLK
            for r in range(d):
                tail = pl.ds(CHUNK - span + r, QBLK, stride=d)
                k_prev = kp_ref[tail, :].astype(BF)
                v_prev = by_head(vp_ref[tail, :])
                for b in range(CHUNK // span):
                    rows = pl.ds(r + span * b, QBLK, stride=d)
                    q0, q1 = by_head(qc_ref[rows, :])
                    k_cur = kc_ref[rows, :].astype(BF)
                    v_cur = by_head(vc_ref[rows, :])
                    s = _mm_nt(jnp.concatenate([q0, q1], axis=0), jnp.concatenate([k_prev, k_cur], axis=0))
                    s = s + mb_ref[di, 0]
                    if b == 0:
                        s = jnp.where(no_prev, NEG, s)
                    m = jnp.max(s, axis=-1, keepdims=True)
                    e = jnp.exp(s - m)
                    den = jnp.sum(e, axis=-1, keepdims=True)
                    eb = e.astype(BF)
                    o = _mm(jnp.concatenate([eb[0:QBLK], eb[QBLK:2 * QBLK]], axis=1),
                            jnp.concatenate([v_prev[0], v_cur[0], v_prev[1], v_cur[1]], axis=0))
                    inv = 1.0 / den
                    lse = m + jnp.log(den)
                    obs[di][rows, :] = o * jnp.where(lo_half, inv[0:QBLK], inv[QBLK:2 * QBLK])
                    lbs[di][rows, :] = jnp.where(lo_half, lse[0:QBLK], lse[QBLK:2 * QBLK])
                    k_prev, v_prev = k_cur, v_cur
        for c0 in range(0, CHUNK, 256):
            rs = slice(c0, c0 + 256)
            l0, l1, l2 = lb0[rs, :], lb1[rs, :], lb2[rs, :]
            mx = jnp.maximum(jnp.maximum(l0, l1), l2)
            w0, w1, w2 = jnp.exp(l0 - mx), jnp.exp(l1 - mx), jnp.exp(l2 - mx)
            tot = w0 + w1 + w2
            o_ref[rs, :] = (ob0[rs, :] * w0 + ob1[rs, :] * w1 + ob2[rs, :] * w2) / tot
            l_ref[rs, :] = mx + jnp.log(tot)

    def cur(col):
        return pl.BlockSpec((CHUNK, 128), lambda hp, n: (n, col + hp))

    def prv(col):
        return pl.BlockSpec((CHUNK, 128), lambda hp, n: (jnp.maximum(n - 1, 0), col + hp))

    out = pl.BlockSpec((CHUNK, 128), lambda hp, n: (n, hp))
    any_spec = pl.BlockSpec(memory_space=pl.ANY)
    sems = [pltpu.SemaphoreType.DMA((6 * nl,)), pltpu.SemaphoreType.DMA((6 * nl,)), pltpu.SemaphoreType.DMA((nl,))]
    res = pl.pallas_call(
        body, name="attn_fwd", grid=(N_HEADS // 2, nc),
        in_specs=[cur(0), prv(4), cur(4), prv(8), cur(8),
                  pl.BlockSpec((3, 1, 2 * QBLK, 2 * QBLK), lambda hp, n: (0, hp, 0, 0))] + [any_spec] * nl,
        out_specs=[out, out] + [any_spec] * nl,
        out_shape=[jax.ShapeDtypeStruct((t, ATTN_W), F32)] * 2
        + [jax.ShapeDtypeStruct((N_CHIPS,) + w.shape, w.dtype) for w in late],
        scratch_shapes=[pltpu.VMEM((CHUNK, 128), F32)] * 6 + (sems if nl else []),
        compiler_params=_params(48, 2),
    )(qkn, qkn, qkn, qkv, qkv, mb, *late)
    return res[0], res[1], list(res[2:])


def _attn_bwd(qkn, qkv, o, lse, do, mb, mbk, riders=()):
    t = qkn.shape[0]
    nc = t // CHUNK
    nr = len(riders)

    def body(*refs):
        (qc_ref, qn_ref, kp_ref, kc_ref, vp_ref, vc_ref, oc_ref, on_ref, lc_ref, ln_ref, dc_ref, dn_ref,
         mb_ref, mbk_ref) = refs[0:14]
        dq_ref, dk_ref, dv_ref = refs[14 + nr:17 + nr]
        if nr:
            step = pl.program_id(0) * nc + pl.program_id(1)
            _ride_scatter(step == 0, step == (N_HEADS // 2) * nc - 1, riders, refs[14:14 + nr],
                          refs[17 + nr:17 + 2 * nr], *refs[17 + 2 * nr:19 + 2 * nr])
        first = pl.program_id(1) == 0
        last = pl.program_id(1) == nc - 1
        lane = lax.broadcasted_iota(jnp.int32, (QBLK, 128), 1)
        lo_half = lane < HEAD_DIM
        kj = lax.broadcasted_iota(jnp.int32, (2 * QBLK, 2 * QBLK), 1)
        no_prev = first & (kj < QBLK)
        ri = lax.broadcasted_iota(jnp.int32, (4 * QBLK, QBLK), 0)
        no_next = last & ((ri & (2 * QBLK - 1)) >= QBLK)

        def by_head(a):
            return jnp.where(lo_half, a, 0.0).astype(BF), jnp.where(lo_half, 0.0, a).astype(BF)

        def query_side(q_ref, d_ref, o_ref_, l_ref_, rows):
            dvals = d_ref[rows, :]
            dd = dvals * o_ref_[rows, :]
            lv = l_ref_[rows, :]
            d0 = jnp.sum(jnp.where(lo_half, dd, 0.0), axis=-1, keepdims=True)
            d1 = jnp.sum(jnp.where(lo_half, 0.0, dd), axis=-1, keepdims=True)
            l0 = jnp.max(jnp.where(lo_half, lv, NEG), axis=-1, keepdims=True)
            l1 = jnp.max(jnp.where(lo_half, NEG, lv), axis=-1, keepdims=True)
            return by_head(q_ref[rows, :]), by_head(dvals), (l0, l1), (d0, d1)

        for di, d in enumerate(DILATIONS):
            span = d * QBLK
            nbk = CHUNK // span
            for r in range(d):
                tail = pl.ds(CHUNK - span + r, QBLK, stride=d)
                k_prev = kp_ref[tail, :]
                kb_prev, km_prev = k_prev.astype(BF), by_head(k_prev)
                vb_prev = vp_ref[tail, :].astype(BF)
                rows0 = pl.ds(r, QBLK, stride=d)
                cur = query_side(qc_ref, dc_ref, oc_ref, lc_ref, rows0)
                for b in range(nbk):
                    rows = pl.ds(r + span * b, QBLK, stride=d)
                    if b == nbk - 1:
                        nxt = query_side(qn_ref, dn_ref, on_ref, ln_ref, rows0)
                    else:
                        nxt = query_side(qc_ref, dc_ref, oc_ref, lc_ref, pl.ds(r + span * (b + 1), QBLK, stride=d))
                    (q0, q1), (do0, do1), (l0, l1), (d0, d1) = cur
                    (qx0, qx1), (dox0, dox1), (lx0, lx1), (dx0, dx1) = nxt
                    k_cur = kc_ref[rows, :]
                    kb_cur, km_cur = k_cur.astype(BF), by_head(k_cur)
                    vb_cur = vc_ref[rows, :].astype(BF)
                    k2 = jnp.concatenate([kb_prev, kb_cur], axis=0)
                    v2 = jnp.concatenate([vb_prev, vb_cur], axis=0)
                    s = _mm_nt(jnp.concatenate([q0, q1], axis=0), k2) + mb_ref[di, 0]
                    if b == 0:
                        s = jnp.where(no_prev, NEG, s)
                    p = jnp.exp(s - jnp.concatenate([l0, l1], axis=0))
                    dp = _mm_nt(jnp.concatenate([do0, do1], axis=0), v2)
                    ds = (p * (dp - jnp.concatenate([d0, d1], axis=0))).astype(BF)
                    dq2 = _mm(jnp.concatenate([ds[0:QBLK], ds[QBLK:2 * QBLK]], axis=1),
                              jnp.concatenate([km_prev[0], km_cur[0], km_prev[1], km_cur[1]], axis=0))
                    if di == 0:
                        dq_ref[rows, :] = dq2
                    else:
                        dq_ref[rows, :] = dq_ref[rows, :] + dq2
                    q4 = jnp.concatenate([q0, qx0, q1, qx1], axis=0)
                    do4 = jnp.concatenate([do0, dox0, do1, dox1], axis=0)
                    s = _mm_nt(q4, kb_cur) + mbk_ref[di, 0]
                    if b == nbk - 1:
                        s = jnp.where(no_next, NEG, s)
                    p = jnp.exp(s - jnp.concatenate([l0, lx0, l1, lx1], axis=0))
                    dv2 = _mm_tn(p.astype(BF), do4)
                    dp = _mm_nt(do4, vb_cur)
                    ds = (p * (dp - jnp.concatenate([d0, dx0, d1, dx1], axis=0))).astype(BF)
                    dk2 = _mm_tn(ds, q4)
                    if di == 0:
                        dk_ref[rows, :] = dk2
                        dv_ref[rows, :] = dv2
                    else:
                        dk_ref[rows, :] = dk_ref[rows, :] + dk2
                        dv_ref[rows, :] = dv_ref[rows, :] + dv2
                    cur = nxt
                    kb_prev, km_prev, vb_prev = kb_cur, km_cur, vb_cur

    def at(shift, col):
        return pl.BlockSpec((CHUNK, 128), lambda hp, n: (jnp.clip(n + shift, 0, nc - 1), col + hp))

    out = pl.BlockSpec((CHUNK, 128), lambda hp, n: (n, hp))
    r_in, r_out, r_sems = _rider_specs(riders)
    res = pl.pallas_call(
        body, name="attn_bwd", grid=(N_HEADS // 2, nc),
        in_specs=[at(0, 0), at(1, 0), at(-1, 4), at(0, 4), at(-1, 8), at(0, 8),
                  at(0, 0), at(1, 0), at(0, 0), at(1, 0), at(0, 0), at(1, 0),
                  pl.BlockSpec((3, 1, 2 * QBLK, 2 * QBLK), lambda hp, n: (0, hp, 0, 0)),
                  pl.BlockSpec((3, 1, 4 * QBLK, QBLK), lambda hp, n: (0, hp, 0, 0))] + r_in,
        out_specs=[out, out, out] + r_in,
        out_shape=[jax.ShapeDtypeStruct((t, ATTN_W), F32)] * 3 + r_out,
        scratch_shapes=r_sems,
        compiler_params=_params(56, 2),
    )(qkn, qkn, qkn, qkn, qkv, qkv, o, o, lse, lse, do, do, mb, mbk, *[g for g, _ in riders])
    return res[0], res[1], res[2], list(res[3:])


def _fwd_ffn(x, ycn, ya, wout, wg4, wu4, g_oa, g_ffn, fcw, fcb, tm):
    t = x.shape[0]
    nt = t // tm

    def body(x_ref, ycn_ref, ya_ref, wout_ref, wg_ref, wu_ref, goa_ref, gffn_ref, fcw_ref, fcb_ref,
             x1_ref, gp_ref, up_ref, act_ref, ycat_ref, h2_ref, cbuf):
        @pl.when(pl.program_id(0) == 0)
        def _():
            cbuf[0:8, :] = jnp.zeros((8, D_FF_PAD), F32)

        yat = ya_ref[...]
        yan = ((yat * _rstd(yat)) * goa_ref[...]).astype(BF)
        ycn = ycn_ref[...]
        ycat_ref[:, 0:CONV_W] = ycn
        ycat_ref[:, CONV_W:D_MODEL] = yan
        x1 = x_ref[...] + _mm(ycn, wout_ref[0:CONV_W, :]) + _mm(yan, wout_ref[CONV_W:D_MODEL, :])
        x1_ref[...] = x1
        h2 = ((x1 * _rstd(x1)) * gffn_ref[...]).astype(BF)
        h2_ref[...] = h2
        for s in range(N_CHIPS):
            lo, hi = s * D_FF_SLAB, (s + 1) * D_FF_SLAB
            gps = _mm(h2, wg_ref[s])
            ups = _mm(h2, wu_ref[s])
            gp_ref[:, lo:hi] = gps
            up_ref[:, lo:hi] = ups
            cbuf[8:8 + tm, lo:hi] = gps
            cb = cbuf[:, lo:hi]
            gate = (fcw_ref[0:1, lo:hi] * _shift_down(cb, 2, tm) + fcw_ref[1:2, lo:hi] * _shift_down(cb, 1, tm)
                    + fcw_ref[2:3, lo:hi] * gps + fcb_ref[:, lo:hi])
            act_ref[:, lo:hi] = ((gate * jax.nn.sigmoid(gate)) * ups).astype(BF)
        cbuf[0:8, :] = cbuf[tm:tm + 8, :]

    return pl.pallas_call(
        body, name="fwd_ffn", grid=(nt,),
        in_specs=[_rows(tm, D_MODEL), _rows(tm, CONV_W), _rows(tm, ATTN_W), _const((D_MODEL, D_MODEL)),
                  _const((N_CHIPS, D_MODEL, D_FF_SLAB)), _const((N_CHIPS, D_MODEL, D_FF_SLAB)),
                  _const((1, ATTN_W)), _const((1, D_MODEL)), _const((3, D_FF_PAD)), _const((1, D_FF_PAD))],
        out_specs=[_rows(tm, D_MODEL), _rows(tm, D_FF_PAD), _rows(tm, D_FF_PAD), _rows(tm, D_FF_PAD),
                   _rows(tm, D_MODEL), _rows(tm, D_MODEL)],
        out_shape=[jax.ShapeDtypeStruct((t, D_MODEL), F32), jax.ShapeDtypeStruct((t, D_FF_PAD), F32),
                   jax.ShapeDtypeStruct((t, D_FF_PAD), F32), jax.ShapeDtypeStruct((t, D_FF_PAD), BF),
                   jax.ShapeDtypeStruct((t, D_MODEL), BF), jax.ShapeDtypeStruct((t, D_MODEL), BF)],
        scratch_shapes=[pltpu.VMEM((tm + 8, D_FF_PAD), F32)],
        compiler_params=_params(56),
    )(x, ycn, ya, wout, wg4, wu4, g_oa, g_ffn, fcw, fcb)


def _fwd_tail(x1, act, p, target, wd4, wpg, wpp4, g_ple, tm):
    t = x1.shape[0]
    nt = t // tm

    def body(x1_ref, act_ref, p_ref, tgt_ref, wd_ref, wpg_ref, wpp_ref, g_ref,
             dx2_ref, h3_ref, ds_ref, dpp_ref, dg_ref, loss_ref, lacc):
        i = pl.program_id(0)

        @pl.when(i == 0)
        def _():
            dg_ref[...] = jnp.zeros_like(dg_ref)
            lacc[...] = jnp.zeros_like(lacc)

        x2 = x1_ref[...]
        for s in range(N_CHIPS):
            x2 = x2 + _mm(act_ref[:, s * D_FF_SLAB:(s + 1) * D_FF_SLAB], wd_ref[s])
        r3 = _rstd(x2)
        xh = x2 * r3
        h3 = (xh * g_ref[...]).astype(BF)
        h3_ref[...] = h3
        sg = jax.nn.sigmoid(_mm(h3, wpg_ref[...]))
        pb = p_ref[...].astype(BF)
        pp = jnp.concatenate([_mm(pb, wpp_ref[s]) for s in range(N_CHIPS)], axis=1)
        err = (x2 + sg * pp) - tgt_ref[...]
        lacc[...] += _colsum(err * err)
        dx3 = err * (1.0 / D_MODEL)
        dpp_ref[...] = (dx3 * sg).astype(BF)
        dsb = ((dx3 * pp) * (sg * (1.0 - sg))).astype(BF)
        ds_ref[...] = dsb
        dh3 = _mm_nt(dsb, wpg_ref[...])
        dg_ref[...] += _colsum(dh3 * xh)
        dx2_ref[...] = dx3 + _norm_bwd(dh3, xh, r3, g_ref[...])

        @pl.when(i == nt - 1)
        def _():
            loss_ref[...] = jnp.full((1, 128), jnp.sum(lacc[...]) * (0.5 / D_MODEL), F32)

    return pl.pallas_call(
        body, name="fwd_tail", grid=(nt,),
        in_specs=[_rows(tm, D_MODEL), _rows(tm, D_FF_PAD), _rows(tm, PLE_DIM), _rows(tm, D_MODEL),
                  _const((N_CHIPS, D_FF_SLAB, D_MODEL)), _const((D_MODEL, D_MODEL)),
                  _const((N_CHIPS, PLE_DIM, PLE_DIM)), _const((1, D_MODEL))],
        out_specs=[_rows(tm, D_MODEL), _rows(tm, D_MODEL), _rows(tm, D_MODEL), _rows(tm, D_MODEL),
                   pl.BlockSpec((1, D_MODEL), lambda i: (0, 0)), pl.BlockSpec((1, 128), lambda i: (0, 0))],
        out_shape=[jax.ShapeDtypeStruct((t, D_MODEL), F32), jax.ShapeDtypeStruct((t, D_MODEL), BF),
                   jax.ShapeDtypeStruct((t, D_MODEL), BF), jax.ShapeDtypeStruct((t, D_MODEL), BF),
                   jax.ShapeDtypeStruct((1, D_MODEL), F32), jax.ShapeDtypeStruct((1, 128), F32)],
        scratch_shapes=[pltpu.VMEM((1, D_MODEL), F32)],
        compiler_params=_params(48),
    )(x1, act, p, target, wd4, wpg, wpp4, g_ple)


def _bwd_ffn_a(dx2, gp, up, wd4, fcw, fcb, tm, riders=()):
    t = dx2.shape[0]
    nt = t // tm
    nr = len(riders)

    def body(*refs):
        dx2_ref, gp_ref, gph_ref, up_ref, wd_ref, fcw_ref, fcb_ref = refs[0:7]
        dgp_ref, dup_ref, dfcw_ref, dfcb_ref = refs[7 + nr:11 + nr]
        cbuf, dbuf = refs[11 + 2 * nr:13 + 2 * nr]
        i = pl.program_id(0)
        if nr:
            _ride_scatter(i == 0, i == nt - 1, riders, refs[7:7 + nr], refs[11 + nr:11 + 2 * nr],
                          *refs[13 + 2 * nr:15 + 2 * nr])

        @pl.when(i == 0)
        def _():
            dbuf[tm:tm + 8, :] = jnp.zeros((8, D_FF_PAD), F32)
            dfcw_ref[...] = jnp.zeros_like(dfcw_ref)
            dfcb_ref[...] = jnp.zeros_like(dfcb_ref)

        not_first_tile = i < nt - 1
        dx2b = dx2_ref[...].astype(BF)
        for s in range(N_CHIPS):
            lo, hi = s * D_FF_SLAB, (s + 1) * D_FF_SLAB
            gps = gp_ref[:, lo:hi]
            cbuf[0:8, lo:hi] = jnp.where(not_first_tile, gph_ref[:, lo:hi], 0.0)
            cbuf[8:8 + tm, lo:hi] = gps
            cb = cbuf[:, lo:hi]
            g1 = _shift_down(cb, 1, tm)
            g2 = _shift_down(cb, 2, tm)
            w0, w1, w2 = fcw_ref[0:1, lo:hi], fcw_ref[1:2, lo:hi], fcw_ref[2:3, lo:hi]
            gate = w0 * g2 + w1 * g1 + w2 * gps + fcb_ref[:, lo:hi]
            sg = jax.nn.sigmoid(gate)
            dact = _mm_nt(dx2b, wd_ref[s])
            dup_ref[:, lo:hi] = (dact * (gate * sg)).astype(BF)
            dgate = (dact * up_ref[:, lo:hi]) * (sg * (1.0 + gate * (1.0 - sg)))
            dfcb_ref[:, lo:hi] += _colsum(dgate)
            dfcw_ref[0:1, lo:hi] += _colsum(dgate * g2)
            dfcw_ref[1:2, lo:hi] += _colsum(dgate * g1)
            dfcw_ref[2:3, lo:hi] += _colsum(dgate * gps)
            dbuf[0:tm, lo:hi] = dgate
            db = dbuf[:, lo:hi]
            dgp = w2 * dgate + w1 * _shift_up(db, 1, tm) + w0 * _shift_up(db, 2, tm)
            dgp_ref[:, lo:hi] = dgp.astype(BF)
        dbuf[tm:tm + 8, :] = dbuf[0:8, :]

    r_in, r_out, r_sems = _rider_specs(riders)
    res = pl.pallas_call(
        body, name="bwd_ffn_a", grid=(nt,),
        in_specs=[_rows(tm, D_MODEL, nt), _rows(tm, D_FF_PAD, nt), _halo(tm, D_FF_PAD, nt), _rows(tm, D_FF_PAD, nt),
                  _const((N_CHIPS, D_FF_SLAB, D_MODEL)), _const((3, D_FF_PAD)), _const((1, D_FF_PAD))] + r_in,
        out_specs=[_rows(tm, D_FF_PAD, nt), _rows(tm, D_FF_PAD, nt),
                   pl.BlockSpec((3, D_FF_PAD), lambda i: (0, 0)), pl.BlockSpec((1, D_FF_PAD), lambda i: (0, 0))] + r_in,
        out_shape=[jax.ShapeDtypeStruct((t, D_FF_PAD), BF), jax.ShapeDtypeStruct((t, D_FF_PAD), BF),
                   jax.ShapeDtypeStruct((3, D_FF_PAD), F32), jax.ShapeDtypeStruct((1, D_FF_PAD), F32)] + r_out,
        scratch_shapes=[pltpu.VMEM((tm + 8, D_FF_PAD), F32), pltpu.VMEM((tm + 8, D_FF_PAD), F32)] + r_sems,
        compiler_params=_params(56),
    )(dx2, gp, gp, up, wd4, fcw, fcb, *[g for g, _ in riders])
    return res[0], res[1], res[2], res[3], list(res[4:])


def _bwd_ffn_b(dgp, dup, dx2, x1, ya, wg4, wu4, wout, g_ffn, g_oa, tm):
    t = dx2.shape[0]
    nt = t // tm

    def body(dgp_ref, dup_ref, dx2_ref, x1_ref, ya_ref, wg_ref, wu_ref, wout_ref, gffn_ref, goa_ref,
             dx1_ref, dycn_ref, dya_ref, dgffn_ref, dgoa_ref):
        @pl.when(pl.program_id(0) == 0)
        def _():
            dgffn_ref[...] = jnp.zeros_like(dgffn_ref)
            dgoa_ref[...] = jnp.zeros_like(dgoa_ref)

        dh2 = jnp.zeros((tm, D_MODEL), F32)
        for s in range(N_CHIPS):
            lo, hi = s * D_FF_SLAB, (s + 1) * D_FF_SLAB
            dh2 = dh2 + _mm_nt(dgp_ref[:, lo:hi], wg_ref[s]) + _mm_nt(dup_ref[:, lo:hi], wu_ref[s])
        x1 = x1_ref[...]
        r2 = _rstd(x1)
        xh = x1 * r2
        dgffn_ref[...] += _colsum(dh2 * xh)
        dx1 = dx2_ref[...] + _norm_bwd(dh2, xh, r2, gffn_ref[...])
        dx1_ref[...] = dx1
        dy = _mm_nt(dx1.astype(BF), wout_ref[...])
        dycn_ref[...] = dy[:, 0:CONV_W]
        dyan = dy[:, CONV_W:D_MODEL]
        yat = ya_ref[...]
        ra = _rstd(yat)
        yah = yat * ra
        dgoa_ref[...] += _colsum(dyan * yah)
        dya_ref[...] = _norm_bwd(dyan, yah, ra, goa_ref[...])

    return pl.pallas_call(
        body, name="bwd_ffn_b", grid=(nt,),
        in_specs=[_rows(tm, D_FF_PAD), _rows(tm, D_FF_PAD), _rows(tm, D_MODEL), _rows(tm, D_MODEL),
                  _rows(tm, ATTN_W), _const((N_CHIPS, D_MODEL, D_FF_SLAB)), _const((N_CHIPS, D_MODEL, D_FF_SLAB)),
                  _const((D_MODEL, D_MODEL)), _const((1, D_MODEL)), _const((1, ATTN_W))],
        out_specs=[_rows(tm, D_MODEL), _rows(tm, CONV_W), _rows(tm, ATTN_W),
                   pl.BlockSpec((1, D_MODEL), lambda i: (0, 0)), pl.BlockSpec((1, ATTN_W), lambda i: (0, 0))],
        out_shape=[jax.ShapeDtypeStruct((t, D_MODEL), F32), jax.ShapeDtypeStruct((t, CONV_W), F32),
                   jax.ShapeDtypeStruct((t, ATTN_W), F32),
                   jax.ShapeDtypeStruct((1, D_MODEL), F32), jax.ShapeDtypeStruct((1, ATTN_W), F32)],
        compiler_params=_params(48),
    )(dgp, dup, dx2, x1, ya, wg4, wu4, wout, g_ffn, g_oa)


def _bwd_mix(x, dx1, zbcx, qkv, dycn, dq, dk, dv, win4, conv_w, conv_b, g_oc, g_mix, gm, gq8, gk8, tm):
    t = x.shape[0]
    nt = t // tm

    def body(x_ref, dx1_ref, z_ref, zh_ref, qkv_ref, dycn_ref, dq_ref, dk_ref, dv_ref, w_ref, cw_ref, cb_ref,
             goc_ref, g_ref, gm_ref, gq_ref, gk_ref,
             gx_ref, h1_ref, dz_ref, dcw_ref, dcb_ref, dgoc_ref, dg_ref, dgq_ref, dgk_ref, ubuf, dbuf):
        i = pl.program_id(0)

        @pl.when(i == 0)
        def _():
            dbuf[tm:tm + 8, :] = jnp.zeros((8, CONV_W), F32)
            dcw_ref[...] = jnp.zeros_like(dcw_ref)
            dcb_ref[...] = jnp.zeros_like(dcb_ref)
            dgoc_ref[...] = jnp.zeros_like(dgoc_ref)
            dg_ref[...] = jnp.zeros_like(dg_ref)
            dgq_ref[...] = jnp.zeros_like(dgq_ref)
            dgk_ref[...] = jnp.zeros_like(dgk_ref)

        not_first_tile = i < nt - 1
        zb = z_ref[:, 0:512]
        zc = z_ref[:, 512:1024]
        zx = z_ref[:, 1024:1536]
        u = zc * zx
        ubuf[0:8, :] = jnp.where(not_first_tile, zh_ref[:, 512:1024] * zh_ref[:, 1024:1536], 0.0)
        ubuf[8:8 + tm, :] = u
        ub = ubuf[...]
        u1 = _shift_down(ub, 1, tm)
        u2 = _shift_down(ub, 2, tm)
        w0, w1, w2 = cw_ref[0:1, :], cw_ref[1:2, :], cw_ref[2:3, :]
        cv = w0 * u2 + w1 * u1 + w2 * u + cb_ref[...]
        yc = zb * cv
        rc = _rstd(yc)
        ych = yc * rc
        dycn = dycn_ref[...]
        dgoc_ref[...] += _colsum(dycn * ych)
        dyc = _norm_bwd(dycn, ych, rc, goc_ref[...])
        dcv = dyc * zb
        dcb_ref[...] += _colsum(dcv)
        dcw_ref[0:1, :] += _colsum(dcv * u2)
        dcw_ref[1:2, :] += _colsum(dcv * u1)
        dcw_ref[2:3, :] += _colsum(dcv * u)
        dbuf[0:tm, :] = dcv
        db = dbuf[...]
        du = w2 * dcv + w1 * _shift_up(db, 1, tm) + w0 * _shift_up(db, 2, tm)
        dbuf[tm:tm + 8, :] = dbuf[0:8, :]
        dz_ref[:, 0:512] = (dyc * cv).astype(BF)
        dz_ref[:, 512:1024] = (du * zx).astype(BF)
        dz_ref[:, 1024:1536] = (du * zc).astype(BF)
        for z0, d_ref, gg_ref, acc_ref, sc in ((0, dq_ref, gq_ref, dgq_ref, HEAD_DIM ** -0.5),
                                               (512, dk_ref, gk_ref, dgk_ref, 1.0)):
            z = qkv_ref[:, z0:z0 + 512]
            rr = lax.rsqrt(_head_mean(z * z, gm_ref) + EPS)
            zh = z * rr
            dn = d_ref[...] * sc
            acc_ref[...] += _colsum(dn * zh)
            dzh = dn * gg_ref[...]
            dz_ref[:, 1536 + z0:1536 + z0 + 512] = (rr * (dzh - zh * _head_mean(dzh * zh, gm_ref))).astype(BF)
        dz_ref[:, 2560:3072] = dv_ref[...].astype(BF)
        dh1 = jnp.zeros((tm, D_MODEL), F32)
        for s in range(N_CHIPS):
            dh1 = dh1 + _mm_nt(dz_ref[:, s * IN_SLAB:(s + 1) * IN_SLAB], w_ref[s])
        xt = x_ref[...]
        r1 = _rstd(xt)
        xh = xt * r1
        h1_ref[...] = (xh * g_ref[...]).astype(BF)
        dg_ref[...] += _colsum(dh1 * xh)
        gx_ref[...] = dx1_ref[...] + _norm_bwd(dh1, xh, r1, g_ref[...])

    def acc(width, rows=1):
        return pl.BlockSpec((rows, width), lambda i: (0, 0))

    return pl.pallas_call(
        body, name="bwd_mix", grid=(nt,),
        in_specs=[_rows(tm, D_MODEL, nt), _rows(tm, D_MODEL, nt), _rows(tm, 1536, nt), _halo(tm, 1536, nt),
                  _rows(tm, 1536, nt), _rows(tm, CONV_W, nt), _rows(tm, ATTN_W, nt), _rows(tm, ATTN_W, nt),
                  _rows(tm, ATTN_W, nt), _const((N_CHIPS, D_MODEL, IN_SLAB)),
                  _const((3, CONV_W)), _const((1, CONV_W)), _const((1, CONV_W)), _const((1, D_MODEL)),
                  _const((ATTN_W, ATTN_W)), _const((1, ATTN_W)), _const((1, ATTN_W))],
        out_specs=[_rows(tm, D_MODEL, nt), _rows(tm, D_MODEL, nt), _rows(tm, 3072, nt),
                   acc(CONV_W, 3), acc(CONV_W), acc(CONV_W), acc(D_MODEL), acc(ATTN_W), acc(ATTN_W)],
        out_shape=[jax.ShapeDtypeStruct((t, D_MODEL), F32), jax.ShapeDtypeStruct((t, D_MODEL), BF),
                   jax.ShapeDtypeStruct((t, 3072), BF), jax.ShapeDtypeStruct((3, CONV_W), F32),
                   jax.ShapeDtypeStruct((1, CONV_W), F32), jax.ShapeDtypeStruct((1, CONV_W), F32),
                   jax.ShapeDtypeStruct((1, D_MODEL), F32), jax.ShapeDtypeStruct((1, ATTN_W), F32),
                   jax.ShapeDtypeStruct((1, ATTN_W), F32)],
        scratch_shapes=[pltpu.VMEM((tm + 8, CONV_W), F32), pltpu.VMEM((tm + 8, CONV_W), F32)],
        compiler_params=_params(56),
    )(x, dx1, zbcx, zbcx, qkv, dycn, dq, dk, dv, win4, conv_w, conv_b, g_oc, g_mix, gm, gq8, gk8)


def _wgrad(a, b, tn, tt, name):
    t, k = a.shape
    n = b.shape[1]
    nt = t // tt

    def body(a_ref, b_ref, o_ref, ob_ref):
        @pl.when(pl.program_id(1) == 0)
        def _():
            o_ref[...] = jnp.zeros_like(o_ref)

        o_ref[...] += _mm_tn(a_ref[...].astype(BF), b_ref[...].astype(BF))

        @pl.when(pl.program_id(1) == nt - 1)
        def _():
            ob_ref[...] = o_ref[...].astype(BF)

    spec = pl.BlockSpec((k, tn), lambda j, i: (0, j))
    return pl.pallas_call(
        body, name=name, grid=(n // tn, nt),
        in_specs=[pl.BlockSpec((tt, k), lambda j, i: (i, 0)), pl.BlockSpec((tt, tn), lambda j, i: (i, j))],
        out_specs=[spec, spec],
        out_shape=[jax.ShapeDtypeStruct((k, n), F32), jax.ShapeDtypeStruct((k, n), BF)],
        compiler_params=_params(48, 2),
    )(a, b)


def _gather_weights(shards, pack):
    nw = len(shards)

    def body(*refs):
        ins = refs[:nw]
        pack_ref = refs[nw]
        outs = refs[nw + 1:2 * nw + 1]
        pack_out = refs[2 * nw + 1]
        send_sems, recv_sems, local_sems = refs[2 * nw + 2:]
        x, y, c = _place()
        me = 2 * x + y
        local, remote = [], []

        def sem(w, j):
            return w * 6 + j

        def push(src, dst, w, j, to):
            return pltpu.make_async_remote_copy(src_ref=src, dst_ref=dst, send_sem=send_sems.at[sem(w, j)],
                                                recv_sem=recv_sems.at[sem(w, j)], device_id=to, device_id_type=MESH)

        def half_rows(w, h):
            half = ins[w].shape[0] // 2
            return pl.ds(pl.multiple_of(h * half, 16), half)

        for w in range(nw):
            local.append(pltpu.make_async_copy(ins[w], outs[w].at[me], local_sems.at[w]))
            for k in (1, 2, 3):
                px, py = _chip_peer(x, y, k)
                mine = half_rows(w, c)
                remote.append(push(ins[w].at[mine], outs[w].at[me, mine], w, k - 1, (px, py, c)))
        local.append(pltpu.make_async_copy(pack_ref, pack_out.at[me], local_sems.at[nw]))
        for k in (1, 2, 3):
            px, py = _chip_peer(x, y, k)
            remote.append(push(pack_ref, pack_out.at[me], nw, k - 1, (px, py, c)))
        for cp in local + remote:
            cp.start()
        for w in range(nw):
            for k in (1, 2, 3):
                landed = outs[w].at[me ^ k, half_rows(w, c)]
                push(landed, landed, w, k - 1, (x, y, c)).wait_recv()
                fw = push(landed, landed, w, 2 + k, (x, y, 1 - c))
                fw.start()
                remote.append(fw)
        for k in (1, 2, 3):
            landed = pack_out.at[me ^ k]
            push(landed, landed, nw, k - 1, (x, y, c)).wait_recv()
        for w in range(nw):
            for k in (1, 2, 3):
                landed = outs[w].at[me ^ k, half_rows(w, 1 - c)]
                push(landed, landed, w, 2 + k, (x, y, c)).wait_recv()
        for cp in remote:
            cp.wait_send()
        for cp in local:
            cp.wait()

    any_spec = pl.BlockSpec(memory_space=pl.ANY)
    out_shape = [jax.ShapeDtypeStruct((N_CHIPS,) + s.shape, s.dtype) for s in shards]
    out_shape.append(jax.ShapeDtypeStruct((N_CHIPS,) + pack.shape, pack.dtype))
    return pl.pallas_call(
        body, name="gather_weights",
        in_specs=[any_spec] * (nw + 1), out_specs=[any_spec] * (nw + 1), out_shape=out_shape,
        scratch_shapes=[pltpu.SemaphoreType.DMA(((nw + 1) * 6,)), pltpu.SemaphoreType.DMA(((nw + 1) * 6,)),
                        pltpu.SemaphoreType.DMA((nw + 1,))],
    )(*shards, pack)


def _adamw(w, g, m, v):
    m = ADAM_B1 * m + (1.0 - ADAM_B1) * g
    v = ADAM_B2 * v + (1.0 - ADAM_B2) * (g * g)
    m_hat = m / (1.0 - ADAM_B1 ** ADAM_STEP)
    v_hat = v / (1.0 - ADAM_B2 ** ADAM_STEP)
    delta = -ADAM_LR * (m_hat / (jnp.sqrt(v_hat) + ADAM_EPS) + ADAM_WD * w)
    return delta, m, v


def _scatter_alone(g16, col_sharded, name):
    def body(g_ref, slots_ref, send_sems, recv_sems):
        copies = _scatter_copies(g_ref, slots_ref, send_sems, recv_sems, 0, col_sharded)
        for cp in copies:
            cp.start()
        for cp in copies:
            cp.wait()

    any_spec = pl.BlockSpec(memory_space=pl.ANY)
    return pl.pallas_call(
        body, name=name, in_specs=[any_spec], out_specs=any_spec,
        out_shape=jax.ShapeDtypeStruct((7,) + _piece_shape(g16.shape, col_sharded), BF),
        scratch_shapes=[pltpu.SemaphoreType.DMA((7,)), pltpu.SemaphoreType.DMA((7,))],
    )(g16)


def _finish_adamw(grad, slots, w, m, v, col_sharded, name):
    r, cw = _piece_shape(grad.shape, col_sharded)
    vr, vc = w.shape
    chunk = 64
    assert r % chunk == 0 and vr % chunk == 0 and vr <= 2 * r and vc <= cw

    def body(g_hbm, slots_ref, w_ref, m_ref, v_ref, go_ref, do_ref, mo_ref, vo_ref, own, full, lsem, c_send, c_recv):
        x, y, c = _place()
        mine_piece = _piece_window(col_sharded, r, cw, 2 * x + y, c)
        cp = pltpu.make_async_copy(g_hbm.at[mine_piece], own, lsem)
        cp.start()
        cp.wait()
        mine = pl.multiple_of(c * r, 8)

        def add(j, carry):
            rows = pl.ds(pl.multiple_of(j * chunk, 8), chunk)
            tot = own[rows, :]
            for k in range(7):
                tot = tot + slots_ref[k, rows, :].astype(F32)
            full[pl.ds(mine + pl.multiple_of(j * chunk, 8), chunk), :] = tot
            return carry

        lax.fori_loop(0, r // chunk, add, 0)
        half = full.at[pl.ds(mine, r), :]
        swap = pltpu.make_async_remote_copy(src_ref=half, dst_ref=half, send_sem=c_send, recv_sem=c_recv,
                                            device_id=(x, y, 1 - c), device_id_type=MESH)
        swap.start()
        swap.wait()

        def update(j, carry):
            rows = pl.ds(pl.multiple_of(j * chunk, 8), chunk)
            g = full[rows, 0:vc]
            delta, mn, vn = _adamw(w_ref[rows, :], g, m_ref[rows, :], v_ref[rows, :])
            go_ref[rows, :] = g
            do_ref[rows, :] = delta
            mo_ref[rows, :] = mn
            vo_ref[rows, :] = vn
            return carry

        lax.fori_loop(0, vr // chunk, update, 0)

    any_spec = pl.BlockSpec(memory_space=pl.ANY)
    vmem = pl.BlockSpec(memory_space=pltpu.VMEM)
    shard = jax.ShapeDtypeStruct((vr, vc), F32)
    return pl.pallas_call(
        body, name=name,
        in_specs=[any_spec, vmem, vmem, vmem, vmem], out_specs=[vmem] * 4, out_shape=[shard] * 4,
        scratch_shapes=[pltpu.VMEM((r, cw), F32), pltpu.VMEM((2 * r, cw), F32), pltpu.SemaphoreType.DMA,
                        pltpu.SemaphoreType.DMA, pltpu.SemaphoreType.DMA],
        compiler_params=pltpu.CompilerParams(vmem_limit_bytes=48 * MIB),
    )(grad, slots, w, m, v)


def _allreduce_small(pack):
    rows = pack.shape[0]

    def body(p_ref, o_ref, slots, send_sems, recv_sems):
        x, y, c = _place()
        me = 4 * x + 2 * y + c
        slots[me] = p_ref[...]
        sends = []
        for k in range(1, 8):
            cp = pltpu.make_async_remote_copy(
                src_ref=p_ref, dst_ref=slots.at[me], send_sem=send_sems.at[k - 1], recv_sem=recv_sems.at[k - 1],
                device_id=(x ^ (k >> 2), y ^ ((k >> 1) & 1), c ^ (k & 1)), device_id_type=MESH)
            cp.start()
            sends.append(cp)
        for cp in sends:
            cp.wait()
        tot = slots[0]
        for j in range(1, 8):
            tot = tot + slots[j]
        o_ref[...] = tot

    vmem = pl.BlockSpec(memory_space=pltpu.VMEM)
    return pl.pallas_call(
        body, name="allreduce_small", in_specs=[vmem], out_specs=vmem,
        out_shape=jax.ShapeDtypeStruct(pack.shape, F32),
        scratch_shapes=[pltpu.VMEM((8, rows, D_MODEL), F32), pltpu.SemaphoreType.DMA((7,)),
                        pltpu.SemaphoreType.DMA((7,))],
    )(pack)


def _adamw_small(ws, gs, ms, vs):
    n = len(ws)

    def body(*refs):
        w_refs, g_refs, m_refs, v_refs = refs[0:n], refs[n:2 * n], refs[2 * n:3 * n], refs[3 * n:4 * n]
        d_refs, mo_refs, vo_refs = refs[4 * n:5 * n], refs[5 * n:6 * n], refs[6 * n:7 * n]
        for j in range(n):
            delta, mn, vn = _adamw(w_refs[j][...], g_refs[j][...], m_refs[j][...], v_refs[j][...])
            d_refs[j][...] = delta
            mo_refs[j][...] = mn
            vo_refs[j][...] = vn

    vmem = pl.BlockSpec(memory_space=pltpu.VMEM)
    shapes = [jax.ShapeDtypeStruct(w.shape, F32) for w in ws]
    outs = pl.pallas_call(
        body, name="adamw_small", in_specs=[vmem] * (4 * n), out_specs=[vmem] * (3 * n), out_shape=shapes * 3,
    )(*ws, *gs, *ms, *vs)
    return outs[0:n], outs[n:2 * n], outs[2 * n:3 * n]


def _pad_slab(a, axis):
    pad = [(0, 0)] * a.ndim
    pad[axis] = (0, D_FF_SLAB - D_FF_SHARD)
    return jnp.pad(a, pad)


def _local_step(x, p, target, wts, late=None):
    (win4, wout, wg4, wu4, wd4, wpg, wpp4, conv_w, fcw, g_mix, conv_b, gq, gk, g_oc, g_oa, g_ffn, fcb, g_ple) = wts
    comm = late is not None
    gm = jnp.kron(jnp.eye(N_HEADS, dtype=F32), jnp.full((HEAD_DIM, HEAD_DIM), 1.0 / HEAD_DIM, F32)).astype(BF)
    gq8, gk8 = jnp.tile(gq, (1, N_HEADS)), jnp.tile(gk, (1, N_HEADS))
    mb, mbk = _mask_tables()
    zbcx, qkv, ycn, qkn = _fwd_mix(x, g_mix, win4, conv_w, conv_b, g_oc, gm, gq8, gk8, 512)
    ya, lse, gathered = _attn_fwd(qkn, qkv, mb, late if comm else ())
    if comm:
        wout, wg4, wu4, wd4, wpg, wpp4 = gathered
        wout, wpg = wout.reshape(D_MODEL, D_MODEL), wpg.reshape(D_MODEL, D_MODEL)
    x1, gp, up, act, ycat, h2 = _fwd_ffn(x, ycn, ya, wout, wg4, wu4, g_oa, g_ffn, fcw, fcb, 256)
    dx2, h3, ds, dpp, dg_ple, loss = _fwd_tail(x1, act, p, target, wd4, wpg, wpp4, g_ple, 256)
    big, big16, slots = {}, {}, {}

    def wgrad(name, a, b, tn):
        big[name], big16[name] = _wgrad(a, b, tn, 512, "wgrad_" + name)
        return (big16[name], _COL_SHARDED[name])

    riders = [wgrad("w_down", act, dx2, 512), wgrad("w_ple_gate", h3, ds, 1024), wgrad("w_ple_proj", p, dpp, 1024)]
    dgp, dup, dfcw, dfcb, got = _bwd_ffn_a(dx2, gp, up, wd4, fcw, fcb, 256, riders if comm else ())
    slots.update(zip(("w_down", "w_ple_gate", "w_ple_proj"), got))
    riders = [wgrad("w_gate", h2, dgp, 1536), wgrad("w_up", h2, dup, 1536)]
    dx1, dycn, dya, dg_ffn, dg_oa = _bwd_ffn_b(dgp, dup, dx2, x1, ya, wg4, wu4, wout, g_ffn, g_oa, 256)
    riders.append(wgrad("w_out", ycat, dx1, 1024))
    dq, dk, dv, got = _attn_bwd(qkn, qkv, ya, lse, dya, mb, mbk, riders if comm else ())
    slots.update(zip(("w_gate", "w_up", "w_out"), got))
    grad_x, h1, dz, dcw, dcb, dg_oc, dg_mix, dgq8, dgk8 = _bwd_mix(
        x, dx1, zbcx, qkv, dycn, dq, dk, dv, win4, conv_w, conv_b, g_oc, g_mix, gm, gq8, gk8, 256)
    g16, cs = wgrad("w_in", h1, dz, 1536)
    if comm:
        slots["w_in"] = _scatter_alone(g16, cs, "scatter_w_in")
    dgq = dgq8.reshape(N_HEADS, HEAD_DIM).sum(axis=0, keepdims=True)
    dgk = dgk8.reshape(N_HEADS, HEAD_DIM).sum(axis=0, keepdims=True)
    small = dict(g_mix=dg_mix, conv_w=dcw, conv_b=dcb, q_norm_g=dgq, k_norm_g=dgk, g_out_conv=dg_oc,
                 g_out_attn=dg_oa, g_ffn=dg_ffn, ffn_conv_w=dfcw, ffn_conv_b=dfcb, g_ple=dg_ple)
    return loss[0, 0], grad_x, big, slots, small


_SMALL_ROWS = 24


def _pack_small(s):
    z64 = jnp.zeros((1, 1024 - 512 - 128), F32)
    rows = [s["g_mix"], s["g_ffn"], s["g_ple"],
            jnp.concatenate([s["conv_b"], s["g_out_conv"]], axis=1),
            jnp.concatenate([s["g_out_attn"], s["q_norm_g"], s["k_norm_g"], z64], axis=1),
            jnp.pad(s["conv_w"], ((0, 0), (0, 512))),
            s["ffn_conv_b"].reshape(3, 1024),
            s["ffn_conv_w"].reshape(9, 1024),
            jnp.zeros((_SMALL_ROWS - 20, 1024), F32)]
    return jnp.concatenate(rows, axis=0)


def _unpack_small(t):
    return dict(g_mix=t[0:1], g_ffn=t[1:2], g_ple=t[2:3], conv_b=t[3:4, 0:512], g_out_conv=t[3:4, 512:1024],
                g_out_attn=t[4:5, 0:512], q_norm_g=t[4:5, 512:576], k_norm_g=t[4:5, 576:640],
                conv_w=t[5:8, 0:512], ffn_conv_b=t[8:11].reshape(1, D_FF_PAD), ffn_conv_w=t[11:20].reshape(3, D_FF_PAD))


def _unpad_ff(a):
    r = a.shape[0]
    return a.reshape(r, N_CHIPS, D_FF_SLAB)[:, :, :D_FF_SHARD].reshape(r, N_CHIPS * D_FF_SHARD)


_BIG = ("w_in", "w_out", "w_gate", "w_up", "w_down", "w_ple_gate", "w_ple_proj")
_COL_SHARDED = dict(w_in=True, w_out=False, w_gate=True, w_up=True, w_down=False, w_ple_gate=False, w_ple_proj=True)
_WEIGHTS = ("g_mix", "w_in", "conv_w", "conv_b", "q_norm_g", "k_norm_g", "g_out_conv", "g_out_attn", "w_out",
            "g_ffn", "w_gate", "w_up", "ffn_conv_w", "ffn_conv_b", "w_down", "g_ple", "w_ple_gate", "w_ple_proj")


def kernel(x, p, g_mix, w_in, conv_w, conv_b, q_norm_g, k_norm_g, g_out_conv, g_out_attn, w_out, g_ffn, w_gate, w_up, ffn_conv_w, ffn_conv_b, w_down, g_ple, w_ple_gate, w_ple_proj, loss_target, m_g_mix, m_w_in, m_conv_w, m_conv_b, m_q_norm_g, m_k_norm_g, m_g_out_conv, m_g_out_attn, m_w_out, m_g_ffn, m_w_gate, m_w_up, m_ffn_conv_w, m_ffn_conv_b, m_w_down, m_g_ple, m_w_ple_gate, m_w_ple_proj, v_g_mix, v_w_in, v_conv_w, v_conv_b, v_q_norm_g, v_k_norm_g, v_g_out_conv, v_g_out_attn, v_w_out, v_g_ffn, v_w_gate, v_w_up, v_ffn_conv_w, v_ffn_conv_b, v_w_down, v_g_ple, v_w_ple_gate, v_w_ple_proj):
    w = dict(g_mix=g_mix, w_in=w_in, conv_w=conv_w, conv_b=conv_b, q_norm_g=q_norm_g, k_norm_g=k_norm_g,
             g_out_conv=g_out_conv, g_out_attn=g_out_attn, w_out=w_out, g_ffn=g_ffn, w_gate=w_gate, w_up=w_up,
             ffn_conv_w=ffn_conv_w, ffn_conv_b=ffn_conv_b, w_down=w_down, g_ple=g_ple, w_ple_gate=w_ple_gate,
             w_ple_proj=w_ple_proj)
    m = dict(g_mix=m_g_mix, w_in=m_w_in, conv_w=m_conv_w, conv_b=m_conv_b, q_norm_g=m_q_norm_g, k_norm_g=m_k_norm_g,
             g_out_conv=m_g_out_conv, g_out_attn=m_g_out_attn, w_out=m_w_out, g_ffn=m_g_ffn, w_gate=m_w_gate,
             w_up=m_w_up, ffn_conv_w=m_ffn_conv_w, ffn_conv_b=m_ffn_conv_b, w_down=m_w_down, g_ple=m_g_ple,
             w_ple_gate=m_w_ple_gate, w_ple_proj=m_w_ple_proj)
    v = dict(g_mix=v_g_mix, w_in=v_w_in, conv_w=v_conv_w, conv_b=v_conv_b, q_norm_g=v_q_norm_g, k_norm_g=v_k_norm_g,
             g_out_conv=v_g_out_conv, g_out_attn=v_g_out_attn, w_out=v_w_out, g_ffn=v_g_ffn, w_gate=v_w_gate,
             w_up=v_w_up, ffn_conv_w=v_ffn_conv_w, ffn_conv_b=v_ffn_conv_b, w_down=v_w_down, g_ple=v_g_ple,
             w_ple_gate=v_w_ple_gate, w_ple_proj=v_w_ple_proj)
    mats = [k for k, a in w.items() if a.ndim == 3]
    w = {k: (a[0] if k in mats else a) for k, a in w.items()}
    m = {k: (a[0] if k in mats else a) for k, a in m.items()}
    v = {k: (a[0] if k in mats else a) for k, a in v.items()}
    chip = 2 * lax.axis_index("x") + lax.axis_index("y")

    late = [w["w_out"].astype(BF), _pad_slab(w["w_gate"], 1).astype(BF), _pad_slab(w["w_up"], 1).astype(BF),
            _pad_slab(w["w_down"], 0).astype(BF), w["w_ple_gate"].astype(BF), w["w_ple_proj"].astype(BF)]
    pack = jnp.pad(jnp.concatenate([w["conv_w"], _pad_slab(w["ffn_conv_w"], 1)], axis=1), ((0, 5), (0, 128)))
    win4, pack4 = _gather_weights([w["w_in"].astype(BF)], pack)
    conv_w_full = pack4[:, 0:3, 0:128].transpose(1, 0, 2).reshape(3, CONV_W)
    fcw_full = pack4[:, 0:3, 128:128 + D_FF_SLAB].transpose(1, 0, 2).reshape(3, D_FF_PAD)
    fcb_pad = _pad_slab(w["ffn_conv_b"].reshape(N_CHIPS, D_FF_SHARD), 1).reshape(1, D_FF_PAD)
    wts = (win4, None, None, None, None, None, None, conv_w_full, fcw_full, w["g_mix"], w["conv_b"], w["q_norm_g"],
           w["k_norm_g"], w["g_out_conv"], w["g_out_attn"], w["g_ffn"], fcb_pad, w["g_ple"])

    loss, grad_x, big, slots, small = _local_step(x[0], p[0, 0], loss_target[0], wts, late)
    loss = lax.psum(loss, ("x", "y", "c"))

    grads, deltas, new_m, new_v = {}, {}, {}, {}
    for name in _BIG:
        grads[name], deltas[name], new_m[name], new_v[name] = _finish_adamw(
            big[name], slots[name], w[name], m[name], v[name], _COL_SHARDED[name], "finish_" + name)
    tot = _unpack_small(_allreduce_small(_pack_small(small)))
    tot["conv_w"] = lax.dynamic_slice_in_dim(tot["conv_w"], chip * 128, 128, axis=1)
    tot["ffn_conv_w"] = lax.dynamic_slice_in_dim(tot["ffn_conv_w"], chip * D_FF_SLAB, D_FF_SLAB, axis=1)[:, :D_FF_SHARD]
    tot["ffn_conv_b"] = _unpad_ff(tot["ffn_conv_b"])
    names = [n for n in _WEIGHTS if n not in _BIG]
    d_s, m_s, v_s = _adamw_small([w[n] for n in names], [tot[n] for n in names], [m[n] for n in names],
                                 [v[n] for n in names])
    for j, n in enumerate(names):
        grads[n], deltas[n], new_m[n], new_v[n] = tot[n], d_s[j], m_s[j], v_s[j]

    out = [loss, grad_x[None]]
    for group in (grads, deltas, new_m, new_v):
        out += [group[n][None] if n in mats else group[n] for n in _WEIGHTS]
    return tuple(out)
```

```python
import jax
import jax.numpy as jnp
from jax import lax
from jax.experimental import pallas as pl
from jax.experimental.pallas import tpu as pltpu

D_MODEL = 1024
CONV_W = 512
N_HEADS = 8
HEAD_DIM = 64
ATTN_W = 512
D_FF_SHARD = 704
D_FF_SLAB = 768
D_FF_PAD = 4 * D_FF_SLAB
IN_SLAB = 768
PLE_DIM = 256
N_CHIPS = 4
QBLK = 128
DILATIONS = (1, 4, 16)
EPS = 1e-6
NEG = -1e30
MESH = pl.DeviceIdType.MESH

ADAM_LR = 0.001
ADAM_B1 = 0.9
ADAM_B2 = 0.999
ADAM_EPS = 1e-08
ADAM_WD = 0.01
ADAM_STEP = 10

BF = jnp.bfloat16
F32 = jnp.float32
MIB = 1024 * 1024


def _mm(a, b):
    return jnp.dot(a, b, preferred_element_type=F32)


def _mm_nt(a, b):
    return lax.dot_general(a, b, (((1,), (1,)), ((), ())), preferred_element_type=F32)


def _mm_tn(a, b):
    return lax.dot_general(a, b, (((0,), (0,)), ((), ())), preferred_element_type=F32)


def _rstd(a):
    return lax.rsqrt(jnp.mean(a * a, axis=-1, keepdims=True) + EPS)


def _norm_bwd(dy, xh, r, g):
    dxh = dy * g
    return r * (dxh - xh * jnp.mean(dxh * xh, axis=-1, keepdims=True))


def _colsum(a):
    return jnp.sum(a, axis=0, keepdims=True)


def _head_mean(a, gm_ref):
    hi = a.astype(BF)
    lo = (a - hi.astype(F32)).astype(BF)
    return _mm(hi, gm_ref[...]) + _mm(lo, gm_ref[...])


def _shift_down(buf, k, tm):
    return pltpu.roll(buf, k, axis=0)[8:8 + tm]


def _shift_up(buf, k, tm):
    return pltpu.roll(buf, tm + 8 - k, axis=0)[0:tm]


def _params(vmem_mib, n_grid=1):
    return pltpu.CompilerParams(dimension_semantics=("arbitrary",) * n_grid, vmem_limit_bytes=vmem_mib * MIB)


def _const(shape):
    n = len(shape)
    return pl.BlockSpec(shape, lambda *_: (0,) * n, pipeline_mode=pl.Buffered(1))


def _rows(tm, width, rev_of=None):
    if rev_of is None:
        return pl.BlockSpec((tm, width), lambda i: (i, 0))
    return pl.BlockSpec((tm, width), lambda i: (rev_of - 1 - i, 0))


def _halo(tm, width, nt):
    return pl.BlockSpec((8, width), lambda i: (jnp.maximum((nt - 1 - i) * (tm // 8) - 1, 0), 0))


def _fwd_mix(x, g_mix, win4, conv_w, conv_b, g_oc, gm, gq8, gk8, tm):
    t = x.shape[0]
    nt = t // tm

    def body(x_ref, g_ref, w_ref, cw_ref, cb_ref, goc_ref, gm_ref, gq_ref, gk_ref,
             zbcx_ref, qkv_ref, ycn_ref, qkn_ref, ubuf):
        @pl.when(pl.program_id(0) == 0)
        def _():
            ubuf[0:8, :] = jnp.zeros((8, CONV_W), F32)

        xt = x_ref[...]
        h = ((xt * _rstd(xt)) * g_ref[...]).astype(BF)
        zbcx_ref[:, 0:IN_SLAB] = _mm(h, w_ref[0])
        zbcx_ref[:, IN_SLAB:2 * IN_SLAB] = _mm(h, w_ref[1])
        qkv_ref[:, 0:IN_SLAB] = _mm(h, w_ref[2])
        qkv_ref[:, IN_SLAB:2 * IN_SLAB] = _mm(h, w_ref[3])
        u = zbcx_ref[:, 512:1024] * zbcx_ref[:, 1024:1536]
        ubuf[8:8 + tm, :] = u
        ub = ubuf[...]
        cv = (cw_ref[0:1, :] * _shift_down(ub, 2, tm) + cw_ref[1:2, :] * _shift_down(ub, 1, tm)
              + cw_ref[2:3, :] * u + cb_ref[...])
        ubuf[0:8, :] = ubuf[tm:tm + 8, :]
        yc = zbcx_ref[:, 0:512] * cv
        ycn_ref[...] = ((yc * _rstd(yc)) * goc_ref[...]).astype(BF)
        zq = qkv_ref[:, 0:512]
        zk = qkv_ref[:, 512:1024]
        rq = lax.rsqrt(_head_mean(zq * zq, gm_ref) + EPS)
        rk = lax.rsqrt(_head_mean(zk * zk, gm_ref) + EPS)
        qkn_ref[:, 0:512] = ((zq * rq) * gq_ref[...]) * (HEAD_DIM ** -0.5)
        qkn_ref[:, 512:1024] = (zk * rk) * gk_ref[...]

    return pl.pallas_call(
        body, name="fwd_mix", grid=(nt,),
        in_specs=[_rows(tm, D_MODEL), _const((1, D_MODEL)), _const((N_CHIPS, D_MODEL, IN_SLAB)),
                  _const((3, CONV_W)), _const((1, CONV_W)), _const((1, CONV_W)), _const((ATTN_W, ATTN_W)),
                  _const((1, ATTN_W)), _const((1, ATTN_W))],
        out_specs=[_rows(tm, 1536), _rows(tm, 1536), _rows(tm, CONV_W), _rows(tm, 1024)],
        out_shape=[jax.ShapeDtypeStruct((t, 1536), F32), jax.ShapeDtypeStruct((t, 1536), F32),
                   jax.ShapeDtypeStruct((t, CONV_W), BF), jax.ShapeDtypeStruct((t, 1024), F32)],
        scratch_shapes=[pltpu.VMEM((tm + 8, CONV_W), F32)],
        compiler_params=_params(48),
    )(x, g_mix, win4, conv_w, conv_b, g_oc, gm, gq8, gk8)


def _place():
    x, y, c = lax.axis_index("x"), lax.axis_index("y"), lax.axis_index("c")
    return x, y, c


def _chip_peer(x, y, k):
    return x ^ (k >> 1), y ^ (k & 1)


def _piece_shape(grad_shape, col_sharded):
    kk, nn = grad_shape
    return (kk // 2, nn // N_CHIPS) if col_sharded else (kk // (2 * N_CHIPS), nn)


def _piece_window(col_sharded, r, cw, s, h):
    if col_sharded:
        return (pl.ds(pl.multiple_of(h * r, 16), r), pl.ds(pl.multiple_of(s * cw, 128), cw))
    return (pl.ds(pl.multiple_of((2 * s + h) * r, 16), r), slice(None))


def _scatter_copies(g_ref, slots_ref, send_sems, recv_sems, base, col_sharded):
    x, y, c = _place()
    r, cw = slots_ref.shape[1:]
    copies = []
    for k in range(1, 8):
        tx, ty, tc = x ^ (k >> 2), y ^ ((k >> 1) & 1), c ^ (k & 1)
        copies.append(pltpu.make_async_remote_copy(
            src_ref=g_ref.at[_piece_window(col_sharded, r, cw, 2 * tx + ty, tc)], dst_ref=slots_ref.at[k - 1],
            send_sem=send_sems.at[base + k - 1], recv_sem=recv_sems.at[base + k - 1],
            device_id=(tx, ty, tc), device_id_type=MESH))
    return copies


def _ride_scatter(first, last, riders, g_refs, slot_refs, send_sems, recv_sems):
    def all_copies():
        out = []
        for j, (_, col_sharded) in enumerate(riders):
            out += _scatter_copies(g_refs[j], slot_refs[j], send_sems, recv_sems, 7 * j, col_sharded)
        return out

    @pl.when(first)
    def _():
        for cp in all_copies():
            cp.start()

    @pl.when(last)
    def _():
        for cp in all_copies():
            cp.wait()


def _rider_specs(riders):
    any_spec = pl.BlockSpec(memory_space=pl.ANY)
    shapes = [jax.ShapeDtypeStruct((7,) + _piece_shape(g.shape, cs), BF) for g, cs in riders]
    sems = [pltpu.SemaphoreType.DMA((7 * len(riders),)), pltpu.SemaphoreType.DMA((7 * len(riders),))] if riders else []
    return [any_spec] * len(riders), shapes, sems


class _Gather:
    def __init__(self, ins, outs, send_sems, recv_sems, local_sems):
        self.ins, self.outs = ins, outs
        self.send_sems, self.recv_sems, self.local_sems = send_sems, recv_sems, local_sems
        self.x, self.y, self.c = _place()
        self.me = 2 * self.x + self.y

    def _push(self, src, dst, w, j, to):
        return pltpu.make_async_remote_copy(src_ref=src, dst_ref=dst, send_sem=self.send_sems.at[6 * w + j],
                                            recv_sem=self.recv_sems.at[6 * w + j], device_id=to, device_id_type=MESH)

    def _half(self, w, h):
        half = self.ins[w].shape[0] // 2
        return pl.ds(pl.multiple_of(h * half, 16), half)

    def _local(self, w):
        return pltpu.make_async_copy(self.ins[w], self.outs[w].at[self.me], self.local_sems.at[w])

    def _ici(self, w, k):
        px, py = _chip_peer(self.x, self.y, k)
        mine = self._half(w, self.c)
        return self._push(self.ins[w].at[mine], self.outs[w].at[self.me, mine], w, k - 1, (px, py, self.c))

    def _landed(self, w, k, h):
        return self.outs[w].at[self.me ^ k, self._half(w, h)]

    def _fwd(self, w, k):
        landed = self._landed(w, k, self.c)
        return self._push(landed, landed, w, 2 + k, (self.x, self.y, 1 - self.c))

    def start(self):
        for w in range(len(self.ins)):
            self._local(w).start()
            for k in (1, 2, 3):
                self._ici(w, k).start()

    def forward(self):
        for w in range(len(self.ins)):
            for k in (1, 2, 3):
                landed = self._landed(w, k, self.c)
                self._push(landed, landed, w, k - 1, (self.x, self.y, self.c)).wait_recv()
                self._fwd(w, k).start()

    def finish(self):
        for w in range(len(self.ins)):
            for k in (1, 2, 3):
                landed = self._landed(w, k, 1 - self.c)
                self._push(landed, landed, w, 2 + k, (self.x, self.y, self.c)).wait_recv()
            for k in (1, 2, 3):
                self._ici(w, k).wait_send()
                self._fwd(w, k).wait_send()
            self._local(w).wait()


def _alibi(h):
    return 2.0 ** (-(h + 1))


CHUNK = 2048


def _mask_tables():
    slopes = jnp.asarray([_alibi(h) for h in range(N_HEADS)], F32)[:, None, None]

    def table(step):
        valid = (step >= 0) & (step <= QBLK)
        return jnp.stack([jnp.where(valid[None], -slopes * (step * d)[None].astype(F32), NEG) for d in DILATIONS])

    i = jnp.arange(QBLK)[:, None]
    j2 = jnp.arange(2 * QBLK)[None, :]
    r2 = jnp.arange(2 * QBLK)[:, None]
    j = jnp.arange(QBLK)[None, :]
    fwd, bwd = table(i + QBLK - j2), table(r2 - j)
    return fwd.reshape(3, N_HEADS // 2, 2 * QBLK, 2 * QBLK), bwd.reshape(3, N_HEADS // 2, 4 * QBLK, QBLK)


def _attn_fwd(qkn, qkv, mb, late=()):
    t = qkn.shape[0]
    nc = t // CHUNK
    nl = len(late)

    def body(*refs):
        qc_ref, kp_ref, kc_ref, vp_ref, vc_ref, mb_ref = refs[0:6]
        o_ref, l_ref = refs[6 + nl:8 + nl]
        ob0, ob1, ob2, lb0, lb1, lb2 = refs[8 + 2 * nl:14 + 2 * nl]
        if nl:
            gather = _Gather(refs[6:6 + nl], refs[8 + nl:8 + 2 * nl], *refs[14 + 2 * nl:17 + 2 * nl])
            step = pl.program_id(0) * nc + pl.program_id(1)
            pl.when(step == 0)(gather.start)
            pl.when(step == 2 * nc)(gather.forward)
            pl.when(step == (N_HEADS // 2) * nc - 1)(gather.finish)
        first = pl.program_id(1) == 0
        lane = lax.broadcasted_iota(jnp.int32, (QBLK, 128), 1)
        lo_half = lane < HEAD_DIM
        kj = lax.broadcasted_iota(jnp.int32, (2 * QBLK, 2 * QBLK), 1)
        no_prev = first & (kj < QBLK)
        obs, lbs = (ob0, ob1, ob2), (lb0, lb1, lb2)

        def by_head(a):
            return jnp.where(lo_half, a, 0.0).astype(BF), jnp.where(lo_half, 0.0, a).astype(BF)

        for di, d in enumerate(DILATIONS):
            span = d * QBLK
            for r in range(d):
                tail = pl.ds(CHUNK - span + r, QBLK, stride=d)
                k_prev = kp_ref[tail, :].astype(BF)
                v_prev = by_head(vp_ref[tail, :])
                for b in range(CHUNK // span):
                    rows = pl.ds(r + span * b, QBLK, stride=d)
                    q0, q1 = by_head(qc_ref[rows, :])
                    k_cur = kc_ref[rows, :].astype(BF)
                    v_cur = by_head(vc_ref[rows, :])
                    s = _mm_nt(jnp.concatenate([q0, q1], axis=0), jnp.concatenate([k_prev, k_cur], axis=0))
                    s = s + mb_ref[di, 0]
                    if b == 0:
                        s = jnp.where(no_prev, NEG, s)
                    m = jnp.max(s, axis=-1, keepdims=True)
                    e = jnp.exp(s - m)
                    den = jnp.sum(e, axis=-1, keepdims=True)
                    eb = e.astype(BF)
                    o = _mm(jnp.concatenate([eb[0:QBLK], eb[QBLK:2 * QBLK]], axis=1),
                            jnp.concatenate([v_prev[0], v_cur[0], v_prev[1], v_cur[1]], axis=0))
                    inv = 1.0 / den
                    lse = m + jnp.log(den)
                    obs[di][rows, :] = o * jnp.where(lo_half, inv[0:QBLK], inv[QBLK:2 * QBLK])
                    lbs[di][rows, :] = jnp.where(lo_half, lse[0:QBLK], lse[QBLK:2 * QBLK])
                    k_prev, v_prev = k_cur, v_cur
        for c0 in range(0, CHUNK, 256):
            rs = slice(c0, c0 + 256)
            l0, l1, l2 = lb0[rs, :], lb1[rs, :], lb2[rs, :]
            mx = jnp.maximum(jnp.maximum(l0, l1), l2)
            w0, w1, w2 = jnp.exp(l0 - mx), jnp.exp(l1 - mx), jnp.exp(l2 - mx)
            tot = w0 + w1 + w2
            o_ref[rs, :] = (ob0[rs, :] * w0 + ob1[rs, :] * w1 + ob2[rs, :] * w2) / tot
            l_ref[rs, :] = mx + jnp.log(tot)

    def cur(col):
        return pl.BlockSpec((CHUNK, 128), lambda hp, n: (n, col + hp))

    def prv(col):
        return pl.BlockSpec((CHUNK, 128), lambda hp, n: (jnp.maximum(n - 1, 0), col + hp))

    out = pl.BlockSpec((CHUNK, 128), lambda hp, n: (n, hp))
    any_spec = pl.BlockSpec(memory_space=pl.ANY)
    sems = [pltpu.SemaphoreType.DMA((6 * nl,)), pltpu.SemaphoreType.DMA((6 * nl,)), pltpu.SemaphoreType.DMA((nl,))]
    res = pl.pallas_call(
        body, name="attn_fwd", grid=(N_HEADS // 2, nc),
        in_specs=[cur(0), prv(4), cur(4), prv(8), cur(8),
                  pl.BlockSpec((3, 1, 2 * QBLK, 2 * QBLK), lambda hp, n: (0, hp, 0, 0))] + [any_spec] * nl,
        out_specs=[out, out] + [any_spec] * nl,
        out_shape=[jax.ShapeDtypeStruct((t, ATTN_W), F32)] * 2
        + [jax.ShapeDtypeStruct((N_CHIPS,) + w.shape, w.dtype) for w in late],
        scratch_shapes=[pltpu.VMEM((CHUNK, 128), F32)] * 6 + (sems if nl else []),
        compiler_params=_params(48, 2),
    )(qkn, qkn, qkn, qkv, qkv, mb, *late)
    return res[0], res[1], list(res[2:])


def _attn_bwd(qkn, qkv, o, lse, do, mb, mbk, riders=()):
    t = qkn.shape[0]
    nc = t // CHUNK
    nr = len(riders)

    def body(*refs):
        (qc_ref, qn_ref, kp_ref, kc_ref, vp_ref, vc_ref, oc_ref, on_ref, lc_ref, ln_ref, dc_ref, dn_ref,
         mb_ref, mbk_ref) = refs[0:14]
        dq_ref, dk_ref, dv_ref = refs[14 + nr:17 + nr]
        if nr:
            step = pl.program_id(0) * nc + pl.program_id(1)
            _ride_scatter(step == 0, step == (N_HEADS // 2) * nc - 1, riders, refs[14:14 + nr],
                          refs[17 + nr:17 + 2 * nr], *refs[17 + 2 * nr:19 + 2 * nr])
        first = pl.program_id(1) == 0
        last = pl.program_id(1) == nc - 1
        lane = lax.broadcasted_iota(jnp.int32, (QBLK, 128), 1)
        lo_half = lane < HEAD_DIM
        kj = lax.broadcasted_iota(jnp.int32, (2 * QBLK, 2 * QBLK), 1)
        no_prev = first & (kj < QBLK)
        ri = lax.broadcasted_iota(jnp.int32, (4 * QBLK, QBLK), 0)
        no_next = last & ((ri & (2 * QBLK - 1)) >= QBLK)

        def by_head(a):
            return jnp.where(lo_half, a, 0.0).astype(BF), jnp.where(lo_half, 0.0, a).astype(BF)

        def query_side(q_ref, d_ref, o_ref_, l_ref_, rows):
            dvals = d_ref[rows, :]
            dd = dvals * o_ref_[rows, :]
            lv = l_ref_[rows, :]
            d0 = jnp.sum(jnp.where(lo_half, dd, 0.0), axis=-1, keepdims=True)
            d1 = jnp.sum(jnp.where(lo_half, 0.0, dd), axis=-1, keepdims=True)
            l0 = jnp.max(jnp.where(lo_half, lv, NEG), axis=-1, keepdims=True)
            l1 = jnp.max(jnp.where(lo_half, NEG, lv), axis=-1, keepdims=True)
            return by_head(q_ref[rows, :]), by_head(dvals), (l0, l1), (d0, d1)

        for di, d in enumerate(DILATIONS):
            span = d * QBLK
            nbk = CHUNK // span
            for r in range(d):
                tail = pl.ds(CHUNK - span + r, QBLK, stride=d)
                k_prev = kp_ref[tail, :]
                kb_prev, km_prev = k_prev.astype(BF), by_head(k_prev)
                vb_prev = vp_ref[tail, :].astype(BF)
                rows0 = pl.ds(r, QBLK, stride=d)
                cur = query_side(qc_ref, dc_ref, oc_ref, lc_ref, rows0)
                for b in range(nbk):
                    rows = pl.ds(r + span * b, QBLK, stride=d)
                    if b == nbk - 1:
                        nxt = query_side(qn_ref, dn_ref, on_ref, ln_ref, rows0)
                    else:
                        nxt = query_side(qc_ref, dc_ref, oc_ref, lc_ref, pl.ds(r + span * (b + 1), QBLK, stride=d))
                    (q0, q1), (do0, do1), (l0, l1), (d0, d1) = cur
                    (qx0, qx1), (dox0, dox1), (lx0, lx1), (dx0, dx1) = nxt
                    k_cur = kc_ref[rows, :]
                    kb_cur, km_cur = k_cur.astype(BF), by_head(k_cur)
                    vb_cur = vc_ref[rows, :].astype(BF)
                    k2 = jnp.concatenate([kb_prev, kb_cur], axis=0)
                    v2 = jnp.concatenate([vb_prev, vb_cur], axis=0)
                    s = _mm_nt(jnp.concatenate([q0, q1], axis=0), k2) + mb_ref[di, 0]
                    if b == 0:
                        s = jnp.where(no_prev, NEG, s)
                    p = jnp.exp(s - jnp.concatenate([l0, l1], axis=0))
                    dp = _mm_nt(jnp.concatenate([do0, do1], axis=0), v2)
                    ds = (p * (dp - jnp.concatenate([d0, d1], axis=0))).astype(BF)
                    dq2 = _mm(jnp.concatenate([ds[0:QBLK], ds[QBLK:2 * QBLK]], axis=1),
                              jnp.concatenate([km_prev[0], km_cur[0], km_prev[1], km_cur[1]], axis=0))
                    if di == 0:
                        dq_ref[rows, :] = dq2
                    else:
                        dq_ref[rows, :] = dq_ref[rows, :] + dq2
                    q4 = jnp.concatenate([q0, qx0, q1, qx1], axis=0)
                    do4 = jnp.concatenate([do0, dox0, do1, dox1], axis=0)
                    s = _mm_nt(q4, kb_cur) + mbk_ref[di, 0]
                    if b == nbk - 1:
                        s = jnp.where(no_next, NEG, s)
                    p = jnp.exp(s - jnp.concatenate([l0, lx0, l1, lx1], axis=0))
                    dv2 = _mm_tn(p.astype(BF), do4)
                    dp = _mm_nt(do4, vb_cur)
                    ds = (p * (dp - jnp.concatenate([d0, dx0, d1, dx1], axis=0))).astype(BF)
                    dk2 = _mm_tn(ds, q4)
                    if di == 0:
                        dk_ref[rows, :] = dk2
                        dv_ref[rows, :] = dv2
                    else:
                        dk_ref[rows, :] = dk_ref[rows, :] + dk2
                        dv_ref[rows, :] = dv_ref[rows, :] + dv2
                    cur = nxt
                    kb_prev, km_prev, vb_prev = kb_cur, km_cur, vb_cur

    def at(shift, col):
        return pl.BlockSpec((CHUNK, 128), lambda hp, n: (jnp.clip(n + shift, 0, nc - 1), col + hp))

    out = pl.BlockSpec((CHUNK, 128), lambda hp, n: (n, hp))
    r_in, r_out, r_sems = _rider_specs(riders)
    res = pl.pallas_call(
        body, name="attn_bwd", grid=(N_HEADS // 2, nc),
        in_specs=[at(0, 0), at(1, 0), at(-1, 4), at(0, 4), at(-1, 8), at(0, 8),
                  at(0, 0), at(1, 0), at(0, 0), at(1, 0), at(0, 0), at(1, 0),
                  pl.BlockSpec((3, 1, 2 * QBLK, 2 * QBLK), lambda hp, n: (0, hp, 0, 0)),
                  pl.BlockSpec((3, 1, 4 * QBLK, QBLK), lambda hp, n: (0, hp, 0, 0))] + r_in,
        out_specs=[out, out, out] + r_in,
        out_shape=[jax.ShapeDtypeStruct((t, ATTN_W), F32)] * 3 + r_out,
        scratch_shapes=r_sems,
        compiler_params=_params(56, 2),
    )(qkn, qkn, qkn, qkn, qkv, qkv, o, o, lse, lse, do, do, mb, mbk, *[g for g, _ in riders])
    return res[0], res[1], res[2], list(res[3:])


def _fwd_ffn(x, ycn, ya, wout, wg4, wu4, g_oa, g_ffn, fcw, fcb, tm, late=()):
    t = x.shape[0]
    nt = t // tm
    nl = len(late)

    def body(*refs):
        x_ref, ycn_ref, ya_ref, wout_ref, wg_ref, wu_ref, goa_ref, gffn_ref, fcw_ref, fcb_ref = refs[0:10]
        x1_ref, gp_ref, up_ref, act_ref, ycat_ref, h2_ref = refs[10 + nl:16 + nl]
        cbuf = refs[16 + 2 * nl]
        if nl:
            gather = _Gather(refs[10:10 + nl], refs[16 + nl:16 + 2 * nl], *refs[17 + 2 * nl:20 + 2 * nl])
            pl.when(pl.program_id(0) == 0)(gather.start)
            pl.when(pl.program_id(0) == nt // 2)(gather.forward)
            pl.when(pl.program_id(0) == nt - 1)(gather.finish)

        @pl.when(pl.program_id(0) == 0)
        def _():
            cbuf[0:8, :] = jnp.zeros((8, D_FF_PAD), F32)

        yat = ya_ref[...]
        yan = ((yat * _rstd(yat)) * goa_ref[...]).astype(BF)
        ycn = ycn_ref[...]
        ycat_ref[:, 0:CONV_W] = ycn
        ycat_ref[:, CONV_W:D_MODEL] = yan
        x1 = x_ref[...] + _mm(ycn, wout_ref[0:CONV_W, :]) + _mm(yan, wout_ref[CONV_W:D_MODEL, :])
        x1_ref[...] = x1
        h2 = ((x1 * _rstd(x1)) * gffn_ref[...]).astype(BF)
        h2_ref[...] = h2
        for s in range(N_CHIPS):
            lo, hi = s * D_FF_SLAB, (s + 1) * D_FF_SLAB
            gps = _mm(h2, wg_ref[s])
            ups = _mm(h2, wu_ref[s])
            gp_ref[:, lo:hi] = gps
            up_ref[:, lo:hi] = ups
            cbuf[8:8 + tm, lo:hi] = gps
            cb = cbuf[:, lo:hi]
            gate = (fcw_ref[0:1, lo:hi] * _shift_down(cb, 2, tm) + fcw_ref[1:2, lo:hi] * _shift_down(cb, 1, tm)
                    + fcw_ref[2:3, lo:hi] * gps + fcb_ref[:, lo:hi])
            act_ref[:, lo:hi] = ((gate * jax.nn.sigmoid(gate)) * ups).astype(BF)
        cbuf[0:8, :] = cbuf[tm:tm + 8, :]

    any_spec = pl.BlockSpec(memory_space=pl.ANY)
    sems = [pltpu.SemaphoreType.DMA((6 * nl,)), pltpu.SemaphoreType.DMA((6 * nl,)), pltpu.SemaphoreType.DMA((nl,))]
    res = pl.pallas_call(
        body, name="fwd_ffn", grid=(nt,),
        in_specs=[_rows(tm, D_MODEL), _rows(tm, CONV_W), _rows(tm, ATTN_W), _const((D_MODEL, D_MODEL)),
                  _const((N_CHIPS, D_MODEL, D_FF_SLAB)), _const((N_CHIPS, D_MODEL, D_FF_SLAB)),
                  _const((1, ATTN_W)), _const((1, D_MODEL)), _const((3, D_FF_PAD)), _const((1, D_FF_PAD))]
        + [any_spec] * nl,
        out_specs=[_rows(tm, D_MODEL), _rows(tm, D_FF_PAD), _rows(tm, D_FF_PAD), _rows(tm, D_FF_PAD),
                   _rows(tm, D_MODEL), _rows(tm, D_MODEL)] + [any_spec] * nl,
        out_shape=[jax.ShapeDtypeStruct((t, D_MODEL), F32), jax.ShapeDtypeStruct((t, D_FF_PAD), F32),
                   jax.ShapeDtypeStruct((t, D_FF_PAD), F32), jax.ShapeDtypeStruct((t, D_FF_PAD), BF),
                   jax.ShapeDtypeStruct((t, D_MODEL), BF), jax.ShapeDtypeStruct((t, D_MODEL), BF)]
        + [jax.ShapeDtypeStruct((N_CHIPS,) + w.shape, w.dtype) for w in late],
        scratch_shapes=[pltpu.VMEM((tm + 8, D_FF_PAD), F32)] + (sems if nl else []),
        compiler_params=_params(56),
    )(x, ycn, ya, wout, wg4, wu4, g_oa, g_ffn, fcw, fcb, *late)
    return tuple(res[0:6]) + (list(res[6:]),)


def _fwd_tail(x1, act, p, target, wd4, wpg, wpp4, g_ple, tm):
    t = x1.shape[0]
    nt = t // tm

    def body(x1_ref, act_ref, p_ref, tgt_ref, wd_ref, wpg_ref, wpp_ref, g_ref,
             dx2_ref, h3_ref, ds_ref, dpp_ref, dg_ref, loss_ref, lacc):
        i = pl.program_id(0)

        @pl.when(i == 0)
        def _():
            dg_ref[...] = jnp.zeros_like(dg_ref)
            lacc[...] = jnp.zeros_like(lacc)

        x2 = x1_ref[...]
        for s in range(N_CHIPS):
            x2 = x2 + _mm(act_ref[:, s * D_FF_SLAB:(s + 1) * D_FF_SLAB], wd_ref[s])
        r3 = _rstd(x2)
        xh = x2 * r3
        h3 = (xh * g_ref[...]).astype(BF)
        h3_ref[...] = h3
        sg = jax.nn.sigmoid(_mm(h3, wpg_ref[...]))
        pb = p_ref[...].astype(BF)
        pp = jnp.concatenate([_mm(pb, wpp_ref[s]) for s in range(N_CHIPS)], axis=1)
        err = (x2 + sg * pp) - tgt_ref[...]
        lacc[...] += _colsum(err * err)
        dx3 = err * (1.0 / D_MODEL)
        dpp_ref[...] = (dx3 * sg).astype(BF)
        dsb = ((dx3 * pp) * (sg * (1.0 - sg))).astype(BF)
        ds_ref[...] = dsb
        dh3 = _mm_nt(dsb, wpg_ref[...])
        dg_ref[...] += _colsum(dh3 * xh)
        dx2_ref[...] = dx3 + _norm_bwd(dh3, xh, r3, g_ref[...])

        @pl.when(i == nt - 1)
        def _():
            loss_ref[...] = jnp.full((1, 128), jnp.sum(lacc[...]) * (0.5 / D_MODEL), F32)

    return pl.pallas_call(
        body, name="fwd_tail", grid=(nt,),
        in_specs=[_rows(tm, D_MODEL), _rows(tm, D_FF_PAD), _rows(tm, PLE_DIM), _rows(tm, D_MODEL),
                  _const((N_CHIPS, D_FF_SLAB, D_MODEL)), _const((D_MODEL, D_MODEL)),
                  _const((N_CHIPS, PLE_DIM, PLE_DIM)), _const((1, D_MODEL))],
        out_specs=[_rows(tm, D_MODEL), _rows(tm, D_MODEL), _rows(tm, D_MODEL), _rows(tm, D_MODEL),
                   pl.BlockSpec((1, D_MODEL), lambda i: (0, 0)), pl.BlockSpec((1, 128), lambda i: (0, 0))],
        out_shape=[jax.ShapeDtypeStruct((t, D_MODEL), F32), jax.ShapeDtypeStruct((t, D_MODEL), BF),
                   jax.ShapeDtypeStruct((t, D_MODEL), BF), jax.ShapeDtypeStruct((t, D_MODEL), BF),
                   jax.ShapeDtypeStruct((1, D_MODEL), F32), jax.ShapeDtypeStruct((1, 128), F32)],
        scratch_shapes=[pltpu.VMEM((1, D_MODEL), F32)],
        compiler_params=_params(48),
    )(x1, act, p, target, wd4, wpg, wpp4, g_ple)


def _bwd_ffn_a(dx2, gp, up, wd4, fcw, fcb, tm, riders=()):
    t = dx2.shape[0]
    nt = t // tm
    nr = len(riders)

    def body(*refs):
        dx2_ref, gp_ref, gph_ref, up_ref, wd_ref, fcw_ref, fcb_ref = refs[0:7]
        dgp_ref, dup_ref, dfcw_ref, dfcb_ref = refs[7 + nr:11 + nr]
        cbuf, dbuf = refs[11 + 2 * nr:13 + 2 * nr]
        i = pl.program_id(0)
        if nr:
            _ride_scatter(i == 0, i == nt - 1, riders, refs[7:7 + nr], refs[11 + nr:11 + 2 * nr],
                          *refs[13 + 2 * nr:15 + 2 * nr])

        @pl.when(i == 0)
        def _():
            dbuf[tm:tm + 8, :] = jnp.zeros((8, D_FF_PAD), F32)
            dfcw_ref[...] = jnp.zeros_like(dfcw_ref)
            dfcb_ref[...] = jnp.zeros_like(dfcb_ref)

        not_first_tile = i < nt - 1
        dx2b = dx2_ref[...].astype(BF)
        for s in range(N_CHIPS):
            lo, hi = s * D_FF_SLAB, (s + 1) * D_FF_SLAB
            gps = gp_ref[:, lo:hi]
            cbuf[0:8, lo:hi] = jnp.where(not_first_tile, gph_ref[:, lo:hi], 0.0)
            cbuf[8:8 + tm, lo:hi] = gps
            cb = cbuf[:, lo:hi]
            g1 = _shift_down(cb, 1, tm)
            g2 = _shift_down(cb, 2, tm)
            w0, w1, w2 = fcw_ref[0:1, lo:hi], fcw_ref[1:2, lo:hi], fcw_ref[2:3, lo:hi]
            gate = w0 * g2 + w1 * g1 + w2 * gps + fcb_ref[:, lo:hi]
            sg = jax.nn.sigmoid(gate)
            dact = _mm_nt(dx2b, wd_ref[s])
            dup_ref[:, lo:hi] = (dact * (gate * sg)).astype(BF)
            dgate = (dact * up_ref[:, lo:hi]) * (sg * (1.0 + gate * (1.0 - sg)))
            dfcb_ref[:, lo:hi] += _colsum(dgate)
            dfcw_ref[0:1, lo:hi] += _colsum(dgate * g2)
            dfcw_ref[1:2, lo:hi] += _colsum(dgate * g1)
            dfcw_ref[2:3, lo:hi] += _colsum(dgate * gps)
            dbuf[0:tm, lo:hi] = dgate
            db = dbuf[:, lo:hi]
            dgp = w2 * dgate + w1 * _shift_up(db, 1, tm) + w0 * _shift_up(db, 2, tm)
            dgp_ref[:, lo:hi] = dgp.astype(BF)
        dbuf[tm:tm + 8, :] = dbuf[0:8, :]

    r_in, r_out, r_sems = _rider_specs(riders)
    res = pl.pallas_call(
        body, name="bwd_ffn_a", grid=(nt,),
        in_specs=[_rows(tm, D_MODEL, nt), _rows(tm, D_FF_PAD, nt), _halo(tm, D_FF_PAD, nt), _rows(tm, D_FF_PAD, nt),
                  _const((N_CHIPS, D_FF_SLAB, D_MODEL)), _const((3, D_FF_PAD)), _const((1, D_FF_PAD))] + r_in,
        out_specs=[_rows(tm, D_FF_PAD, nt), _rows(tm, D_FF_PAD, nt),
                   pl.BlockSpec((3, D_FF_PAD), lambda i: (0, 0)), pl.BlockSpec((1, D_FF_PAD), lambda i: (0, 0))] + r_in,
        out_shape=[jax.ShapeDtypeStruct((t, D_FF_PAD), BF), jax.ShapeDtypeStruct((t, D_FF_PAD), BF),
                   jax.ShapeDtypeStruct((3, D_FF_PAD), F32), jax.ShapeDtypeStruct((1, D_FF_PAD), F32)] + r_out,
        scratch_shapes=[pltpu.VMEM((tm + 8, D_FF_PAD), F32), pltpu.VMEM((tm + 8, D_FF_PAD), F32)] + r_sems,
        compiler_params=_params(56),
    )(dx2, gp, gp, up, wd4, fcw, fcb, *[g for g, _ in riders])
    return res[0], res[1], res[2], res[3], list(res[4:])


def _bwd_ffn_b(dgp, dup, dx2, x1, ya, wg4, wu4, wout, g_ffn, g_oa, tm):
    t = dx2.shape[0]
    nt = t // tm

    def body(dgp_ref, dup_ref, dx2_ref, x1_ref, ya_ref, wg_ref, wu_ref, wout_ref, gffn_ref, goa_ref,
             dx1_ref, dycn_ref, dya_ref, dgffn_ref, dgoa_ref):
        @pl.when(pl.program_id(0) == 0)
        def _():
            dgffn_ref[...] = jnp.zeros_like(dgffn_ref)
            dgoa_ref[...] = jnp.zeros_like(dgoa_ref)

        dh2 = jnp.zeros((tm, D_MODEL), F32)
        for s in range(N_CHIPS):
            lo, hi = s * D_FF_SLAB, (s + 1) * D_FF_SLAB
            dh2 = dh2 + _mm_nt(dgp_ref[:, lo:hi], wg_ref[s]) + _mm_nt(dup_ref[:, lo:hi], wu_ref[s])
        x1 = x1_ref[...]
        r2 = _rstd(x1)
        xh = x1 * r2
        dgffn_ref[...] += _colsum(dh2 * xh)
        dx1 = dx2_ref[...] + _norm_bwd(dh2, xh, r2, gffn_ref[...])
        dx1_ref[...] = dx1
        dy = _mm_nt(dx1.astype(BF), wout_ref[...])
        dycn_ref[...] = dy[:, 0:CONV_W]
        dyan = dy[:, CONV_W:D_MODEL]
        yat = ya_ref[...]
        ra = _rstd(yat)
        yah = yat * ra
        dgoa_ref[...] += _colsum(dyan * yah)
        dya_ref[...] = _norm_bwd(dyan, yah, ra, goa_ref[...])

    return pl.pallas_call(
        body, name="bwd_ffn_b", grid=(nt,),
        in_specs=[_rows(tm, D_FF_PAD), _rows(tm, D_FF_PAD), _rows(tm, D_MODEL), _rows(tm, D_MODEL),
                  _rows(tm, ATTN_W), _const((N_CHIPS, D_MODEL, D_FF_SLAB)), _const((N_CHIPS, D_MODEL, D_FF_SLAB)),
                  _const((D_MODEL, D_MODEL)), _const((1, D_MODEL)), _const((1, ATTN_W))],
        out_specs=[_rows(tm, D_MODEL), _rows(tm, CONV_W), _rows(tm, ATTN_W),
                   pl.BlockSpec((1, D_MODEL), lambda i: (0, 0)), pl.BlockSpec((1, ATTN_W), lambda i: (0, 0))],
        out_shape=[jax.ShapeDtypeStruct((t, D_MODEL), F32), jax.ShapeDtypeStruct((t, CONV_W), F32),
                   jax.ShapeDtypeStruct((t, ATTN_W), F32),
                   jax.ShapeDtypeStruct((1, D_MODEL), F32), jax.ShapeDtypeStruct((1, ATTN_W), F32)],
        compiler_params=_params(48),
    )(dgp, dup, dx2, x1, ya, wg4, wu4, wout, g_ffn, g_oa)


def _bwd_mix(x, dx1, zbcx, qkv, dycn, dq, dk, dv, win4, conv_w, conv_b, g_oc, g_mix, gm, gq8, gk8, tm):
    t = x.shape[0]
    nt = t // tm

    def body(x_ref, dx1_ref, z_ref, zh_ref, qkv_ref, dycn_ref, dq_ref, dk_ref, dv_ref, w_ref, cw_ref, cb_ref,
             goc_ref, g_ref, gm_ref, gq_ref, gk_ref,
             gx_ref, h1_ref, dz_ref, dcw_ref, dcb_ref, dgoc_ref, dg_ref, dgq_ref, dgk_ref, ubuf, dbuf):
        i = pl.program_id(0)

        @pl.when(i == 0)
        def _():
            dbuf[tm:tm + 8, :] = jnp.zeros((8, CONV_W), F32)
            dcw_ref[...] = jnp.zeros_like(dcw_ref)
            dcb_ref[...] = jnp.zeros_like(dcb_ref)
            dgoc_ref[...] = jnp.zeros_like(dgoc_ref)
            dg_ref[...] = jnp.zeros_like(dg_ref)
            dgq_ref[...] = jnp.zeros_like(dgq_ref)
            dgk_ref[...] = jnp.zeros_like(dgk_ref)

        not_first_tile = i < nt - 1
        zb = z_ref[:, 0:512]
        zc = z_ref[:, 512:1024]
        zx = z_ref[:, 1024:1536]
        u = zc * zx
        ubuf[0:8, :] = jnp.where(not_first_tile, zh_ref[:, 512:1024] * zh_ref[:, 1024:1536], 0.0)
        ubuf[8:8 + tm, :] = u
        ub = ubuf[...]
        u1 = _shift_down(ub, 1, tm)
        u2 = _shift_down(ub, 2, tm)
        w0, w1, w2 = cw_ref[0:1, :], cw_ref[1:2, :], cw_ref[2:3, :]
        cv = w0 * u2 + w1 * u1 + w2 * u + cb_ref[...]
        yc = zb * cv
        rc = _rstd(yc)
        ych = yc * rc
        dycn = dycn_ref[...]
        dgoc_ref[...] += _colsum(dycn * ych)
        dyc = _norm_bwd(dycn, ych, rc, goc_ref[...])
        dcv = dyc * zb
        dcb_ref[...] += _colsum(dcv)
        dcw_ref[0:1, :] += _colsum(dcv * u2)
        dcw_ref[1:2, :] += _colsum(dcv * u1)
        dcw_ref[2:3, :] += _colsum(dcv * u)
        dbuf[0:tm, :] = dcv
        db = dbuf[...]
        du = w2 * dcv + w1 * _shift_up(db, 1, tm) + w0 * _shift_up(db, 2, tm)
        dbuf[tm:tm + 8, :] = dbuf[0:8, :]
        dz_ref[:, 0:512] = (dyc * cv).astype(BF)
        dz_ref[:, 512:1024] = (du * zx).astype(BF)
        dz_ref[:, 1024:1536] = (du * zc).astype(BF)
        for z0, d_ref, gg_ref, acc_ref, sc in ((0, dq_ref, gq_ref, dgq_ref, HEAD_DIM ** -0.5),
                                               (512, dk_ref, gk_ref, dgk_ref, 1.0)):
            z = qkv_ref[:, z0:z0 + 512]
            rr = lax.rsqrt(_head_mean(z * z, gm_ref) + EPS)
            zh = z * rr
            dn = d_ref[...] * sc
            acc_ref[...] += _colsum(dn * zh)
            dzh = dn * gg_ref[...]
            dz_ref[:, 1536 + z0:1536 + z0 + 512] = (rr * (dzh - zh * _head_mean(dzh * zh, gm_ref))).astype(BF)
        dz_ref[:, 2560:3072] = dv_ref[...].astype(BF)
        dh1 = jnp.zeros((tm, D_MODEL), F32)
        for s in range(N_CHIPS):
            dh1 = dh1 + _mm_nt(dz_ref[:, s * IN_SLAB:(s + 1) * IN_SLAB], w_ref[s])
        xt = x_ref[...]
        r1 = _rstd(xt)
        xh = xt * r1
        h1_ref[...] = (xh * g_ref[...]).astype(BF)
        dg_ref[...] += _colsum(dh1 * xh)
        gx_ref[...] = dx1_ref[...] + _norm_bwd(dh1, xh, r1, g_ref[...])

    def acc(width, rows=1):
        return pl.BlockSpec((rows, width), lambda i: (0, 0))

    return pl.pallas_call(
        body, name="bwd_mix", grid=(nt,),
        in_specs=[_rows(tm, D_MODEL, nt), _rows(tm, D_MODEL, nt), _rows(tm, 1536, nt), _halo(tm, 1536, nt),
                  _rows(tm, 1536, nt), _rows(tm, CONV_W, nt), _rows(tm, ATTN_W, nt), _rows(tm, ATTN_W, nt),
                  _rows(tm, ATTN_W, nt), _const((N_CHIPS, D_MODEL, IN_SLAB)),
                  _const((3, CONV_W)), _const((1, CONV_W)), _const((1, CONV_W)), _const((1, D_MODEL)),
                  _const((ATTN_W, ATTN_W)), _const((1, ATTN_W)), _const((1, ATTN_W))],
        out_specs=[_rows(tm, D_MODEL, nt), _rows(tm, D_MODEL, nt), _rows(tm, 3072, nt),
                   acc(CONV_W, 3), acc(CONV_W), acc(CONV_W), acc(D_MODEL), acc(ATTN_W), acc(ATTN_W)],
        out_shape=[jax.ShapeDtypeStruct((t, D_MODEL), F32), jax.ShapeDtypeStruct((t, D_MODEL), BF),
                   jax.ShapeDtypeStruct((t, 3072), BF), jax.ShapeDtypeStruct((3, CONV_W), F32),
                   jax.ShapeDtypeStruct((1, CONV_W), F32), jax.ShapeDtypeStruct((1, CONV_W), F32),
                   jax.ShapeDtypeStruct((1, D_MODEL), F32), jax.ShapeDtypeStruct((1, ATTN_W), F32),
                   jax.ShapeDtypeStruct((1, ATTN_W), F32)],
        scratch_shapes=[pltpu.VMEM((tm + 8, CONV_W), F32), pltpu.VMEM((tm + 8, CONV_W), F32)],
        compiler_params=_params(56),
    )(x, dx1, zbcx, zbcx, qkv, dycn, dq, dk, dv, win4, conv_w, conv_b, g_oc, g_mix, gm, gq8, gk8)


def _wgrad(a, b, tn, tt, name):
    t, k = a.shape
    n = b.shape[1]
    nt = t // tt

    def body(a_ref, b_ref, o_ref, ob_ref):
        @pl.when(pl.program_id(1) == 0)
        def _():
            o_ref[...] = jnp.zeros_like(o_ref)

        o_ref[...] += _mm_tn(a_ref[...].astype(BF), b_ref[...].astype(BF))

        @pl.when(pl.program_id(1) == nt - 1)
        def _():
            ob_ref[...] = o_ref[...].astype(BF)

    spec = pl.BlockSpec((k, tn), lambda j, i: (0, j))
    return pl.pallas_call(
        body, name=name, grid=(n // tn, nt),
        in_specs=[pl.BlockSpec((tt, k), lambda j, i: (i, 0)), pl.BlockSpec((tt, tn), lambda j, i: (i, j))],
        out_specs=[spec, spec],
        out_shape=[jax.ShapeDtypeStruct((k, n), F32), jax.ShapeDtypeStruct((k, n), BF)],
        compiler_params=_params(48, 2),
    )(a, b)


def _gather_weights(shards, pack):
    nw = len(shards)

    def body(*refs):
        ins = refs[:nw]
        pack_ref = refs[nw]
        outs = refs[nw + 1:2 * nw + 1]
        pack_out = refs[2 * nw + 1]
        send_sems, recv_sems, local_sems = refs[2 * nw + 2:]
        x, y, c = _place()
        me = 2 * x + y
        local, remote = [], []

        def sem(w, j):
            return w * 6 + j

        def push(src, dst, w, j, to):
            return pltpu.make_async_remote_copy(src_ref=src, dst_ref=dst, send_sem=send_sems.at[sem(w, j)],
                                                recv_sem=recv_sems.at[sem(w, j)], device_id=to, device_id_type=MESH)

        def half_rows(w, h):
            half = ins[w].shape[0] // 2
            return pl.ds(pl.multiple_of(h * half, 16), half)

        for w in range(nw):
            local.append(pltpu.make_async_copy(ins[w], outs[w].at[me], local_sems.at[w]))
            for k in (1, 2, 3):
                px, py = _chip_peer(x, y, k)
                mine = half_rows(w, c)
                remote.append(push(ins[w].at[mine], outs[w].at[me, mine], w, k - 1, (px, py, c)))
        local.append(pltpu.make_async_copy(pack_ref, pack_out.at[me], local_sems.at[nw]))
        for k in (1, 2, 3):
            px, py = _chip_peer(x, y, k)
            remote.append(push(pack_ref, pack_out.at[me], nw, k - 1, (px, py, c)))
        for cp in local + remote:
            cp.start()
        for w in range(nw):
            for k in (1, 2, 3):
                landed = outs[w].at[me ^ k, half_rows(w, c)]
                push(landed, landed, w, k - 1, (x, y, c)).wait_recv()
                fw = push(landed, landed, w, 2 + k, (x, y, 1 - c))
                fw.start()
                remote.append(fw)
        for k in (1, 2, 3):
            landed = pack_out.at[me ^ k]
            push(landed, landed, nw, k - 1, (x, y, c)).wait_recv()
        for w in range(nw):
            for k in (1, 2, 3):
                landed = outs[w].at[me ^ k, half_rows(w, 1 - c)]
                push(landed, landed, w, 2 + k, (x, y, c)).wait_recv()
        for cp in remote:
            cp.wait_send()
        for cp in local:
            cp.wait()

    any_spec = pl.BlockSpec(memory_space=pl.ANY)
    out_shape = [jax.ShapeDtypeStruct((N_CHIPS,) + s.shape, s.dtype) for s in shards]
    out_shape.append(jax.ShapeDtypeStruct((N_CHIPS,) + pack.shape, pack.dtype))
    return pl.pallas_call(
        body, name="gather_weights",
        in_specs=[any_spec] * (nw + 1), out_specs=[any_spec] * (nw + 1), out_shape=out_shape,
        scratch_shapes=[pltpu.SemaphoreType.DMA(((nw + 1) * 6,)), pltpu.SemaphoreType.DMA(((nw + 1) * 6,)),
                        pltpu.SemaphoreType.DMA((nw + 1,))],
    )(*shards, pack)


def _adamw(w, g, m, v):
    m = ADAM_B1 * m + (1.0 - ADAM_B1) * g
    v = ADAM_B2 * v + (1.0 - ADAM_B2) * (g * g)
    m_hat = m / (1.0 - ADAM_B1 ** ADAM_STEP)
    v_hat = v / (1.0 - ADAM_B2 ** ADAM_STEP)
    delta = -ADAM_LR * (m_hat / (jnp.sqrt(v_hat) + ADAM_EPS) + ADAM_WD * w)
    return delta, m, v


def _scatter_alone(g16, col_sharded, name):
    def body(g_ref, slots_ref, send_sems, recv_sems):
        copies = _scatter_copies(g_ref, slots_ref, send_sems, recv_sems, 0, col_sharded)
        for cp in copies:
            cp.start()
        for cp in copies:
            cp.wait()

    any_spec = pl.BlockSpec(memory_space=pl.ANY)
    return pl.pallas_call(
        body, name=name, in_specs=[any_spec], out_specs=any_spec,
        out_shape=jax.ShapeDtypeStruct((7,) + _piece_shape(g16.shape, col_sharded), BF),
        scratch_shapes=[pltpu.SemaphoreType.DMA((7,)), pltpu.SemaphoreType.DMA((7,))],
    )(g16)


def _finish_reduce(grad, slots, col_sharded, name):
    r, cw = _piece_shape(grad.shape, col_sharded)
    chunk = 64
    assert r % chunk == 0

    def body(g_hbm, slots_ref, full, own, lsem, c_send, c_recv):
        x, y, c = _place()
        cp = pltpu.make_async_copy(g_hbm.at[_piece_window(col_sharded, r, cw, 2 * x + y, c)], own, lsem)
        cp.start()
        cp.wait()
        mine = pl.multiple_of(c * r, 8)

        def add(j, carry):
            rows = pl.ds(pl.multiple_of(j * chunk, 8), chunk)
            tot = own[rows, :]
            for k in range(7):
                tot = tot + slots_ref[k, rows, :].astype(F32)
            full[pl.ds(mine + pl.multiple_of(j * chunk, 8), chunk), :] = tot
            return carry

        lax.fori_loop(0, r // chunk, add, 0)
        half = full.at[pl.ds(mine, r), :]
        swap = pltpu.make_async_remote_copy(src_ref=half, dst_ref=half, send_sem=c_send, recv_sem=c_recv,
                                            device_id=(x, y, 1 - c), device_id_type=MESH)
        swap.start()
        swap.wait()

    vmem = pl.BlockSpec(memory_space=pltpu.VMEM)
    return pl.pallas_call(
        body, name=name, in_specs=[pl.BlockSpec(memory_space=pl.ANY), vmem], out_specs=vmem,
        out_shape=jax.ShapeDtypeStruct((2 * r, cw), F32),
        scratch_shapes=[pltpu.VMEM((r, cw), F32), pltpu.SemaphoreType.DMA, pltpu.SemaphoreType.DMA,
                        pltpu.SemaphoreType.DMA],
        compiler_params=pltpu.CompilerParams(vmem_limit_bytes=32 * MIB),
    )(grad, slots)


def _adamw_big(g, w, m, v, name):
    vr, vc = w.shape
    cw = g.shape[1]
    rows = 64

    def body(g_ref, w_ref, m_ref, v_ref, go_ref, do_ref, mo_ref, vo_ref):
        gg = g_ref[:, 0:vc]
        delta, mn, vn = _adamw(w_ref[...], gg, m_ref[...], v_ref[...])
        go_ref[...] = gg
        do_ref[...] = delta
        mo_ref[...] = mn
        vo_ref[...] = vn

    blk = pl.BlockSpec((rows, vc), lambda i: (i, 0))
    shard = jax.ShapeDtypeStruct((vr, vc), F32)
    return pl.pallas_call(
        body, name=name, grid=(vr // rows,),
        in_specs=[pl.BlockSpec((rows, cw), lambda i: (i, 0)), blk, blk, blk], out_specs=[blk] * 4,
        out_shape=[shard] * 4, compiler_params=_params(32),
    )(g, w, m, v)


def _allreduce_small(pack):
    rows = pack.shape[0]

    def body(p_ref, o_ref, slots, send_sems, recv_sems):
        x, y, c = _place()
        me = 4 * x + 2 * y + c
        slots[me] = p_ref[...]
        sends = []
        for k in range(1, 8):
            cp = pltpu.make_async_remote_copy(
                src_ref=p_ref, dst_ref=slots.at[me], send_sem=send_sems.at[k - 1], recv_sem=recv_sems.at[k - 1],
                device_id=(x ^ (k >> 2), y ^ ((k >> 1) & 1), c ^ (k & 1)), device_id_type=MESH)
            cp.start()
            sends.append(cp)
        for cp in sends:
            cp.wait()
        tot = slots[0]
        for j in range(1, 8):
            tot = tot + slots[j]
        o_ref[...] = tot

    vmem = pl.BlockSpec(memory_space=pltpu.VMEM)
    return pl.pallas_call(
        body, name="allreduce_small", in_specs=[vmem], out_specs=vmem,
        out_shape=jax.ShapeDtypeStruct(pack.shape, F32),
        scratch_shapes=[pltpu.VMEM((8, rows, D_MODEL), F32), pltpu.SemaphoreType.DMA((7,)),
                        pltpu.SemaphoreType.DMA((7,))],
    )(pack)


def _adamw_small(ws, gs, ms, vs):
    n = len(ws)

    def body(*refs):
        w_refs, g_refs, m_refs, v_refs = refs[0:n], refs[n:2 * n], refs[2 * n:3 * n], refs[3 * n:4 * n]
        d_refs, mo_refs, vo_refs = refs[4 * n:5 * n], refs[5 * n:6 * n], refs[6 * n:7 * n]
        for j in range(n):
            delta, mn, vn = _adamw(w_refs[j][...], g_refs[j][...], m_refs[j][...], v_refs[j][...])
            d_refs[j][...] = delta
            mo_refs[j][...] = mn
            vo_refs[j][...] = vn

    vmem = pl.BlockSpec(memory_space=pltpu.VMEM)
    shapes = [jax.ShapeDtypeStruct(w.shape, F32) for w in ws]
    outs = pl.pallas_call(
        body, name="adamw_small", in_specs=[vmem] * (4 * n), out_specs=[vmem] * (3 * n), out_shape=shapes * 3,
    )(*ws, *gs, *ms, *vs)
    return outs[0:n], outs[n:2 * n], outs[2 * n:3 * n]


def _pad_slab(a, axis):
    pad = [(0, 0)] * a.ndim
    pad[axis] = (0, D_FF_SLAB - D_FF_SHARD)
    return jnp.pad(a, pad)


def _local_step(x, p, target, wts, late=None):
    (win4, wout, wg4, wu4, wd4, wpg, wpp4, conv_w, fcw, g_mix, conv_b, gq, gk, g_oc, g_oa, g_ffn, fcb, g_ple) = wts
    comm = late is not None
    gm = jnp.kron(jnp.eye(N_HEADS, dtype=F32), jnp.full((HEAD_DIM, HEAD_DIM), 1.0 / HEAD_DIM, F32)).astype(BF)
    gq8, gk8 = jnp.tile(gq, (1, N_HEADS)), jnp.tile(gk, (1, N_HEADS))
    mb, mbk = _mask_tables()
    zbcx, qkv, ycn, qkn = _fwd_mix(x, g_mix, win4, conv_w, conv_b, g_oc, gm, gq8, gk8, 512)
    ya, lse, gathered = _attn_fwd(qkn, qkv, mb, late[0:3] if comm else ())
    if comm:
        wout, wg4, wu4 = gathered
        wout = wout.reshape(D_MODEL, D_MODEL)
    x1, gp, up, act, ycat, h2, gathered = _fwd_ffn(x, ycn, ya, wout, wg4, wu4, g_oa, g_ffn, fcw, fcb, 256,
                                                    late[3:6] if comm else ())
    if comm:
        wd4, wpg, wpp4 = gathered
        wpg = wpg.reshape(D_MODEL, D_MODEL)
    dx2, h3, ds, dpp, dg_ple, loss = _fwd_tail(x1, act, p, target, wd4, wpg, wpp4, g_ple, 256)
    big, big16, slots = {}, {}, {}

    def wgrad(name, a, b, tn):
        big[name], big16[name] = _wgrad(a, b, tn, 512, "wgrad_" + name)
        return (big16[name], _COL_SHARDED[name])

    riders = [wgrad("w_down", act, dx2, 512), wgrad("w_ple_gate", h3, ds, 1024), wgrad("w_ple_proj", p, dpp, 1024)]
    dgp, dup, dfcw, dfcb, got = _bwd_ffn_a(dx2, gp, up, wd4, fcw, fcb, 256, riders if comm else ())
    slots.update(zip(("w_down", "w_ple_gate", "w_ple_proj"), got))
    riders = [wgrad("w_gate", h2, dgp, 1536), wgrad("w_up", h2, dup, 1536)]
    dx1, dycn, dya, dg_ffn, dg_oa = _bwd_ffn_b(dgp, dup, dx2, x1, ya, wg4, wu4, wout, g_ffn, g_oa, 256)
    riders.append(wgrad("w_out", ycat, dx1, 1024))
    dq, dk, dv, got = _attn_bwd(qkn, qkv, ya, lse, dya, mb, mbk, riders if comm else ())
    slots.update(zip(("w_gate", "w_up", "w_out"), got))
    grad_x, h1, dz, dcw, dcb, dg_oc, dg_mix, dgq8, dgk8 = _bwd_mix(
        x, dx1, zbcx, qkv, dycn, dq, dk, dv, win4, conv_w, conv_b, g_oc, g_mix, gm, gq8, gk8, 256)
    g16, cs = wgrad("w_in", h1, dz, 1536)
    if comm:
        slots["w_in"] = _scatter_alone(g16, cs, "scatter_w_in")
    dgq = dgq8.reshape(N_HEADS, HEAD_DIM).sum(axis=0, keepdims=True)
    dgk = dgk8.reshape(N_HEADS, HEAD_DIM).sum(axis=0, keepdims=True)
    small = dict(g_mix=dg_mix, conv_w=dcw, conv_b=dcb, q_norm_g=dgq, k_norm_g=dgk, g_out_conv=dg_oc,
                 g_out_attn=dg_oa, g_ffn=dg_ffn, ffn_conv_w=dfcw, ffn_conv_b=dfcb, g_ple=dg_ple)
    return loss[0, 0], grad_x, big, slots, small


_SMALL_ROWS = 24


def _pack_small(s):
    z64 = jnp.zeros((1, 1024 - 512 - 128), F32)
    rows = [s["g_mix"], s["g_ffn"], s["g_ple"],
            jnp.concatenate([s["conv_b"], s["g_out_conv"]], axis=1),
            jnp.concatenate([s["g_out_attn"], s["q_norm_g"], s["k_norm_g"], z64], axis=1),
            jnp.pad(s["conv_w"], ((0, 0), (0, 512))),
            s["ffn_conv_b"].reshape(3, 1024),
            s["ffn_conv_w"].reshape(9, 1024),
            jnp.zeros((_SMALL_ROWS - 20, 1024), F32)]
    return jnp.concatenate(rows, axis=0)


def _unpack_small(t):
    return dict(g_mix=t[0:1], g_ffn=t[1:2], g_ple=t[2:3], conv_b=t[3:4, 0:512], g_out_conv=t[3:4, 512:1024],
                g_out_attn=t[4:5, 0:512], q_norm_g=t[4:5, 512:576], k_norm_g=t[4:5, 576:640],
                conv_w=t[5:8, 0:512], ffn_conv_b=t[8:11].reshape(1, D_FF_PAD), ffn_conv_w=t[11:20].reshape(3, D_FF_PAD))


def _unpad_ff(a):
    r = a.shape[0]
    return a.reshape(r, N_CHIPS, D_FF_SLAB)[:, :, :D_FF_SHARD].reshape(r, N_CHIPS * D_FF_SHARD)


_BIG = ("w_in", "w_out", "w_gate", "w_up", "w_down", "w_ple_gate", "w_ple_proj")
_COL_SHARDED = dict(w_in=True, w_out=False, w_gate=True, w_up=True, w_down=False, w_ple_gate=False, w_ple_proj=True)
_WEIGHTS = ("g_mix", "w_in", "conv_w", "conv_b", "q_norm_g", "k_norm_g", "g_out_conv", "g_out_attn", "w_out",
            "g_ffn", "w_gate", "w_up", "ffn_conv_w", "ffn_conv_b", "w_down", "g_ple", "w_ple_gate", "w_ple_proj")


def kernel(x, p, g_mix, w_in, conv_w, conv_b, q_norm_g, k_norm_g, g_out_conv, g_out_attn, w_out, g_ffn, w_gate, w_up, ffn_conv_w, ffn_conv_b, w_down, g_ple, w_ple_gate, w_ple_proj, loss_target, m_g_mix, m_w_in, m_conv_w, m_conv_b, m_q_norm_g, m_k_norm_g, m_g_out_conv, m_g_out_attn, m_w_out, m_g_ffn, m_w_gate, m_w_up, m_ffn_conv_w, m_ffn_conv_b, m_w_down, m_g_ple, m_w_ple_gate, m_w_ple_proj, v_g_mix, v_w_in, v_conv_w, v_conv_b, v_q_norm_g, v_k_norm_g, v_g_out_conv, v_g_out_attn, v_w_out, v_g_ffn, v_w_gate, v_w_up, v_ffn_conv_w, v_ffn_conv_b, v_w_down, v_g_ple, v_w_ple_gate, v_w_ple_proj):
    w = dict(g_mix=g_mix, w_in=w_in, conv_w=conv_w, conv_b=conv_b, q_norm_g=q_norm_g, k_norm_g=k_norm_g,
             g_out_conv=g_out_conv, g_out_attn=g_out_attn, w_out=w_out, g_ffn=g_ffn, w_gate=w_gate, w_up=w_up,
             ffn_conv_w=ffn_conv_w, ffn_conv_b=ffn_conv_b, w_down=w_down, g_ple=g_ple, w_ple_gate=w_ple_gate,
             w_ple_proj=w_ple_proj)
    m = dict(g_mix=m_g_mix, w_in=m_w_in, conv_w=m_conv_w, conv_b=m_conv_b, q_norm_g=m_q_norm_g, k_norm_g=m_k_norm_g,
             g_out_conv=m_g_out_conv, g_out_attn=m_g_out_attn, w_out=m_w_out, g_ffn=m_g_ffn, w_gate=m_w_gate,
             w_up=m_w_up, ffn_conv_w=m_ffn_conv_w, ffn_conv_b=m_ffn_conv_b, w_down=m_w_down, g_ple=m_g_ple,
             w_ple_gate=m_w_ple_gate, w_ple_proj=m_w_ple_proj)
    v = dict(g_mix=v_g_mix, w_in=v_w_in, conv_w=v_conv_w, conv_b=v_conv_b, q_norm_g=v_q_norm_g, k_norm_g=v_k_norm_g,
             g_out_conv=v_g_out_conv, g_out_attn=v_g_out_attn, w_out=v_w_out, g_ffn=v_g_ffn, w_gate=v_w_gate,
             w_up=v_w_up, ffn_conv_w=v_ffn_conv_w, ffn_conv_b=v_ffn_conv_b, w_down=v_w_down, g_ple=v_g_ple,
             w_ple_gate=v_w_ple_gate, w_ple_proj=v_w_ple_proj)
    mats = [k for k, a in w.items() if a.ndim == 3]
    w = {k: (a[0] if k in mats else a) for k, a in w.items()}
    m = {k: (a[0] if k in mats else a) for k, a in m.items()}
    v = {k: (a[0] if k in mats else a) for k, a in v.items()}
    chip = 2 * lax.axis_index("x") + lax.axis_index("y")

    late = [w["w_out"].astype(BF), _pad_slab(w["w_gate"], 1).astype(BF), _pad_slab(w["w_up"], 1).astype(BF),
            _pad_slab(w["w_down"], 0).astype(BF), w["w_ple_gate"].astype(BF), w["w_ple_proj"].astype(BF)]
    pack = jnp.pad(jnp.concatenate([w["conv_w"], _pad_slab(w["ffn_conv_w"], 1)], axis=1), ((0, 5), (0, 128)))
    win4, pack4 = _gather_weights([w["w_in"].astype(BF)], pack)
    conv_w_full = pack4[:, 0:3, 0:128].transpose(1, 0, 2).reshape(3, CONV_W)
    fcw_full = pack4[:, 0:3, 128:128 + D_FF_SLAB].transpose(1, 0, 2).reshape(3, D_FF_PAD)
    fcb_pad = _pad_slab(w["ffn_conv_b"].reshape(N_CHIPS, D_FF_SHARD), 1).reshape(1, D_FF_PAD)
    wts = (win4, None, None, None, None, None, None, conv_w_full, fcw_full, w["g_mix"], w["conv_b"], w["q_norm_g"],
           w["k_norm_g"], w["g_out_conv"], w["g_out_attn"], w["g_ffn"], fcb_pad, w["g_ple"])

    loss, grad_x, big, slots, small = _local_step(x[0], p[0, 0], loss_target[0], wts, late)
    loss = lax.psum(loss, ("x", "y", "c"))

    grads, deltas, new_m, new_v = {}, {}, {}, {}
    for name in _BIG:
        total = _finish_reduce(big[name], slots[name], _COL_SHARDED[name], "finish_" + name)
        grads[name], deltas[name], new_m[name], new_v[name] = _adamw_big(total, w[name], m[name], v[name],
                                                                         "adamw_" + name)
    tot = _unpack_small(_allreduce_small(_pack_small(small)))
    tot["conv_w"] = lax.dynamic_slice_in_dim(tot["conv_w"], chip * 128, 128, axis=1)
    tot["ffn_conv_w"] = lax.dynamic_slice_in_dim(tot["ffn_conv_w"], chip * D_FF_SLAB, D_FF_SLAB, axis=1)[:, :D_FF_SHARD]
    tot["ffn_conv_b"] = _unpad_ff(tot["ffn_conv_b"])
    names = [n for n in _WEIGHTS if n not in _BIG]
    d_s, m_s, v_s = _adamw_small([w[n] for n in names], [tot[n] for n in names], [m[n] for n in names],
                                 [v[n] for n in names])
    for j, n in enumerate(names):
        grads[n], deltas[n], new_m[n], new_v[n] = tot[n], d_s[j], m_s[j], v_s[j]

    out = [loss, grad_x[None]]
    for group in (grads, deltas, new_m, new_v):
        out += [group[n][None] if n in mats else group[n] for n in _WEIGHTS]
    return tuple(out)
```

```python
import jax
import jax.numpy as jnp
from jax import lax
from jax.experimental import pallas as pl
from jax.experimental.pallas import tpu as pltpu

D_MODEL = 1024
CONV_W = 512
N_HEADS = 8
HEAD_DIM = 64
ATTN_W = 512
D_FF = 2816
D_FF_SHARD = 704
FF_SLABS = ((0, 1408), (1408, 2816))
IN_SLAB = 768
PLE_DIM = 256
N_CHIPS = 4
QBLK = 128
DILATIONS = (1, 4, 16)
EPS = 1e-6
NEG = -1e30
MESH = pl.DeviceIdType.MESH

ADAM_LR = 0.001
ADAM_B1 = 0.9
ADAM_B2 = 0.999
ADAM_EPS = 1e-08
ADAM_WD = 0.01
ADAM_STEP = 10

BF = jnp.bfloat16
F32 = jnp.float32
MIB = 1024 * 1024


def _mm(a, b):
    return jnp.dot(a, b, preferred_element_type=F32)


def _mm_nt(a, b):
    return lax.dot_general(a, b, (((1,), (1,)), ((), ())), preferred_element_type=F32)


def _mm_tn(a, b):
    return lax.dot_general(a, b, (((0,), (0,)), ((), ())), preferred_element_type=F32)


def _rstd(a):
    return lax.rsqrt(jnp.mean(a * a, axis=-1, keepdims=True) + EPS)


def _norm_bwd(dy, xh, r, g):
    dxh = dy * g
    return r * (dxh - xh * jnp.mean(dxh * xh, axis=-1, keepdims=True))


def _colsum(a):
    return jnp.sum(a, axis=0, keepdims=True)


def _head_mean(a, gm_ref):
    hi = a.astype(BF)
    lo = (a - hi.astype(F32)).astype(BF)
    return _mm(hi, gm_ref[...]) + _mm(lo, gm_ref[...])


def _shift_down(buf, k, tm):
    return pltpu.roll(buf, k, axis=0)[8:8 + tm]


def _shift_up(buf, k, tm):
    return pltpu.roll(buf, tm + 8 - k, axis=0)[0:tm]


def _params(vmem_mib, n_grid=1):
    return pltpu.CompilerParams(dimension_semantics=("arbitrary",) * n_grid, vmem_limit_bytes=vmem_mib * MIB)


def _const(shape):
    n = len(shape)
    return pl.BlockSpec(shape, lambda *_: (0,) * n, pipeline_mode=pl.Buffered(1))


def _rows(tm, width, rev_of=None):
    if rev_of is None:
        return pl.BlockSpec((tm, width), lambda i: (i, 0))
    return pl.BlockSpec((tm, width), lambda i: (rev_of - 1 - i, 0))


def _halo(tm, width, nt):
    return pl.BlockSpec((8, width), lambda i: (jnp.maximum((nt - 1 - i) * (tm // 8) - 1, 0), 0))


def _fwd_mix(x, g_mix, win4, conv_w, conv_b, g_oc, gm, gq8, gk8, tm):
    t = x.shape[0]
    nt = t // tm

    def body(x_ref, g_ref, w_ref, cw_ref, cb_ref, goc_ref, gm_ref, gq_ref, gk_ref,
             zbcx_ref, qkv_ref, ycn_ref, qkn_ref, ubuf):
        @pl.when(pl.program_id(0) == 0)
        def _():
            ubuf[0:8, :] = jnp.zeros((8, CONV_W), F32)

        xt = x_ref[...]
        h = ((xt * _rstd(xt)) * g_ref[...]).astype(BF)
        zbcx_ref[:, 0:IN_SLAB] = _mm(h, w_ref[0])
        zbcx_ref[:, IN_SLAB:2 * IN_SLAB] = _mm(h, w_ref[1])
        qkv_ref[:, 0:IN_SLAB] = _mm(h, w_ref[2])
        qkv_ref[:, IN_SLAB:2 * IN_SLAB] = _mm(h, w_ref[3])
        u = zbcx_ref[:, 512:1024] * zbcx_ref[:, 1024:1536]
        ubuf[8:8 + tm, :] = u
        ub = ubuf[...]
        cv = (cw_ref[0:1, :] * _shift_down(ub, 2, tm) + cw_ref[1:2, :] * _shift_down(ub, 1, tm)
              + cw_ref[2:3, :] * u + cb_ref[...])
        ubuf[0:8, :] = ubuf[tm:tm + 8, :]
        yc = zbcx_ref[:, 0:512] * cv
        ycn_ref[...] = ((yc * _rstd(yc)) * goc_ref[...]).astype(BF)
        zq = qkv_ref[:, 0:512]
        zk = qkv_ref[:, 512:1024]
        rq = lax.rsqrt(_head_mean(zq * zq, gm_ref) + EPS)
        rk = lax.rsqrt(_head_mean(zk * zk, gm_ref) + EPS)
        qkn_ref[:, 0:512] = ((zq * rq) * gq_ref[...]) * (HEAD_DIM ** -0.5)
        qkn_ref[:, 512:1024] = (zk * rk) * gk_ref[...]

    return pl.pallas_call(
        body, name="fwd_mix", grid=(nt,),
        in_specs=[_rows(tm, D_MODEL), _const((1, D_MODEL)), _const((N_CHIPS, D_MODEL, IN_SLAB)),
                  _const((3, CONV_W)), _const((1, CONV_W)), _const((1, CONV_W)), _const((ATTN_W, ATTN_W)),
                  _const((1, ATTN_W)), _const((1, ATTN_W))],
        out_specs=[_rows(tm, 1536), _rows(tm, 1536), _rows(tm, CONV_W), _rows(tm, 1024)],
        out_shape=[jax.ShapeDtypeStruct((t, 1536), F32), jax.ShapeDtypeStruct((t, 1536), F32),
                   jax.ShapeDtypeStruct((t, CONV_W), BF), jax.ShapeDtypeStruct((t, 1024), F32)],
        scratch_shapes=[pltpu.VMEM((tm + 8, CONV_W), F32)],
        compiler_params=_params(48),
    )(x, g_mix, win4, conv_w, conv_b, g_oc, gm, gq8, gk8)


def _place():
    x, y, c = lax.axis_index("x"), lax.axis_index("y"), lax.axis_index("c")
    return x, y, c


def _chip_peer(x, y, k):
    return x ^ (k >> 1), y ^ (k & 1)


def _piece_shape(grad_shape, col_sharded):
    kk, nn = grad_shape
    return (kk // 2, nn // N_CHIPS) if col_sharded else (kk // (2 * N_CHIPS), nn)


def _piece_window(col_sharded, r, cw, s, h):
    if col_sharded:
        return (pl.ds(pl.multiple_of(h * r, 16), r), pl.ds(pl.multiple_of(s * cw, 128), cw))
    return (pl.ds(pl.multiple_of((2 * s + h) * r, 16), r), slice(None))


def _scatter_copies(g_ref, slots_ref, send_sems, recv_sems, base, col_sharded):
    x, y, c = _place()
    r, cw = slots_ref.shape[1:]
    copies = []
    for k in range(1, 8):
        tx, ty, tc = x ^ (k >> 2), y ^ ((k >> 1) & 1), c ^ (k & 1)
        copies.append(pltpu.make_async_remote_copy(
            src_ref=g_ref.at[_piece_window(col_sharded, r, cw, 2 * tx + ty, tc)], dst_ref=slots_ref.at[k - 1],
            send_sem=send_sems.at[base + k - 1], recv_sem=recv_sems.at[base + k - 1],
            device_id=(tx, ty, tc), device_id_type=MESH))
    return copies


def _ride_scatter(first, last, riders, g_refs, slot_refs, send_sems, recv_sems):
    def all_copies():
        out = []
        for j, (_, col_sharded) in enumerate(riders):
            out += _scatter_copies(g_refs[j], slot_refs[j], send_sems, recv_sems, 7 * j, col_sharded)
        return out

    @pl.when(first)
    def _():
        for cp in all_copies():
            cp.start()

    @pl.when(last)
    def _():
        for cp in all_copies():
            cp.wait()


def _rider_specs(riders):
    any_spec = pl.BlockSpec(memory_space=pl.ANY)
    shapes = [jax.ShapeDtypeStruct((7,) + _piece_shape(g.shape, cs), BF) for g, cs in riders]
    sems = [pltpu.SemaphoreType.DMA((7 * len(riders),)), pltpu.SemaphoreType.DMA((7 * len(riders),))] if riders else []
    return [any_spec] * len(riders), shapes, sems


class _Gather:
    def __init__(self, ins, outs, send_sems, recv_sems, local_sems):
        self.ins, self.outs = ins, outs
        self.send_sems, self.recv_sems, self.local_sems = send_sems, recv_sems, local_sems
        self.x, self.y, self.c = _place()
        self.me = 2 * self.x + self.y

    def _push(self, src, dst, w, j, to):
        return pltpu.make_async_remote_copy(src_ref=src, dst_ref=dst, send_sem=self.send_sems.at[6 * w + j],
                                            recv_sem=self.recv_sems.at[6 * w + j], device_id=to, device_id_type=MESH)

    def _half(self, w, h):
        half = self.ins[w].shape[0] // 2
        return pl.ds(pl.multiple_of(h * half, 16), half)

    def _local(self, w):
        return pltpu.make_async_copy(self.ins[w], self.outs[w].at[self.me], self.local_sems.at[w])

    def _ici(self, w, k):
        px, py = _chip_peer(self.x, self.y, k)
        mine = self._half(w, self.c)
        return self._push(self.ins[w].at[mine], self.outs[w].at[self.me, mine], w, k - 1, (px, py, self.c))

    def _landed(self, w, k, h):
        return self.outs[w].at[self.me ^ k, self._half(w, h)]

    def _fwd(self, w, k):
        landed = self._landed(w, k, self.c)
        return self._push(landed, landed, w, 2 + k, (self.x, self.y, 1 - self.c))

    def start(self):
        for w in range(len(self.ins)):
            self._local(w).start()
            for k in (1, 2, 3):
                self._ici(w, k).start()

    def forward(self):
        for w in range(len(self.ins)):
            for k in (1, 2, 3):
                landed = self._landed(w, k, self.c)
                self._push(landed, landed, w, k - 1, (self.x, self.y, self.c)).wait_recv()
                self._fwd(w, k).start()

    def finish(self):
        for w in range(len(self.ins)):
            for k in (1, 2, 3):
                landed = self._landed(w, k, 1 - self.c)
                self._push(landed, landed, w, 2 + k, (self.x, self.y, self.c)).wait_recv()
            for k in (1, 2, 3):
                self._ici(w, k).wait_send()
                self._fwd(w, k).wait_send()
            self._local(w).wait()


def _alibi(h):
    return 2.0 ** (-(h + 1))


CHUNK = 2048


def _mask_tables():
    slopes = jnp.asarray([_alibi(h) for h in range(N_HEADS)], F32)[:, None, None]

    def table(step):
        valid = (step >= 0) & (step <= QBLK)
        return jnp.stack([jnp.where(valid[None], -slopes * (step * d)[None].astype(F32), NEG) for d in DILATIONS])

    i = jnp.arange(QBLK)[:, None]
    j2 = jnp.arange(2 * QBLK)[None, :]
    r2 = jnp.arange(2 * QBLK)[:, None]
    j = jnp.arange(QBLK)[None, :]
    fwd, bwd = table(i + QBLK - j2), table(r2 - j)
    return fwd.reshape(3, N_HEADS // 2, 2 * QBLK, 2 * QBLK), bwd.reshape(3, N_HEADS // 2, 4 * QBLK, QBLK)


def _attn_fwd(qkn, qkv, mb, late=()):
    t = qkn.shape[0]
    nc = t // CHUNK
    nl = len(late)

    def body(*refs):
        qc_ref, kp_ref, kc_ref, vp_ref, vc_ref, mb_ref = refs[0:6]
        o_ref, l_ref = refs[6 + nl:8 + nl]
        ob0, ob1, ob2, lb0, lb1, lb2 = refs[8 + 2 * nl:14 + 2 * nl]
        if nl:
            gather = _Gather(refs[6:6 + nl], refs[8 + nl:8 + 2 * nl], *refs[14 + 2 * nl:17 + 2 * nl])
            step = pl.program_id(0) * nc + pl.program_id(1)
            pl.when(step == 0)(gather.start)
            pl.when(step == 2 * nc)(gather.forward)
            pl.when(step == (N_HEADS // 2) * nc - 1)(gather.finish)
        first = pl.program_id(1) == 0
        lane = lax.broadcasted_iota(jnp.int32, (QBLK, 128), 1)
        lo_half = lane < HEAD_DIM
        kj = lax.broadcasted_iota(jnp.int32, (2 * QBLK, 2 * QBLK), 1)
        no_prev = first & (kj < QBLK)
        obs, lbs = (ob0, ob1, ob2), (lb0, lb1, lb2)

        def by_head(a):
            return jnp.where(lo_half, a, 0.0).astype(BF), jnp.where(lo_half, 0.0, a).astype(BF)

        for di, d in enumerate(DILATIONS):
            span = d * QBLK
            for r in range(d):
                tail = pl.ds(CHUNK - span + r, QBLK, stride=d)
                k_prev = kp_ref[tail, :].astype(BF)
                v_prev = by_head(vp_ref[tail, :])
                for b in range(CHUNK // span):
                    rows = pl.ds(r + span * b, QBLK, stride=d)
                    q0, q1 = by_head(qc_ref[rows, :])
                    k_cur = kc_ref[rows, :].astype(BF)
                    v_cur = by_head(vc_ref[rows, :])
                    s = _mm_nt(jnp.concatenate([q0, q1], axis=0), jnp.concatenate([k_prev, k_cur], axis=0))
                    s = s + mb_ref[di, 0]
                    if b == 0:
                        s = jnp.where(no_prev, NEG, s)
                    m = jnp.max(s, axis=-1, keepdims=True)
                    e = jnp.exp(s - m)
                    den = jnp.sum(e, axis=-1, keepdims=True)
                    eb = e.astype(BF)
                    o = _mm(jnp.concatenate([eb[0:QBLK], eb[QBLK:2 * QBLK]], axis=1),
                            jnp.concatenate([v_prev[0], v_cur[0], v_prev[1], v_cur[1]], axis=0))
                    inv = 1.0 / den
                    lse = m + jnp.log(den)
                    obs[di][rows, :] = o * jnp.where(lo_half, inv[0:QBLK], inv[QBLK:2 * QBLK])
                    lbs[di][rows, :] = jnp.where(lo_half, lse[0:QBLK], lse[QBLK:2 * QBLK])
                    k_prev, v_prev = k_cur, v_cur
        for c0 in range(0, CHUNK, 256):
            rs = slice(c0, c0 + 256)
            l0, l1, l2 = lb0[rs, :], lb1[rs, :], lb2[rs, :]
            mx = jnp.maximum(jnp.maximum(l0, l1), l2)
            w0, w1, w2 = jnp.exp(l0 - mx), jnp.exp(l1 - mx), jnp.exp(l2 - mx)
            tot = w0 + w1 + w2
            o_ref[rs, :] = (ob0[rs, :] * w0 + ob1[rs, :] * w1 + ob2[rs, :] * w2) / tot
            l_ref[rs, :] = mx + jnp.log(tot)

    def cur(col):
        return pl.BlockSpec((CHUNK, 128), lambda hp, n: (n, col + hp))

    def prv(col):
        return pl.BlockSpec((CHUNK, 128), lambda hp, n: (jnp.maximum(n - 1, 0), col + hp))

    out = pl.BlockSpec((CHUNK, 128), lambda hp, n: (n, hp))
    any_spec = pl.BlockSpec(memory_space=pl.ANY)
    sems = [pltpu.SemaphoreType.DMA((6 * nl,)), pltpu.SemaphoreType.DMA((6 * nl,)), pltpu.SemaphoreType.DMA((nl,))]
    res = pl.pallas_call(
        body, name="attn_fwd", grid=(N_HEADS // 2, nc),
        in_specs=[cur(0), prv(4), cur(4), prv(8), cur(8),
                  pl.BlockSpec((3, 1, 2 * QBLK, 2 * QBLK), lambda hp, n: (0, hp, 0, 0))] + [any_spec] * nl,
        out_specs=[out, out] + [any_spec] * nl,
        out_shape=[jax.ShapeDtypeStruct((t, ATTN_W), F32)] * 2
        + [jax.ShapeDtypeStruct((N_CHIPS,) + w.shape, w.dtype) for w in late],
        scratch_shapes=[pltpu.VMEM((CHUNK, 128), F32)] * 6 + (sems if nl else []),
        compiler_params=_params(48, 2),
    )(qkn, qkn, qkn, qkv, qkv, mb, *late)
    return res[0], res[1], list(res[2:])


def _attn_bwd(qkn, qkv, o, lse, do, mb, mbk, riders=()):
    t = qkn.shape[0]
    nc = t // CHUNK
    nr = len(riders)

    def body(*refs):
        (qc_ref, qn_ref, kp_ref, kc_ref, vp_ref, vc_ref, oc_ref, on_ref, lc_ref, ln_ref, dc_ref, dn_ref,
         mb_ref, mbk_ref) = refs[0:14]
        dq_ref, dk_ref, dv_ref = refs[14 + nr:17 + nr]
        if nr:
            step = pl.program_id(0) * nc + pl.program_id(1)
            _ride_scatter(step == 0, step == (N_HEADS // 2) * nc - 1, riders, refs[14:14 + nr],
                          refs[17 + nr:17 + 2 * nr], *refs[17 + 2 * nr:19 + 2 * nr])
        first = pl.program_id(1) == 0
        last = pl.program_id(1) == nc - 1
        lane = lax.broadcasted_iota(jnp.int32, (QBLK, 128), 1)
        lo_half = lane < HEAD_DIM
        kj = lax.broadcasted_iota(jnp.int32, (2 * QBLK, 2 * QBLK), 1)
        no_prev = first & (kj < QBLK)
        ri = lax.broadcasted_iota(jnp.int32, (4 * QBLK, QBLK), 0)
        no_next = last & ((ri & (2 * QBLK - 1)) >= QBLK)

        def by_head(a):
            return jnp.where(lo_half, a, 0.0).astype(BF), jnp.where(lo_half, 0.0, a).astype(BF)

        def query_side(q_ref, d_ref, o_ref_, l_ref_, rows):
            dvals = d_ref[rows, :]
            dd = dvals * o_ref_[rows, :]
            lv = l_ref_[rows, :]
            d0 = jnp.sum(jnp.where(lo_half, dd, 0.0), axis=-1, keepdims=True)
            d1 = jnp.sum(jnp.where(lo_half, 0.0, dd), axis=-1, keepdims=True)
            l0 = jnp.max(jnp.where(lo_half, lv, NEG), axis=-1, keepdims=True)
            l1 = jnp.max(jnp.where(lo_half, NEG, lv), axis=-1, keepdims=True)
            return by_head(q_ref[rows, :]), by_head(dvals), (l0, l1), (d0, d1)

        for di, d in enumerate(DILATIONS):
            span = d * QBLK
            nbk = CHUNK // span
            for r in range(d):
                tail = pl.ds(CHUNK - span + r, QBLK, stride=d)
                k_prev = kp_ref[tail, :]
                kb_prev, km_prev = k_prev.astype(BF), by_head(k_prev)
                vb_prev = vp_ref[tail, :].astype(BF)
                rows0 = pl.ds(r, QBLK, stride=d)
                cur = query_side(qc_ref, dc_ref, oc_ref, lc_ref, rows0)
                for b in range(nbk):
                    rows = pl.ds(r + span * b, QBLK, stride=d)
                    if b == nbk - 1:
                        nxt = query_side(qn_ref, dn_ref, on_ref, ln_ref, rows0)
                    else:
                        nxt = query_side(qc_ref, dc_ref, oc_ref, lc_ref, pl.ds(r + span * (b + 1), QBLK, stride=d))
                    (q0, q1), (do0, do1), (l0, l1), (d0, d1) = cur
                    (qx0, qx1), (dox0, dox1), (lx0, lx1), (dx0, dx1) = nxt
                    k_cur = kc_ref[rows, :]
                    kb_cur, km_cur = k_cur.astype(BF), by_head(k_cur)
                    vb_cur = vc_ref[rows, :].astype(BF)
                    k2 = jnp.concatenate([kb_prev, kb_cur], axis=0)
                    v2 = jnp.concatenate([vb_prev, vb_cur], axis=0)
                    s = _mm_nt(jnp.concatenate([q0, q1], axis=0), k2) + mb_ref[di, 0]
                    if b == 0:
                        s = jnp.where(no_prev, NEG, s)
                    p = jnp.exp(s - jnp.concatenate([l0, l1], axis=0))
                    dp = _mm_nt(jnp.concatenate([do0, do1], axis=0), v2)
                    ds = (p * (dp - jnp.concatenate([d0, d1], axis=0))).astype(BF)
                    dq2 = _mm(jnp.concatenate([ds[0:QBLK], ds[QBLK:2 * QBLK]], axis=1),
                              jnp.concatenate([km_prev[0], km_cur[0], km_prev[1], km_cur[1]], axis=0))
                    if di == 0:
                        dq_ref[rows, :] = dq2
                    else:
                        dq_ref[rows, :] = dq_ref[rows, :] + dq2
                    q4 = jnp.concatenate([q0, qx0, q1, qx1], axis=0)
                    do4 = jnp.concatenate([do0, dox0, do1, dox1], axis=0)
                    s = _mm_nt(q4, kb_cur) + mbk_ref[di, 0]
                    if b == nbk - 1:
                        s = jnp.where(no_next, NEG, s)
                    p = jnp.exp(s - jnp.concatenate([l0, lx0, l1, lx1], axis=0))
                    dv2 = _mm_tn(p.astype(BF), do4)
                    dp = _mm_nt(do4, vb_cur)
                    ds = (p * (dp - jnp.concatenate([d0, dx0, d1, dx1], axis=0))).astype(BF)
                    dk2 = _mm_tn(ds, q4)
                    if di == 0:
                        dk_ref[rows, :] = dk2
                        dv_ref[rows, :] = dv2
                    else:
                        dk_ref[rows, :] = dk_ref[rows, :] + dk2
                        dv_ref[rows, :] = dv_ref[rows, :] + dv2
                    cur = nxt
                    kb_prev, km_prev, vb_prev = kb_cur, km_cur, vb_cur

    def at(shift, col):
        return pl.BlockSpec((CHUNK, 128), lambda hp, n: (jnp.clip(n + shift, 0, nc - 1), col + hp))

    out = pl.BlockSpec((CHUNK, 128), lambda hp, n: (n, hp))
    r_in, r_out, r_sems = _rider_specs(riders)
    res = pl.pallas_call(
        body, name="attn_bwd", grid=(N_HEADS // 2, nc),
        in_specs=[at(0, 0), at(1, 0), at(-1, 4), at(0, 4), at(-1, 8), at(0, 8),
                  at(0, 0), at(1, 0), at(0, 0), at(1, 0), at(0, 0), at(1, 0),
                  pl.BlockSpec((3, 1, 2 * QBLK, 2 * QBLK), lambda hp, n: (0, hp, 0, 0)),
                  pl.BlockSpec((3, 1, 4 * QBLK, QBLK), lambda hp, n: (0, hp, 0, 0))] + r_in,
        out_specs=[out, out, out] + r_in,
        out_shape=[jax.ShapeDtypeStruct((t, ATTN_W), F32)] * 3 + r_out,
        scratch_shapes=r_sems,
        compiler_params=_params(56, 2),
    )(qkn, qkn, qkn, qkn, qkv, qkv, o, o, lse, lse, do, do, mb, mbk, *[g for g, _ in riders])
    return res[0], res[1], res[2], list(res[3:])


def _fwd_ffn(x, ycn, ya, wout, wg4, wu4, g_oa, g_ffn, fcw, fcb, tm, late=()):
    t = x.shape[0]
    nt = t // tm
    nl = len(late)

    def body(*refs):
        x_ref, ycn_ref, ya_ref, wout_ref, wg_ref, wu_ref, goa_ref, gffn_ref, fcw_ref, fcb_ref = refs[0:10]
        x1_ref, gp_ref, up_ref, act_ref, ycat_ref, h2_ref = refs[10 + nl:16 + nl]
        cbuf = refs[16 + 2 * nl]
        if nl:
            gather = _Gather(refs[10:10 + nl], refs[16 + nl:16 + 2 * nl], *refs[17 + 2 * nl:20 + 2 * nl])
            pl.when(pl.program_id(0) == 0)(gather.start)
            pl.when(pl.program_id(0) == nt // 2)(gather.forward)
            pl.when(pl.program_id(0) == nt - 1)(gather.finish)

        @pl.when(pl.program_id(0) == 0)
        def _():
            cbuf[0:8, :] = jnp.zeros((8, D_FF), F32)

        yat = ya_ref[...]
        yan = ((yat * _rstd(yat)) * goa_ref[...]).astype(BF)
        ycn = ycn_ref[...]
        ycat_ref[:, 0:CONV_W] = ycn
        ycat_ref[:, CONV_W:D_MODEL] = yan
        x1 = x_ref[...] + _mm(ycn, wout_ref[0:CONV_W, :]) + _mm(yan, wout_ref[CONV_W:D_MODEL, :])
        x1_ref[...] = x1
        h2 = ((x1 * _rstd(x1)) * gffn_ref[...]).astype(BF)
        h2_ref[...] = h2
        for lo, hi in FF_SLABS:
            gps = _mm_nt(h2, wg_ref[lo:hi, :])
            ups = _mm_nt(h2, wu_ref[lo:hi, :])
            gp_ref[:, lo:hi] = gps
            up_ref[:, lo:hi] = ups
            cbuf[8:8 + tm, lo:hi] = gps
            cb = cbuf[:, lo:hi]
            gate = (fcw_ref[0:1, lo:hi] * _shift_down(cb, 2, tm) + fcw_ref[1:2, lo:hi] * _shift_down(cb, 1, tm)
                    + fcw_ref[2:3, lo:hi] * gps + fcb_ref[:, lo:hi])
            act_ref[:, lo:hi] = ((gate * jax.nn.sigmoid(gate)) * ups).astype(BF)
        cbuf[0:8, :] = cbuf[tm:tm + 8, :]

    any_spec = pl.BlockSpec(memory_space=pl.ANY)
    sems = [pltpu.SemaphoreType.DMA((6 * nl,)), pltpu.SemaphoreType.DMA((6 * nl,)), pltpu.SemaphoreType.DMA((nl,))]
    res = pl.pallas_call(
        body, name="fwd_ffn", grid=(nt,),
        in_specs=[_rows(tm, D_MODEL), _rows(tm, CONV_W), _rows(tm, ATTN_W), _const((D_MODEL, D_MODEL)),
                  _const((D_FF, D_MODEL)), _const((D_FF, D_MODEL)),
                  _const((1, ATTN_W)), _const((1, D_MODEL)), _const((3, D_FF)), _const((1, D_FF))]
        + [any_spec] * nl,
        out_specs=[_rows(tm, D_MODEL), _rows(tm, D_FF), _rows(tm, D_FF), _rows(tm, D_FF),
                   _rows(tm, D_MODEL), _rows(tm, D_MODEL)] + [any_spec] * nl,
        out_shape=[jax.ShapeDtypeStruct((t, D_MODEL), F32), jax.ShapeDtypeStruct((t, D_FF), F32),
                   jax.ShapeDtypeStruct((t, D_FF), F32), jax.ShapeDtypeStruct((t, D_FF), BF),
                   jax.ShapeDtypeStruct((t, D_MODEL), BF), jax.ShapeDtypeStruct((t, D_MODEL), BF)]
        + [jax.ShapeDtypeStruct((N_CHIPS,) + w.shape, w.dtype) for w in late],
        scratch_shapes=[pltpu.VMEM((tm + 8, D_FF), F32)] + (sems if nl else []),
        compiler_params=_params(56),
    )(x, ycn, ya, wout, wg4, wu4, g_oa, g_ffn, fcw, fcb, *late)
    return tuple(res[0:6]) + (list(res[6:]),)


def _fwd_tail(x1, act, p, target, wd4, wpg, wpp4, g_ple, tm):
    t = x1.shape[0]
    nt = t // tm

    def body(x1_ref, act_ref, p_ref, tgt_ref, wd_ref, wpg_ref, wpp_ref, g_ref,
             dx2_ref, h3_ref, ds_ref, dpp_ref, dg_ref, loss_ref, lacc):
        i = pl.program_id(0)

        @pl.when(i == 0)
        def _():
            dg_ref[...] = jnp.zeros_like(dg_ref)
            lacc[...] = jnp.zeros_like(lacc)

        x2 = x1_ref[...]
        for lo, hi in FF_SLABS:
            x2 = x2 + _mm(act_ref[:, lo:hi], wd_ref[lo:hi, :])
        r3 = _rstd(x2)
        xh = x2 * r3
        h3 = (xh * g_ref[...]).astype(BF)
        h3_ref[...] = h3
        sg = jax.nn.sigmoid(_mm(h3, wpg_ref[...]))
        pb = p_ref[...].astype(BF)
        pp = jnp.concatenate([_mm(pb, wpp_ref[s]) for s in range(N_CHIPS)], axis=1)
        err = (x2 + sg * pp) - tgt_ref[...]
        lacc[...] += _colsum(err * err)
        dx3 = err * (1.0 / D_MODEL)
        dpp_ref[...] = (dx3 * sg).astype(BF)
        dsb = ((dx3 * pp) * (sg * (1.0 - sg))).astype(BF)
        ds_ref[...] = dsb
        dh3 = _mm_nt(dsb, wpg_ref[...])
        dg_ref[...] += _colsum(dh3 * xh)
        dx2_ref[...] = dx3 + _norm_bwd(dh3, xh, r3, g_ref[...])

        @pl.when(i == nt - 1)
        def _():
            loss_ref[...] = jnp.full((1, 128), jnp.sum(lacc[...]) * (0.5 / D_MODEL), F32)

    return pl.pallas_call(
        body, name="fwd_tail", grid=(nt,),
        in_specs=[_rows(tm, D_MODEL), _rows(tm, D_FF), _rows(tm, PLE_DIM), _rows(tm, D_MODEL),
                  _const((D_FF, D_MODEL)), _const((D_MODEL, D_MODEL)),
                  _const((N_CHIPS, PLE_DIM, PLE_DIM)), _const((1, D_MODEL))],
        out_specs=[_rows(tm, D_MODEL), _rows(tm, D_MODEL), _rows(tm, D_MODEL), _rows(tm, D_MODEL),
                   pl.BlockSpec((1, D_MODEL), lambda i: (0, 0)), pl.BlockSpec((1, 128), lambda i: (0, 0))],
        out_shape=[jax.ShapeDtypeStruct((t, D_MODEL), F32), jax.ShapeDtypeStruct((t, D_MODEL), BF),
                   jax.ShapeDtypeStruct((t, D_MODEL), BF), jax.ShapeDtypeStruct((t, D_MODEL), BF),
                   jax.ShapeDtypeStruct((1, D_MODEL), F32), jax.ShapeDtypeStruct((1, 128), F32)],
        scratch_shapes=[pltpu.VMEM((1, D_MODEL), F32)],
        compiler_params=_params(48),
    )(x1, act, p, target, wd4, wpg, wpp4, g_ple)


def _bwd_ffn_a(dx2, gp, up, wd4, fcw, fcb, tm, riders=()):
    t = dx2.shape[0]
    nt = t // tm
    nr = len(riders)

    def body(*refs):
        dx2_ref, gp_ref, gph_ref, up_ref, wd_ref, fcw_ref, fcb_ref = refs[0:7]
        dgp_ref, dup_ref, dfcw_ref, dfcb_ref = refs[7 + nr:11 + nr]
        cbuf, dbuf = refs[11 + 2 * nr:13 + 2 * nr]
        i = pl.program_id(0)
        if nr:
            _ride_scatter(i == 0, i == nt - 1, riders, refs[7:7 + nr], refs[11 + nr:11 + 2 * nr],
                          *refs[13 + 2 * nr:15 + 2 * nr])

        @pl.when(i == 0)
        def _():
            dbuf[tm:tm + 8, :] = jnp.zeros((8, D_FF), F32)
            dfcw_ref[...] = jnp.zeros_like(dfcw_ref)
            dfcb_ref[...] = jnp.zeros_like(dfcb_ref)

        not_first_tile = i < nt - 1
        dx2b = dx2_ref[...].astype(BF)
        for lo, hi in FF_SLABS:
            gps = gp_ref[:, lo:hi]
            cbuf[0:8, lo:hi] = jnp.where(not_first_tile, gph_ref[:, lo:hi], 0.0)
            cbuf[8:8 + tm, lo:hi] = gps
            cb = cbuf[:, lo:hi]
            g1 = _shift_down(cb, 1, tm)
            g2 = _shift_down(cb, 2, tm)
            w0, w1, w2 = fcw_ref[0:1, lo:hi], fcw_ref[1:2, lo:hi], fcw_ref[2:3, lo:hi]
            gate = w0 * g2 + w1 * g1 + w2 * gps + fcb_ref[:, lo:hi]
            sg = jax.nn.sigmoid(gate)
            dact = _mm_nt(dx2b, wd_ref[lo:hi, :])
            dup_ref[:, lo:hi] = (dact * (gate * sg)).astype(BF)
            dgate = (dact * up_ref[:, lo:hi]) * (sg * (1.0 + gate * (1.0 - sg)))
            dfcb_ref[:, lo:hi] += _colsum(dgate)
            dfcw_ref[0:1, lo:hi] += _colsum(dgate * g2)
            dfcw_ref[1:2, lo:hi] += _colsum(dgate * g1)
            dfcw_ref[2:3, lo:hi] += _colsum(dgate * gps)
            dbuf[0:tm, lo:hi] = dgate
            db = dbuf[:, lo:hi]
            dgp = w2 * dgate + w1 * _shift_up(db, 1, tm) + w0 * _shift_up(db, 2, tm)
            dgp_ref[:, lo:hi] = dgp.astype(BF)
        dbuf[tm:tm + 8, :] = dbuf[0:8, :]

    r_in, r_out, r_sems = _rider_specs(riders)
    res = pl.pallas_call(
        body, name="bwd_ffn_a", grid=(nt,),
        in_specs=[_rows(tm, D_MODEL, nt), _rows(tm, D_FF, nt), _halo(tm, D_FF, nt), _rows(tm, D_FF, nt),
                  _const((D_FF, D_MODEL)), _const((3, D_FF)), _const((1, D_FF))] + r_in,
        out_specs=[_rows(tm, D_FF, nt), _rows(tm, D_FF, nt),
                   pl.BlockSpec((3, D_FF), lambda i: (0, 0)), pl.BlockSpec((1, D_FF), lambda i: (0, 0))] + r_in,
        out_shape=[jax.ShapeDtypeStruct((t, D_FF), BF), jax.ShapeDtypeStruct((t, D_FF), BF),
                   jax.ShapeDtypeStruct((3, D_FF), F32), jax.ShapeDtypeStruct((1, D_FF), F32)] + r_out,
        scratch_shapes=[pltpu.VMEM((tm + 8, D_FF), F32), pltpu.VMEM((tm + 8, D_FF), F32)] + r_sems,
        compiler_params=_params(56),
    )(dx2, gp, gp, up, wd4, fcw, fcb, *[g for g, _ in riders])
    return res[0], res[1], res[2], res[3], list(res[4:])


def _bwd_ffn_b(dgp, dup, dx2, x1, ya, wg4, wu4, wout, g_ffn, g_oa, tm):
    t = dx2.shape[0]
    nt = t // tm

    def body(dgp_ref, dup_ref, dx2_ref, x1_ref, ya_ref, wg_ref, wu_ref, wout_ref, gffn_ref, goa_ref,
             dx1_ref, dycn_ref, dya_ref, dgffn_ref, dgoa_ref):
        @pl.when(pl.program_id(0) == 0)
        def _():
            dgffn_ref[...] = jnp.zeros_like(dgffn_ref)
            dgoa_ref[...] = jnp.zeros_like(dgoa_ref)

        dh2 = jnp.zeros((tm, D_MODEL), F32)
        for lo, hi in FF_SLABS:
            dh2 = dh2 + _mm(dgp_ref[:, lo:hi], wg_ref[lo:hi, :]) + _mm(dup_ref[:, lo:hi], wu_ref[lo:hi, :])
        x1 = x1_ref[...]
        r2 = _rstd(x1)
        xh = x1 * r2
        dgffn_ref[...] += _colsum(dh2 * xh)
        dx1 = dx2_ref[...] + _norm_bwd(dh2, xh, r2, gffn_ref[...])
        dx1_ref[...] = dx1
        dy = _mm_nt(dx1.astype(BF), wout_ref[...])
        dycn_ref[...] = dy[:, 0:CONV_W]
        dyan = dy[:, CONV_W:D_MODEL]
        yat = ya_ref[...]
        ra = _rstd(yat)
        yah = yat * ra
        dgoa_ref[...] += _colsum(dyan * yah)
        dya_ref[...] = _norm_bwd(dyan, yah, ra, goa_ref[...])

    return pl.pallas_call(
        body, name="bwd_ffn_b", grid=(nt,),
        in_specs=[_rows(tm, D_FF), _rows(tm, D_FF), _rows(tm, D_MODEL), _rows(tm, D_MODEL),
                  _rows(tm, ATTN_W), _const((D_FF, D_MODEL)), _const((D_FF, D_MODEL)),
                  _const((D_MODEL, D_MODEL)), _const((1, D_MODEL)), _const((1, ATTN_W))],
        out_specs=[_rows(tm, D_MODEL), _rows(tm, CONV_W), _rows(tm, ATTN_W),
                   pl.BlockSpec((1, D_MODEL), lambda i: (0, 0)), pl.BlockSpec((1, ATTN_W), lambda i: (0, 0))],
        out_shape=[jax.ShapeDtypeStruct((t, D_MODEL), F32), jax.ShapeDtypeStruct((t, CONV_W), F32),
                   jax.ShapeDtypeStruct((t, ATTN_W), F32),
                   jax.ShapeDtypeStruct((1, D_MODEL), F32), jax.ShapeDtypeStruct((1, ATTN_W), F32)],
        compiler_params=_params(48),
    )(dgp, dup, dx2, x1, ya, wg4, wu4, wout, g_ffn, g_oa)


def _bwd_mix(x, dx1, zbcx, qkv, dycn, dq, dk, dv, win4, conv_w, conv_b, g_oc, g_mix, gm, gq8, gk8, tm):
    t = x.shape[0]
    nt = t // tm

    def body(x_ref, dx1_ref, z_ref, zh_ref, qkv_ref, dycn_ref, dq_ref, dk_ref, dv_ref, w_ref, cw_ref, cb_ref,
             goc_ref, g_ref, gm_ref, gq_ref, gk_ref,
             gx_ref, h1_ref, dz_ref, dcw_ref, dcb_ref, dgoc_ref, dg_ref, dgq_ref, dgk_ref, ubuf, dbuf):
        i = pl.program_id(0)

        @pl.when(i == 0)
        def _():
            dbuf[tm:tm + 8, :] = jnp.zeros((8, CONV_W), F32)
            dcw_ref[...] = jnp.zeros_like(dcw_ref)
            dcb_ref[...] = jnp.zeros_like(dcb_ref)
            dgoc_ref[...] = jnp.zeros_like(dgoc_ref)
            dg_ref[...] = jnp.zeros_like(dg_ref)
            dgq_ref[...] = jnp.zeros_like(dgq_ref)
            dgk_ref[...] = jnp.zeros_like(dgk_ref)

        not_first_tile = i < nt - 1
        zb = z_ref[:, 0:512]
        zc = z_ref[:, 512:1024]
        zx = z_ref[:, 1024:1536]
        u = zc * zx
        ubuf[0:8, :] = jnp.where(not_first_tile, zh_ref[:, 512:1024] * zh_ref[:, 1024:1536], 0.0)
        ubuf[8:8 + tm, :] = u
        ub = ubuf[...]
        u1 = _shift_down(ub, 1, tm)
        u2 = _shift_down(ub, 2, tm)
        w0, w1, w2 = cw_ref[0:1, :], cw_ref[1:2, :], cw_ref[2:3, :]
        cv = w0 * u2 + w1 * u1 + w2 * u + cb_ref[...]
        yc = zb * cv
        rc = _rstd(yc)
        ych = yc * rc
        dycn = dycn_ref[...]
        dgoc_ref[...] += _colsum(dycn * ych)
        dyc = _norm_bwd(dycn, ych, rc, goc_ref[...])
        dcv = dyc * zb
        dcb_ref[...] += _colsum(dcv)
        dcw_ref[0:1, :] += _colsum(dcv * u2)
        dcw_ref[1:2, :] += _colsum(dcv * u1)
        dcw_ref[2:3, :] += _colsum(dcv * u)
        dbuf[0:tm, :] = dcv
        db = dbuf[...]
        du = w2 * dcv + w1 * _shift_up(db, 1, tm) + w0 * _shift_up(db, 2, tm)
        dbuf[tm:tm + 8, :] = dbuf[0:8, :]
        dz_ref[:, 0:512] = (dyc * cv).astype(BF)
        dz_ref[:, 512:1024] = (du * zx).astype(BF)
        dz_ref[:, 1024:1536] = (du * zc).astype(BF)
        for z0, d_ref, gg_ref, acc_ref, sc in ((0, dq_ref, gq_ref, dgq_ref, HEAD_DIM ** -0.5),
                                               (512, dk_ref, gk_ref, dgk_ref, 1.0)):
            z = qkv_ref[:, z0:z0 + 512]
            rr = lax.rsqrt(_head_mean(z * z, gm_ref) + EPS)
            zh = z * rr
            dn = d_ref[...] * sc
            acc_ref[...] += _colsum(dn * zh)
            dzh = dn * gg_ref[...]
            dz_ref[:, 1536 + z0:1536 + z0 + 512] = (rr * (dzh - zh * _head_mean(dzh * zh, gm_ref))).astype(BF)
        dz_ref[:, 2560:3072] = dv_ref[...].astype(BF)
        dh1 = jnp.zeros((tm, D_MODEL), F32)
        for s in range(N_CHIPS):
            dh1 = dh1 + _mm_nt(dz_ref[:, s * IN_SLAB:(s + 1) * IN_SLAB], w_ref[s])
        xt = x_ref[...]
        r1 = _rstd(xt)
        xh = xt * r1
        h1_ref[...] = (xh * g_ref[...]).astype(BF)
        dg_ref[...] += _colsum(dh1 * xh)
        gx_ref[...] = dx1_ref[...] + _norm_bwd(dh1, xh, r1, g_ref[...])

    def acc(width, rows=1):
        return pl.BlockSpec((rows, width), lambda i: (0, 0))

    return pl.pallas_call(
        body, name="bwd_mix", grid=(nt,),
        in_specs=[_rows(tm, D_MODEL, nt), _rows(tm, D_MODEL, nt), _rows(tm, 1536, nt), _halo(tm, 1536, nt),
                  _rows(tm, 1536, nt), _rows(tm, CONV_W, nt), _rows(tm, ATTN_W, nt), _rows(tm, ATTN_W, nt),
                  _rows(tm, ATTN_W, nt), _const((N_CHIPS, D_MODEL, IN_SLAB)),
                  _const((3, CONV_W)), _const((1, CONV_W)), _const((1, CONV_W)), _const((1, D_MODEL)),
                  _const((ATTN_W, ATTN_W)), _const((1, ATTN_W)), _const((1, ATTN_W))],
        out_specs=[_rows(tm, D_MODEL, nt), _rows(tm, D_MODEL, nt), _rows(tm, 3072, nt),
                   acc(CONV_W, 3), acc(CONV_W), acc(CONV_W), acc(D_MODEL), acc(ATTN_W), acc(ATTN_W)],
        out_shape=[jax.ShapeDtypeStruct((t, D_MODEL), F32), jax.ShapeDtypeStruct((t, D_MODEL), BF),
                   jax.ShapeDtypeStruct((t, 3072), BF), jax.ShapeDtypeStruct((3, CONV_W), F32),
                   jax.ShapeDtypeStruct((1, CONV_W), F32), jax.ShapeDtypeStruct((1, CONV_W), F32),
                   jax.ShapeDtypeStruct((1, D_MODEL), F32), jax.ShapeDtypeStruct((1, ATTN_W), F32),
                   jax.ShapeDtypeStruct((1, ATTN_W), F32)],
        scratch_shapes=[pltpu.VMEM((tm + 8, CONV_W), F32), pltpu.VMEM((tm + 8, CONV_W), F32)],
        compiler_params=_params(56),
    )(x, dx1, zbcx, zbcx, qkv, dycn, dq, dk, dv, win4, conv_w, conv_b, g_oc, g_mix, gm, gq8, gk8)


def _wgrad(a, b, tn, tt, name):
    t, k = a.shape
    n = b.shape[1]
    nt = t // tt

    def body(a_ref, b_ref, o_ref, ob_ref):
        @pl.when(pl.program_id(1) == 0)
        def _():
            o_ref[...] = jnp.zeros_like(o_ref)

        o_ref[...] += _mm_tn(a_ref[...].astype(BF), b_ref[...].astype(BF))

        @pl.when(pl.program_id(1) == nt - 1)
        def _():
            ob_ref[...] = o_ref[...].astype(BF)

    spec = pl.BlockSpec((k, tn), lambda j, i: (0, j))
    return pl.pallas_call(
        body, name=name, grid=(n // tn, nt),
        in_specs=[pl.BlockSpec((tt, k), lambda j, i: (i, 0)), pl.BlockSpec((tt, tn), lambda j, i: (i, j))],
        out_specs=[spec, spec],
        out_shape=[jax.ShapeDtypeStruct((k, n), F32), jax.ShapeDtypeStruct((k, n), BF)],
        compiler_params=_params(48, 2),
    )(a, b)


def _gather_weights(shards, pack):
    nw = len(shards)

    def body(*refs):
        ins = refs[:nw]
        pack_ref = refs[nw]
        outs = refs[nw + 1:2 * nw + 1]
        pack_out = refs[2 * nw + 1]
        send_sems, recv_sems, local_sems = refs[2 * nw + 2:]
        x, y, c = _place()
        me = 2 * x + y
        local, remote = [], []

        def sem(w, j):
            return w * 6 + j

        def push(src, dst, w, j, to):
            return pltpu.make_async_remote_copy(src_ref=src, dst_ref=dst, send_sem=send_sems.at[sem(w, j)],
                                                recv_sem=recv_sems.at[sem(w, j)], device_id=to, device_id_type=MESH)

        def half_rows(w, h):
            half = ins[w].shape[0] // 2
            return pl.ds(pl.multiple_of(h * half, 16), half)

        for w in range(nw):
            local.append(pltpu.make_async_copy(ins[w], outs[w].at[me], local_sems.at[w]))
            for k in (1, 2, 3):
                px, py = _chip_peer(x, y, k)
                mine = half_rows(w, c)
                remote.append(push(ins[w].at[mine], outs[w].at[me, mine], w, k - 1, (px, py, c)))
        local.append(pltpu.make_async_copy(pack_ref, pack_out.at[me], local_sems.at[nw]))
        for k in (1, 2, 3):
            px, py = _chip_peer(x, y, k)
            remote.append(push(pack_ref, pack_out.at[me], nw, k - 1, (px, py, c)))
        for cp in local + remote:
            cp.start()
        for w in range(nw):
            for k in (1, 2, 3):
                landed = outs[w].at[me ^ k, half_rows(w, c)]
                push(landed, landed, w, k - 1, (x, y, c)).wait_recv()
                fw = push(landed, landed, w, 2 + k, (x, y, 1 - c))
                fw.start()
                remote.append(fw)
        for k in (1, 2, 3):
            landed = pack_out.at[me ^ k]
            push(landed, landed, nw, k - 1, (x, y, c)).wait_recv()
        for w in range(nw):
            for k in (1, 2, 3):
                landed = outs[w].at[me ^ k, half_rows(w, 1 - c)]
                push(landed, landed, w, 2 + k, (x, y, c)).wait_recv()
        for cp in remote:
            cp.wait_send()
        for cp in local:
            cp.wait()

    any_spec = pl.BlockSpec(memory_space=pl.ANY)
    out_shape = [jax.ShapeDtypeStruct((N_CHIPS,) + s.shape, s.dtype) for s in shards]
    out_shape.append(jax.ShapeDtypeStruct((N_CHIPS,) + pack.shape, pack.dtype))
    return pl.pallas_call(
        body, name="gather_weights",
        in_specs=[any_spec] * (nw + 1), out_specs=[any_spec] * (nw + 1), out_shape=out_shape,
        scratch_shapes=[pltpu.SemaphoreType.DMA(((nw + 1) * 6,)), pltpu.SemaphoreType.DMA(((nw + 1) * 6,)),
                        pltpu.SemaphoreType.DMA((nw + 1,))],
    )(*shards, pack)


def _adamw(w, g, m, v):
    m = ADAM_B1 * m + (1.0 - ADAM_B1) * g
    v = ADAM_B2 * v + (1.0 - ADAM_B2) * (g * g)
    m_hat = m / (1.0 - ADAM_B1 ** ADAM_STEP)
    v_hat = v / (1.0 - ADAM_B2 ** ADAM_STEP)
    delta = -ADAM_LR * (m_hat / (jnp.sqrt(v_hat) + ADAM_EPS) + ADAM_WD * w)
    return delta, m, v


def _scatter_alone(g16, col_sharded, name):
    def body(g_ref, slots_ref, send_sems, recv_sems):
        copies = _scatter_copies(g_ref, slots_ref, send_sems, recv_sems, 0, col_sharded)
        for cp in copies:
            cp.start()
        for cp in copies:
            cp.wait()

    any_spec = pl.BlockSpec(memory_space=pl.ANY)
    return pl.pallas_call(
        body, name=name, in_specs=[any_spec], out_specs=any_spec,
        out_shape=jax.ShapeDtypeStruct((7,) + _piece_shape(g16.shape, col_sharded), BF),
        scratch_shapes=[pltpu.SemaphoreType.DMA((7,)), pltpu.SemaphoreType.DMA((7,))],
    )(g16)


def _finish_reduce(grad, slots, col_sharded, name):
    r, cw = _piece_shape(grad.shape, col_sharded)
    chunk = 32
    assert r % chunk == 0

    def body(g_hbm, slots_ref, full, own, lsem, c_send, c_recv):
        x, y, c = _place()
        cp = pltpu.make_async_copy(g_hbm.at[_piece_window(col_sharded, r, cw, 2 * x + y, c)], own, lsem)
        cp.start()
        cp.wait()
        mine = pl.multiple_of(c * r, 8)

        def add(j, carry):
            rows = pl.ds(pl.multiple_of(j * chunk, 8), chunk)
            tot = own[rows, :]
            for k in range(7):
                tot = tot + slots_ref[k, rows, :].astype(F32)
            full[pl.ds(mine + pl.multiple_of(j * chunk, 8), chunk), :] = tot
            return carry

        lax.fori_loop(0, r // chunk, add, 0)
        half = full.at[pl.ds(mine, r), :]
        swap = pltpu.make_async_remote_copy(src_ref=half, dst_ref=half, send_sem=c_send, recv_sem=c_recv,
                                            device_id=(x, y, 1 - c), device_id_type=MESH)
        swap.start()
        swap.wait()

    vmem = pl.BlockSpec(memory_space=pltpu.VMEM)
    return pl.pallas_call(
        body, name=name, in_specs=[pl.BlockSpec(memory_space=pl.ANY), vmem], out_specs=vmem,
        out_shape=jax.ShapeDtypeStruct((2 * r, cw), F32),
        scratch_shapes=[pltpu.VMEM((r, cw), F32), pltpu.SemaphoreType.DMA, pltpu.SemaphoreType.DMA,
                        pltpu.SemaphoreType.DMA],
        compiler_params=pltpu.CompilerParams(vmem_limit_bytes=32 * MIB),
    )(grad, slots)


def _adamw_big(g, w, m, v, name):
    vr, vc = w.shape
    assert g.shape == w.shape
    rows = 64

    def body(g_ref, w_ref, m_ref, v_ref, go_ref, do_ref, mo_ref, vo_ref):
        gg = g_ref[...]
        delta, mn, vn = _adamw(w_ref[...], gg, m_ref[...], v_ref[...])
        go_ref[...] = gg
        do_ref[...] = delta
        mo_ref[...] = mn
        vo_ref[...] = vn

    blk = pl.BlockSpec((rows, vc), lambda i: (i, 0))
    shard = jax.ShapeDtypeStruct((vr, vc), F32)
    return pl.pallas_call(
        body, name=name, grid=(vr // rows,),
        in_specs=[blk, blk, blk, blk], out_specs=[blk] * 4,
        out_shape=[shard] * 4, compiler_params=_params(32),
    )(g, w, m, v)


def _allreduce_small(pack):
    rows = pack.shape[0]

    def body(p_ref, o_ref, slots, send_sems, recv_sems):
        x, y, c = _place()
        me = 4 * x + 2 * y + c
        slots[me] = p_ref[...]
        sends = []
        for k in range(1, 8):
            cp = pltpu.make_async_remote_copy(
                src_ref=p_ref, dst_ref=slots.at[me], send_sem=send_sems.at[k - 1], recv_sem=recv_sems.at[k - 1],
                device_id=(x ^ (k >> 2), y ^ ((k >> 1) & 1), c ^ (k & 1)), device_id_type=MESH)
            cp.start()
            sends.append(cp)
        for cp in sends:
            cp.wait()
        tot = slots[0]
        for j in range(1, 8):
            tot = tot + slots[j]
        o_ref[...] = tot

    vmem = pl.BlockSpec(memory_space=pltpu.VMEM)
    return pl.pallas_call(
        body, name="allreduce_small", in_specs=[vmem], out_specs=vmem,
        out_shape=jax.ShapeDtypeStruct(pack.shape, F32),
        scratch_shapes=[pltpu.VMEM((8, rows, D_MODEL), F32), pltpu.SemaphoreType.DMA((7,)),
                        pltpu.SemaphoreType.DMA((7,))],
    )(pack)


def _adamw_small(ws, gs, ms, vs):
    n = len(ws)

    def body(*refs):
        w_refs, g_refs, m_refs, v_refs = refs[0:n], refs[n:2 * n], refs[2 * n:3 * n], refs[3 * n:4 * n]
        d_refs, mo_refs, vo_refs = refs[4 * n:5 * n], refs[5 * n:6 * n], refs[6 * n:7 * n]
        for j in range(n):
            delta, mn, vn = _adamw(w_refs[j][...], g_refs[j][...], m_refs[j][...], v_refs[j][...])
            d_refs[j][...] = delta
            mo_refs[j][...] = mn
            vo_refs[j][...] = vn

    vmem = pl.BlockSpec(memory_space=pltpu.VMEM)
    shapes = [jax.ShapeDtypeStruct(w.shape, F32) for w in ws]
    outs = pl.pallas_call(
        body, name="adamw_small", in_specs=[vmem] * (4 * n), out_specs=[vmem] * (3 * n), out_shape=shapes * 3,
    )(*ws, *gs, *ms, *vs)
    return outs[0:n], outs[n:2 * n], outs[2 * n:3 * n]


def _local_step(x, p, target, wts, late=None):
    (win4, wout, wg4, wu4, wd4, wpg, wpp4, conv_w, fcw, g_mix, conv_b, gq, gk, g_oc, g_oa, g_ffn, fcb, g_ple) = wts
    comm = late is not None
    gm = jnp.kron(jnp.eye(N_HEADS, dtype=F32), jnp.full((HEAD_DIM, HEAD_DIM), 1.0 / HEAD_DIM, F32)).astype(BF)
    gq8, gk8 = jnp.tile(gq, (1, N_HEADS)), jnp.tile(gk, (1, N_HEADS))
    mb, mbk = _mask_tables()
    zbcx, qkv, ycn, qkn = _fwd_mix(x, g_mix, win4, conv_w, conv_b, g_oc, gm, gq8, gk8, 512)
    ya, lse, gathered = _attn_fwd(qkn, qkv, mb, late[0:3] if comm else ())
    if comm:
        wout, wg4, wu4 = (g.reshape(-1, D_MODEL) for g in gathered)
    x1, gp, up, act, ycat, h2, gathered = _fwd_ffn(x, ycn, ya, wout, wg4, wu4, g_oa, g_ffn, fcw, fcb, 256,
                                                    late[3:6] if comm else ())
    if comm:
        wd4, wpg, wpp4 = gathered
        wd4, wpg = wd4.reshape(D_FF, D_MODEL), wpg.reshape(D_MODEL, D_MODEL)
    dx2, h3, ds, dpp, dg_ple, loss = _fwd_tail(x1, act, p, target, wd4, wpg, wpp4, g_ple, 256)
    big, big16, slots = {}, {}, {}

    def wgrad(name, a, b, tn):
        big[name], big16[name] = _wgrad(a, b, tn, 512, "wgrad_" + name)
        return (big16[name], _COL_SHARDED[name])

    riders = [wgrad("w_down", act, dx2, 512), wgrad("w_ple_gate", h3, ds, 1024), wgrad("w_ple_proj", p, dpp, 1024)]
    dgp, dup, dfcw, dfcb, got = _bwd_ffn_a(dx2, gp, up, wd4, fcw, fcb, 256, riders if comm else ())
    slots.update(zip(("w_down", "w_ple_gate", "w_ple_proj"), got))
    riders = [wgrad("w_gate", dgp, h2, 512), wgrad("w_up", dup, h2, 512)]
    dx1, dycn, dya, dg_ffn, dg_oa = _bwd_ffn_b(dgp, dup, dx2, x1, ya, wg4, wu4, wout, g_ffn, g_oa, 256)
    riders.append(wgrad("w_out", ycat, dx1, 1024))
    dq, dk, dv, got = _attn_bwd(qkn, qkv, ya, lse, dya, mb, mbk, riders if comm else ())
    slots.update(zip(("w_gate", "w_up", "w_out"), got))
    grad_x, h1, dz, dcw, dcb, dg_oc, dg_mix, dgq8, dgk8 = _bwd_mix(
        x, dx1, zbcx, qkv, dycn, dq, dk, dv, win4, conv_w, conv_b, g_oc, g_mix, gm, gq8, gk8, 256)
    g16, cs = wgrad("w_in", h1, dz, 1536)
    if comm:
        slots["w_in"] = _scatter_alone(g16, cs, "scatter_w_in")
    dgq = dgq8.reshape(N_HEADS, HEAD_DIM).sum(axis=0, keepdims=True)
    dgk = dgk8.reshape(N_HEADS, HEAD_DIM).sum(axis=0, keepdims=True)
    small = dict(g_mix=dg_mix, conv_w=dcw, conv_b=dcb, q_norm_g=dgq, k_norm_g=dgk, g_out_conv=dg_oc,
                 g_out_attn=dg_oa, g_ffn=dg_ffn, ffn_conv_w=dfcw, ffn_conv_b=dfcb, g_ple=dg_ple)
    return loss[0, 0], grad_x, big, slots, small


_SMALL_ROWS = 24


def _pack_small(s):
    z64 = jnp.zeros((1, 1024 - 512 - 128), F32)
    rows = [s["g_mix"], s["g_ffn"], s["g_ple"],
            jnp.concatenate([s["conv_b"], s["g_out_conv"]], axis=1),
            jnp.concatenate([s["g_out_attn"], s["q_norm_g"], s["k_norm_g"], z64], axis=1),
            jnp.pad(s["conv_w"], ((0, 0), (0, 512))),
            jnp.pad(s["ffn_conv_b"], ((0, 0), (0, 3072 - D_FF))).reshape(3, 1024),
            jnp.pad(s["ffn_conv_w"], ((0, 0), (0, 3072 - D_FF))).reshape(9, 1024),
            jnp.zeros((_SMALL_ROWS - 20, 1024), F32)]
    return jnp.concatenate(rows, axis=0)


def _unpack_small(t):
    return dict(g_mix=t[0:1], g_ffn=t[1:2], g_ple=t[2:3], conv_b=t[3:4, 0:512], g_out_conv=t[3:4, 512:1024],
                g_out_attn=t[4:5, 0:512], q_norm_g=t[4:5, 512:576], k_norm_g=t[4:5, 576:640],
                conv_w=t[5:8, 0:512], ffn_conv_b=t[8:11].reshape(1, 3072)[:, :D_FF],
                ffn_conv_w=t[11:20].reshape(3, 3072)[:, :D_FF])


_BIG = ("w_in", "w_out", "w_gate", "w_up", "w_down", "w_ple_gate", "w_ple_proj")
_COL_SHARDED = dict(w_in=True, w_out=False, w_gate=False, w_up=False, w_down=False, w_ple_gate=False, w_ple_proj=True)
_TRANSPOSED = ("w_gate", "w_up")
_WEIGHTS = ("g_mix", "w_in", "conv_w", "conv_b", "q_norm_g", "k_norm_g", "g_out_conv", "g_out_attn", "w_out",
            "g_ffn", "w_gate", "w_up", "ffn_conv_w", "ffn_conv_b", "w_down", "g_ple", "w_ple_gate", "w_ple_proj")


def kernel(x, p, g_mix, w_in, conv_w, conv_b, q_norm_g, k_norm_g, g_out_conv, g_out_attn, w_out, g_ffn, w_gate, w_up, ffn_conv_w, ffn_conv_b, w_down, g_ple, w_ple_gate, w_ple_proj, loss_target, m_g_mix, m_w_in, m_conv_w, m_conv_b, m_q_norm_g, m_k_norm_g, m_g_out_conv, m_g_out_attn, m_w_out, m_g_ffn, m_w_gate, m_w_up, m_ffn_conv_w, m_ffn_conv_b, m_w_down, m_g_ple, m_w_ple_gate, m_w_ple_proj, v_g_mix, v_w_in, v_conv_w, v_conv_b, v_q_norm_g, v_k_norm_g, v_g_out_conv, v_g_out_attn, v_w_out, v_g_ffn, v_w_gate, v_w_up, v_ffn_conv_w, v_ffn_conv_b, v_w_down, v_g_ple, v_w_ple_gate, v_w_ple_proj):
    w = dict(g_mix=g_mix, w_in=w_in, conv_w=conv_w, conv_b=conv_b, q_norm_g=q_norm_g, k_norm_g=k_norm_g,
             g_out_conv=g_out_conv, g_out_attn=g_out_attn, w_out=w_out, g_ffn=g_ffn, w_gate=w_gate, w_up=w_up,
             ffn_conv_w=ffn_conv_w, ffn_conv_b=ffn_conv_b, w_down=w_down, g_ple=g_ple, w_ple_gate=w_ple_gate,
             w_ple_proj=w_ple_proj)
    m = dict(g_mix=m_g_mix, w_in=m_w_in, conv_w=m_conv_w, conv_b=m_conv_b, q_norm_g=m_q_norm_g, k_norm_g=m_k_norm_g,
             g_out_conv=m_g_out_conv, g_out_attn=m_g_out_attn, w_out=m_w_out, g_ffn=m_g_ffn, w_gate=m_w_gate,
             w_up=m_w_up, ffn_conv_w=m_ffn_conv_w, ffn_conv_b=m_ffn_conv_b, w_down=m_w_down, g_ple=m_g_ple,
             w_ple_gate=m_w_ple_gate, w_ple_proj=m_w_ple_proj)
    v = dict(g_mix=v_g_mix, w_in=v_w_in, conv_w=v_conv_w, conv_b=v_conv_b, q_norm_g=v_q_norm_g, k_norm_g=v_k_norm_g,
             g_out_conv=v_g_out_conv, g_out_attn=v_g_out_attn, w_out=v_w_out, g_ffn=v_g_ffn, w_gate=v_w_gate,
             w_up=v_w_up, ffn_conv_w=v_ffn_conv_w, ffn_conv_b=v_ffn_conv_b, w_down=v_w_down, g_ple=v_g_ple,
             w_ple_gate=v_w_ple_gate, w_ple_proj=v_w_ple_proj)
    mats = [k for k, a in w.items() if a.ndim == 3]
    w = {k: (a[0] if k in mats else a) for k, a in w.items()}
    m = {k: (a[0] if k in mats else a) for k, a in m.items()}
    v = {k: (a[0] if k in mats else a) for k, a in v.items()}
    for n in _TRANSPOSED:
        w[n], m[n], v[n] = w[n].T, m[n].T, v[n].T
    chip = 2 * lax.axis_index("x") + lax.axis_index("y")

    late = [w[n].astype(BF) for n in ("w_out", "w_gate", "w_up", "w_down", "w_ple_gate", "w_ple_proj")]
    pack = jnp.pad(jnp.concatenate([w["conv_w"], w["ffn_conv_w"]], axis=1), ((0, 5), (0, 1024 - 128 - D_FF_SHARD)))
    win4, pack4 = _gather_weights([w["w_in"].astype(BF)], pack)
    conv_w_full = pack4[:, 0:3, 0:128].transpose(1, 0, 2).reshape(3, CONV_W)
    fcw_full = pack4[:, 0:3, 128:128 + D_FF_SHARD].transpose(1, 0, 2).reshape(3, D_FF)
    wts = (win4, None, None, None, None, None, None, conv_w_full, fcw_full, w["g_mix"], w["conv_b"], w["q_norm_g"],
           w["k_norm_g"], w["g_out_conv"], w["g_out_attn"], w["g_ffn"], w["ffn_conv_b"], w["g_ple"])

    loss, grad_x, big, slots, small = _local_step(x[0], p[0, 0], loss_target[0], wts, late)
    loss = lax.psum(loss, ("x", "y", "c"))

    grads, deltas, new_m, new_v = {}, {}, {}, {}
    for name in _BIG:
        total = _finish_reduce(big[name], slots[name], _COL_SHARDED[name], "finish_" + name)
        grads[name], deltas[name], new_m[name], new_v[name] = _adamw_big(total, w[name], m[name], v[name],
                                                                         "adamw_" + name)
    tot = _unpack_small(_allreduce_small(_pack_small(small)))
    tot["conv_w"] = lax.dynamic_slice_in_dim(tot["conv_w"], chip * 128, 128, axis=1)
    tot["ffn_conv_w"] = lax.dynamic_slice_in_dim(tot["ffn_conv_w"], chip * D_FF_SHARD, D_FF_SHARD, axis=1)
    names = [n for n in _WEIGHTS if n not in _BIG]
    d_s, m_s, v_s = _adamw_small([w[n] for n in names], [tot[n] for n in names], [m[n] for n in names],
                                 [v[n] for n in names])
    for j, n in enumerate(names):
        grads[n], deltas[n], new_m[n], new_v[n] = tot[n], d_s[j], m_s[j], v_s[j]

    out = [loss, grad_x[None]]
    for group in (grads, deltas, new_m, new_v):
        for n in _TRANSPOSED:
            group[n] = group[n].T
        out += [group[n][None] if n in mats else group[n] for n in _WEIGHTS]
    return tuple(out)
```

```python
import jax
import jax.numpy as jnp
from jax import lax
from jax.experimental import pallas as pl
from jax.experimental.pallas import tpu as pltpu

D_MODEL = 1024
CONV_W = 512
N_HEADS = 8
HEAD_DIM = 64
ATTN_W = 512
D_FF = 2816
D_FF_SHARD = 704
FF_SLABS = ((0, 1408), (1408, 2816))
IN_SLAB = 768
PLE_DIM = 256
N_CHIPS = 4
QBLK = 128
DILATIONS = (1, 4, 16)
EPS = 1e-6
NEG = -1e30
MESH = pl.DeviceIdType.MESH

ADAM_LR = 0.001
ADAM_B1 = 0.9
ADAM_B2 = 0.999
ADAM_EPS = 1e-08
ADAM_WD = 0.01
ADAM_STEP = 10

BF = jnp.bfloat16
F32 = jnp.float32
MIB = 1024 * 1024


def _mm(a, b):
    return jnp.dot(a, b, preferred_element_type=F32)


def _mm_nt(a, b):
    return lax.dot_general(a, b, (((1,), (1,)), ((), ())), preferred_element_type=F32)


def _mm_tn(a, b):
    return lax.dot_general(a, b, (((0,), (0,)), ((), ())), preferred_element_type=F32)


def _rstd(a):
    return lax.rsqrt(jnp.mean(a * a, axis=-1, keepdims=True) + EPS)


def _norm_bwd(dy, xh, r, g):
    dxh = dy * g
    return r * (dxh - xh * jnp.mean(dxh * xh, axis=-1, keepdims=True))


def _colsum(a):
    return jnp.sum(a, axis=0, keepdims=True)


def _head_mean(a, gm_ref):
    return _mm(a.astype(BF), gm_ref[...])


def _shift_down(buf, k, tm):
    return pltpu.roll(buf, k, axis=0)[8:8 + tm]


def _shift_up(buf, k, tm):
    return pltpu.roll(buf, tm + 8 - k, axis=0)[0:tm]


def _params(vmem_mib, n_grid=1):
    return pltpu.CompilerParams(dimension_semantics=("arbitrary",) * n_grid, vmem_limit_bytes=vmem_mib * MIB)


def _const(shape):
    n = len(shape)
    return pl.BlockSpec(shape, lambda *_: (0,) * n, pipeline_mode=pl.Buffered(1))


def _rows(tm, width, rev_of=None):
    if rev_of is None:
        return pl.BlockSpec((tm, width), lambda i: (i, 0))
    return pl.BlockSpec((tm, width), lambda i: (rev_of - 1 - i, 0))


def _halo(tm, width, nt):
    return pl.BlockSpec((8, width), lambda i: (jnp.maximum((nt - 1 - i) * (tm // 8) - 1, 0), 0))


def _fwd_mix(x, g_mix, win4, conv_w, conv_b, g_oc, gm, gq8, gk8, tm):
    t = x.shape[0]
    nt = t // tm

    def body(x_ref, g_ref, w_ref, cw_ref, cb_ref, goc_ref, gm_ref, gq_ref, gk_ref,
             zbcx_ref, qkv_ref, ycn_ref, qkn_ref, ubuf):
        @pl.when(pl.program_id(0) == 0)
        def _():
            ubuf[0:8, :] = jnp.zeros((8, CONV_W), F32)

        xt = x_ref[...]
        h = ((xt * _rstd(xt)) * g_ref[...]).astype(BF)
        zbcx_ref[:, 0:IN_SLAB] = _mm(h, w_ref[0])
        zbcx_ref[:, IN_SLAB:2 * IN_SLAB] = _mm(h, w_ref[1])
        qkv_ref[:, 0:IN_SLAB] = _mm(h, w_ref[2])
        qkv_ref[:, IN_SLAB:2 * IN_SLAB] = _mm(h, w_ref[3])
        u = zbcx_ref[:, 512:1024] * zbcx_ref[:, 1024:1536]
        ubuf[8:8 + tm, :] = u
        ub = ubuf[...]
        cv = (cw_ref[0:1, :] * _shift_down(ub, 2, tm) + cw_ref[1:2, :] * _shift_down(ub, 1, tm)
              + cw_ref[2:3, :] * u + cb_ref[...])
        ubuf[0:8, :] = ubuf[tm:tm + 8, :]
        yc = zbcx_ref[:, 0:512] * cv
        ycn_ref[...] = ((yc * _rstd(yc)) * goc_ref[...]).astype(BF)
        zq = qkv_ref[:, 0:512]
        zk = qkv_ref[:, 512:1024]
        rq = lax.rsqrt(_head_mean(zq * zq, gm_ref) + EPS)
        rk = lax.rsqrt(_head_mean(zk * zk, gm_ref) + EPS)
        qkn_ref[:, 0:512] = ((zq * rq) * gq_ref[...]) * (HEAD_DIM ** -0.5)
        qkn_ref[:, 512:1024] = (zk * rk) * gk_ref[...]

    return pl.pallas_call(
        body, name="fwd_mix", grid=(nt,),
        in_specs=[_rows(tm, D_MODEL), _const((1, D_MODEL)), _const((N_CHIPS, D_MODEL, IN_SLAB)),
                  _const((3, CONV_W)), _const((1, CONV_W)), _const((1, CONV_W)), _const((ATTN_W, ATTN_W)),
                  _const((1, ATTN_W)), _const((1, ATTN_W))],
        out_specs=[_rows(tm, 1536), _rows(tm, 1536), _rows(tm, CONV_W), _rows(tm, 1024)],
        out_shape=[jax.ShapeDtypeStruct((t, 1536), F32), jax.ShapeDtypeStruct((t, 1536), F32),
                   jax.ShapeDtypeStruct((t, CONV_W), BF), jax.ShapeDtypeStruct((t, 1024), F32)],
        scratch_shapes=[pltpu.VMEM((tm + 8, CONV_W), F32)],
        compiler_params=_params(48),
    )(x, g_mix, win4, conv_w, conv_b, g_oc, gm, gq8, gk8)


def _place():
    x, y, c = lax.axis_index("x"), lax.axis_index("y"), lax.axis_index("c")
    return x, y, c


def _chip_peer(x, y, k):
    return x ^ (k >> 1), y ^ (k & 1)


def _piece_shape(grad_shape, col_sharded):
    kk, nn = grad_shape
    return (kk // 2, nn // N_CHIPS) if col_sharded else (kk // (2 * N_CHIPS), nn)


def _piece_window(col_sharded, r, cw, s, h):
    if col_sharded:
        return (pl.ds(pl.multiple_of(h * r, 16), r), pl.ds(pl.multiple_of(s * cw, 128), cw))
    return (pl.ds(pl.multiple_of((2 * s + h) * r, 16), r), slice(None))


def _scatter_copies(g_ref, slots_ref, send_sems, recv_sems, base, col_sharded):
    x, y, c = _place()
    r, cw = slots_ref.shape[1:]
    copies = []
    for k in range(1, 8):
        tx, ty, tc = x ^ (k >> 2), y ^ ((k >> 1) & 1), c ^ (k & 1)
        copies.append(pltpu.make_async_remote_copy(
            src_ref=g_ref.at[_piece_window(col_sharded, r, cw, 2 * tx + ty, tc)], dst_ref=slots_ref.at[k - 1],
            send_sem=send_sems.at[base + k - 1], recv_sem=recv_sems.at[base + k - 1],
            device_id=(tx, ty, tc), device_id_type=MESH))
    return copies


def _ride_scatter(first, last, riders, g_refs, slot_refs, send_sems, recv_sems):
    def all_copies():
        out = []
        for j, (_, col_sharded) in enumerate(riders):
            out += _scatter_copies(g_refs[j], slot_refs[j], send_sems, recv_sems, 7 * j, col_sharded)
        return out

    @pl.when(first)
    def _():
        for cp in all_copies():
            cp.start()

    @pl.when(last)
    def _():
        for cp in all_copies():
            cp.wait()


def _rider_specs(riders):
    any_spec = pl.BlockSpec(memory_space=pl.ANY)
    shapes = [jax.ShapeDtypeStruct((7,) + _piece_shape(g.shape, cs), BF) for g, cs in riders]
    sems = [pltpu.SemaphoreType.DMA((7 * len(riders),)), pltpu.SemaphoreType.DMA((7 * len(riders),))] if riders else []
    return [any_spec] * len(riders), shapes, sems


class _Gather:
    def __init__(self, ins, outs, send_sems, recv_sems, local_sems):
        self.ins, self.outs = ins, outs
        self.send_sems, self.recv_sems, self.local_sems = send_sems, recv_sems, local_sems
        self.x, self.y, self.c = _place()
        self.me = 2 * self.x + self.y

    def _push(self, src, dst, w, j, to):
        return pltpu.make_async_remote_copy(src_ref=src, dst_ref=dst, send_sem=self.send_sems.at[6 * w + j],
                                            recv_sem=self.recv_sems.at[6 * w + j], device_id=to, device_id_type=MESH)

    def _half(self, w, h):
        half = self.ins[w].shape[0] // 2
        return pl.ds(pl.multiple_of(h * half, 16), half)

    def _local(self, w):
        return pltpu.make_async_copy(self.ins[w], self.outs[w].at[self.me], self.local_sems.at[w])

    def _ici(self, w, k):
        px, py = _chip_peer(self.x, self.y, k)
        mine = self._half(w, self.c)
        return self._push(self.ins[w].at[mine], self.outs[w].at[self.me, mine], w, k - 1, (px, py, self.c))

    def _landed(self, w, k, h):
        return self.outs[w].at[self.me ^ k, self._half(w, h)]

    def _fwd(self, w, k):
        landed = self._landed(w, k, self.c)
        return self._push(landed, landed, w, 2 + k, (self.x, self.y, 1 - self.c))

    def start(self):
        for w in range(len(self.ins)):
            self._local(w).start()
            for k in (1, 2, 3):
                self._ici(w, k).start()

    def forward(self):
        for w in range(len(self.ins)):
            for k in (1, 2, 3):
                landed = self._landed(w, k, self.c)
                self._push(landed, landed, w, k - 1, (self.x, self.y, self.c)).wait_recv()
                self._fwd(w, k).start()

    def finish(self):
        for w in range(len(self.ins)):
            for k in (1, 2, 3):
                landed = self._landed(w, k, 1 - self.c)
                self._push(landed, landed, w, 2 + k, (self.x, self.y, self.c)).wait_recv()
            for k in (1, 2, 3):
                self._ici(w, k).wait_send()
                self._fwd(w, k).wait_send()
            self._local(w).wait()


def _alibi(h):
    return 2.0 ** (-(h + 1))


CHUNK = 2048


def _mask_tables():
    slopes = jnp.asarray([_alibi(h) for h in range(N_HEADS)], F32)[:, None, None]

    def table(step):
        valid = (step >= 0) & (step <= QBLK)
        return jnp.stack([jnp.where(valid[None], -slopes * (step * d)[None].astype(F32), NEG) for d in DILATIONS])

    i = jnp.arange(QBLK)[:, None]
    j2 = jnp.arange(2 * QBLK)[None, :]
    r2 = jnp.arange(2 * QBLK)[:, None]
    j = jnp.arange(QBLK)[None, :]
    fwd, bwd = table(i + QBLK - j2), table(r2 - j)
    return fwd.reshape(3, N_HEADS // 2, 2 * QBLK, 2 * QBLK), bwd.reshape(3, N_HEADS // 2, 4 * QBLK, QBLK)


def _attn_fwd(qkn, qkv, mb, late=()):
    t = qkn.shape[0]
    nc = t // CHUNK
    nl = len(late)

    def body(*refs):
        qc_ref, kp_ref, kc_ref, vp_ref, vc_ref, mb_ref = refs[0:6]
        o_ref, l_ref = refs[6 + nl:8 + nl]
        ob0, ob1, ob2, lb0, lb1, lb2 = refs[8 + 2 * nl:14 + 2 * nl]
        if nl:
            gather = _Gather(refs[6:6 + nl], refs[8 + nl:8 + 2 * nl], *refs[14 + 2 * nl:17 + 2 * nl])
            step = pl.program_id(0) * nc + pl.program_id(1)
            pl.when(step == 0)(gather.start)
            pl.when(step == 2 * nc)(gather.forward)
            pl.when(step == (N_HEADS // 2) * nc - 1)(gather.finish)
        first = pl.program_id(1) == 0
        lane = lax.broadcasted_iota(jnp.int32, (QBLK, 128), 1)
        lo_half = lane < HEAD_DIM
        kj = lax.broadcasted_iota(jnp.int32, (2 * QBLK, 2 * QBLK), 1)
        no_prev = first & (kj < QBLK)
        obs, lbs = (ob0, ob1, ob2), (lb0, lb1, lb2)

        def by_head(a):
            return jnp.where(lo_half, a, 0.0).astype(BF), jnp.where(lo_half, 0.0, a).astype(BF)

        for di, d in enumerate(DILATIONS):
            span = d * QBLK
            for r in range(d):
                tail = pl.ds(CHUNK - span + r, QBLK, stride=d)
                k_prev = kp_ref[tail, :].astype(BF)
                v_prev = by_head(vp_ref[tail, :])
                for b in range(CHUNK // span):
                    rows = pl.ds(r + span * b, QBLK, stride=d)
                    q0, q1 = by_head(qc_ref[rows, :])
                    k_cur = kc_ref[rows, :].astype(BF)
                    v_cur = by_head(vc_ref[rows, :])
                    s = _mm_nt(jnp.concatenate([q0, q1], axis=0), jnp.concatenate([k_prev, k_cur], axis=0))
                    s = s + mb_ref[di, 0]
                    if b == 0:
                        s = jnp.where(no_prev, NEG, s)
                    m = jnp.max(s, axis=-1, keepdims=True)
                    e = jnp.exp(s - m)
                    den = jnp.sum(e, axis=-1, keepdims=True)
                    eb = e.astype(BF)
                    o = _mm(jnp.concatenate([eb[0:QBLK], eb[QBLK:2 * QBLK]], axis=1),
                            jnp.concatenate([v_prev[0], v_cur[0], v_prev[1], v_cur[1]], axis=0))
                    inv = 1.0 / den
                    lse = m + jnp.log(den)
                    obs[di][rows, :] = o * jnp.where(lo_half, inv[0:QBLK], inv[QBLK:2 * QBLK])
                    lbs[di][rows, :] = jnp.where(lo_half, lse[0:QBLK], lse[QBLK:2 * QBLK])
                    k_prev, v_prev = k_cur, v_cur
        for c0 in range(0, CHUNK, 256):
            rs = slice(c0, c0 + 256)
            l0, l1, l2 = lb0[rs, :], lb1[rs, :], lb2[rs, :]
            mx = jnp.maximum(jnp.maximum(l0, l1), l2)
            w0, w1, w2 = jnp.exp(l0 - mx), jnp.exp(l1 - mx), jnp.exp(l2 - mx)
            tot = w0 + w1 + w2
            o_ref[rs, :] = (ob0[rs, :] * w0 + ob1[rs, :] * w1 + ob2[rs, :] * w2) / tot
            l_ref[rs, :] = mx + jnp.log(tot)

    def cur(col):
        return pl.BlockSpec((CHUNK, 128), lambda hp, n: (n, col + hp))

    def prv(col):
        return pl.BlockSpec((CHUNK, 128), lambda hp, n: (jnp.maximum(n - 1, 0), col + hp))

    out = pl.BlockSpec((CHUNK, 128), lambda hp, n: (n, hp))
    any_spec = pl.BlockSpec(memory_space=pl.ANY)
    sems = [pltpu.SemaphoreType.DMA((6 * nl,)), pltpu.SemaphoreType.DMA((6 * nl,)), pltpu.SemaphoreType.DMA((nl,))]
    res = pl.pallas_call(
        body, name="attn_fwd", grid=(N_HEADS // 2, nc),
        in_specs=[cur(0), prv(4), cur(4), prv(8), cur(8),
                  pl.BlockSpec((3, 1, 2 * QBLK, 2 * QBLK), lambda hp, n: (0, hp, 0, 0))] + [any_spec] * nl,
        out_specs=[out, out] + [any_spec] * nl,
        out_shape=[jax.ShapeDtypeStruct((t, ATTN_W), F32)] * 2
        + [jax.ShapeDtypeStruct((N_CHIPS,) + w.shape, w.dtype) for w in late],
        scratch_shapes=[pltpu.VMEM((CHUNK, 128), F32)] * 6 + (sems if nl else []),
        compiler_params=_params(48, 2),
    )(qkn, qkn, qkn, qkv, qkv, mb, *late)
    return res[0], res[1], list(res[2:])


def _attn_bwd(qkn, qkv, o, lse, do, mb, mbk, riders=()):
    t = qkn.shape[0]
    nc = t // CHUNK
    nr = len(riders)

    def body(*refs):
        (qc_ref, qn_ref, kp_ref, kc_ref, vp_ref, vc_ref, oc_ref, on_ref, lc_ref, ln_ref, dc_ref, dn_ref,
         mb_ref, mbk_ref) = refs[0:14]
        dq_ref, dk_ref, dv_ref = refs[14 + nr:17 + nr]
        if nr:
            step = pl.program_id(0) * nc + pl.program_id(1)
            _ride_scatter(step == 0, step == (N_HEADS // 2) * nc - 1, riders, refs[14:14 + nr],
                          refs[17 + nr:17 + 2 * nr], *refs[17 + 2 * nr:19 + 2 * nr])
        first = pl.program_id(1) == 0
        last = pl.program_id(1) == nc - 1
        lane = lax.broadcasted_iota(jnp.int32, (QBLK, 128), 1)
        lo_half = lane < HEAD_DIM
        kj = lax.broadcasted_iota(jnp.int32, (2 * QBLK, 2 * QBLK), 1)
        no_prev = first & (kj < QBLK)
        ri = lax.broadcasted_iota(jnp.int32, (4 * QBLK, QBLK), 0)
        no_next = last & ((ri & (2 * QBLK - 1)) >= QBLK)

        def by_head(a):
            return jnp.where(lo_half, a, 0.0).astype(BF), jnp.where(lo_half, 0.0, a).astype(BF)

        def query_side(q_ref, d_ref, o_ref_, l_ref_, rows):
            dvals = d_ref[rows, :]
            dd = dvals * o_ref_[rows, :]
            lv = l_ref_[rows, :]
            d0 = jnp.sum(jnp.where(lo_half, dd, 0.0), axis=-1, keepdims=True)
            d1 = jnp.sum(jnp.where(lo_half, 0.0, dd), axis=-1, keepdims=True)
            l0 = jnp.max(jnp.where(lo_half, lv, NEG), axis=-1, keepdims=True)
            l1 = jnp.max(jnp.where(lo_half, NEG, lv), axis=-1, keepdims=True)
            return by_head(q_ref[rows, :]), by_head(dvals), (l0, l1), (d0, d1)

        for di, d in enumerate(DILATIONS):
            span = d * QBLK
            nbk = CHUNK // span
            for r in range(d):
                tail = pl.ds(CHUNK - span + r, QBLK, stride=d)
                k_prev = kp_ref[tail, :]
                kb_prev, km_prev = k_prev.astype(BF), by_head(k_prev)
                vb_prev = vp_ref[tail, :].astype(BF)
                rows0 = pl.ds(r, QBLK, stride=d)
                cur = query_side(qc_ref, dc_ref, oc_ref, lc_ref, rows0)
                for b in range(nbk):
                    rows = pl.ds(r + span * b, QBLK, stride=d)
                    if b == nbk - 1:
                        nxt = query_side(qn_ref, dn_ref, on_ref, ln_ref, rows0)
                    else:
                        nxt = query_side(qc_ref, dc_ref, oc_ref, lc_ref, pl.ds(r + span * (b + 1), QBLK, stride=d))
                    (q0, q1), (do0, do1), (l0, l1), (d0, d1) = cur
                    (qx0, qx1), (dox0, dox1), (lx0, lx1), (dx0, dx1) = nxt
                    k_cur = kc_ref[rows, :]
                    kb_cur, km_cur = k_cur.astype(BF), by_head(k_cur)
                    vb_cur = vc_ref[rows, :].astype(BF)
                    k2 = jnp.concatenate([kb_prev, kb_cur], axis=0)
                    v2 = jnp.concatenate([vb_prev, vb_cur], axis=0)
                    s = _mm_nt(jnp.concatenate([q0, q1], axis=0), k2) + mb_ref[di, 0]
                    if b == 0:
                        s = jnp.where(no_prev, NEG, s)
                    p = jnp.exp(s - jnp.concatenate([l0, l1], axis=0))
                    dp = _mm_nt(jnp.concatenate([do0, do1], axis=0), v2)
                    ds = (p * (dp - jnp.concatenate([d0, d1], axis=0))).astype(BF)
                    dq2 = _mm(jnp.concatenate([ds[0:QBLK], ds[QBLK:2 * QBLK]], axis=1),
                              jnp.concatenate([km_prev[0], km_cur[0], km_prev[1], km_cur[1]], axis=0))
                    if di == 0:
                        dq_ref[rows, :] = dq2
                    else:
                        dq_ref[rows, :] = dq_ref[rows, :] + dq2
                    q4 = jnp.concatenate([q0, qx0, q1, qx1], axis=0)
                    do4 = jnp.concatenate([do0, dox0, do1, dox1], axis=0)
                    s = _mm_nt(q4, kb_cur) + mbk_ref[di, 0]
                    if b == nbk - 1:
                        s = jnp.where(no_next, NEG, s)
                    p = jnp.exp(s - jnp.concatenate([l0, lx0, l1, lx1], axis=0))
                    dv2 = _mm_tn(p.astype(BF), do4)
                    dp = _mm_nt(do4, vb_cur)
                    ds = (p * (dp - jnp.concatenate([d0, dx0, d1, dx1], axis=0))).astype(BF)
                    dk2 = _mm_tn(ds, q4)
                    if di == 0:
                        dk_ref[rows, :] = dk2
                        dv_ref[rows, :] = dv2
                    else:
                        dk_ref[rows, :] = dk_ref[rows, :] + dk2
                        dv_ref[rows, :] = dv_ref[rows, :] + dv2
                    cur = nxt
                    kb_prev, km_prev, vb_prev = kb_cur, km_cur, vb_cur

    def at(shift, col):
        return pl.BlockSpec((CHUNK, 128), lambda hp, n: (jnp.clip(n + shift, 0, nc - 1), col + hp))

    out = pl.BlockSpec((CHUNK, 128), lambda hp, n: (n, hp))
    r_in, r_out, r_sems = _rider_specs(riders)
    res = pl.pallas_call(
        body, name="attn_bwd", grid=(N_HEADS // 2, nc),
        in_specs=[at(0, 0), at(1, 0), at(-1, 4), at(0, 4), at(-1, 8), at(0, 8),
                  at(0, 0), at(1, 0), at(0, 0), at(1, 0), at(0, 0), at(1, 0),
                  pl.BlockSpec((3, 1, 2 * QBLK, 2 * QBLK), lambda hp, n: (0, hp, 0, 0)),
                  pl.BlockSpec((3, 1, 4 * QBLK, QBLK), lambda hp, n: (0, hp, 0, 0))] + r_in,
        out_specs=[out, out, out] + r_in,
        out_shape=[jax.ShapeDtypeStruct((t, ATTN_W), F32)] * 3 + r_out,
        scratch_shapes=r_sems,
        compiler_params=_params(56, 2),
    )(qkn, qkn, qkn, qkn, qkv, qkv, o, o, lse, lse, do, do, mb, mbk, *[g for g, _ in riders])
    return res[0], res[1], res[2], list(res[3:])


def _fwd_ffn(x, ycn, ya, wout, wg4, wu4, g_oa, g_ffn, fcw, fcb, tm, late=()):
    t = x.shape[0]
    nt = t // tm
    nl = len(late)

    def body(*refs):
        x_ref, ycn_ref, ya_ref, wout_ref, wg_ref, wu_ref, goa_ref, gffn_ref, fcw_ref, fcb_ref = refs[0:10]
        x1_ref, gp_ref, up_ref, act_ref, ycat_ref, h2_ref = refs[10 + nl:16 + nl]
        cbuf = refs[16 + 2 * nl]
        if nl:
            gather = _Gather(refs[10:10 + nl], refs[16 + nl:16 + 2 * nl], *refs[17 + 2 * nl:20 + 2 * nl])
            pl.when(pl.program_id(0) == 0)(gather.start)
            pl.when(pl.program_id(0) == nt // 2)(gather.forward)
            pl.when(pl.program_id(0) == nt - 1)(gather.finish)

        @pl.when(pl.program_id(0) == 0)
        def _():
            cbuf[0:8, :] = jnp.zeros((8, D_FF), F32)

        yat = ya_ref[...]
        yan = ((yat * _rstd(yat)) * goa_ref[...]).astype(BF)
        ycn = ycn_ref[...]
        ycat_ref[:, 0:CONV_W] = ycn
        ycat_ref[:, CONV_W:D_MODEL] = yan
        x1 = x_ref[...] + _mm(ycn, wout_ref[0:CONV_W, :]) + _mm(yan, wout_ref[CONV_W:D_MODEL, :])
        x1_ref[...] = x1
        h2 = ((x1 * _rstd(x1)) * gffn_ref[...]).astype(BF)
        h2_ref[...] = h2
        for lo, hi in FF_SLABS:
            gps = _mm_nt(h2, wg_ref[lo:hi, :])
            ups = _mm_nt(h2, wu_ref[lo:hi, :])
            gp_ref[:, lo:hi] = gps
            up_ref[:, lo:hi] = ups
            cbuf[8:8 + tm, lo:hi] = gps
            cb = cbuf[:, lo:hi]
            gate = (fcw_ref[0:1, lo:hi] * _shift_down(cb, 2, tm) + fcw_ref[1:2, lo:hi] * _shift_down(cb, 1, tm)
                    + fcw_ref[2:3, lo:hi] * gps + fcb_ref[:, lo:hi])
            act_ref[:, lo:hi] = ((gate * jax.nn.sigmoid(gate)) * ups).astype(BF)
        cbuf[0:8, :] = cbuf[tm:tm + 8, :]

    any_spec = pl.BlockSpec(memory_space=pl.ANY)
    sems = [pltpu.SemaphoreType.DMA((6 * nl,)), pltpu.SemaphoreType.DMA((6 * nl,)), pltpu.SemaphoreType.DMA((nl,))]
    res = pl.pallas_call(
        body, name="fwd_ffn", grid=(nt,),
        in_specs=[_rows(tm, D_MODEL), _rows(tm, CONV_W), _rows(tm, ATTN_W), _const((D_MODEL, D_MODEL)),
                  _const((D_FF, D_MODEL)), _const((D_FF, D_MODEL)),
                  _const((1, ATTN_W)), _const((1, D_MODEL)), _const((3, D_FF)), _const((1, D_FF))]
        + [any_spec] * nl,
        out_specs=[_rows(tm, D_MODEL), _rows(tm, D_FF), _rows(tm, D_FF), _rows(tm, D_FF),
                   _rows(tm, D_MODEL), _rows(tm, D_MODEL)] + [any_spec] * nl,
        out_shape=[jax.ShapeDtypeStruct((t, D_MODEL), F32), jax.ShapeDtypeStruct((t, D_FF), F32),
                   jax.ShapeDtypeStruct((t, D_FF), F32), jax.ShapeDtypeStruct((t, D_FF), BF),
                   jax.ShapeDtypeStruct((t, D_MODEL), BF), jax.ShapeDtypeStruct((t, D_MODEL), BF)]
        + [jax.ShapeDtypeStruct((N_CHIPS,) + w.shape, w.dtype) for w in late],
        scratch_shapes=[pltpu.VMEM((tm + 8, D_FF), F32)] + (sems if nl else []),
        compiler_params=_params(56),
    )(x, ycn, ya, wout, wg4, wu4, g_oa, g_ffn, fcw, fcb, *late)
    return tuple(res[0:6]) + (list(res[6:]),)


def _fwd_tail(x1, act, p, target, wd4, wpg, wpp4, g_ple, tm):
    t = x1.shape[0]
    nt = t // tm

    def body(x1_ref, act_ref, p_ref, tgt_ref, wd_ref, wpg_ref, wpp_ref, g_ref,
             dx2_ref, h3_ref, ds_ref, dpp_ref, dg_ref, loss_ref, lacc):
        i = pl.program_id(0)

        @pl.when(i == 0)
        def _():
            dg_ref[...] = jnp.zeros_like(dg_ref)
            lacc[...] = jnp.zeros_like(lacc)

        x2 = x1_ref[...]
        for lo, hi in FF_SLABS:
            x2 = x2 + _mm(act_ref[:, lo:hi], wd_ref[lo:hi, :])
        r3 = _rstd(x2)
        xh = x2 * r3
        h3 = (xh * g_ref[...]).astype(BF)
        h3_ref[...] = h3
        sg = jax.nn.sigmoid(_mm(h3, wpg_ref[...]))
        pb = p_ref[...].astype(BF)
        pp = jnp.concatenate([_mm(pb, wpp_ref[s]) for s in range(N_CHIPS)], axis=1)
        err = (x2 + sg * pp) - tgt_ref[...]
        lacc[...] += _colsum(err * err)
        dx3 = err * (1.0 / D_MODEL)
        dpp_ref[...] = (dx3 * sg).astype(BF)
        dsb = ((dx3 * pp) * (sg * (1.0 - sg))).astype(BF)
        ds_ref[...] = dsb
        dh3 = _mm_nt(dsb, wpg_ref[...])
        dg_ref[...] += _colsum(dh3 * xh)
        dx2_ref[...] = dx3 + _norm_bwd(dh3, xh, r3, g_ref[...])

        @pl.when(i == nt - 1)
        def _():
            loss_ref[...] = jnp.full((1, 128), jnp.sum(lacc[...]) * (0.5 / D_MODEL), F32)

    return pl.pallas_call(
        body, name="fwd_tail", grid=(nt,),
        in_specs=[_rows(tm, D_MODEL), _rows(tm, D_FF), _rows(tm, PLE_DIM), _rows(tm, D_MODEL),
                  _const((D_FF, D_MODEL)), _const((D_MODEL, D_MODEL)),
                  _const((N_CHIPS, PLE_DIM, PLE_DIM)), _const((1, D_MODEL))],
        out_specs=[_rows(tm, D_MODEL), _rows(tm, D_MODEL), _rows(tm, D_MODEL), _rows(tm, D_MODEL),
                   pl.BlockSpec((1, D_MODEL), lambda i: (0, 0)), pl.BlockSpec((1, 128), lambda i: (0, 0))],
        out_shape=[jax.ShapeDtypeStruct((t, D_MODEL), F32), jax.ShapeDtypeStruct((t, D_MODEL), BF),
                   jax.ShapeDtypeStruct((t, D_MODEL), BF), jax.ShapeDtypeStruct((t, D_MODEL), BF),
                   jax.ShapeDtypeStruct((1, D_MODEL), F32), jax.ShapeDtypeStruct((1, 128), F32)],
        scratch_shapes=[pltpu.VMEM((1, D_MODEL), F32)],
        compiler_params=_params(48),
    )(x1, act, p, target, wd4, wpg, wpp4, g_ple)


def _bwd_ffn_a(dx2, gp, up, wd4, fcw, fcb, tm, riders=()):
    t = dx2.shape[0]
    nt = t // tm
    nr = len(riders)

    def body(*refs):
        dx2_ref, gp_ref, gph_ref, up_ref, wd_ref, fcw_ref, fcb_ref = refs[0:7]
        dgp_ref, dup_ref, dfcw_ref, dfcb_ref = refs[7 + nr:11 + nr]
        cbuf, dbuf = refs[11 + 2 * nr:13 + 2 * nr]
        i = pl.program_id(0)
        if nr:
            _ride_scatter(i == 0, i == nt - 1, riders, refs[7:7 + nr], refs[11 + nr:11 + 2 * nr],
                          *refs[13 + 2 * nr:15 + 2 * nr])

        @pl.when(i == 0)
        def _():
            dbuf[tm:tm + 8, :] = jnp.zeros((8, D_FF), F32)
            dfcw_ref[...] = jnp.zeros_like(dfcw_ref)
            dfcb_ref[...] = jnp.zeros_like(dfcb_ref)

        not_first_tile = i < nt - 1
        dx2b = dx2_ref[...].astype(BF)
        for lo, hi in FF_SLABS:
            gps = gp_ref[:, lo:hi]
            cbuf[0:8, lo:hi] = jnp.where(not_first_tile, gph_ref[:, lo:hi], 0.0)
            cbuf[8:8 + tm, lo:hi] = gps
            cb = cbuf[:, lo:hi]
            g1 = _shift_down(cb, 1, tm)
            g2 = _shift_down(cb, 2, tm)
            w0, w1, w2 = fcw_ref[0:1, lo:hi], fcw_ref[1:2, lo:hi], fcw_ref[2:3, lo:hi]
            gate = w0 * g2 + w1 * g1 + w2 * gps + fcb_ref[:, lo:hi]
            sg = jax.nn.sigmoid(gate)
            dact = _mm_nt(dx2b, wd_ref[lo:hi, :])
            dup_ref[:, lo:hi] = (dact * (gate * sg)).astype(BF)
            dgate = (dact * up_ref[:, lo:hi]) * (sg * (1.0 + gate * (1.0 - sg)))
            dfcb_ref[:, lo:hi] += _colsum(dgate)
            dfcw_ref[0:1, lo:hi] += _colsum(dgate * g2)
            dfcw_ref[1:2, lo:hi] += _colsum(dgate * g1)
            dfcw_ref[2:3, lo:hi] += _colsum(dgate * gps)
            dbuf[0:tm, lo:hi] = dgate
            db = dbuf[:, lo:hi]
            dgp = w2 * dgate + w1 * _shift_up(db, 1, tm) + w0 * _shift_up(db, 2, tm)
            dgp_ref[:, lo:hi] = dgp.astype(BF)
        dbuf[tm:tm + 8, :] = dbuf[0:8, :]

    r_in, r_out, r_sems = _rider_specs(riders)
    res = pl.pallas_call(
        body, name="bwd_ffn_a", grid=(nt,),
        in_specs=[_rows(tm, D_MODEL, nt), _rows(tm, D_FF, nt), _halo(tm, D_FF, nt), _rows(tm, D_FF, nt),
                  _const((D_FF, D_MODEL)), _const((3, D_FF)), _const((1, D_FF))] + r_in,
        out_specs=[_rows(tm, D_FF, nt), _rows(tm, D_FF, nt),
                   pl.BlockSpec((3, D_FF), lambda i: (0, 0)), pl.BlockSpec((1, D_FF), lambda i: (0, 0))] + r_in,
        out_shape=[jax.ShapeDtypeStruct((t, D_FF), BF), jax.ShapeDtypeStruct((t, D_FF), BF),
                   jax.ShapeDtypeStruct((3, D_FF), F32), jax.ShapeDtypeStruct((1, D_FF), F32)] + r_out,
        scratch_shapes=[pltpu.VMEM((tm + 8, D_FF), F32), pltpu.VMEM((tm + 8, D_FF), F32)] + r_sems,
        compiler_params=_params(56),
    )(dx2, gp, gp, up, wd4, fcw, fcb, *[g for g, _ in riders])
    return res[0], res[1], res[2], res[3], list(res[4:])


def _bwd_ffn_b(dgp, dup, dx2, x1, ya, wg4, wu4, wout, g_ffn, g_oa, tm):
    t = dx2.shape[0]
    nt = t // tm

    def body(dgp_ref, dup_ref, dx2_ref, x1_ref, ya_ref, wg_ref, wu_ref, wout_ref, gffn_ref, goa_ref,
             dx1_ref, dycn_ref, dya_ref, dgffn_ref, dgoa_ref):
        @pl.when(pl.program_id(0) == 0)
        def _():
            dgffn_ref[...] = jnp.zeros_like(dgffn_ref)
            dgoa_ref[...] = jnp.zeros_like(dgoa_ref)

        dh2 = jnp.zeros((tm, D_MODEL), F32)
        for lo, hi in FF_SLABS:
            dh2 = dh2 + _mm(dgp_ref[:, lo:hi], wg_ref[lo:hi, :]) + _mm(dup_ref[:, lo:hi], wu_ref[lo:hi, :])
        x1 = x1_ref[...]
        r2 = _rstd(x1)
        xh = x1 * r2
        dgffn_ref[...] += _colsum(dh2 * xh)
        dx1 = dx2_ref[...] + _norm_bwd(dh2, xh, r2, gffn_ref[...])
        dx1_ref[...] = dx1
        dy = _mm_nt(dx1.astype(BF), wout_ref[...])
        dycn_ref[...] = dy[:, 0:CONV_W]
        dyan = dy[:, CONV_W:D_MODEL]
        yat = ya_ref[...]
        ra = _rstd(yat)
        yah = yat * ra
        dgoa_ref[...] += _colsum(dyan * yah)
        dya_ref[...] = _norm_bwd(dyan, yah, ra, goa_ref[...])

    return pl.pallas_call(
        body, name="bwd_ffn_b", grid=(nt,),
        in_specs=[_rows(tm, D_FF), _rows(tm, D_FF), _rows(tm, D_MODEL), _rows(tm, D_MODEL),
                  _rows(tm, ATTN_W), _const((D_FF, D_MODEL)), _const((D_FF, D_MODEL)),
                  _const((D_MODEL, D_MODEL)), _const((1, D_MODEL)), _const((1, ATTN_W))],
        out_specs=[_rows(tm, D_MODEL), _rows(tm, CONV_W), _rows(tm, ATTN_W),
                   pl.BlockSpec((1, D_MODEL), lambda i: (0, 0)), pl.BlockSpec((1, ATTN_W), lambda i: (0, 0))],
        out_shape=[jax.ShapeDtypeStruct((t, D_MODEL), F32), jax.ShapeDtypeStruct((t, CONV_W), F32),
                   jax.ShapeDtypeStruct((t, ATTN_W), F32),
                   jax.ShapeDtypeStruct((1, D_MODEL), F32), jax.ShapeDtypeStruct((1, ATTN_W), F32)],
        compiler_params=_params(48),
    )(dgp, dup, dx2, x1, ya, wg4, wu4, wout, g_ffn, g_oa)


def _bwd_mix(x, dx1, zbcx, qkv, dycn, dq, dk, dv, win4, conv_w, conv_b, g_oc, g_mix, gm, gq8, gk8, tm):
    t = x.shape[0]
    nt = t // tm

    def body(x_ref, dx1_ref, z_ref, zh_ref, qkv_ref, dycn_ref, dq_ref, dk_ref, dv_ref, w_ref, cw_ref, cb_ref,
             goc_ref, g_ref, gm_ref, gq_ref, gk_ref,
             gx_ref, h1_ref, dz_ref, dcw_ref, dcb_ref, dgoc_ref, dg_ref, dgq_ref, dgk_ref, ubuf, dbuf):
        i = pl.program_id(0)

        @pl.when(i == 0)
        def _():
            dbuf[tm:tm + 8, :] = jnp.zeros((8, CONV_W), F32)
            dcw_ref[...] = jnp.zeros_like(dcw_ref)
            dcb_ref[...] = jnp.zeros_like(dcb_ref)
            dgoc_ref[...] = jnp.zeros_like(dgoc_ref)
            dg_ref[...] = jnp.zeros_like(dg_ref)
            dgq_ref[...] = jnp.zeros_like(dgq_ref)
            dgk_ref[...] = jnp.zeros_like(dgk_ref)

        not_first_tile = i < nt - 1
        zb = z_ref[:, 0:512]
        zc = z_ref[:, 512:1024]
        zx = z_ref[:, 1024:1536]
        u = zc * zx
        ubuf[0:8, :] = jnp.where(not_first_tile, zh_ref[:, 512:1024] * zh_ref[:, 1024:1536], 0.0)
        ubuf[8:8 + tm, :] = u
        ub = ubuf[...]
        u1 = _shift_down(ub, 1, tm)
        u2 = _shift_down(ub, 2, tm)
        w0, w1, w2 = cw_ref[0:1, :], cw_ref[1:2, :], cw_ref[2:3, :]
        cv = w0 * u2 + w1 * u1 + w2 * u + cb_ref[...]
        yc = zb * cv
        rc = _rstd(yc)
        ych = yc * rc
        dycn = dycn_ref[...]
        dgoc_ref[...] += _colsum(dycn * ych)
        dyc = _norm_bwd(dycn, ych, rc, goc_ref[...])
        dcv = dyc * zb
        dcb_ref[...] += _colsum(dcv)
        dcw_ref[0:1, :] += _colsum(dcv * u2)
        dcw_ref[1:2, :] += _colsum(dcv * u1)
        dcw_ref[2:3, :] += _colsum(dcv * u)
        dbuf[0:tm, :] = dcv
        db = dbuf[...]
        du = w2 * dcv + w1 * _shift_up(db, 1, tm) + w0 * _shift_up(db, 2, tm)
        dbuf[tm:tm + 8, :] = dbuf[0:8, :]
        dz_ref[:, 0:512] = (dyc * cv).astype(BF)
        dz_ref[:, 512:1024] = (du * zx).astype(BF)
        dz_ref[:, 1024:1536] = (du * zc).astype(BF)
        for z0, d_ref, gg_ref, acc_ref, sc in ((0, dq_ref, gq_ref, dgq_ref, HEAD_DIM ** -0.5),
                                               (512, dk_ref, gk_ref, dgk_ref, 1.0)):
            z = qkv_ref[:, z0:z0 + 512]
            rr = lax.rsqrt(_head_mean(z * z, gm_ref) + EPS)
            zh = z * rr
            dn = d_ref[...] * sc
            acc_ref[...] += _colsum(dn * zh)
            dzh = dn * gg_ref[...]
            dz_ref[:, 1536 + z0:1536 + z0 + 512] = (rr * (dzh - zh * _head_mean(dzh * zh, gm_ref))).astype(BF)
        dz_ref[:, 2560:3072] = dv_ref[...].astype(BF)
        dh1 = jnp.zeros((tm, D_MODEL), F32)
        for s in range(N_CHIPS):
            dh1 = dh1 + _mm_nt(dz_ref[:, s * IN_SLAB:(s + 1) * IN_SLAB], w_ref[s])
        xt = x_ref[...]
        r1 = _rstd(xt)
        xh = xt * r1
        h1_ref[...] = (xh * g_ref[...]).astype(BF)
        dg_ref[...] += _colsum(dh1 * xh)
        gx_ref[...] = dx1_ref[...] + _norm_bwd(dh1, xh, r1, g_ref[...])

    def acc(width, rows=1):
        return pl.BlockSpec((rows, width), lambda i: (0, 0))

    return pl.pallas_call(
        body, name="bwd_mix", grid=(nt,),
        in_specs=[_rows(tm, D_MODEL, nt), _rows(tm, D_MODEL, nt), _rows(tm, 1536, nt), _halo(tm, 1536, nt),
                  _rows(tm, 1536, nt), _rows(tm, CONV_W, nt), _rows(tm, ATTN_W, nt), _rows(tm, ATTN_W, nt),
                  _rows(tm, ATTN_W, nt), _const((N_CHIPS, D_MODEL, IN_SLAB)),
                  _const((3, CONV_W)), _const((1, CONV_W)), _const((1, CONV_W)), _const((1, D_MODEL)),
                  _const((ATTN_W, ATTN_W)), _const((1, ATTN_W)), _const((1, ATTN_W))],
        out_specs=[_rows(tm, D_MODEL, nt), _rows(tm, D_MODEL, nt), _rows(tm, 3072, nt),
                   acc(CONV_W, 3), acc(CONV_W), acc(CONV_W), acc(D_MODEL), acc(ATTN_W), acc(ATTN_W)],
        out_shape=[jax.ShapeDtypeStruct((t, D_MODEL), F32), jax.ShapeDtypeStruct((t, D_MODEL), BF),
                   jax.ShapeDtypeStruct((t, 3072), BF), jax.ShapeDtypeStruct((3, CONV_W), F32),
                   jax.ShapeDtypeStruct((1, CONV_W), F32), jax.ShapeDtypeStruct((1, CONV_W), F32),
                   jax.ShapeDtypeStruct((1, D_MODEL), F32), jax.ShapeDtypeStruct((1, ATTN_W), F32),
                   jax.ShapeDtypeStruct((1, ATTN_W), F32)],
        scratch_shapes=[pltpu.VMEM((tm + 8, CONV_W), F32), pltpu.VMEM((tm + 8, CONV_W), F32)],
        compiler_params=_params(56),
    )(x, dx1, zbcx, zbcx, qkv, dycn, dq, dk, dv, win4, conv_w, conv_b, g_oc, g_mix, gm, gq8, gk8)


def _wgrad(a, b, tn, tt, name):
    t, k = a.shape
    n = b.shape[1]
    nt = t // tt

    def body(a_ref, b_ref, o_ref, ob_ref):
        @pl.when(pl.program_id(1) == 0)
        def _():
            o_ref[...] = jnp.zeros_like(o_ref)

        o_ref[...] += _mm_tn(a_ref[...].astype(BF), b_ref[...].astype(BF))

        @pl.when(pl.program_id(1) == nt - 1)
        def _():
            ob_ref[...] = o_ref[...].astype(BF)

    spec = pl.BlockSpec((k, tn), lambda j, i: (0, j))
    return pl.pallas_call(
        body, name=name, grid=(n // tn, nt),
        in_specs=[pl.BlockSpec((tt, k), lambda j, i: (i, 0)), pl.BlockSpec((tt, tn), lambda j, i: (i, j))],
        out_specs=[spec, spec],
        out_shape=[jax.ShapeDtypeStruct((k, n), F32), jax.ShapeDtypeStruct((k, n), BF)],
        compiler_params=_params(48, 2),
    )(a, b)


def _gather_weights(shards, pack):
    nw = len(shards)

    def body(*refs):
        ins = refs[:nw]
        pack_ref = refs[nw]
        outs = refs[nw + 1:2 * nw + 1]
        pack_out = refs[2 * nw + 1]
        send_sems, recv_sems, local_sems = refs[2 * nw + 2:]
        x, y, c = _place()
        me = 2 * x + y
        local, remote = [], []

        def sem(w, j):
            return w * 6 + j

        def push(src, dst, w, j, to):
            return pltpu.make_async_remote_copy(src_ref=src, dst_ref=dst, send_sem=send_sems.at[sem(w, j)],
                                                recv_sem=recv_sems.at[sem(w, j)], device_id=to, device_id_type=MESH)

        def half_rows(w, h):
            half = ins[w].shape[0] // 2
            return pl.ds(pl.multiple_of(h * half, 16), half)

        for w in range(nw):
            local.append(pltpu.make_async_copy(ins[w], outs[w].at[me], local_sems.at[w]))
            for k in (1, 2, 3):
                px, py = _chip_peer(x, y, k)
                mine = half_rows(w, c)
                remote.append(push(ins[w].at[mine], outs[w].at[me, mine], w, k - 1, (px, py, c)))
        local.append(pltpu.make_async_copy(pack_ref, pack_out.at[me], local_sems.at[nw]))
        for k in (1, 2, 3):
            px, py = _chip_peer(x, y, k)
            remote.append(push(pack_ref, pack_out.at[me], nw, k - 1, (px, py, c)))
        for cp in local + remote:
            cp.start()
        for w in range(nw):
            for k in (1, 2, 3):
                landed = outs[w].at[me ^ k, half_rows(w, c)]
                push(landed, landed, w, k - 1, (x, y, c)).wait_recv()
                fw = push(landed, landed, w, 2 + k, (x, y, 1 - c))
                fw.start()
                remote.append(fw)
        for k in (1, 2, 3):
            landed = pack_out.at[me ^ k]
            push(landed, landed, nw, k - 1, (x, y, c)).wait_recv()
        for w in range(nw):
            for k in (1, 2, 3):
                landed = outs[w].at[me ^ k, half_rows(w, 1 - c)]
                push(landed, landed, w, 2 + k, (x, y, c)).wait_recv()
        for cp in remote:
            cp.wait_send()
        for cp in local:
            cp.wait()

    any_spec = pl.BlockSpec(memory_space=pl.ANY)
    out_shape = [jax.ShapeDtypeStruct((N_CHIPS,) + s.shape, s.dtype) for s in shards]
    out_shape.append(jax.ShapeDtypeStruct((N_CHIPS,) + pack.shape, pack.dtype))
    return pl.pallas_call(
        body, name="gather_weights",
        in_specs=[any_spec] * (nw + 1), out_specs=[any_spec] * (nw + 1), out_shape=out_shape,
        scratch_shapes=[pltpu.SemaphoreType.DMA(((nw + 1) * 6,)), pltpu.SemaphoreType.DMA(((nw + 1) * 6,)),
                        pltpu.SemaphoreType.DMA((nw + 1,))],
    )(*shards, pack)


def _adamw(w, g, m, v):
    m = ADAM_B1 * m + (1.0 - ADAM_B1) * g
    v = ADAM_B2 * v + (1.0 - ADAM_B2) * (g * g)
    m_hat = m / (1.0 - ADAM_B1 ** ADAM_STEP)
    v_hat = v / (1.0 - ADAM_B2 ** ADAM_STEP)
    delta = -ADAM_LR * (m_hat / (jnp.sqrt(v_hat) + ADAM_EPS) + ADAM_WD * w)
    return delta, m, v


def _scatter_alone(g16, col_sharded, name):
    def body(g_ref, slots_ref, send_sems, recv_sems):
        copies = _scatter_copies(g_ref, slots_ref, send_sems, recv_sems, 0, col_sharded)
        for cp in copies:
            cp.start()
        for cp in copies:
            cp.wait()

    any_spec = pl.BlockSpec(memory_space=pl.ANY)
    return pl.pallas_call(
        body, name=name, in_specs=[any_spec], out_specs=any_spec,
        out_shape=jax.ShapeDtypeStruct((7,) + _piece_shape(g16.shape, col_sharded), BF),
        scratch_shapes=[pltpu.SemaphoreType.DMA((7,)), pltpu.SemaphoreType.DMA((7,))],
    )(g16)


def _finish_reduce(grad, slots, col_sharded, name):
    r, cw = _piece_shape(grad.shape, col_sharded)
    chunk = 32
    assert r % chunk == 0

    def body(g_hbm, slots_ref, full, own, lsem, c_send, c_recv):
        x, y, c = _place()
        cp = pltpu.make_async_copy(g_hbm.at[_piece_window(col_sharded, r, cw, 2 * x + y, c)], own, lsem)
        cp.start()
        cp.wait()
        mine = pl.multiple_of(c * r, 8)

        def add(j, carry):
            rows = pl.ds(pl.multiple_of(j * chunk, 8), chunk)
            tot = own[rows, :]
            for k in range(7):
                tot = tot + slots_ref[k, rows, :].astype(F32)
            full[pl.ds(mine + pl.multiple_of(j * chunk, 8), chunk), :] = tot
            return carry

        lax.fori_loop(0, r // chunk, add, 0)
        half = full.at[pl.ds(mine, r), :]
        swap = pltpu.make_async_remote_copy(src_ref=half, dst_ref=half, send_sem=c_send, recv_sem=c_recv,
                                            device_id=(x, y, 1 - c), device_id_type=MESH)
        swap.start()
        swap.wait()

    vmem = pl.BlockSpec(memory_space=pltpu.VMEM)
    return pl.pallas_call(
        body, name=name, in_specs=[pl.BlockSpec(memory_space=pl.ANY), vmem], out_specs=vmem,
        out_shape=jax.ShapeDtypeStruct((2 * r, cw), F32),
        scratch_shapes=[pltpu.VMEM((r, cw), F32), pltpu.SemaphoreType.DMA, pltpu.SemaphoreType.DMA,
                        pltpu.SemaphoreType.DMA],
        compiler_params=pltpu.CompilerParams(vmem_limit_bytes=32 * MIB),
    )(grad, slots)


def _adamw_big(g, w, m, v, name):
    vr, vc = w.shape
    assert g.shape == w.shape
    rows = 64

    def body(g_ref, w_ref, m_ref, v_ref, go_ref, do_ref, mo_ref, vo_ref):
        gg = g_ref[...]
        delta, mn, vn = _adamw(w_ref[...], gg, m_ref[...], v_ref[...])
        go_ref[...] = gg
        do_ref[...] = delta
        mo_ref[...] = mn
        vo_ref[...] = vn

    blk = pl.BlockSpec((rows, vc), lambda i: (i, 0))
    shard = jax.ShapeDtypeStruct((vr, vc), F32)
    return pl.pallas_call(
        body, name=name, grid=(vr // rows,),
        in_specs=[blk, blk, blk, blk], out_specs=[blk] * 4,
        out_shape=[shard] * 4, compiler_params=_params(32),
    )(g, w, m, v)


def _allreduce_small(pack):
    rows = pack.shape[0]

    def body(p_ref, o_ref, slots, send_sems, recv_sems):
        x, y, c = _place()
        me = 4 * x + 2 * y + c
        slots[me] = p_ref[...]
        sends = []
        for k in range(1, 8):
            cp = pltpu.make_async_remote_copy(
                src_ref=p_ref, dst_ref=slots.at[me], send_sem=send_sems.at[k - 1], recv_sem=recv_sems.at[k - 1],
                device_id=(x ^ (k >> 2), y ^ ((k >> 1) & 1), c ^ (k & 1)), device_id_type=MESH)
            cp.start()
            sends.append(cp)
        for cp in sends:
            cp.wait()
        tot = slots[0]
        for j in range(1, 8):
            tot = tot + slots[j]
        o_ref[...] = tot

    vmem = pl.BlockSpec(memory_space=pltpu.VMEM)
    return pl.pallas_call(
        body, name="allreduce_small", in_specs=[vmem], out_specs=vmem,
        out_shape=jax.ShapeDtypeStruct(pack.shape, F32),
        scratch_shapes=[pltpu.VMEM((8, rows, D_MODEL), F32), pltpu.SemaphoreType.DMA((7,)),
                        pltpu.SemaphoreType.DMA((7,))],
    )(pack)


def _adamw_small(ws, gs, ms, vs):
    n = len(ws)

    def body(*refs):
        w_refs, g_refs, m_refs, v_refs = refs[0:n], refs[n:2 * n], refs[2 * n:3 * n], refs[3 * n:4 * n]
        d_refs, mo_refs, vo_refs = refs[4 * n:5 * n], refs[5 * n:6 * n], refs[6 * n:7 * n]
        for j in range(n):
            delta, mn, vn = _adamw(w_refs[j][...], g_refs[j][...], m_refs[j][...], v_refs[j][...])
            d_refs[j][...] = delta
            mo_refs[j][...] = mn
            vo_refs[j][...] = vn

    vmem = pl.BlockSpec(memory_space=pltpu.VMEM)
    shapes = [jax.ShapeDtypeStruct(w.shape, F32) for w in ws]
    outs = pl.pallas_call(
        body, name="adamw_small", in_specs=[vmem] * (4 * n), out_specs=[vmem] * (3 * n), out_shape=shapes * 3,
    )(*ws, *gs, *ms, *vs)
    return outs[0:n], outs[n:2 * n], outs[2 * n:3 * n]


def _local_step(x, p, target, wts, late=None):
    (win4, wout, wg4, wu4, wd4, wpg, wpp4, conv_w, fcw, g_mix, conv_b, gq, gk, g_oc, g_oa, g_ffn, fcb, g_ple) = wts
    comm = late is not None
    gm = jnp.kron(jnp.eye(N_HEADS, dtype=F32), jnp.full((HEAD_DIM, HEAD_DIM), 1.0 / HEAD_DIM, F32)).astype(BF)
    gq8, gk8 = jnp.tile(gq, (1, N_HEADS)), jnp.tile(gk, (1, N_HEADS))
    mb, mbk = _mask_tables()
    zbcx, qkv, ycn, qkn = _fwd_mix(x, g_mix, win4, conv_w, conv_b, g_oc, gm, gq8, gk8, 512)
    ya, lse, gathered = _attn_fwd(qkn, qkv, mb, late[0:3] if comm else ())
    if comm:
        wout, wg4, wu4 = (g.reshape(-1, D_MODEL) for g in gathered)
    x1, gp, up, act, ycat, h2, gathered = _fwd_ffn(x, ycn, ya, wout, wg4, wu4, g_oa, g_ffn, fcw, fcb, 256,
                                                    late[3:6] if comm else ())
    if comm:
        wd4, wpg, wpp4 = gathered
        wd4, wpg = wd4.reshape(D_FF, D_MODEL), wpg.reshape(D_MODEL, D_MODEL)
    dx2, h3, ds, dpp, dg_ple, loss = _fwd_tail(x1, act, p, target, wd4, wpg, wpp4, g_ple, 512)
    big, big16, slots = {}, {}, {}

    def wgrad(name, a, b, tn):
        big[name], big16[name] = _wgrad(a, b, tn, 1024, "wgrad_" + name)
        return (big16[name], _COL_SHARDED[name])

    riders = [wgrad("w_down", act, dx2, 512), wgrad("w_ple_gate", h3, ds, 1024), wgrad("w_ple_proj", p, dpp, 1024)]
    dgp, dup, dfcw, dfcb, got = _bwd_ffn_a(dx2, gp, up, wd4, fcw, fcb, 256, riders if comm else ())
    slots.update(zip(("w_down", "w_ple_gate", "w_ple_proj"), got))
    riders = [wgrad("w_gate", dgp, h2, 512), wgrad("w_up", dup, h2, 512)]
    dx1, dycn, dya, dg_ffn, dg_oa = _bwd_ffn_b(dgp, dup, dx2, x1, ya, wg4, wu4, wout, g_ffn, g_oa, 512)
    riders.append(wgrad("w_out", ycat, dx1, 1024))
    dq, dk, dv, got = _attn_bwd(qkn, qkv, ya, lse, dya, mb, mbk, riders if comm else ())
    slots.update(zip(("w_gate", "w_up", "w_out"), got))
    grad_x, h1, dz, dcw, dcb, dg_oc, dg_mix, dgq8, dgk8 = _bwd_mix(
        x, dx1, zbcx, qkv, dycn, dq, dk, dv, win4, conv_w, conv_b, g_oc, g_mix, gm, gq8, gk8, 512)
    g16, cs = wgrad("w_in", h1, dz, 1536)
    if comm:
        slots["w_in"] = _scatter_alone(g16, cs, "scatter_w_in")
    dgq = dgq8.reshape(N_HEADS, HEAD_DIM).sum(axis=0, keepdims=True)
    dgk = dgk8.reshape(N_HEADS, HEAD_DIM).sum(axis=0, keepdims=True)
    small = dict(g_mix=dg_mix, conv_w=dcw, conv_b=dcb, q_norm_g=dgq, k_norm_g=dgk, g_out_conv=dg_oc,
                 g_out_attn=dg_oa, g_ffn=dg_ffn, ffn_conv_w=dfcw, ffn_conv_b=dfcb, g_ple=dg_ple)
    return loss[0, 0], grad_x, big, slots, small


_SMALL_ROWS = 24


def _pack_small(s):
    z64 = jnp.zeros((1, 1024 - 512 - 128), F32)
    rows = [s["g_mix"], s["g_ffn"], s["g_ple"],
            jnp.concatenate([s["conv_b"], s["g_out_conv"]], axis=1),
            jnp.concatenate([s["g_out_attn"], s["q_norm_g"], s["k_norm_g"], z64], axis=1),
            jnp.pad(s["conv_w"], ((0, 0), (0, 512))),
            jnp.pad(s["ffn_conv_b"], ((0, 0), (0, 3072 - D_FF))).reshape(3, 1024),
            jnp.pad(s["ffn_conv_w"], ((0, 0), (0, 3072 - D_FF))).reshape(9, 1024),
            jnp.zeros((_SMALL_ROWS - 20, 1024), F32)]
    return jnp.concatenate(rows, axis=0)


def _unpack_small(t):
    return dict(g_mix=t[0:1], g_ffn=t[1:2], g_ple=t[2:3], conv_b=t[3:4, 0:512], g_out_conv=t[3:4, 512:1024],
                g_out_attn=t[4:5, 0:512], q_norm_g=t[4:5, 512:576], k_norm_g=t[4:5, 576:640],
                conv_w=t[5:8, 0:512], ffn_conv_b=t[8:11].reshape(1, 3072)[:, :D_FF],
                ffn_conv_w=t[11:20].reshape(3, 3072)[:, :D_FF])


_BIG = ("w_in", "w_out", "w_gate", "w_up", "w_down", "w_ple_gate", "w_ple_proj")
_COL_SHARDED = dict(w_in=True, w_out=False, w_gate=False, w_up=False, w_down=False, w_ple_gate=False, w_ple_proj=True)
_TRANSPOSED = ("w_gate", "w_up")
_WEIGHTS = ("g_mix", "w_in", "conv_w", "conv_b", "q_norm_g", "k_norm_g", "g_out_conv", "g_out_attn", "w_out",
            "g_ffn", "w_gate", "w_up", "ffn_conv_w", "ffn_conv_b", "w_down", "g_ple", "w_ple_gate", "w_ple_proj")


def kernel(x, p, g_mix, w_in, conv_w, conv_b, q_norm_g, k_norm_g, g_out_conv, g_out_attn, w_out, g_ffn, w_gate, w_up, ffn_conv_w, ffn_conv_b, w_down, g_ple, w_ple_gate, w_ple_proj, loss_target, m_g_mix, m_w_in, m_conv_w, m_conv_b, m_q_norm_g, m_k_norm_g, m_g_out_conv, m_g_out_attn, m_w_out, m_g_ffn, m_w_gate, m_w_up, m_ffn_conv_w, m_ffn_conv_b, m_w_down, m_g_ple, m_w_ple_gate, m_w_ple_proj, v_g_mix, v_w_in, v_conv_w, v_conv_b, v_q_norm_g, v_k_norm_g, v_g_out_conv, v_g_out_attn, v_w_out, v_g_ffn, v_w_gate, v_w_up, v_ffn_conv_w, v_ffn_conv_b, v_w_down, v_g_ple, v_w_ple_gate, v_w_ple_proj):
    w = dict(g_mix=g_mix, w_in=w_in, conv_w=conv_w, conv_b=conv_b, q_norm_g=q_norm_g, k_norm_g=k_norm_g,
             g_out_conv=g_out_conv, g_out_attn=g_out_attn, w_out=w_out, g_ffn=g_ffn, w_gate=w_gate, w_up=w_up,
             ffn_conv_w=ffn_conv_w, ffn_conv_b=ffn_conv_b, w_down=w_down, g_ple=g_ple, w_ple_gate=w_ple_gate,
             w_ple_proj=w_ple_proj)
    m = dict(g_mix=m_g_mix, w_in=m_w_in, conv_w=m_conv_w, conv_b=m_conv_b, q_norm_g=m_q_norm_g, k_norm_g=m_k_norm_g,
             g_out_conv=m_g_out_conv, g_out_attn=m_g_out_attn, w_out=m_w_out, g_ffn=m_g_ffn, w_gate=m_w_gate,
             w_up=m_w_up, ffn_conv_w=m_ffn_conv_w, ffn_conv_b=m_ffn_conv_b, w_down=m_w_down, g_ple=m_g_ple,
             w_ple_gate=m_w_ple_gate, w_ple_proj=m_w_ple_proj)
    v = dict(g_mix=v_g_mix, w_in=v_w_in, conv_w=v_conv_w, conv_b=v_conv_b, q_norm_g=v_q_norm_g, k_norm_g=v_k_norm_g,
             g_out_conv=v_g_out_conv, g_out_attn=v_g_out_attn, w_out=v_w_out, g_ffn=v_g_ffn, w_gate=v_w_gate,
             w_up=v_w_up, ffn_conv_w=v_ffn_conv_w, ffn_conv_b=v_ffn_conv_b, w_down=v_w_down, g_ple=v_g_ple,
             w_ple_gate=v_w_ple_gate, w_ple_proj=v_w_ple_proj)
    mats = [k for k, a in w.items() if a.ndim == 3]
    w = {k: (a[0] if k in mats else a) for k, a in w.items()}
    m = {k: (a[0] if k in mats else a) for k, a in m.items()}
    v = {k: (a[0] if k in mats else a) for k, a in v.items()}
    for n in _TRANSPOSED:
        w[n], m[n], v[n] = w[n].T, m[n].T, v[n].T
    chip = 2 * lax.axis_index("x") + lax.axis_index("y")

    late = [w[n].astype(BF) for n in ("w_out", "w_gate", "w_up", "w_down", "w_ple_gate", "w_ple_proj")]
    pack = jnp.pad(jnp.concatenate([w["conv_w"], w["ffn_conv_w"]], axis=1), ((0, 5), (0, 1024 - 128 - D_FF_SHARD)))
    win4, pack4 = _gather_weights([w["w_in"].astype(BF)], pack)
    conv_w_full = pack4[:, 0:3, 0:128].transpose(1, 0, 2).reshape(3, CONV_W)
    fcw_full = pack4[:, 0:3, 128:128 + D_FF_SHARD].transpose(1, 0, 2).reshape(3, D_FF)
    wts = (win4, None, None, None, None, None, None, conv_w_full, fcw_full, w["g_mix"], w["conv_b"], w["q_norm_g"],
           w["k_norm_g"], w["g_out_conv"], w["g_out_attn"], w["g_ffn"], w["ffn_conv_b"], w["g_ple"])

    loss, grad_x, big, slots, small = _local_step(x[0], p[0, 0], loss_target[0], wts, late)
    loss = lax.psum(loss, ("x", "y", "c"))

    grads, deltas, new_m, new_v = {}, {}, {}, {}
    for name in _BIG:
        total = _finish_reduce(big[name], slots[name], _COL_SHARDED[name], "finish_" + name)
        grads[name], deltas[name], new_m[name], new_v[name] = _adamw_big(total, w[name], m[name], v[name],
                                                                         "adamw_" + name)
    tot = _unpack_small(_allreduce_small(_pack_small(small)))
    tot["conv_w"] = lax.dynamic_slice_in_dim(tot["conv_w"], chip * 128, 128, axis=1)
    tot["ffn_conv_w"] = lax.dynamic_slice_in_dim(tot["ffn_conv_w"], chip * D_FF_SHARD, D_FF_SHARD, axis=1)
    names = [n for n in _WEIGHTS if n not in _BIG]
    d_s, m_s, v_s = _adamw_small([w[n] for n in names], [tot[n] for n in names], [m[n] for n in names],
                                 [v[n] for n in names])
    for j, n in enumerate(names):
        grads[n], deltas[n], new_m[n], new_v[n] = tot[n], d_s[j], m_s[j], v_s[j]

    out = [loss, grad_x[None]]
    for group in (grads, deltas, new_m, new_v):
        for n in _TRANSPOSED:
            group[n] = group[n].T
        out += [group[n][None] if n in mats else group[n] for n in _WEIGHTS]
    return tuple(out)
```

```python
import jax
import jax.numpy as jnp
from jax import lax
from jax.experimental import pallas as pl
from jax.experimental.pallas import tpu as pltpu

D_MODEL = 1024
CONV_W = 512
N_HEADS = 8
HEAD_DIM = 64
ATTN_W = 512
D_FF = 2816
D_FF_SHARD = 704
FF_SLABS = ((0, 1408), (1408, 2816))
IN_SLAB = 768
PLE_DIM = 256
N_CHIPS = 4
QBLK = 128
DILATIONS = (1, 4, 16)
EPS = 1e-6
NEG = -1e30
MESH = pl.DeviceIdType.MESH

ADAM_LR = 0.001
ADAM_B1 = 0.9
ADAM_B2 = 0.999
ADAM_EPS = 1e-08
ADAM_WD = 0.01
ADAM_STEP = 10

BF = jnp.bfloat16
F32 = jnp.float32
MIB = 1024 * 1024


def _mm(a, b):
    return jnp.dot(a, b, preferred_element_type=F32)


def _mm_nt(a, b):
    return lax.dot_general(a, b, (((1,), (1,)), ((), ())), preferred_element_type=F32)


def _mm_tn(a, b):
    return lax.dot_general(a, b, (((0,), (0,)), ((), ())), preferred_element_type=F32)


def _rstd(a):
    return lax.rsqrt(jnp.mean(a * a, axis=-1, keepdims=True) + EPS)


def _norm_bwd(dy, xh, r, g):
    dxh = dy * g
    return r * (dxh - xh * jnp.mean(dxh * xh, axis=-1, keepdims=True))


def _colsum(a):
    return jnp.sum(a, axis=0, keepdims=True)


def _head_mean(a, gm_ref):
    return _mm(a.astype(BF), gm_ref[...])


def _shift_down(buf, k, tm):
    return pltpu.roll(buf, k, axis=0)[8:8 + tm]


def _shift_up(buf, k, tm):
    return pltpu.roll(buf, tm + 8 - k, axis=0)[0:tm]


def _params(vmem_mib, n_grid=1):
    return pltpu.CompilerParams(dimension_semantics=("arbitrary",) * n_grid, vmem_limit_bytes=vmem_mib * MIB)


def _const(shape):
    n = len(shape)
    return pl.BlockSpec(shape, lambda *_: (0,) * n, pipeline_mode=pl.Buffered(1))


def _rows(tm, width, rev_of=None):
    if rev_of is None:
        return pl.BlockSpec((tm, width), lambda i: (i, 0))
    return pl.BlockSpec((tm, width), lambda i: (rev_of - 1 - i, 0))


def _halo(tm, width, nt):
    return pl.BlockSpec((8, width), lambda i: (jnp.maximum((nt - 1 - i) * (tm // 8) - 1, 0), 0))


def _fwd_mix(x, g_mix, win4, conv_w, conv_b, g_oc, gm, gq8, gk8, tm):
    t = x.shape[0]
    nt = t // tm

    def body(x_ref, g_ref, w_ref, cw_ref, cb_ref, goc_ref, gm_ref, gq_ref, gk_ref,
             zbcx_ref, qkv_ref, ycn_ref, qkn_ref, ubuf):
        @pl.when(pl.program_id(0) == 0)
        def _():
            ubuf[0:8, :] = jnp.zeros((8, CONV_W), F32)

        xt = x_ref[...]
        h = ((xt * _rstd(xt)) * g_ref[...]).astype(BF)
        zbcx_ref[:, 0:IN_SLAB] = _mm(h, w_ref[0])
        zbcx_ref[:, IN_SLAB:2 * IN_SLAB] = _mm(h, w_ref[1])
        qkv_ref[:, 0:IN_SLAB] = _mm(h, w_ref[2])
        qkv_ref[:, IN_SLAB:2 * IN_SLAB] = _mm(h, w_ref[3])
        u = zbcx_ref[:, 512:1024] * zbcx_ref[:, 1024:1536]
        ubuf[8:8 + tm, :] = u
        ub = ubuf[...]
        cv = (cw_ref[0:1, :] * _shift_down(ub, 2, tm) + cw_ref[1:2, :] * _shift_down(ub, 1, tm)
              + cw_ref[2:3, :] * u + cb_ref[...])
        ubuf[0:8, :] = ubuf[tm:tm + 8, :]
        yc = zbcx_ref[:, 0:512] * cv
        ycn_ref[...] = ((yc * _rstd(yc)) * goc_ref[...]).astype(BF)
        zq = qkv_ref[:, 0:512]
        zk = qkv_ref[:, 512:1024]
        rq = lax.rsqrt(_head_mean(zq * zq, gm_ref) + EPS)
        rk = lax.rsqrt(_head_mean(zk * zk, gm_ref) + EPS)
        qkn_ref[:, 0:512] = ((zq * rq) * gq_ref[...]) * (HEAD_DIM ** -0.5)
        qkn_ref[:, 512:1024] = (zk * rk) * gk_ref[...]

    return pl.pallas_call(
        body, name="fwd_mix", grid=(nt,),
        in_specs=[_rows(tm, D_MODEL), _const((1, D_MODEL)), _const((N_CHIPS, D_MODEL, IN_SLAB)),
                  _const((3, CONV_W)), _const((1, CONV_W)), _const((1, CONV_W)), _const((ATTN_W, ATTN_W)),
                  _const((1, ATTN_W)), _const((1, ATTN_W))],
        out_specs=[_rows(tm, 1536), _rows(tm, 1536), _rows(tm, CONV_W), _rows(tm, 1024)],
        out_shape=[jax.ShapeDtypeStruct((t, 1536), F32), jax.ShapeDtypeStruct((t, 1536), F32),
                   jax.ShapeDtypeStruct((t, CONV_W), BF), jax.ShapeDtypeStruct((t, 1024), F32)],
        scratch_shapes=[pltpu.VMEM((tm + 8, CONV_W), F32)],
        compiler_params=_params(48),
    )(x, g_mix, win4, conv_w, conv_b, g_oc, gm, gq8, gk8)


def _place():
    x, y, c = lax.axis_index("x"), lax.axis_index("y"), lax.axis_index("c")
    return x, y, c


def _chip_peer(x, y, k):
    return x ^ (k >> 1), y ^ (k & 1)


def _piece_shape(grad_shape, col_sharded):
    kk, nn = grad_shape
    return (kk // 2, nn // N_CHIPS) if col_sharded else (kk // (2 * N_CHIPS), nn)


def _piece_window(col_sharded, r, cw, s, h):
    if col_sharded:
        return (pl.ds(pl.multiple_of(h * r, 16), r), pl.ds(pl.multiple_of(s * cw, 128), cw))
    return (pl.ds(pl.multiple_of((2 * s + h) * r, 16), r), slice(None))


def _scatter_copies(g_ref, slots_ref, send_sems, recv_sems, base, col_sharded):
    x, y, c = _place()
    r, cw = slots_ref.shape[1:]
    copies = []
    for k in range(1, 8):
        tx, ty, tc = x ^ (k >> 2), y ^ ((k >> 1) & 1), c ^ (k & 1)
        copies.append(pltpu.make_async_remote_copy(
            src_ref=g_ref.at[_piece_window(col_sharded, r, cw, 2 * tx + ty, tc)], dst_ref=slots_ref.at[k - 1],
            send_sem=send_sems.at[base + k - 1], recv_sem=recv_sems.at[base + k - 1],
            device_id=(tx, ty, tc), device_id_type=MESH))
    return copies


def _ride_scatter(first, last, riders, g_refs, slot_refs, send_sems, recv_sems):
    def all_copies():
        out = []
        for j, (_, col_sharded) in enumerate(riders):
            out += _scatter_copies(g_refs[j], slot_refs[j], send_sems, recv_sems, 7 * j, col_sharded)
        return out

    @pl.when(first)
    def _():
        for cp in all_copies():
            cp.start()

    @pl.when(last)
    def _():
        for cp in all_copies():
            cp.wait()


def _rider_specs(riders):
    any_spec = pl.BlockSpec(memory_space=pl.ANY)
    shapes = [jax.ShapeDtypeStruct((7,) + _piece_shape(g.shape, cs), BF) for g, cs in riders]
    sems = [pltpu.SemaphoreType.DMA((7 * len(riders),)), pltpu.SemaphoreType.DMA((7 * len(riders),))] if riders else []
    return [any_spec] * len(riders), shapes, sems


class _Gather:
    def __init__(self, ins, outs, send_sems, recv_sems, local_sems):
        self.ins, self.outs = ins, outs
        self.send_sems, self.recv_sems, self.local_sems = send_sems, recv_sems, local_sems
        self.x, self.y, self.c = _place()
        self.me = 2 * self.x + self.y

    def _push(self, src, dst, w, j, to):
        return pltpu.make_async_remote_copy(src_ref=src, dst_ref=dst, send_sem=self.send_sems.at[6 * w + j],
                                            recv_sem=self.recv_sems.at[6 * w + j], device_id=to, device_id_type=MESH)

    def _half(self, w, h):
        half = self.ins[w].shape[0] // 2
        return pl.ds(pl.multiple_of(h * half, 16), half)

    def _local(self, w):
        return pltpu.make_async_copy(self.ins[w], self.outs[w].at[self.me], self.local_sems.at[w])

    def _ici(self, w, k):
        px, py = _chip_peer(self.x, self.y, k)
        mine = self._half(w, self.c)
        return self._push(self.ins[w].at[mine], self.outs[w].at[self.me, mine], w, k - 1, (px, py, self.c))

    def _landed(self, w, k, h):
        return self.outs[w].at[self.me ^ k, self._half(w, h)]

    def _fwd(self, w, k):
        landed = self._landed(w, k, self.c)
        return self._push(landed, landed, w, 2 + k, (self.x, self.y, 1 - self.c))

    def start(self):
        for w in range(len(self.ins)):
            self._local(w).start()
            for k in (1, 2, 3):
                self._ici(w, k).start()

    def forward(self):
        for w in range(len(self.ins)):
            for k in (1, 2, 3):
                landed = self._landed(w, k, self.c)
                self._push(landed, landed, w, k - 1, (self.x, self.y, self.c)).wait_recv()
                self._fwd(w, k).start()

    def finish(self):
        for w in range(len(self.ins)):
            for k in (1, 2, 3):
                landed = self._landed(w, k, 1 - self.c)
                self._push(landed, landed, w, 2 + k, (self.x, self.y, self.c)).wait_recv()
            for k in (1, 2, 3):
                self._ici(w, k).wait_send()
                self._fwd(w, k).wait_send()
            self._local(w).wait()


def _alibi(h):
    return 2.0 ** (-(h + 1))


CHUNK = 2048


def _mask_table():
    slopes = jnp.asarray([_alibi(h) for h in range(N_HEADS)], F32)[:, None, None]
    step = jnp.arange(QBLK)[:, None] + QBLK - jnp.arange(2 * QBLK)[None, :]
    valid = (step >= 0) & (step <= QBLK)
    tab = jnp.stack([jnp.where(valid[None], -slopes * (step * d)[None].astype(F32), NEG) for d in DILATIONS])
    return tab.reshape(3, N_HEADS // 2, 2 * QBLK, 2 * QBLK)


def _attn_fwd(qkn, qkv, mb, late=()):
    t = qkn.shape[0]
    nc = t // CHUNK
    nl = len(late)

    def body(*refs):
        qc_ref, kp_ref, kc_ref, vp_ref, vc_ref, mb_ref = refs[0:6]
        o_ref, l_ref = refs[6 + nl:8 + nl]
        ob0, ob1, ob2, lb0, lb1, lb2 = refs[8 + 2 * nl:14 + 2 * nl]
        if nl:
            gather = _Gather(refs[6:6 + nl], refs[8 + nl:8 + 2 * nl], *refs[14 + 2 * nl:17 + 2 * nl])
            step = pl.program_id(0) * nc + pl.program_id(1)
            pl.when(step == 0)(gather.start)
            pl.when(step == 2 * nc)(gather.forward)
            pl.when(step == (N_HEADS // 2) * nc - 1)(gather.finish)
        first = pl.program_id(1) == 0
        lane = lax.broadcasted_iota(jnp.int32, (QBLK, 128), 1)
        lo_half = lane < HEAD_DIM
        kj = lax.broadcasted_iota(jnp.int32, (2 * QBLK, 2 * QBLK), 1)
        no_prev = first & (kj < QBLK)
        obs, lbs = (ob0, ob1, ob2), (lb0, lb1, lb2)

        def by_head(a):
            return jnp.where(lo_half, a, 0.0).astype(BF), jnp.where(lo_half, 0.0, a).astype(BF)

        for di, d in enumerate(DILATIONS):
            span = d * QBLK
            for r in range(d):
                tail = pl.ds(CHUNK - span + r, QBLK, stride=d)
                k_prev = kp_ref[tail, :].astype(BF)
                v_prev = by_head(vp_ref[tail, :])
                for b in range(CHUNK // span):
                    rows = pl.ds(r + span * b, QBLK, stride=d)
                    q0, q1 = by_head(qc_ref[rows, :])
                    k_cur = kc_ref[rows, :].astype(BF)
                    v_cur = by_head(vc_ref[rows, :])
                    s = _mm_nt(jnp.concatenate([q0, q1], axis=0), jnp.concatenate([k_prev, k_cur], axis=0))
                    s = s + mb_ref[di, 0]
                    if b == 0:
                        s = jnp.where(no_prev, NEG, s)
                    m = jnp.max(s, axis=-1, keepdims=True)
                    e = jnp.exp(s - m)
                    den = jnp.sum(e, axis=-1, keepdims=True)
                    eb = e.astype(BF)
                    o = _mm(jnp.concatenate([eb[0:QBLK], eb[QBLK:2 * QBLK]], axis=1),
                            jnp.concatenate([v_prev[0], v_cur[0], v_prev[1], v_cur[1]], axis=0))
                    inv = 1.0 / den
                    lse = m + jnp.log(den)
                    obs[di][rows, :] = o * jnp.where(lo_half, inv[0:QBLK], inv[QBLK:2 * QBLK])
                    lbs[di][rows, :] = jnp.where(lo_half, lse[0:QBLK], lse[QBLK:2 * QBLK])
                    k_prev, v_prev = k_cur, v_cur
        for c0 in range(0, CHUNK, 256):
            rs = slice(c0, c0 + 256)
            l0, l1, l2 = lb0[rs, :], lb1[rs, :], lb2[rs, :]
            mx = jnp.maximum(jnp.maximum(l0, l1), l2)
            w0, w1, w2 = jnp.exp(l0 - mx), jnp.exp(l1 - mx), jnp.exp(l2 - mx)
            tot = w0 + w1 + w2
            o_ref[rs, :] = (ob0[rs, :] * w0 + ob1[rs, :] * w1 + ob2[rs, :] * w2) / tot
            l_ref[rs, :] = mx + jnp.log(tot)

    def cur(col):
        return pl.BlockSpec((CHUNK, 128), lambda hp, n: (n, col + hp))

    def prv(col):
        return pl.BlockSpec((CHUNK, 128), lambda hp, n: (jnp.maximum(n - 1, 0), col + hp))

    out = pl.BlockSpec((CHUNK, 128), lambda hp, n: (n, hp))
    any_spec = pl.BlockSpec(memory_space=pl.ANY)
    sems = [pltpu.SemaphoreType.DMA((6 * nl,)), pltpu.SemaphoreType.DMA((6 * nl,)), pltpu.SemaphoreType.DMA((nl,))]
    res = pl.pallas_call(
        body, name="attn_fwd", grid=(N_HEADS // 2, nc),
        in_specs=[cur(0), prv(4), cur(4), prv(8), cur(8),
                  pl.BlockSpec((3, 1, 2 * QBLK, 2 * QBLK), lambda hp, n: (0, hp, 0, 0))] + [any_spec] * nl,
        out_specs=[out, out] + [any_spec] * nl,
        out_shape=[jax.ShapeDtypeStruct((t, ATTN_W), F32)] * 2
        + [jax.ShapeDtypeStruct((N_CHIPS,) + w.shape, w.dtype) for w in late],
        scratch_shapes=[pltpu.VMEM((CHUNK, 128), F32)] * 6 + (sems if nl else []),
        compiler_params=_params(48, 2),
    )(qkn, qkn, qkn, qkv, qkv, mb, *late)
    return res[0], res[1], list(res[2:])


def _attn_bwd(qkn, qkv, o, lse, do, mb, riders=()):
    t = qkn.shape[0]
    nc = t // CHUNK
    nr = len(riders)

    def body(*refs):
        (qc_ref, qn_ref, kp_ref, kc_ref, vp_ref, vc_ref, oc_ref, on_ref, lc_ref, ln_ref, dc_ref, dn_ref,
         mb_ref) = refs[0:13]
        dq_ref, dk_ref, dv_ref = refs[13 + nr:16 + nr]
        if nr:
            step = pl.program_id(0) * nc + pl.program_id(1)
            _ride_scatter(step == 0, step == (N_HEADS // 2) * nc - 1, riders, refs[13:13 + nr],
                          refs[16 + nr:16 + 2 * nr], *refs[16 + 2 * nr:18 + 2 * nr])
        first = pl.program_id(1) == 0
        last = pl.program_id(1) == nc - 1
        lane = lax.broadcasted_iota(jnp.int32, (QBLK, 128), 1)
        lo_half = lane < HEAD_DIM
        kj = lax.broadcasted_iota(jnp.int32, (2 * QBLK, 2 * QBLK), 1)
        no_prev = first & (kj < QBLK)

        def by_head(a):
            return jnp.where(lo_half, a, 0.0).astype(BF), jnp.where(lo_half, 0.0, a).astype(BF)

        def query_side(q_ref, d_ref, o_ref_, l_ref_, rows):
            dvals = d_ref[rows, :]
            dd = dvals * o_ref_[rows, :]
            lv = l_ref_[rows, :]
            d0 = jnp.sum(jnp.where(lo_half, dd, 0.0), axis=-1, keepdims=True)
            d1 = jnp.sum(jnp.where(lo_half, 0.0, dd), axis=-1, keepdims=True)
            l0 = jnp.max(jnp.where(lo_half, lv, NEG), axis=-1, keepdims=True)
            l1 = jnp.max(jnp.where(lo_half, NEG, lv), axis=-1, keepdims=True)
            return (jnp.concatenate(by_head(q_ref[rows, :]), axis=0), jnp.concatenate(by_head(dvals), axis=0),
                    jnp.concatenate([l0, l1], axis=0), jnp.concatenate([d0, d1], axis=0))

        def tile(qs, dos, lcol, dcol, keys, vals, bias, dead):
            s = _mm_nt(qs, keys) + bias
            if dead is not None:
                s = jnp.where(dead, NEG, s)
            p = jnp.exp(s - lcol)
            ds = p * (_mm_nt(dos, vals) - dcol)
            return p.astype(BF), ds.astype(BF)

        def put(ref, di, rows, val):
            if di == 0:
                ref[rows, :] = val
            else:
                ref[rows, :] = ref[rows, :] + val

        for di, d in enumerate(DILATIONS):
            span = d * QBLK
            nbk = CHUNK // span
            for r in range(d):
                tail = pl.ds(CHUNK - span + r, QBLK, stride=d)
                k_prev = kp_ref[tail, :]
                kb_prev, km_prev = k_prev.astype(BF), by_head(k_prev)
                vb_prev = vp_ref[tail, :].astype(BF)
                rows_prev, dk_part, dv_part = None, None, None
                for b in range(nbk):
                    rows = pl.ds(r + span * b, QBLK, stride=d)
                    qs, dos, lcol, dcol = query_side(qc_ref, dc_ref, oc_ref, lc_ref, rows)
                    k_cur = kc_ref[rows, :]
                    kb_cur, km_cur = k_cur.astype(BF), by_head(k_cur)
                    vb_cur = vc_ref[rows, :].astype(BF)
                    p, ds = tile(qs, dos, lcol, dcol, jnp.concatenate([kb_prev, kb_cur], axis=0),
                                 jnp.concatenate([vb_prev, vb_cur], axis=0), mb_ref[di, 0],
                                 no_prev if b == 0 else None)
                    put(dq_ref, di, rows,
                        _mm(jnp.concatenate([ds[0:QBLK], ds[QBLK:2 * QBLK]], axis=1),
                            jnp.concatenate([km_prev[0], km_cur[0], km_prev[1], km_cur[1]], axis=0)))
                    dk2 = _mm_tn(ds, qs)
                    dv2 = _mm_tn(p, dos)
                    if b > 0:
                        put(dk_ref, di, rows_prev, dk_part + dk2[0:QBLK])
                        put(dv_ref, di, rows_prev, dv_part + dv2[0:QBLK])
                    rows_prev, dk_part, dv_part = rows, dk2[QBLK:2 * QBLK], dv2[QBLK:2 * QBLK]
                    kb_prev, km_prev, vb_prev = kb_cur, km_cur, vb_cur
                qs, dos, lcol, dcol = query_side(qn_ref, dn_ref, on_ref, ln_ref, pl.ds(r, QBLK, stride=d))
                p, ds = tile(qs, dos, lcol, dcol, kb_prev, vb_prev, mb_ref[di, 0, :, 0:QBLK], last)
                put(dk_ref, di, rows_prev, dk_part + _mm_tn(ds, qs))
                put(dv_ref, di, rows_prev, dv_part + _mm_tn(p, dos))

    def at(shift, col):
        return pl.BlockSpec((CHUNK, 128), lambda hp, n: (jnp.clip(n + shift, 0, nc - 1), col + hp))

    out = pl.BlockSpec((CHUNK, 128), lambda hp, n: (n, hp))
    r_in, r_out, r_sems = _rider_specs(riders)
    res = pl.pallas_call(
        body, name="attn_bwd", grid=(N_HEADS // 2, nc),
        in_specs=[at(0, 0), at(1, 0), at(-1, 4), at(0, 4), at(-1, 8), at(0, 8),
                  at(0, 0), at(1, 0), at(0, 0), at(1, 0), at(0, 0), at(1, 0),
                  pl.BlockSpec((3, 1, 2 * QBLK, 2 * QBLK), lambda hp, n: (0, hp, 0, 0))] + r_in,
        out_specs=[out, out, out] + r_in,
        out_shape=[jax.ShapeDtypeStruct((t, ATTN_W), F32)] * 3 + r_out,
        scratch_shapes=r_sems,
        compiler_params=_params(56, 2),
    )(qkn, qkn, qkn, qkn, qkv, qkv, o, o, lse, lse, do, do, mb, *[g for g, _ in riders])
    return res[0], res[1], res[2], list(res[3:])


def _fwd_ffn(x, ycn, ya, wout, wg4, wu4, g_oa, g_ffn, fcw, fcb, tm, late=()):
    t = x.shape[0]
    nt = t // tm
    nl = len(late)

    def body(*refs):
        x_ref, ycn_ref, ya_ref, wout_ref, wg_ref, wu_ref, goa_ref, gffn_ref, fcw_ref, fcb_ref = refs[0:10]
        x1_ref, gp_ref, up_ref, act_ref, ycat_ref, h2_ref = refs[10 + nl:16 + nl]
        cbuf = refs[16 + 2 * nl]
        if nl:
            gather = _Gather(refs[10:10 + nl], refs[16 + nl:16 + 2 * nl], *refs[17 + 2 * nl:20 + 2 * nl])
            pl.when(pl.program_id(0) == 0)(gather.start)
            pl.when(pl.program_id(0) == nt // 2)(gather.forward)
            pl.when(pl.program_id(0) == nt - 1)(gather.finish)

        @pl.when(pl.program_id(0) == 0)
        def _():
            cbuf[0:8, :] = jnp.zeros((8, D_FF), F32)

        yat = ya_ref[...]
        yan = ((yat * _rstd(yat)) * goa_ref[...]).astype(BF)
        ycn = ycn_ref[...]
        ycat_ref[:, 0:CONV_W] = ycn
        ycat_ref[:, CONV_W:D_MODEL] = yan
        x1 = x_ref[...] + _mm(ycn, wout_ref[0:CONV_W, :]) + _mm(yan, wout_ref[CONV_W:D_MODEL, :])
        x1_ref[...] = x1
        h2 = ((x1 * _rstd(x1)) * gffn_ref[...]).astype(BF)
        h2_ref[...] = h2
        for lo, hi in FF_SLABS:
            gps = _mm_nt(h2, wg_ref[lo:hi, :])
            ups = _mm_nt(h2, wu_ref[lo:hi, :])
            gp_ref[:, lo:hi] = gps
            up_ref[:, lo:hi] = ups
            cbuf[8:8 + tm, lo:hi] = gps
            cb = cbuf[:, lo:hi]
            gate = (fcw_ref[0:1, lo:hi] * _shift_down(cb, 2, tm) + fcw_ref[1:2, lo:hi] * _shift_down(cb, 1, tm)
                    + fcw_ref[2:3, lo:hi] * gps + fcb_ref[:, lo:hi])
            act_ref[:, lo:hi] = ((gate * jax.nn.sigmoid(gate)) * ups).astype(BF)
        cbuf[0:8, :] = cbuf[tm:tm + 8, :]

    any_spec = pl.BlockSpec(memory_space=pl.ANY)
    sems = [pltpu.SemaphoreType.DMA((6 * nl,)), pltpu.SemaphoreType.DMA((6 * nl,)), pltpu.SemaphoreType.DMA((nl,))]
    res = pl.pallas_call(
        body, name="fwd_ffn", grid=(nt,),
        in_specs=[_rows(tm, D_MODEL), _rows(tm, CONV_W), _rows(tm, ATTN_W), _const((D_MODEL, D_MODEL)),
                  _const((D_FF, D_MODEL)), _const((D_FF, D_MODEL)),
                  _const((1, ATTN_W)), _const((1, D_MODEL)), _const((3, D_FF)), _const((1, D_FF))]
        + [any_spec] * nl,
        out_specs=[_rows(tm, D_MODEL), _rows(tm, D_FF), _rows(tm, D_FF), _rows(tm, D_FF),
                   _rows(tm, D_MODEL), _rows(tm, D_MODEL)] + [any_spec] * nl,
        out_shape=[jax.ShapeDtypeStruct((t, D_MODEL), F32), jax.ShapeDtypeStruct((t, D_FF), F32),
                   jax.ShapeDtypeStruct((t, D_FF), F32), jax.ShapeDtypeStruct((t, D_FF), BF),
                   jax.ShapeDtypeStruct((t, D_MODEL), BF), jax.ShapeDtypeStruct((t, D_MODEL), BF)]
        + [jax.ShapeDtypeStruct((N_CHIPS,) + w.shape, w.dtype) for w in late],
        scratch_shapes=[pltpu.VMEM((tm + 8, D_FF), F32)] + (sems if nl else []),
        compiler_params=_params(56),
    )(x, ycn, ya, wout, wg4, wu4, g_oa, g_ffn, fcw, fcb, *late)
    return tuple(res[0:6]) + (list(res[6:]),)


def _fwd_tail(x1, act, p, target, wd4, wpg, wpp4, g_ple, tm):
    t = x1.shape[0]
    nt = t // tm

    def body(x1_ref, act_ref, p_ref, tgt_ref, wd_ref, wpg_ref, wpp_ref, g_ref,
             dx2_ref, h3_ref, ds_ref, dpp_ref, dg_ref, loss_ref, lacc):
        i = pl.program_id(0)

        @pl.when(i == 0)
        def _():
            dg_ref[...] = jnp.zeros_like(dg_ref)
            lacc[...] = jnp.zeros_like(lacc)

        x2 = x1_ref[...]
        for lo, hi in FF_SLABS:
            x2 = x2 + _mm(act_ref[:, lo:hi], wd_ref[lo:hi, :])
        r3 = _rstd(x2)
        xh = x2 * r3
        h3 = (xh * g_ref[...]).astype(BF)
        h3_ref[...] = h3
        sg = jax.nn.sigmoid(_mm(h3, wpg_ref[...]))
        pb = p_ref[...].astype(BF)
        pp = jnp.concatenate([_mm(pb, wpp_ref[s]) for s in range(N_CHIPS)], axis=1)
        err = (x2 + sg * pp) - tgt_ref[...]
        lacc[...] += _colsum(err * err)
        dx3 = err * (1.0 / D_MODEL)
        dpp_ref[...] = (dx3 * sg).astype(BF)
        dsb = ((dx3 * pp) * (sg * (1.0 - sg))).astype(BF)
        ds_ref[...] = dsb
        dh3 = _mm_nt(dsb, wpg_ref[...])
        dg_ref[...] += _colsum(dh3 * xh)
        dx2_ref[...] = dx3 + _norm_bwd(dh3, xh, r3, g_ref[...])

        @pl.when(i == nt - 1)
        def _():
            loss_ref[...] = jnp.full((1, 128), jnp.sum(lacc[...]) * (0.5 / D_MODEL), F32)

    return pl.pallas_call(
        body, name="fwd_tail", grid=(nt,),
        in_specs=[_rows(tm, D_MODEL), _rows(tm, D_FF), _rows(tm, PLE_DIM), _rows(tm, D_MODEL),
                  _const((D_FF, D_MODEL)), _const((D_MODEL, D_MODEL)),
                  _const((N_CHIPS, PLE_DIM, PLE_DIM)), _const((1, D_MODEL))],
        out_specs=[_rows(tm, D_MODEL), _rows(tm, D_MODEL), _rows(tm, D_MODEL), _rows(tm, D_MODEL),
                   pl.BlockSpec((1, D_MODEL), lambda i: (0, 0)), pl.BlockSpec((1, 128), lambda i: (0, 0))],
        out_shape=[jax.ShapeDtypeStruct((t, D_MODEL), F32), jax.ShapeDtypeStruct((t, D_MODEL), BF),
                   jax.ShapeDtypeStruct((t, D_MODEL), BF), jax.ShapeDtypeStruct((t, D_MODEL), BF),
                   jax.ShapeDtypeStruct((1, D_MODEL), F32), jax.ShapeDtypeStruct((1, 128), F32)],
        scratch_shapes=[pltpu.VMEM((1, D_MODEL), F32)],
        compiler_params=_params(48),
    )(x1, act, p, target, wd4, wpg, wpp4, g_ple)


def _bwd_ffn_a(dx2, gp, up, wd4, fcw, fcb, tm, riders=()):
    t = dx2.shape[0]
    nt = t // tm
    nr = len(riders)

    def body(*refs):
        dx2_ref, gp_ref, gph_ref, up_ref, wd_ref, fcw_ref, fcb_ref = refs[0:7]
        dgp_ref, dup_ref, dfcw_ref, dfcb_ref = refs[7 + nr:11 + nr]
        cbuf, dbuf = refs[11 + 2 * nr:13 + 2 * nr]
        i = pl.program_id(0)
        if nr:
            _ride_scatter(i == 0, i == nt - 1, riders, refs[7:7 + nr], refs[11 + nr:11 + 2 * nr],
                          *refs[13 + 2 * nr:15 + 2 * nr])

        @pl.when(i == 0)
        def _():
            dbuf[tm:tm + 8, :] = jnp.zeros((8, D_FF), F32)
            dfcw_ref[...] = jnp.zeros_like(dfcw_ref)
            dfcb_ref[...] = jnp.zeros_like(dfcb_ref)

        not_first_tile = i < nt - 1
        dx2b = dx2_ref[...].astype(BF)
        for lo, hi in FF_SLABS:
            gps = gp_ref[:, lo:hi]
            cbuf[0:8, lo:hi] = jnp.where(not_first_tile, gph_ref[:, lo:hi], 0.0)
            cbuf[8:8 + tm, lo:hi] = gps
            cb = cbuf[:, lo:hi]
            g1 = _shift_down(cb, 1, tm)
            g2 = _shift_down(cb, 2, tm)
            w0, w1, w2 = fcw_ref[0:1, lo:hi], fcw_ref[1:2, lo:hi], fcw_ref[2:3, lo:hi]
            gate = w0 * g2 + w1 * g1 + w2 * gps + fcb_ref[:, lo:hi]
            sg = jax.nn.sigmoid(gate)
            dact = _mm_nt(dx2b, wd_ref[lo:hi, :])
            dup_ref[:, lo:hi] = (dact * (gate * sg)).astype(BF)
            dgate = (dact * up_ref[:, lo:hi]) * (sg * (1.0 + gate * (1.0 - sg)))
            dfcb_ref[:, lo:hi] += _colsum(dgate)
            dfcw_ref[0:1, lo:hi] += _colsum(dgate * g2)
            dfcw_ref[1:2, lo:hi] += _colsum(dgate * g1)
            dfcw_ref[2:3, lo:hi] += _colsum(dgate * gps)
            dbuf[0:tm, lo:hi] = dgate
            db = dbuf[:, lo:hi]
            dgp = w2 * dgate + w1 * _shift_up(db, 1, tm) + w0 * _shift_up(db, 2, tm)
            dgp_ref[:, lo:hi] = dgp.astype(BF)
        dbuf[tm:tm + 8, :] = dbuf[0:8, :]

    r_in, r_out, r_sems = _rider_specs(riders)
    res = pl.pallas_call(
        body, name="bwd_ffn_a", grid=(nt,),
        in_specs=[_rows(tm, D_MODEL, nt), _rows(tm, D_FF, nt), _halo(tm, D_FF, nt), _rows(tm, D_FF, nt),
                  _const((D_FF, D_MODEL)), _const((3, D_FF)), _const((1, D_FF))] + r_in,
        out_specs=[_rows(tm, D_FF, nt), _rows(tm, D_FF, nt),
                   pl.BlockSpec((3, D_FF), lambda i: (0, 0)), pl.BlockSpec((1, D_FF), lambda i: (0, 0))] + r_in,
        out_shape=[jax.ShapeDtypeStruct((t, D_FF), BF), jax.ShapeDtypeStruct((t, D_FF), BF),
                   jax.ShapeDtypeStruct((3, D_FF), F32), jax.ShapeDtypeStruct((1, D_FF), F32)] + r_out,
        scratch_shapes=[pltpu.VMEM((tm + 8, D_FF), F32), pltpu.VMEM((tm + 8, D_FF), F32)] + r_sems,
        compiler_params=_params(56),
    )(dx2, gp, gp, up, wd4, fcw, fcb, *[g for g, _ in riders])
    return res[0], res[1], res[2], res[3], list(res[4:])


def _bwd_ffn_b(dgp, dup, dx2, x1, ya, wg4, wu4, wout, g_ffn, g_oa, tm):
    t = dx2.shape[0]
    nt = t // tm

    def body(dgp_ref, dup_ref, dx2_ref, x1_ref, ya_ref, wg_ref, wu_ref, wout_ref, gffn_ref, goa_ref,
             dx1_ref, dycn_ref, dya_ref, dgffn_ref, dgoa_ref):
        @pl.when(pl.program_id(0) == 0)
        def _():
            dgffn_ref[...] = jnp.zeros_like(dgffn_ref)
            dgoa_ref[...] = jnp.zeros_like(dgoa_ref)

        dh2 = jnp.zeros((tm, D_MODEL), F32)
        for lo, hi in FF_SLABS:
            dh2 = dh2 + _mm(dgp_ref[:, lo:hi], wg_ref[lo:hi, :]) + _mm(dup_ref[:, lo:hi], wu_ref[lo:hi, :])
        x1 = x1_ref[...]
        r2 = _rstd(x1)
        xh = x1 * r2
        dgffn_ref[...] += _colsum(dh2 * xh)
        dx1 = dx2_ref[...] + _norm_bwd(dh2, xh, r2, gffn_ref[...])
        dx1_ref[...] = dx1
        dy = _mm_nt(dx1.astype(BF), wout_ref[...])
        dycn_ref[...] = dy[:, 0:CONV_W]
        dyan = dy[:, CONV_W:D_MODEL]
        yat = ya_ref[...]
        ra = _rstd(yat)
        yah = yat * ra
        dgoa_ref[...] += _colsum(dyan * yah)
        dya_ref[...] = _norm_bwd(dyan, yah, ra, goa_ref[...])

    return pl.pallas_call(
        body, name="bwd_ffn_b", grid=(nt,),
        in_specs=[_rows(tm, D_FF), _rows(tm, D_FF), _rows(tm, D_MODEL), _rows(tm, D_MODEL),
                  _rows(tm, ATTN_W), _const((D_FF, D_MODEL)), _const((D_FF, D_MODEL)),
                  _const((D_MODEL, D_MODEL)), _const((1, D_MODEL)), _const((1, ATTN_W))],
        out_specs=[_rows(tm, D_MODEL), _rows(tm, CONV_W), _rows(tm, ATTN_W),
                   pl.BlockSpec((1, D_MODEL), lambda i: (0, 0)), pl.BlockSpec((1, ATTN_W), lambda i: (0, 0))],
        out_shape=[jax.ShapeDtypeStruct((t, D_MODEL), F32), jax.ShapeDtypeStruct((t, CONV_W), F32),
                   jax.ShapeDtypeStruct((t, ATTN_W), F32),
                   jax.ShapeDtypeStruct((1, D_MODEL), F32), jax.ShapeDtypeStruct((1, ATTN_W), F32)],
        compiler_params=_params(48),
    )(dgp, dup, dx2, x1, ya, wg4, wu4, wout, g_ffn, g_oa)


def _bwd_mix(x, dx1, zbcx, qkv, dycn, dq, dk, dv, win4, conv_w, conv_b, g_oc, g_mix, gm, gq8, gk8, tm):
    t = x.shape[0]
    nt = t // tm

    def body(x_ref, dx1_ref, z_ref, zh_ref, qkv_ref, dycn_ref, dq_ref, dk_ref, dv_ref, w_ref, cw_ref, cb_ref,
             goc_ref, g_ref, gm_ref, gq_ref, gk_ref,
             gx_ref, h1_ref, dz_ref, dcw_ref, dcb_ref, dgoc_ref, dg_ref, dgq_ref, dgk_ref, ubuf, dbuf):
        i = pl.program_id(0)

        @pl.when(i == 0)
        def _():
            dbuf[tm:tm + 8, :] = jnp.zeros((8, CONV_W), F32)
            dcw_ref[...] = jnp.zeros_like(dcw_ref)
            dcb_ref[...] = jnp.zeros_like(dcb_ref)
            dgoc_ref[...] = jnp.zeros_like(dgoc_ref)
            dg_ref[...] = jnp.zeros_like(dg_ref)
            dgq_ref[...] = jnp.zeros_like(dgq_ref)
            dgk_ref[...] = jnp.zeros_like(dgk_ref)

        not_first_tile = i < nt - 1
        zb = z_ref[:, 0:512]
        zc = z_ref[:, 512:1024]
        zx = z_ref[:, 1024:1536]
        u = zc * zx
        ubuf[0:8, :] = jnp.where(not_first_tile, zh_ref[:, 512:1024] * zh_ref[:, 1024:1536], 0.0)
        ubuf[8:8 + tm, :] = u
        ub = ubuf[...]
        u1 = _shift_down(ub, 1, tm)
        u2 = _shift_down(ub, 2, tm)
        w0, w1, w2 = cw_ref[0:1, :], cw_ref[1:2, :], cw_ref[2:3, :]
        cv = w0 * u2 + w1 * u1 + w2 * u + cb_ref[...]
        yc = zb * cv
        rc = _rstd(yc)
        ych = yc * rc
        dycn = dycn_ref[...]
        dgoc_ref[...] += _colsum(dycn * ych)
        dyc = _norm_bwd(dycn, ych, rc, goc_ref[...])
        dcv = dyc * zb
        dcb_ref[...] += _colsum(dcv)
        dcw_ref[0:1, :] += _colsum(dcv * u2)
        dcw_ref[1:2, :] += _colsum(dcv * u1)
        dcw_ref[2:3, :] += _colsum(dcv * u)
        dbuf[0:tm, :] = dcv
        db = dbuf[...]
        du = w2 * dcv + w1 * _shift_up(db, 1, tm) + w0 * _shift_up(db, 2, tm)
        dbuf[tm:tm + 8, :] = dbuf[0:8, :]
        dz_ref[:, 0:512] = (dyc * cv).astype(BF)
        dz_ref[:, 512:1024] = (du * zx).astype(BF)
        dz_ref[:, 1024:1536] = (du * zc).astype(BF)
        for z0, d_ref, gg_ref, acc_ref, sc in ((0, dq_ref, gq_ref, dgq_ref, HEAD_DIM ** -0.5),
                                               (512, dk_ref, gk_ref, dgk_ref, 1.0)):
            z = qkv_ref[:, z0:z0 + 512]
            rr = lax.rsqrt(_head_mean(z * z, gm_ref) + EPS)
            zh = z * rr
            dn = d_ref[...] * sc
            acc_ref[...] += _colsum(dn * zh)
            dzh = dn * gg_ref[...]
            dz_ref[:, 1536 + z0:1536 + z0 + 512] = (rr * (dzh - zh * _head_mean(dzh * zh, gm_ref))).astype(BF)
        dz_ref[:, 2560:3072] = dv_ref[...].astype(BF)
        dh1 = jnp.zeros((tm, D_MODEL), F32)
        for s in range(N_CHIPS):
            dh1 = dh1 + _mm_nt(dz_ref[:, s * IN_SLAB:(s + 1) * IN_SLAB], w_ref[s])
        xt = x_ref[...]
        r1 = _rstd(xt)
        xh = xt * r1
        h1_ref[...] = (xh * g_ref[...]).astype(BF)
        dg_ref[...] += _colsum(dh1 * xh)
        gx_ref[...] = dx1_ref[...] + _norm_bwd(dh1, xh, r1, g_ref[...])

    def acc(width, rows=1):
        return pl.BlockSpec((rows, width), lambda i: (0, 0))

    return pl.pallas_call(
        body, name="bwd_mix", grid=(nt,),
        in_specs=[_rows(tm, D_MODEL, nt), _rows(tm, D_MODEL, nt), _rows(tm, 1536, nt), _halo(tm, 1536, nt),
                  _rows(tm, 1536, nt), _rows(tm, CONV_W, nt), _rows(tm, ATTN_W, nt), _rows(tm, ATTN_W, nt),
                  _rows(tm, ATTN_W, nt), _const((N_CHIPS, D_MODEL, IN_SLAB)),
                  _const((3, CONV_W)), _const((1, CONV_W)), _const((1, CONV_W)), _const((1, D_MODEL)),
                  _const((ATTN_W, ATTN_W)), _const((1, ATTN_W)), _const((1, ATTN_W))],
        out_specs=[_rows(tm, D_MODEL, nt), _rows(tm, D_MODEL, nt), _rows(tm, 3072, nt),
                   acc(CONV_W, 3), acc(CONV_W), acc(CONV_W), acc(D_MODEL), acc(ATTN_W), acc(ATTN_W)],
        out_shape=[jax.ShapeDtypeStruct((t, D_MODEL), F32), jax.ShapeDtypeStruct((t, D_MODEL), BF),
                   jax.ShapeDtypeStruct((t, 3072), BF), jax.ShapeDtypeStruct((3, CONV_W), F32),
                   jax.ShapeDtypeStruct((1, CONV_W), F32), jax.ShapeDtypeStruct((1, CONV_W), F32),
                   jax.ShapeDtypeStruct((1, D_MODEL), F32), jax.ShapeDtypeStruct((1, ATTN_W), F32),
                   jax.ShapeDtypeStruct((1, ATTN_W), F32)],
        scratch_shapes=[pltpu.VMEM((tm + 8, CONV_W), F32), pltpu.VMEM((tm + 8, CONV_W), F32)],
        compiler_params=_params(56),
    )(x, dx1, zbcx, zbcx, qkv, dycn, dq, dk, dv, win4, conv_w, conv_b, g_oc, g_mix, gm, gq8, gk8)


def _wgrad(a, b, tn, tt, name):
    t, k = a.shape
    n = b.shape[1]
    nt = t // tt

    def body(a_ref, b_ref, o_ref, ob_ref):
        @pl.when(pl.program_id(1) == 0)
        def _():
            o_ref[...] = jnp.zeros_like(o_ref)

        o_ref[...] += _mm_tn(a_ref[...].astype(BF), b_ref[...].astype(BF))

        @pl.when(pl.program_id(1) == nt - 1)
        def _():
            ob_ref[...] = o_ref[...].astype(BF)

    spec = pl.BlockSpec((k, tn), lambda j, i: (0, j))
    return pl.pallas_call(
        body, name=name, grid=(n // tn, nt),
        in_specs=[pl.BlockSpec((tt, k), lambda j, i: (i, 0)), pl.BlockSpec((tt, tn), lambda j, i: (i, j))],
        out_specs=[spec, spec],
        out_shape=[jax.ShapeDtypeStruct((k, n), F32), jax.ShapeDtypeStruct((k, n), BF)],
        compiler_params=_params(48, 2),
    )(a, b)


def _gather_weights(shards, pack):
    nw = len(shards)

    def body(*refs):
        ins = refs[:nw]
        pack_ref = refs[nw]
        outs = refs[nw + 1:2 * nw + 1]
        pack_out = refs[2 * nw + 1]
        send_sems, recv_sems, local_sems = refs[2 * nw + 2:]
        x, y, c = _place()
        me = 2 * x + y
        local, remote = [], []

        def sem(w, j):
            return w * 6 + j

        def push(src, dst, w, j, to):
            return pltpu.make_async_remote_copy(src_ref=src, dst_ref=dst, send_sem=send_sems.at[sem(w, j)],
                                                recv_sem=recv_sems.at[sem(w, j)], device_id=to, device_id_type=MESH)

        def half_rows(w, h):
            half = ins[w].shape[0] // 2
            return pl.ds(pl.multiple_of(h * half, 16), half)

        for w in range(nw):
            local.append(pltpu.make_async_copy(ins[w], outs[w].at[me], local_sems.at[w]))
            for k in (1, 2, 3):
                px, py = _chip_peer(x, y, k)
                mine = half_rows(w, c)
                remote.append(push(ins[w].at[mine], outs[w].at[me, mine], w, k - 1, (px, py, c)))
        local.append(pltpu.make_async_copy(pack_ref, pack_out.at[me], local_sems.at[nw]))
        for k in (1, 2, 3):
            px, py = _chip_peer(x, y, k)
            remote.append(push(pack_ref, pack_out.at[me], nw, k - 1, (px, py, c)))
        for cp in local + remote:
            cp.start()
        for w in range(nw):
            for k in (1, 2, 3):
                landed = outs[w].at[me ^ k, half_rows(w, c)]
                push(landed, landed, w, k - 1, (x, y, c)).wait_recv()
                fw = push(landed, landed, w, 2 + k, (x, y, 1 - c))
                fw.start()
                remote.append(fw)
        for k in (1, 2, 3):
            landed = pack_out.at[me ^ k]
            push(landed, landed, nw, k - 1, (x, y, c)).wait_recv()
        for w in range(nw):
            for k in (1, 2, 3):
                landed = outs[w].at[me ^ k, half_rows(w, 1 - c)]
                push(landed, landed, w, 2 + k, (x, y, c)).wait_recv()
        for cp in remote:
            cp.wait_send()
        for cp in local:
            cp.wait()

    any_spec = pl.BlockSpec(memory_space=pl.ANY)
    out_shape = [jax.ShapeDtypeStruct((N_CHIPS,) + s.shape, s.dtype) for s in shards]
    out_shape.append(jax.ShapeDtypeStruct((N_CHIPS,) + pack.shape, pack.dtype))
    return pl.pallas_call(
        body, name="gather_weights",
        in_specs=[any_spec] * (nw + 1), out_specs=[any_spec] * (nw + 1), out_shape=out_shape,
        scratch_shapes=[pltpu.SemaphoreType.DMA(((nw + 1) * 6,)), pltpu.SemaphoreType.DMA(((nw + 1) * 6,)),
                        pltpu.SemaphoreType.DMA((nw + 1,))],
    )(*shards, pack)


def _adamw(w, g, m, v):
    m = ADAM_B1 * m + (1.0 - ADAM_B1) * g
    v = ADAM_B2 * v + (1.0 - ADAM_B2) * (g * g)
    m_hat = m / (1.0 - ADAM_B1 ** ADAM_STEP)
    v_hat = v / (1.0 - ADAM_B2 ** ADAM_STEP)
    delta = -ADAM_LR * (m_hat / (jnp.sqrt(v_hat) + ADAM_EPS) + ADAM_WD * w)
    return delta, m, v


def _scatter_alone(g16, col_sharded, name):
    def body(g_ref, slots_ref, send_sems, recv_sems):
        copies = _scatter_copies(g_ref, slots_ref, send_sems, recv_sems, 0, col_sharded)
        for cp in copies:
            cp.start()
        for cp in copies:
            cp.wait()

    any_spec = pl.BlockSpec(memory_space=pl.ANY)
    return pl.pallas_call(
        body, name=name, in_specs=[any_spec], out_specs=any_spec,
        out_shape=jax.ShapeDtypeStruct((7,) + _piece_shape(g16.shape, col_sharded), BF),
        scratch_shapes=[pltpu.SemaphoreType.DMA((7,)), pltpu.SemaphoreType.DMA((7,))],
    )(g16)


def _finish_reduce(grad, slots, col_sharded, name):
    r, cw = _piece_shape(grad.shape, col_sharded)
    chunk = 32
    assert r % chunk == 0

    def body(g_hbm, slots_ref, full, own, lsem, c_send, c_recv):
        x, y, c = _place()
        cp = pltpu.make_async_copy(g_hbm.at[_piece_window(col_sharded, r, cw, 2 * x + y, c)], own, lsem)
        cp.start()
        cp.wait()
        mine = pl.multiple_of(c * r, 8)

        def add(j, carry):
            rows = pl.ds(pl.multiple_of(j * chunk, 8), chunk)
            tot = own[rows, :]
            for k in range(7):
                tot = tot + slots_ref[k, rows, :].astype(F32)
            full[pl.ds(mine + pl.multiple_of(j * chunk, 8), chunk), :] = tot
            return carry

        lax.fori_loop(0, r // chunk, add, 0)
        half = full.at[pl.ds(mine, r), :]
        swap = pltpu.make_async_remote_copy(src_ref=half, dst_ref=half, send_sem=c_send, recv_sem=c_recv,
                                            device_id=(x, y, 1 - c), device_id_type=MESH)
        swap.start()
        swap.wait()

    vmem = pl.BlockSpec(memory_space=pltpu.VMEM)
    return pl.pallas_call(
        body, name=name, in_specs=[pl.BlockSpec(memory_space=pl.ANY), vmem], out_specs=vmem,
        out_shape=jax.ShapeDtypeStruct((2 * r, cw), F32),
        scratch_shapes=[pltpu.VMEM((r, cw), F32), pltpu.SemaphoreType.DMA, pltpu.SemaphoreType.DMA,
                        pltpu.SemaphoreType.DMA],
        compiler_params=pltpu.CompilerParams(vmem_limit_bytes=32 * MIB),
    )(grad, slots)


def _adamw_big(g, w, m, v, name):
    vr, vc = w.shape
    assert g.shape == w.shape
    rows = 64

    def body(g_ref, w_ref, m_ref, v_ref, go_ref, do_ref, mo_ref, vo_ref):
        gg = g_ref[...]
        delta, mn, vn = _adamw(w_ref[...], gg, m_ref[...], v_ref[...])
        go_ref[...] = gg
        do_ref[...] = delta
        mo_ref[...] = mn
        vo_ref[...] = vn

    blk = pl.BlockSpec((rows, vc), lambda i: (i, 0))
    shard = jax.ShapeDtypeStruct((vr, vc), F32)
    return pl.pallas_call(
        body, name=name, grid=(vr // rows,),
        in_specs=[blk, blk, blk, blk], out_specs=[blk] * 4,
        out_shape=[shard] * 4, compiler_params=_params(32),
    )(g, w, m, v)


def _allreduce_small(pack):
    rows = pack.shape[0]

    def body(p_ref, o_ref, slots, send_sems, recv_sems):
        x, y, c = _place()
        me = 4 * x + 2 * y + c
        slots[me] = p_ref[...]
        sends = []
        for k in range(1, 8):
            cp = pltpu.make_async_remote_copy(
                src_ref=p_ref, dst_ref=slots.at[me], send_sem=send_sems.at[k - 1], recv_sem=recv_sems.at[k - 1],
                device_id=(x ^ (k >> 2), y ^ ((k >> 1) & 1), c ^ (k & 1)), device_id_type=MESH)
            cp.start()
            sends.append(cp)
        for cp in sends:
            cp.wait()
        tot = slots[0]
        for j in range(1, 8):
            tot = tot + slots[j]
        o_ref[...] = tot

    vmem = pl.BlockSpec(memory_space=pltpu.VMEM)
    return pl.pallas_call(
        body, name="allreduce_small", in_specs=[vmem], out_specs=vmem,
        out_shape=jax.ShapeDtypeStruct(pack.shape, F32),
        scratch_shapes=[pltpu.VMEM((8, rows, D_MODEL), F32), pltpu.SemaphoreType.DMA((7,)),
                        pltpu.SemaphoreType.DMA((7,))],
    )(pack)


def _adamw_small(ws, gs, ms, vs):
    n = len(ws)

    def body(*refs):
        w_refs, g_refs, m_refs, v_refs = refs[0:n], refs[n:2 * n], refs[2 * n:3 * n], refs[3 * n:4 * n]
        d_refs, mo_refs, vo_refs = refs[4 * n:5 * n], refs[5 * n:6 * n], refs[6 * n:7 * n]
        for j in range(n):
            delta, mn, vn = _adamw(w_refs[j][...], g_refs[j][...], m_refs[j][...], v_refs[j][...])
            d_refs[j][...] = delta
            mo_refs[j][...] = mn
            vo_refs[j][...] = vn

    vmem = pl.BlockSpec(memory_space=pltpu.VMEM)
    shapes = [jax.ShapeDtypeStruct(w.shape, F32) for w in ws]
    outs = pl.pallas_call(
        body, name="adamw_small", in_specs=[vmem] * (4 * n), out_specs=[vmem] * (3 * n), out_shape=shapes * 3,
    )(*ws, *gs, *ms, *vs)
    return outs[0:n], outs[n:2 * n], outs[2 * n:3 * n]


def _local_step(x, p, target, wts, late=None):
    (win4, wout, wg4, wu4, wd4, wpg, wpp4, conv_w, fcw, g_mix, conv_b, gq, gk, g_oc, g_oa, g_ffn, fcb, g_ple) = wts
    comm = late is not None
    gm = jnp.kron(jnp.eye(N_HEADS, dtype=F32), jnp.full((HEAD_DIM, HEAD_DIM), 1.0 / HEAD_DIM, F32)).astype(BF)
    gq8, gk8 = jnp.tile(gq, (1, N_HEADS)), jnp.tile(gk, (1, N_HEADS))
    mb = _mask_table()
    zbcx, qkv, ycn, qkn = _fwd_mix(x, g_mix, win4, conv_w, conv_b, g_oc, gm, gq8, gk8, 512)
    ya, lse, gathered = _attn_fwd(qkn, qkv, mb, late[0:3] if comm else ())
    if comm:
        wout, wg4, wu4 = (g.reshape(-1, D_MODEL) for g in gathered)
    x1, gp, up, act, ycat, h2, gathered = _fwd_ffn(x, ycn, ya, wout, wg4, wu4, g_oa, g_ffn, fcw, fcb, 256,
                                                    late[3:6] if comm else ())
    if comm:
        wd4, wpg, wpp4 = gathered
        wd4, wpg = wd4.reshape(D_FF, D_MODEL), wpg.reshape(D_MODEL, D_MODEL)
    dx2, h3, ds, dpp, dg_ple, loss = _fwd_tail(x1, act, p, target, wd4, wpg, wpp4, g_ple, 512)
    big, big16, slots = {}, {}, {}

    def wgrad(name, a, b, tn):
        big[name], big16[name] = _wgrad(a, b, tn, 1024, "wgrad_" + name)
        return (big16[name], _COL_SHARDED[name])

    riders = [wgrad("w_down", act, dx2, 512), wgrad("w_ple_gate", h3, ds, 1024), wgrad("w_ple_proj", p, dpp, 1024)]
    dgp, dup, dfcw, dfcb, got = _bwd_ffn_a(dx2, gp, up, wd4, fcw, fcb, 256, riders if comm else ())
    slots.update(zip(("w_down", "w_ple_gate", "w_ple_proj"), got))
    riders = [wgrad("w_gate", dgp, h2, 512), wgrad("w_up", dup, h2, 512)]
    dx1, dycn, dya, dg_ffn, dg_oa = _bwd_ffn_b(dgp, dup, dx2, x1, ya, wg4, wu4, wout, g_ffn, g_oa, 512)
    riders.append(wgrad("w_out", ycat, dx1, 1024))
    dq, dk, dv, got = _attn_bwd(qkn, qkv, ya, lse, dya, mb, riders if comm else ())
    slots.update(zip(("w_gate", "w_up", "w_out"), got))
    grad_x, h1, dz, dcw, dcb, dg_oc, dg_mix, dgq8, dgk8 = _bwd_mix(
        x, dx1, zbcx, qkv, dycn, dq, dk, dv, win4, conv_w, conv_b, g_oc, g_mix, gm, gq8, gk8, 512)
    g16, cs = wgrad("w_in", h1, dz, 1536)
    if comm:
        slots["w_in"] = _scatter_alone(g16, cs, "scatter_w_in")
    dgq = dgq8.reshape(N_HEADS, HEAD_DIM).sum(axis=0, keepdims=True)
    dgk = dgk8.reshape(N_HEADS, HEAD_DIM).sum(axis=0, keepdims=True)
    small = dict(g_mix=dg_mix, conv_w=dcw, conv_b=dcb, q_norm_g=dgq, k_norm_g=dgk, g_out_conv=dg_oc,
                 g_out_attn=dg_oa, g_ffn=dg_ffn, ffn_conv_w=dfcw, ffn_conv_b=dfcb, g_ple=dg_ple)
    return loss[0, 0], grad_x, big, slots, small


_SMALL_ROWS = 24


def _pack_small(s):
    z64 = jnp.zeros((1, 1024 - 512 - 128), F32)
    rows = [s["g_mix"], s["g_ffn"], s["g_ple"],
            jnp.concatenate([s["conv_b"], s["g_out_conv"]], axis=1),
            jnp.concatenate([s["g_out_attn"], s["q_norm_g"], s["k_norm_g"], z64], axis=1),
            jnp.pad(s["conv_w"], ((0, 0), (0, 512))),
            jnp.pad(s["ffn_conv_b"], ((0, 0), (0, 3072 - D_FF))).reshape(3, 1024),
            jnp.pad(s["ffn_conv_w"], ((0, 0), (0, 3072 - D_FF))).reshape(9, 1024),
            jnp.zeros((_SMALL_ROWS - 20, 1024), F32)]
    return jnp.concatenate(rows, axis=0)


def _unpack_small(t):
    return dict(g_mix=t[0:1], g_ffn=t[1:2], g_ple=t[2:3], conv_b=t[3:4, 0:512], g_out_conv=t[3:4, 512:1024],
                g_out_attn=t[4:5, 0:512], q_norm_g=t[4:5, 512:576], k_norm_g=t[4:5, 576:640],
                conv_w=t[5:8, 0:512], ffn_conv_b=t[8:11].reshape(1, 3072)[:, :D_FF],
                ffn_conv_w=t[11:20].reshape(3, 3072)[:, :D_FF])


_BIG = ("w_in", "w_out", "w_gate", "w_up", "w_down", "w_ple_gate", "w_ple_proj")
_COL_SHARDED = dict(w_in=True, w_out=False, w_gate=False, w_up=False, w_down=False, w_ple_gate=False, w_ple_proj=True)
_TRANSPOSED = ("w_gate", "w_up")
_WEIGHTS = ("g_mix", "w_in", "conv_w", "conv_b", "q_norm_g", "k_norm_g", "g_out_conv", "g_out_attn", "w_out",
            "g_ffn", "w_gate", "w_up", "ffn_conv_w", "ffn_conv_b", "w_down", "g_ple", "w_ple_gate", "w_ple_proj")


def kernel(x, p, g_mix, w_in, conv_w, conv_b, q_norm_g, k_norm_g, g_out_conv, g_out_attn, w_out, g_ffn, w_gate, w_up, ffn_conv_w, ffn_conv_b, w_down, g_ple, w_ple_gate, w_ple_proj, loss_target, m_g_mix, m_w_in, m_conv_w, m_conv_b, m_q_norm_g, m_k_norm_g, m_g_out_conv, m_g_out_attn, m_w_out, m_g_ffn, m_w_gate, m_w_up, m_ffn_conv_w, m_ffn_conv_b, m_w_down, m_g_ple, m_w_ple_gate, m_w_ple_proj, v_g_mix, v_w_in, v_conv_w, v_conv_b, v_q_norm_g, v_k_norm_g, v_g_out_conv, v_g_out_attn, v_w_out, v_g_ffn, v_w_gate, v_w_up, v_ffn_conv_w, v_ffn_conv_b, v_w_down, v_g_ple, v_w_ple_gate, v_w_ple_proj):
    w = dict(g_mix=g_mix, w_in=w_in, conv_w=conv_w, conv_b=conv_b, q_norm_g=q_norm_g, k_norm_g=k_norm_g,
             g_out_conv=g_out_conv, g_out_attn=g_out_attn, w_out=w_out, g_ffn=g_ffn, w_gate=w_gate, w_up=w_up,
             ffn_conv_w=ffn_conv_w, ffn_conv_b=ffn_conv_b, w_down=w_down, g_ple=g_ple, w_ple_gate=w_ple_gate,
             w_ple_proj=w_ple_proj)
    m = dict(g_mix=m_g_mix, w_in=m_w_in, conv_w=m_conv_w, conv_b=m_conv_b, q_norm_g=m_q_norm_g, k_norm_g=m_k_norm_g,
             g_out_conv=m_g_out_conv, g_out_attn=m_g_out_attn, w_out=m_w_out, g_ffn=m_g_ffn, w_gate=m_w_gate,
             w_up=m_w_up, ffn_conv_w=m_ffn_conv_w, ffn_conv_b=m_ffn_conv_b, w_down=m_w_down, g_ple=m_g_ple,
             w_ple_gate=m_w_ple_gate, w_ple_proj=m_w_ple_proj)
    v = dict(g_mix=v_g_mix, w_in=v_w_in, conv_w=v_conv_w, conv_b=v_conv_b, q_norm_g=v_q_norm_g, k_norm_g=v_k_norm_g,
             g_out_conv=v_g_out_conv, g_out_attn=v_g_out_attn, w_out=v_w_out, g_ffn=v_g_ffn, w_gate=v_w_gate,
             w_up=v_w_up, ffn_conv_w=v_ffn_conv_w, ffn_conv_b=v_ffn_conv_b, w_down=v_w_down, g_ple=v_g_ple,
             w_ple_gate=v_w_ple_gate, w_ple_proj=v_w_ple_proj)
    mats = [k for k, a in w.items() if a.ndim == 3]
    w = {k: (a[0] if k in mats else a) for k, a in w.items()}
    m = {k: (a[0] if k in mats else a) for k, a in m.items()}
    v = {k: (a[0] if k in mats else a) for k, a in v.items()}
    for n in _TRANSPOSED:
        w[n], m[n], v[n] = w[n].T, m[n].T, v[n].T
    chip = 2 * lax.axis_index("x") + lax.axis_index("y")

    late = [w[n].astype(BF) for n in ("w_out", "w_gate", "w_up", "w_down", "w_ple_gate", "w_ple_proj")]
    pack = jnp.pad(jnp.concatenate([w["conv_w"], w["ffn_conv_w"]], axis=1), ((0, 5), (0, 1024 - 128 - D_FF_SHARD)))
    win4, pack4 = _gather_weights([w["w_in"].astype(BF)], pack)
    conv_w_full = pack4[:, 0:3, 0:128].transpose(1, 0, 2).reshape(3, CONV_W)
    fcw_full = pack4[:, 0:3, 128:128 + D_FF_SHARD].transpose(1, 0, 2).reshape(3, D_FF)
    wts = (win4, None, None, None, None, None, None, conv_w_full, fcw_full, w["g_mix"], w["conv_b"], w["q_norm_g"],
           w["k_norm_g"], w["g_out_conv"], w["g_out_attn"], w["g_ffn"], w["ffn_conv_b"], w["g_ple"])

    loss, grad_x, big, slots, small = _local_step(x[0], p[0, 0], loss_target[0], wts, late)
    loss = lax.psum(loss, ("x", "y", "c"))

    grads, deltas, new_m, new_v = {}, {}, {}, {}
    for name in _BIG:
        total = _finish_reduce(big[name], slots[name], _COL_SHARDED[name], "finish_" + name)
        grads[name], deltas[name], new_m[name], new_v[name] = _adamw_big(total, w[name], m[name], v[name],
                                                                         "adamw_" + name)
    tot = _unpack_small(_allreduce_small(_pack_small(small)))
    tot["conv_w"] = lax.dynamic_slice_in_dim(tot["conv_w"], chip * 128, 128, axis=1)
    tot["ffn_conv_w"] = lax.dynamic_slice_in_dim(tot["ffn_conv_w"], chip * D_FF_SHARD, D_FF_SHARD, axis=1)
    names = [n for n in _WEIGHTS if n not in _BIG]
    d_s, m_s, v_s = _adamw_small([w[n] for n in names], [tot[n] for n in names], [m[n] for n in names],
                                 [v[n] for n in names])
    for j, n in enumerate(names):
        grads[n], deltas[n], new_m[n], new_v[n] = tot[n], d_s[j], m_s[j], v_s[j]

    out = [loss, grad_x[None]]
    for group in (grads, deltas, new_m, new_v):
        for n in _TRANSPOSED:
            group[n] = group[n].T
        out += [group[n][None] if n in mats else group[n] for n in _WEIGHTS]
    return tuple(out)
```

```python
import jax
import jax.numpy as jnp
from jax import lax
from jax.experimental import pallas as pl
from jax.experimental.pallas import tpu as pltpu

D_MODEL = 1024
CONV_W = 512
N_HEADS = 8
HEAD_DIM = 64
ATTN_W = 512
D_FF = 2816
D_FF_SHARD = 704
FF_SLABS = ((0, 1408), (1408, 2816))
IN_SLAB = 768
PLE_DIM = 256
N_CHIPS = 4
QBLK = 128
DILATIONS = (1, 4, 16)
EPS = 1e-6
NEG = -1e30
MESH = pl.DeviceIdType.MESH

ADAM_LR = 0.001
ADAM_B1 = 0.9
ADAM_B2 = 0.999
ADAM_EPS = 1e-08
ADAM_WD = 0.01
ADAM_STEP = 10

BF = jnp.bfloat16
F32 = jnp.float32
MIB = 1024 * 1024


def _mm(a, b):
    return jnp.dot(a, b, preferred_element_type=F32)


def _mm_nt(a, b):
    return lax.dot_general(a, b, (((1,), (1,)), ((), ())), preferred_element_type=F32)


def _mm_tn(a, b):
    return lax.dot_general(a, b, (((0,), (0,)), ((), ())), preferred_element_type=F32)


def _rstd(a):
    return lax.rsqrt(jnp.mean(a * a, axis=-1, keepdims=True) + EPS)


def _norm_bwd(dy, xh, r, g):
    dxh = dy * g
    return r * (dxh - xh * jnp.mean(dxh * xh, axis=-1, keepdims=True))


def _colsum(a):
    return jnp.sum(a, axis=0, keepdims=True)


def _head_mean(a, gm_ref):
    return _mm(a.astype(BF), gm_ref[...])


def _shift_down(buf, k, tm):
    return pltpu.roll(buf, k, axis=0)[8:8 + tm]


def _shift_up(buf, k, tm):
    return pltpu.roll(buf, tm + 8 - k, axis=0)[0:tm]


def _params(vmem_mib, n_grid=1):
    return pltpu.CompilerParams(dimension_semantics=("arbitrary",) * n_grid, vmem_limit_bytes=vmem_mib * MIB)


def _const(shape):
    n = len(shape)
    return pl.BlockSpec(shape, lambda *_: (0,) * n, pipeline_mode=pl.Buffered(1))


def _rows(tm, width, rev_of=None):
    if rev_of is None:
        return pl.BlockSpec((tm, width), lambda i: (i, 0))
    return pl.BlockSpec((tm, width), lambda i: (rev_of - 1 - i, 0))


def _halo(tm, width, nt):
    return pl.BlockSpec((8, width), lambda i: (jnp.maximum((nt - 1 - i) * (tm // 8) - 1, 0), 0))


def _fwd_mix(x, g_mix, win4, conv_w, conv_b, g_oc, gm, gq8, gk8, tm):
    t = x.shape[0]
    nt = t // tm

    def body(x_ref, g_ref, w_ref, cw_ref, cb_ref, goc_ref, gm_ref, gq_ref, gk_ref,
             zbcx_ref, qkv_ref, ycn_ref, qkn_ref, ubuf):
        @pl.when(pl.program_id(0) == 0)
        def _():
            ubuf[0:8, :] = jnp.zeros((8, CONV_W), F32)

        xt = x_ref[...]
        h = ((xt * _rstd(xt)) * g_ref[...]).astype(BF)
        zbcx_ref[:, 0:IN_SLAB] = _mm(h, w_ref[0])
        zbcx_ref[:, IN_SLAB:2 * IN_SLAB] = _mm(h, w_ref[1])
        qkv_ref[:, 0:IN_SLAB] = _mm(h, w_ref[2])
        qkv_ref[:, IN_SLAB:2 * IN_SLAB] = _mm(h, w_ref[3])
        u = zbcx_ref[:, 512:1024] * zbcx_ref[:, 1024:1536]
        ubuf[8:8 + tm, :] = u
        ub = ubuf[...]
        cv = (cw_ref[0:1, :] * _shift_down(ub, 2, tm) + cw_ref[1:2, :] * _shift_down(ub, 1, tm)
              + cw_ref[2:3, :] * u + cb_ref[...])
        ubuf[0:8, :] = ubuf[tm:tm + 8, :]
        yc = zbcx_ref[:, 0:512] * cv
        ycn_ref[...] = ((yc * _rstd(yc)) * goc_ref[...]).astype(BF)
        zq = qkv_ref[:, 0:512]
        zk = qkv_ref[:, 512:1024]
        rq = lax.rsqrt(_head_mean(zq * zq, gm_ref) + EPS)
        rk = lax.rsqrt(_head_mean(zk * zk, gm_ref) + EPS)
        qkn_ref[:, 0:512] = ((zq * rq) * gq_ref[...]) * (HEAD_DIM ** -0.5)
        qkn_ref[:, 512:1024] = (zk * rk) * gk_ref[...]

    return pl.pallas_call(
        body, name="fwd_mix", grid=(nt,),
        in_specs=[_rows(tm, D_MODEL), _const((1, D_MODEL)), _const((N_CHIPS, D_MODEL, IN_SLAB)),
                  _const((3, CONV_W)), _const((1, CONV_W)), _const((1, CONV_W)), _const((ATTN_W, ATTN_W)),
                  _const((1, ATTN_W)), _const((1, ATTN_W))],
        out_specs=[_rows(tm, 1536), _rows(tm, 1536), _rows(tm, CONV_W), _rows(tm, 1024)],
        out_shape=[jax.ShapeDtypeStruct((t, 1536), F32), jax.ShapeDtypeStruct((t, 1536), F32),
                   jax.ShapeDtypeStruct((t, CONV_W), BF), jax.ShapeDtypeStruct((t, 1024), F32)],
        scratch_shapes=[pltpu.VMEM((tm + 8, CONV_W), F32)],
        compiler_params=_params(48),
    )(x, g_mix, win4, conv_w, conv_b, g_oc, gm, gq8, gk8)


def _place():
    x, y, c = lax.axis_index("x"), lax.axis_index("y"), lax.axis_index("c")
    return x, y, c


def _chip_peer(x, y, k):
    return x ^ (k >> 1), y ^ (k & 1)


def _piece_shape(grad_shape, col_sharded):
    kk, nn = grad_shape
    return (kk // 2, nn // N_CHIPS) if col_sharded else (kk // (2 * N_CHIPS), nn)


def _piece_window(col_sharded, r, cw, s, h):
    if col_sharded:
        return (pl.ds(pl.multiple_of(h * r, 16), r), pl.ds(pl.multiple_of(s * cw, 128), cw))
    return (pl.ds(pl.multiple_of((2 * s + h) * r, 16), r), slice(None))


def _scatter_copies(g_ref, slots_ref, send_sems, recv_sems, base, col_sharded):
    x, y, c = _place()
    r, cw = slots_ref.shape[1:]
    copies = []
    for k in range(1, 8):
        tx, ty, tc = x ^ (k >> 2), y ^ ((k >> 1) & 1), c ^ (k & 1)
        copies.append(pltpu.make_async_remote_copy(
            src_ref=g_ref.at[_piece_window(col_sharded, r, cw, 2 * tx + ty, tc)], dst_ref=slots_ref.at[k - 1],
            send_sem=send_sems.at[base + k - 1], recv_sem=recv_sems.at[base + k - 1],
            device_id=(tx, ty, tc), device_id_type=MESH))
    return copies


def _ride_scatter(first, last, riders, g_refs, slot_refs, send_sems, recv_sems):
    def all_copies():
        out = []
        for j, (_, col_sharded) in enumerate(riders):
            out += _scatter_copies(g_refs[j], slot_refs[j], send_sems, recv_sems, 7 * j, col_sharded)
        return out

    @pl.when(first)
    def _():
        for cp in all_copies():
            cp.start()

    @pl.when(last)
    def _():
        for cp in all_copies():
            cp.wait()


def _rider_specs(riders):
    any_spec = pl.BlockSpec(memory_space=pl.ANY)
    shapes = [jax.ShapeDtypeStruct((7,) + _piece_shape(g.shape, cs), BF) for g, cs in riders]
    sems = [pltpu.SemaphoreType.DMA((7 * len(riders),)), pltpu.SemaphoreType.DMA((7 * len(riders),))] if riders else []
    return [any_spec] * len(riders), shapes, sems


class _Gather:
    def __init__(self, ins, outs, send_sems, recv_sems, local_sems):
        self.ins, self.outs = ins, outs
        self.send_sems, self.recv_sems, self.local_sems = send_sems, recv_sems, local_sems
        self.x, self.y, self.c = _place()
        self.me = 2 * self.x + self.y

    def _push(self, src, dst, w, j, to):
        return pltpu.make_async_remote_copy(src_ref=src, dst_ref=dst, send_sem=self.send_sems.at[6 * w + j],
                                            recv_sem=self.recv_sems.at[6 * w + j], device_id=to, device_id_type=MESH)

    def _half(self, w, h):
        half = self.ins[w].shape[0] // 2
        return pl.ds(pl.multiple_of(h * half, 16), half)

    def _local(self, w):
        return pltpu.make_async_copy(self.ins[w], self.outs[w].at[self.me], self.local_sems.at[w])

    def _ici(self, w, k):
        px, py = _chip_peer(self.x, self.y, k)
        mine = self._half(w, self.c)
        return self._push(self.ins[w].at[mine], self.outs[w].at[self.me, mine], w, k - 1, (px, py, self.c))

    def _landed(self, w, k, h):
        return self.outs[w].at[self.me ^ k, self._half(w, h)]

    def _fwd(self, w, k):
        landed = self._landed(w, k, self.c)
        return self._push(landed, landed, w, 2 + k, (self.x, self.y, 1 - self.c))

    def start(self):
        for w in range(len(self.ins)):
            self._local(w).start()
            for k in (1, 2, 3):
                self._ici(w, k).start()

    def forward(self):
        for w in range(len(self.ins)):
            for k in (1, 2, 3):
                landed = self._landed(w, k, self.c)
                self._push(landed, landed, w, k - 1, (self.x, self.y, self.c)).wait_recv()
                self._fwd(w, k).start()

    def finish(self):
        for w in range(len(self.ins)):
            for k in (1, 2, 3):
                landed = self._landed(w, k, 1 - self.c)
                self._push(landed, landed, w, 2 + k, (self.x, self.y, self.c)).wait_recv()
            for k in (1, 2, 3):
                self._ici(w, k).wait_send()
                self._fwd(w, k).wait_send()
            self._local(w).wait()


def _alibi(h):
    return 2.0 ** (-(h + 1))


CHUNK = 2048


def _mask_table():
    slopes = jnp.asarray([_alibi(h) for h in range(N_HEADS)], F32)[:, None, None]
    step = jnp.arange(QBLK)[:, None] + QBLK - jnp.arange(2 * QBLK)[None, :]
    valid = (step >= 0) & (step <= QBLK)
    tab = jnp.stack([jnp.where(valid[None], -slopes * (step * d)[None].astype(F32), NEG) for d in DILATIONS])
    return tab.reshape(3, N_HEADS // 2, 2 * QBLK, 2 * QBLK)


def _attn_fwd(qkn, qkv, mb, late=()):
    t = qkn.shape[0]
    nc = t // CHUNK
    nl = len(late)

    def body(*refs):
        qc_ref, kp_ref, kc_ref, vp_ref, vc_ref, mb_ref = refs[0:6]
        o_ref, l_ref = refs[6 + nl:8 + nl]
        ob0, ob1, ob2, lb0, lb1, lb2 = refs[8 + 2 * nl:14 + 2 * nl]
        if nl:
            gather = _Gather(refs[6:6 + nl], refs[8 + nl:8 + 2 * nl], *refs[14 + 2 * nl:17 + 2 * nl])
            step = pl.program_id(0) * nc + pl.program_id(1)
            pl.when(step == 0)(gather.start)
            pl.when(step == 2 * nc)(gather.forward)
            pl.when(step == (N_HEADS // 2) * nc - 1)(gather.finish)
        first = pl.program_id(1) == 0
        lane = lax.broadcasted_iota(jnp.int32, (QBLK, 128), 1)
        lo_half = lane < HEAD_DIM
        kj = lax.broadcasted_iota(jnp.int32, (2 * QBLK, 2 * QBLK), 1)
        no_prev = first & (kj < QBLK)
        obs, lbs = (ob0, ob1, ob2), (lb0, lb1, lb2)

        def by_head(a):
            return jnp.where(lo_half, a, 0.0).astype(BF), jnp.where(lo_half, 0.0, a).astype(BF)

        for di, d in enumerate(DILATIONS):
            span = d * QBLK
            for r in range(d):
                tail = pl.ds(CHUNK - span + r, QBLK, stride=d)
                k_prev = kp_ref[tail, :].astype(BF)
                v_prev = by_head(vp_ref[tail, :])
                for b in range(CHUNK // span):
                    rows = pl.ds(r + span * b, QBLK, stride=d)
                    q0, q1 = by_head(qc_ref[rows, :])
                    k_cur = kc_ref[rows, :].astype(BF)
                    v_cur = by_head(vc_ref[rows, :])
                    s = _mm_nt(jnp.concatenate([q0, q1], axis=0), jnp.concatenate([k_prev, k_cur], axis=0))
                    s = s + mb_ref[di, 0]
                    if b == 0:
                        s = jnp.where(no_prev, NEG, s)
                    m = jnp.max(s, axis=-1, keepdims=True)
                    e = jnp.exp(s - m)
                    den = jnp.sum(e, axis=-1, keepdims=True)
                    eb = e.astype(BF)
                    o = _mm(jnp.concatenate([eb[0:QBLK], eb[QBLK:2 * QBLK]], axis=1),
                            jnp.concatenate([v_prev[0], v_cur[0], v_prev[1], v_cur[1]], axis=0))
                    inv = 1.0 / den
                    lse = m + jnp.log(den)
                    obs[di][rows, :] = o * jnp.where(lo_half, inv[0:QBLK], inv[QBLK:2 * QBLK])
                    lbs[di][rows, :] = jnp.where(lo_half, lse[0:QBLK], lse[QBLK:2 * QBLK])
                    k_prev, v_prev = k_cur, v_cur
        for c0 in range(0, CHUNK, 256):
            rs = slice(c0, c0 + 256)
            l0, l1, l2 = lb0[rs, :], lb1[rs, :], lb2[rs, :]
            mx = jnp.maximum(jnp.maximum(l0, l1), l2)
            w0, w1, w2 = jnp.exp(l0 - mx), jnp.exp(l1 - mx), jnp.exp(l2 - mx)
            tot = w0 + w1 + w2
            o_ref[rs, :] = (ob0[rs, :] * w0 + ob1[rs, :] * w1 + ob2[rs, :] * w2) / tot
            l_ref[rs, :] = mx + jnp.log(tot)

    def cur(col):
        return pl.BlockSpec((CHUNK, 128), lambda hp, n: (n, col + hp))

    def prv(col):
        return pl.BlockSpec((CHUNK, 128), lambda hp, n: (jnp.maximum(n - 1, 0), col + hp))

    out = pl.BlockSpec((CHUNK, 128), lambda hp, n: (n, hp))
    any_spec = pl.BlockSpec(memory_space=pl.ANY)
    sems = [pltpu.SemaphoreType.DMA((6 * nl,)), pltpu.SemaphoreType.DMA((6 * nl,)), pltpu.SemaphoreType.DMA((nl,))]
    res = pl.pallas_call(
        body, name="attn_fwd", grid=(N_HEADS // 2, nc),
        in_specs=[cur(0), prv(4), cur(4), prv(8), cur(8),
                  pl.BlockSpec((3, 1, 2 * QBLK, 2 * QBLK), lambda hp, n: (0, hp, 0, 0))] + [any_spec] * nl,
        out_specs=[out, out] + [any_spec] * nl,
        out_shape=[jax.ShapeDtypeStruct((t, ATTN_W), F32)] * 2
        + [jax.ShapeDtypeStruct((N_CHIPS,) + w.shape, w.dtype) for w in late],
        scratch_shapes=[pltpu.VMEM((CHUNK, 128), F32)] * 6 + (sems if nl else []),
        compiler_params=_params(48, 2),
    )(qkn, qkn, qkn, qkv, qkv, mb, *late)
    return res[0], res[1], list(res[2:])


def _attn_bwd(qkn, qkv, o, lse, do, mb, riders=()):
    t = qkn.shape[0]
    nc = t // CHUNK
    nr = len(riders)

    def body(*refs):
        (qc_ref, qn_ref, kp_ref, kc_ref, vp_ref, vc_ref, oc_ref, on_ref, lc_ref, ln_ref, dc_ref, dn_ref,
         mb_ref) = refs[0:13]
        dq_ref, dk_ref, dv_ref = refs[13 + nr:16 + nr]
        if nr:
            step = pl.program_id(0) * nc + pl.program_id(1)
            _ride_scatter(step == 0, step == (N_HEADS // 2) * nc - 1, riders, refs[13:13 + nr],
                          refs[16 + nr:16 + 2 * nr], *refs[16 + 2 * nr:18 + 2 * nr])
        first = pl.program_id(1) == 0
        last = pl.program_id(1) == nc - 1
        lane = lax.broadcasted_iota(jnp.int32, (QBLK, 128), 1)
        lo_half = lane < HEAD_DIM
        kj = lax.broadcasted_iota(jnp.int32, (2 * QBLK, 2 * QBLK), 1)
        no_prev = first & (kj < QBLK)

        def by_head(a):
            return jnp.where(lo_half, a, 0.0).astype(BF), jnp.where(lo_half, 0.0, a).astype(BF)

        def query_side(q_ref, d_ref, o_ref_, l_ref_, rows):
            dvals = d_ref[rows, :]
            dd = dvals * o_ref_[rows, :]
            lv = l_ref_[rows, :]
            d0 = jnp.sum(jnp.where(lo_half, dd, 0.0), axis=-1, keepdims=True)
            d1 = jnp.sum(jnp.where(lo_half, 0.0, dd), axis=-1, keepdims=True)
            l0 = jnp.max(jnp.where(lo_half, lv, NEG), axis=-1, keepdims=True)
            l1 = jnp.max(jnp.where(lo_half, NEG, lv), axis=-1, keepdims=True)
            return (jnp.concatenate(by_head(q_ref[rows, :]), axis=0), jnp.concatenate(by_head(dvals), axis=0),
                    jnp.concatenate([l0, l1], axis=0), jnp.concatenate([d0, d1], axis=0))

        def tile(qs, dos, lcol, dcol, keys, vals, bias, dead):
            s = _mm_nt(qs, keys) + bias
            if dead is not None:
                s = jnp.where(dead, NEG, s)
            p = jnp.exp(s - lcol)
            ds = p * (_mm_nt(dos, vals) - dcol)
            return p.astype(BF), ds.astype(BF)

        def put(ref, di, rows, val):
            if di == 0:
                ref[rows, :] = val
            else:
                ref[rows, :] = ref[rows, :] + val

        for di, d in enumerate(DILATIONS):
            span = d * QBLK
            nbk = CHUNK // span
            for r in range(d):
                tail = pl.ds(CHUNK - span + r, QBLK, stride=d)
                k_prev = kp_ref[tail, :]
                kb_prev, km_prev = k_prev.astype(BF), by_head(k_prev)
                vb_prev = vp_ref[tail, :].astype(BF)
                rows_prev, dk_part, dv_part = None, None, None
                for b in range(nbk):
                    rows = pl.ds(r + span * b, QBLK, stride=d)
                    qs, dos, lcol, dcol = query_side(qc_ref, dc_ref, oc_ref, lc_ref, rows)
                    k_cur = kc_ref[rows, :]
                    kb_cur, km_cur = k_cur.astype(BF), by_head(k_cur)
                    vb_cur = vc_ref[rows, :].astype(BF)
                    p, ds = tile(qs, dos, lcol, dcol, jnp.concatenate([kb_prev, kb_cur], axis=0),
                                 jnp.concatenate([vb_prev, vb_cur], axis=0), mb_ref[di, 0],
                                 no_prev if b == 0 else None)
                    put(dq_ref, di, rows,
                        _mm(jnp.concatenate([ds[0:QBLK], ds[QBLK:2 * QBLK]], axis=1),
                            jnp.concatenate([km_prev[0], km_cur[0], km_prev[1], km_cur[1]], axis=0)))
                    dk2 = _mm_tn(ds, qs)
                    dv2 = _mm_tn(p, dos)
                    if b > 0:
                        put(dk_ref, di, rows_prev, dk_part + dk2[0:QBLK])
                        put(dv_ref, di, rows_prev, dv_part + dv2[0:QBLK])
                    rows_prev, dk_part, dv_part = rows, dk2[QBLK:2 * QBLK], dv2[QBLK:2 * QBLK]
                    kb_prev, km_prev, vb_prev = kb_cur, km_cur, vb_cur
                qs, dos, lcol, dcol = query_side(qn_ref, dn_ref, on_ref, ln_ref, pl.ds(r, QBLK, stride=d))
                p, ds = tile(qs, dos, lcol, dcol, kb_prev, vb_prev, mb_ref[di, 0, :, 0:QBLK], last)
                put(dk_ref, di, rows_prev, dk_part + _mm_tn(ds, qs))
                put(dv_ref, di, rows_prev, dv_part + _mm_tn(p, dos))

    def at(shift, col):
        return pl.BlockSpec((CHUNK, 128), lambda hp, n: (jnp.clip(n + shift, 0, nc - 1), col + hp))

    out = pl.BlockSpec((CHUNK, 128), lambda hp, n: (n, hp))
    r_in, r_out, r_sems = _rider_specs(riders)
    res = pl.pallas_call(
        body, name="attn_bwd", grid=(N_HEADS // 2, nc),
        in_specs=[at(0, 0), at(1, 0), at(-1, 4), at(0, 4), at(-1, 8), at(0, 8),
                  at(0, 0), at(1, 0), at(0, 0), at(1, 0), at(0, 0), at(1, 0),
                  pl.BlockSpec((3, 1, 2 * QBLK, 2 * QBLK), lambda hp, n: (0, hp, 0, 0))] + r_in,
        out_specs=[out, out, out] + r_in,
        out_shape=[jax.ShapeDtypeStruct((t, ATTN_W), F32)] * 3 + r_out,
        scratch_shapes=r_sems,
        compiler_params=_params(56, 2),
    )(qkn, qkn, qkn, qkn, qkv, qkv, o, o, lse, lse, do, do, mb, *[g for g, _ in riders])
    return res[0], res[1], res[2], list(res[3:])


def _fwd_ffn(x, ycn, ya, wout, wg4, wu4, g_oa, g_ffn, fcw, fcb, tm, late=()):
    t = x.shape[0]
    nt = t // tm
    nl = len(late)

    def body(*refs):
        x_ref, ycn_ref, ya_ref, wout_ref, wg_ref, wu_ref, goa_ref, gffn_ref, fcw_ref, fcb_ref = refs[0:10]
        x1_ref, gp_ref, up_ref, gate_ref, act_ref, ycat_ref, h2_ref = refs[10 + nl:17 + nl]
        cbuf = refs[17 + 2 * nl]
        if nl:
            gather = _Gather(refs[10:10 + nl], refs[17 + nl:17 + 2 * nl], *refs[18 + 2 * nl:21 + 2 * nl])
            pl.when(pl.program_id(0) == 0)(gather.start)
            pl.when(pl.program_id(0) == nt // 2)(gather.forward)
            pl.when(pl.program_id(0) == nt - 1)(gather.finish)

        @pl.when(pl.program_id(0) == 0)
        def _():
            cbuf[0:8, :] = jnp.zeros((8, D_FF), F32)

        yat = ya_ref[...]
        yan = ((yat * _rstd(yat)) * goa_ref[...]).astype(BF)
        ycn = ycn_ref[...]
        ycat_ref[:, 0:CONV_W] = ycn
        ycat_ref[:, CONV_W:D_MODEL] = yan
        x1 = x_ref[...] + _mm(ycn, wout_ref[0:CONV_W, :]) + _mm(yan, wout_ref[CONV_W:D_MODEL, :])
        x1_ref[...] = x1
        h2 = ((x1 * _rstd(x1)) * gffn_ref[...]).astype(BF)
        h2_ref[...] = h2
        for lo, hi in FF_SLABS:
            gps = _mm_nt(h2, wg_ref[lo:hi, :])
            ups = _mm_nt(h2, wu_ref[lo:hi, :])
            gp_ref[:, lo:hi] = gps.astype(BF)
            up_ref[:, lo:hi] = ups.astype(BF)
            cbuf[8:8 + tm, lo:hi] = gps
            cb = cbuf[:, lo:hi]
            gate = (fcw_ref[0:1, lo:hi] * _shift_down(cb, 2, tm) + fcw_ref[1:2, lo:hi] * _shift_down(cb, 1, tm)
                    + fcw_ref[2:3, lo:hi] * gps + fcb_ref[:, lo:hi])
            gate_ref[:, lo:hi] = gate.astype(BF)
            act_ref[:, lo:hi] = ((gate * jax.nn.sigmoid(gate)) * ups).astype(BF)
        cbuf[0:8, :] = cbuf[tm:tm + 8, :]

    any_spec = pl.BlockSpec(memory_space=pl.ANY)
    sems = [pltpu.SemaphoreType.DMA((6 * nl,)), pltpu.SemaphoreType.DMA((6 * nl,)), pltpu.SemaphoreType.DMA((nl,))]
    res = pl.pallas_call(
        body, name="fwd_ffn", grid=(nt,),
        in_specs=[_rows(tm, D_MODEL), _rows(tm, CONV_W), _rows(tm, ATTN_W), _const((D_MODEL, D_MODEL)),
                  _const((D_FF, D_MODEL)), _const((D_FF, D_MODEL)),
                  _const((1, ATTN_W)), _const((1, D_MODEL)), _const((3, D_FF)), _const((1, D_FF))]
        + [any_spec] * nl,
        out_specs=[_rows(tm, D_MODEL), _rows(tm, D_FF), _rows(tm, D_FF), _rows(tm, D_FF), _rows(tm, D_FF),
                   _rows(tm, D_MODEL), _rows(tm, D_MODEL)] + [any_spec] * nl,
        out_shape=[jax.ShapeDtypeStruct((t, D_MODEL), F32), jax.ShapeDtypeStruct((t, D_FF), BF),
                   jax.ShapeDtypeStruct((t, D_FF), BF), jax.ShapeDtypeStruct((t, D_FF), BF),
                   jax.ShapeDtypeStruct((t, D_FF), BF),
                   jax.ShapeDtypeStruct((t, D_MODEL), BF), jax.ShapeDtypeStruct((t, D_MODEL), BF)]
        + [jax.ShapeDtypeStruct((N_CHIPS,) + w.shape, w.dtype) for w in late],
        scratch_shapes=[pltpu.VMEM((tm + 8, D_FF), F32)] + (sems if nl else []),
        compiler_params=_params(56),
    )(x, ycn, ya, wout, wg4, wu4, g_oa, g_ffn, fcw, fcb, *late)
    return tuple(res[0:7]) + (list(res[7:]),)


def _fwd_tail(x1, act, p, target, wd4, wpg, wpp4, g_ple, tm):
    t = x1.shape[0]
    nt = t // tm

    def body(x1_ref, act_ref, p_ref, tgt_ref, wd_ref, wpg_ref, wpp_ref, g_ref,
             dx2_ref, h3_ref, ds_ref, dpp_ref, dg_ref, loss_ref, lacc):
        i = pl.program_id(0)

        @pl.when(i == 0)
        def _():
            dg_ref[...] = jnp.zeros_like(dg_ref)
            lacc[...] = jnp.zeros_like(lacc)

        x2 = x1_ref[...]
        for lo, hi in FF_SLABS:
            x2 = x2 + _mm(act_ref[:, lo:hi], wd_ref[lo:hi, :])
        r3 = _rstd(x2)
        xh = x2 * r3
        h3 = (xh * g_ref[...]).astype(BF)
        h3_ref[...] = h3
        sg = jax.nn.sigmoid(_mm(h3, wpg_ref[...]))
        pb = p_ref[...].astype(BF)
        pp = jnp.concatenate([_mm(pb, wpp_ref[s]) for s in range(N_CHIPS)], axis=1)
        err = (x2 + sg * pp) - tgt_ref[...]
        lacc[...] += _colsum(err * err)
        dx3 = err * (1.0 / D_MODEL)
        dpp_ref[...] = (dx3 * sg).astype(BF)
        dsb = ((dx3 * pp) * (sg * (1.0 - sg))).astype(BF)
        ds_ref[...] = dsb
        dh3 = _mm_nt(dsb, wpg_ref[...])
        dg_ref[...] += _colsum(dh3 * xh)
        dx2_ref[...] = dx3 + _norm_bwd(dh3, xh, r3, g_ref[...])

        @pl.when(i == nt - 1)
        def _():
            loss_ref[...] = jnp.full((1, 128), jnp.sum(lacc[...]) * (0.5 / D_MODEL), F32)

    return pl.pallas_call(
        body, name="fwd_tail", grid=(nt,),
        in_specs=[_rows(tm, D_MODEL), _rows(tm, D_FF), _rows(tm, PLE_DIM), _rows(tm, D_MODEL),
                  _const((D_FF, D_MODEL)), _const((D_MODEL, D_MODEL)),
                  _const((N_CHIPS, PLE_DIM, PLE_DIM)), _const((1, D_MODEL))],
        out_specs=[_rows(tm, D_MODEL), _rows(tm, D_MODEL), _rows(tm, D_MODEL), _rows(tm, D_MODEL),
                   pl.BlockSpec((1, D_MODEL), lambda i: (0, 0)), pl.BlockSpec((1, 128), lambda i: (0, 0))],
        out_shape=[jax.ShapeDtypeStruct((t, D_MODEL), F32), jax.ShapeDtypeStruct((t, D_MODEL), BF),
                   jax.ShapeDtypeStruct((t, D_MODEL), BF), jax.ShapeDtypeStruct((t, D_MODEL), BF),
                   jax.ShapeDtypeStruct((1, D_MODEL), F32), jax.ShapeDtypeStruct((1, 128), F32)],
        scratch_shapes=[pltpu.VMEM((1, D_MODEL), F32)],
        compiler_params=_params(48),
    )(x1, act, p, target, wd4, wpg, wpp4, g_ple)


def _bwd_ffn_a(dx2, gate, gp, up, wd4, fcw, tm, riders=()):
    t = dx2.shape[0]
    nt = t // tm
    nr = len(riders)

    def body(*refs):
        dx2_ref, gate_ref, gp_ref, up_ref, wd_ref, fcw_ref = refs[0:6]
        dgp_ref, dup_ref, dfcw_ref, dfcb_ref = refs[6 + nr:10 + nr]
        dbuf = refs[10 + 2 * nr]
        i = pl.program_id(0)
        if nr:
            _ride_scatter(i == 0, i == nt - 1, riders, refs[6:6 + nr], refs[10 + nr:10 + 2 * nr],
                          *refs[11 + 2 * nr:13 + 2 * nr])

        @pl.when(i == 0)
        def _():
            dbuf[tm:tm + 8, :] = jnp.zeros((8, D_FF), F32)
            dfcw_ref[...] = jnp.zeros_like(dfcw_ref)
            dfcb_ref[...] = jnp.zeros_like(dfcb_ref)

        dx2b = dx2_ref[...].astype(BF)
        for lo, hi in FF_SLABS:
            gate = gate_ref[:, lo:hi].astype(F32)
            gps = gp_ref[:, lo:hi].astype(F32)
            w0, w1, w2 = fcw_ref[0:1, lo:hi], fcw_ref[1:2, lo:hi], fcw_ref[2:3, lo:hi]
            sg = jax.nn.sigmoid(gate)
            dact = _mm_nt(dx2b, wd_ref[lo:hi, :])
            dup_ref[:, lo:hi] = (dact * (gate * sg)).astype(BF)
            dgate = (dact * up_ref[:, lo:hi].astype(F32)) * (sg * (1.0 + gate * (1.0 - sg)))
            dbuf[0:tm, lo:hi] = dgate
            db = dbuf[:, lo:hi]
            d1 = _shift_up(db, 1, tm)
            d2 = _shift_up(db, 2, tm)
            dfcb_ref[:, lo:hi] += _colsum(dgate)
            dfcw_ref[0:1, lo:hi] += _colsum(d2 * gps)
            dfcw_ref[1:2, lo:hi] += _colsum(d1 * gps)
            dfcw_ref[2:3, lo:hi] += _colsum(dgate * gps)
            dgp_ref[:, lo:hi] = (w2 * dgate + w1 * d1 + w0 * d2).astype(BF)
        dbuf[tm:tm + 8, :] = dbuf[0:8, :]

    r_in, r_out, r_sems = _rider_specs(riders)
    res = pl.pallas_call(
        body, name="bwd_ffn_a", grid=(nt,),
        in_specs=[_rows(tm, D_MODEL, nt), _rows(tm, D_FF, nt), _rows(tm, D_FF, nt), _rows(tm, D_FF, nt),
                  _const((D_FF, D_MODEL)), _const((3, D_FF))] + r_in,
        out_specs=[_rows(tm, D_FF, nt), _rows(tm, D_FF, nt),
                   pl.BlockSpec((3, D_FF), lambda i: (0, 0)), pl.BlockSpec((1, D_FF), lambda i: (0, 0))] + r_in,
        out_shape=[jax.ShapeDtypeStruct((t, D_FF), BF), jax.ShapeDtypeStruct((t, D_FF), BF),
                   jax.ShapeDtypeStruct((3, D_FF), F32), jax.ShapeDtypeStruct((1, D_FF), F32)] + r_out,
        scratch_shapes=[pltpu.VMEM((tm + 8, D_FF), F32)] + r_sems,
        compiler_params=_params(56),
    )(dx2, gate, gp, up, wd4, fcw, *[g for g, _ in riders])
    return res[0], res[1], res[2], res[3], list(res[4:])


def _bwd_ffn_b(dgp, dup, dx2, x1, ya, wg4, wu4, wout, g_ffn, g_oa, tm):
    t = dx2.shape[0]
    nt = t // tm

    def body(dgp_ref, dup_ref, dx2_ref, x1_ref, ya_ref, wg_ref, wu_ref, wout_ref, gffn_ref, goa_ref,
             dx1_ref, dycn_ref, dya_ref, dgffn_ref, dgoa_ref):
        @pl.when(pl.program_id(0) == 0)
        def _():
            dgffn_ref[...] = jnp.zeros_like(dgffn_ref)
            dgoa_ref[...] = jnp.zeros_like(dgoa_ref)

        dh2 = jnp.zeros((tm, D_MODEL), F32)
        for lo, hi in FF_SLABS:
            dh2 = dh2 + _mm(dgp_ref[:, lo:hi], wg_ref[lo:hi, :]) + _mm(dup_ref[:, lo:hi], wu_ref[lo:hi, :])
        x1 = x1_ref[...]
        r2 = _rstd(x1)
        xh = x1 * r2
        dgffn_ref[...] += _colsum(dh2 * xh)
        dx1 = dx2_ref[...] + _norm_bwd(dh2, xh, r2, gffn_ref[...])
        dx1_ref[...] = dx1
        dy = _mm_nt(dx1.astype(BF), wout_ref[...])
        dycn_ref[...] = dy[:, 0:CONV_W]
        dyan = dy[:, CONV_W:D_MODEL]
        yat = ya_ref[...]
        ra = _rstd(yat)
        yah = yat * ra
        dgoa_ref[...] += _colsum(dyan * yah)
        dya_ref[...] = _norm_bwd(dyan, yah, ra, goa_ref[...])

    return pl.pallas_call(
        body, name="bwd_ffn_b", grid=(nt,),
        in_specs=[_rows(tm, D_FF), _rows(tm, D_FF), _rows(tm, D_MODEL), _rows(tm, D_MODEL),
                  _rows(tm, ATTN_W), _const((D_FF, D_MODEL)), _const((D_FF, D_MODEL)),
                  _const((D_MODEL, D_MODEL)), _const((1, D_MODEL)), _const((1, ATTN_W))],
        out_specs=[_rows(tm, D_MODEL), _rows(tm, CONV_W), _rows(tm, ATTN_W),
                   pl.BlockSpec((1, D_MODEL), lambda i: (0, 0)), pl.BlockSpec((1, ATTN_W), lambda i: (0, 0))],
        out_shape=[jax.ShapeDtypeStruct((t, D_MODEL), F32), jax.ShapeDtypeStruct((t, CONV_W), F32),
                   jax.ShapeDtypeStruct((t, ATTN_W), F32),
                   jax.ShapeDtypeStruct((1, D_MODEL), F32), jax.ShapeDtypeStruct((1, ATTN_W), F32)],
        compiler_params=_params(48),
    )(dgp, dup, dx2, x1, ya, wg4, wu4, wout, g_ffn, g_oa)


def _bwd_mix(x, dx1, zbcx, qkv, dycn, dq, dk, dv, win4, conv_w, conv_b, g_oc, g_mix, gm, gq8, gk8, tm):
    t = x.shape[0]
    nt = t // tm

    def body(x_ref, dx1_ref, z_ref, zh_ref, qkv_ref, dycn_ref, dq_ref, dk_ref, dv_ref, w_ref, cw_ref, cb_ref,
             goc_ref, g_ref, gm_ref, gq_ref, gk_ref,
             gx_ref, h1_ref, dz_ref, dcw_ref, dcb_ref, dgoc_ref, dg_ref, dgq_ref, dgk_ref, ubuf, dbuf):
        i = pl.program_id(0)

        @pl.when(i == 0)
        def _():
            dbuf[tm:tm + 8, :] = jnp.zeros((8, CONV_W), F32)
            dcw_ref[...] = jnp.zeros_like(dcw_ref)
            dcb_ref[...] = jnp.zeros_like(dcb_ref)
            dgoc_ref[...] = jnp.zeros_like(dgoc_ref)
            dg_ref[...] = jnp.zeros_like(dg_ref)
            dgq_ref[...] = jnp.zeros_like(dgq_ref)
            dgk_ref[...] = jnp.zeros_like(dgk_ref)

        not_first_tile = i < nt - 1
        zb = z_ref[:, 0:512]
        zc = z_ref[:, 512:1024]
        zx = z_ref[:, 1024:1536]
        u = zc * zx
        ubuf[0:8, :] = jnp.where(not_first_tile, zh_ref[:, 512:1024] * zh_ref[:, 1024:1536], 0.0)
        ubuf[8:8 + tm, :] = u
        ub = ubuf[...]
        u1 = _shift_down(ub, 1, tm)
        u2 = _shift_down(ub, 2, tm)
        w0, w1, w2 = cw_ref[0:1, :], cw_ref[1:2, :], cw_ref[2:3, :]
        cv = w0 * u2 + w1 * u1 + w2 * u + cb_ref[...]
        yc = zb * cv
        rc = _rstd(yc)
        ych = yc * rc
        dycn = dycn_ref[...]
        dgoc_ref[...] += _colsum(dycn * ych)
        dyc = _norm_bwd(dycn, ych, rc, goc_ref[...])
        dcv = dyc * zb
        dcb_ref[...] += _colsum(dcv)
        dcw_ref[0:1, :] += _colsum(dcv * u2)
        dcw_ref[1:2, :] += _colsum(dcv * u1)
        dcw_ref[2:3, :] += _colsum(dcv * u)
        dbuf[0:tm, :] = dcv
        db = dbuf[...]
        du = w2 * dcv + w1 * _shift_up(db, 1, tm) + w0 * _shift_up(db, 2, tm)
        dbuf[tm:tm + 8, :] = dbuf[0:8, :]
        dz_ref[:, 0:512] = (dyc * cv).astype(BF)
        dz_ref[:, 512:1024] = (du * zx).astype(BF)
        dz_ref[:, 1024:1536] = (du * zc).astype(BF)
        for z0, d_ref, gg_ref, acc_ref, sc in ((0, dq_ref, gq_ref, dgq_ref, HEAD_DIM ** -0.5),
                                               (512, dk_ref, gk_ref, dgk_ref, 1.0)):
            z = qkv_ref[:, z0:z0 + 512]
            rr = lax.rsqrt(_head_mean(z * z, gm_ref) + EPS)
            zh = z * rr
            dn = d_ref[...] * sc
            acc_ref[...] += _colsum(dn * zh)
            dzh = dn * gg_ref[...]
            dz_ref[:, 1536 + z0:1536 + z0 + 512] = (rr * (dzh - zh * _head_mean(dzh * zh, gm_ref))).astype(BF)
        dz_ref[:, 2560:3072] = dv_ref[...].astype(BF)
        dh1 = jnp.zeros((tm, D_MODEL), F32)
        for s in range(N_CHIPS):
            dh1 = dh1 + _mm_nt(dz_ref[:, s * IN_SLAB:(s + 1) * IN_SLAB], w_ref[s])
        xt = x_ref[...]
        r1 = _rstd(xt)
        xh = xt * r1
        h1_ref[...] = (xh * g_ref[...]).astype(BF)
        dg_ref[...] += _colsum(dh1 * xh)
        gx_ref[...] = dx1_ref[...] + _norm_bwd(dh1, xh, r1, g_ref[...])

    def acc(width, rows=1):
        return pl.BlockSpec((rows, width), lambda i: (0, 0))

    return pl.pallas_call(
        body, name="bwd_mix", grid=(nt,),
        in_specs=[_rows(tm, D_MODEL, nt), _rows(tm, D_MODEL, nt), _rows(tm, 1536, nt), _halo(tm, 1536, nt),
                  _rows(tm, 1536, nt), _rows(tm, CONV_W, nt), _rows(tm, ATTN_W, nt), _rows(tm, ATTN_W, nt),
                  _rows(tm, ATTN_W, nt), _const((N_CHIPS, D_MODEL, IN_SLAB)),
                  _const((3, CONV_W)), _const((1, CONV_W)), _const((1, CONV_W)), _const((1, D_MODEL)),
                  _const((ATTN_W, ATTN_W)), _const((1, ATTN_W)), _const((1, ATTN_W))],
        out_specs=[_rows(tm, D_MODEL, nt), _rows(tm, D_MODEL, nt), _rows(tm, 3072, nt),
                   acc(CONV_W, 3), acc(CONV_W), acc(CONV_W), acc(D_MODEL), acc(ATTN_W), acc(ATTN_W)],
        out_shape=[jax.ShapeDtypeStruct((t, D_MODEL), F32), jax.ShapeDtypeStruct((t, D_MODEL), BF),
                   jax.ShapeDtypeStruct((t, 3072), BF), jax.ShapeDtypeStruct((3, CONV_W), F32),
                   jax.ShapeDtypeStruct((1, CONV_W), F32), jax.ShapeDtypeStruct((1, CONV_W), F32),
                   jax.ShapeDtypeStruct((1, D_MODEL), F32), jax.ShapeDtypeStruct((1, ATTN_W), F32),
                   jax.ShapeDtypeStruct((1, ATTN_W), F32)],
        scratch_shapes=[pltpu.VMEM((tm + 8, CONV_W), F32), pltpu.VMEM((tm + 8, CONV_W), F32)],
        compiler_params=_params(56),
    )(x, dx1, zbcx, zbcx, qkv, dycn, dq, dk, dv, win4, conv_w, conv_b, g_oc, g_mix, gm, gq8, gk8)


def _wgrad(a, b, tn, tt, name):
    t, k = a.shape
    n = b.shape[1]
    nt = t // tt

    def body(a_ref, b_ref, o_ref, ob_ref):
        @pl.when(pl.program_id(1) == 0)
        def _():
            o_ref[...] = jnp.zeros_like(o_ref)

        o_ref[...] += _mm_tn(a_ref[...].astype(BF), b_ref[...].astype(BF))

        @pl.when(pl.program_id(1) == nt - 1)
        def _():
            ob_ref[...] = o_ref[...].astype(BF)

    spec = pl.BlockSpec((k, tn), lambda j, i: (0, j))
    return pl.pallas_call(
        body, name=name, grid=(n // tn, nt),
        in_specs=[pl.BlockSpec((tt, k), lambda j, i: (i, 0)), pl.BlockSpec((tt, tn), lambda j, i: (i, j))],
        out_specs=[spec, spec],
        out_shape=[jax.ShapeDtypeStruct((k, n), F32), jax.ShapeDtypeStruct((k, n), BF)],
        compiler_params=_params(48, 2),
    )(a, b)


def _gather_weights(shards, pack):
    nw = len(shards)

    def body(*refs):
        ins = refs[:nw]
        pack_ref = refs[nw]
        outs = refs[nw + 1:2 * nw + 1]
        pack_out = refs[2 * nw + 1]
        send_sems, recv_sems, local_sems = refs[2 * nw + 2:]
        x, y, c = _place()
        me = 2 * x + y
        local, remote = [], []

        def sem(w, j):
            return w * 6 + j

        def push(src, dst, w, j, to):
            return pltpu.make_async_remote_copy(src_ref=src, dst_ref=dst, send_sem=send_sems.at[sem(w, j)],
                                                recv_sem=recv_sems.at[sem(w, j)], device_id=to, device_id_type=MESH)

        def half_rows(w, h):
            half = ins[w].shape[0] // 2
            return pl.ds(pl.multiple_of(h * half, 16), half)

        for w in range(nw):
            local.append(pltpu.make_async_copy(ins[w], outs[w].at[me], local_sems.at[w]))
            for k in (1, 2, 3):
                px, py = _chip_peer(x, y, k)
                mine = half_rows(w, c)
                remote.append(push(ins[w].at[mine], outs[w].at[me, mine], w, k - 1, (px, py, c)))
        local.append(pltpu.make_async_copy(pack_ref, pack_out.at[me], local_sems.at[nw]))
        for k in (1, 2, 3):
            px, py = _chip_peer(x, y, k)
            remote.append(push(pack_ref, pack_out.at[me], nw, k - 1, (px, py, c)))
        for cp in local + remote:
            cp.start()
        for w in range(nw):
            for k in (1, 2, 3):
                landed = outs[w].at[me ^ k, half_rows(w, c)]
                push(landed, landed, w, k - 1, (x, y, c)).wait_recv()
                fw = push(landed, landed, w, 2 + k, (x, y, 1 - c))
                fw.start()
                remote.append(fw)
        for k in (1, 2, 3):
            landed = pack_out.at[me ^ k]
            push(landed, landed, nw, k - 1, (x, y, c)).wait_recv()
        for w in range(nw):
            for k in (1, 2, 3):
                landed = outs[w].at[me ^ k, half_rows(w, 1 - c)]
                push(landed, landed, w, 2 + k, (x, y, c)).wait_recv()
        for cp in remote:
            cp.wait_send()
        for cp in local:
            cp.wait()

    any_spec = pl.BlockSpec(memory_space=pl.ANY)
    out_shape = [jax.ShapeDtypeStruct((N_CHIPS,) + s.shape, s.dtype) for s in shards]
    out_shape.append(jax.ShapeDtypeStruct((N_CHIPS,) + pack.shape, pack.dtype))
    return pl.pallas_call(
        body, name="gather_weights",
        in_specs=[any_spec] * (nw + 1), out_specs=[any_spec] * (nw + 1), out_shape=out_shape,
        scratch_shapes=[pltpu.SemaphoreType.DMA(((nw + 1) * 6,)), pltpu.SemaphoreType.DMA(((nw + 1) * 6,)),
                        pltpu.SemaphoreType.DMA((nw + 1,))],
    )(*shards, pack)


def _adamw(w, g, m, v):
    m = ADAM_B1 * m + (1.0 - ADAM_B1) * g
    v = ADAM_B2 * v + (1.0 - ADAM_B2) * (g * g)
    m_hat = m / (1.0 - ADAM_B1 ** ADAM_STEP)
    v_hat = v / (1.0 - ADAM_B2 ** ADAM_STEP)
    delta = -ADAM_LR * (m_hat / (jnp.sqrt(v_hat) + ADAM_EPS) + ADAM_WD * w)
    return delta, m, v


def _scatter_alone(g16, col_sharded, name):
    def body(g_ref, slots_ref, send_sems, recv_sems):
        copies = _scatter_copies(g_ref, slots_ref, send_sems, recv_sems, 0, col_sharded)
        for cp in copies:
            cp.start()
        for cp in copies:
            cp.wait()

    any_spec = pl.BlockSpec(memory_space=pl.ANY)
    return pl.pallas_call(
        body, name=name, in_specs=[any_spec], out_specs=any_spec,
        out_shape=jax.ShapeDtypeStruct((7,) + _piece_shape(g16.shape, col_sharded), BF),
        scratch_shapes=[pltpu.SemaphoreType.DMA((7,)), pltpu.SemaphoreType.DMA((7,))],
    )(g16)


def _finish_reduce(grad, slots, col_sharded, name):
    r, cw = _piece_shape(grad.shape, col_sharded)
    chunk = 32
    assert r % chunk == 0

    def body(g_hbm, slots_ref, full, own, lsem, c_send, c_recv):
        x, y, c = _place()
        cp = pltpu.make_async_copy(g_hbm.at[_piece_window(col_sharded, r, cw, 2 * x + y, c)], own, lsem)
        cp.start()
        cp.wait()
        mine = pl.multiple_of(c * r, 8)

        def add(j, carry):
            rows = pl.ds(pl.multiple_of(j * chunk, 8), chunk)
            tot = own[rows, :]
            for k in range(7):
                tot = tot + slots_ref[k, rows, :].astype(F32)
            full[pl.ds(mine + pl.multiple_of(j * chunk, 8), chunk), :] = tot
            return carry

        lax.fori_loop(0, r // chunk, add, 0)
        half = full.at[pl.ds(mine, r), :]
        swap = pltpu.make_async_remote_copy(src_ref=half, dst_ref=half, send_sem=c_send, recv_sem=c_recv,
                                            device_id=(x, y, 1 - c), device_id_type=MESH)
        swap.start()
        swap.wait()

    vmem = pl.BlockSpec(memory_space=pltpu.VMEM)
    return pl.pallas_call(
        body, name=name, in_specs=[pl.BlockSpec(memory_space=pl.ANY), vmem], out_specs=vmem,
        out_shape=jax.ShapeDtypeStruct((2 * r, cw), F32),
        scratch_shapes=[pltpu.VMEM((r, cw), F32), pltpu.SemaphoreType.DMA, pltpu.SemaphoreType.DMA,
                        pltpu.SemaphoreType.DMA],
        compiler_params=pltpu.CompilerParams(vmem_limit_bytes=32 * MIB),
    )(grad, slots)


def _adamw_big(g, w, m, v, name):
    vr, vc = w.shape
    assert g.shape == w.shape
    rows = 64

    def body(g_ref, w_ref, m_ref, v_ref, go_ref, do_ref, mo_ref, vo_ref):
        gg = g_ref[...]
        delta, mn, vn = _adamw(w_ref[...], gg, m_ref[...], v_ref[...])
        go_ref[...] = gg
        do_ref[...] = delta
        mo_ref[...] = mn
        vo_ref[...] = vn

    blk = pl.BlockSpec((rows, vc), lambda i: (i, 0))
    shard = jax.ShapeDtypeStruct((vr, vc), F32)
    return pl.pallas_call(
        body, name=name, grid=(vr // rows,),
        in_specs=[blk, blk, blk, blk], out_specs=[blk] * 4,
        out_shape=[shard] * 4, compiler_params=_params(32),
    )(g, w, m, v)


def _allreduce_small(pack):
    rows = pack.shape[0]

    def body(p_ref, o_ref, slots, send_sems, recv_sems):
        x, y, c = _place()
        me = 4 * x + 2 * y + c
        slots[me] = p_ref[...]
        sends = []
        for k in range(1, 8):
            cp = pltpu.make_async_remote_copy(
                src_ref=p_ref, dst_ref=slots.at[me], send_sem=send_sems.at[k - 1], recv_sem=recv_sems.at[k - 1],
                device_id=(x ^ (k >> 2), y ^ ((k >> 1) & 1), c ^ (k & 1)), device_id_type=MESH)
            cp.start()
            sends.append(cp)
        for cp in sends:
            cp.wait()
        tot = slots[0]
        for j in range(1, 8):
            tot = tot + slots[j]
        o_ref[...] = tot

    vmem = pl.BlockSpec(memory_space=pltpu.VMEM)
    return pl.pallas_call(
        body, name="allreduce_small", in_specs=[vmem], out_specs=vmem,
        out_shape=jax.ShapeDtypeStruct(pack.shape, F32),
        scratch_shapes=[pltpu.VMEM((8, rows, D_MODEL), F32), pltpu.SemaphoreType.DMA((7,)),
                        pltpu.SemaphoreType.DMA((7,))],
    )(pack)


def _adamw_small(ws, gs, ms, vs):
    n = len(ws)

    def body(*refs):
        w_refs, g_refs, m_refs, v_refs = refs[0:n], refs[n:2 * n], refs[2 * n:3 * n], refs[3 * n:4 * n]
        d_refs, mo_refs, vo_refs = refs[4 * n:5 * n], refs[5 * n:6 * n], refs[6 * n:7 * n]
        for j in range(n):
            delta, mn, vn = _adamw(w_refs[j][...], g_refs[j][...], m_refs[j][...], v_refs[j][...])
            d_refs[j][...] = delta
            mo_refs[j][...] = mn
            vo_refs[j][...] = vn

    vmem = pl.BlockSpec(memory_space=pltpu.VMEM)
    shapes = [jax.ShapeDtypeStruct(w.shape, F32) for w in ws]
    outs = pl.pallas_call(
        body, name="adamw_small", in_specs=[vmem] * (4 * n), out_specs=[vmem] * (3 * n), out_shape=shapes * 3,
    )(*ws, *gs, *ms, *vs)
    return outs[0:n], outs[n:2 * n], outs[2 * n:3 * n]


def _local_step(x, p, target, wts, late=None):
    (win4, wout, wg4, wu4, wd4, wpg, wpp4, conv_w, fcw, g_mix, conv_b, gq, gk, g_oc, g_oa, g_ffn, fcb, g_ple) = wts
    comm = late is not None
    gm = jnp.kron(jnp.eye(N_HEADS, dtype=F32), jnp.full((HEAD_DIM, HEAD_DIM), 1.0 / HEAD_DIM, F32)).astype(BF)
    gq8, gk8 = jnp.tile(gq, (1, N_HEADS)), jnp.tile(gk, (1, N_HEADS))
    mb = _mask_table()
    zbcx, qkv, ycn, qkn = _fwd_mix(x, g_mix, win4, conv_w, conv_b, g_oc, gm, gq8, gk8, 512)
    ya, lse, gathered = _attn_fwd(qkn, qkv, mb, late[0:3] if comm else ())
    if comm:
        wout, wg4, wu4 = (g.reshape(-1, D_MODEL) for g in gathered)
    x1, gp, up, gate, act, ycat, h2, gathered = _fwd_ffn(x, ycn, ya, wout, wg4, wu4, g_oa, g_ffn, fcw, fcb, 256,
                                                    late[3:6] if comm else ())
    if comm:
        wd4, wpg, wpp4 = gathered
        wd4, wpg = wd4.reshape(D_FF, D_MODEL), wpg.reshape(D_MODEL, D_MODEL)
    dx2, h3, ds, dpp, dg_ple, loss = _fwd_tail(x1, act, p, target, wd4, wpg, wpp4, g_ple, 512)
    big, big16, slots = {}, {}, {}

    def wgrad(name, a, b, tn):
        big[name], big16[name] = _wgrad(a, b, tn, 1024, "wgrad_" + name)
        return (big16[name], _COL_SHARDED[name])

    riders = [wgrad("w_down", act, dx2, 512), wgrad("w_ple_gate", h3, ds, 1024), wgrad("w_ple_proj", p, dpp, 1024)]
    dgp, dup, dfcw, dfcb, got = _bwd_ffn_a(dx2, gate, gp, up, wd4, fcw, 512, riders if comm else ())
    slots.update(zip(("w_down", "w_ple_gate", "w_ple_proj"), got))
    riders = [wgrad("w_gate", dgp, h2, 512), wgrad("w_up", dup, h2, 512)]
    dx1, dycn, dya, dg_ffn, dg_oa = _bwd_ffn_b(dgp, dup, dx2, x1, ya, wg4, wu4, wout, g_ffn, g_oa, 512)
    riders.append(wgrad("w_out", ycat, dx1, 1024))
    dq, dk, dv, got = _attn_bwd(qkn, qkv, ya, lse, dya, mb, riders if comm else ())
    slots.update(zip(("w_gate", "w_up", "w_out"), got))
    grad_x, h1, dz, dcw, dcb, dg_oc, dg_mix, dgq8, dgk8 = _bwd_mix(
        x, dx1, zbcx, qkv, dycn, dq, dk, dv, win4, conv_w, conv_b, g_oc, g_mix, gm, gq8, gk8, 512)
    g16, cs = wgrad("w_in", h1, dz, 1536)
    if comm:
        slots["w_in"] = _scatter_alone(g16, cs, "scatter_w_in")
    dgq = dgq8.reshape(N_HEADS, HEAD_DIM).sum(axis=0, keepdims=True)
    dgk = dgk8.reshape(N_HEADS, HEAD_DIM).sum(axis=0, keepdims=True)
    small = dict(g_mix=dg_mix, conv_w=dcw, conv_b=dcb, q_norm_g=dgq, k_norm_g=dgk, g_out_conv=dg_oc,
                 g_out_attn=dg_oa, g_ffn=dg_ffn, ffn_conv_w=dfcw, ffn_conv_b=dfcb, g_ple=dg_ple)
    return loss[0, 0], grad_x, big, slots, small


_SMALL_ROWS = 24


def _pack_small(s):
    z64 = jnp.zeros((1, 1024 - 512 - 128), F32)
    rows = [s["g_mix"], s["g_ffn"], s["g_ple"],
            jnp.concatenate([s["conv_b"], s["g_out_conv"]], axis=1),
            jnp.concatenate([s["g_out_attn"], s["q_norm_g"], s["k_norm_g"], z64], axis=1),
            jnp.pad(s["conv_w"], ((0, 0), (0, 512))),
            jnp.pad(s["ffn_conv_b"], ((0, 0), (0, 3072 - D_FF))).reshape(3, 1024),
            jnp.pad(s["ffn_conv_w"], ((0, 0), (0, 3072 - D_FF))).reshape(9, 1024),
            jnp.zeros((_SMALL_ROWS - 20, 1024), F32)]
    return jnp.concatenate(rows, axis=0)


def _unpack_small(t):
    return dict(g_mix=t[0:1], g_ffn=t[1:2], g_ple=t[2:3], conv_b=t[3:4, 0:512], g_out_conv=t[3:4, 512:1024],
                g_out_attn=t[4:5, 0:512], q_norm_g=t[4:5, 512:576], k_norm_g=t[4:5, 576:640],
                conv_w=t[5:8, 0:512], ffn_conv_b=t[8:11].reshape(1, 3072)[:, :D_FF],
                ffn_conv_w=t[11:20].reshape(3, 3072)[:, :D_FF])


_BIG = ("w_in", "w_out", "w_gate", "w_up", "w_down", "w_ple_gate", "w_ple_proj")
_COL_SHARDED = dict(w_in=True, w_out=False, w_gate=False, w_up=False, w_down=False, w_ple_gate=False, w_ple_proj=True)
_TRANSPOSED = ("w_gate", "w_up")
_WEIGHTS = ("g_mix", "w_in", "conv_w", "conv_b", "q_norm_g", "k_norm_g", "g_out_conv", "g_out_attn", "w_out",
            "g_ffn", "w_gate", "w_up", "ffn_conv_w", "ffn_conv_b", "w_down", "g_ple", "w_ple_gate", "w_ple_proj")


def kernel(x, p, g_mix, w_in, conv_w, conv_b, q_norm_g, k_norm_g, g_out_conv, g_out_attn, w_out, g_ffn, w_gate, w_up, ffn_conv_w, ffn_conv_b, w_down, g_ple, w_ple_gate, w_ple_proj, loss_target, m_g_mix, m_w_in, m_conv_w, m_conv_b, m_q_norm_g, m_k_norm_g, m_g_out_conv, m_g_out_attn, m_w_out, m_g_ffn, m_w_gate, m_w_up, m_ffn_conv_w, m_ffn_conv_b, m_w_down, m_g_ple, m_w_ple_gate, m_w_ple_proj, v_g_mix, v_w_in, v_conv_w, v_conv_b, v_q_norm_g, v_k_norm_g, v_g_out_conv, v_g_out_attn, v_w_out, v_g_ffn, v_w_gate, v_w_up, v_ffn_conv_w, v_ffn_conv_b, v_w_down, v_g_ple, v_w_ple_gate, v_w_ple_proj):
    w = dict(g_mix=g_mix, w_in=w_in, conv_w=conv_w, conv_b=conv_b, q_norm_g=q_norm_g, k_norm_g=k_norm_g,
             g_out_conv=g_out_conv, g_out_attn=g_out_attn, w_out=w_out, g_ffn=g_ffn, w_gate=w_gate, w_up=w_up,
             ffn_conv_w=ffn_conv_w, ffn_conv_b=ffn_conv_b, w_down=w_down, g_ple=g_ple, w_ple_gate=w_ple_gate,
             w_ple_proj=w_ple_proj)
    m = dict(g_mix=m_g_mix, w_in=m_w_in, conv_w=m_conv_w, conv_b=m_conv_b, q_norm_g=m_q_norm_g, k_norm_g=m_k_norm_g,
             g_out_conv=m_g_out_conv, g_out_attn=m_g_out_attn, w_out=m_w_out, g_ffn=m_g_ffn, w_gate=m_w_gate,
             w_up=m_w_up, ffn_conv_w=m_ffn_conv_w, ffn_conv_b=m_ffn_conv_b, w_down=m_w_down, g_ple=m_g_ple,
             w_ple_gate=m_w_ple_gate, w_ple_proj=m_w_ple_proj)
    v = dict(g_mix=v_g_mix, w_in=v_w_in, conv_w=v_conv_w, conv_b=v_conv_b, q_norm_g=v_q_norm_g, k_norm_g=v_k_norm_g,
             g_out_conv=v_g_out_conv, g_out_attn=v_g_out_attn, w_out=v_w_out, g_ffn=v_g_ffn, w_gate=v_w_gate,
             w_up=v_w_up, ffn_conv_w=v_ffn_conv_w, ffn_conv_b=v_ffn_conv_b, w_down=v_w_down, g_ple=v_g_ple,
             w_ple_gate=v_w_ple_gate, w_ple_proj=v_w_ple_proj)
    mats = [k for k, a in w.items() if a.ndim == 3]
    w = {k: (a[0] if k in mats else a) for k, a in w.items()}
    m = {k: (a[0] if k in mats else a) for k, a in m.items()}
    v = {k: (a[0] if k in mats else a) for k, a in v.items()}
    for n in _TRANSPOSED:
        w[n], m[n], v[n] = w[n].T, m[n].T, v[n].T
    chip = 2 * lax.axis_index("x") + lax.axis_index("y")

    late = [w[n].astype(BF) for n in ("w_out", "w_gate", "w_up", "w_down", "w_ple_gate", "w_ple_proj")]
    pack = jnp.pad(jnp.concatenate([w["conv_w"], w["ffn_conv_w"]], axis=1), ((0, 5), (0, 1024 - 128 - D_FF_SHARD)))
    win4, pack4 = _gather_weights([w["w_in"].astype(BF)], pack)
    conv_w_full = pack4[:, 0:3, 0:128].transpose(1, 0, 2).reshape(3, CONV_W)
    fcw_full = pack4[:, 0:3, 128:128 + D_FF_SHARD].transpose(1, 0, 2).reshape(3, D_FF)
    wts = (win4, None, None, None, None, None, None, conv_w_full, fcw_full, w["g_mix"], w["conv_b"], w["q_norm_g"],
           w["k_norm_g"], w["g_out_conv"], w["g_out_attn"], w["g_ffn"], w["ffn_conv_b"], w["g_ple"])

    loss, grad_x, big, slots, small = _local_step(x[0], p[0, 0], loss_target[0], wts, late)
    loss = lax.psum(loss, ("x", "y", "c"))

    grads, deltas, new_m, new_v = {}, {}, {}, {}
    for name in _BIG:
        total = _finish_reduce(big[name], slots[name], _COL_SHARDED[name], "finish_" + name)
        grads[name], deltas[name], new_m[name], new_v[name] = _adamw_big(total, w[name], m[name], v[name],
                                                                         "adamw_" + name)
    tot = _unpack_small(_allreduce_small(_pack_small(small)))
    tot["conv_w"] = lax.dynamic_slice_in_dim(tot["conv_w"], chip * 128, 128, axis=1)
    tot["ffn_conv_w"] = lax.dynamic_slice_in_dim(tot["ffn_conv_w"], chip * D_FF_SHARD, D_FF_SHARD, axis=1)
    names = [n for n in _WEIGHTS if n not in _BIG]
    d_s, m_s, v_s = _adamw_small([w[n] for n in names], [tot[n] for n in names], [m[n] for n in names],
                                 [v[n] for n in names])
    for j, n in enumerate(names):
        grads[n], deltas[n], new_m[n], new_v[n] = tot[n], d_s[j], m_s[j], v_s[j]

    out = [loss, grad_x[None]]
    for group in (grads, deltas, new_m, new_v):
        for n in _TRANSPOSED:
            group[n] = group[n].T
        out += [group[n][None] if n in mats else group[n] for n in _WEIGHTS]
    return tuple(out)
```

```python
import jax
import jax.numpy as jnp
from jax import lax
from jax.experimental import pallas as pl
from jax.experimental.pallas import tpu as pltpu

D_MODEL = 1024
CONV_W = 512
N_HEADS = 8
HEAD_DIM = 64
ATTN_W = 512
D_FF = 2816
D_FF_SHARD = 704
FF_SLABS = ((0, 1408), (1408, 2816))
IN_SLAB = 768
PLE_DIM = 256
N_CHIPS = 4
QBLK = 128
DILATIONS = (1, 4, 16)
EPS = 1e-6
NEG = -1e30
MESH = pl.DeviceIdType.MESH

ADAM_LR = 0.001
ADAM_B1 = 0.9
ADAM_B2 = 0.999
ADAM_EPS = 1e-08
ADAM_WD = 0.01
ADAM_STEP = 10

BF = jnp.bfloat16
F32 = jnp.float32
MIB = 1024 * 1024


def _mm(a, b):
    return jnp.dot(a, b, preferred_element_type=F32)


def _mm_nt(a, b):
    return lax.dot_general(a, b, (((1,), (1,)), ((), ())), preferred_element_type=F32)


def _mm_tn(a, b):
    return lax.dot_general(a, b, (((0,), (0,)), ((), ())), preferred_element_type=F32)


def _rstd(a):
    return lax.rsqrt(jnp.mean(a * a, axis=-1, keepdims=True) + EPS)


def _norm_bwd(dy, xh, r, g):
    dxh = dy * g
    return r * (dxh - xh * jnp.mean(dxh * xh, axis=-1, keepdims=True))


def _colsum(a):
    return jnp.sum(a, axis=0, keepdims=True)


def _head_mean(a, gm_ref):
    return _mm(a.astype(BF), gm_ref[...])


def _shift_down(buf, k, tm):
    return pltpu.roll(buf, k, axis=0)[8:8 + tm]


def _shift_up(buf, k, tm):
    return pltpu.roll(buf, tm + 8 - k, axis=0)[0:tm]


def _params(vmem_mib, n_grid=1):
    return pltpu.CompilerParams(dimension_semantics=("arbitrary",) * n_grid, vmem_limit_bytes=vmem_mib * MIB)


def _const(shape):
    n = len(shape)
    return pl.BlockSpec(shape, lambda *_: (0,) * n, pipeline_mode=pl.Buffered(1))


def _rows(tm, width, rev_of=None):
    if rev_of is None:
        return pl.BlockSpec((tm, width), lambda i: (i, 0))
    return pl.BlockSpec((tm, width), lambda i: (rev_of - 1 - i, 0))


def _halo(tm, width, nt):
    return pl.BlockSpec((8, width), lambda i: (jnp.maximum((nt - 1 - i) * (tm // 8) - 1, 0), 0))


def _fwd_mix(x, g_mix, win4, conv_w, conv_b, g_oc, gm, gq8, gk8, tm):
    t = x.shape[0]
    nt = t // tm

    def body(x_ref, g_ref, w_ref, cw_ref, cb_ref, goc_ref, gm_ref, gq_ref, gk_ref,
             zbcx_ref, qkv_ref, ycn_ref, qkn_ref, ubuf):
        @pl.when(pl.program_id(0) == 0)
        def _():
            ubuf[0:8, :] = jnp.zeros((8, CONV_W), F32)

        xt = x_ref[...]
        h = ((xt * _rstd(xt)) * g_ref[...]).astype(BF)
        zbcx_ref[:, 0:IN_SLAB] = _mm(h, w_ref[0])
        zbcx_ref[:, IN_SLAB:2 * IN_SLAB] = _mm(h, w_ref[1])
        qkv_ref[:, 0:IN_SLAB] = _mm(h, w_ref[2])
        qkv_ref[:, IN_SLAB:2 * IN_SLAB] = _mm(h, w_ref[3])
        u = zbcx_ref[:, 512:1024] * zbcx_ref[:, 1024:1536]
        ubuf[8:8 + tm, :] = u
        ub = ubuf[...]
        cv = (cw_ref[0:1, :] * _shift_down(ub, 2, tm) + cw_ref[1:2, :] * _shift_down(ub, 1, tm)
              + cw_ref[2:3, :] * u + cb_ref[...])
        ubuf[0:8, :] = ubuf[tm:tm + 8, :]
        yc = zbcx_ref[:, 0:512] * cv
        ycn_ref[...] = ((yc * _rstd(yc)) * goc_ref[...]).astype(BF)
        zq = qkv_ref[:, 0:512]
        zk = qkv_ref[:, 512:1024]
        rq = lax.rsqrt(_head_mean(zq * zq, gm_ref) + EPS)
        rk = lax.rsqrt(_head_mean(zk * zk, gm_ref) + EPS)
        qkn_ref[:, 0:512] = ((zq * rq) * gq_ref[...]) * (HEAD_DIM ** -0.5)
        qkn_ref[:, 512:1024] = (zk * rk) * gk_ref[...]

    return pl.pallas_call(
        body, name="fwd_mix", grid=(nt,),
        in_specs=[_rows(tm, D_MODEL), _const((1, D_MODEL)), _const((N_CHIPS, D_MODEL, IN_SLAB)),
                  _const((3, CONV_W)), _const((1, CONV_W)), _const((1, CONV_W)), _const((ATTN_W, ATTN_W)),
                  _const((1, ATTN_W)), _const((1, ATTN_W))],
        out_specs=[_rows(tm, 1536), _rows(tm, 1536), _rows(tm, CONV_W), _rows(tm, 1024)],
        out_shape=[jax.ShapeDtypeStruct((t, 1536), F32), jax.ShapeDtypeStruct((t, 1536), F32),
                   jax.ShapeDtypeStruct((t, CONV_W), BF), jax.ShapeDtypeStruct((t, 1024), F32)],
        scratch_shapes=[pltpu.VMEM((tm + 8, CONV_W), F32)],
        compiler_params=_params(48),
    )(x, g_mix, win4, conv_w, conv_b, g_oc, gm, gq8, gk8)


def _place():
    x, y, c = lax.axis_index("x"), lax.axis_index("y"), lax.axis_index("c")
    return x, y, c


def _chip_peer(x, y, k):
    return x ^ (k >> 1), y ^ (k & 1)


def _piece_shape(grad_shape, col_sharded):
    kk, nn = grad_shape
    return (kk // 2, nn // N_CHIPS) if col_sharded else (kk // (2 * N_CHIPS), nn)


def _piece_window(col_sharded, r, cw, s, h):
    if col_sharded:
        return (pl.ds(pl.multiple_of(h * r, 16), r), pl.ds(pl.multiple_of(s * cw, 128), cw))
    return (pl.ds(pl.multiple_of((2 * s + h) * r, 16), r), slice(None))


def _scatter_copies(g_ref, slots_ref, send_sems, recv_sems, base, col_sharded):
    x, y, c = _place()
    r, cw = slots_ref.shape[1:]
    copies = []
    for k in range(1, 8):
        tx, ty, tc = x ^ (k >> 2), y ^ ((k >> 1) & 1), c ^ (k & 1)
        copies.append(pltpu.make_async_remote_copy(
            src_ref=g_ref.at[_piece_window(col_sharded, r, cw, 2 * tx + ty, tc)], dst_ref=slots_ref.at[k - 1],
            send_sem=send_sems.at[base + k - 1], recv_sem=recv_sems.at[base + k - 1],
            device_id=(tx, ty, tc), device_id_type=MESH))
    return copies


def _ride_scatter(first, last, riders, g_refs, slot_refs, send_sems, recv_sems):
    def all_copies():
        out = []
        for j, (_, col_sharded) in enumerate(riders):
            out += _scatter_copies(g_refs[j], slot_refs[j], send_sems, recv_sems, 7 * j, col_sharded)
        return out

    @pl.when(first)
    def _():
        for cp in all_copies():
            cp.start()

    @pl.when(last)
    def _():
        for cp in all_copies():
            cp.wait()


def _rider_specs(riders):
    any_spec = pl.BlockSpec(memory_space=pl.ANY)
    shapes = [jax.ShapeDtypeStruct((7,) + _piece_shape(g.shape, cs), BF) for g, cs in riders]
    sems = [pltpu.SemaphoreType.DMA((7 * len(riders),)), pltpu.SemaphoreType.DMA((7 * len(riders),))] if riders else []
    return [any_spec] * len(riders), shapes, sems


class _Gather:
    def __init__(self, ins, outs, send_sems, recv_sems, local_sems):
        self.ins, self.outs = ins, outs
        self.send_sems, self.recv_sems, self.local_sems = send_sems, recv_sems, local_sems
        self.x, self.y, self.c = _place()
        self.me = 2 * self.x + self.y

    def _push(self, src, dst, w, j, to):
        return pltpu.make_async_remote_copy(src_ref=src, dst_ref=dst, send_sem=self.send_sems.at[6 * w + j],
                                            recv_sem=self.recv_sems.at[6 * w + j], device_id=to, device_id_type=MESH)

    def _half(self, w, h):
        half = self.ins[w].shape[0] // 2
        return pl.ds(pl.multiple_of(h * half, 16), half)

    def _local(self, w):
        return pltpu.make_async_copy(self.ins[w], self.outs[w].at[self.me], self.local_sems.at[w])

    def _ici(self, w, k):
        px, py = _chip_peer(self.x, self.y, k)
        mine = self._half(w, self.c)
        return self._push(self.ins[w].at[mine], self.outs[w].at[self.me, mine], w, k - 1, (px, py, self.c))

    def _landed(self, w, k, h):
        return self.outs[w].at[self.me ^ k, self._half(w, h)]

    def _fwd(self, w, k):
        landed = self._landed(w, k, self.c)
        return self._push(landed, landed, w, 2 + k, (self.x, self.y, 1 - self.c))

    def start(self):
        for w in range(len(self.ins)):
            self._local(w).start()
            for k in (1, 2, 3):
                self._ici(w, k).start()

    def forward(self):
        for w in range(len(self.ins)):
            for k in (1, 2, 3):
                landed = self._landed(w, k, self.c)
                self._push(landed, landed, w, k - 1, (self.x, self.y, self.c)).wait_recv()
                self._fwd(w, k).start()

    def finish(self):
        for w in range(len(self.ins)):
            for k in (1, 2, 3):
                landed = self._landed(w, k, 1 - self.c)
                self._push(landed, landed, w, 2 + k, (self.x, self.y, self.c)).wait_recv()
            for k in (1, 2, 3):
                self._ici(w, k).wait_send()
                self._fwd(w, k).wait_send()
            self._local(w).wait()


def _alibi(h):
    return 2.0 ** (-(h + 1))


CHUNK = 2048


def _mask_table():
    slopes = jnp.asarray([_alibi(h) for h in range(N_HEADS)], F32)[:, None, None]
    step = jnp.arange(QBLK)[:, None] + QBLK - jnp.arange(2 * QBLK)[None, :]
    valid = (step >= 0) & (step <= QBLK)
    tab = jnp.stack([jnp.where(valid[None], -slopes * (step * d)[None].astype(F32), NEG) for d in DILATIONS])
    return tab.reshape(3, N_HEADS // 2, 2 * QBLK, 2 * QBLK)


def _attn_fwd(qkn, qkv, mb, late=()):
    t = qkn.shape[0]
    nc = t // CHUNK
    nl = len(late)

    def body(*refs):
        qc_ref, kp_ref, kc_ref, vp_ref, vc_ref, mb_ref = refs[0:6]
        o_ref, l_ref = refs[6 + nl:8 + nl]
        ob0, ob1, ob2, lb0, lb1, lb2 = refs[8 + 2 * nl:14 + 2 * nl]
        if nl:
            gather = _Gather(refs[6:6 + nl], refs[8 + nl:8 + 2 * nl], *refs[14 + 2 * nl:17 + 2 * nl])
            step = pl.program_id(0) * nc + pl.program_id(1)
            pl.when(step == 0)(gather.start)
            pl.when(step == 2 * nc)(gather.forward)
            pl.when(step == (N_HEADS // 2) * nc - 1)(gather.finish)
        first = pl.program_id(1) == 0
        lane = lax.broadcasted_iota(jnp.int32, (QBLK, 128), 1)
        lo_half = lane < HEAD_DIM
        kj = lax.broadcasted_iota(jnp.int32, (2 * QBLK, 2 * QBLK), 1)
        no_prev = first & (kj < QBLK)
        obs, lbs = (ob0, ob1, ob2), (lb0, lb1, lb2)

        def by_head(a):
            return jnp.where(lo_half, a, 0.0).astype(BF), jnp.where(lo_half, 0.0, a).astype(BF)

        for di, d in enumerate(DILATIONS):
            span = d * QBLK
            for r in range(d):
                tail = pl.ds(CHUNK - span + r, QBLK, stride=d)
                k_prev = kp_ref[tail, :].astype(BF)
                v_prev = by_head(vp_ref[tail, :])
                for b in range(CHUNK // span):
                    rows = pl.ds(r + span * b, QBLK, stride=d)
                    q0, q1 = by_head(qc_ref[rows, :])
                    k_cur = kc_ref[rows, :].astype(BF)
                    v_cur = by_head(vc_ref[rows, :])
                    s = _mm_nt(jnp.concatenate([q0, q1], axis=0), jnp.concatenate([k_prev, k_cur], axis=0))
                    s = s + mb_ref[di, 0]
                    if b == 0:
                        s = jnp.where(no_prev, NEG, s)
                    m = jnp.max(s, axis=-1, keepdims=True)
                    e = jnp.exp(s - m)
                    den = jnp.sum(e, axis=-1, keepdims=True)
                    eb = e.astype(BF)
                    o = _mm(jnp.concatenate([eb[0:QBLK], eb[QBLK:2 * QBLK]], axis=1),
                            jnp.concatenate([v_prev[0], v_cur[0], v_prev[1], v_cur[1]], axis=0))
                    inv = 1.0 / den
                    lse = m + jnp.log(den)
                    obs[di][rows, :] = o * jnp.where(lo_half, inv[0:QBLK], inv[QBLK:2 * QBLK])
                    lbs[di][rows, :] = jnp.where(lo_half, lse[0:QBLK], lse[QBLK:2 * QBLK])
                    k_prev, v_prev = k_cur, v_cur
        for c0 in range(0, CHUNK, 256):
            rs = slice(c0, c0 + 256)
            l0, l1, l2 = lb0[rs, :], lb1[rs, :], lb2[rs, :]
            mx = jnp.maximum(jnp.maximum(l0, l1), l2)
            w0, w1, w2 = jnp.exp(l0 - mx), jnp.exp(l1 - mx), jnp.exp(l2 - mx)
            tot = w0 + w1 + w2
            o_ref[rs, :] = (ob0[rs, :] * w0 + ob1[rs, :] * w1 + ob2[rs, :] * w2) / tot
            l_ref[rs, :] = mx + jnp.log(tot)

    def cur(col):
        return pl.BlockSpec((CHUNK, 128), lambda hp, n: (n, col + hp))

    def prv(col):
        return pl.BlockSpec((CHUNK, 128), lambda hp, n: (jnp.maximum(n - 1, 0), col + hp))

    out = pl.BlockSpec((CHUNK, 128), lambda hp, n: (n, hp))
    any_spec = pl.BlockSpec(memory_space=pl.ANY)
    sems = [pltpu.SemaphoreType.DMA((6 * nl,)), pltpu.SemaphoreType.DMA((6 * nl,)), pltpu.SemaphoreType.DMA((nl,))]
    res = pl.pallas_call(
        body, name="attn_fwd", grid=(N_HEADS // 2, nc),
        in_specs=[cur(0), prv(4), cur(4), prv(8), cur(8),
                  pl.BlockSpec((3, 1, 2 * QBLK, 2 * QBLK), lambda hp, n: (0, hp, 0, 0))] + [any_spec] * nl,
        out_specs=[out, out] + [any_spec] * nl,
        out_shape=[jax.ShapeDtypeStruct((t, ATTN_W), F32)] * 2
        + [jax.ShapeDtypeStruct((N_CHIPS,) + w.shape, w.dtype) for w in late],
        scratch_shapes=[pltpu.VMEM((CHUNK, 128), F32)] * 6 + (sems if nl else []),
        compiler_params=_params(48, 2),
    )(qkn, qkn, qkn, qkv, qkv, mb, *late)
    return res[0], res[1], list(res[2:])


def _attn_bwd(qkn, qkv, o, lse, do, mb, riders=()):
    t = qkn.shape[0]
    nc = t // CHUNK
    nr = len(riders)

    def body(*refs):
        (qc_ref, qn_ref, kp_ref, kc_ref, vp_ref, vc_ref, oc_ref, on_ref, lc_ref, ln_ref, dc_ref, dn_ref,
         mb_ref) = refs[0:13]
        dq_ref, dk_ref, dv_ref = refs[13 + nr:16 + nr]
        if nr:
            step = pl.program_id(0) * nc + pl.program_id(1)
            _ride_scatter(step == 0, step == (N_HEADS // 2) * nc - 1, riders, refs[13:13 + nr],
                          refs[16 + nr:16 + 2 * nr], *refs[16 + 2 * nr:18 + 2 * nr])
        first = pl.program_id(1) == 0
        last = pl.program_id(1) == nc - 1
        lane = lax.broadcasted_iota(jnp.int32, (QBLK, 128), 1)
        lo_half = lane < HEAD_DIM
        kj = lax.broadcasted_iota(jnp.int32, (2 * QBLK, 2 * QBLK), 1)
        no_prev = first & (kj < QBLK)

        def by_head(a):
            return jnp.where(lo_half, a, 0.0).astype(BF), jnp.where(lo_half, 0.0, a).astype(BF)

        def query_side(q_ref, d_ref, o_ref_, l_ref_, rows):
            dvals = d_ref[rows, :]
            dd = dvals * o_ref_[rows, :]
            lv = l_ref_[rows, :]
            d0 = jnp.sum(jnp.where(lo_half, dd, 0.0), axis=-1, keepdims=True)
            d1 = jnp.sum(jnp.where(lo_half, 0.0, dd), axis=-1, keepdims=True)
            l0 = jnp.max(jnp.where(lo_half, lv, NEG), axis=-1, keepdims=True)
            l1 = jnp.max(jnp.where(lo_half, NEG, lv), axis=-1, keepdims=True)
            return (jnp.concatenate(by_head(q_ref[rows, :]), axis=0), jnp.concatenate(by_head(dvals), axis=0),
                    jnp.concatenate([l0, l1], axis=0), jnp.concatenate([d0, d1], axis=0))

        def tile(qs, dos, lcol, dcol, keys, vals, bias, dead):
            s = _mm_nt(qs, keys) + bias
            if dead is not None:
                s = jnp.where(dead, NEG, s)
            p = jnp.exp(s - lcol)
            ds = p * (_mm_nt(dos, vals) - dcol)
            return p.astype(BF), ds.astype(BF)

        def put(ref, di, rows, val):
            if di == 0:
                ref[rows, :] = val
            else:
                ref[rows, :] = ref[rows, :] + val

        for di, d in enumerate(DILATIONS):
            span = d * QBLK
            nbk = CHUNK // span
            for r in range(d):
                tail = pl.ds(CHUNK - span + r, QBLK, stride=d)
                k_prev = kp_ref[tail, :]
                kb_prev, km_prev = k_prev.astype(BF), by_head(k_prev)
                vb_prev = vp_ref[tail, :].astype(BF)
                rows_prev, dk_part, dv_part = None, None, None
                for b in range(nbk):
                    rows = pl.ds(r + span * b, QBLK, stride=d)
                    qs, dos, lcol, dcol = query_side(qc_ref, dc_ref, oc_ref, lc_ref, rows)
                    k_cur = kc_ref[rows, :]
                    kb_cur, km_cur = k_cur.astype(BF), by_head(k_cur)
                    vb_cur = vc_ref[rows, :].astype(BF)
                    p, ds = tile(qs, dos, lcol, dcol, jnp.concatenate([kb_prev, kb_cur], axis=0),
                                 jnp.concatenate([vb_prev, vb_cur], axis=0), mb_ref[di, 0],
                                 no_prev if b == 0 else None)
                    put(dq_ref, di, rows,
                        _mm(jnp.concatenate([ds[0:QBLK], ds[QBLK:2 * QBLK]], axis=1),
                            jnp.concatenate([km_prev[0], km_cur[0], km_prev[1], km_cur[1]], axis=0)))
                    dk2 = _mm_tn(ds, qs)
                    dv2 = _mm_tn(p, dos)
                    if b > 0:
                        put(dk_ref, di, rows_prev, dk_part + dk2[0:QBLK])
                        put(dv_ref, di, rows_prev, dv_part + dv2[0:QBLK])
                    rows_prev, dk_part, dv_part = rows, dk2[QBLK:2 * QBLK], dv2[QBLK:2 * QBLK]
                    kb_prev, km_prev, vb_prev = kb_cur, km_cur, vb_cur
                qs, dos, lcol, dcol = query_side(qn_ref, dn_ref, on_ref, ln_ref, pl.ds(r, QBLK, stride=d))
                p, ds = tile(qs, dos, lcol, dcol, kb_prev, vb_prev, mb_ref[di, 0, :, 0:QBLK], last)
                put(dk_ref, di, rows_prev, dk_part + _mm_tn(ds, qs))
                put(dv_ref, di, rows_prev, dv_part + _mm_tn(p, dos))

    def at(shift, col):
        return pl.BlockSpec((CHUNK, 128), lambda hp, n: (jnp.clip(n + shift, 0, nc - 1), col + hp))

    out = pl.BlockSpec((CHUNK, 128), lambda hp, n: (n, hp))
    r_in, r_out, r_sems = _rider_specs(riders)
    res = pl.pallas_call(
        body, name="attn_bwd", grid=(N_HEADS // 2, nc),
        in_specs=[at(0, 0), at(1, 0), at(-1, 4), at(0, 4), at(-1, 8), at(0, 8),
                  at(0, 0), at(1, 0), at(0, 0), at(1, 0), at(0, 0), at(1, 0),
                  pl.BlockSpec((3, 1, 2 * QBLK, 2 * QBLK), lambda hp, n: (0, hp, 0, 0))] + r_in,
        out_specs=[out, out, out] + r_in,
        out_shape=[jax.ShapeDtypeStruct((t, ATTN_W), F32)] * 3 + r_out,
        scratch_shapes=r_sems,
        compiler_params=_params(56, 2),
    )(qkn, qkn, qkn, qkn, qkv, qkv, o, o, lse, lse, do, do, mb, *[g for g, _ in riders])
    return res[0], res[1], res[2], list(res[3:])


def _fwd_ffn(x, ycn, ya, wout, wg4, wu4, g_oa, g_ffn, fcw, fcb, tm, late=()):
    t = x.shape[0]
    nt = t // tm
    nl = len(late)

    def body(*refs):
        x_ref, ycn_ref, ya_ref, wout_ref, wg_ref, wu_ref, goa_ref, gffn_ref, fcw_ref, fcb_ref = refs[0:10]
        x1_ref, gp_ref, up_ref, gate_ref, act_ref, ycat_ref, h2_ref = refs[10 + nl:17 + nl]
        cbuf = refs[17 + 2 * nl]
        if nl:
            gather = _Gather(refs[10:10 + nl], refs[17 + nl:17 + 2 * nl], *refs[18 + 2 * nl:21 + 2 * nl])
            pl.when(pl.program_id(0) == 0)(gather.start)
            pl.when(pl.program_id(0) == nt // 2)(gather.forward)
            pl.when(pl.program_id(0) == nt - 1)(gather.finish)

        @pl.when(pl.program_id(0) == 0)
        def _():
            cbuf[0:8, :] = jnp.zeros((8, D_FF), F32)

        yat = ya_ref[...]
        yan = ((yat * _rstd(yat)) * goa_ref[...]).astype(BF)
        ycn = ycn_ref[...]
        ycat_ref[:, 0:CONV_W] = ycn
        ycat_ref[:, CONV_W:D_MODEL] = yan
        x1 = x_ref[...] + _mm(ycn, wout_ref[0:CONV_W, :]) + _mm(yan, wout_ref[CONV_W:D_MODEL, :])
        x1_ref[...] = x1
        h2 = ((x1 * _rstd(x1)) * gffn_ref[...]).astype(BF)
        h2_ref[...] = h2
        for lo, hi in FF_SLABS:
            gps = _mm_nt(h2, wg_ref[lo:hi, :])
            ups = _mm_nt(h2, wu_ref[lo:hi, :])
            gp_ref[:, lo:hi] = gps.astype(BF)
            up_ref[:, lo:hi] = ups.astype(BF)
            cbuf[8:8 + tm, lo:hi] = gps
            cb = cbuf[:, lo:hi]
            gate = (fcw_ref[0:1, lo:hi] * _shift_down(cb, 2, tm) + fcw_ref[1:2, lo:hi] * _shift_down(cb, 1, tm)
                    + fcw_ref[2:3, lo:hi] * gps + fcb_ref[:, lo:hi])
            gate_ref[:, lo:hi] = gate.astype(BF)
            act_ref[:, lo:hi] = ((gate * jax.nn.sigmoid(gate)) * ups).astype(BF)
        cbuf[0:8, :] = cbuf[tm:tm + 8, :]

    any_spec = pl.BlockSpec(memory_space=pl.ANY)
    sems = [pltpu.SemaphoreType.DMA((6 * nl,)), pltpu.SemaphoreType.DMA((6 * nl,)), pltpu.SemaphoreType.DMA((nl,))]
    res = pl.pallas_call(
        body, name="fwd_ffn", grid=(nt,),
        in_specs=[_rows(tm, D_MODEL), _rows(tm, CONV_W), _rows(tm, ATTN_W), _const((D_MODEL, D_MODEL)),
                  _const((D_FF, D_MODEL)), _const((D_FF, D_MODEL)),
                  _const((1, ATTN_W)), _const((1, D_MODEL)), _const((3, D_FF)), _const((1, D_FF))]
        + [any_spec] * nl,
        out_specs=[_rows(tm, D_MODEL), _rows(tm, D_FF), _rows(tm, D_FF), _rows(tm, D_FF), _rows(tm, D_FF),
                   _rows(tm, D_MODEL), _rows(tm, D_MODEL)] + [any_spec] * nl,
        out_shape=[jax.ShapeDtypeStruct((t, D_MODEL), F32), jax.ShapeDtypeStruct((t, D_FF), BF),
                   jax.ShapeDtypeStruct((t, D_FF), BF), jax.ShapeDtypeStruct((t, D_FF), BF),
                   jax.ShapeDtypeStruct((t, D_FF), BF),
                   jax.ShapeDtypeStruct((t, D_MODEL), BF), jax.ShapeDtypeStruct((t, D_MODEL), BF)]
        + [jax.ShapeDtypeStruct((N_CHIPS,) + w.shape, w.dtype) for w in late],
        scratch_shapes=[pltpu.VMEM((tm + 8, D_FF), F32)] + (sems if nl else []),
        compiler_params=_params(56),
    )(x, ycn, ya, wout, wg4, wu4, g_oa, g_ffn, fcw, fcb, *late)
    return tuple(res[0:7]) + (list(res[7:]),)


def _fwd_tail(x1, act, p, target, wd4, wpg, wpp4, g_ple, tm):
    t = x1.shape[0]
    nt = t // tm

    def body(x1_ref, act_ref, p_ref, tgt_ref, wd_ref, wpg_ref, wpp_ref, g_ref,
             dx2_ref, h3_ref, ds_ref, dpp_ref, dg_ref, loss_ref, lacc):
        i = pl.program_id(0)

        @pl.when(i == 0)
        def _():
            dg_ref[...] = jnp.zeros_like(dg_ref)
            lacc[...] = jnp.zeros_like(lacc)

        x2 = x1_ref[...]
        for lo, hi in FF_SLABS:
            x2 = x2 + _mm(act_ref[:, lo:hi], wd_ref[lo:hi, :])
        r3 = _rstd(x2)
        xh = x2 * r3
        h3 = (xh * g_ref[...]).astype(BF)
        h3_ref[...] = h3
        sg = jax.nn.sigmoid(_mm(h3, wpg_ref[...]))
        pb = p_ref[...].astype(BF)
        pp = jnp.concatenate([_mm(pb, wpp_ref[s]) for s in range(N_CHIPS)], axis=1)
        err = (x2 + sg * pp) - tgt_ref[...]
        lacc[...] += _colsum(err * err)
        dx3 = err * (1.0 / D_MODEL)
        dpp_ref[...] = (dx3 * sg).astype(BF)
        dsb = ((dx3 * pp) * (sg * (1.0 - sg))).astype(BF)
        ds_ref[...] = dsb
        dh3 = _mm_nt(dsb, wpg_ref[...])
        dg_ref[...] += _colsum(dh3 * xh)
        dx2_ref[...] = dx3 + _norm_bwd(dh3, xh, r3, g_ref[...])

        @pl.when(i == nt - 1)
        def _():
            loss_ref[...] = jnp.full((1, 128), jnp.sum(lacc[...]) * (0.5 / D_MODEL), F32)

    return pl.pallas_call(
        body, name="fwd_tail", grid=(nt,),
        in_specs=[_rows(tm, D_MODEL), _rows(tm, D_FF), _rows(tm, PLE_DIM), _rows(tm, D_MODEL),
                  _const((D_FF, D_MODEL)), _const((D_MODEL, D_MODEL)),
                  _const((N_CHIPS, PLE_DIM, PLE_DIM)), _const((1, D_MODEL))],
        out_specs=[_rows(tm, D_MODEL), _rows(tm, D_MODEL), _rows(tm, D_MODEL), _rows(tm, D_MODEL),
                   pl.BlockSpec((1, D_MODEL), lambda i: (0, 0)), pl.BlockSpec((1, 128), lambda i: (0, 0))],
        out_shape=[jax.ShapeDtypeStruct((t, D_MODEL), F32), jax.ShapeDtypeStruct((t, D_MODEL), BF),
                   jax.ShapeDtypeStruct((t, D_MODEL), BF), jax.ShapeDtypeStruct((t, D_MODEL), BF),
                   jax.ShapeDtypeStruct((1, D_MODEL), F32), jax.ShapeDtypeStruct((1, 128), F32)],
        scratch_shapes=[pltpu.VMEM((1, D_MODEL), F32)],
        compiler_params=_params(48),
    )(x1, act, p, target, wd4, wpg, wpp4, g_ple)


def _bwd_ffn_a(dx2, gate, gp, up, wd4, fcw, tm, riders=()):
    t = dx2.shape[0]
    nt = t // tm
    nr = len(riders)

    def body(*refs):
        dx2_ref, gate_ref, gp_ref, up_ref, wd_ref, fcw_ref = refs[0:6]
        dgp_ref, dup_ref, dfcw_ref, dfcb_ref = refs[6 + nr:10 + nr]
        dbuf = refs[10 + 2 * nr]
        i = pl.program_id(0)
        if nr:
            _ride_scatter(i == 0, i == nt - 1, riders, refs[6:6 + nr], refs[10 + nr:10 + 2 * nr],
                          *refs[11 + 2 * nr:13 + 2 * nr])

        @pl.when(i == 0)
        def _():
            dbuf[tm:tm + 8, :] = jnp.zeros((8, D_FF), F32)
            dfcw_ref[...] = jnp.zeros_like(dfcw_ref)
            dfcb_ref[...] = jnp.zeros_like(dfcb_ref)

        dx2b = dx2_ref[...].astype(BF)
        for lo, hi in FF_SLABS:
            gate = gate_ref[:, lo:hi].astype(F32)
            gps = gp_ref[:, lo:hi].astype(F32)
            w0, w1, w2 = fcw_ref[0:1, lo:hi], fcw_ref[1:2, lo:hi], fcw_ref[2:3, lo:hi]
            sg = jax.nn.sigmoid(gate)
            dact = _mm_nt(dx2b, wd_ref[lo:hi, :])
            dup_ref[:, lo:hi] = (dact * (gate * sg)).astype(BF)
            dgate = (dact * up_ref[:, lo:hi].astype(F32)) * (sg * (1.0 + gate * (1.0 - sg)))
            dbuf[0:tm, lo:hi] = dgate
            db = dbuf[:, lo:hi]
            d1 = _shift_up(db, 1, tm)
            d2 = _shift_up(db, 2, tm)
            dfcb_ref[:, lo:hi] += _colsum(dgate)
            dfcw_ref[0:1, lo:hi] += _colsum(d2 * gps)
            dfcw_ref[1:2, lo:hi] += _colsum(d1 * gps)
            dfcw_ref[2:3, lo:hi] += _colsum(dgate * gps)
            dgp_ref[:, lo:hi] = (w2 * dgate + w1 * d1 + w0 * d2).astype(BF)
        dbuf[tm:tm + 8, :] = dbuf[0:8, :]

    r_in, r_out, r_sems = _rider_specs(riders)
    res = pl.pallas_call(
        body, name="bwd_ffn_a", grid=(nt,),
        in_specs=[_rows(tm, D_MODEL, nt), _rows(tm, D_FF, nt), _rows(tm, D_FF, nt), _rows(tm, D_FF, nt),
                  _const((D_FF, D_MODEL)), _const((3, D_FF))] + r_in,
        out_specs=[_rows(tm, D_FF, nt), _rows(tm, D_FF, nt),
                   pl.BlockSpec((3, D_FF), lambda i: (0, 0)), pl.BlockSpec((1, D_FF), lambda i: (0, 0))] + r_in,
        out_shape=[jax.ShapeDtypeStruct((t, D_FF), BF), jax.ShapeDtypeStruct((t, D_FF), BF),
                   jax.ShapeDtypeStruct((3, D_FF), F32), jax.ShapeDtypeStruct((1, D_FF), F32)] + r_out,
        scratch_shapes=[pltpu.VMEM((tm + 8, D_FF), F32)] + r_sems,
        compiler_params=_params(56),
    )(dx2, gate, gp, up, wd4, fcw, *[g for g, _ in riders])
    return res[0], res[1], res[2], res[3], list(res[4:])


def _bwd_ffn_b(dgp, dup, dx2, x1, ya, wg4, wu4, wout, g_ffn, g_oa, tm):
    t = dx2.shape[0]
    nt = t // tm

    def body(dgp_ref, dup_ref, dx2_ref, x1_ref, ya_ref, wg_ref, wu_ref, wout_ref, gffn_ref, goa_ref,
             dx1_ref, dycn_ref, dya_ref, dgffn_ref, dgoa_ref):
        @pl.when(pl.program_id(0) == 0)
        def _():
            dgffn_ref[...] = jnp.zeros_like(dgffn_ref)
            dgoa_ref[...] = jnp.zeros_like(dgoa_ref)

        dh2 = jnp.zeros((tm, D_MODEL), F32)
        for lo, hi in FF_SLABS:
            dh2 = dh2 + _mm(dgp_ref[:, lo:hi], wg_ref[lo:hi, :]) + _mm(dup_ref[:, lo:hi], wu_ref[lo:hi, :])
        x1 = x1_ref[...]
        r2 = _rstd(x1)
        xh = x1 * r2
        dgffn_ref[...] += _colsum(dh2 * xh)
        dx1 = dx2_ref[...] + _norm_bwd(dh2, xh, r2, gffn_ref[...])
        dx1_ref[...] = dx1
        dy = _mm_nt(dx1.astype(BF), wout_ref[...])
        dycn_ref[...] = dy[:, 0:CONV_W]
        dyan = dy[:, CONV_W:D_MODEL]
        yat = ya_ref[...]
        ra = _rstd(yat)
        yah = yat * ra
        dgoa_ref[...] += _colsum(dyan * yah)
        dya_ref[...] = _norm_bwd(dyan, yah, ra, goa_ref[...])

    return pl.pallas_call(
        body, name="bwd_ffn_b", grid=(nt,),
        in_specs=[_rows(tm, D_FF), _rows(tm, D_FF), _rows(tm, D_MODEL), _rows(tm, D_MODEL),
                  _rows(tm, ATTN_W), _const((D_FF, D_MODEL)), _const((D_FF, D_MODEL)),
                  _const((D_MODEL, D_MODEL)), _const((1, D_MODEL)), _const((1, ATTN_W))],
        out_specs=[_rows(tm, D_MODEL), _rows(tm, CONV_W), _rows(tm, ATTN_W),
                   pl.BlockSpec((1, D_MODEL), lambda i: (0, 0)), pl.BlockSpec((1, ATTN_W), lambda i: (0, 0))],
        out_shape=[jax.ShapeDtypeStruct((t, D_MODEL), F32), jax.ShapeDtypeStruct((t, CONV_W), F32),
                   jax.ShapeDtypeStruct((t, ATTN_W), F32),
                   jax.ShapeDtypeStruct((1, D_MODEL), F32), jax.ShapeDtypeStruct((1, ATTN_W), F32)],
        compiler_params=_params(48),
    )(dgp, dup, dx2, x1, ya, wg4, wu4, wout, g_ffn, g_oa)


def _bwd_mix(x, zbcx, qkv, dycn, dq, dk, dv, conv_w, conv_b, g_oc, g_mix, gm, gq8, gk8, tm):
    t = x.shape[0]
    nt = t // tm

    def body(x_ref, z_ref, zh_ref, qkv_ref, dycn_ref, dq_ref, dk_ref, dv_ref, cw_ref, cb_ref,
             goc_ref, g_ref, gm_ref, gq_ref, gk_ref,
             h1_ref, dz_ref, dcw_ref, dcb_ref, dgoc_ref, dgq_ref, dgk_ref, ubuf, dbuf):
        i = pl.program_id(0)

        @pl.when(i == 0)
        def _():
            dbuf[tm:tm + 8, :] = jnp.zeros((8, CONV_W), F32)
            dcw_ref[...] = jnp.zeros_like(dcw_ref)
            dcb_ref[...] = jnp.zeros_like(dcb_ref)
            dgoc_ref[...] = jnp.zeros_like(dgoc_ref)
            dgq_ref[...] = jnp.zeros_like(dgq_ref)
            dgk_ref[...] = jnp.zeros_like(dgk_ref)

        not_first_tile = i < nt - 1
        zb = z_ref[:, 0:512]
        zc = z_ref[:, 512:1024]
        zx = z_ref[:, 1024:1536]
        u = zc * zx
        ubuf[0:8, :] = jnp.where(not_first_tile, zh_ref[:, 512:1024] * zh_ref[:, 1024:1536], 0.0)
        ubuf[8:8 + tm, :] = u
        ub = ubuf[...]
        u1 = _shift_down(ub, 1, tm)
        u2 = _shift_down(ub, 2, tm)
        w0, w1, w2 = cw_ref[0:1, :], cw_ref[1:2, :], cw_ref[2:3, :]
        cv = w0 * u2 + w1 * u1 + w2 * u + cb_ref[...]
        yc = zb * cv
        rc = _rstd(yc)
        ych = yc * rc
        dycn = dycn_ref[...]
        dgoc_ref[...] += _colsum(dycn * ych)
        dyc = _norm_bwd(dycn, ych, rc, goc_ref[...])
        dcv = dyc * zb
        dcb_ref[...] += _colsum(dcv)
        dcw_ref[0:1, :] += _colsum(dcv * u2)
        dcw_ref[1:2, :] += _colsum(dcv * u1)
        dcw_ref[2:3, :] += _colsum(dcv * u)
        dbuf[0:tm, :] = dcv
        db = dbuf[...]
        du = w2 * dcv + w1 * _shift_up(db, 1, tm) + w0 * _shift_up(db, 2, tm)
        dbuf[tm:tm + 8, :] = dbuf[0:8, :]
        dz_ref[:, 0:512] = (dyc * cv).astype(BF)
        dz_ref[:, 512:1024] = (du * zx).astype(BF)
        dz_ref[:, 1024:1536] = (du * zc).astype(BF)
        for z0, d_ref, gg_ref, acc_ref, sc in ((0, dq_ref, gq_ref, dgq_ref, HEAD_DIM ** -0.5),
                                               (512, dk_ref, gk_ref, dgk_ref, 1.0)):
            z = qkv_ref[:, z0:z0 + 512]
            rr = lax.rsqrt(_head_mean(z * z, gm_ref) + EPS)
            zh = z * rr
            dn = d_ref[...] * sc
            acc_ref[...] += _colsum(dn * zh)
            dzh = dn * gg_ref[...]
            dz_ref[:, 1536 + z0:1536 + z0 + 512] = (rr * (dzh - zh * _head_mean(dzh * zh, gm_ref))).astype(BF)
        dz_ref[:, 2560:3072] = dv_ref[...].astype(BF)
        xt = x_ref[...]
        h1_ref[...] = ((xt * _rstd(xt)) * g_ref[...]).astype(BF)

    def acc(width, rows=1):
        return pl.BlockSpec((rows, width), lambda i: (0, 0))

    return pl.pallas_call(
        body, name="bwd_mix", grid=(nt,),
        in_specs=[_rows(tm, D_MODEL, nt), _rows(tm, 1536, nt), _halo(tm, 1536, nt),
                  _rows(tm, 1536, nt), _rows(tm, CONV_W, nt), _rows(tm, ATTN_W, nt), _rows(tm, ATTN_W, nt),
                  _rows(tm, ATTN_W, nt),
                  _const((3, CONV_W)), _const((1, CONV_W)), _const((1, CONV_W)), _const((1, D_MODEL)),
                  _const((ATTN_W, ATTN_W)), _const((1, ATTN_W)), _const((1, ATTN_W))],
        out_specs=[_rows(tm, D_MODEL, nt), _rows(tm, 3072, nt),
                   acc(CONV_W, 3), acc(CONV_W), acc(CONV_W), acc(ATTN_W), acc(ATTN_W)],
        out_shape=[jax.ShapeDtypeStruct((t, D_MODEL), BF),
                   jax.ShapeDtypeStruct((t, 3072), BF), jax.ShapeDtypeStruct((3, CONV_W), F32),
                   jax.ShapeDtypeStruct((1, CONV_W), F32), jax.ShapeDtypeStruct((1, CONV_W), F32),
                   jax.ShapeDtypeStruct((1, ATTN_W), F32), jax.ShapeDtypeStruct((1, ATTN_W), F32)],
        scratch_shapes=[pltpu.VMEM((tm + 8, CONV_W), F32), pltpu.VMEM((tm + 8, CONV_W), F32)],
        compiler_params=_params(56),
    )(x, zbcx, zbcx, qkv, dycn, dq, dk, dv, conv_w, conv_b, g_oc, g_mix, gm, gq8, gk8)


def _bwd_in(x, dx1, dz, win4, g_mix, tm, riders=()):
    t = x.shape[0]
    nt = t // tm
    nr = len(riders)

    def body(*refs):
        x_ref, dx1_ref, dz_ref, w_ref, g_ref = refs[0:5]
        gx_ref, dg_ref = refs[5 + nr:7 + nr]
        i = pl.program_id(0)
        if nr:
            _ride_scatter(i == 0, i == nt - 1, riders, refs[5:5 + nr], refs[7 + nr:7 + 2 * nr],
                          *refs[7 + 2 * nr:9 + 2 * nr])

        @pl.when(i == 0)
        def _():
            dg_ref[...] = jnp.zeros_like(dg_ref)

        dh1 = jnp.zeros((tm, D_MODEL), F32)
        for s in range(N_CHIPS):
            dh1 = dh1 + _mm_nt(dz_ref[:, s * IN_SLAB:(s + 1) * IN_SLAB], w_ref[s])
        xt = x_ref[...]
        r1 = _rstd(xt)
        xh = xt * r1
        dg_ref[...] += _colsum(dh1 * xh)
        gx_ref[...] = dx1_ref[...] + _norm_bwd(dh1, xh, r1, g_ref[...])

    r_in, r_out, r_sems = _rider_specs(riders)
    res = pl.pallas_call(
        body, name="bwd_in", grid=(nt,),
        in_specs=[_rows(tm, D_MODEL), _rows(tm, D_MODEL), _rows(tm, 3072), _const((N_CHIPS, D_MODEL, IN_SLAB)),
                  _const((1, D_MODEL))] + r_in,
        out_specs=[_rows(tm, D_MODEL), pl.BlockSpec((1, D_MODEL), lambda i: (0, 0))] + r_in,
        out_shape=[jax.ShapeDtypeStruct((t, D_MODEL), F32), jax.ShapeDtypeStruct((1, D_MODEL), F32)] + r_out,
        scratch_shapes=r_sems,
        compiler_params=_params(48),
    )(x, dx1, dz, win4, g_mix, *[g for g, _ in riders])
    return res[0], res[1], list(res[2:])


def _wgrad(a, b, tn, tt, name):
    t, k = a.shape
    n = b.shape[1]
    nt = t // tt

    def body(a_ref, b_ref, o_ref, ob_ref):
        @pl.when(pl.program_id(1) == 0)
        def _():
            o_ref[...] = jnp.zeros_like(o_ref)

        o_ref[...] += _mm_tn(a_ref[...].astype(BF), b_ref[...].astype(BF))

        @pl.when(pl.program_id(1) == nt - 1)
        def _():
            ob_ref[...] = o_ref[...].astype(BF)

    spec = pl.BlockSpec((k, tn), lambda j, i: (0, j))
    return pl.pallas_call(
        body, name=name, grid=(n // tn, nt),
        in_specs=[pl.BlockSpec((tt, k), lambda j, i: (i, 0)), pl.BlockSpec((tt, tn), lambda j, i: (i, j))],
        out_specs=[spec, spec],
        out_shape=[jax.ShapeDtypeStruct((k, n), F32), jax.ShapeDtypeStruct((k, n), BF)],
        compiler_params=_params(48, 2),
    )(a, b)


def _gather_weights(shards, pack):
    nw = len(shards)

    def body(*refs):
        ins = refs[:nw]
        pack_ref = refs[nw]
        outs = refs[nw + 1:2 * nw + 1]
        pack_out = refs[2 * nw + 1]
        send_sems, recv_sems, local_sems = refs[2 * nw + 2:]
        x, y, c = _place()
        me = 2 * x + y
        local, remote = [], []

        def sem(w, j):
            return w * 6 + j

        def push(src, dst, w, j, to):
            return pltpu.make_async_remote_copy(src_ref=src, dst_ref=dst, send_sem=send_sems.at[sem(w, j)],
                                                recv_sem=recv_sems.at[sem(w, j)], device_id=to, device_id_type=MESH)

        def half_rows(w, h):
            half = ins[w].shape[0] // 2
            return pl.ds(pl.multiple_of(h * half, 16), half)

        for w in range(nw):
            local.append(pltpu.make_async_copy(ins[w], outs[w].at[me], local_sems.at[w]))
            for k in (1, 2, 3):
                px, py = _chip_peer(x, y, k)
                mine = half_rows(w, c)
                remote.append(push(ins[w].at[mine], outs[w].at[me, mine], w, k - 1, (px, py, c)))
        local.append(pltpu.make_async_copy(pack_ref, pack_out.at[me], local_sems.at[nw]))
        for k in (1, 2, 3):
            px, py = _chip_peer(x, y, k)
            remote.append(push(pack_ref, pack_out.at[me], nw, k - 1, (px, py, c)))
        for cp in local + remote:
            cp.start()
        for w in range(nw):
            for k in (1, 2, 3):
                landed = outs[w].at[me ^ k, half_rows(w, c)]
                push(landed, landed, w, k - 1, (x, y, c)).wait_recv()
                fw = push(landed, landed, w, 2 + k, (x, y, 1 - c))
                fw.start()
                remote.append(fw)
        for k in (1, 2, 3):
            landed = pack_out.at[me ^ k]
            push(landed, landed, nw, k - 1, (x, y, c)).wait_recv()
        for w in range(nw):
            for k in (1, 2, 3):
                landed = outs[w].at[me ^ k, half_rows(w, 1 - c)]
                push(landed, landed, w, 2 + k, (x, y, c)).wait_recv()
        for cp in remote:
            cp.wait_send()
        for cp in local:
            cp.wait()

    any_spec = pl.BlockSpec(memory_space=pl.ANY)
    out_shape = [jax.ShapeDtypeStruct((N_CHIPS,) + s.shape, s.dtype) for s in shards]
    out_shape.append(jax.ShapeDtypeStruct((N_CHIPS,) + pack.shape, pack.dtype))
    return pl.pallas_call(
        body, name="gather_weights",
        in_specs=[any_spec] * (nw + 1), out_specs=[any_spec] * (nw + 1), out_shape=out_shape,
        scratch_shapes=[pltpu.SemaphoreType.DMA(((nw + 1) * 6,)), pltpu.SemaphoreType.DMA(((nw + 1) * 6,)),
                        pltpu.SemaphoreType.DMA((nw + 1,))],
    )(*shards, pack)


def _adamw(w, g, m, v):
    m = ADAM_B1 * m + (1.0 - ADAM_B1) * g
    v = ADAM_B2 * v + (1.0 - ADAM_B2) * (g * g)
    m_hat = m / (1.0 - ADAM_B1 ** ADAM_STEP)
    v_hat = v / (1.0 - ADAM_B2 ** ADAM_STEP)
    delta = -ADAM_LR * (m_hat / (jnp.sqrt(v_hat) + ADAM_EPS) + ADAM_WD * w)
    return delta, m, v


def _finish_reduce(grad, slots, col_sharded, name):
    r, cw = _piece_shape(grad.shape, col_sharded)
    chunk = 32
    assert r % chunk == 0

    def body(g_hbm, slots_ref, full, own, lsem, c_send, c_recv):
        x, y, c = _place()
        cp = pltpu.make_async_copy(g_hbm.at[_piece_window(col_sharded, r, cw, 2 * x + y, c)], own, lsem)
        cp.start()
        cp.wait()
        mine = pl.multiple_of(c * r, 8)

        def add(j, carry):
            rows = pl.ds(pl.multiple_of(j * chunk, 8), chunk)
            tot = own[rows, :]
            for k in range(7):
                tot = tot + slots_ref[k, rows, :].astype(F32)
            full[pl.ds(mine + pl.multiple_of(j * chunk, 8), chunk), :] = tot
            return carry

        lax.fori_loop(0, r // chunk, add, 0)
        half = full.at[pl.ds(mine, r), :]
        swap = pltpu.make_async_remote_copy(src_ref=half, dst_ref=half, send_sem=c_send, recv_sem=c_recv,
                                            device_id=(x, y, 1 - c), device_id_type=MESH)
        swap.start()
        swap.wait()

    vmem = pl.BlockSpec(memory_space=pltpu.VMEM)
    return pl.pallas_call(
        body, name=name, in_specs=[pl.BlockSpec(memory_space=pl.ANY), vmem], out_specs=vmem,
        out_shape=jax.ShapeDtypeStruct((2 * r, cw), F32),
        scratch_shapes=[pltpu.VMEM((r, cw), F32), pltpu.SemaphoreType.DMA, pltpu.SemaphoreType.DMA,
                        pltpu.SemaphoreType.DMA],
        compiler_params=pltpu.CompilerParams(vmem_limit_bytes=32 * MIB),
    )(grad, slots)


def _adamw_big(g, w, m, v, name):
    vr, vc = w.shape
    assert g.shape == w.shape
    rows = 64

    def body(g_ref, w_ref, m_ref, v_ref, go_ref, do_ref, mo_ref, vo_ref):
        gg = g_ref[...]
        delta, mn, vn = _adamw(w_ref[...], gg, m_ref[...], v_ref[...])
        go_ref[...] = gg
        do_ref[...] = delta
        mo_ref[...] = mn
        vo_ref[...] = vn

    blk = pl.BlockSpec((rows, vc), lambda i: (i, 0))
    shard = jax.ShapeDtypeStruct((vr, vc), F32)
    return pl.pallas_call(
        body, name=name, grid=(vr // rows,),
        in_specs=[blk, blk, blk, blk], out_specs=[blk] * 4,
        out_shape=[shard] * 4, compiler_params=_params(32),
    )(g, w, m, v)


def _allreduce_small(pack):
    rows = pack.shape[0]

    def body(p_ref, o_ref, slots, send_sems, recv_sems):
        x, y, c = _place()
        me = 4 * x + 2 * y + c
        slots[me] = p_ref[...]
        sends = []
        for k in range(1, 8):
            cp = pltpu.make_async_remote_copy(
                src_ref=p_ref, dst_ref=slots.at[me], send_sem=send_sems.at[k - 1], recv_sem=recv_sems.at[k - 1],
                device_id=(x ^ (k >> 2), y ^ ((k >> 1) & 1), c ^ (k & 1)), device_id_type=MESH)
            cp.start()
            sends.append(cp)
        for cp in sends:
            cp.wait()
        tot = slots[0]
        for j in range(1, 8):
            tot = tot + slots[j]
        o_ref[...] = tot

    vmem = pl.BlockSpec(memory_space=pltpu.VMEM)
    return pl.pallas_call(
        body, name="allreduce_small", in_specs=[vmem], out_specs=vmem,
        out_shape=jax.ShapeDtypeStruct(pack.shape, F32),
        scratch_shapes=[pltpu.VMEM((8, rows, D_MODEL), F32), pltpu.SemaphoreType.DMA((7,)),
                        pltpu.SemaphoreType.DMA((7,))],
    )(pack)


def _adamw_small(ws, gs, ms, vs):
    n = len(ws)

    def body(*refs):
        w_refs, g_refs, m_refs, v_refs = refs[0:n], refs[n:2 * n], refs[2 * n:3 * n], refs[3 * n:4 * n]
        d_refs, mo_refs, vo_refs = refs[4 * n:5 * n], refs[5 * n:6 * n], refs[6 * n:7 * n]
        for j in range(n):
            delta, mn, vn = _adamw(w_refs[j][...], g_refs[j][...], m_refs[j][...], v_refs[j][...])
            d_refs[j][...] = delta
            mo_refs[j][...] = mn
            vo_refs[j][...] = vn

    vmem = pl.BlockSpec(memory_space=pltpu.VMEM)
    shapes = [jax.ShapeDtypeStruct(w.shape, F32) for w in ws]
    outs = pl.pallas_call(
        body, name="adamw_small", in_specs=[vmem] * (4 * n), out_specs=[vmem] * (3 * n), out_shape=shapes * 3,
    )(*ws, *gs, *ms, *vs)
    return outs[0:n], outs[n:2 * n], outs[2 * n:3 * n]


def _local_step(x, p, target, wts, late=None):
    (win4, wout, wg4, wu4, wd4, wpg, wpp4, conv_w, fcw, g_mix, conv_b, gq, gk, g_oc, g_oa, g_ffn, fcb, g_ple) = wts
    comm = late is not None
    gm = jnp.kron(jnp.eye(N_HEADS, dtype=F32), jnp.full((HEAD_DIM, HEAD_DIM), 1.0 / HEAD_DIM, F32)).astype(BF)
    gq8, gk8 = jnp.tile(gq, (1, N_HEADS)), jnp.tile(gk, (1, N_HEADS))
    mb = _mask_table()
    zbcx, qkv, ycn, qkn = _fwd_mix(x, g_mix, win4, conv_w, conv_b, g_oc, gm, gq8, gk8, 512)
    ya, lse, gathered = _attn_fwd(qkn, qkv, mb, late[0:3] if comm else ())
    if comm:
        wout, wg4, wu4 = (g.reshape(-1, D_MODEL) for g in gathered)
    x1, gp, up, gate, act, ycat, h2, gathered = _fwd_ffn(x, ycn, ya, wout, wg4, wu4, g_oa, g_ffn, fcw, fcb, 256,
                                                    late[3:6] if comm else ())
    if comm:
        wd4, wpg, wpp4 = gathered
        wd4, wpg = wd4.reshape(D_FF, D_MODEL), wpg.reshape(D_MODEL, D_MODEL)
    dx2, h3, ds, dpp, dg_ple, loss = _fwd_tail(x1, act, p, target, wd4, wpg, wpp4, g_ple, 512)
    big, big16, slots = {}, {}, {}

    def wgrad(name, a, b, tn):
        big[name], big16[name] = _wgrad(a, b, tn, 1024, "wgrad_" + name)
        return (big16[name], _COL_SHARDED[name])

    riders = [wgrad("w_down", act, dx2, 512), wgrad("w_ple_gate", h3, ds, 1024), wgrad("w_ple_proj", p, dpp, 1024)]
    dgp, dup, dfcw, dfcb, got = _bwd_ffn_a(dx2, gate, gp, up, wd4, fcw, 512, riders if comm else ())
    slots.update(zip(("w_down", "w_ple_gate", "w_ple_proj"), got))
    riders = [wgrad("w_gate", dgp, h2, 512), wgrad("w_up", dup, h2, 512)]
    dx1, dycn, dya, dg_ffn, dg_oa = _bwd_ffn_b(dgp, dup, dx2, x1, ya, wg4, wu4, wout, g_ffn, g_oa, 512)
    riders.append(wgrad("w_out", ycat, dx1, 1024))
    dq, dk, dv, got = _attn_bwd(qkn, qkv, ya, lse, dya, mb, riders if comm else ())
    slots.update(zip(("w_gate", "w_up", "w_out"), got))
    h1, dz, dcw, dcb, dg_oc, dgq8, dgk8 = _bwd_mix(
        x, zbcx, qkv, dycn, dq, dk, dv, conv_w, conv_b, g_oc, g_mix, gm, gq8, gk8, 512)
    riders = [wgrad("w_in", h1, dz, 1536)]
    grad_x, dg_mix, got = _bwd_in(x, dx1, dz, win4, g_mix, 512, riders if comm else ())
    slots.update(zip(("w_in",), got))
    dgq = dgq8.reshape(N_HEADS, HEAD_DIM).sum(axis=0, keepdims=True)
    dgk = dgk8.reshape(N_HEADS, HEAD_DIM).sum(axis=0, keepdims=True)
    small = dict(g_mix=dg_mix, conv_w=dcw, conv_b=dcb, q_norm_g=dgq, k_norm_g=dgk, g_out_conv=dg_oc,
                 g_out_attn=dg_oa, g_ffn=dg_ffn, ffn_conv_w=dfcw, ffn_conv_b=dfcb, g_ple=dg_ple)
    return loss[0, 0], grad_x, big, slots, small


_SMALL_ROWS = 24


def _pack_small(s):
    z64 = jnp.zeros((1, 1024 - 512 - 128), F32)
    rows = [s["g_mix"], s["g_ffn"], s["g_ple"],
            jnp.concatenate([s["conv_b"], s["g_out_conv"]], axis=1),
            jnp.concatenate([s["g_out_attn"], s["q_norm_g"], s["k_norm_g"], z64], axis=1),
            jnp.pad(s["conv_w"], ((0, 0), (0, 512))),
            jnp.pad(s["ffn_conv_b"], ((0, 0), (0, 3072 - D_FF))).reshape(3, 1024),
            jnp.pad(s["ffn_conv_w"], ((0, 0), (0, 3072 - D_FF))).reshape(9, 1024),
            jnp.zeros((_SMALL_ROWS - 20, 1024), F32)]
    return jnp.concatenate(rows, axis=0)


def _unpack_small(t):
    return dict(g_mix=t[0:1], g_ffn=t[1:2], g_ple=t[2:3], conv_b=t[3:4, 0:512], g_out_conv=t[3:4, 512:1024],
                g_out_attn=t[4:5, 0:512], q_norm_g=t[4:5, 512:576], k_norm_g=t[4:5, 576:640],
                conv_w=t[5:8, 0:512], ffn_conv_b=t[8:11].reshape(1, 3072)[:, :D_FF],
                ffn_conv_w=t[11:20].reshape(3, 3072)[:, :D_FF])


_BIG = ("w_in", "w_out", "w_gate", "w_up", "w_down", "w_ple_gate", "w_ple_proj")
_COL_SHARDED = dict(w_in=True, w_out=False, w_gate=False, w_up=False, w_down=False, w_ple_gate=False, w_ple_proj=True)
_TRANSPOSED = ("w_gate", "w_up")
_WEIGHTS = ("g_mix", "w_in", "conv_w", "conv_b", "q_norm_g", "k_norm_g", "g_out_conv", "g_out_attn", "w_out",
            "g_ffn", "w_gate", "w_up", "ffn_conv_w", "ffn_conv_b", "w_down", "g_ple", "w_ple_gate", "w_ple_proj")


def kernel(x, p, g_mix, w_in, conv_w, conv_b, q_norm_g, k_norm_g, g_out_conv, g_out_attn, w_out, g_ffn, w_gate, w_up, ffn_conv_w, ffn_conv_b, w_down, g_ple, w_ple_gate, w_ple_proj, loss_target, m_g_mix, m_w_in, m_conv_w, m_conv_b, m_q_norm_g, m_k_norm_g, m_g_out_conv, m_g_out_attn, m_w_out, m_g_ffn, m_w_gate, m_w_up, m_ffn_conv_w, m_ffn_conv_b, m_w_down, m_g_ple, m_w_ple_gate, m_w_ple_proj, v_g_mix, v_w_in, v_conv_w, v_conv_b, v_q_norm_g, v_k_norm_g, v_g_out_conv, v_g_out_attn, v_w_out, v_g_ffn, v_w_gate, v_w_up, v_ffn_conv_w, v_ffn_conv_b, v_w_down, v_g_ple, v_w_ple_gate, v_w_ple_proj):
    w = dict(g_mix=g_mix, w_in=w_in, conv_w=conv_w, conv_b=conv_b, q_norm_g=q_norm_g, k_norm_g=k_norm_g,
             g_out_conv=g_out_conv, g_out_attn=g_out_attn, w_out=w_out, g_ffn=g_ffn, w_gate=w_gate, w_up=w_up,
             ffn_conv_w=ffn_conv_w, ffn_conv_b=ffn_conv_b, w_down=w_down, g_ple=g_ple, w_ple_gate=w_ple_gate,
             w_ple_proj=w_ple_proj)
    m = dict(g_mix=m_g_mix, w_in=m_w_in, conv_w=m_conv_w, conv_b=m_conv_b, q_norm_g=m_q_norm_g, k_norm_g=m_k_norm_g,
             g_out_conv=m_g_out_conv, g_out_attn=m_g_out_attn, w_out=m_w_out, g_ffn=m_g_ffn, w_gate=m_w_gate,
             w_up=m_w_up, ffn_conv_w=m_ffn_conv_w, ffn_conv_b=m_ffn_conv_b, w_down=m_w_down, g_ple=m_g_ple,
             w_ple_gate=m_w_ple_gate, w_ple_proj=m_w_ple_proj)
    v = dict(g_mix=v_g_mix, w_in=v_w_in, conv_w=v_conv_w, conv_b=v_conv_b, q_norm_g=v_q_norm_g, k_norm_g=v_k_norm_g,
             g_out_conv=v_g_out_conv, g_out_attn=v_g_out_attn, w_out=v_w_out, g_ffn=v_g_ffn, w_gate=v_w_gate,
             w_up=v_w_up, ffn_conv_w=v_ffn_conv_w, ffn_conv_b=v_ffn_conv_b, w_down=v_w_down, g_ple=v_g_ple,
             w_ple_gate=v_w_ple_gate, w_ple_proj=v_w_ple_proj)
    mats = [k for k, a in w.items() if a.ndim == 3]
    w = {k: (a[0] if k in mats else a) for k, a in w.items()}
    m = {k: (a[0] if k in mats else a) for k, a in m.items()}
    v = {k: (a[0] if k in mats else a) for k, a in v.items()}
    for n in _TRANSPOSED:
        w[n], m[n], v[n] = w[n].T, m[n].T, v[n].T
    chip = 2 * lax.axis_index("x") + lax.axis_index("y")

    late = [w[n].astype(BF) for n in ("w_out", "w_gate", "w_up", "w_down", "w_ple_gate", "w_ple_proj")]
    pack = jnp.pad(jnp.concatenate([w["conv_w"], w["ffn_conv_w"]], axis=1), ((0, 5), (0, 1024 - 128 - D_FF_SHARD)))
    win4, pack4 = _gather_weights([w["w_in"].astype(BF)], pack)
    conv_w_full = pack4[:, 0:3, 0:128].transpose(1, 0, 2).reshape(3, CONV_W)
    fcw_full = pack4[:, 0:3, 128:128 + D_FF_SHARD].transpose(1, 0, 2).reshape(3, D_FF)
    wts = (win4, None, None, None, None, None, None, conv_w_full, fcw_full, w["g_mix"], w["conv_b"], w["q_norm_g"],
           w["k_norm_g"], w["g_out_conv"], w["g_out_attn"], w["g_ffn"], w["ffn_conv_b"], w["g_ple"])

    loss, grad_x, big, slots, small = _local_step(x[0], p[0, 0], loss_target[0], wts, late)
    loss = lax.psum(loss, ("x", "y", "c"))

    grads, deltas, new_m, new_v = {}, {}, {}, {}
    for name in _BIG:
        total = _finish_reduce(big[name], slots[name], _COL_SHARDED[name], "finish_" + name)
        grads[name], deltas[name], new_m[name], new_v[name] = _adamw_big(total, w[name], m[name], v[name],
                                                                         "adamw_" + name)
    tot = _unpack_small(_allreduce_small(_pack_small(small)))
    tot["conv_w"] = lax.dynamic_slice_in_dim(tot["conv_w"], chip * 128, 128, axis=1)
    tot["ffn_conv_w"] = lax.dynamic_slice_in_dim(tot["ffn_conv_w"], chip * D_FF_SHARD, D_FF_SHARD, axis=1)
    names = [n for n in _WEIGHTS if n not in _BIG]
    d_s, m_s, v_s = _adamw_small([w[n] for n in names], [tot[n] for n in names], [m[n] for n in names],
                                 [v[n] for n in names])
    for j, n in enumerate(names):
        grads[n], deltas[n], new_m[n], new_v[n] = tot[n], d_s[j], m_s[j], v_s[j]

    out = [loss, grad_x[None]]
    for group in (grads, deltas, new_m, new_v):
        for n in _TRANSPOSED:
            group[n] = group[n].T
        out += [group[n][None] if n in mats else group[n] for n in _WEIGHTS]
    return tuple(out)
```

```python
import jax
import jax.numpy as jnp
from jax import lax
from jax.experimental import pallas as pl
from jax.experimental.pallas import tpu as pltpu

D_MODEL = 1024
CONV_W = 512
N_HEADS = 8
HEAD_DIM = 64
ATTN_W = 512
D_FF = 2816
D_FF_SHARD = 704
FF_SLABS = ((0, 1408), (1408, 2816))
IN_SLAB = 768
PLE_DIM = 256
N_CHIPS = 4
QBLK = 128
DILATIONS = (1, 4, 16)
EPS = 1e-6
NEG = -1e30
MESH = pl.DeviceIdType.MESH

ADAM_LR = 0.001
ADAM_B1 = 0.9
ADAM_B2 = 0.999
ADAM_EPS = 1e-08
ADAM_WD = 0.01
ADAM_STEP = 10

BF = jnp.bfloat16
F32 = jnp.float32
MIB = 1024 * 1024


def _mm(a, b):
    return jnp.dot(a, b, preferred_element_type=F32)


def _mm_nt(a, b):
    return lax.dot_general(a, b, (((1,), (1,)), ((), ())), preferred_element_type=F32)


def _mm_tn(a, b):
    return lax.dot_general(a, b, (((0,), (0,)), ((), ())), preferred_element_type=F32)


def _rstd(a):
    return lax.rsqrt(jnp.mean(a * a, axis=-1, keepdims=True) + EPS)


def _norm_bwd(dy, xh, r, g):
    dxh = dy * g
    return r * (dxh - xh * jnp.mean(dxh * xh, axis=-1, keepdims=True))


def _colsum(a):
    return jnp.sum(a, axis=0, keepdims=True)


def _head_mean(a, gm_ref):
    return _mm(a.astype(BF), gm_ref[...])


def _shift_down(buf, k, tm):
    return pltpu.roll(buf, k, axis=0)[8:8 + tm]


def _shift_up(buf, k, tm):
    return pltpu.roll(buf, tm + 8 - k, axis=0)[0:tm]


def _params(vmem_mib, n_grid=1):
    return pltpu.CompilerParams(dimension_semantics=("arbitrary",) * n_grid, vmem_limit_bytes=vmem_mib * MIB)


def _const(shape):
    n = len(shape)
    return pl.BlockSpec(shape, lambda *_: (0,) * n, pipeline_mode=pl.Buffered(1))


def _rows(tm, width, rev_of=None):
    if rev_of is None:
        return pl.BlockSpec((tm, width), lambda i: (i, 0))
    return pl.BlockSpec((tm, width), lambda i: (rev_of - 1 - i, 0))


def _halo(tm, width, nt):
    return pl.BlockSpec((8, width), lambda i: (jnp.maximum((nt - 1 - i) * (tm // 8) - 1, 0), 0))


def _fwd_mix(x, g_mix, win4, conv_w, conv_b, g_oc, gm, gq8, gk8, tm):
    t = x.shape[0]
    nt = t // tm

    def body(x_ref, g_ref, w_ref, cw_ref, cb_ref, goc_ref, gm_ref, gq_ref, gk_ref,
             zbcx_ref, qkv_ref, ycn_ref, qkn_ref, ubuf):
        @pl.when(pl.program_id(0) == 0)
        def _():
            ubuf[0:8, :] = jnp.zeros((8, CONV_W), F32)

        xt = x_ref[...]
        h = ((xt * _rstd(xt)) * g_ref[...]).astype(BF)
        zbcx_ref[:, 0:IN_SLAB] = _mm(h, w_ref[0])
        zbcx_ref[:, IN_SLAB:2 * IN_SLAB] = _mm(h, w_ref[1])
        qkv_ref[:, 0:IN_SLAB] = _mm(h, w_ref[2])
        qkv_ref[:, IN_SLAB:2 * IN_SLAB] = _mm(h, w_ref[3])
        u = zbcx_ref[:, 512:1024] * zbcx_ref[:, 1024:1536]
        ubuf[8:8 + tm, :] = u
        ub = ubuf[...]
        cv = (cw_ref[0:1, :] * _shift_down(ub, 2, tm) + cw_ref[1:2, :] * _shift_down(ub, 1, tm)
              + cw_ref[2:3, :] * u + cb_ref[...])
        ubuf[0:8, :] = ubuf[tm:tm + 8, :]
        yc = zbcx_ref[:, 0:512] * cv
        ycn_ref[...] = ((yc * _rstd(yc)) * goc_ref[...]).astype(BF)
        zq = qkv_ref[:, 0:512]
        zk = qkv_ref[:, 512:1024]
        rq = lax.rsqrt(_head_mean(zq * zq, gm_ref) + EPS)
        rk = lax.rsqrt(_head_mean(zk * zk, gm_ref) + EPS)
        qkn_ref[:, 0:512] = ((zq * rq) * gq_ref[...]) * (HEAD_DIM ** -0.5)
        qkn_ref[:, 512:1024] = (zk * rk) * gk_ref[...]

    return pl.pallas_call(
        body, name="fwd_mix", grid=(nt,),
        in_specs=[_rows(tm, D_MODEL), _const((1, D_MODEL)), _const((N_CHIPS, D_MODEL, IN_SLAB)),
                  _const((3, CONV_W)), _const((1, CONV_W)), _const((1, CONV_W)), _const((ATTN_W, ATTN_W)),
                  _const((1, ATTN_W)), _const((1, ATTN_W))],
        out_specs=[_rows(tm, 1536), _rows(tm, 1536), _rows(tm, CONV_W), _rows(tm, 1024)],
        out_shape=[jax.ShapeDtypeStruct((t, 1536), F32), jax.ShapeDtypeStruct((t, 1536), F32),
                   jax.ShapeDtypeStruct((t, CONV_W), BF), jax.ShapeDtypeStruct((t, 1024), F32)],
        scratch_shapes=[pltpu.VMEM((tm + 8, CONV_W), F32)],
        compiler_params=_params(48),
    )(x, g_mix, win4, conv_w, conv_b, g_oc, gm, gq8, gk8)


def _place():
    x, y, c = lax.axis_index("x"), lax.axis_index("y"), lax.axis_index("c")
    return x, y, c


def _chip_peer(x, y, k):
    return x ^ (k >> 1), y ^ (k & 1)


def _piece_shape(grad_shape, col_sharded):
    kk, nn = grad_shape
    return (kk // 2, nn // N_CHIPS) if col_sharded else (kk // (2 * N_CHIPS), nn)


def _piece_window(col_sharded, r, cw, s, h):
    if col_sharded:
        return (pl.ds(pl.multiple_of(h * r, 16), r), pl.ds(pl.multiple_of(s * cw, 128), cw))
    return (pl.ds(pl.multiple_of((2 * s + h) * r, 16), r), slice(None))


def _scatter_copies(g_ref, slots_ref, send_sems, recv_sems, base, col_sharded):
    x, y, c = _place()
    r, cw = slots_ref.shape[1:]
    copies = []
    for k in range(1, 8):
        tx, ty, tc = x ^ (k >> 2), y ^ ((k >> 1) & 1), c ^ (k & 1)
        copies.append(pltpu.make_async_remote_copy(
            src_ref=g_ref.at[_piece_window(col_sharded, r, cw, 2 * tx + ty, tc)], dst_ref=slots_ref.at[k - 1],
            send_sem=send_sems.at[base + k - 1], recv_sem=recv_sems.at[base + k - 1],
            device_id=(tx, ty, tc), device_id_type=MESH))
    return copies


def _ride_scatter(first, last, riders, g_refs, slot_refs, send_sems, recv_sems):
    def all_copies():
        out = []
        for j, (_, col_sharded) in enumerate(riders):
            out += _scatter_copies(g_refs[j], slot_refs[j], send_sems, recv_sems, 7 * j, col_sharded)
        return out

    @pl.when(first)
    def _():
        for cp in all_copies():
            cp.start()

    @pl.when(last)
    def _():
        for cp in all_copies():
            cp.wait()


def _rider_specs(riders):
    any_spec = pl.BlockSpec(memory_space=pl.ANY)
    shapes = [jax.ShapeDtypeStruct((7,) + _piece_shape(g.shape, cs), BF) for g, cs in riders]
    sems = [pltpu.SemaphoreType.DMA((7 * len(riders),)), pltpu.SemaphoreType.DMA((7 * len(riders),))] if riders else []
    return [any_spec] * len(riders), shapes, sems


class _Gather:
    def __init__(self, ins, outs, send_sems, recv_sems, local_sems):
        self.ins, self.outs = ins, outs
        self.send_sems, self.recv_sems, self.local_sems = send_sems, recv_sems, local_sems
        self.x, self.y, self.c = _place()
        self.me = 2 * self.x + self.y

    def _push(self, src, dst, w, j, to):
        return pltpu.make_async_remote_copy(src_ref=src, dst_ref=dst, send_sem=self.send_sems.at[6 * w + j],
                                            recv_sem=self.recv_sems.at[6 * w + j], device_id=to, device_id_type=MESH)

    def _half(self, w, h):
        half = self.ins[w].shape[0] // 2
        return pl.ds(pl.multiple_of(h * half, 16), half)

    def _local(self, w):
        return pltpu.make_async_copy(self.ins[w], self.outs[w].at[self.me], self.local_sems.at[w])

    def _ici(self, w, k):
        px, py = _chip_peer(self.x, self.y, k)
        mine = self._half(w, self.c)
        return self._push(self.ins[w].at[mine], self.outs[w].at[self.me, mine], w, k - 1, (px, py, self.c))

    def _landed(self, w, k, h):
        return self.outs[w].at[self.me ^ k, self._half(w, h)]

    def _fwd(self, w, k):
        landed = self._landed(w, k, self.c)
        return self._push(landed, landed, w, 2 + k, (self.x, self.y, 1 - self.c))

    def start(self):
        for w in range(len(self.ins)):
            self._local(w).start()
            for k in (1, 2, 3):
                self._ici(w, k).start()

    def forward(self):
        for w in range(len(self.ins)):
            for k in (1, 2, 3):
                landed = self._landed(w, k, self.c)
                self._push(landed, landed, w, k - 1, (self.x, self.y, self.c)).wait_recv()
                self._fwd(w, k).start()

    def finish(self):
        for w in range(len(self.ins)):
            for k in (1, 2, 3):
                landed = self._landed(w, k, 1 - self.c)
                self._push(landed, landed, w, 2 + k, (self.x, self.y, self.c)).wait_recv()
            for k in (1, 2, 3):
                self._ici(w, k).wait_send()
                self._fwd(w, k).wait_send()
            self._local(w).wait()


def _alibi(h):
    return 2.0 ** (-(h + 1))


CHUNK = 2048


def _mask_table():
    slopes = jnp.asarray([_alibi(h) for h in range(N_HEADS)], F32)[:, None, None]
    step = jnp.arange(QBLK)[:, None] + QBLK - jnp.arange(2 * QBLK)[None, :]
    valid = (step >= 0) & (step <= QBLK)
    tab = jnp.stack([jnp.where(valid[None], -slopes * (step * d)[None].astype(F32), NEG) for d in DILATIONS])
    return tab.reshape(3, N_HEADS // 2, 2 * QBLK, 2 * QBLK)


def _attn_fwd(qkn, qkv, mb, late=()):
    t = qkn.shape[0]
    nc = t // CHUNK
    nl = len(late)

    def body(*refs):
        qc_ref, kp_ref, kc_ref, vp_ref, vc_ref, mb_ref = refs[0:6]
        o_ref, l_ref = refs[6 + nl:8 + nl]
        ob0, ob1, ob2, lb0, lb1, lb2 = refs[8 + 2 * nl:14 + 2 * nl]
        if nl:
            gather = _Gather(refs[6:6 + nl], refs[8 + nl:8 + 2 * nl], *refs[14 + 2 * nl:17 + 2 * nl])
            step = pl.program_id(0) * nc + pl.program_id(1)
            pl.when(step == 0)(gather.start)
            pl.when(step == 2 * nc)(gather.forward)
            pl.when(step == (N_HEADS // 2) * nc - 1)(gather.finish)
        first = pl.program_id(1) == 0
        lane = lax.broadcasted_iota(jnp.int32, (QBLK, 128), 1)
        lo_half = lane < HEAD_DIM
        kj = lax.broadcasted_iota(jnp.int32, (2 * QBLK, 2 * QBLK), 1)
        no_prev = first & (kj < QBLK)
        obs, lbs = (ob0, ob1, ob2), (lb0, lb1, lb2)

        def by_head(a):
            return jnp.where(lo_half, a, 0.0).astype(BF), jnp.where(lo_half, 0.0, a).astype(BF)

        for di, d in enumerate(DILATIONS):
            span = d * QBLK
            for r in range(d):
                tail = pl.ds(CHUNK - span + r, QBLK, stride=d)
                k_prev = kp_ref[tail, :].astype(BF)
                v_prev = by_head(vp_ref[tail, :])
                for b in range(CHUNK // span):
                    rows = pl.ds(r + span * b, QBLK, stride=d)
                    q0, q1 = by_head(qc_ref[rows, :])
                    k_cur = kc_ref[rows, :].astype(BF)
                    v_cur = by_head(vc_ref[rows, :])
                    s = _mm_nt(jnp.concatenate([q0, q1], axis=0), jnp.concatenate([k_prev, k_cur], axis=0))
                    s = s + mb_ref[di, 0]
                    if b == 0:
                        s = jnp.where(no_prev, NEG, s)
                    m = jnp.max(s, axis=-1, keepdims=True)
                    e = jnp.exp(s - m)
                    den = jnp.sum(e, axis=-1, keepdims=True)
                    eb = e.astype(BF)
                    o = _mm(jnp.concatenate([eb[0:QBLK], eb[QBLK:2 * QBLK]], axis=1),
                            jnp.concatenate([v_prev[0], v_cur[0], v_prev[1], v_cur[1]], axis=0))
                    inv = 1.0 / den
                    lse = m + jnp.log(den)
                    obs[di][rows, :] = o * jnp.where(lo_half, inv[0:QBLK], inv[QBLK:2 * QBLK])
                    lbs[di][rows, :] = jnp.where(lo_half, lse[0:QBLK], lse[QBLK:2 * QBLK])
                    k_prev, v_prev = k_cur, v_cur
        for c0 in range(0, CHUNK, 256):
            rs = slice(c0, c0 + 256)
            l0, l1, l2 = lb0[rs, :], lb1[rs, :], lb2[rs, :]
            mx = jnp.maximum(jnp.maximum(l0, l1), l2)
            w0, w1, w2 = jnp.exp(l0 - mx), jnp.exp(l1 - mx), jnp.exp(l2 - mx)
            tot = w0 + w1 + w2
            o_ref[rs, :] = (ob0[rs, :] * w0 + ob1[rs, :] * w1 + ob2[rs, :] * w2) / tot
            l_ref[rs, :] = mx + jnp.log(tot)

    def cur(col):
        return pl.BlockSpec((CHUNK, 128), lambda hp, n: (n, col + hp))

    def prv(col):
        return pl.BlockSpec((CHUNK, 128), lambda hp, n: (jnp.maximum(n - 1, 0), col + hp))

    out = pl.BlockSpec((CHUNK, 128), lambda hp, n: (n, hp))
    any_spec = pl.BlockSpec(memory_space=pl.ANY)
    sems = [pltpu.SemaphoreType.DMA((6 * nl,)), pltpu.SemaphoreType.DMA((6 * nl,)), pltpu.SemaphoreType.DMA((nl,))]
    res = pl.pallas_call(
        body, name="attn_fwd", grid=(N_HEADS // 2, nc),
        in_specs=[cur(0), prv(4), cur(4), prv(8), cur(8),
                  pl.BlockSpec((3, 1, 2 * QBLK, 2 * QBLK), lambda hp, n: (0, hp, 0, 0))] + [any_spec] * nl,
        out_specs=[out, out] + [any_spec] * nl,
        out_shape=[jax.ShapeDtypeStruct((t, ATTN_W), F32)] * 2
        + [jax.ShapeDtypeStruct((N_CHIPS,) + w.shape, w.dtype) for w in late],
        scratch_shapes=[pltpu.VMEM((CHUNK, 128), F32)] * 6 + (sems if nl else []),
        compiler_params=_params(48, 2),
    )(qkn, qkn, qkn, qkv, qkv, mb, *late)
    return res[0], res[1], list(res[2:])


def _attn_bwd(qkn, qkv, o, lse, do, mb, riders=()):
    t = qkn.shape[0]
    nc = t // CHUNK
    nr = len(riders)

    def body(*refs):
        (qc_ref, qn_ref, kp_ref, kc_ref, vp_ref, vc_ref, oc_ref, on_ref, lc_ref, ln_ref, dc_ref, dn_ref,
         mb_ref) = refs[0:13]
        dq_ref, dk_ref, dv_ref = refs[13 + nr:16 + nr]
        if nr:
            step = pl.program_id(0) * nc + pl.program_id(1)
            _ride_scatter(step == 0, step == (N_HEADS // 2) * nc - 1, riders, refs[13:13 + nr],
                          refs[16 + nr:16 + 2 * nr], *refs[16 + 2 * nr:18 + 2 * nr])
        first = pl.program_id(1) == 0
        last = pl.program_id(1) == nc - 1
        lane = lax.broadcasted_iota(jnp.int32, (QBLK, 128), 1)
        lo_half = lane < HEAD_DIM
        kj = lax.broadcasted_iota(jnp.int32, (2 * QBLK, 2 * QBLK), 1)
        no_prev = first & (kj < QBLK)

        def by_head(a):
            return jnp.where(lo_half, a, 0.0).astype(BF), jnp.where(lo_half, 0.0, a).astype(BF)

        def query_side(q_ref, d_ref, o_ref_, l_ref_, rows):
            dvals = d_ref[rows, :]
            dd = dvals * o_ref_[rows, :]
            lv = l_ref_[rows, :]
            d0 = jnp.sum(jnp.where(lo_half, dd, 0.0), axis=-1, keepdims=True)
            d1 = jnp.sum(jnp.where(lo_half, 0.0, dd), axis=-1, keepdims=True)
            l0 = jnp.max(jnp.where(lo_half, lv, NEG), axis=-1, keepdims=True)
            l1 = jnp.max(jnp.where(lo_half, NEG, lv), axis=-1, keepdims=True)
            return (jnp.concatenate(by_head(q_ref[rows, :]), axis=0), jnp.concatenate(by_head(dvals), axis=0),
                    jnp.concatenate([l0, l1], axis=0), jnp.concatenate([d0, d1], axis=0))

        def tile(qs, dos, lcol, dcol, keys, vals, bias, dead):
            s = _mm_nt(qs, keys) + bias
            if dead is not None:
                s = jnp.where(dead, NEG, s)
            p = jnp.exp(s - lcol)
            ds = p * (_mm_nt(dos, vals) - dcol)
            return p.astype(BF), ds.astype(BF)

        def put(ref, di, rows, val):
            if di == 0:
                ref[rows, :] = val
            else:
                ref[rows, :] = ref[rows, :] + val

        for di, d in enumerate(DILATIONS):
            span = d * QBLK
            nbk = CHUNK // span
            for r in range(d):
                tail = pl.ds(CHUNK - span + r, QBLK, stride=d)
                k_prev = kp_ref[tail, :]
                kb_prev, km_prev = k_prev.astype(BF), by_head(k_prev)
                vb_prev = vp_ref[tail, :].astype(BF)
                rows_prev, dk_part, dv_part = None, None, None
                for b in range(nbk):
                    rows = pl.ds(r + span * b, QBLK, stride=d)
                    qs, dos, lcol, dcol = query_side(qc_ref, dc_ref, oc_ref, lc_ref, rows)
                    k_cur = kc_ref[rows, :]
                    kb_cur, km_cur = k_cur.astype(BF), by_head(k_cur)
                    vb_cur = vc_ref[rows, :].astype(BF)
                    p, ds = tile(qs, dos, lcol, dcol, jnp.concatenate([kb_prev, kb_cur], axis=0),
                                 jnp.concatenate([vb_prev, vb_cur], axis=0), mb_ref[di, 0],
                                 no_prev if b == 0 else None)
                    put(dq_ref, di, rows,
                        _mm(jnp.concatenate([ds[0:QBLK], ds[QBLK:2 * QBLK]], axis=1),
                            jnp.concatenate([km_prev[0], km_cur[0], km_prev[1], km_cur[1]], axis=0)))
                    dk2 = _mm_tn(ds, qs)
                    dv2 = _mm_tn(p, dos)
                    if b > 0:
                        put(dk_ref, di, rows_prev, dk_part + dk2[0:QBLK])
                        put(dv_ref, di, rows_prev, dv_part + dv2[0:QBLK])
                    rows_prev, dk_part, dv_part = rows, dk2[QBLK:2 * QBLK], dv2[QBLK:2 * QBLK]
                    kb_prev, km_prev, vb_prev = kb_cur, km_cur, vb_cur
                qs, dos, lcol, dcol = query_side(qn_ref, dn_ref, on_ref, ln_ref, pl.ds(r, QBLK, stride=d))
                p, ds = tile(qs, dos, lcol, dcol, kb_prev, vb_prev, mb_ref[di, 0, :, 0:QBLK], last)
                put(dk_ref, di, rows_prev, dk_part + _mm_tn(ds, qs))
                put(dv_ref, di, rows_prev, dv_part + _mm_tn(p, dos))

    def at(shift, col):
        return pl.BlockSpec((CHUNK, 128), lambda hp, n: (jnp.clip(n + shift, 0, nc - 1), col + hp))

    out = pl.BlockSpec((CHUNK, 128), lambda hp, n: (n, hp))
    r_in, r_out, r_sems = _rider_specs(riders)
    res = pl.pallas_call(
        body, name="attn_bwd", grid=(N_HEADS // 2, nc),
        in_specs=[at(0, 0), at(1, 0), at(-1, 4), at(0, 4), at(-1, 8), at(0, 8),
                  at(0, 0), at(1, 0), at(0, 0), at(1, 0), at(0, 0), at(1, 0),
                  pl.BlockSpec((3, 1, 2 * QBLK, 2 * QBLK), lambda hp, n: (0, hp, 0, 0))] + r_in,
        out_specs=[out, out, out] + r_in,
        out_shape=[jax.ShapeDtypeStruct((t, ATTN_W), F32)] * 3 + r_out,
        scratch_shapes=r_sems,
        compiler_params=_params(56, 2),
    )(qkn, qkn, qkn, qkn, qkv, qkv, o, o, lse, lse, do, do, mb, *[g for g, _ in riders])
    return res[0], res[1], res[2], list(res[3:])


def _fwd_ffn(x, ycn, ya, wout, wg4, wu4, g_oa, g_ffn, fcw, fcb, tm, late=()):
    t = x.shape[0]
    nt = t // tm
    nl = len(late)

    def body(*refs):
        x_ref, ycn_ref, ya_ref, wout_ref, wg_ref, wu_ref, goa_ref, gffn_ref, fcw_ref, fcb_ref = refs[0:10]
        x1_ref, gp_ref, up_ref, gate_ref, act_ref, ycat_ref, h2_ref = refs[10 + nl:17 + nl]
        cbuf = refs[17 + 2 * nl]
        if nl:
            gather = _Gather(refs[10:10 + nl], refs[17 + nl:17 + 2 * nl], *refs[18 + 2 * nl:21 + 2 * nl])
            pl.when(pl.program_id(0) == 0)(gather.start)
            pl.when(pl.program_id(0) == nt // 2)(gather.forward)
            pl.when(pl.program_id(0) == nt - 1)(gather.finish)

        @pl.when(pl.program_id(0) == 0)
        def _():
            cbuf[0:8, :] = jnp.zeros((8, D_FF), F32)

        yat = ya_ref[...]
        yan = ((yat * _rstd(yat)) * goa_ref[...]).astype(BF)
        ycn = ycn_ref[...]
        ycat_ref[:, 0:CONV_W] = ycn
        ycat_ref[:, CONV_W:D_MODEL] = yan
        x1 = x_ref[...] + _mm(ycn, wout_ref[0:CONV_W, :]) + _mm(yan, wout_ref[CONV_W:D_MODEL, :])
        x1_ref[...] = x1
        h2 = ((x1 * _rstd(x1)) * gffn_ref[...]).astype(BF)
        h2_ref[...] = h2
        for lo, hi in FF_SLABS:
            gps = _mm_nt(h2, wg_ref[lo:hi, :])
            ups = _mm_nt(h2, wu_ref[lo:hi, :])
            gp_ref[:, lo:hi] = gps.astype(BF)
            up_ref[:, lo:hi] = ups.astype(BF)
            cbuf[8:8 + tm, lo:hi] = gps
            cb = cbuf[:, lo:hi]
            gate = (fcw_ref[0:1, lo:hi] * _shift_down(cb, 2, tm) + fcw_ref[1:2, lo:hi] * _shift_down(cb, 1, tm)
                    + fcw_ref[2:3, lo:hi] * gps + fcb_ref[:, lo:hi])
            gate_ref[:, lo:hi] = gate.astype(BF)
            act_ref[:, lo:hi] = ((gate * jax.nn.sigmoid(gate)) * ups).astype(BF)
        cbuf[0:8, :] = cbuf[tm:tm + 8, :]

    any_spec = pl.BlockSpec(memory_space=pl.ANY)
    sems = [pltpu.SemaphoreType.DMA((6 * nl,)), pltpu.SemaphoreType.DMA((6 * nl,)), pltpu.SemaphoreType.DMA((nl,))]
    res = pl.pallas_call(
        body, name="fwd_ffn", grid=(nt,),
        in_specs=[_rows(tm, D_MODEL), _rows(tm, CONV_W), _rows(tm, ATTN_W), _const((D_MODEL, D_MODEL)),
                  _const((D_FF, D_MODEL)), _const((D_FF, D_MODEL)),
                  _const((1, ATTN_W)), _const((1, D_MODEL)), _const((3, D_FF)), _const((1, D_FF))]
        + [any_spec] * nl,
        out_specs=[_rows(tm, D_MODEL), _rows(tm, D_FF), _rows(tm, D_FF), _rows(tm, D_FF), _rows(tm, D_FF),
                   _rows(tm, D_MODEL), _rows(tm, D_MODEL)] + [any_spec] * nl,
        out_shape=[jax.ShapeDtypeStruct((t, D_MODEL), F32), jax.ShapeDtypeStruct((t, D_FF), BF),
                   jax.ShapeDtypeStruct((t, D_FF), BF), jax.ShapeDtypeStruct((t, D_FF), BF),
                   jax.ShapeDtypeStruct((t, D_FF), BF),
                   jax.ShapeDtypeStruct((t, D_MODEL), BF), jax.ShapeDtypeStruct((t, D_MODEL), BF)]
        + [jax.ShapeDtypeStruct((N_CHIPS,) + w.shape, w.dtype) for w in late],
        scratch_shapes=[pltpu.VMEM((tm + 8, D_FF), F32)] + (sems if nl else []),
        compiler_params=_params(56),
    )(x, ycn, ya, wout, wg4, wu4, g_oa, g_ffn, fcw, fcb, *late)
    return tuple(res[0:7]) + (list(res[7:]),)


def _fwd_tail(x1, act, p, target, wd4, wpg, wpp4, g_ple, tm):
    t = x1.shape[0]
    nt = t // tm

    def body(x1_ref, act_ref, p_ref, tgt_ref, wd_ref, wpg_ref, wpp_ref, g_ref,
             dx2_ref, h3_ref, ds_ref, dpp_ref, dg_ref, loss_ref, lacc):
        i = pl.program_id(0)

        @pl.when(i == 0)
        def _():
            dg_ref[...] = jnp.zeros_like(dg_ref)
            lacc[...] = jnp.zeros_like(lacc)

        x2 = x1_ref[...]
        for lo, hi in FF_SLABS:
            x2 = x2 + _mm(act_ref[:, lo:hi], wd_ref[lo:hi, :])
        r3 = _rstd(x2)
        xh = x2 * r3
        h3 = (xh * g_ref[...]).astype(BF)
        h3_ref[...] = h3
        sg = jax.nn.sigmoid(_mm(h3, wpg_ref[...]))
        pb = p_ref[...].astype(BF)
        pp = jnp.concatenate([_mm(pb, wpp_ref[s]) for s in range(N_CHIPS)], axis=1)
        err = (x2 + sg * pp) - tgt_ref[...]
        lacc[...] += _colsum(err * err)
        dx3 = err * (1.0 / D_MODEL)
        dpp_ref[...] = (dx3 * sg).astype(BF)
        dsb = ((dx3 * pp) * (sg * (1.0 - sg))).astype(BF)
        ds_ref[...] = dsb
        dh3 = _mm_nt(dsb, wpg_ref[...])
        dg_ref[...] += _colsum(dh3 * xh)
        dx2_ref[...] = dx3 + _norm_bwd(dh3, xh, r3, g_ref[...])

        @pl.when(i == nt - 1)
        def _():
            loss_ref[...] = jnp.full((1, 128), jnp.sum(lacc[...]) * (0.5 / D_MODEL), F32)

    return pl.pallas_call(
        body, name="fwd_tail", grid=(nt,),
        in_specs=[_rows(tm, D_MODEL), _rows(tm, D_FF), _rows(tm, PLE_DIM), _rows(tm, D_MODEL),
                  _const((D_FF, D_MODEL)), _const((D_MODEL, D_MODEL)),
                  _const((N_CHIPS, PLE_DIM, PLE_DIM)), _const((1, D_MODEL))],
        out_specs=[_rows(tm, D_MODEL), _rows(tm, D_MODEL), _rows(tm, D_MODEL), _rows(tm, D_MODEL),
                   pl.BlockSpec((1, D_MODEL), lambda i: (0, 0)), pl.BlockSpec((1, 128), lambda i: (0, 0))],
        out_shape=[jax.ShapeDtypeStruct((t, D_MODEL), F32), jax.ShapeDtypeStruct((t, D_MODEL), BF),
                   jax.ShapeDtypeStruct((t, D_MODEL), BF), jax.ShapeDtypeStruct((t, D_MODEL), BF),
                   jax.ShapeDtypeStruct((1, D_MODEL), F32), jax.ShapeDtypeStruct((1, 128), F32)],
        scratch_shapes=[pltpu.VMEM((1, D_MODEL), F32)],
        compiler_params=_params(48),
    )(x1, act, p, target, wd4, wpg, wpp4, g_ple)


def _bwd_ffn_a(dx2, gate, gp, up, wd4, fcw, tm, riders=()):
    t = dx2.shape[0]
    nt = t // tm
    nr = len(riders)

    def body(*refs):
        dx2_ref, gate_ref, gp_ref, up_ref, wd_ref, fcw_ref = refs[0:6]
        dgp_ref, dup_ref, dfcw_ref, dfcb_ref = refs[6 + nr:10 + nr]
        dbuf = refs[10 + 2 * nr]
        i = pl.program_id(0)
        if nr:
            _ride_scatter(i == 0, i == nt - 1, riders, refs[6:6 + nr], refs[10 + nr:10 + 2 * nr],
                          *refs[11 + 2 * nr:13 + 2 * nr])

        @pl.when(i == 0)
        def _():
            dbuf[tm:tm + 8, :] = jnp.zeros((8, D_FF), F32)
            dfcw_ref[...] = jnp.zeros_like(dfcw_ref)
            dfcb_ref[...] = jnp.zeros_like(dfcb_ref)

        dx2b = dx2_ref[...].astype(BF)
        for lo, hi in FF_SLABS:
            gate = gate_ref[:, lo:hi].astype(F32)
            gps = gp_ref[:, lo:hi].astype(F32)
            w0, w1, w2 = fcw_ref[0:1, lo:hi], fcw_ref[1:2, lo:hi], fcw_ref[2:3, lo:hi]
            sg = jax.nn.sigmoid(gate)
            dact = _mm_nt(dx2b, wd_ref[lo:hi, :])
            dup_ref[:, lo:hi] = (dact * (gate * sg)).astype(BF)
            dgate = (dact * up_ref[:, lo:hi].astype(F32)) * (sg * (1.0 + gate * (1.0 - sg)))
            dbuf[0:tm, lo:hi] = dgate
            db = dbuf[:, lo:hi]
            d1 = _shift_up(db, 1, tm)
            d2 = _shift_up(db, 2, tm)
            dfcb_ref[:, lo:hi] += _colsum(dgate)
            dfcw_ref[0:1, lo:hi] += _colsum(d2 * gps)
            dfcw_ref[1:2, lo:hi] += _colsum(d1 * gps)
            dfcw_ref[2:3, lo:hi] += _colsum(dgate * gps)
            dgp_ref[:, lo:hi] = (w2 * dgate + w1 * d1 + w0 * d2).astype(BF)
        dbuf[tm:tm + 8, :] = dbuf[0:8, :]

    r_in, r_out, r_sems = _rider_specs(riders)
    res = pl.pallas_call(
        body, name="bwd_ffn_a", grid=(nt,),
        in_specs=[_rows(tm, D_MODEL, nt), _rows(tm, D_FF, nt), _rows(tm, D_FF, nt), _rows(tm, D_FF, nt),
                  _const((D_FF, D_MODEL)), _const((3, D_FF))] + r_in,
        out_specs=[_rows(tm, D_FF, nt), _rows(tm, D_FF, nt),
                   pl.BlockSpec((3, D_FF), lambda i: (0, 0)), pl.BlockSpec((1, D_FF), lambda i: (0, 0))] + r_in,
        out_shape=[jax.ShapeDtypeStruct((t, D_FF), BF), jax.ShapeDtypeStruct((t, D_FF), BF),
                   jax.ShapeDtypeStruct((3, D_FF), F32), jax.ShapeDtypeStruct((1, D_FF), F32)] + r_out,
        scratch_shapes=[pltpu.VMEM((tm + 8, D_FF), F32)] + r_sems,
        compiler_params=_params(56),
    )(dx2, gate, gp, up, wd4, fcw, *[g for g, _ in riders])
    return res[0], res[1], res[2], res[3], list(res[4:])


def _bwd_ffn_b(dgp, dup, dx2, x1, ya, wg4, wu4, wout, g_ffn, g_oa, tm):
    t = dx2.shape[0]
    nt = t // tm

    def body(dgp_ref, dup_ref, dx2_ref, x1_ref, ya_ref, wg_ref, wu_ref, wout_ref, gffn_ref, goa_ref,
             dx1_ref, dycn_ref, dya_ref, dgffn_ref, dgoa_ref):
        @pl.when(pl.program_id(0) == 0)
        def _():
            dgffn_ref[...] = jnp.zeros_like(dgffn_ref)
            dgoa_ref[...] = jnp.zeros_like(dgoa_ref)

        dh2 = jnp.zeros((tm, D_MODEL), F32)
        for lo, hi in FF_SLABS:
            dh2 = dh2 + _mm(dgp_ref[:, lo:hi], wg_ref[lo:hi, :]) + _mm(dup_ref[:, lo:hi], wu_ref[lo:hi, :])
        x1 = x1_ref[...]
        r2 = _rstd(x1)
        xh = x1 * r2
        dgffn_ref[...] += _colsum(dh2 * xh)
        dx1 = dx2_ref[...] + _norm_bwd(dh2, xh, r2, gffn_ref[...])
        dx1_ref[...] = dx1
        dy = _mm_nt(dx1.astype(BF), wout_ref[...])
        dycn_ref[...] = dy[:, 0:CONV_W]
        dyan = dy[:, CONV_W:D_MODEL]
        yat = ya_ref[...]
        ra = _rstd(yat)
        yah = yat * ra
        dgoa_ref[...] += _colsum(dyan * yah)
        dya_ref[...] = _norm_bwd(dyan, yah, ra, goa_ref[...])

    return pl.pallas_call(
        body, name="bwd_ffn_b", grid=(nt,),
        in_specs=[_rows(tm, D_FF), _rows(tm, D_FF), _rows(tm, D_MODEL), _rows(tm, D_MODEL),
                  _rows(tm, ATTN_W), _const((D_FF, D_MODEL)), _const((D_FF, D_MODEL)),
                  _const((D_MODEL, D_MODEL)), _const((1, D_MODEL)), _const((1, ATTN_W))],
        out_specs=[_rows(tm, D_MODEL), _rows(tm, CONV_W), _rows(tm, ATTN_W),
                   pl.BlockSpec((1, D_MODEL), lambda i: (0, 0)), pl.BlockSpec((1, ATTN_W), lambda i: (0, 0))],
        out_shape=[jax.ShapeDtypeStruct((t, D_MODEL), F32), jax.ShapeDtypeStruct((t, CONV_W), F32),
                   jax.ShapeDtypeStruct((t, ATTN_W), F32),
                   jax.ShapeDtypeStruct((1, D_MODEL), F32), jax.ShapeDtypeStruct((1, ATTN_W), F32)],
        compiler_params=_params(48),
    )(dgp, dup, dx2, x1, ya, wg4, wu4, wout, g_ffn, g_oa)


def _bwd_mix(x, zbcx, qkv, dycn, dq, dk, dv, conv_w, conv_b, g_oc, g_mix, gm, gq8, gk8, tm):
    t = x.shape[0]
    nt = t // tm

    def body(x_ref, z_ref, zh_ref, qkv_ref, dycn_ref, dq_ref, dk_ref, dv_ref, cw_ref, cb_ref,
             goc_ref, g_ref, gm_ref, gq_ref, gk_ref,
             dz_ref, dcw_ref, dcb_ref, dgoc_ref, dgq_ref, dgk_ref, gw32_ref, gw16_ref,
             ubuf, dbuf, wacc, wstage, osem):
        i = pl.program_id(0)

        @pl.when(i == 0)
        def _():
            wacc[...] = jnp.zeros_like(wacc)
            dbuf[tm:tm + 8, :] = jnp.zeros((8, CONV_W), F32)
            dcw_ref[...] = jnp.zeros_like(dcw_ref)
            dcb_ref[...] = jnp.zeros_like(dcb_ref)
            dgoc_ref[...] = jnp.zeros_like(dgoc_ref)
            dgq_ref[...] = jnp.zeros_like(dgq_ref)
            dgk_ref[...] = jnp.zeros_like(dgk_ref)

        not_first_tile = i < nt - 1
        zb = z_ref[:, 0:512]
        zc = z_ref[:, 512:1024]
        zx = z_ref[:, 1024:1536]
        u = zc * zx
        ubuf[0:8, :] = jnp.where(not_first_tile, zh_ref[:, 512:1024] * zh_ref[:, 1024:1536], 0.0)
        ubuf[8:8 + tm, :] = u
        ub = ubuf[...]
        u1 = _shift_down(ub, 1, tm)
        u2 = _shift_down(ub, 2, tm)
        w0, w1, w2 = cw_ref[0:1, :], cw_ref[1:2, :], cw_ref[2:3, :]
        cv = w0 * u2 + w1 * u1 + w2 * u + cb_ref[...]
        yc = zb * cv
        rc = _rstd(yc)
        ych = yc * rc
        dycn = dycn_ref[...]
        dgoc_ref[...] += _colsum(dycn * ych)
        dyc = _norm_bwd(dycn, ych, rc, goc_ref[...])
        dcv = dyc * zb
        dcb_ref[...] += _colsum(dcv)
        dcw_ref[0:1, :] += _colsum(dcv * u2)
        dcw_ref[1:2, :] += _colsum(dcv * u1)
        dcw_ref[2:3, :] += _colsum(dcv * u)
        dbuf[0:tm, :] = dcv
        db = dbuf[...]
        du = w2 * dcv + w1 * _shift_up(db, 1, tm) + w0 * _shift_up(db, 2, tm)
        dbuf[tm:tm + 8, :] = dbuf[0:8, :]
        dz_ref[:, 0:512] = (dyc * cv).astype(BF)
        dz_ref[:, 512:1024] = (du * zx).astype(BF)
        dz_ref[:, 1024:1536] = (du * zc).astype(BF)
        for z0, d_ref, gg_ref, acc_ref, sc in ((0, dq_ref, gq_ref, dgq_ref, HEAD_DIM ** -0.5),
                                               (512, dk_ref, gk_ref, dgk_ref, 1.0)):
            z = qkv_ref[:, z0:z0 + 512]
            rr = lax.rsqrt(_head_mean(z * z, gm_ref) + EPS)
            zh = z * rr
            dn = d_ref[...] * sc
            acc_ref[...] += _colsum(dn * zh)
            dzh = dn * gg_ref[...]
            dz_ref[:, 1536 + z0:1536 + z0 + 512] = (rr * (dzh - zh * _head_mean(dzh * zh, gm_ref))).astype(BF)
        dz_ref[:, 2560:3072] = dv_ref[...].astype(BF)
        xt = x_ref[...]
        h1 = ((xt * _rstd(xt)) * g_ref[...]).astype(BF)
        for s in range(N_CHIPS):
            cols = slice(s * IN_SLAB, (s + 1) * IN_SLAB)
            wacc[:, cols] += _mm_tn(h1, dz_ref[:, cols])

        @pl.when(i == nt - 1)
        def _():
            wstage[...] = wacc[...].astype(BF)
            out32 = pltpu.make_async_copy(wacc, gw32_ref, osem.at[0])
            out16 = pltpu.make_async_copy(wstage, gw16_ref, osem.at[1])
            out32.start()
            out16.start()
            out32.wait()
            out16.wait()

    def acc(width, rows=1):
        return pl.BlockSpec((rows, width), lambda i: (0, 0))

    return pl.pallas_call(
        body, name="bwd_mix", grid=(nt,),
        in_specs=[_rows(tm, D_MODEL, nt), _rows(tm, 1536, nt), _halo(tm, 1536, nt),
                  _rows(tm, 1536, nt), _rows(tm, CONV_W, nt), _rows(tm, ATTN_W, nt), _rows(tm, ATTN_W, nt),
                  _rows(tm, ATTN_W, nt),
                  _const((3, CONV_W)), _const((1, CONV_W)), _const((1, CONV_W)), _const((1, D_MODEL)),
                  _const((ATTN_W, ATTN_W)), _const((1, ATTN_W)), _const((1, ATTN_W))],
        out_specs=[_rows(tm, 3072, nt),
                   acc(CONV_W, 3), acc(CONV_W), acc(CONV_W), acc(ATTN_W), acc(ATTN_W),
                   pl.BlockSpec(memory_space=pl.ANY), pl.BlockSpec(memory_space=pl.ANY)],
        out_shape=[jax.ShapeDtypeStruct((t, 3072), BF), jax.ShapeDtypeStruct((3, CONV_W), F32),
                   jax.ShapeDtypeStruct((1, CONV_W), F32), jax.ShapeDtypeStruct((1, CONV_W), F32),
                   jax.ShapeDtypeStruct((1, ATTN_W), F32), jax.ShapeDtypeStruct((1, ATTN_W), F32),
                   jax.ShapeDtypeStruct((D_MODEL, 3072), F32), jax.ShapeDtypeStruct((D_MODEL, 3072), BF)],
        scratch_shapes=[pltpu.VMEM((tm + 8, CONV_W), F32), pltpu.VMEM((tm + 8, CONV_W), F32),
                        pltpu.VMEM((D_MODEL, 3072), F32), pltpu.VMEM((D_MODEL, 3072), BF),
                        pltpu.SemaphoreType.DMA((2,))],
        compiler_params=_params(56),
    )(x, zbcx, zbcx, qkv, dycn, dq, dk, dv, conv_w, conv_b, g_oc, g_mix, gm, gq8, gk8)


def _bwd_in(x, dx1, dz, win4, g_mix, tm, riders=()):
    t = x.shape[0]
    nt = t // tm
    nr = len(riders)

    def body(*refs):
        x_ref, dx1_ref, dz_ref, w_ref, g_ref = refs[0:5]
        gx_ref, dg_ref = refs[5 + nr:7 + nr]
        i = pl.program_id(0)
        if nr:
            _ride_scatter(i == 0, i == nt - 1, riders, refs[5:5 + nr], refs[7 + nr:7 + 2 * nr],
                          *refs[7 + 2 * nr:9 + 2 * nr])

        @pl.when(i == 0)
        def _():
            dg_ref[...] = jnp.zeros_like(dg_ref)

        dh1 = jnp.zeros((tm, D_MODEL), F32)
        for s in range(N_CHIPS):
            dh1 = dh1 + _mm_nt(dz_ref[:, s * IN_SLAB:(s + 1) * IN_SLAB], w_ref[s])
        xt = x_ref[...]
        r1 = _rstd(xt)
        xh = xt * r1
        dg_ref[...] += _colsum(dh1 * xh)
        gx_ref[...] = dx1_ref[...] + _norm_bwd(dh1, xh, r1, g_ref[...])

    r_in, r_out, r_sems = _rider_specs(riders)
    res = pl.pallas_call(
        body, name="bwd_in", grid=(nt,),
        in_specs=[_rows(tm, D_MODEL), _rows(tm, D_MODEL), _rows(tm, 3072), _const((N_CHIPS, D_MODEL, IN_SLAB)),
                  _const((1, D_MODEL))] + r_in,
        out_specs=[_rows(tm, D_MODEL), pl.BlockSpec((1, D_MODEL), lambda i: (0, 0))] + r_in,
        out_shape=[jax.ShapeDtypeStruct((t, D_MODEL), F32), jax.ShapeDtypeStruct((1, D_MODEL), F32)] + r_out,
        scratch_shapes=r_sems,
        compiler_params=_params(48),
    )(x, dx1, dz, win4, g_mix, *[g for g, _ in riders])
    return res[0], res[1], list(res[2:])


def _wgrad(a, b, tn, tt, name):
    t, k = a.shape
    n = b.shape[1]
    nt = t // tt

    def body(a_ref, b_ref, o_ref, ob_ref):
        @pl.when(pl.program_id(1) == 0)
        def _():
            o_ref[...] = jnp.zeros_like(o_ref)

        o_ref[...] += _mm_tn(a_ref[...].astype(BF), b_ref[...].astype(BF))

        @pl.when(pl.program_id(1) == nt - 1)
        def _():
            ob_ref[...] = o_ref[...].astype(BF)

    spec = pl.BlockSpec((k, tn), lambda j, i: (0, j))
    return pl.pallas_call(
        body, name=name, grid=(n // tn, nt),
        in_specs=[pl.BlockSpec((tt, k), lambda j, i: (i, 0)), pl.BlockSpec((tt, tn), lambda j, i: (i, j))],
        out_specs=[spec, spec],
        out_shape=[jax.ShapeDtypeStruct((k, n), F32), jax.ShapeDtypeStruct((k, n), BF)],
        compiler_params=_params(48, 2),
    )(a, b)


def _gather_weights(shards, pack):
    nw = len(shards)

    def body(*refs):
        ins = refs[:nw]
        pack_ref = refs[nw]
        outs = refs[nw + 1:2 * nw + 1]
        pack_out = refs[2 * nw + 1]
        send_sems, recv_sems, local_sems = refs[2 * nw + 2:]
        x, y, c = _place()
        me = 2 * x + y
        local, remote = [], []

        def sem(w, j):
            return w * 6 + j

        def push(src, dst, w, j, to):
            return pltpu.make_async_remote_copy(src_ref=src, dst_ref=dst, send_sem=send_sems.at[sem(w, j)],
                                                recv_sem=recv_sems.at[sem(w, j)], device_id=to, device_id_type=MESH)

        def half_rows(w, h):
            half = ins[w].shape[0] // 2
            return pl.ds(pl.multiple_of(h * half, 16), half)

        for w in range(nw):
            local.append(pltpu.make_async_copy(ins[w], outs[w].at[me], local_sems.at[w]))
            for k in (1, 2, 3):
                px, py = _chip_peer(x, y, k)
                mine = half_rows(w, c)
                remote.append(push(ins[w].at[mine], outs[w].at[me, mine], w, k - 1, (px, py, c)))
        local.append(pltpu.make_async_copy(pack_ref, pack_out.at[me], local_sems.at[nw]))
        for k in (1, 2, 3):
            px, py = _chip_peer(x, y, k)
            remote.append(push(pack_ref, pack_out.at[me], nw, k - 1, (px, py, c)))
        for cp in local + remote:
            cp.start()
        for w in range(nw):
            for k in (1, 2, 3):
                landed = outs[w].at[me ^ k, half_rows(w, c)]
                push(landed, landed, w, k - 1, (x, y, c)).wait_recv()
                fw = push(landed, landed, w, 2 + k, (x, y, 1 - c))
                fw.start()
                remote.append(fw)
        for k in (1, 2, 3):
            landed = pack_out.at[me ^ k]
            push(landed, landed, nw, k - 1, (x, y, c)).wait_recv()
        for w in range(nw):
            for k in (1, 2, 3):
                landed = outs[w].at[me ^ k, half_rows(w, 1 - c)]
                push(landed, landed, w, 2 + k, (x, y, c)).wait_recv()
        for cp in remote:
            cp.wait_send()
        for cp in local:
            cp.wait()

    any_spec = pl.BlockSpec(memory_space=pl.ANY)
    out_shape = [jax.ShapeDtypeStruct((N_CHIPS,) + s.shape, s.dtype) for s in shards]
    out_shape.append(jax.ShapeDtypeStruct((N_CHIPS,) + pack.shape, pack.dtype))
    return pl.pallas_call(
        body, name="gather_weights",
        in_specs=[any_spec] * (nw + 1), out_specs=[any_spec] * (nw + 1), out_shape=out_shape,
        scratch_shapes=[pltpu.SemaphoreType.DMA(((nw + 1) * 6,)), pltpu.SemaphoreType.DMA(((nw + 1) * 6,)),
                        pltpu.SemaphoreType.DMA((nw + 1,))],
    )(*shards, pack)


def _adamw(w, g, m, v):
    m = ADAM_B1 * m + (1.0 - ADAM_B1) * g
    v = ADAM_B2 * v + (1.0 - ADAM_B2) * (g * g)
    m_hat = m / (1.0 - ADAM_B1 ** ADAM_STEP)
    v_hat = v / (1.0 - ADAM_B2 ** ADAM_STEP)
    delta = -ADAM_LR * (m_hat / (jnp.sqrt(v_hat) + ADAM_EPS) + ADAM_WD * w)
    return delta, m, v


def _finish_reduce(grad, slots, col_sharded, name):
    r, cw = _piece_shape(grad.shape, col_sharded)
    chunk = 32
    assert r % chunk == 0

    def body(g_hbm, slots_ref, full, own, lsem, c_send, c_recv):
        x, y, c = _place()
        cp = pltpu.make_async_copy(g_hbm.at[_piece_window(col_sharded, r, cw, 2 * x + y, c)], own, lsem)
        cp.start()
        cp.wait()
        mine = pl.multiple_of(c * r, 8)

        def add(j, carry):
            rows = pl.ds(pl.multiple_of(j * chunk, 8), chunk)
            tot = own[rows, :]
            for k in range(7):
                tot = tot + slots_ref[k, rows, :].astype(F32)
            full[pl.ds(mine + pl.multiple_of(j * chunk, 8), chunk), :] = tot
            return carry

        lax.fori_loop(0, r // chunk, add, 0)
        half = full.at[pl.ds(mine, r), :]
        swap = pltpu.make_async_remote_copy(src_ref=half, dst_ref=half, send_sem=c_send, recv_sem=c_recv,
                                            device_id=(x, y, 1 - c), device_id_type=MESH)
        swap.start()
        swap.wait()

    vmem = pl.BlockSpec(memory_space=pltpu.VMEM)
    return pl.pallas_call(
        body, name=name, in_specs=[pl.BlockSpec(memory_space=pl.ANY), vmem], out_specs=vmem,
        out_shape=jax.ShapeDtypeStruct((2 * r, cw), F32),
        scratch_shapes=[pltpu.VMEM((r, cw), F32), pltpu.SemaphoreType.DMA, pltpu.SemaphoreType.DMA,
                        pltpu.SemaphoreType.DMA],
        compiler_params=pltpu.CompilerParams(vmem_limit_bytes=32 * MIB),
    )(grad, slots)


def _adamw_big(g, w, m, v, name):
    vr, vc = w.shape
    assert g.shape == w.shape
    rows = 64

    def body(g_ref, w_ref, m_ref, v_ref, go_ref, do_ref, mo_ref, vo_ref):
        gg = g_ref[...]
        delta, mn, vn = _adamw(w_ref[...], gg, m_ref[...], v_ref[...])
        go_ref[...] = gg
        do_ref[...] = delta
        mo_ref[...] = mn
        vo_ref[...] = vn

    blk = pl.BlockSpec((rows, vc), lambda i: (i, 0))
    shard = jax.ShapeDtypeStruct((vr, vc), F32)
    return pl.pallas_call(
        body, name=name, grid=(vr // rows,),
        in_specs=[blk, blk, blk, blk], out_specs=[blk] * 4,
        out_shape=[shard] * 4, compiler_params=_params(32),
    )(g, w, m, v)


def _allreduce_small(pack):
    rows = pack.shape[0]

    def body(p_ref, o_ref, slots, send_sems, recv_sems):
        x, y, c = _place()
        me = 4 * x + 2 * y + c
        slots[me] = p_ref[...]
        sends = []
        for k in range(1, 8):
            cp = pltpu.make_async_remote_copy(
                src_ref=p_ref, dst_ref=slots.at[me], send_sem=send_sems.at[k - 1], recv_sem=recv_sems.at[k - 1],
                device_id=(x ^ (k >> 2), y ^ ((k >> 1) & 1), c ^ (k & 1)), device_id_type=MESH)
            cp.start()
            sends.append(cp)
        for cp in sends:
            cp.wait()
        tot = slots[0]
        for j in range(1, 8):
            tot = tot + slots[j]
        o_ref[...] = tot

    vmem = pl.BlockSpec(memory_space=pltpu.VMEM)
    return pl.pallas_call(
        body, name="allreduce_small", in_specs=[vmem], out_specs=vmem,
        out_shape=jax.ShapeDtypeStruct(pack.shape, F32),
        scratch_shapes=[pltpu.VMEM((8, rows, D_MODEL), F32), pltpu.SemaphoreType.DMA((7,)),
                        pltpu.SemaphoreType.DMA((7,))],
    )(pack)


def _adamw_small(ws, gs, ms, vs):
    n = len(ws)

    def body(*refs):
        w_refs, g_refs, m_refs, v_refs = refs[0:n], refs[n:2 * n], refs[2 * n:3 * n], refs[3 * n:4 * n]
        d_refs, mo_refs, vo_refs = refs[4 * n:5 * n], refs[5 * n:6 * n], refs[6 * n:7 * n]
        for j in range(n):
            delta, mn, vn = _adamw(w_refs[j][...], g_refs[j][...], m_refs[j][...], v_refs[j][...])
            d_refs[j][...] = delta
            mo_refs[j][...] = mn
            vo_refs[j][...] = vn

    vmem = pl.BlockSpec(memory_space=pltpu.VMEM)
    shapes = [jax.ShapeDtypeStruct(w.shape, F32) for w in ws]
    outs = pl.pallas_call(
        body, name="adamw_small", in_specs=[vmem] * (4 * n), out_specs=[vmem] * (3 * n), out_shape=shapes * 3,
    )(*ws, *gs, *ms, *vs)
    return outs[0:n], outs[n:2 * n], outs[2 * n:3 * n]


def _local_step(x, p, target, wts, late=None):
    (win4, wout, wg4, wu4, wd4, wpg, wpp4, conv_w, fcw, g_mix, conv_b, gq, gk, g_oc, g_oa, g_ffn, fcb, g_ple) = wts
    comm = late is not None
    gm = jnp.kron(jnp.eye(N_HEADS, dtype=F32), jnp.full((HEAD_DIM, HEAD_DIM), 1.0 / HEAD_DIM, F32)).astype(BF)
    gq8, gk8 = jnp.tile(gq, (1, N_HEADS)), jnp.tile(gk, (1, N_HEADS))
    mb = _mask_table()
    zbcx, qkv, ycn, qkn = _fwd_mix(x, g_mix, win4, conv_w, conv_b, g_oc, gm, gq8, gk8, 512)
    ya, lse, gathered = _attn_fwd(qkn, qkv, mb, late[0:3] if comm else ())
    if comm:
        wout, wg4, wu4 = (g.reshape(-1, D_MODEL) for g in gathered)
    x1, gp, up, gate, act, ycat, h2, gathered = _fwd_ffn(x, ycn, ya, wout, wg4, wu4, g_oa, g_ffn, fcw, fcb, 256,
                                                    late[3:6] if comm else ())
    if comm:
        wd4, wpg, wpp4 = gathered
        wd4, wpg = wd4.reshape(D_FF, D_MODEL), wpg.reshape(D_MODEL, D_MODEL)
    dx2, h3, ds, dpp, dg_ple, loss = _fwd_tail(x1, act, p, target, wd4, wpg, wpp4, g_ple, 512)
    big, big16, slots = {}, {}, {}

    def wgrad(name, a, b, tn):
        big[name], big16[name] = _wgrad(a, b, tn, 1024, "wgrad_" + name)
        return (big16[name], _COL_SHARDED[name])

    riders = [wgrad("w_down", act, dx2, 512), wgrad("w_ple_gate", h3, ds, 1024), wgrad("w_ple_proj", p, dpp, 1024)]
    dgp, dup, dfcw, dfcb, got = _bwd_ffn_a(dx2, gate, gp, up, wd4, fcw, 512, riders if comm else ())
    slots.update(zip(("w_down", "w_ple_gate", "w_ple_proj"), got))
    riders = [wgrad("w_gate", dgp, h2, 512), wgrad("w_up", dup, h2, 512)]
    dx1, dycn, dya, dg_ffn, dg_oa = _bwd_ffn_b(dgp, dup, dx2, x1, ya, wg4, wu4, wout, g_ffn, g_oa, 512)
    riders.append(wgrad("w_out", ycat, dx1, 1024))
    dq, dk, dv, got = _attn_bwd(qkn, qkv, ya, lse, dya, mb, riders if comm else ())
    slots.update(zip(("w_gate", "w_up", "w_out"), got))
    dz, dcw, dcb, dg_oc, dgq8, dgk8, big["w_in"], big16["w_in"] = _bwd_mix(
        x, zbcx, qkv, dycn, dq, dk, dv, conv_w, conv_b, g_oc, g_mix, gm, gq8, gk8, 512)
    riders = [(big16["w_in"], _COL_SHARDED["w_in"])]
    grad_x, dg_mix, got = _bwd_in(x, dx1, dz, win4, g_mix, 512, riders if comm else ())
    slots.update(zip(("w_in",), got))
    dgq = dgq8.reshape(N_HEADS, HEAD_DIM).sum(axis=0, keepdims=True)
    dgk = dgk8.reshape(N_HEADS, HEAD_DIM).sum(axis=0, keepdims=True)
    small = dict(g_mix=dg_mix, conv_w=dcw, conv_b=dcb, q_norm_g=dgq, k_norm_g=dgk, g_out_conv=dg_oc,
                 g_out_attn=dg_oa, g_ffn=dg_ffn, ffn_conv_w=dfcw, ffn_conv_b=dfcb, g_ple=dg_ple)
    return loss[0, 0], grad_x, big, slots, small


_SMALL_ROWS = 24


def _pack_small(s):
    z64 = jnp.zeros((1, 1024 - 512 - 128), F32)
    rows = [s["g_mix"], s["g_ffn"], s["g_ple"],
            jnp.concatenate([s["conv_b"], s["g_out_conv"]], axis=1),
            jnp.concatenate([s["g_out_attn"], s["q_norm_g"], s["k_norm_g"], z64], axis=1),
            jnp.pad(s["conv_w"], ((0, 0), (0, 512))),
            jnp.pad(s["ffn_conv_b"], ((0, 0), (0, 3072 - D_FF))).reshape(3, 1024),
            jnp.pad(s["ffn_conv_w"], ((0, 0), (0, 3072 - D_FF))).reshape(9, 1024),
            jnp.zeros((_SMALL_ROWS - 20, 1024), F32)]
    return jnp.concatenate(rows, axis=0)


def _unpack_small(t):
    return dict(g_mix=t[0:1], g_ffn=t[1:2], g_ple=t[2:3], conv_b=t[3:4, 0:512], g_out_conv=t[3:4, 512:1024],
                g_out_attn=t[4:5, 0:512], q_norm_g=t[4:5, 512:576], k_norm_g=t[4:5, 576:640],
                conv_w=t[5:8, 0:512], ffn_conv_b=t[8:11].reshape(1, 3072)[:, :D_FF],
                ffn_conv_w=t[11:20].reshape(3, 3072)[:, :D_FF])


_BIG = ("w_in", "w_out", "w_gate", "w_up", "w_down", "w_ple_gate", "w_ple_proj")
_COL_SHARDED = dict(w_in=True, w_out=False, w_gate=False, w_up=False, w_down=False, w_ple_gate=False, w_ple_proj=True)
_TRANSPOSED = ("w_gate", "w_up")
_WEIGHTS = ("g_mix", "w_in", "conv_w", "conv_b", "q_norm_g", "k_norm_g", "g_out_conv", "g_out_attn", "w_out",
            "g_ffn", "w_gate", "w_up", "ffn_conv_w", "ffn_conv_b", "w_down", "g_ple", "w_ple_gate", "w_ple_proj")


def kernel(x, p, g_mix, w_in, conv_w, conv_b, q_norm_g, k_norm_g, g_out_conv, g_out_attn, w_out, g_ffn, w_gate, w_up, ffn_conv_w, ffn_conv_b, w_down, g_ple, w_ple_gate, w_ple_proj, loss_target, m_g_mix, m_w_in, m_conv_w, m_conv_b, m_q_norm_g, m_k_norm_g, m_g_out_conv, m_g_out_attn, m_w_out, m_g_ffn, m_w_gate, m_w_up, m_ffn_conv_w, m_ffn_conv_b, m_w_down, m_g_ple, m_w_ple_gate, m_w_ple_proj, v_g_mix, v_w_in, v_conv_w, v_conv_b, v_q_norm_g, v_k_norm_g, v_g_out_conv, v_g_out_attn, v_w_out, v_g_ffn, v_w_gate, v_w_up, v_ffn_conv_w, v_ffn_conv_b, v_w_down, v_g_ple, v_w_ple_gate, v_w_ple_proj):
    w = dict(g_mix=g_mix, w_in=w_in, conv_w=conv_w, conv_b=conv_b, q_norm_g=q_norm_g, k_norm_g=k_norm_g,
             g_out_conv=g_out_conv, g_out_attn=g_out_attn, w_out=w_out, g_ffn=g_ffn, w_gate=w_gate, w_up=w_up,
             ffn_conv_w=ffn_conv_w, ffn_conv_b=ffn_conv_b, w_down=w_down, g_ple=g_ple, w_ple_gate=w_ple_gate,
             w_ple_proj=w_ple_proj)
    m = dict(g_mix=m_g_mix, w_in=m_w_in, conv_w=m_conv_w, conv_b=m_conv_b, q_norm_g=m_q_norm_g, k_norm_g=m_k_norm_g,
             g_out_conv=m_g_out_conv, g_out_attn=m_g_out_attn, w_out=m_w_out, g_ffn=m_g_ffn, w_gate=m_w_gate,
             w_up=m_w_up, ffn_conv_w=m_ffn_conv_w, ffn_conv_b=m_ffn_conv_b, w_down=m_w_down, g_ple=m_g_ple,
             w_ple_gate=m_w_ple_gate, w_ple_proj=m_w_ple_proj)
    v = dict(g_mix=v_g_mix, w_in=v_w_in, conv_w=v_conv_w, conv_b=v_conv_b, q_norm_g=v_q_norm_g, k_norm_g=v_k_norm_g,
             g_out_conv=v_g_out_conv, g_out_attn=v_g_out_attn, w_out=v_w_out, g_ffn=v_g_ffn, w_gate=v_w_gate,
             w_up=v_w_up, ffn_conv_w=v_ffn_conv_w, ffn_conv_b=v_ffn_conv_b, w_down=v_w_down, g_ple=v_g_ple,
             w_ple_gate=v_w_ple_gate, w_ple_proj=v_w_ple_proj)
    mats = [k for k, a in w.items() if a.ndim == 3]
    w = {k: (a[0] if k in mats else a) for k, a in w.items()}
    m = {k: (a[0] if k in mats else a) for k, a in m.items()}
    v = {k: (a[0] if k in mats else a) for k, a in v.items()}
    for n in _TRANSPOSED:
        w[n], m[n], v[n] = w[n].T, m[n].T, v[n].T
    chip = 2 * lax.axis_index("x") + lax.axis_index("y")

    late = [w[n].astype(BF) for n in ("w_out", "w_gate", "w_up", "w_down", "w_ple_gate", "w_ple_proj")]
    pack = jnp.pad(jnp.concatenate([w["conv_w"], w["ffn_conv_w"]], axis=1), ((0, 5), (0, 1024 - 128 - D_FF_SHARD)))
    win4, pack4 = _gather_weights([w["w_in"].astype(BF)], pack)
    conv_w_full = pack4[:, 0:3, 0:128].transpose(1, 0, 2).reshape(3, CONV_W)
    fcw_full = pack4[:, 0:3, 128:128 + D_FF_SHARD].transpose(1, 0, 2).reshape(3, D_FF)
    wts = (win4, None, None, None, None, None, None, conv_w_full, fcw_full, w["g_mix"], w["conv_b"], w["q_norm_g"],
           w["k_norm_g"], w["g_out_conv"], w["g_out_attn"], w["g_ffn"], w["ffn_conv_b"], w["g_ple"])

    loss, grad_x, big, slots, small = _local_step(x[0], p[0, 0], loss_target[0], wts, late)
    loss = lax.psum(loss, ("x", "y", "c"))

    grads, deltas, new_m, new_v = {}, {}, {}, {}
    for name in _BIG:
        total = _finish_reduce(big[name], slots[name], _COL_SHARDED[name], "finish_" + name)
        grads[name], deltas[name], new_m[name], new_v[name] = _adamw_big(total, w[name], m[name], v[name],
                                                                         "adamw_" + name)
    tot = _unpack_small(_allreduce_small(_pack_small(small)))
    tot["conv_w"] = lax.dynamic_slice_in_dim(tot["conv_w"], chip * 128, 128, axis=1)
    tot["ffn_conv_w"] = lax.dynamic_slice_in_dim(tot["ffn_conv_w"], chip * D_FF_SHARD, D_FF_SHARD, axis=1)
    names = [n for n in _WEIGHTS if n not in _BIG]
    d_s, m_s, v_s = _adamw_small([w[n] for n in names], [tot[n] for n in names], [m[n] for n in names],
                                 [v[n] for n in names])
    for j, n in enumerate(names):
        grads[n], deltas[n], new_m[n], new_v[n] = tot[n], d_s[j], m_s[j], v_s[j]

    out = [loss, grad_x[None]]
    for group in (grads, deltas, new_m, new_v):
        for n in _TRANSPOSED:
            group[n] = group[n].T
        out += [group[n][None] if n in mats else group[n] for n in _WEIGHTS]
    return tuple(out)
```

```python
import jax
import jax.numpy as jnp
from jax import lax
from jax.experimental import pallas as pl
from jax.experimental.pallas import tpu as pltpu

D_MODEL = 1024
CONV_W = 512
N_HEADS = 8
HEAD_DIM = 64
ATTN_W = 512
D_FF = 2816
D_FF_SHARD = 704
FF_SLABS = ((0, 1408), (1408, 2816))
IN_SLAB = 768
PLE_DIM = 256
N_CHIPS = 4
QBLK = 128
DILATIONS = (1, 4, 16)
EPS = 1e-6
NEG = -1e30
MESH = pl.DeviceIdType.MESH

ADAM_LR = 0.001
ADAM_B1 = 0.9
ADAM_B2 = 0.999
ADAM_EPS = 1e-08
ADAM_WD = 0.01
ADAM_STEP = 10

BF = jnp.bfloat16
F32 = jnp.float32
MIB = 1024 * 1024


def _mm(a, b):
    return jnp.dot(a, b, preferred_element_type=F32)


def _mm_nt(a, b):
    return lax.dot_general(a, b, (((1,), (1,)), ((), ())), preferred_element_type=F32)


def _mm_tn(a, b):
    return lax.dot_general(a, b, (((0,), (0,)), ((), ())), preferred_element_type=F32)


def _rstd(a):
    return lax.rsqrt(jnp.mean(a * a, axis=-1, keepdims=True) + EPS)


def _norm_bwd(dy, xh, r, g):
    dxh = dy * g
    return r * (dxh - xh * jnp.mean(dxh * xh, axis=-1, keepdims=True))


def _colsum(a):
    return jnp.sum(a, axis=0, keepdims=True)


def _head_mean(a, gm_ref):
    return _mm(a.astype(BF), gm_ref[...])


def _shift_down(buf, k, tm):
    return pltpu.roll(buf, k, axis=0)[8:8 + tm]


def _shift_up(buf, k, tm):
    return pltpu.roll(buf, tm + 8 - k, axis=0)[0:tm]


def _params(vmem_mib, n_grid=1):
    return pltpu.CompilerParams(dimension_semantics=("arbitrary",) * n_grid, vmem_limit_bytes=vmem_mib * MIB)


def _const(shape):
    n = len(shape)
    return pl.BlockSpec(shape, lambda *_: (0,) * n, pipeline_mode=pl.Buffered(1))


def _rows(tm, width, rev_of=None):
    if rev_of is None:
        return pl.BlockSpec((tm, width), lambda i: (i, 0))
    return pl.BlockSpec((tm, width), lambda i: (rev_of - 1 - i, 0))


def _halo(tm, width, nt):
    return pl.BlockSpec((8, width), lambda i: (jnp.maximum((nt - 1 - i) * (tm // 8) - 1, 0), 0))


def _fwd_mix(x, g_mix, win4, conv_w, conv_b, g_oc, gm, gq8, gk8, tm):
    t = x.shape[0]
    nt = t // tm

    def body(x_ref, g_ref, w_ref, cw_ref, cb_ref, goc_ref, gm_ref, gq_ref, gk_ref,
             zbcx_ref, qkv_ref, ycn_ref, qkn_ref, ubuf):
        @pl.when(pl.program_id(0) == 0)
        def _():
            ubuf[0:8, :] = jnp.zeros((8, CONV_W), F32)

        xt = x_ref[...]
        h = ((xt * _rstd(xt)) * g_ref[...]).astype(BF)
        zbcx_ref[:, 0:IN_SLAB] = _mm(h, w_ref[0])
        zbcx_ref[:, IN_SLAB:2 * IN_SLAB] = _mm(h, w_ref[1])
        qkv_ref[:, 0:IN_SLAB] = _mm(h, w_ref[2])
        qkv_ref[:, IN_SLAB:2 * IN_SLAB] = _mm(h, w_ref[3])
        u = zbcx_ref[:, 512:1024] * zbcx_ref[:, 1024:1536]
        ubuf[8:8 + tm, :] = u
        ub = ubuf[...]
        cv = (cw_ref[0:1, :] * _shift_down(ub, 2, tm) + cw_ref[1:2, :] * _shift_down(ub, 1, tm)
              + cw_ref[2:3, :] * u + cb_ref[...])
        ubuf[0:8, :] = ubuf[tm:tm + 8, :]
        yc = zbcx_ref[:, 0:512] * cv
        ycn_ref[...] = ((yc * _rstd(yc)) * goc_ref[...]).astype(BF)
        zq = qkv_ref[:, 0:512]
        zk = qkv_ref[:, 512:1024]
        rq = lax.rsqrt(_head_mean(zq * zq, gm_ref) + EPS)
        rk = lax.rsqrt(_head_mean(zk * zk, gm_ref) + EPS)
        qkn_ref[:, 0:512] = ((zq * rq) * gq_ref[...]) * (HEAD_DIM ** -0.5)
        qkn_ref[:, 512:1024] = (zk * rk) * gk_ref[...]

    return pl.pallas_call(
        body, name="fwd_mix", grid=(nt,),
        in_specs=[_rows(tm, D_MODEL), _const((1, D_MODEL)), _const((N_CHIPS, D_MODEL, IN_SLAB)),
                  _const((3, CONV_W)), _const((1, CONV_W)), _const((1, CONV_W)), _const((ATTN_W, ATTN_W)),
                  _const((1, ATTN_W)), _const((1, ATTN_W))],
        out_specs=[_rows(tm, 1536), _rows(tm, 1536), _rows(tm, CONV_W), _rows(tm, 1024)],
        out_shape=[jax.ShapeDtypeStruct((t, 1536), F32), jax.ShapeDtypeStruct((t, 1536), F32),
                   jax.ShapeDtypeStruct((t, CONV_W), BF), jax.ShapeDtypeStruct((t, 1024), F32)],
        scratch_shapes=[pltpu.VMEM((tm + 8, CONV_W), F32)],
        compiler_params=_params(48),
    )(x, g_mix, win4, conv_w, conv_b, g_oc, gm, gq8, gk8)


def _place():
    x, y, c = lax.axis_index("x"), lax.axis_index("y"), lax.axis_index("c")
    return x, y, c


def _chip_peer(x, y, k):
    return x ^ (k >> 1), y ^ (k & 1)


def _piece_shape(grad_shape, col_sharded):
    kk, nn = grad_shape
    return (kk // 2, nn // N_CHIPS) if col_sharded else (kk // (2 * N_CHIPS), nn)


def _piece_window(col_sharded, r, cw, s, h):
    if col_sharded:
        return (pl.ds(pl.multiple_of(h * r, 16), r), pl.ds(pl.multiple_of(s * cw, 128), cw))
    return (pl.ds(pl.multiple_of((2 * s + h) * r, 16), r), slice(None))


def _scatter_copies(g_ref, slots_ref, send_sems, recv_sems, base, col_sharded):
    x, y, c = _place()
    r, cw = slots_ref.shape[1:]
    copies = []
    for k in range(1, 8):
        tx, ty, tc = x ^ (k >> 2), y ^ ((k >> 1) & 1), c ^ (k & 1)
        copies.append(pltpu.make_async_remote_copy(
            src_ref=g_ref.at[_piece_window(col_sharded, r, cw, 2 * tx + ty, tc)], dst_ref=slots_ref.at[k - 1],
            send_sem=send_sems.at[base + k - 1], recv_sem=recv_sems.at[base + k - 1],
            device_id=(tx, ty, tc), device_id_type=MESH))
    return copies


def _ride_scatter(first, last, riders, g_refs, slot_refs, send_sems, recv_sems):
    def all_copies():
        out = []
        for j, (_, col_sharded) in enumerate(riders):
            out += _scatter_copies(g_refs[j], slot_refs[j], send_sems, recv_sems, 7 * j, col_sharded)
        return out

    @pl.when(first)
    def _():
        for cp in all_copies():
            cp.start()

    @pl.when(last)
    def _():
        for cp in all_copies():
            cp.wait()


def _rider_specs(riders):
    any_spec = pl.BlockSpec(memory_space=pl.ANY)
    shapes = [jax.ShapeDtypeStruct((7,) + _piece_shape(g.shape, cs), BF) for g, cs in riders]
    sems = [pltpu.SemaphoreType.DMA((7 * len(riders),)), pltpu.SemaphoreType.DMA((7 * len(riders),))] if riders else []
    return [any_spec] * len(riders), shapes, sems


class _Gather:
    def __init__(self, ins, outs, send_sems, recv_sems, local_sems):
        self.ins, self.outs = ins, outs
        self.send_sems, self.recv_sems, self.local_sems = send_sems, recv_sems, local_sems
        self.x, self.y, self.c = _place()
        self.me = 2 * self.x + self.y

    def _push(self, src, dst, w, j, to):
        return pltpu.make_async_remote_copy(src_ref=src, dst_ref=dst, send_sem=self.send_sems.at[6 * w + j],
                                            recv_sem=self.recv_sems.at[6 * w + j], device_id=to, device_id_type=MESH)

    def _half(self, w, h):
        half = self.ins[w].shape[0] // 2
        return pl.ds(pl.multiple_of(h * half, 16), half)

    def _local(self, w):
        return pltpu.make_async_copy(self.ins[w], self.outs[w].at[self.me], self.local_sems.at[w])

    def _ici(self, w, k):
        px, py = _chip_peer(self.x, self.y, k)
        mine = self._half(w, self.c)
        return self._push(self.ins[w].at[mine], self.outs[w].at[self.me, mine], w, k - 1, (px, py, self.c))

    def _landed(self, w, k, h):
        return self.outs[w].at[self.me ^ k, self._half(w, h)]

    def _fwd(self, w, k):
        landed = self._landed(w, k, self.c)
        return self._push(landed, landed, w, 2 + k, (self.x, self.y, 1 - self.c))

    def start(self):
        for w in range(len(self.ins)):
            self._local(w).start()
            for k in (1, 2, 3):
                self._ici(w, k).start()

    def forward(self):
        for w in range(len(self.ins)):
            for k in (1, 2, 3):
                landed = self._landed(w, k, self.c)
                self._push(landed, landed, w, k - 1, (self.x, self.y, self.c)).wait_recv()
                self._fwd(w, k).start()

    def finish(self):
        for w in range(len(self.ins)):
            for k in (1, 2, 3):
                landed = self._landed(w, k, 1 - self.c)
                self._push(landed, landed, w, 2 + k, (self.x, self.y, self.c)).wait_recv()
            for k in (1, 2, 3):
                self._ici(w, k).wait_send()
                self._fwd(w, k).wait_send()
            self._local(w).wait()


def _alibi(h):
    return 2.0 ** (-(h + 1))


CHUNK = 2048


def _mask_table():
    slopes = jnp.asarray([_alibi(h) for h in range(N_HEADS)], F32)[:, None, None]
    step = jnp.arange(QBLK)[:, None] + QBLK - jnp.arange(2 * QBLK)[None, :]
    valid = (step >= 0) & (step <= QBLK)
    tab = jnp.stack([jnp.where(valid[None], -slopes * (step * d)[None].astype(F32), NEG) for d in DILATIONS])
    return tab.reshape(3, N_HEADS // 2, 2 * QBLK, 2 * QBLK)


def _attn_fwd(qkn, qkv, mb, late=()):
    t = qkn.shape[0]
    nc = t // CHUNK
    nl = len(late)

    def body(*refs):
        qc_ref, kp_ref, kc_ref, vp_ref, vc_ref, mb_ref = refs[0:6]
        o_ref, l_ref = refs[6 + nl:8 + nl]
        ob0, ob1, ob2, lb0, lb1, lb2 = refs[8 + 2 * nl:14 + 2 * nl]
        if nl:
            gather = _Gather(refs[6:6 + nl], refs[8 + nl:8 + 2 * nl], *refs[14 + 2 * nl:17 + 2 * nl])
            step = pl.program_id(0) * nc + pl.program_id(1)
            pl.when(step == 0)(gather.start)
            pl.when(step == 2 * nc)(gather.forward)
            pl.when(step == (N_HEADS // 2) * nc - 1)(gather.finish)
        first = pl.program_id(1) == 0
        lane = lax.broadcasted_iota(jnp.int32, (QBLK, 128), 1)
        lo_half = lane < HEAD_DIM
        kj = lax.broadcasted_iota(jnp.int32, (2 * QBLK, 2 * QBLK), 1)
        no_prev = first & (kj < QBLK)
        obs, lbs = (ob0, ob1, ob2), (lb0, lb1, lb2)

        def by_head(a):
            return jnp.where(lo_half, a, 0.0).astype(BF), jnp.where(lo_half, 0.0, a).astype(BF)

        for di, d in enumerate(DILATIONS):
            span = d * QBLK
            for r in range(d):
                tail = pl.ds(CHUNK - span + r, QBLK, stride=d)
                k_prev = kp_ref[tail, :].astype(BF)
                v_prev = by_head(vp_ref[tail, :])
                for b in range(CHUNK // span):
                    rows = pl.ds(r + span * b, QBLK, stride=d)
                    q0, q1 = by_head(qc_ref[rows, :])
                    k_cur = kc_ref[rows, :].astype(BF)
                    v_cur = by_head(vc_ref[rows, :])
                    s = _mm_nt(jnp.concatenate([q0, q1], axis=0), jnp.concatenate([k_prev, k_cur], axis=0))
                    s = s + mb_ref[di, 0]
                    if b == 0:
                        s = jnp.where(no_prev, NEG, s)
                    m = jnp.max(s, axis=-1, keepdims=True)
                    e = jnp.exp(s - m)
                    den = jnp.sum(e, axis=-1, keepdims=True)
                    eb = e.astype(BF)
                    o = _mm(jnp.concatenate([eb[0:QBLK], eb[QBLK:2 * QBLK]], axis=1),
                            jnp.concatenate([v_prev[0], v_cur[0], v_prev[1], v_cur[1]], axis=0))
                    inv = 1.0 / den
                    lse = m + jnp.log(den)
                    obs[di][rows, :] = o * jnp.where(lo_half, inv[0:QBLK], inv[QBLK:2 * QBLK])
                    lbs[di][rows, :] = jnp.where(lo_half, lse[0:QBLK], lse[QBLK:2 * QBLK])
                    k_prev, v_prev = k_cur, v_cur
        for c0 in range(0, CHUNK, 256):
            rs = slice(c0, c0 + 256)
            l0, l1, l2 = lb0[rs, :], lb1[rs, :], lb2[rs, :]
            mx = jnp.maximum(jnp.maximum(l0, l1), l2)
            w0, w1, w2 = jnp.exp(l0 - mx), jnp.exp(l1 - mx), jnp.exp(l2 - mx)
            tot = w0 + w1 + w2
            o_ref[rs, :] = (ob0[rs, :] * w0 + ob1[rs, :] * w1 + ob2[rs, :] * w2) / tot
            l_ref[rs, :] = mx + jnp.log(tot)

    def cur(col):
        return pl.BlockSpec((CHUNK, 128), lambda hp, n: (n, col + hp))

    def prv(col):
        return pl.BlockSpec((CHUNK, 128), lambda hp, n: (jnp.maximum(n - 1, 0), col + hp))

    out = pl.BlockSpec((CHUNK, 128), lambda hp, n: (n, hp))
    any_spec = pl.BlockSpec(memory_space=pl.ANY)
    sems = [pltpu.SemaphoreType.DMA((6 * nl,)), pltpu.SemaphoreType.DMA((6 * nl,)), pltpu.SemaphoreType.DMA((nl,))]
    res = pl.pallas_call(
        body, name="attn_fwd", grid=(N_HEADS // 2, nc),
        in_specs=[cur(0), prv(4), cur(4), prv(8), cur(8),
                  pl.BlockSpec((3, 1, 2 * QBLK, 2 * QBLK), lambda hp, n: (0, hp, 0, 0))] + [any_spec] * nl,
        out_specs=[out, out] + [any_spec] * nl,
        out_shape=[jax.ShapeDtypeStruct((t, ATTN_W), F32)] * 2
        + [jax.ShapeDtypeStruct((N_CHIPS,) + w.shape, w.dtype) for w in late],
        scratch_shapes=[pltpu.VMEM((CHUNK, 128), F32)] * 6 + (sems if nl else []),
        compiler_params=_params(48, 2),
    )(qkn, qkn, qkn, qkv, qkv, mb, *late)
    return res[0], res[1], list(res[2:])


def _attn_bwd(qkn, qkv, o, lse, do, mb, riders=()):
    t = qkn.shape[0]
    nc = t // CHUNK
    nr = len(riders)

    def body(*refs):
        (qc_ref, qn_ref, kp_ref, kc_ref, vp_ref, vc_ref, oc_ref, on_ref, lc_ref, ln_ref, dc_ref, dn_ref,
         mb_ref) = refs[0:13]
        dq_ref, dk_ref, dv_ref = refs[13 + nr:16 + nr]
        if nr:
            step = pl.program_id(0) * nc + pl.program_id(1)
            _ride_scatter(step == 0, step == (N_HEADS // 2) * nc - 1, riders, refs[13:13 + nr],
                          refs[16 + nr:16 + 2 * nr], *refs[16 + 2 * nr:18 + 2 * nr])
        first = pl.program_id(1) == 0
        last = pl.program_id(1) == nc - 1
        lane = lax.broadcasted_iota(jnp.int32, (QBLK, 128), 1)
        lo_half = lane < HEAD_DIM
        kj = lax.broadcasted_iota(jnp.int32, (2 * QBLK, 2 * QBLK), 1)
        no_prev = first & (kj < QBLK)

        def by_head(a):
            return jnp.where(lo_half, a, 0.0).astype(BF), jnp.where(lo_half, 0.0, a).astype(BF)

        def query_side(q_ref, d_ref, o_ref_, l_ref_, rows):
            dvals = d_ref[rows, :]
            dd = dvals * o_ref_[rows, :]
            lv = l_ref_[rows, :]
            d0 = jnp.sum(jnp.where(lo_half, dd, 0.0), axis=-1, keepdims=True)
            d1 = jnp.sum(jnp.where(lo_half, 0.0, dd), axis=-1, keepdims=True)
            l0 = jnp.max(jnp.where(lo_half, lv, NEG), axis=-1, keepdims=True)
            l1 = jnp.max(jnp.where(lo_half, NEG, lv), axis=-1, keepdims=True)
            return (jnp.concatenate(by_head(q_ref[rows, :]), axis=0), jnp.concatenate(by_head(dvals), axis=0),
                    jnp.concatenate([l0, l1], axis=0), jnp.concatenate([d0, d1], axis=0))

        def tile(qs, dos, lcol, dcol, keys, vals, bias, dead):
            s = _mm_nt(qs, keys) + bias
            if dead is not None:
                s = jnp.where(dead, NEG, s)
            p = jnp.exp(s - lcol)
            ds = p * (_mm_nt(dos, vals) - dcol)
            return p.astype(BF), ds.astype(BF)

        def put(ref, di, rows, val):
            if di == 0:
                ref[rows, :] = val
            else:
                ref[rows, :] = ref[rows, :] + val

        for di, d in enumerate(DILATIONS):
            span = d * QBLK
            nbk = CHUNK // span
            for r in range(d):
                tail = pl.ds(CHUNK - span + r, QBLK, stride=d)
                k_prev = kp_ref[tail, :]
                kb_prev, km_prev = k_prev.astype(BF), by_head(k_prev)
                vb_prev = vp_ref[tail, :].astype(BF)
                rows_prev, dk_part, dv_part = None, None, None
                for b in range(nbk):
                    rows = pl.ds(r + span * b, QBLK, stride=d)
                    qs, dos, lcol, dcol = query_side(qc_ref, dc_ref, oc_ref, lc_ref, rows)
                    k_cur = kc_ref[rows, :]
                    kb_cur, km_cur = k_cur.astype(BF), by_head(k_cur)
                    vb_cur = vc_ref[rows, :].astype(BF)
                    p, ds = tile(qs, dos, lcol, dcol, jnp.concatenate([kb_prev, kb_cur], axis=0),
                                 jnp.concatenate([vb_prev, vb_cur], axis=0), mb_ref[di, 0],
                                 no_prev if b == 0 else None)
                    put(dq_ref, di, rows,
                        _mm(jnp.concatenate([ds[0:QBLK], ds[QBLK:2 * QBLK]], axis=1),
                            jnp.concatenate([km_prev[0], km_cur[0], km_prev[1], km_cur[1]], axis=0)))
                    dk2 = _mm_tn(ds, qs)
                    dv2 = _mm_tn(p, dos)
                    if b > 0:
                        put(dk_ref, di, rows_prev, dk_part + dk2[0:QBLK])
                        put(dv_ref, di, rows_prev, dv_part + dv2[0:QBLK])
                    rows_prev, dk_part, dv_part = rows, dk2[QBLK:2 * QBLK], dv2[QBLK:2 * QBLK]
                    kb_prev, km_prev, vb_prev = kb_cur, km_cur, vb_cur
                qs, dos, lcol, dcol = query_side(qn_ref, dn_ref, on_ref, ln_ref, pl.ds(r, QBLK, stride=d))
                p, ds = tile(qs, dos, lcol, dcol, kb_prev, vb_prev, mb_ref[di, 0, :, 0:QBLK], last)
                put(dk_ref, di, rows_prev, dk_part + _mm_tn(ds, qs))
                put(dv_ref, di, rows_prev, dv_part + _mm_tn(p, dos))

    def at(shift, col):
        return pl.BlockSpec((CHUNK, 128), lambda hp, n: (jnp.clip(n + shift, 0, nc - 1), col + hp))

    out = pl.BlockSpec((CHUNK, 128), lambda hp, n: (n, hp))
    r_in, r_out, r_sems = _rider_specs(riders)
    res = pl.pallas_call(
        body, name="attn_bwd", grid=(N_HEADS // 2, nc),
        in_specs=[at(0, 0), at(1, 0), at(-1, 4), at(0, 4), at(-1, 8), at(0, 8),
                  at(0, 0), at(1, 0), at(0, 0), at(1, 0), at(0, 0), at(1, 0),
                  pl.BlockSpec((3, 1, 2 * QBLK, 2 * QBLK), lambda hp, n: (0, hp, 0, 0))] + r_in,
        out_specs=[out, out, out] + r_in,
        out_shape=[jax.ShapeDtypeStruct((t, ATTN_W), F32)] * 3 + r_out,
        scratch_shapes=r_sems,
        compiler_params=_params(56, 2),
    )(qkn, qkn, qkn, qkn, qkv, qkv, o, o, lse, lse, do, do, mb, *[g for g, _ in riders])
    return res[0], res[1], res[2], list(res[3:])


def _fwd_ffn(x, ycn, ya, wout, wg4, wu4, g_oa, g_ffn, fcw, fcb, tm, late=()):
    t = x.shape[0]
    nt = t // tm
    nl = len(late)

    def body(*refs):
        x_ref, ycn_ref, ya_ref, wout_ref, wg_ref, wu_ref, goa_ref, gffn_ref, fcw_ref, fcb_ref = refs[0:10]
        x1_ref, gp_ref, up_ref, gate_ref, act_ref, ycat_ref, h2_ref = refs[10 + nl:17 + nl]
        cbuf = refs[17 + 2 * nl]
        if nl:
            gather = _Gather(refs[10:10 + nl], refs[17 + nl:17 + 2 * nl], *refs[18 + 2 * nl:21 + 2 * nl])
            pl.when(pl.program_id(0) == 0)(gather.start)
            pl.when(pl.program_id(0) == nt // 2)(gather.forward)
            pl.when(pl.program_id(0) == nt - 1)(gather.finish)

        @pl.when(pl.program_id(0) == 0)
        def _():
            cbuf[0:8, :] = jnp.zeros((8, D_FF), F32)

        yat = ya_ref[...]
        yan = ((yat * _rstd(yat)) * goa_ref[...]).astype(BF)
        ycn = ycn_ref[...]
        ycat_ref[:, 0:CONV_W] = ycn
        ycat_ref[:, CONV_W:D_MODEL] = yan
        x1 = x_ref[...] + _mm(ycn, wout_ref[0:CONV_W, :]) + _mm(yan, wout_ref[CONV_W:D_MODEL, :])
        x1_ref[...] = x1
        h2 = ((x1 * _rstd(x1)) * gffn_ref[...]).astype(BF)
        h2_ref[...] = h2
        for lo, hi in FF_SLABS:
            gps = _mm_nt(h2, wg_ref[lo:hi, :])
            ups = _mm_nt(h2, wu_ref[lo:hi, :])
            gp_ref[:, lo:hi] = gps.astype(BF)
            up_ref[:, lo:hi] = ups.astype(BF)
            cbuf[8:8 + tm, lo:hi] = gps
            cb = cbuf[:, lo:hi]
            gate = (fcw_ref[0:1, lo:hi] * _shift_down(cb, 2, tm) + fcw_ref[1:2, lo:hi] * _shift_down(cb, 1, tm)
                    + fcw_ref[2:3, lo:hi] * gps + fcb_ref[:, lo:hi])
            gate_ref[:, lo:hi] = gate.astype(BF)
            act_ref[:, lo:hi] = ((gate * jax.nn.sigmoid(gate)) * ups).astype(BF)
        cbuf[0:8, :] = cbuf[tm:tm + 8, :]

    any_spec = pl.BlockSpec(memory_space=pl.ANY)
    sems = [pltpu.SemaphoreType.DMA((6 * nl,)), pltpu.SemaphoreType.DMA((6 * nl,)), pltpu.SemaphoreType.DMA((nl,))]
    res = pl.pallas_call(
        body, name="fwd_ffn", grid=(nt,),
        in_specs=[_rows(tm, D_MODEL), _rows(tm, CONV_W), _rows(tm, ATTN_W), _const((D_MODEL, D_MODEL)),
                  _const((D_FF, D_MODEL)), _const((D_FF, D_MODEL)),
                  _const((1, ATTN_W)), _const((1, D_MODEL)), _const((3, D_FF)), _const((1, D_FF))]
        + [any_spec] * nl,
        out_specs=[_rows(tm, D_MODEL), _rows(tm, D_FF), _rows(tm, D_FF), _rows(tm, D_FF), _rows(tm, D_FF),
                   _rows(tm, D_MODEL), _rows(tm, D_MODEL)] + [any_spec] * nl,
        out_shape=[jax.ShapeDtypeStruct((t, D_MODEL), F32), jax.ShapeDtypeStruct((t, D_FF), BF),
                   jax.ShapeDtypeStruct((t, D_FF), BF), jax.ShapeDtypeStruct((t, D_FF), BF),
                   jax.ShapeDtypeStruct((t, D_FF), BF),
                   jax.ShapeDtypeStruct((t, D_MODEL), BF), jax.ShapeDtypeStruct((t, D_MODEL), BF)]
        + [jax.ShapeDtypeStruct((N_CHIPS,) + w.shape, w.dtype) for w in late],
        scratch_shapes=[pltpu.VMEM((tm + 8, D_FF), F32)] + (sems if nl else []),
        compiler_params=_params(56),
    )(x, ycn, ya, wout, wg4, wu4, g_oa, g_ffn, fcw, fcb, *late)
    return tuple(res[0:7]) + (list(res[7:]),)


def _fwd_tail(x1, act, p, target, wd4, wpg, wpp4, g_ple, tm):
    t = x1.shape[0]
    nt = t // tm

    def body(x1_ref, act_ref, p_ref, tgt_ref, wd_ref, wpg_ref, wpp_ref, g_ref,
             dx2_ref, h3_ref, ds_ref, dpp_ref, dg_ref, loss_ref, lacc):
        i = pl.program_id(0)

        @pl.when(i == 0)
        def _():
            dg_ref[...] = jnp.zeros_like(dg_ref)
            lacc[...] = jnp.zeros_like(lacc)

        x2 = x1_ref[...]
        for lo, hi in FF_SLABS:
            x2 = x2 + _mm(act_ref[:, lo:hi], wd_ref[lo:hi, :])
        r3 = _rstd(x2)
        xh = x2 * r3
        h3 = (xh * g_ref[...]).astype(BF)
        h3_ref[...] = h3
        sg = jax.nn.sigmoid(_mm(h3, wpg_ref[...]))
        pb = p_ref[...].astype(BF)
        pp = jnp.concatenate([_mm(pb, wpp_ref[s]) for s in range(N_CHIPS)], axis=1)
        err = (x2 + sg * pp) - tgt_ref[...]
        lacc[...] += _colsum(err * err)
        dx3 = err * (1.0 / D_MODEL)
        dpp_ref[...] = (dx3 * sg).astype(BF)
        dsb = ((dx3 * pp) * (sg * (1.0 - sg))).astype(BF)
        ds_ref[...] = dsb
        dh3 = _mm_nt(dsb, wpg_ref[...])
        dg_ref[...] += _colsum(dh3 * xh)
        dx2_ref[...] = dx3 + _norm_bwd(dh3, xh, r3, g_ref[...])

        @pl.when(i == nt - 1)
        def _():
            loss_ref[...] = jnp.full((1, 128), jnp.sum(lacc[...]) * (0.5 / D_MODEL), F32)

    return pl.pallas_call(
        body, name="fwd_tail", grid=(nt,),
        in_specs=[_rows(tm, D_MODEL), _rows(tm, D_FF), _rows(tm, PLE_DIM), _rows(tm, D_MODEL),
                  _const((D_FF, D_MODEL)), _const((D_MODEL, D_MODEL)),
                  _const((N_CHIPS, PLE_DIM, PLE_DIM)), _const((1, D_MODEL))],
        out_specs=[_rows(tm, D_MODEL), _rows(tm, D_MODEL), _rows(tm, D_MODEL), _rows(tm, D_MODEL),
                   pl.BlockSpec((1, D_MODEL), lambda i: (0, 0)), pl.BlockSpec((1, 128), lambda i: (0, 0))],
        out_shape=[jax.ShapeDtypeStruct((t, D_MODEL), F32), jax.ShapeDtypeStruct((t, D_MODEL), BF),
                   jax.ShapeDtypeStruct((t, D_MODEL), BF), jax.ShapeDtypeStruct((t, D_MODEL), BF),
                   jax.ShapeDtypeStruct((1, D_MODEL), F32), jax.ShapeDtypeStruct((1, 128), F32)],
        scratch_shapes=[pltpu.VMEM((1, D_MODEL), F32)],
        compiler_params=_params(48),
    )(x1, act, p, target, wd4, wpg, wpp4, g_ple)


def _bwd_ffn_a(dx2, gate, gp, up, wd4, fcw, tm, riders=()):
    t = dx2.shape[0]
    nt = t // tm
    nr = len(riders)

    def body(*refs):
        dx2_ref, gate_ref, gp_ref, up_ref, wd_ref, fcw_ref = refs[0:6]
        dgp_ref, dup_ref, dfcw_ref, dfcb_ref = refs[6 + nr:10 + nr]
        dbuf = refs[10 + 2 * nr]
        i = pl.program_id(0)
        if nr:
            _ride_scatter(i == 0, i == nt - 1, riders, refs[6:6 + nr], refs[10 + nr:10 + 2 * nr],
                          *refs[11 + 2 * nr:13 + 2 * nr])

        @pl.when(i == 0)
        def _():
            dbuf[tm:tm + 8, :] = jnp.zeros((8, D_FF), F32)
            dfcw_ref[...] = jnp.zeros_like(dfcw_ref)
            dfcb_ref[...] = jnp.zeros_like(dfcb_ref)

        dx2b = dx2_ref[...].astype(BF)
        for lo, hi in FF_SLABS:
            gate = gate_ref[:, lo:hi].astype(F32)
            gps = gp_ref[:, lo:hi].astype(F32)
            w0, w1, w2 = fcw_ref[0:1, lo:hi], fcw_ref[1:2, lo:hi], fcw_ref[2:3, lo:hi]
            sg = jax.nn.sigmoid(gate)
            dact = _mm_nt(dx2b, wd_ref[lo:hi, :])
            dup_ref[:, lo:hi] = (dact * (gate * sg)).astype(BF)
            dgate = (dact * up_ref[:, lo:hi].astype(F32)) * (sg * (1.0 + gate * (1.0 - sg)))
            dbuf[0:tm, lo:hi] = dgate
            db = dbuf[:, lo:hi]
            d1 = _shift_up(db, 1, tm)
            d2 = _shift_up(db, 2, tm)
            dfcb_ref[:, lo:hi] += _colsum(dgate)
            dfcw_ref[0:1, lo:hi] += _colsum(d2 * gps)
            dfcw_ref[1:2, lo:hi] += _colsum(d1 * gps)
            dfcw_ref[2:3, lo:hi] += _colsum(dgate * gps)
            dgp_ref[:, lo:hi] = (w2 * dgate + w1 * d1 + w0 * d2).astype(BF)
        dbuf[tm:tm + 8, :] = dbuf[0:8, :]

    r_in, r_out, r_sems = _rider_specs(riders)
    res = pl.pallas_call(
        body, name="bwd_ffn_a", grid=(nt,),
        in_specs=[_rows(tm, D_MODEL, nt), _rows(tm, D_FF, nt), _rows(tm, D_FF, nt), _rows(tm, D_FF, nt),
                  _const((D_FF, D_MODEL)), _const((3, D_FF))] + r_in,
        out_specs=[_rows(tm, D_FF, nt), _rows(tm, D_FF, nt),
                   pl.BlockSpec((3, D_FF), lambda i: (0, 0)), pl.BlockSpec((1, D_FF), lambda i: (0, 0))] + r_in,
        out_shape=[jax.ShapeDtypeStruct((t, D_FF), BF), jax.ShapeDtypeStruct((t, D_FF), BF),
                   jax.ShapeDtypeStruct((3, D_FF), F32), jax.ShapeDtypeStruct((1, D_FF), F32)] + r_out,
        scratch_shapes=[pltpu.VMEM((tm + 8, D_FF), F32)] + r_sems,
        compiler_params=_params(56),
    )(dx2, gate, gp, up, wd4, fcw, *[g for g, _ in riders])
    return res[0], res[1], res[2], res[3], list(res[4:])


def _bwd_ffn_b(dgp, dup, dx2, x1, ya, wg4, wu4, wout, g_ffn, g_oa, tm):
    t = dx2.shape[0]
    nt = t // tm

    def body(dgp_ref, dup_ref, dx2_ref, x1_ref, ya_ref, wg_ref, wu_ref, wout_ref, gffn_ref, goa_ref,
             dx1_ref, dycn_ref, dya_ref, dgffn_ref, dgoa_ref):
        @pl.when(pl.program_id(0) == 0)
        def _():
            dgffn_ref[...] = jnp.zeros_like(dgffn_ref)
            dgoa_ref[...] = jnp.zeros_like(dgoa_ref)

        dh2 = jnp.zeros((tm, D_MODEL), F32)
        for lo, hi in FF_SLABS:
            dh2 = dh2 + _mm(dgp_ref[:, lo:hi], wg_ref[lo:hi, :]) + _mm(dup_ref[:, lo:hi], wu_ref[lo:hi, :])
        x1 = x1_ref[...]
        r2 = _rstd(x1)
        xh = x1 * r2
        dgffn_ref[...] += _colsum(dh2 * xh)
        dx1 = dx2_ref[...] + _norm_bwd(dh2, xh, r2, gffn_ref[...])
        dx1_ref[...] = dx1
        dy = _mm_nt(dx1.astype(BF), wout_ref[...])
        dycn_ref[...] = dy[:, 0:CONV_W]
        dyan = dy[:, CONV_W:D_MODEL]
        yat = ya_ref[...]
        ra = _rstd(yat)
        yah = yat * ra
        dgoa_ref[...] += _colsum(dyan * yah)
        dya_ref[...] = _norm_bwd(dyan, yah, ra, goa_ref[...])

    return pl.pallas_call(
        body, name="bwd_ffn_b", grid=(nt,),
        in_specs=[_rows(tm, D_FF), _rows(tm, D_FF), _rows(tm, D_MODEL), _rows(tm, D_MODEL),
                  _rows(tm, ATTN_W), _const((D_FF, D_MODEL)), _const((D_FF, D_MODEL)),
                  _const((D_MODEL, D_MODEL)), _const((1, D_MODEL)), _const((1, ATTN_W))],
        out_specs=[_rows(tm, D_MODEL), _rows(tm, CONV_W), _rows(tm, ATTN_W),
                   pl.BlockSpec((1, D_MODEL), lambda i: (0, 0)), pl.BlockSpec((1, ATTN_W), lambda i: (0, 0))],
        out_shape=[jax.ShapeDtypeStruct((t, D_MODEL), F32), jax.ShapeDtypeStruct((t, CONV_W), F32),
                   jax.ShapeDtypeStruct((t, ATTN_W), F32),
                   jax.ShapeDtypeStruct((1, D_MODEL), F32), jax.ShapeDtypeStruct((1, ATTN_W), F32)],
        compiler_params=_params(48),
    )(dgp, dup, dx2, x1, ya, wg4, wu4, wout, g_ffn, g_oa)


def _bwd_mix(x, zbcx, qkv, dycn, dq, dk, dv, conv_w, conv_b, g_oc, g_mix, gm, gq8, gk8, tm):
    t = x.shape[0]
    nt = t // tm

    def body(x_ref, z_ref, zh_ref, qkv_ref, dycn_ref, dq_ref, dk_ref, dv_ref, cw_ref, cb_ref,
             goc_ref, g_ref, gm_ref, gq_ref, gk_ref,
             dz_ref, dcw_ref, dcb_ref, dgoc_ref, dgq_ref, dgk_ref, gw32_ref, gw16_ref,
             ubuf, dbuf, wacc, wstage, osem):
        i = pl.program_id(0)

        @pl.when(i == 0)
        def _():
            wacc[...] = jnp.zeros_like(wacc)
            dbuf[tm:tm + 8, :] = jnp.zeros((8, CONV_W), F32)
            dcw_ref[...] = jnp.zeros_like(dcw_ref)
            dcb_ref[...] = jnp.zeros_like(dcb_ref)
            dgoc_ref[...] = jnp.zeros_like(dgoc_ref)
            dgq_ref[...] = jnp.zeros_like(dgq_ref)
            dgk_ref[...] = jnp.zeros_like(dgk_ref)

        not_first_tile = i < nt - 1
        zb = z_ref[:, 0:512]
        zc = z_ref[:, 512:1024]
        zx = z_ref[:, 1024:1536]
        u = zc * zx
        ubuf[0:8, :] = jnp.where(not_first_tile, zh_ref[:, 512:1024] * zh_ref[:, 1024:1536], 0.0)
        ubuf[8:8 + tm, :] = u
        ub = ubuf[...]
        u1 = _shift_down(ub, 1, tm)
        u2 = _shift_down(ub, 2, tm)
        w0, w1, w2 = cw_ref[0:1, :], cw_ref[1:2, :], cw_ref[2:3, :]
        cv = w0 * u2 + w1 * u1 + w2 * u + cb_ref[...]
        yc = zb * cv
        rc = _rstd(yc)
        ych = yc * rc
        dycn = dycn_ref[...]
        dgoc_ref[...] += _colsum(dycn * ych)
        dyc = _norm_bwd(dycn, ych, rc, goc_ref[...])
        dcv = dyc * zb
        dcb_ref[...] += _colsum(dcv)
        dcw_ref[0:1, :] += _colsum(dcv * u2)
        dcw_ref[1:2, :] += _colsum(dcv * u1)
        dcw_ref[2:3, :] += _colsum(dcv * u)
        dbuf[0:tm, :] = dcv
        db = dbuf[...]
        du = w2 * dcv + w1 * _shift_up(db, 1, tm) + w0 * _shift_up(db, 2, tm)
        dbuf[tm:tm + 8, :] = dbuf[0:8, :]
        dz_ref[:, 0:512] = (dyc * cv).astype(BF)
        dz_ref[:, 512:1024] = (du * zx).astype(BF)
        dz_ref[:, 1024:1536] = (du * zc).astype(BF)
        for z0, d_ref, gg_ref, acc_ref, sc in ((0, dq_ref, gq_ref, dgq_ref, HEAD_DIM ** -0.5),
                                               (512, dk_ref, gk_ref, dgk_ref, 1.0)):
            z = qkv_ref[:, z0:z0 + 512]
            rr = lax.rsqrt(_head_mean(z * z, gm_ref) + EPS)
            zh = z * rr
            dn = d_ref[...] * sc
            acc_ref[...] += _colsum(dn * zh)
            dzh = dn * gg_ref[...]
            dz_ref[:, 1536 + z0:1536 + z0 + 512] = (rr * (dzh - zh * _head_mean(dzh * zh, gm_ref))).astype(BF)
        dz_ref[:, 2560:3072] = dv_ref[...].astype(BF)
        xt = x_ref[...]
        h1 = ((xt * _rstd(xt)) * g_ref[...]).astype(BF)
        for s in range(N_CHIPS):
            cols = slice(s * IN_SLAB, (s + 1) * IN_SLAB)
            wacc[:, cols] += _mm_tn(h1, dz_ref[:, cols])

        @pl.when(i == nt - 1)
        def _():
            wstage[...] = wacc[...].astype(BF)
            out32 = pltpu.make_async_copy(wacc, gw32_ref, osem.at[0])
            out16 = pltpu.make_async_copy(wstage, gw16_ref, osem.at[1])
            out32.start()
            out16.start()
            out32.wait()
            out16.wait()

    def acc(width, rows=1):
        return pl.BlockSpec((rows, width), lambda i: (0, 0))

    return pl.pallas_call(
        body, name="bwd_mix", grid=(nt,),
        in_specs=[_rows(tm, D_MODEL, nt), _rows(tm, 1536, nt), _halo(tm, 1536, nt),
                  _rows(tm, 1536, nt), _rows(tm, CONV_W, nt), _rows(tm, ATTN_W, nt), _rows(tm, ATTN_W, nt),
                  _rows(tm, ATTN_W, nt),
                  _const((3, CONV_W)), _const((1, CONV_W)), _const((1, CONV_W)), _const((1, D_MODEL)),
                  _const((ATTN_W, ATTN_W)), _const((1, ATTN_W)), _const((1, ATTN_W))],
        out_specs=[_rows(tm, 3072, nt),
                   acc(CONV_W, 3), acc(CONV_W), acc(CONV_W), acc(ATTN_W), acc(ATTN_W),
                   pl.BlockSpec(memory_space=pl.ANY), pl.BlockSpec(memory_space=pl.ANY)],
        out_shape=[jax.ShapeDtypeStruct((t, 3072), BF), jax.ShapeDtypeStruct((3, CONV_W), F32),
                   jax.ShapeDtypeStruct((1, CONV_W), F32), jax.ShapeDtypeStruct((1, CONV_W), F32),
                   jax.ShapeDtypeStruct((1, ATTN_W), F32), jax.ShapeDtypeStruct((1, ATTN_W), F32),
                   jax.ShapeDtypeStruct((D_MODEL, 3072), F32), jax.ShapeDtypeStruct((D_MODEL, 3072), BF)],
        scratch_shapes=[pltpu.VMEM((tm + 8, CONV_W), F32), pltpu.VMEM((tm + 8, CONV_W), F32),
                        pltpu.VMEM((D_MODEL, 3072), F32), pltpu.VMEM((D_MODEL, 3072), BF),
                        pltpu.SemaphoreType.DMA((2,))],
        compiler_params=_params(56),
    )(x, zbcx, zbcx, qkv, dycn, dq, dk, dv, conv_w, conv_b, g_oc, g_mix, gm, gq8, gk8)


def _bwd_in(x, dx1, dz, win4, g_mix, tm, riders=()):
    t = x.shape[0]
    nt = t // tm
    nr = len(riders)

    def body(*refs):
        x_ref, dx1_ref, dz_ref, w_ref, g_ref = refs[0:5]
        gx_ref, dg_ref = refs[5 + nr:7 + nr]
        i = pl.program_id(0)
        if nr:
            _ride_scatter(i == 0, i == nt - 1, riders, refs[5:5 + nr], refs[7 + nr:7 + 2 * nr],
                          *refs[7 + 2 * nr:9 + 2 * nr])

        @pl.when(i == 0)
        def _():
            dg_ref[...] = jnp.zeros_like(dg_ref)

        dh1 = jnp.zeros((tm, D_MODEL), F32)
        for s in range(N_CHIPS):
            dh1 = dh1 + _mm_nt(dz_ref[:, s * IN_SLAB:(s + 1) * IN_SLAB], w_ref[s])
        xt = x_ref[...]
        r1 = _rstd(xt)
        xh = xt * r1
        dg_ref[...] += _colsum(dh1 * xh)
        gx_ref[...] = dx1_ref[...] + _norm_bwd(dh1, xh, r1, g_ref[...])

    r_in, r_out, r_sems = _rider_specs(riders)
    res = pl.pallas_call(
        body, name="bwd_in", grid=(nt,),
        in_specs=[_rows(tm, D_MODEL), _rows(tm, D_MODEL), _rows(tm, 3072), _const((N_CHIPS, D_MODEL, IN_SLAB)),
                  _const((1, D_MODEL))] + r_in,
        out_specs=[_rows(tm, D_MODEL), pl.BlockSpec((1, D_MODEL), lambda i: (0, 0))] + r_in,
        out_shape=[jax.ShapeDtypeStruct((t, D_MODEL), F32), jax.ShapeDtypeStruct((1, D_MODEL), F32)] + r_out,
        scratch_shapes=r_sems,
        compiler_params=_params(48),
    )(x, dx1, dz, win4, g_mix, *[g for g, _ in riders])
    return res[0], res[1], list(res[2:])


def _wgrad(a, b, tn, tt, name):
    t, k = a.shape
    n = b.shape[1]
    nt = t // tt

    def body(a_ref, b_ref, o_ref, ob_ref):
        @pl.when(pl.program_id(1) == 0)
        def _():
            o_ref[...] = jnp.zeros_like(o_ref)

        o_ref[...] += _mm_tn(a_ref[...].astype(BF), b_ref[...].astype(BF))

        @pl.when(pl.program_id(1) == nt - 1)
        def _():
            ob_ref[...] = o_ref[...].astype(BF)

    spec = pl.BlockSpec((k, tn), lambda j, i: (0, j))
    return pl.pallas_call(
        body, name=name, grid=(n // tn, nt),
        in_specs=[pl.BlockSpec((tt, k), lambda j, i: (i, 0)), pl.BlockSpec((tt, tn), lambda j, i: (i, j))],
        out_specs=[spec, spec],
        out_shape=[jax.ShapeDtypeStruct((k, n), F32), jax.ShapeDtypeStruct((k, n), BF)],
        compiler_params=_params(48, 2),
    )(a, b)


def _gather_weights(shards, pack):
    nw = len(shards)

    def body(*refs):
        ins = refs[:nw]
        pack_ref = refs[nw]
        outs = refs[nw + 1:2 * nw + 1]
        pack_out = refs[2 * nw + 1]
        send_sems, recv_sems, local_sems = refs[2 * nw + 2:]
        x, y, c = _place()
        me = 2 * x + y
        local, remote = [], []

        def sem(w, j):
            return w * 6 + j

        def push(src, dst, w, j, to):
            return pltpu.make_async_remote_copy(src_ref=src, dst_ref=dst, send_sem=send_sems.at[sem(w, j)],
                                                recv_sem=recv_sems.at[sem(w, j)], device_id=to, device_id_type=MESH)

        def half_rows(w, h):
            half = ins[w].shape[0] // 2
            return pl.ds(pl.multiple_of(h * half, 16), half)

        for w in range(nw):
            local.append(pltpu.make_async_copy(ins[w], outs[w].at[me], local_sems.at[w]))
            for k in (1, 2, 3):
                px, py = _chip_peer(x, y, k)
                mine = half_rows(w, c)
                remote.append(push(ins[w].at[mine], outs[w].at[me, mine], w, k - 1, (px, py, c)))
        local.append(pltpu.make_async_copy(pack_ref, pack_out.at[me], local_sems.at[nw]))
        for k in (1, 2, 3):
            px, py = _chip_peer(x, y, k)
            remote.append(push(pack_ref, pack_out.at[me], nw, k - 1, (px, py, c)))
        for cp in local + remote:
            cp.start()
        for w in range(nw):
            for k in (1, 2, 3):
                landed = outs[w].at[me ^ k, half_rows(w, c)]
                push(landed, landed, w, k - 1, (x, y, c)).wait_recv()
                fw = push(landed, landed, w, 2 + k, (x, y, 1 - c))
                fw.start()
                remote.append(fw)
        for k in (1, 2, 3):
            landed = pack_out.at[me ^ k]
            push(landed, landed, nw, k - 1, (x, y, c)).wait_recv()
        for w in range(nw):
            for k in (1, 2, 3):
                landed = outs[w].at[me ^ k, half_rows(w, 1 - c)]
                push(landed, landed, w, 2 + k, (x, y, c)).wait_recv()
        for cp in remote:
            cp.wait_send()
        for cp in local:
            cp.wait()

    any_spec = pl.BlockSpec(memory_space=pl.ANY)
    out_shape = [jax.ShapeDtypeStruct((N_CHIPS,) + s.shape, s.dtype) for s in shards]
    out_shape.append(jax.ShapeDtypeStruct((N_CHIPS,) + pack.shape, pack.dtype))
    return pl.pallas_call(
        body, name="gather_weights",
        in_specs=[any_spec] * (nw + 1), out_specs=[any_spec] * (nw + 1), out_shape=out_shape,
        scratch_shapes=[pltpu.SemaphoreType.DMA(((nw + 1) * 6,)), pltpu.SemaphoreType.DMA(((nw + 1) * 6,)),
                        pltpu.SemaphoreType.DMA((nw + 1,))],
    )(*shards, pack)


def _adamw(w, g, m, v):
    m = ADAM_B1 * m + (1.0 - ADAM_B1) * g
    v = ADAM_B2 * v + (1.0 - ADAM_B2) * (g * g)
    m_hat = m / (1.0 - ADAM_B1 ** ADAM_STEP)
    v_hat = v / (1.0 - ADAM_B2 ** ADAM_STEP)
    delta = -ADAM_LR * (m_hat / (jnp.sqrt(v_hat) + ADAM_EPS) + ADAM_WD * w)
    return delta, m, v


def _finish_reduce(grad, slots, col_sharded, name):
    r, cw = _piece_shape(grad.shape, col_sharded)
    chunk = 32
    assert r % chunk == 0

    def body(g_hbm, slots_ref, full, own, lsem, c_send, c_recv):
        x, y, c = _place()
        cp = pltpu.make_async_copy(g_hbm.at[_piece_window(col_sharded, r, cw, 2 * x + y, c)], own, lsem)
        cp.start()
        cp.wait()
        mine = pl.multiple_of(c * r, 8)

        def add(j, carry):
            rows = pl.ds(pl.multiple_of(j * chunk, 8), chunk)
            tot = own[rows, :]
            for k in range(7):
                tot = tot + slots_ref[k, rows, :].astype(F32)
            full[pl.ds(mine + pl.multiple_of(j * chunk, 8), chunk), :] = tot
            return carry

        lax.fori_loop(0, r // chunk, add, 0)
        half = full.at[pl.ds(mine, r), :]
        swap = pltpu.make_async_remote_copy(src_ref=half, dst_ref=half, send_sem=c_send, recv_sem=c_recv,
                                            device_id=(x, y, 1 - c), device_id_type=MESH)
        swap.start()
        swap.wait()

    vmem = pl.BlockSpec(memory_space=pltpu.VMEM)
    return pl.pallas_call(
        body, name=name, in_specs=[pl.BlockSpec(memory_space=pl.ANY), vmem], out_specs=vmem,
        out_shape=jax.ShapeDtypeStruct((2 * r, cw), F32),
        scratch_shapes=[pltpu.VMEM((r, cw), F32), pltpu.SemaphoreType.DMA, pltpu.SemaphoreType.DMA,
                        pltpu.SemaphoreType.DMA],
        compiler_params=pltpu.CompilerParams(vmem_limit_bytes=32 * MIB),
    )(grad, slots)


def _adamw_big(g, w, m, v, name):
    vr, vc = w.shape
    assert g.shape == w.shape
    rows = 64

    def body(g_ref, w_ref, m_ref, v_ref, go_ref, do_ref, mo_ref, vo_ref):
        gg = g_ref[...]
        delta, mn, vn = _adamw(w_ref[...], gg, m_ref[...], v_ref[...])
        go_ref[...] = gg
        do_ref[...] = delta
        mo_ref[...] = mn
        vo_ref[...] = vn

    blk = pl.BlockSpec((rows, vc), lambda i: (i, 0))
    shard = jax.ShapeDtypeStruct((vr, vc), F32)
    return pl.pallas_call(
        body, name=name, grid=(vr // rows,),
        in_specs=[blk, blk, blk, blk], out_specs=[blk] * 4,
        out_shape=[shard] * 4, compiler_params=_params(32),
    )(g, w, m, v)


def _allreduce_small(pack):
    rows = pack.shape[0]

    def body(p_ref, o_ref, slots, send_sems, recv_sems):
        x, y, c = _place()
        me = 4 * x + 2 * y + c
        slots[me] = p_ref[...]
        sends = []
        for k in range(1, 8):
            cp = pltpu.make_async_remote_copy(
                src_ref=p_ref, dst_ref=slots.at[me], send_sem=send_sems.at[k - 1], recv_sem=recv_sems.at[k - 1],
                device_id=(x ^ (k >> 2), y ^ ((k >> 1) & 1), c ^ (k & 1)), device_id_type=MESH)
            cp.start()
            sends.append(cp)
        for cp in sends:
            cp.wait()
        tot = slots[0]
        for j in range(1, 8):
            tot = tot + slots[j]
        o_ref[...] = tot

    vmem = pl.BlockSpec(memory_space=pltpu.VMEM)
    return pl.pallas_call(
        body, name="allreduce_small", in_specs=[vmem], out_specs=vmem,
        out_shape=jax.ShapeDtypeStruct(pack.shape, F32),
        scratch_shapes=[pltpu.VMEM((8, rows, D_MODEL), F32), pltpu.SemaphoreType.DMA((7,)),
                        pltpu.SemaphoreType.DMA((7,))],
    )(pack)


def _adamw_small(ws, gs, ms, vs):
    n = len(ws)

    def body(*refs):
        w_refs, g_refs, m_refs, v_refs = refs[0:n], refs[n:2 * n], refs[2 * n:3 * n], refs[3 * n:4 * n]
        d_refs, mo_refs, vo_refs = refs[4 * n:5 * n], refs[5 * n:6 * n], refs[6 * n:7 * n]
        for j in range(n):
            delta, mn, vn = _adamw(w_refs[j][...], g_refs[j][...], m_refs[j][...], v_refs[j][...])
            d_refs[j][...] = delta
            mo_refs[j][...] = mn
            vo_refs[j][...] = vn

    vmem = pl.BlockSpec(memory_space=pltpu.VMEM)
    shapes = [jax.ShapeDtypeStruct(w.shape, F32) for w in ws]
    outs = pl.pallas_call(
        body, name="adamw_small", in_specs=[vmem] * (4 * n), out_specs=[vmem] * (3 * n), out_shape=shapes * 3,
    )(*ws, *gs, *ms, *vs)
    return outs[0:n], outs[n:2 * n], outs[2 * n:3 * n]


def _local_step(x, p, target, wts, late=None):
    (win4, wout, wg4, wu4, wd4, wpg, wpp4, conv_w, fcw, g_mix, conv_b, gq, gk, g_oc, g_oa, g_ffn, fcb, g_ple) = wts
    comm = late is not None
    gm = jnp.kron(jnp.eye(N_HEADS, dtype=F32), jnp.full((HEAD_DIM, HEAD_DIM), 1.0 / HEAD_DIM, F32)).astype(BF)
    gq8, gk8 = jnp.tile(gq, (1, N_HEADS)), jnp.tile(gk, (1, N_HEADS))
    mb = _mask_table()
    zbcx, qkv, ycn, qkn = _fwd_mix(x, g_mix, win4, conv_w, conv_b, g_oc, gm, gq8, gk8, 512)
    ya, lse, gathered = _attn_fwd(qkn, qkv, mb, late[0:3] if comm else ())
    if comm:
        wout, wg4, wu4 = (g.reshape(-1, D_MODEL) for g in gathered)
    x1, gp, up, gate, act, ycat, h2, gathered = _fwd_ffn(x, ycn, ya, wout, wg4, wu4, g_oa, g_ffn, fcw, fcb, 256,
                                                    late[3:6] if comm else ())
    if comm:
        wd4, wpg, wpp4 = gathered
        wd4, wpg = wd4.reshape(D_FF, D_MODEL), wpg.reshape(D_MODEL, D_MODEL)
    dx2, h3, ds, dpp, dg_ple, loss = _fwd_tail(x1, act, p, target, wd4, wpg, wpp4, g_ple, 512)
    big, big16, slots = {}, {}, {}

    def wgrad(name, a, b, tn):
        big[name], big16[name] = _wgrad(a, b, tn, 1024, "wgrad_" + name)
        return (big16[name], _COL_SHARDED[name])

    riders = [wgrad("w_down", act, dx2, 512), wgrad("w_ple_gate", h3, ds, 1024), wgrad("w_ple_proj", p, dpp, 1024)]
    dgp, dup, dfcw, dfcb, got = _bwd_ffn_a(dx2, gate, gp, up, wd4, fcw, 512, riders if comm else ())
    slots.update(zip(("w_down", "w_ple_gate", "w_ple_proj"), got))
    riders = [wgrad("w_gate", dgp, h2, 512), wgrad("w_up", dup, h2, 512)]
    dx1, dycn, dya, dg_ffn, dg_oa = _bwd_ffn_b(dgp, dup, dx2, x1, ya, wg4, wu4, wout, g_ffn, g_oa, 512)
    riders.append(wgrad("w_out", ycat, dx1, 1024))
    dq, dk, dv, got = _attn_bwd(qkn, qkv, ya, lse, dya, mb, riders if comm else ())
    slots.update(zip(("w_gate", "w_up", "w_out"), got))
    dz, dcw, dcb, dg_oc, dgq8, dgk8, big["w_in"], big16["w_in"] = _bwd_mix(
        x, zbcx, qkv, dycn, dq, dk, dv, conv_w, conv_b, g_oc, g_mix, gm, gq8, gk8, 512)
    riders = [(big16["w_in"], _COL_SHARDED["w_in"])]
    grad_x, dg_mix, got = _bwd_in(x, dx1, dz, win4, g_mix, 512, riders if comm else ())
    slots.update(zip(("w_in",), got))
    dgq = dgq8.reshape(N_HEADS, HEAD_DIM).sum(axis=0, keepdims=True)
    dgk = dgk8.reshape(N_HEADS, HEAD_DIM).sum(axis=0, keepdims=True)
    small = dict(g_mix=dg_mix, conv_w=dcw, conv_b=dcb, q_norm_g=dgq, k_norm_g=dgk, g_out_conv=dg_oc,
                 g_out_attn=dg_oa, g_ffn=dg_ffn, ffn_conv_w=dfcw, ffn_conv_b=dfcb, g_ple=dg_ple)
    return loss[0, 0], grad_x, big, slots, small


_SMALL_ROWS = 24


def _pack_small(s, loss):
    z64 = jnp.zeros((1, 1024 - 512 - 128), F32)
    rows = [s["g_mix"], s["g_ffn"], s["g_ple"],
            jnp.concatenate([s["conv_b"], s["g_out_conv"]], axis=1),
            jnp.concatenate([s["g_out_attn"], s["q_norm_g"], s["k_norm_g"], z64], axis=1),
            jnp.pad(s["conv_w"], ((0, 0), (0, 512))),
            jnp.pad(s["ffn_conv_b"], ((0, 0), (0, 3072 - D_FF))).reshape(3, 1024),
            jnp.pad(s["ffn_conv_w"], ((0, 0), (0, 3072 - D_FF))).reshape(9, 1024),
            jnp.pad(loss.reshape(1, 1), ((0, 0), (0, 1023))),
            jnp.zeros((_SMALL_ROWS - 21, 1024), F32)]
    return jnp.concatenate(rows, axis=0)


def _unpack_small(t):
    return dict(g_mix=t[0:1], g_ffn=t[1:2], g_ple=t[2:3], conv_b=t[3:4, 0:512], g_out_conv=t[3:4, 512:1024],
                g_out_attn=t[4:5, 0:512], q_norm_g=t[4:5, 512:576], k_norm_g=t[4:5, 576:640],
                conv_w=t[5:8, 0:512], ffn_conv_b=t[8:11].reshape(1, 3072)[:, :D_FF],
                ffn_conv_w=t[11:20].reshape(3, 3072)[:, :D_FF], loss=t[20, 0])


_BIG = ("w_in", "w_out", "w_gate", "w_up", "w_down", "w_ple_gate", "w_ple_proj")
_COL_SHARDED = dict(w_in=True, w_out=False, w_gate=False, w_up=False, w_down=False, w_ple_gate=False, w_ple_proj=True)
_TRANSPOSED = ("w_gate", "w_up")
_WEIGHTS = ("g_mix", "w_in", "conv_w", "conv_b", "q_norm_g", "k_norm_g", "g_out_conv", "g_out_attn", "w_out",
            "g_ffn", "w_gate", "w_up", "ffn_conv_w", "ffn_conv_b", "w_down", "g_ple", "w_ple_gate", "w_ple_proj")


def kernel(x, p, g_mix, w_in, conv_w, conv_b, q_norm_g, k_norm_g, g_out_conv, g_out_attn, w_out, g_ffn, w_gate, w_up, ffn_conv_w, ffn_conv_b, w_down, g_ple, w_ple_gate, w_ple_proj, loss_target, m_g_mix, m_w_in, m_conv_w, m_conv_b, m_q_norm_g, m_k_norm_g, m_g_out_conv, m_g_out_attn, m_w_out, m_g_ffn, m_w_gate, m_w_up, m_ffn_conv_w, m_ffn_conv_b, m_w_down, m_g_ple, m_w_ple_gate, m_w_ple_proj, v_g_mix, v_w_in, v_conv_w, v_conv_b, v_q_norm_g, v_k_norm_g, v_g_out_conv, v_g_out_attn, v_w_out, v_g_ffn, v_w_gate, v_w_up, v_ffn_conv_w, v_ffn_conv_b, v_w_down, v_g_ple, v_w_ple_gate, v_w_ple_proj):
    w = dict(g_mix=g_mix, w_in=w_in, conv_w=conv_w, conv_b=conv_b, q_norm_g=q_norm_g, k_norm_g=k_norm_g,
             g_out_conv=g_out_conv, g_out_attn=g_out_attn, w_out=w_out, g_ffn=g_ffn, w_gate=w_gate, w_up=w_up,
             ffn_conv_w=ffn_conv_w, ffn_conv_b=ffn_conv_b, w_down=w_down, g_ple=g_ple, w_ple_gate=w_ple_gate,
             w_ple_proj=w_ple_proj)
    m = dict(g_mix=m_g_mix, w_in=m_w_in, conv_w=m_conv_w, conv_b=m_conv_b, q_norm_g=m_q_norm_g, k_norm_g=m_k_norm_g,
             g_out_conv=m_g_out_conv, g_out_attn=m_g_out_attn, w_out=m_w_out, g_ffn=m_g_ffn, w_gate=m_w_gate,
             w_up=m_w_up, ffn_conv_w=m_ffn_conv_w, ffn_conv_b=m_ffn_conv_b, w_down=m_w_down, g_ple=m_g_ple,
             w_ple_gate=m_w_ple_gate, w_ple_proj=m_w_ple_proj)
    v = dict(g_mix=v_g_mix, w_in=v_w_in, conv_w=v_conv_w, conv_b=v_conv_b, q_norm_g=v_q_norm_g, k_norm_g=v_k_norm_g,
             g_out_conv=v_g_out_conv, g_out_attn=v_g_out_attn, w_out=v_w_out, g_ffn=v_g_ffn, w_gate=v_w_gate,
             w_up=v_w_up, ffn_conv_w=v_ffn_conv_w, ffn_conv_b=v_ffn_conv_b, w_down=v_w_down, g_ple=v_g_ple,
             w_ple_gate=v_w_ple_gate, w_ple_proj=v_w_ple_proj)
    mats = [k for k, a in w.items() if a.ndim == 3]
    w = {k: (a[0] if k in mats else a) for k, a in w.items()}
    m = {k: (a[0] if k in mats else a) for k, a in m.items()}
    v = {k: (a[0] if k in mats else a) for k, a in v.items()}
    for n in _TRANSPOSED:
        w[n], m[n], v[n] = w[n].T, m[n].T, v[n].T
    chip = 2 * lax.axis_index("x") + lax.axis_index("y")

    late = [w[n].astype(BF) for n in ("w_out", "w_gate", "w_up", "w_down", "w_ple_gate", "w_ple_proj")]
    pack = jnp.pad(jnp.concatenate([w["conv_w"], w["ffn_conv_w"]], axis=1), ((0, 5), (0, 1024 - 128 - D_FF_SHARD)))
    win4, pack4 = _gather_weights([w["w_in"].astype(BF)], pack)
    conv_w_full = pack4[:, 0:3, 0:128].transpose(1, 0, 2).reshape(3, CONV_W)
    fcw_full = pack4[:, 0:3, 128:128 + D_FF_SHARD].transpose(1, 0, 2).reshape(3, D_FF)
    wts = (win4, None, None, None, None, None, None, conv_w_full, fcw_full, w["g_mix"], w["conv_b"], w["q_norm_g"],
           w["k_norm_g"], w["g_out_conv"], w["g_out_attn"], w["g_ffn"], w["ffn_conv_b"], w["g_ple"])

    loss, grad_x, big, slots, small = _local_step(x[0], p[0, 0], loss_target[0], wts, late)

    grads, deltas, new_m, new_v = {}, {}, {}, {}
    for name in _BIG:
        total = _finish_reduce(big[name], slots[name], _COL_SHARDED[name], "finish_" + name)
        grads[name], deltas[name], new_m[name], new_v[name] = _adamw_big(total, w[name], m[name], v[name],
                                                                         "adamw_" + name)
    tot = _unpack_small(_allreduce_small(_pack_small(small, loss)))
    loss = tot.pop("loss")
    tot["conv_w"] = lax.dynamic_slice_in_dim(tot["conv_w"], chip * 128, 128, axis=1)
    tot["ffn_conv_w"] = lax.dynamic_slice_in_dim(tot["ffn_conv_w"], chip * D_FF_SHARD, D_FF_SHARD, axis=1)
    names = [n for n in _WEIGHTS if n not in _BIG]
    d_s, m_s, v_s = _adamw_small([w[n] for n in names], [tot[n] for n in names], [m[n] for n in names],
                                 [v[n] for n in names])
    for j, n in enumerate(names):
        grads[n], deltas[n], new_m[n], new_v[n] = tot[n], d_s[j], m_s[j], v_s[j]

    out = [loss, grad_x[None]]
    for group in (grads, deltas, new_m, new_v):
        for n in _TRANSPOSED:
            group[n] = group[n].T
        out += [group[n][None] if n in mats else group[n] for n in _WEIGHTS]
    return tuple(out)
```

```python
import jax
import jax.numpy as jnp
from jax import lax
from jax.experimental import pallas as pl
from jax.experimental.pallas import tpu as pltpu

D_MODEL = 1024
CONV_W = 512
N_HEADS = 8
HEAD_DIM = 64
ATTN_W = 512
D_FF = 2816
D_FF_SHARD = 704
FF_SLABS = ((0, 1408), (1408, 2816))
IN_SLAB = 768
PLE_DIM = 256
N_CHIPS = 4
QBLK = 128
DILATIONS = (1, 4, 16)
EPS = 1e-6
NEG = -1e30
MESH = pl.DeviceIdType.MESH

ADAM_LR = 0.001
ADAM_B1 = 0.9
ADAM_B2 = 0.999
ADAM_EPS = 1e-08
ADAM_WD = 0.01
ADAM_STEP = 10

BF = jnp.bfloat16
F32 = jnp.float32
MIB = 1024 * 1024


def _mm(a, b):
    return jnp.dot(a, b, preferred_element_type=F32)


def _mm_nt(a, b):
    return lax.dot_general(a, b, (((1,), (1,)), ((), ())), preferred_element_type=F32)


def _mm_tn(a, b):
    return lax.dot_general(a, b, (((0,), (0,)), ((), ())), preferred_element_type=F32)


def _rstd(a):
    return lax.rsqrt(jnp.mean(a * a, axis=-1, keepdims=True) + EPS)


def _norm_bwd(dy, xh, r, g):
    dxh = dy * g
    return r * (dxh - xh * jnp.mean(dxh * xh, axis=-1, keepdims=True))


def _colsum(a):
    return jnp.sum(a, axis=0, keepdims=True)


def _head_mean(a, gm_ref):
    return _mm(a.astype(BF), gm_ref[...])


def _shift_down(buf, k, tm):
    return pltpu.roll(buf, k, axis=0)[8:8 + tm]


def _shift_up(buf, k, tm):
    return pltpu.roll(buf, tm + 8 - k, axis=0)[0:tm]


def _params(vmem_mib, n_grid=1):
    return pltpu.CompilerParams(dimension_semantics=("arbitrary",) * n_grid, vmem_limit_bytes=vmem_mib * MIB)


def _const(shape):
    n = len(shape)
    return pl.BlockSpec(shape, lambda *_: (0,) * n, pipeline_mode=pl.Buffered(1))


def _rows(tm, width, rev_of=None):
    if rev_of is None:
        return pl.BlockSpec((tm, width), lambda i: (i, 0))
    return pl.BlockSpec((tm, width), lambda i: (rev_of - 1 - i, 0))


def _halo(tm, width, nt):
    return pl.BlockSpec((8, width), lambda i: (jnp.maximum((nt - 1 - i) * (tm // 8) - 1, 0), 0))


def _fwd_mix(x, g_mix, win4, conv_w, conv_b, g_oc, gm, gq8, gk8, tm):
    t = x.shape[0]
    nt = t // tm

    def body(x_ref, g_ref, w_ref, cw_ref, cb_ref, goc_ref, gm_ref, gq_ref, gk_ref,
             zbcx_ref, qkv_ref, ycn_ref, qkn_ref, ubuf):
        @pl.when(pl.program_id(0) == 0)
        def _():
            ubuf[0:8, :] = jnp.zeros((8, CONV_W), F32)

        xt = x_ref[...]
        h = ((xt * _rstd(xt)) * g_ref[...]).astype(BF)
        zbcx_ref[:, 0:IN_SLAB] = _mm(h, w_ref[0])
        zbcx_ref[:, IN_SLAB:2 * IN_SLAB] = _mm(h, w_ref[1])
        qkv_ref[:, 0:IN_SLAB] = _mm(h, w_ref[2])
        qkv_ref[:, IN_SLAB:2 * IN_SLAB] = _mm(h, w_ref[3])
        u = zbcx_ref[:, 512:1024] * zbcx_ref[:, 1024:1536]
        ubuf[8:8 + tm, :] = u
        ub = ubuf[...]
        cv = (cw_ref[0:1, :] * _shift_down(ub, 2, tm) + cw_ref[1:2, :] * _shift_down(ub, 1, tm)
              + cw_ref[2:3, :] * u + cb_ref[...])
        ubuf[0:8, :] = ubuf[tm:tm + 8, :]
        yc = zbcx_ref[:, 0:512] * cv
        ycn_ref[...] = ((yc * _rstd(yc)) * goc_ref[...]).astype(BF)
        zq = qkv_ref[:, 0:512]
        zk = qkv_ref[:, 512:1024]
        rq = lax.rsqrt(_head_mean(zq * zq, gm_ref) + EPS)
        rk = lax.rsqrt(_head_mean(zk * zk, gm_ref) + EPS)
        qkn_ref[:, 0:512] = ((zq * rq) * gq_ref[...]) * (HEAD_DIM ** -0.5)
        qkn_ref[:, 512:1024] = (zk * rk) * gk_ref[...]

    return pl.pallas_call(
        body, name="fwd_mix", grid=(nt,),
        in_specs=[_rows(tm, D_MODEL), _const((1, D_MODEL)), _const((N_CHIPS, D_MODEL, IN_SLAB)),
                  _const((3, CONV_W)), _const((1, CONV_W)), _const((1, CONV_W)), _const((ATTN_W, ATTN_W)),
                  _const((1, ATTN_W)), _const((1, ATTN_W))],
        out_specs=[_rows(tm, 1536), _rows(tm, 1536), _rows(tm, CONV_W), _rows(tm, 1024)],
        out_shape=[jax.ShapeDtypeStruct((t, 1536), F32), jax.ShapeDtypeStruct((t, 1536), F32),
                   jax.ShapeDtypeStruct((t, CONV_W), BF), jax.ShapeDtypeStruct((t, 1024), F32)],
        scratch_shapes=[pltpu.VMEM((tm + 8, CONV_W), F32)],
        compiler_params=_params(48),
    )(x, g_mix, win4, conv_w, conv_b, g_oc, gm, gq8, gk8)


def _place():
    x, y, c = lax.axis_index("x"), lax.axis_index("y"), lax.axis_index("c")
    return x, y, c


def _chip_peer(x, y, k):
    return x ^ (k >> 1), y ^ (k & 1)


def _piece_shape(grad_shape, col_sharded):
    kk, nn = grad_shape
    return (kk // 2, nn // N_CHIPS) if col_sharded else (kk // (2 * N_CHIPS), nn)


def _piece_window(col_sharded, r, cw, s, h):
    if col_sharded:
        return (pl.ds(pl.multiple_of(h * r, 16), r), pl.ds(pl.multiple_of(s * cw, 128), cw))
    return (pl.ds(pl.multiple_of((2 * s + h) * r, 16), r), slice(None))


def _scatter_copies(g_ref, slots_ref, send_sems, recv_sems, base, col_sharded):
    x, y, c = _place()
    r, cw = slots_ref.shape[1:]
    copies = []
    for k in range(1, 8):
        tx, ty, tc = x ^ (k >> 2), y ^ ((k >> 1) & 1), c ^ (k & 1)
        copies.append(pltpu.make_async_remote_copy(
            src_ref=g_ref.at[_piece_window(col_sharded, r, cw, 2 * tx + ty, tc)], dst_ref=slots_ref.at[k - 1],
            send_sem=send_sems.at[base + k - 1], recv_sem=recv_sems.at[base + k - 1],
            device_id=(tx, ty, tc), device_id_type=MESH))
    return copies


def _ride_scatter(first, last, riders, g_refs, slot_refs, send_sems, recv_sems):
    def all_copies():
        out = []
        for j, (_, col_sharded) in enumerate(riders):
            out += _scatter_copies(g_refs[j], slot_refs[j], send_sems, recv_sems, 7 * j, col_sharded)
        return out

    @pl.when(first)
    def _():
        for cp in all_copies():
            cp.start()

    @pl.when(last)
    def _():
        for cp in all_copies():
            cp.wait()


def _rider_specs(riders):
    any_spec = pl.BlockSpec(memory_space=pl.ANY)
    shapes = [jax.ShapeDtypeStruct((7,) + _piece_shape(g.shape, cs), BF) for g, cs in riders]
    sems = [pltpu.SemaphoreType.DMA((7 * len(riders),)), pltpu.SemaphoreType.DMA((7 * len(riders),))] if riders else []
    return [any_spec] * len(riders), shapes, sems


class _Gather:
    def __init__(self, ins, outs, send_sems, recv_sems, local_sems):
        self.ins, self.outs = ins, outs
        self.send_sems, self.recv_sems, self.local_sems = send_sems, recv_sems, local_sems
        self.x, self.y, self.c = _place()
        self.me = 2 * self.x + self.y

    def _push(self, src, dst, w, j, to):
        return pltpu.make_async_remote_copy(src_ref=src, dst_ref=dst, send_sem=self.send_sems.at[6 * w + j],
                                            recv_sem=self.recv_sems.at[6 * w + j], device_id=to, device_id_type=MESH)

    def _half(self, w, h):
        half = self.ins[w].shape[0] // 2
        return pl.ds(pl.multiple_of(h * half, 16), half)

    def _local(self, w):
        return pltpu.make_async_copy(self.ins[w], self.outs[w].at[self.me], self.local_sems.at[w])

    def _ici(self, w, k):
        px, py = _chip_peer(self.x, self.y, k)
        mine = self._half(w, self.c)
        return self._push(self.ins[w].at[mine], self.outs[w].at[self.me, mine], w, k - 1, (px, py, self.c))

    def _landed(self, w, k, h):
        return self.outs[w].at[self.me ^ k, self._half(w, h)]

    def _fwd(self, w, k):
        landed = self._landed(w, k, self.c)
        return self._push(landed, landed, w, 2 + k, (self.x, self.y, 1 - self.c))

    def start(self):
        for w in range(len(self.ins)):
            self._local(w).start()
            for k in (1, 2, 3):
                self._ici(w, k).start()

    def forward(self):
        for w in range(len(self.ins)):
            for k in (1, 2, 3):
                landed = self._landed(w, k, self.c)
                self._push(landed, landed, w, k - 1, (self.x, self.y, self.c)).wait_recv()
                self._fwd(w, k).start()

    def finish(self):
        for w in range(len(self.ins)):
            for k in (1, 2, 3):
                landed = self._landed(w, k, 1 - self.c)
                self._push(landed, landed, w, 2 + k, (self.x, self.y, self.c)).wait_recv()
            for k in (1, 2, 3):
                self._ici(w, k).wait_send()
                self._fwd(w, k).wait_send()
            self._local(w).wait()


def _alibi(h):
    return 2.0 ** (-(h + 1))


CHUNK = 2048


def _mask_table():
    slopes = jnp.asarray([_alibi(h) for h in range(N_HEADS)], F32)[:, None, None]
    step = jnp.arange(QBLK)[:, None] + QBLK - jnp.arange(2 * QBLK)[None, :]
    valid = (step >= 0) & (step <= QBLK)
    tab = jnp.stack([jnp.where(valid[None], -slopes * (step * d)[None].astype(F32), NEG) for d in DILATIONS])
    return tab.reshape(3, N_HEADS // 2, 2 * QBLK, 2 * QBLK)


def _attn_fwd(qkn, qkv, mb, late=()):
    t = qkn.shape[0]
    nc = t // CHUNK
    nl = len(late)

    def body(*refs):
        qc_ref, kp_ref, kc_ref, vp_ref, vc_ref, mb_ref = refs[0:6]
        o_ref, l_ref = refs[6 + nl:8 + nl]
        ob0, ob1, ob2, lb0, lb1, lb2 = refs[8 + 2 * nl:14 + 2 * nl]
        if nl:
            gather = _Gather(refs[6:6 + nl], refs[8 + nl:8 + 2 * nl], *refs[14 + 2 * nl:17 + 2 * nl])
            step = pl.program_id(0) * nc + pl.program_id(1)
            pl.when(step == 0)(gather.start)
            pl.when(step == 2 * nc)(gather.forward)
            pl.when(step == (N_HEADS // 2) * nc - 1)(gather.finish)
        first = pl.program_id(1) == 0
        lane = lax.broadcasted_iota(jnp.int32, (QBLK, 128), 1)
        lo_half = lane < HEAD_DIM
        kj = lax.broadcasted_iota(jnp.int32, (2 * QBLK, 2 * QBLK), 1)
        no_prev = first & (kj < QBLK)
        obs, lbs = (ob0, ob1, ob2), (lb0, lb1, lb2)

        def by_head(a):
            return jnp.where(lo_half, a, 0.0).astype(BF), jnp.where(lo_half, 0.0, a).astype(BF)

        for di, d in enumerate(DILATIONS):
            span = d * QBLK
            for r in range(d):
                tail = pl.ds(CHUNK - span + r, QBLK, stride=d)
                k_prev = kp_ref[tail, :].astype(BF)
                v_prev = by_head(vp_ref[tail, :])
                for b in range(CHUNK // span):
                    rows = pl.ds(r + span * b, QBLK, stride=d)
                    q0, q1 = by_head(qc_ref[rows, :])
                    k_cur = kc_ref[rows, :].astype(BF)
                    v_cur = by_head(vc_ref[rows, :])
                    s = _mm_nt(jnp.concatenate([q0, q1], axis=0), jnp.concatenate([k_prev, k_cur], axis=0))
                    s = s + mb_ref[di, 0]
                    if b == 0:
                        s = jnp.where(no_prev, NEG, s)
                    m = jnp.max(s, axis=-1, keepdims=True)
                    e = jnp.exp(s - m)
                    den = jnp.sum(e, axis=-1, keepdims=True)
                    eb = e.astype(BF)
                    o = _mm(jnp.concatenate([eb[0:QBLK], eb[QBLK:2 * QBLK]], axis=1),
                            jnp.concatenate([v_prev[0], v_cur[0], v_prev[1], v_cur[1]], axis=0))
                    inv = 1.0 / den
                    lse = m + jnp.log(den)
                    obs[di][rows, :] = o * jnp.where(lo_half, inv[0:QBLK], inv[QBLK:2 * QBLK])
                    lbs[di][rows, :] = jnp.where(lo_half, lse[0:QBLK], lse[QBLK:2 * QBLK])
                    k_prev, v_prev = k_cur, v_cur
        for c0 in range(0, CHUNK, 256):
            rs = slice(c0, c0 + 256)
            l0, l1, l2 = lb0[rs, :], lb1[rs, :], lb2[rs, :]
            mx = jnp.maximum(jnp.maximum(l0, l1), l2)
            w0, w1, w2 = jnp.exp(l0 - mx), jnp.exp(l1 - mx), jnp.exp(l2 - mx)
            tot = w0 + w1 + w2
            o_ref[rs, :] = (ob0[rs, :] * w0 + ob1[rs, :] * w1 + ob2[rs, :] * w2) / tot
            l_ref[rs, :] = mx + jnp.log(tot)

    def cur(col):
        return pl.BlockSpec((CHUNK, 128), lambda hp, n: (n, col + hp))

    def prv(col):
        return pl.BlockSpec((CHUNK, 128), lambda hp, n: (jnp.maximum(n - 1, 0), col + hp))

    out = pl.BlockSpec((CHUNK, 128), lambda hp, n: (n, hp))
    any_spec = pl.BlockSpec(memory_space=pl.ANY)
    sems = [pltpu.SemaphoreType.DMA((6 * nl,)), pltpu.SemaphoreType.DMA((6 * nl,)), pltpu.SemaphoreType.DMA((nl,))]
    res = pl.pallas_call(
        body, name="attn_fwd", grid=(N_HEADS // 2, nc),
        in_specs=[cur(0), prv(4), cur(4), prv(8), cur(8),
                  pl.BlockSpec((3, 1, 2 * QBLK, 2 * QBLK), lambda hp, n: (0, hp, 0, 0))] + [any_spec] * nl,
        out_specs=[out, out] + [any_spec] * nl,
        out_shape=[jax.ShapeDtypeStruct((t, ATTN_W), F32)] * 2
        + [jax.ShapeDtypeStruct((N_CHIPS,) + w.shape, w.dtype) for w in late],
        scratch_shapes=[pltpu.VMEM((CHUNK, 128), F32)] * 6 + (sems if nl else []),
        compiler_params=_params(48, 2),
    )(qkn, qkn, qkn, qkv, qkv, mb, *late)
    return res[0], res[1], list(res[2:])


def _attn_bwd(qkn, qkv, o, lse, do, mb, riders=()):
    t = qkn.shape[0]
    nc = t // CHUNK
    nr = len(riders)

    def body(*refs):
        (qc_ref, qn_ref, kp_ref, kc_ref, vp_ref, vc_ref, oc_ref, on_ref, lc_ref, ln_ref, dc_ref, dn_ref,
         mb_ref) = refs[0:13]
        dq_ref, dk_ref, dv_ref = refs[13 + nr:16 + nr]
        if nr:
            step = pl.program_id(0) * nc + pl.program_id(1)
            _ride_scatter(step == 0, step == (N_HEADS // 2) * nc - 1, riders, refs[13:13 + nr],
                          refs[16 + nr:16 + 2 * nr], *refs[16 + 2 * nr:18 + 2 * nr])
        first = pl.program_id(1) == 0
        last = pl.program_id(1) == nc - 1
        lane = lax.broadcasted_iota(jnp.int32, (QBLK, 128), 1)
        lo_half = lane < HEAD_DIM
        kj = lax.broadcasted_iota(jnp.int32, (2 * QBLK, 2 * QBLK), 1)
        no_prev = first & (kj < QBLK)

        def by_head(a):
            return jnp.where(lo_half, a, 0.0).astype(BF), jnp.where(lo_half, 0.0, a).astype(BF)

        def query_side(q_ref, d_ref, o_ref_, l_ref_, rows):
            dvals = d_ref[rows, :]
            dd = dvals * o_ref_[rows, :]
            lv = l_ref_[rows, :]
            d0 = jnp.sum(jnp.where(lo_half, dd, 0.0), axis=-1, keepdims=True)
            d1 = jnp.sum(jnp.where(lo_half, 0.0, dd), axis=-1, keepdims=True)
            l0 = jnp.max(jnp.where(lo_half, lv, NEG), axis=-1, keepdims=True)
            l1 = jnp.max(jnp.where(lo_half, NEG, lv), axis=-1, keepdims=True)
            return (jnp.concatenate(by_head(q_ref[rows, :]), axis=0), jnp.concatenate(by_head(dvals), axis=0),
                    jnp.concatenate([l0, l1], axis=0), jnp.concatenate([d0, d1], axis=0))

        def tile(qs, dos, lcol, dcol, keys, vals, bias, dead):
            s = _mm_nt(qs, keys) + bias
            if dead is not None:
                s = jnp.where(dead, NEG, s)
            p = jnp.exp(s - lcol)
            ds = p * (_mm_nt(dos, vals) - dcol)
            return p.astype(BF), ds.astype(BF)

        def put(ref, di, rows, val):
            if di == 0:
                ref[rows, :] = val
            else:
                ref[rows, :] = ref[rows, :] + val

        for di, d in enumerate(DILATIONS):
            span = d * QBLK
            nbk = CHUNK // span
            for r in range(d):
                tail = pl.ds(CHUNK - span + r, QBLK, stride=d)
                k_prev = kp_ref[tail, :]
                kb_prev, km_prev = k_prev.astype(BF), by_head(k_prev)
                vb_prev = vp_ref[tail, :].astype(BF)
                rows_prev, dk_part, dv_part = None, None, None
                for b in range(nbk):
                    rows = pl.ds(r + span * b, QBLK, stride=d)
                    qs, dos, lcol, dcol = query_side(qc_ref, dc_ref, oc_ref, lc_ref, rows)
                    k_cur = kc_ref[rows, :]
                    kb_cur, km_cur = k_cur.astype(BF), by_head(k_cur)
                    vb_cur = vc_ref[rows, :].astype(BF)
                    p, ds = tile(qs, dos, lcol, dcol, jnp.concatenate([kb_prev, kb_cur], axis=0),
                                 jnp.concatenate([vb_prev, vb_cur], axis=0), mb_ref[di, 0],
                                 no_prev if b == 0 else None)
                    put(dq_ref, di, rows,
                        _mm(jnp.concatenate([ds[0:QBLK], ds[QBLK:2 * QBLK]], axis=1),
                            jnp.concatenate([km_prev[0], km_cur[0], km_prev[1], km_cur[1]], axis=0)))
                    dk2 = _mm_tn(ds, qs)
                    dv2 = _mm_tn(p, dos)
                    if b > 0:
                        put(dk_ref, di, rows_prev, dk_part + dk2[0:QBLK])
                        put(dv_ref, di, rows_prev, dv_part + dv2[0:QBLK])
                    rows_prev, dk_part, dv_part = rows, dk2[QBLK:2 * QBLK], dv2[QBLK:2 * QBLK]
                    kb_prev, km_prev, vb_prev = kb_cur, km_cur, vb_cur
                qs, dos, lcol, dcol = query_side(qn_ref, dn_ref, on_ref, ln_ref, pl.ds(r, QBLK, stride=d))
                p, ds = tile(qs, dos, lcol, dcol, kb_prev, vb_prev, mb_ref[di, 0, :, 0:QBLK], last)
                put(dk_ref, di, rows_prev, dk_part + _mm_tn(ds, qs))
                put(dv_ref, di, rows_prev, dv_part + _mm_tn(p, dos))

    def at(shift, col):
        return pl.BlockSpec((CHUNK, 128), lambda hp, n: (jnp.clip(n + shift, 0, nc - 1), col + hp))

    out = pl.BlockSpec((CHUNK, 128), lambda hp, n: (n, hp))
    r_in, r_out, r_sems = _rider_specs(riders)
    res = pl.pallas_call(
        body, name="attn_bwd", grid=(N_HEADS // 2, nc),
        in_specs=[at(0, 0), at(1, 0), at(-1, 4), at(0, 4), at(-1, 8), at(0, 8),
                  at(0, 0), at(1, 0), at(0, 0), at(1, 0), at(0, 0), at(1, 0),
                  pl.BlockSpec((3, 1, 2 * QBLK, 2 * QBLK), lambda hp, n: (0, hp, 0, 0))] + r_in,
        out_specs=[out, out, out] + r_in,
        out_shape=[jax.ShapeDtypeStruct((t, ATTN_W), F32)] * 3 + r_out,
        scratch_shapes=r_sems,
        compiler_params=_params(56, 2),
    )(qkn, qkn, qkn, qkn, qkv, qkv, o, o, lse, lse, do, do, mb, *[g for g, _ in riders])
    return res[0], res[1], res[2], list(res[3:])


def _fwd_ffn(x, ycn, ya, wout, wg4, wu4, g_oa, g_ffn, fcw, fcb, tm, late=()):
    t = x.shape[0]
    nt = t // tm
    nl = len(late)

    def body(*refs):
        x_ref, ycn_ref, ya_ref, wout_ref, wg_ref, wu_ref, goa_ref, gffn_ref, fcw_ref, fcb_ref = refs[0:10]
        x1_ref, gp_ref, up_ref, gate_ref, act_ref, ycat_ref, h2_ref = refs[10 + nl:17 + nl]
        cbuf = refs[17 + 2 * nl]
        if nl:
            gather = _Gather(refs[10:10 + nl], refs[17 + nl:17 + 2 * nl], *refs[18 + 2 * nl:21 + 2 * nl])
            pl.when(pl.program_id(0) == 0)(gather.start)
            pl.when(pl.program_id(0) == nt // 2)(gather.forward)
            pl.when(pl.program_id(0) == nt - 1)(gather.finish)

        @pl.when(pl.program_id(0) == 0)
        def _():
            cbuf[0:8, :] = jnp.zeros((8, D_FF), F32)

        yat = ya_ref[...]
        yan = ((yat * _rstd(yat)) * goa_ref[...]).astype(BF)
        ycn = ycn_ref[...]
        ycat_ref[:, 0:CONV_W] = ycn
        ycat_ref[:, CONV_W:D_MODEL] = yan
        x1 = x_ref[...] + _mm(ycn, wout_ref[0:CONV_W, :]) + _mm(yan, wout_ref[CONV_W:D_MODEL, :])
        x1_ref[...] = x1
        h2 = ((x1 * _rstd(x1)) * gffn_ref[...]).astype(BF)
        h2_ref[...] = h2
        for lo, hi in FF_SLABS:
            gps = _mm_nt(h2, wg_ref[lo:hi, :])
            ups = _mm_nt(h2, wu_ref[lo:hi, :])
            gp_ref[:, lo:hi] = gps.astype(BF)
            up_ref[:, lo:hi] = ups.astype(BF)
            cbuf[8:8 + tm, lo:hi] = gps
            cb = cbuf[:, lo:hi]
            gate = (fcw_ref[0:1, lo:hi] * _shift_down(cb, 2, tm) + fcw_ref[1:2, lo:hi] * _shift_down(cb, 1, tm)
                    + fcw_ref[2:3, lo:hi] * gps + fcb_ref[:, lo:hi])
            gate_ref[:, lo:hi] = gate.astype(BF)
            act_ref[:, lo:hi] = ((gate * jax.nn.sigmoid(gate)) * ups).astype(BF)
        cbuf[0:8, :] = cbuf[tm:tm + 8, :]

    any_spec = pl.BlockSpec(memory_space=pl.ANY)
    sems = [pltpu.SemaphoreType.DMA((6 * nl,)), pltpu.SemaphoreType.DMA((6 * nl,)), pltpu.SemaphoreType.DMA((nl,))]
    res = pl.pallas_call(
        body, name="fwd_ffn", grid=(nt,),
        in_specs=[_rows(tm, D_MODEL), _rows(tm, CONV_W), _rows(tm, ATTN_W), _const((D_MODEL, D_MODEL)),
                  _const((D_FF, D_MODEL)), _const((D_FF, D_MODEL)),
                  _const((1, ATTN_W)), _const((1, D_MODEL)), _const((3, D_FF)), _const((1, D_FF))]
        + [any_spec] * nl,
        out_specs=[_rows(tm, D_MODEL), _rows(tm, D_FF), _rows(tm, D_FF), _rows(tm, D_FF), _rows(tm, D_FF),
                   _rows(tm, D_MODEL), _rows(tm, D_MODEL)] + [any_spec] * nl,
        out_shape=[jax.ShapeDtypeStruct((t, D_MODEL), F32), jax.ShapeDtypeStruct((t, D_FF), BF),
                   jax.ShapeDtypeStruct((t, D_FF), BF), jax.ShapeDtypeStruct((t, D_FF), BF),
                   jax.ShapeDtypeStruct((t, D_FF), BF),
                   jax.ShapeDtypeStruct((t, D_MODEL), BF), jax.ShapeDtypeStruct((t, D_MODEL), BF)]
        + [jax.ShapeDtypeStruct((N_CHIPS,) + w.shape, w.dtype) for w in late],
        scratch_shapes=[pltpu.VMEM((tm + 8, D_FF), F32)] + (sems if nl else []),
        compiler_params=_params(56),
    )(x, ycn, ya, wout, wg4, wu4, g_oa, g_ffn, fcw, fcb, *late)
    return tuple(res[0:7]) + (list(res[7:]),)


def _fwd_tail(x1, act, p, target, wd4, wpg, wpp4, g_ple, tm):
    t = x1.shape[0]
    nt = t // tm

    def body(x1_ref, act_ref, p_ref, tgt_ref, wd_ref, wpg_ref, wpp_ref, g_ref,
             dx2_ref, h3_ref, ds_ref, dpp_ref, dg_ref, loss_ref, lacc):
        i = pl.program_id(0)

        @pl.when(i == 0)
        def _():
            dg_ref[...] = jnp.zeros_like(dg_ref)
            lacc[...] = jnp.zeros_like(lacc)

        x2 = x1_ref[...]
        for lo, hi in FF_SLABS:
            x2 = x2 + _mm(act_ref[:, lo:hi], wd_ref[lo:hi, :])
        r3 = _rstd(x2)
        xh = x2 * r3
        h3 = (xh * g_ref[...]).astype(BF)
        h3_ref[...] = h3
        sg = jax.nn.sigmoid(_mm(h3, wpg_ref[...]))
        pb = p_ref[...].astype(BF)
        pp = jnp.concatenate([_mm(pb, wpp_ref[s]) for s in range(N_CHIPS)], axis=1)
        err = (x2 + sg * pp) - tgt_ref[...]
        lacc[...] += _colsum(err * err)
        dx3 = err * (1.0 / D_MODEL)
        dpp_ref[...] = (dx3 * sg).astype(BF)
        dsb = ((dx3 * pp) * (sg * (1.0 - sg))).astype(BF)
        ds_ref[...] = dsb
        dh3 = _mm_nt(dsb, wpg_ref[...])
        dg_ref[...] += _colsum(dh3 * xh)
        dx2_ref[...] = dx3 + _norm_bwd(dh3, xh, r3, g_ref[...])

        @pl.when(i == nt - 1)
        def _():
            loss_ref[...] = jnp.full((1, 128), jnp.sum(lacc[...]) * (0.5 / D_MODEL), F32)

    return pl.pallas_call(
        body, name="fwd_tail", grid=(nt,),
        in_specs=[_rows(tm, D_MODEL), _rows(tm, D_FF), _rows(tm, PLE_DIM), _rows(tm, D_MODEL),
                  _const((D_FF, D_MODEL)), _const((D_MODEL, D_MODEL)),
                  _const((N_CHIPS, PLE_DIM, PLE_DIM)), _const((1, D_MODEL))],
        out_specs=[_rows(tm, D_MODEL), _rows(tm, D_MODEL), _rows(tm, D_MODEL), _rows(tm, D_MODEL),
                   pl.BlockSpec((1, D_MODEL), lambda i: (0, 0)), pl.BlockSpec((1, 128), lambda i: (0, 0))],
        out_shape=[jax.ShapeDtypeStruct((t, D_MODEL), F32), jax.ShapeDtypeStruct((t, D_MODEL), BF),
                   jax.ShapeDtypeStruct((t, D_MODEL), BF), jax.ShapeDtypeStruct((t, D_MODEL), BF),
                   jax.ShapeDtypeStruct((1, D_MODEL), F32), jax.ShapeDtypeStruct((1, 128), F32)],
        scratch_shapes=[pltpu.VMEM((1, D_MODEL), F32)],
        compiler_params=_params(48),
    )(x1, act, p, target, wd4, wpg, wpp4, g_ple)


def _bwd_ffn_a(dx2, gate, gp, up, wd4, fcw, tm, riders=()):
    t = dx2.shape[0]
    nt = t // tm
    nr = len(riders)

    def body(*refs):
        dx2_ref, gate_ref, gp_ref, up_ref, wd_ref, fcw_ref = refs[0:6]
        dgp_ref, dup_ref, dfcw_ref, dfcb_ref = refs[6 + nr:10 + nr]
        dbuf = refs[10 + 2 * nr]
        i = pl.program_id(0)
        if nr:
            _ride_scatter(i == 0, i == nt - 1, riders, refs[6:6 + nr], refs[10 + nr:10 + 2 * nr],
                          *refs[11 + 2 * nr:13 + 2 * nr])

        @pl.when(i == 0)
        def _():
            dbuf[tm:tm + 8, :] = jnp.zeros((8, D_FF), F32)
            dfcw_ref[...] = jnp.zeros_like(dfcw_ref)
            dfcb_ref[...] = jnp.zeros_like(dfcb_ref)

        dx2b = dx2_ref[...].astype(BF)
        for lo, hi in FF_SLABS:
            gate = gate_ref[:, lo:hi].astype(F32)
            gps = gp_ref[:, lo:hi].astype(F32)
            w0, w1, w2 = fcw_ref[0:1, lo:hi], fcw_ref[1:2, lo:hi], fcw_ref[2:3, lo:hi]
            sg = jax.nn.sigmoid(gate)
            dact = _mm_nt(dx2b, wd_ref[lo:hi, :])
            dup_ref[:, lo:hi] = (dact * (gate * sg)).astype(BF)
            dgate = (dact * up_ref[:, lo:hi].astype(F32)) * (sg * (1.0 + gate * (1.0 - sg)))
            dbuf[0:tm, lo:hi] = dgate
            db = dbuf[:, lo:hi]
            d1 = _shift_up(db, 1, tm)
            d2 = _shift_up(db, 2, tm)
            dfcb_ref[:, lo:hi] += _colsum(dgate)
            dfcw_ref[0:1, lo:hi] += _colsum(d2 * gps)
            dfcw_ref[1:2, lo:hi] += _colsum(d1 * gps)
            dfcw_ref[2:3, lo:hi] += _colsum(dgate * gps)
            dgp_ref[:, lo:hi] = (w2 * dgate + w1 * d1 + w0 * d2).astype(BF)
        dbuf[tm:tm + 8, :] = dbuf[0:8, :]

    r_in, r_out, r_sems = _rider_specs(riders)
    res = pl.pallas_call(
        body, name="bwd_ffn_a", grid=(nt,),
        in_specs=[_rows(tm, D_MODEL, nt), _rows(tm, D_FF, nt), _rows(tm, D_FF, nt), _rows(tm, D_FF, nt),
                  _const((D_FF, D_MODEL)), _const((3, D_FF))] + r_in,
        out_specs=[_rows(tm, D_FF, nt), _rows(tm, D_FF, nt),
                   pl.BlockSpec((3, D_FF), lambda i: (0, 0)), pl.BlockSpec((1, D_FF), lambda i: (0, 0))] + r_in,
        out_shape=[jax.ShapeDtypeStruct((t, D_FF), BF), jax.ShapeDtypeStruct((t, D_FF), BF),
                   jax.ShapeDtypeStruct((3, D_FF), F32), jax.ShapeDtypeStruct((1, D_FF), F32)] + r_out,
        scratch_shapes=[pltpu.VMEM((tm + 8, D_FF), F32)] + r_sems,
        compiler_params=_params(56),
    )(dx2, gate, gp, up, wd4, fcw, *[g for g, _ in riders])
    return res[0], res[1], res[2], res[3], list(res[4:])


def _bwd_ffn_b(dgp, dup, dx2, x1, ya, wg4, wu4, wout, g_ffn, g_oa, tm):
    t = dx2.shape[0]
    nt = t // tm

    def body(dgp_ref, dup_ref, dx2_ref, x1_ref, ya_ref, wg_ref, wu_ref, wout_ref, gffn_ref, goa_ref,
             dx1_ref, dycn_ref, dya_ref, dgffn_ref, dgoa_ref):
        @pl.when(pl.program_id(0) == 0)
        def _():
            dgffn_ref[...] = jnp.zeros_like(dgffn_ref)
            dgoa_ref[...] = jnp.zeros_like(dgoa_ref)

        dh2 = jnp.zeros((tm, D_MODEL), F32)
        for lo, hi in FF_SLABS:
            dh2 = dh2 + _mm(dgp_ref[:, lo:hi], wg_ref[lo:hi, :]) + _mm(dup_ref[:, lo:hi], wu_ref[lo:hi, :])
        x1 = x1_ref[...]
        r2 = _rstd(x1)
        xh = x1 * r2
        dgffn_ref[...] += _colsum(dh2 * xh)
        dx1 = dx2_ref[...] + _norm_bwd(dh2, xh, r2, gffn_ref[...])
        dx1_ref[...] = dx1
        dy = _mm_nt(dx1.astype(BF), wout_ref[...])
        dycn_ref[...] = dy[:, 0:CONV_W]
        dyan = dy[:, CONV_W:D_MODEL]
        yat = ya_ref[...]
        ra = _rstd(yat)
        yah = yat * ra
        dgoa_ref[...] += _colsum(dyan * yah)
        dya_ref[...] = _norm_bwd(dyan, yah, ra, goa_ref[...])

    return pl.pallas_call(
        body, name="bwd_ffn_b", grid=(nt,),
        in_specs=[_rows(tm, D_FF), _rows(tm, D_FF), _rows(tm, D_MODEL), _rows(tm, D_MODEL),
                  _rows(tm, ATTN_W), _const((D_FF, D_MODEL)), _const((D_FF, D_MODEL)),
                  _const((D_MODEL, D_MODEL)), _const((1, D_MODEL)), _const((1, ATTN_W))],
        out_specs=[_rows(tm, D_MODEL), _rows(tm, CONV_W), _rows(tm, ATTN_W),
                   pl.BlockSpec((1, D_MODEL), lambda i: (0, 0)), pl.BlockSpec((1, ATTN_W), lambda i: (0, 0))],
        out_shape=[jax.ShapeDtypeStruct((t, D_MODEL), F32), jax.ShapeDtypeStruct((t, CONV_W), F32),
                   jax.ShapeDtypeStruct((t, ATTN_W), F32),
                   jax.ShapeDtypeStruct((1, D_MODEL), F32), jax.ShapeDtypeStruct((1, ATTN_W), F32)],
        compiler_params=_params(48),
    )(dgp, dup, dx2, x1, ya, wg4, wu4, wout, g_ffn, g_oa)


def _bwd_mix(x, zbcx, qkv, dycn, dq, dk, dv, conv_w, conv_b, g_oc, g_mix, gm, gq8, gk8, tm):
    t = x.shape[0]
    nt = t // tm

    def body(x_ref, z_ref, zh_ref, qkv_ref, dycn_ref, dq_ref, dk_ref, dv_ref, cw_ref, cb_ref,
             goc_ref, g_ref, gm_ref, gq_ref, gk_ref,
             dz_ref, dcw_ref, dcb_ref, dgoc_ref, dgq_ref, dgk_ref, gw32_ref, gw16_ref,
             ubuf, dbuf, wacc, wstage, osem):
        i = pl.program_id(0)

        @pl.when(i == 0)
        def _():
            wacc[...] = jnp.zeros_like(wacc)
            dbuf[tm:tm + 8, :] = jnp.zeros((8, CONV_W), F32)
            dcw_ref[...] = jnp.zeros_like(dcw_ref)
            dcb_ref[...] = jnp.zeros_like(dcb_ref)
            dgoc_ref[...] = jnp.zeros_like(dgoc_ref)
            dgq_ref[...] = jnp.zeros_like(dgq_ref)
            dgk_ref[...] = jnp.zeros_like(dgk_ref)

        not_first_tile = i < nt - 1
        zb = z_ref[:, 0:512]
        zc = z_ref[:, 512:1024]
        zx = z_ref[:, 1024:1536]
        u = zc * zx
        ubuf[0:8, :] = jnp.where(not_first_tile, zh_ref[:, 512:1024] * zh_ref[:, 1024:1536], 0.0)
        ubuf[8:8 + tm, :] = u
        ub = ubuf[...]
        u1 = _shift_down(ub, 1, tm)
        u2 = _shift_down(ub, 2, tm)
        w0, w1, w2 = cw_ref[0:1, :], cw_ref[1:2, :], cw_ref[2:3, :]
        cv = w0 * u2 + w1 * u1 + w2 * u + cb_ref[...]
        yc = zb * cv
        rc = _rstd(yc)
        ych = yc * rc
        dycn = dycn_ref[...]
        dgoc_ref[...] += _colsum(dycn * ych)
        dyc = _norm_bwd(dycn, ych, rc, goc_ref[...])
        dcv = dyc * zb
        dcb_ref[...] += _colsum(dcv)
        dcw_ref[0:1, :] += _colsum(dcv * u2)
        dcw_ref[1:2, :] += _colsum(dcv * u1)
        dcw_ref[2:3, :] += _colsum(dcv * u)
        dbuf[0:tm, :] = dcv
        db = dbuf[...]
        du = w2 * dcv + w1 * _shift_up(db, 1, tm) + w0 * _shift_up(db, 2, tm)
        dbuf[tm:tm + 8, :] = dbuf[0:8, :]
        dz_ref[:, 0:512] = (dyc * cv).astype(BF)
        dz_ref[:, 512:1024] = (du * zx).astype(BF)
        dz_ref[:, 1024:1536] = (du * zc).astype(BF)
        for z0, d_ref, gg_ref, acc_ref, sc in ((0, dq_ref, gq_ref, dgq_ref, HEAD_DIM ** -0.5),
                                               (512, dk_ref, gk_ref, dgk_ref, 1.0)):
            z = qkv_ref[:, z0:z0 + 512]
            rr = lax.rsqrt(_head_mean(z * z, gm_ref) + EPS)
            zh = z * rr
            dn = d_ref[...] * sc
            acc_ref[...] += _colsum(dn * zh)
            dzh = dn * gg_ref[...]
            dz_ref[:, 1536 + z0:1536 + z0 + 512] = (rr * (dzh - zh * _head_mean(dzh * zh, gm_ref))).astype(BF)
        dz_ref[:, 2560:3072] = dv_ref[...].astype(BF)
        xt = x_ref[...]
        h1 = ((xt * _rstd(xt)) * g_ref[...]).astype(BF)
        for s in range(N_CHIPS):
            cols = slice(s * IN_SLAB, (s + 1) * IN_SLAB)
            wacc[:, cols] += _mm_tn(h1, dz_ref[:, cols])

        @pl.when(i == nt - 1)
        def _():
            wstage[...] = wacc[...].astype(BF)
            out32 = pltpu.make_async_copy(wacc, gw32_ref, osem.at[0])
            out16 = pltpu.make_async_copy(wstage, gw16_ref, osem.at[1])
            out32.start()
            out16.start()
            out32.wait()
            out16.wait()

    def acc(width, rows=1):
        return pl.BlockSpec((rows, width), lambda i: (0, 0))

    return pl.pallas_call(
        body, name="bwd_mix", grid=(nt,),
        in_specs=[_rows(tm, D_MODEL, nt), _rows(tm, 1536, nt), _halo(tm, 1536, nt),
                  _rows(tm, 1536, nt), _rows(tm, CONV_W, nt), _rows(tm, ATTN_W, nt), _rows(tm, ATTN_W, nt),
                  _rows(tm, ATTN_W, nt),
                  _const((3, CONV_W)), _const((1, CONV_W)), _const((1, CONV_W)), _const((1, D_MODEL)),
                  _const((ATTN_W, ATTN_W)), _const((1, ATTN_W)), _const((1, ATTN_W))],
        out_specs=[_rows(tm, 3072, nt),
                   acc(CONV_W, 3), acc(CONV_W), acc(CONV_W), acc(ATTN_W), acc(ATTN_W),
                   pl.BlockSpec(memory_space=pl.ANY), pl.BlockSpec(memory_space=pl.ANY)],
        out_shape=[jax.ShapeDtypeStruct((t, 3072), BF), jax.ShapeDtypeStruct((3, CONV_W), F32),
                   jax.ShapeDtypeStruct((1, CONV_W), F32), jax.ShapeDtypeStruct((1, CONV_W), F32),
                   jax.ShapeDtypeStruct((1, ATTN_W), F32), jax.ShapeDtypeStruct((1, ATTN_W), F32),
                   jax.ShapeDtypeStruct((D_MODEL, 3072), F32), jax.ShapeDtypeStruct((D_MODEL, 3072), BF)],
        scratch_shapes=[pltpu.VMEM((tm + 8, CONV_W), F32), pltpu.VMEM((tm + 8, CONV_W), F32),
                        pltpu.VMEM((D_MODEL, 3072), F32), pltpu.VMEM((D_MODEL, 3072), BF),
                        pltpu.SemaphoreType.DMA((2,))],
        compiler_params=_params(56),
    )(x, zbcx, zbcx, qkv, dycn, dq, dk, dv, conv_w, conv_b, g_oc, g_mix, gm, gq8, gk8)


def _bwd_in(x, dx1, dz, win4, g_mix, tm, riders=()):
    t = x.shape[0]
    nt = t // tm
    nr = len(riders)

    def body(*refs):
        x_ref, dx1_ref, dz_ref, w_ref, g_ref = refs[0:5]
        gx_ref, dg_ref = refs[5 + nr:7 + nr]
        i = pl.program_id(0)
        if nr:
            _ride_scatter(i == 0, i == nt - 1, riders, refs[5:5 + nr], refs[7 + nr:7 + 2 * nr],
                          *refs[7 + 2 * nr:9 + 2 * nr])

        @pl.when(i == 0)
        def _():
            dg_ref[...] = jnp.zeros_like(dg_ref)

        dh1 = jnp.zeros((tm, D_MODEL), F32)
        for s in range(N_CHIPS):
            dh1 = dh1 + _mm_nt(dz_ref[:, s * IN_SLAB:(s + 1) * IN_SLAB], w_ref[s])
        xt = x_ref[...]
        r1 = _rstd(xt)
        xh = xt * r1
        dg_ref[...] += _colsum(dh1 * xh)
        gx_ref[...] = dx1_ref[...] + _norm_bwd(dh1, xh, r1, g_ref[...])

    r_in, r_out, r_sems = _rider_specs(riders)
    res = pl.pallas_call(
        body, name="bwd_in", grid=(nt,),
        in_specs=[_rows(tm, D_MODEL), _rows(tm, D_MODEL), _rows(tm, 3072), _const((N_CHIPS, D_MODEL, IN_SLAB)),
                  _const((1, D_MODEL))] + r_in,
        out_specs=[_rows(tm, D_MODEL), pl.BlockSpec((1, D_MODEL), lambda i: (0, 0))] + r_in,
        out_shape=[jax.ShapeDtypeStruct((t, D_MODEL), F32), jax.ShapeDtypeStruct((1, D_MODEL), F32)] + r_out,
        scratch_shapes=r_sems,
        compiler_params=_params(48),
    )(x, dx1, dz, win4, g_mix, *[g for g, _ in riders])
    return res[0], res[1], list(res[2:])


def _wgrad(a, b, tn, tt, name):
    t, k = a.shape
    n = b.shape[1]
    nt = t // tt

    def body(a_ref, b_ref, o_ref, ob_ref):
        @pl.when(pl.program_id(1) == 0)
        def _():
            o_ref[...] = jnp.zeros_like(o_ref)

        o_ref[...] += _mm_tn(a_ref[...].astype(BF), b_ref[...].astype(BF))

        @pl.when(pl.program_id(1) == nt - 1)
        def _():
            ob_ref[...] = o_ref[...].astype(BF)

    spec = pl.BlockSpec((k, tn), lambda j, i: (0, j))
    return pl.pallas_call(
        body, name=name, grid=(n // tn, nt),
        in_specs=[pl.BlockSpec((tt, k), lambda j, i: (i, 0)), pl.BlockSpec((tt, tn), lambda j, i: (i, j))],
        out_specs=[spec, spec],
        out_shape=[jax.ShapeDtypeStruct((k, n), F32), jax.ShapeDtypeStruct((k, n), BF)],
        compiler_params=_params(48, 2),
    )(a, b)


def _gather_weights(shards, pack):
    nw = len(shards)

    def body(*refs):
        ins = refs[:nw]
        pack_ref = refs[nw]
        outs = refs[nw + 1:2 * nw + 1]
        pack_out = refs[2 * nw + 1]
        send_sems, recv_sems, local_sems = refs[2 * nw + 2:]
        x, y, c = _place()
        me = 2 * x + y
        local, remote = [], []

        def sem(w, j):
            return w * 6 + j

        def push(src, dst, w, j, to):
            return pltpu.make_async_remote_copy(src_ref=src, dst_ref=dst, send_sem=send_sems.at[sem(w, j)],
                                                recv_sem=recv_sems.at[sem(w, j)], device_id=to, device_id_type=MESH)

        def half_rows(w, h):
            half = ins[w].shape[0] // 2
            return pl.ds(pl.multiple_of(h * half, 16), half)

        for w in range(nw):
            local.append(pltpu.make_async_copy(ins[w], outs[w].at[me], local_sems.at[w]))
            for k in (1, 2, 3):
                px, py = _chip_peer(x, y, k)
                mine = half_rows(w, c)
                remote.append(push(ins[w].at[mine], outs[w].at[me, mine], w, k - 1, (px, py, c)))
        local.append(pltpu.make_async_copy(pack_ref, pack_out.at[me], local_sems.at[nw]))
        for k in (1, 2, 3):
            px, py = _chip_peer(x, y, k)
            remote.append(push(pack_ref, pack_out.at[me], nw, k - 1, (px, py, c)))
        for cp in local + remote:
            cp.start()
        for w in range(nw):
            for k in (1, 2, 3):
                landed = outs[w].at[me ^ k, half_rows(w, c)]
                push(landed, landed, w, k - 1, (x, y, c)).wait_recv()
                fw = push(landed, landed, w, 2 + k, (x, y, 1 - c))
                fw.start()
                remote.append(fw)
        for k in (1, 2, 3):
            landed = pack_out.at[me ^ k]
            push(landed, landed, nw, k - 1, (x, y, c)).wait_recv()
        for w in range(nw):
            for k in (1, 2, 3):
                landed = outs[w].at[me ^ k, half_rows(w, 1 - c)]
                push(landed, landed, w, 2 + k, (x, y, c)).wait_recv()
        for cp in remote:
            cp.wait_send()
        for cp in local:
            cp.wait()

    any_spec = pl.BlockSpec(memory_space=pl.ANY)
    out_shape = [jax.ShapeDtypeStruct((N_CHIPS,) + s.shape, s.dtype) for s in shards]
    out_shape.append(jax.ShapeDtypeStruct((N_CHIPS,) + pack.shape, pack.dtype))
    return pl.pallas_call(
        body, name="gather_weights",
        in_specs=[any_spec] * (nw + 1), out_specs=[any_spec] * (nw + 1), out_shape=out_shape,
        scratch_shapes=[pltpu.SemaphoreType.DMA(((nw + 1) * 6,)), pltpu.SemaphoreType.DMA(((nw + 1) * 6,)),
                        pltpu.SemaphoreType.DMA((nw + 1,))],
    )(*shards, pack)


def _adamw(w, g, m, v):
    m = ADAM_B1 * m + (1.0 - ADAM_B1) * g
    v = ADAM_B2 * v + (1.0 - ADAM_B2) * (g * g)
    m_hat = m / (1.0 - ADAM_B1 ** ADAM_STEP)
    v_hat = v / (1.0 - ADAM_B2 ** ADAM_STEP)
    delta = -ADAM_LR * (m_hat / (jnp.sqrt(v_hat) + ADAM_EPS) + ADAM_WD * w)
    return delta, m, v


def _finish_reduce(grad, slots, col_sharded, name):
    r, cw = _piece_shape(grad.shape, col_sharded)
    chunk = 32
    assert r % chunk == 0

    def body(g_hbm, slots_ref, full, own, lsem, c_send, c_recv):
        x, y, c = _place()
        cp = pltpu.make_async_copy(g_hbm.at[_piece_window(col_sharded, r, cw, 2 * x + y, c)], own, lsem)
        cp.start()
        cp.wait()
        mine = pl.multiple_of(c * r, 8)

        def add(j, carry):
            rows = pl.ds(pl.multiple_of(j * chunk, 8), chunk)
            tot = own[rows, :]
            for k in range(7):
                tot = tot + slots_ref[k, rows, :].astype(F32)
            full[pl.ds(mine + pl.multiple_of(j * chunk, 8), chunk), :] = tot
            return carry

        lax.fori_loop(0, r // chunk, add, 0)
        half = full.at[pl.ds(mine, r), :]
        swap = pltpu.make_async_remote_copy(src_ref=half, dst_ref=half, send_sem=c_send, recv_sem=c_recv,
                                            device_id=(x, y, 1 - c), device_id_type=MESH)
        swap.start()
        swap.wait()

    vmem = pl.BlockSpec(memory_space=pltpu.VMEM)
    return pl.pallas_call(
        body, name=name, in_specs=[pl.BlockSpec(memory_space=pltpu.HBM), vmem], out_specs=vmem,
        out_shape=jax.ShapeDtypeStruct((2 * r, cw), F32),
        scratch_shapes=[pltpu.VMEM((r, cw), F32), pltpu.SemaphoreType.DMA, pltpu.SemaphoreType.DMA,
                        pltpu.SemaphoreType.DMA],
        compiler_params=pltpu.CompilerParams(vmem_limit_bytes=32 * MIB),
    )(grad, slots)


def _adamw_big(g, w, m, v, name):
    vr, vc = w.shape
    assert g.shape == w.shape
    rows = 64

    def body(g_ref, w_ref, m_ref, v_ref, go_ref, do_ref, mo_ref, vo_ref):
        gg = g_ref[...]
        delta, mn, vn = _adamw(w_ref[...], gg, m_ref[...], v_ref[...])
        go_ref[...] = gg
        do_ref[...] = delta
        mo_ref[...] = mn
        vo_ref[...] = vn

    blk = pl.BlockSpec((rows, vc), lambda i: (i, 0))
    shard = jax.ShapeDtypeStruct((vr, vc), F32)
    return pl.pallas_call(
        body, name=name, grid=(vr // rows,),
        in_specs=[blk, blk, blk, blk], out_specs=[blk] * 4,
        out_shape=[shard] * 4, compiler_params=_params(32),
    )(g, w, m, v)


def _allreduce_small(pack):
    rows = pack.shape[0]

    def body(p_ref, o_ref, slots, send_sems, recv_sems):
        x, y, c = _place()
        me = 4 * x + 2 * y + c
        slots[me] = p_ref[...]
        sends = []
        for k in range(1, 8):
            cp = pltpu.make_async_remote_copy(
                src_ref=p_ref, dst_ref=slots.at[me], send_sem=send_sems.at[k - 1], recv_sem=recv_sems.at[k - 1],
                device_id=(x ^ (k >> 2), y ^ ((k >> 1) & 1), c ^ (k & 1)), device_id_type=MESH)
            cp.start()
            sends.append(cp)
        for cp in sends:
            cp.wait()
        tot = slots[0]
        for j in range(1, 8):
            tot = tot + slots[j]
        o_ref[...] = tot

    vmem = pl.BlockSpec(memory_space=pltpu.VMEM)
    return pl.pallas_call(
        body, name="allreduce_small", in_specs=[vmem], out_specs=vmem,
        out_shape=jax.ShapeDtypeStruct(pack.shape, F32),
        scratch_shapes=[pltpu.VMEM((8, rows, D_MODEL), F32), pltpu.SemaphoreType.DMA((7,)),
                        pltpu.SemaphoreType.DMA((7,))],
    )(pack)


def _adamw_small(ws, gs, ms, vs):
    n = len(ws)

    def body(*refs):
        w_refs, g_refs, m_refs, v_refs = refs[0:n], refs[n:2 * n], refs[2 * n:3 * n], refs[3 * n:4 * n]
        d_refs, mo_refs, vo_refs = refs[4 * n:5 * n], refs[5 * n:6 * n], refs[6 * n:7 * n]
        for j in range(n):
            delta, mn, vn = _adamw(w_refs[j][...], g_refs[j][...], m_refs[j][...], v_refs[j][...])
            d_refs[j][...] = delta
            mo_refs[j][...] = mn
            vo_refs[j][...] = vn

    vmem = pl.BlockSpec(memory_space=pltpu.VMEM)
    shapes = [jax.ShapeDtypeStruct(w.shape, F32) for w in ws]
    outs = pl.pallas_call(
        body, name="adamw_small", in_specs=[vmem] * (4 * n), out_specs=[vmem] * (3 * n), out_shape=shapes * 3,
    )(*ws, *gs, *ms, *vs)
    return outs[0:n], outs[n:2 * n], outs[2 * n:3 * n]


def _local_step(x, p, target, wts, late=None):
    (win4, wout, wg4, wu4, wd4, wpg, wpp4, conv_w, fcw, g_mix, conv_b, gq, gk, g_oc, g_oa, g_ffn, fcb, g_ple) = wts
    comm = late is not None
    gm = jnp.kron(jnp.eye(N_HEADS, dtype=F32), jnp.full((HEAD_DIM, HEAD_DIM), 1.0 / HEAD_DIM, F32)).astype(BF)
    gq8, gk8 = jnp.tile(gq, (1, N_HEADS)), jnp.tile(gk, (1, N_HEADS))
    mb = _mask_table()
    zbcx, qkv, ycn, qkn = _fwd_mix(x, g_mix, win4, conv_w, conv_b, g_oc, gm, gq8, gk8, 512)
    ya, lse, gathered = _attn_fwd(qkn, qkv, mb, late[0:3] if comm else ())
    if comm:
        wout, wg4, wu4 = (g.reshape(-1, D_MODEL) for g in gathered)
    x1, gp, up, gate, act, ycat, h2, gathered = _fwd_ffn(x, ycn, ya, wout, wg4, wu4, g_oa, g_ffn, fcw, fcb, 256,
                                                    late[3:6] if comm else ())
    if comm:
        wd4, wpg, wpp4 = gathered
        wd4, wpg = wd4.reshape(D_FF, D_MODEL), wpg.reshape(D_MODEL, D_MODEL)
    dx2, h3, ds, dpp, dg_ple, loss = _fwd_tail(x1, act, p, target, wd4, wpg, wpp4, g_ple, 512)
    big, big16, slots = {}, {}, {}

    def wgrad(name, a, b, tn):
        big[name], big16[name] = _wgrad(a, b, tn, 1024, "wgrad_" + name)
        return (big16[name], _COL_SHARDED[name])

    riders = [wgrad("w_down", act, dx2, 512), wgrad("w_ple_gate", h3, ds, 1024), wgrad("w_ple_proj", p, dpp, 1024)]
    dgp, dup, dfcw, dfcb, got = _bwd_ffn_a(dx2, gate, gp, up, wd4, fcw, 512, riders if comm else ())
    slots.update(zip(("w_down", "w_ple_gate", "w_ple_proj"), got))
    riders = [wgrad("w_gate", dgp, h2, 512), wgrad("w_up", dup, h2, 512)]
    dx1, dycn, dya, dg_ffn, dg_oa = _bwd_ffn_b(dgp, dup, dx2, x1, ya, wg4, wu4, wout, g_ffn, g_oa, 512)
    riders.append(wgrad("w_out", ycat, dx1, 1024))
    dq, dk, dv, got = _attn_bwd(qkn, qkv, ya, lse, dya, mb, riders if comm else ())
    slots.update(zip(("w_gate", "w_up", "w_out"), got))
    dz, dcw, dcb, dg_oc, dgq8, dgk8, big["w_in"], big16["w_in"] = _bwd_mix(
        x, zbcx, qkv, dycn, dq, dk, dv, conv_w, conv_b, g_oc, g_mix, gm, gq8, gk8, 512)
    riders = [(big16["w_in"], _COL_SHARDED["w_in"])]
    grad_x, dg_mix, got = _bwd_in(x, dx1, dz, win4, g_mix, 512, riders if comm else ())
    slots.update(zip(("w_in",), got))
    dgq = dgq8.reshape(N_HEADS, HEAD_DIM).sum(axis=0, keepdims=True)
    dgk = dgk8.reshape(N_HEADS, HEAD_DIM).sum(axis=0, keepdims=True)
    small = dict(g_mix=dg_mix, conv_w=dcw, conv_b=dcb, q_norm_g=dgq, k_norm_g=dgk, g_out_conv=dg_oc,
                 g_out_attn=dg_oa, g_ffn=dg_ffn, ffn_conv_w=dfcw, ffn_conv_b=dfcb, g_ple=dg_ple)
    return loss[0, 0], grad_x, big, slots, small


_SMALL_ROWS = 24


def _pack_small(s, loss):
    z64 = jnp.zeros((1, 1024 - 512 - 128), F32)
    rows = [s["g_mix"], s["g_ffn"], s["g_ple"],
            jnp.concatenate([s["conv_b"], s["g_out_conv"]], axis=1),
            jnp.concatenate([s["g_out_attn"], s["q_norm_g"], s["k_norm_g"], z64], axis=1),
            jnp.pad(s["conv_w"], ((0, 0), (0, 512))),
            jnp.pad(s["ffn_conv_b"], ((0, 0), (0, 3072 - D_FF))).reshape(3, 1024),
            jnp.pad(s["ffn_conv_w"], ((0, 0), (0, 3072 - D_FF))).reshape(9, 1024),
            jnp.pad(loss.reshape(1, 1), ((0, 0), (0, 1023))),
            jnp.zeros((_SMALL_ROWS - 21, 1024), F32)]
    return jnp.concatenate(rows, axis=0)


def _unpack_small(t):
    return dict(g_mix=t[0:1], g_ffn=t[1:2], g_ple=t[2:3], conv_b=t[3:4, 0:512], g_out_conv=t[3:4, 512:1024],
                g_out_attn=t[4:5, 0:512], q_norm_g=t[4:5, 512:576], k_norm_g=t[4:5, 576:640],
                conv_w=t[5:8, 0:512], ffn_conv_b=t[8:11].reshape(1, 3072)[:, :D_FF],
                ffn_conv_w=t[11:20].reshape(3, 3072)[:, :D_FF], loss=t[20, 0])


_BIG = ("w_in", "w_out", "w_gate", "w_up", "w_down", "w_ple_gate", "w_ple_proj")
_COL_SHARDED = dict(w_in=True, w_out=False, w_gate=False, w_up=False, w_down=False, w_ple_gate=False, w_ple_proj=True)
_TRANSPOSED = ("w_gate", "w_up")
_WEIGHTS = ("g_mix", "w_in", "conv_w", "conv_b", "q_norm_g", "k_norm_g", "g_out_conv", "g_out_attn", "w_out",
            "g_ffn", "w_gate", "w_up", "ffn_conv_w", "ffn_conv_b", "w_down", "g_ple", "w_ple_gate", "w_ple_proj")


def kernel(x, p, g_mix, w_in, conv_w, conv_b, q_norm_g, k_norm_g, g_out_conv, g_out_attn, w_out, g_ffn, w_gate, w_up, ffn_conv_w, ffn_conv_b, w_down, g_ple, w_ple_gate, w_ple_proj, loss_target, m_g_mix, m_w_in, m_conv_w, m_conv_b, m_q_norm_g, m_k_norm_g, m_g_out_conv, m_g_out_attn, m_w_out, m_g_ffn, m_w_gate, m_w_up, m_ffn_conv_w, m_ffn_conv_b, m_w_down, m_g_ple, m_w_ple_gate, m_w_ple_proj, v_g_mix, v_w_in, v_conv_w, v_conv_b, v_q_norm_g, v_k_norm_g, v_g_out_conv, v_g_out_attn, v_w_out, v_g_ffn, v_w_gate, v_w_up, v_ffn_conv_w, v_ffn_conv_b, v_w_down, v_g_ple, v_w_ple_gate, v_w_ple_proj):
    w = dict(g_mix=g_mix, w_in=w_in, conv_w=conv_w, conv_b=conv_b, q_norm_g=q_norm_g, k_norm_g=k_norm_g,
             g_out_conv=g_out_conv, g_out_attn=g_out_attn, w_out=w_out, g_ffn=g_ffn, w_gate=w_gate, w_up=w_up,
             ffn_conv_w=ffn_conv_w, ffn_conv_b=ffn_conv_b, w_down=w_down, g_ple=g_ple, w_ple_gate=w_ple_gate,
             w_ple_proj=w_ple_proj)
    m = dict(g_mix=m_g_mix, w_in=m_w_in, conv_w=m_conv_w, conv_b=m_conv_b, q_norm_g=m_q_norm_g, k_norm_g=m_k_norm_g,
             g_out_conv=m_g_out_conv, g_out_attn=m_g_out_attn, w_out=m_w_out, g_ffn=m_g_ffn, w_gate=m_w_gate,
             w_up=m_w_up, ffn_conv_w=m_ffn_conv_w, ffn_conv_b=m_ffn_conv_b, w_down=m_w_down, g_ple=m_g_ple,
             w_ple_gate=m_w_ple_gate, w_ple_proj=m_w_ple_proj)
    v = dict(g_mix=v_g_mix, w_in=v_w_in, conv_w=v_conv_w, conv_b=v_conv_b, q_norm_g=v_q_norm_g, k_norm_g=v_k_norm_g,
             g_out_conv=v_g_out_conv, g_out_attn=v_g_out_attn, w_out=v_w_out, g_ffn=v_g_ffn, w_gate=v_w_gate,
             w_up=v_w_up, ffn_conv_w=v_ffn_conv_w, ffn_conv_b=v_ffn_conv_b, w_down=v_w_down, g_ple=v_g_ple,
             w_ple_gate=v_w_ple_gate, w_ple_proj=v_w_ple_proj)
    mats = [k for k, a in w.items() if a.ndim == 3]
    w = {k: (a[0] if k in mats else a) for k, a in w.items()}
    m = {k: (a[0] if k in mats else a) for k, a in m.items()}
    v = {k: (a[0] if k in mats else a) for k, a in v.items()}
    for n in _TRANSPOSED:
        w[n], m[n], v[n] = w[n].T, m[n].T, v[n].T
    chip = 2 * lax.axis_index("x") + lax.axis_index("y")

    late = [w[n].astype(BF) for n in ("w_out", "w_gate", "w_up", "w_down", "w_ple_gate", "w_ple_proj")]
    pack = jnp.pad(jnp.concatenate([w["conv_w"], w["ffn_conv_w"]], axis=1), ((0, 5), (0, 1024 - 128 - D_FF_SHARD)))
    win4, pack4 = _gather_weights([w["w_in"].astype(BF)], pack)
    conv_w_full = pack4[:, 0:3, 0:128].transpose(1, 0, 2).reshape(3, CONV_W)
    fcw_full = pack4[:, 0:3, 128:128 + D_FF_SHARD].transpose(1, 0, 2).reshape(3, D_FF)
    wts = (win4, None, None, None, None, None, None, conv_w_full, fcw_full, w["g_mix"], w["conv_b"], w["q_norm_g"],
           w["k_norm_g"], w["g_out_conv"], w["g_out_attn"], w["g_ffn"], w["ffn_conv_b"], w["g_ple"])

    loss, grad_x, big, slots, small = _local_step(x[0], p[0, 0], loss_target[0], wts, late)

    grads, deltas, new_m, new_v = {}, {}, {}, {}
    for name in _BIG:
        total = _finish_reduce(big[name], slots[name], _COL_SHARDED[name], "finish_" + name)
        grads[name], deltas[name], new_m[name], new_v[name] = _adamw_big(total, w[name], m[name], v[name],
                                                                         "adamw_" + name)
    tot = _unpack_small(_allreduce_small(_pack_small(small, loss)))
    loss = tot.pop("loss")
    tot["conv_w"] = lax.dynamic_slice_in_dim(tot["conv_w"], chip * 128, 128, axis=1)
    tot["ffn_conv_w"] = lax.dynamic_slice_in_dim(tot["ffn_conv_w"], chip * D_FF_SHARD, D_FF_SHARD, axis=1)
    names = [n for n in _WEIGHTS if n not in _BIG]
    d_s, m_s, v_s = _adamw_small([w[n] for n in names], [tot[n] for n in names], [m[n] for n in names],
                                 [v[n] for n in names])
    for j, n in enumerate(names):
        grads[n], deltas[n], new_m[n], new_v[n] = tot[n], d_s[j], m_s[j], v_s[j]

    out = [loss, grad_x[None]]
    for group in (grads, deltas, new_m, new_v):
        for n in _TRANSPOSED:
            group[n] = group[n].T
        out += [group[n][None] if n in mats else group[n] for n in _WEIGHTS]
    return tuple(out)
```

```python
import jax
import jax.numpy as jnp
from jax import lax
from jax.experimental import pallas as pl
from jax.experimental.pallas import tpu as pltpu

D_MODEL = 1024
CONV_W = 512
N_HEADS = 8
HEAD_DIM = 64
ATTN_W = 512
D_FF = 2816
D_FF_SHARD = 704
FF_SLABS = ((0, 1408), (1408, 2816))
IN_SLAB = 768
PLE_DIM = 256
N_CHIPS = 4
QBLK = 128
DILATIONS = (1, 4, 16)
EPS = 1e-6
NEG = -1e30
MESH = pl.DeviceIdType.MESH

ADAM_LR = 0.001
ADAM_B1 = 0.9
ADAM_B2 = 0.999
ADAM_EPS = 1e-08
ADAM_WD = 0.01
ADAM_STEP = 10

BF = jnp.bfloat16
F32 = jnp.float32
MIB = 1024 * 1024


def _mm(a, b):
    return jnp.dot(a, b, preferred_element_type=F32)


def _mm_nt(a, b):
    return lax.dot_general(a, b, (((1,), (1,)), ((), ())), preferred_element_type=F32)


def _mm_tn(a, b):
    return lax.dot_general(a, b, (((0,), (0,)), ((), ())), preferred_element_type=F32)


def _rstd(a):
    return lax.rsqrt(jnp.mean(a * a, axis=-1, keepdims=True) + EPS)


def _norm_bwd(dy, xh, r, g):
    dxh = dy * g
    return r * (dxh - xh * jnp.mean(dxh * xh, axis=-1, keepdims=True))


def _colsum(a):
    return jnp.sum(a, axis=0, keepdims=True)


def _head_mean(a, gm_ref):
    return _mm(a.astype(BF), gm_ref[...])


def _shift_down(buf, k, tm):
    return pltpu.roll(buf, k, axis=0)[8:8 + tm]


def _shift_up(buf, k, tm):
    return pltpu.roll(buf, tm + 8 - k, axis=0)[0:tm]


def _params(vmem_mib, n_grid=1):
    return pltpu.CompilerParams(dimension_semantics=("arbitrary",) * n_grid, vmem_limit_bytes=vmem_mib * MIB)


def _const(shape):
    n = len(shape)
    return pl.BlockSpec(shape, lambda *_: (0,) * n, pipeline_mode=pl.Buffered(1))


def _rows(tm, width, rev_of=None):
    if rev_of is None:
        return pl.BlockSpec((tm, width), lambda i: (i, 0))
    return pl.BlockSpec((tm, width), lambda i: (rev_of - 1 - i, 0))


def _halo(tm, width, nt):
    return pl.BlockSpec((8, width), lambda i: (jnp.maximum((nt - 1 - i) * (tm // 8) - 1, 0), 0))


def _fwd_mix(x, g_mix, win4, conv_w, conv_b, g_oc, gm, gq8, gk8, tm):
    t = x.shape[0]
    nt = t // tm

    def body(x_ref, g_ref, w_ref, cw_ref, cb_ref, goc_ref, gm_ref, gq_ref, gk_ref,
             zbcx_ref, qkv_ref, ycn_ref, qkn_ref, ubuf):
        @pl.when(pl.program_id(0) == 0)
        def _():
            ubuf[0:8, :] = jnp.zeros((8, CONV_W), F32)

        xt = x_ref[...]
        h = ((xt * _rstd(xt)) * g_ref[...]).astype(BF)
        zbcx_ref[:, 0:IN_SLAB] = _mm(h, w_ref[0])
        zbcx_ref[:, IN_SLAB:2 * IN_SLAB] = _mm(h, w_ref[1])
        qkv_ref[:, 0:IN_SLAB] = _mm(h, w_ref[2])
        qkv_ref[:, IN_SLAB:2 * IN_SLAB] = _mm(h, w_ref[3])
        u = zbcx_ref[:, 512:1024] * zbcx_ref[:, 1024:1536]
        ubuf[8:8 + tm, :] = u
        ub = ubuf[...]
        cv = (cw_ref[0:1, :] * _shift_down(ub, 2, tm) + cw_ref[1:2, :] * _shift_down(ub, 1, tm)
              + cw_ref[2:3, :] * u + cb_ref[...])
        ubuf[0:8, :] = ubuf[tm:tm + 8, :]
        yc = zbcx_ref[:, 0:512] * cv
        ycn_ref[...] = ((yc * _rstd(yc)) * goc_ref[...]).astype(BF)
        zq = qkv_ref[:, 0:512]
        zk = qkv_ref[:, 512:1024]
        rq = lax.rsqrt(_head_mean(zq * zq, gm_ref) + EPS)
        rk = lax.rsqrt(_head_mean(zk * zk, gm_ref) + EPS)
        qkn_ref[:, 0:512] = ((zq * rq) * gq_ref[...]) * (HEAD_DIM ** -0.5)
        qkn_ref[:, 512:1024] = (zk * rk) * gk_ref[...]

    return pl.pallas_call(
        body, name="fwd_mix", grid=(nt,),
        in_specs=[_rows(tm, D_MODEL), _const((1, D_MODEL)), _const((N_CHIPS, D_MODEL, IN_SLAB)),
                  _const((3, CONV_W)), _const((1, CONV_W)), _const((1, CONV_W)), _const((ATTN_W, ATTN_W)),
                  _const((1, ATTN_W)), _const((1, ATTN_W))],
        out_specs=[_rows(tm, 1536), _rows(tm, 1536), _rows(tm, CONV_W), _rows(tm, 1024)],
        out_shape=[jax.ShapeDtypeStruct((t, 1536), F32), jax.ShapeDtypeStruct((t, 1536), F32),
                   jax.ShapeDtypeStruct((t, CONV_W), BF), jax.ShapeDtypeStruct((t, 1024), F32)],
        scratch_shapes=[pltpu.VMEM((tm + 8, CONV_W), F32)],
        compiler_params=_params(48),
    )(x, g_mix, win4, conv_w, conv_b, g_oc, gm, gq8, gk8)


def _place():
    x, y, c = lax.axis_index("x"), lax.axis_index("y"), lax.axis_index("c")
    return x, y, c


def _chip_peer(x, y, k):
    return x ^ (k >> 1), y ^ (k & 1)


def _piece_shape(grad_shape, col_sharded):
    kk, nn = grad_shape
    return (kk // 2, nn // N_CHIPS) if col_sharded else (kk // (2 * N_CHIPS), nn)


def _piece_window(col_sharded, r, cw, s, h):
    if col_sharded:
        return (pl.ds(pl.multiple_of(h * r, 16), r), pl.ds(pl.multiple_of(s * cw, 128), cw))
    return (pl.ds(pl.multiple_of((2 * s + h) * r, 16), r), slice(None))


def _scatter_copies(g_ref, slots_ref, send_sems, recv_sems, base, col_sharded):
    x, y, c = _place()
    r, cw = slots_ref.shape[1:]
    copies = []
    for k in range(1, 8):
        tx, ty, tc = x ^ (k >> 2), y ^ ((k >> 1) & 1), c ^ (k & 1)
        copies.append(pltpu.make_async_remote_copy(
            src_ref=g_ref.at[_piece_window(col_sharded, r, cw, 2 * tx + ty, tc)], dst_ref=slots_ref.at[k - 1],
            send_sem=send_sems.at[base + k - 1], recv_sem=recv_sems.at[base + k - 1],
            device_id=(tx, ty, tc), device_id_type=MESH))
    return copies


def _ride_scatter(first, last, riders, g_refs, slot_refs, send_sems, recv_sems):
    def all_copies():
        out = []
        for j, (_, col_sharded) in enumerate(riders):
            out += _scatter_copies(g_refs[j], slot_refs[j], send_sems, recv_sems, 7 * j, col_sharded)
        return out

    @pl.when(first)
    def _():
        for cp in all_copies():
            cp.start()

    @pl.when(last)
    def _():
        for cp in all_copies():
            cp.wait()


def _rider_specs(riders):
    any_spec = pl.BlockSpec(memory_space=pl.ANY)
    shapes = [jax.ShapeDtypeStruct((7,) + _piece_shape(g.shape, cs), BF) for g, cs in riders]
    sems = [pltpu.SemaphoreType.DMA((7 * len(riders),)), pltpu.SemaphoreType.DMA((7 * len(riders),))] if riders else []
    return [any_spec] * len(riders), shapes, sems


class _Gather:
    def __init__(self, ins, outs, send_sems, recv_sems, local_sems):
        self.ins, self.outs = ins, outs
        self.send_sems, self.recv_sems, self.local_sems = send_sems, recv_sems, local_sems
        self.x, self.y, self.c = _place()
        self.me = 2 * self.x + self.y

    def _push(self, src, dst, w, j, to):
        return pltpu.make_async_remote_copy(src_ref=src, dst_ref=dst, send_sem=self.send_sems.at[6 * w + j],
                                            recv_sem=self.recv_sems.at[6 * w + j], device_id=to, device_id_type=MESH)

    def _half(self, w, h):
        half = self.ins[w].shape[0] // 2
        return pl.ds(pl.multiple_of(h * half, 16), half)

    def _local(self, w):
        return pltpu.make_async_copy(self.ins[w], self.outs[w].at[self.me], self.local_sems.at[w])

    def _ici(self, w, k):
        px, py = _chip_peer(self.x, self.y, k)
        mine = self._half(w, self.c)
        return self._push(self.ins[w].at[mine], self.outs[w].at[self.me, mine], w, k - 1, (px, py, self.c))

    def _landed(self, w, k, h):
        return self.outs[w].at[self.me ^ k, self._half(w, h)]

    def _fwd(self, w, k):
        landed = self._landed(w, k, self.c)
        return self._push(landed, landed, w, 2 + k, (self.x, self.y, 1 - self.c))

    def start(self):
        for w in range(len(self.ins)):
            self._local(w).start()
            for k in (1, 2, 3):
                self._ici(w, k).start()

    def forward(self):
        for w in range(len(self.ins)):
            for k in (1, 2, 3):
                landed = self._landed(w, k, self.c)
                self._push(landed, landed, w, k - 1, (self.x, self.y, self.c)).wait_recv()
                self._fwd(w, k).start()

    def finish(self):
        for w in range(len(self.ins)):
            for k in (1, 2, 3):
                landed = self._landed(w, k, 1 - self.c)
                self._push(landed, landed, w, 2 + k, (self.x, self.y, self.c)).wait_recv()
            for k in (1, 2, 3):
                self._ici(w, k).wait_send()
                self._fwd(w, k).wait_send()
            self._local(w).wait()


def _alibi(h):
    return 2.0 ** (-(h + 1))


CHUNK = 2048


def _mask_table():
    slopes = jnp.asarray([_alibi(h) for h in range(N_HEADS)], F32)[:, None, None]
    step = jnp.arange(QBLK)[:, None] + QBLK - jnp.arange(2 * QBLK)[None, :]
    valid = (step >= 0) & (step <= QBLK)
    tab = jnp.stack([jnp.where(valid[None], -slopes * (step * d)[None].astype(F32), NEG) for d in DILATIONS])
    return tab.reshape(3, N_HEADS // 2, 2 * QBLK, 2 * QBLK)


def _attn_fwd(qkn, qkv, mb, late=()):
    t = qkn.shape[0]
    nc = t // CHUNK
    nl = len(late)

    def body(*refs):
        qc_ref, kp_ref, kc_ref, vp_ref, vc_ref, mb_ref = refs[0:6]
        o_ref, l_ref = refs[6 + nl:8 + nl]
        ob0, ob1, ob2, lb0, lb1, lb2 = refs[8 + 2 * nl:14 + 2 * nl]
        if nl:
            gather = _Gather(refs[6:6 + nl], refs[8 + nl:8 + 2 * nl], *refs[14 + 2 * nl:17 + 2 * nl])
            step = pl.program_id(0) * nc + pl.program_id(1)
            pl.when(step == 0)(gather.start)
            pl.when(step == 2 * nc)(gather.forward)
            pl.when(step == (N_HEADS // 2) * nc - 1)(gather.finish)
        first = pl.program_id(1) == 0
        lane = lax.broadcasted_iota(jnp.int32, (QBLK, 128), 1)
        lo_half = lane < HEAD_DIM
        kj = lax.broadcasted_iota(jnp.int32, (2 * QBLK, 2 * QBLK), 1)
        no_prev = first & (kj < QBLK)
        obs, lbs = (ob0, ob1, ob2), (lb0, lb1, lb2)

        def by_head(a):
            return jnp.where(lo_half, a, 0.0).astype(BF), jnp.where(lo_half, 0.0, a).astype(BF)

        for di, d in enumerate(DILATIONS):
            span = d * QBLK
            for r in range(d):
                tail = pl.ds(CHUNK - span + r, QBLK, stride=d)
                k_prev = kp_ref[tail, :].astype(BF)
                v_prev = by_head(vp_ref[tail, :])
                for b in range(CHUNK // span):
                    rows = pl.ds(r + span * b, QBLK, stride=d)
                    q0, q1 = by_head(qc_ref[rows, :])
                    k_cur = kc_ref[rows, :].astype(BF)
                    v_cur = by_head(vc_ref[rows, :])
                    s = _mm_nt(jnp.concatenate([q0, q1], axis=0), jnp.concatenate([k_prev, k_cur], axis=0))
                    s = s + mb_ref[di, 0]
                    if b == 0:
                        s = jnp.where(no_prev, NEG, s)
                    m = jnp.max(s, axis=-1, keepdims=True)
                    e = jnp.exp(s - m)
                    den = jnp.sum(e, axis=-1, keepdims=True)
                    eb = e.astype(BF)
                    o = _mm(jnp.concatenate([eb[0:QBLK], eb[QBLK:2 * QBLK]], axis=1),
                            jnp.concatenate([v_prev[0], v_cur[0], v_prev[1], v_cur[1]], axis=0))
                    inv = 1.0 / den
                    lse = m + jnp.log(den)
                    obs[di][rows, :] = o * jnp.where(lo_half, inv[0:QBLK], inv[QBLK:2 * QBLK])
                    lbs[di][rows, :] = jnp.where(lo_half, lse[0:QBLK], lse[QBLK:2 * QBLK])
                    k_prev, v_prev = k_cur, v_cur
        for c0 in range(0, CHUNK, 256):
            rs = slice(c0, c0 + 256)
            l0, l1, l2 = lb0[rs, :], lb1[rs, :], lb2[rs, :]
            mx = jnp.maximum(jnp.maximum(l0, l1), l2)
            w0, w1, w2 = jnp.exp(l0 - mx), jnp.exp(l1 - mx), jnp.exp(l2 - mx)
            tot = w0 + w1 + w2
            o_ref[rs, :] = (ob0[rs, :] * w0 + ob1[rs, :] * w1 + ob2[rs, :] * w2) / tot
            l_ref[rs, :] = mx + jnp.log(tot)

    def cur(col):
        return pl.BlockSpec((CHUNK, 128), lambda hp, n: (n, col + hp))

    def prv(col):
        return pl.BlockSpec((CHUNK, 128), lambda hp, n: (jnp.maximum(n - 1, 0), col + hp))

    out = pl.BlockSpec((CHUNK, 128), lambda hp, n: (n, hp))
    any_spec = pl.BlockSpec(memory_space=pl.ANY)
    sems = [pltpu.SemaphoreType.DMA((6 * nl,)), pltpu.SemaphoreType.DMA((6 * nl,)), pltpu.SemaphoreType.DMA((nl,))]
    res = pl.pallas_call(
        body, name="attn_fwd", grid=(N_HEADS // 2, nc),
        in_specs=[cur(0), prv(4), cur(4), prv(8), cur(8),
                  pl.BlockSpec((3, 1, 2 * QBLK, 2 * QBLK), lambda hp, n: (0, hp, 0, 0))] + [any_spec] * nl,
        out_specs=[out, out] + [any_spec] * nl,
        out_shape=[jax.ShapeDtypeStruct((t, ATTN_W), F32)] * 2
        + [jax.ShapeDtypeStruct((N_CHIPS,) + w.shape, w.dtype) for w in late],
        scratch_shapes=[pltpu.VMEM((CHUNK, 128), F32)] * 6 + (sems if nl else []),
        compiler_params=_params(48, 2),
    )(qkn, qkn, qkn, qkv, qkv, mb, *late)
    return res[0], res[1], list(res[2:])


def _attn_bwd(qkn, qkv, o, lse, do, mb, riders=()):
    t = qkn.shape[0]
    nc = t // CHUNK
    nr = len(riders)

    def body(*refs):
        (qc_ref, qn_ref, kp_ref, kc_ref, vp_ref, vc_ref, oc_ref, on_ref, lc_ref, ln_ref, dc_ref, dn_ref,
         mb_ref) = refs[0:13]
        dq_ref, dk_ref, dv_ref = refs[13 + nr:16 + nr]
        if nr:
            step = pl.program_id(0) * nc + pl.program_id(1)
            _ride_scatter(step == 0, step == (N_HEADS // 2) * nc - 1, riders, refs[13:13 + nr],
                          refs[16 + nr:16 + 2 * nr], *refs[16 + 2 * nr:18 + 2 * nr])
        first = pl.program_id(1) == 0
        last = pl.program_id(1) == nc - 1
        lane = lax.broadcasted_iota(jnp.int32, (QBLK, 128), 1)
        lo_half = lane < HEAD_DIM
        kj = lax.broadcasted_iota(jnp.int32, (2 * QBLK, 2 * QBLK), 1)
        no_prev = first & (kj < QBLK)

        def by_head(a):
            return jnp.where(lo_half, a, 0.0).astype(BF), jnp.where(lo_half, 0.0, a).astype(BF)

        def query_side(q_ref, d_ref, o_ref_, l_ref_, rows):
            dvals = d_ref[rows, :]
            dd = dvals * o_ref_[rows, :]
            lv = l_ref_[rows, :]
            d0 = jnp.sum(jnp.where(lo_half, dd, 0.0), axis=-1, keepdims=True)
            d1 = jnp.sum(jnp.where(lo_half, 0.0, dd), axis=-1, keepdims=True)
            l0 = jnp.max(jnp.where(lo_half, lv, NEG), axis=-1, keepdims=True)
            l1 = jnp.max(jnp.where(lo_half, NEG, lv), axis=-1, keepdims=True)
            return (jnp.concatenate(by_head(q_ref[rows, :]), axis=0), jnp.concatenate(by_head(dvals), axis=0),
                    jnp.concatenate([l0, l1], axis=0), jnp.concatenate([d0, d1], axis=0))

        def tile(qs, dos, lcol, dcol, keys, vals, bias, dead):
            s = _mm_nt(qs, keys) + bias
            if dead is not None:
                s = jnp.where(dead, NEG, s)
            p = jnp.exp(s - lcol)
            ds = p * (_mm_nt(dos, vals) - dcol)
            return p.astype(BF), ds.astype(BF)

        def put(ref, di, rows, val):
            if di == 0:
                ref[rows, :] = val
            else:
                ref[rows, :] = ref[rows, :] + val

        for di, d in enumerate(DILATIONS):
            span = d * QBLK
            nbk = CHUNK // span
            for r in range(d):
                tail = pl.ds(CHUNK - span + r, QBLK, stride=d)
                k_prev = kp_ref[tail, :]
                kb_prev, km_prev = k_prev.astype(BF), by_head(k_prev)
                vb_prev = vp_ref[tail, :].astype(BF)
                rows_prev, dk_part, dv_part = None, None, None
                for b in range(nbk):
                    rows = pl.ds(r + span * b, QBLK, stride=d)
                    qs, dos, lcol, dcol = query_side(qc_ref, dc_ref, oc_ref, lc_ref, rows)
                    k_cur = kc_ref[rows, :]
                    kb_cur, km_cur = k_cur.astype(BF), by_head(k_cur)
                    vb_cur = vc_ref[rows, :].astype(BF)
                    p, ds = tile(qs, dos, lcol, dcol, jnp.concatenate([kb_prev, kb_cur], axis=0),
                                 jnp.concatenate([vb_prev, vb_cur], axis=0), mb_ref[di, 0],
                                 no_prev if b == 0 else None)
                    put(dq_ref, di, rows,
                        _mm(jnp.concatenate([ds[0:QBLK], ds[QBLK:2 * QBLK]], axis=1),
                            jnp.concatenate([km_prev[0], km_cur[0], km_prev[1], km_cur[1]], axis=0)))
                    dk2 = _mm_tn(ds, qs)
                    dv2 = _mm_tn(p, dos)
                    if b > 0:
                        put(dk_ref, di, rows_prev, dk_part + dk2[0:QBLK])
                        put(dv_ref, di, rows_prev, dv_part + dv2[0:QBLK])
                    rows_prev, dk_part, dv_part = rows, dk2[QBLK:2 * QBLK], dv2[QBLK:2 * QBLK]
                    kb_prev, km_prev, vb_prev = kb_cur, km_cur, vb_cur
                qs, dos, lcol, dcol = query_side(qn_ref, dn_ref, on_ref, ln_ref, pl.ds(r, QBLK, stride=d))
                p, ds = tile(qs, dos, lcol, dcol, kb_prev, vb_prev, mb_ref[di, 0, :, 0:QBLK], last)
                put(dk_ref, di, rows_prev, dk_part + _mm_tn(ds, qs))
                put(dv_ref, di, rows_prev, dv_part + _mm_tn(p, dos))

    def at(shift, col):
        return pl.BlockSpec((CHUNK, 128), lambda hp, n: (jnp.clip(n + shift, 0, nc - 1), col + hp))

    out = pl.BlockSpec((CHUNK, 128), lambda hp, n: (n, hp))
    r_in, r_out, r_sems = _rider_specs(riders)
    res = pl.pallas_call(
        body, name="attn_bwd", grid=(N_HEADS // 2, nc),
        in_specs=[at(0, 0), at(1, 0), at(-1, 4), at(0, 4), at(-1, 8), at(0, 8),
                  at(0, 0), at(1, 0), at(0, 0), at(1, 0), at(0, 0), at(1, 0),
                  pl.BlockSpec((3, 1, 2 * QBLK, 2 * QBLK), lambda hp, n: (0, hp, 0, 0))] + r_in,
        out_specs=[out, out, out] + r_in,
        out_shape=[jax.ShapeDtypeStruct((t, ATTN_W), F32)] * 3 + r_out,
        scratch_shapes=r_sems,
        compiler_params=_params(56, 2),
    )(qkn, qkn, qkn, qkn, qkv, qkv, o, o, lse, lse, do, do, mb, *[g for g, _ in riders])
    return res[0], res[1], res[2], list(res[3:])


def _fwd_ffn(x, ycn, ya, wout, wg4, wu4, g_oa, g_ffn, fcw, fcb, tm, late=()):
    t = x.shape[0]
    nt = t // tm
    nl = len(late)

    def body(*refs):
        x_ref, ycn_ref, ya_ref, wout_ref, wg_ref, wu_ref, goa_ref, gffn_ref, fcw_ref, fcb_ref = refs[0:10]
        x1_ref, gp_ref, up_ref, gate_ref, act_ref, ycat_ref, h2_ref = refs[10 + nl:17 + nl]
        cbuf = refs[17 + 2 * nl]
        if nl:
            gather = _Gather(refs[10:10 + nl], refs[17 + nl:17 + 2 * nl], *refs[18 + 2 * nl:21 + 2 * nl])
            pl.when(pl.program_id(0) == 0)(gather.start)
            pl.when(pl.program_id(0) == nt // 2)(gather.forward)
            pl.when(pl.program_id(0) == nt - 1)(gather.finish)

        @pl.when(pl.program_id(0) == 0)
        def _():
            cbuf[0:8, :] = jnp.zeros((8, D_FF), F32)

        yat = ya_ref[...]
        yan = ((yat * _rstd(yat)) * goa_ref[...]).astype(BF)
        ycn = ycn_ref[...]
        ycat_ref[:, 0:CONV_W] = ycn
        ycat_ref[:, CONV_W:D_MODEL] = yan
        x1 = x_ref[...] + _mm(ycn, wout_ref[0:CONV_W, :]) + _mm(yan, wout_ref[CONV_W:D_MODEL, :])
        x1_ref[...] = x1
        h2 = ((x1 * _rstd(x1)) * gffn_ref[...]).astype(BF)
        h2_ref[...] = h2
        for lo, hi in FF_SLABS:
            gps = _mm_nt(h2, wg_ref[lo:hi, :])
            ups = _mm_nt(h2, wu_ref[lo:hi, :])
            gp_ref[:, lo:hi] = gps.astype(BF)
            up_ref[:, lo:hi] = ups.astype(BF)
            cbuf[8:8 + tm, lo:hi] = gps
            cb = cbuf[:, lo:hi]
            gate = (fcw_ref[0:1, lo:hi] * _shift_down(cb, 2, tm) + fcw_ref[1:2, lo:hi] * _shift_down(cb, 1, tm)
                    + fcw_ref[2:3, lo:hi] * gps + fcb_ref[:, lo:hi])
            gate_ref[:, lo:hi] = gate.astype(BF)
            act_ref[:, lo:hi] = ((gate * jax.nn.sigmoid(gate)) * ups).astype(BF)
        cbuf[0:8, :] = cbuf[tm:tm + 8, :]

    any_spec = pl.BlockSpec(memory_space=pl.ANY)
    sems = [pltpu.SemaphoreType.DMA((6 * nl,)), pltpu.SemaphoreType.DMA((6 * nl,)), pltpu.SemaphoreType.DMA((nl,))]
    res = pl.pallas_call(
        body, name="fwd_ffn", grid=(nt,),
        in_specs=[_rows(tm, D_MODEL), _rows(tm, CONV_W), _rows(tm, ATTN_W), _const((D_MODEL, D_MODEL)),
                  _const((D_FF, D_MODEL)), _const((D_FF, D_MODEL)),
                  _const((1, ATTN_W)), _const((1, D_MODEL)), _const((3, D_FF)), _const((1, D_FF))]
        + [any_spec] * nl,
        out_specs=[_rows(tm, D_MODEL), _rows(tm, D_FF), _rows(tm, D_FF), _rows(tm, D_FF), _rows(tm, D_FF),
                   _rows(tm, D_MODEL), _rows(tm, D_MODEL)] + [any_spec] * nl,
        out_shape=[jax.ShapeDtypeStruct((t, D_MODEL), F32), jax.ShapeDtypeStruct((t, D_FF), BF),
                   jax.ShapeDtypeStruct((t, D_FF), BF), jax.ShapeDtypeStruct((t, D_FF), BF),
                   jax.ShapeDtypeStruct((t, D_FF), BF),
                   jax.ShapeDtypeStruct((t, D_MODEL), BF), jax.ShapeDtypeStruct((t, D_MODEL), BF)]
        + [jax.ShapeDtypeStruct((N_CHIPS,) + w.shape, w.dtype) for w in late],
        scratch_shapes=[pltpu.VMEM((tm + 8, D_FF), F32)] + (sems if nl else []),
        compiler_params=_params(56),
    )(x, ycn, ya, wout, wg4, wu4, g_oa, g_ffn, fcw, fcb, *late)
    return tuple(res[0:7]) + (list(res[7:]),)


def _fwd_tail(x1, act, p, target, wd4, wpg, wpp4, g_ple, tm):
    t = x1.shape[0]
    nt = t // tm

    def body(x1_ref, act_ref, p_ref, tgt_ref, wd_ref, wpg_ref, wpp_ref, g_ref,
             dx2_ref, h3_ref, ds_ref, dpp_ref, dg_ref, loss_ref, lacc):
        i = pl.program_id(0)

        @pl.when(i == 0)
        def _():
            dg_ref[...] = jnp.zeros_like(dg_ref)
            lacc[...] = jnp.zeros_like(lacc)

        x2 = x1_ref[...]
        for lo, hi in FF_SLABS:
            x2 = x2 + _mm(act_ref[:, lo:hi], wd_ref[lo:hi, :])
        r3 = _rstd(x2)
        xh = x2 * r3
        h3 = (xh * g_ref[...]).astype(BF)
        h3_ref[...] = h3
        sg = jax.nn.sigmoid(_mm(h3, wpg_ref[...]))
        pb = p_ref[...].astype(BF)
        pp = jnp.concatenate([_mm(pb, wpp_ref[s]) for s in range(N_CHIPS)], axis=1)
        err = (x2 + sg * pp) - tgt_ref[...]
        lacc[...] += _colsum(err * err)
        dx3 = err * (1.0 / D_MODEL)
        dpp_ref[...] = (dx3 * sg).astype(BF)
        dsb = ((dx3 * pp) * (sg * (1.0 - sg))).astype(BF)
        ds_ref[...] = dsb
        dh3 = _mm_nt(dsb, wpg_ref[...])
        dg_ref[...] += _colsum(dh3 * xh)
        dx2_ref[...] = dx3 + _norm_bwd(dh3, xh, r3, g_ref[...])

        @pl.when(i == nt - 1)
        def _():
            loss_ref[...] = jnp.full((1, 128), jnp.sum(lacc[...]) * (0.5 / D_MODEL), F32)

    return pl.pallas_call(
        body, name="fwd_tail", grid=(nt,),
        in_specs=[_rows(tm, D_MODEL), _rows(tm, D_FF), _rows(tm, PLE_DIM), _rows(tm, D_MODEL),
                  _const((D_FF, D_MODEL)), _const((D_MODEL, D_MODEL)),
                  _const((N_CHIPS, PLE_DIM, PLE_DIM)), _const((1, D_MODEL))],
        out_specs=[_rows(tm, D_MODEL), _rows(tm, D_MODEL), _rows(tm, D_MODEL), _rows(tm, D_MODEL),
                   pl.BlockSpec((1, D_MODEL), lambda i: (0, 0)), pl.BlockSpec((1, 128), lambda i: (0, 0))],
        out_shape=[jax.ShapeDtypeStruct((t, D_MODEL), F32), jax.ShapeDtypeStruct((t, D_MODEL), BF),
                   jax.ShapeDtypeStruct((t, D_MODEL), BF), jax.ShapeDtypeStruct((t, D_MODEL), BF),
                   jax.ShapeDtypeStruct((1, D_MODEL), F32), jax.ShapeDtypeStruct((1, 128), F32)],
        scratch_shapes=[pltpu.VMEM((1, D_MODEL), F32)],
        compiler_params=_params(48),
    )(x1, act, p, target, wd4, wpg, wpp4, g_ple)


def _bwd_ffn_a(dx2, gate, gp, up, wd4, fcw, tm, riders=()):
    t = dx2.shape[0]
    nt = t // tm
    nr = len(riders)

    def body(*refs):
        dx2_ref, gate_ref, gp_ref, up_ref, wd_ref, fcw_ref = refs[0:6]
        dgp_ref, dup_ref, dfcw_ref, dfcb_ref = refs[6 + nr:10 + nr]
        dbuf = refs[10 + 2 * nr]
        i = pl.program_id(0)
        if nr:
            _ride_scatter(i == 0, i == nt - 1, riders, refs[6:6 + nr], refs[10 + nr:10 + 2 * nr],
                          *refs[11 + 2 * nr:13 + 2 * nr])

        @pl.when(i == 0)
        def _():
            dbuf[tm:tm + 8, :] = jnp.zeros((8, D_FF), F32)
            dfcw_ref[...] = jnp.zeros_like(dfcw_ref)
            dfcb_ref[...] = jnp.zeros_like(dfcb_ref)

        dx2b = dx2_ref[...].astype(BF)
        for lo, hi in FF_SLABS:
            gate = gate_ref[:, lo:hi].astype(F32)
            gps = gp_ref[:, lo:hi].astype(F32)
            w0, w1, w2 = fcw_ref[0:1, lo:hi], fcw_ref[1:2, lo:hi], fcw_ref[2:3, lo:hi]
            sg = jax.nn.sigmoid(gate)
            dact = _mm_nt(dx2b, wd_ref[lo:hi, :])
            dup_ref[:, lo:hi] = (dact * (gate * sg)).astype(BF)
            dgate = (dact * up_ref[:, lo:hi].astype(F32)) * (sg * (1.0 + gate * (1.0 - sg)))
            dbuf[0:tm, lo:hi] = dgate
            db = dbuf[:, lo:hi]
            d1 = _shift_up(db, 1, tm)
            d2 = _shift_up(db, 2, tm)
            dfcb_ref[:, lo:hi] += _colsum(dgate)
            dfcw_ref[0:1, lo:hi] += _colsum(d2 * gps)
            dfcw_ref[1:2, lo:hi] += _colsum(d1 * gps)
            dfcw_ref[2:3, lo:hi] += _colsum(dgate * gps)
            dgp_ref[:, lo:hi] = (w2 * dgate + w1 * d1 + w0 * d2).astype(BF)
        dbuf[tm:tm + 8, :] = dbuf[0:8, :]

    r_in, r_out, r_sems = _rider_specs(riders)
    res = pl.pallas_call(
        body, name="bwd_ffn_a", grid=(nt,),
        in_specs=[_rows(tm, D_MODEL, nt), _rows(tm, D_FF, nt), _rows(tm, D_FF, nt), _rows(tm, D_FF, nt),
                  _const((D_FF, D_MODEL)), _const((3, D_FF))] + r_in,
        out_specs=[_rows(tm, D_FF, nt), _rows(tm, D_FF, nt),
                   pl.BlockSpec((3, D_FF), lambda i: (0, 0)), pl.BlockSpec((1, D_FF), lambda i: (0, 0))] + r_in,
        out_shape=[jax.ShapeDtypeStruct((t, D_FF), BF), jax.ShapeDtypeStruct((t, D_FF), BF),
                   jax.ShapeDtypeStruct((3, D_FF), F32), jax.ShapeDtypeStruct((1, D_FF), F32)] + r_out,
        scratch_shapes=[pltpu.VMEM((tm + 8, D_FF), F32)] + r_sems,
        compiler_params=_params(56),
    )(dx2, gate, gp, up, wd4, fcw, *[g for g, _ in riders])
    return res[0], res[1], res[2], res[3], list(res[4:])


def _bwd_ffn_b(dgp, dup, dx2, x1, ya, wg4, wu4, wout, g_ffn, g_oa, tm):
    t = dx2.shape[0]
    nt = t // tm

    def body(dgp_ref, dup_ref, dx2_ref, x1_ref, ya_ref, wg_ref, wu_ref, wout_ref, gffn_ref, goa_ref,
             dx1_ref, dycn_ref, dya_ref, dgffn_ref, dgoa_ref):
        @pl.when(pl.program_id(0) == 0)
        def _():
            dgffn_ref[...] = jnp.zeros_like(dgffn_ref)
            dgoa_ref[...] = jnp.zeros_like(dgoa_ref)

        dh2 = jnp.zeros((tm, D_MODEL), F32)
        for lo, hi in FF_SLABS:
            dh2 = dh2 + _mm(dgp_ref[:, lo:hi], wg_ref[lo:hi, :]) + _mm(dup_ref[:, lo:hi], wu_ref[lo:hi, :])
        x1 = x1_ref[...]
        r2 = _rstd(x1)
        xh = x1 * r2
        dgffn_ref[...] += _colsum(dh2 * xh)
        dx1 = dx2_ref[...] + _norm_bwd(dh2, xh, r2, gffn_ref[...])
        dx1_ref[...] = dx1
        dy = _mm_nt(dx1.astype(BF), wout_ref[...])
        dycn_ref[...] = dy[:, 0:CONV_W]
        dyan = dy[:, CONV_W:D_MODEL]
        yat = ya_ref[...]
        ra = _rstd(yat)
        yah = yat * ra
        dgoa_ref[...] += _colsum(dyan * yah)
        dya_ref[...] = _norm_bwd(dyan, yah, ra, goa_ref[...])

    return pl.pallas_call(
        body, name="bwd_ffn_b", grid=(nt,),
        in_specs=[_rows(tm, D_FF), _rows(tm, D_FF), _rows(tm, D_MODEL), _rows(tm, D_MODEL),
                  _rows(tm, ATTN_W), _const((D_FF, D_MODEL)), _const((D_FF, D_MODEL)),
                  _const((D_MODEL, D_MODEL)), _const((1, D_MODEL)), _const((1, ATTN_W))],
        out_specs=[_rows(tm, D_MODEL), _rows(tm, CONV_W), _rows(tm, ATTN_W),
                   pl.BlockSpec((1, D_MODEL), lambda i: (0, 0)), pl.BlockSpec((1, ATTN_W), lambda i: (0, 0))],
        out_shape=[jax.ShapeDtypeStruct((t, D_MODEL), F32), jax.ShapeDtypeStruct((t, CONV_W), F32),
                   jax.ShapeDtypeStruct((t, ATTN_W), F32),
                   jax.ShapeDtypeStruct((1, D_MODEL), F32), jax.ShapeDtypeStruct((1, ATTN_W), F32)],
        compiler_params=_params(48),
    )(dgp, dup, dx2, x1, ya, wg4, wu4, wout, g_ffn, g_oa)


def _bwd_mix(x, zbcx, qkv, dycn, dq, dk, dv, conv_w, conv_b, g_oc, g_mix, gm, gq8, gk8, tm):
    t = x.shape[0]
    nt = t // tm

    def body(x_ref, z_ref, zh_ref, qkv_ref, dycn_ref, dq_ref, dk_ref, dv_ref, cw_ref, cb_ref,
             goc_ref, g_ref, gm_ref, gq_ref, gk_ref,
             dz_ref, dcw_ref, dcb_ref, dgoc_ref, dgq_ref, dgk_ref, gw32_ref, gw16_ref,
             ubuf, dbuf, wacc, wstage, osem):
        i = pl.program_id(0)

        @pl.when(i == 0)
        def _():
            wacc[...] = jnp.zeros_like(wacc)
            dbuf[tm:tm + 8, :] = jnp.zeros((8, CONV_W), F32)
            dcw_ref[...] = jnp.zeros_like(dcw_ref)
            dcb_ref[...] = jnp.zeros_like(dcb_ref)
            dgoc_ref[...] = jnp.zeros_like(dgoc_ref)
            dgq_ref[...] = jnp.zeros_like(dgq_ref)
            dgk_ref[...] = jnp.zeros_like(dgk_ref)

        not_first_tile = i < nt - 1
        zb = z_ref[:, 0:512]
        zc = z_ref[:, 512:1024]
        zx = z_ref[:, 1024:1536]
        u = zc * zx
        ubuf[0:8, :] = jnp.where(not_first_tile, zh_ref[:, 512:1024] * zh_ref[:, 1024:1536], 0.0)
        ubuf[8:8 + tm, :] = u
        ub = ubuf[...]
        u1 = _shift_down(ub, 1, tm)
        u2 = _shift_down(ub, 2, tm)
        w0, w1, w2 = cw_ref[0:1, :], cw_ref[1:2, :], cw_ref[2:3, :]
        cv = w0 * u2 + w1 * u1 + w2 * u + cb_ref[...]
        yc = zb * cv
        rc = _rstd(yc)
        ych = yc * rc
        dycn = dycn_ref[...]
        dgoc_ref[...] += _colsum(dycn * ych)
        dyc = _norm_bwd(dycn, ych, rc, goc_ref[...])
        dcv = dyc * zb
        dcb_ref[...] += _colsum(dcv)
        dcw_ref[0:1, :] += _colsum(dcv * u2)
        dcw_ref[1:2, :] += _colsum(dcv * u1)
        dcw_ref[2:3, :] += _colsum(dcv * u)
        dbuf[0:tm, :] = dcv
        db = dbuf[...]
        du = w2 * dcv + w1 * _shift_up(db, 1, tm) + w0 * _shift_up(db, 2, tm)
        dbuf[tm:tm + 8, :] = dbuf[0:8, :]
        dz_ref[:, 0:512] = (dyc * cv).astype(BF)
        dz_ref[:, 512:1024] = (du * zx).astype(BF)
        dz_ref[:, 1024:1536] = (du * zc).astype(BF)
        for z0, d_ref, gg_ref, acc_ref, sc in ((0, dq_ref, gq_ref, dgq_ref, HEAD_DIM ** -0.5),
                                               (512, dk_ref, gk_ref, dgk_ref, 1.0)):
            z = qkv_ref[:, z0:z0 + 512]
            rr = lax.rsqrt(_head_mean(z * z, gm_ref) + EPS)
            zh = z * rr
            dn = d_ref[...] * sc
            acc_ref[...] += _colsum(dn * zh)
            dzh = dn * gg_ref[...]
            dz_ref[:, 1536 + z0:1536 + z0 + 512] = (rr * (dzh - zh * _head_mean(dzh * zh, gm_ref))).astype(BF)
        dz_ref[:, 2560:3072] = dv_ref[...].astype(BF)
        xt = x_ref[...]
        h1 = ((xt * _rstd(xt)) * g_ref[...]).astype(BF)
        for s in range(N_CHIPS):
            cols = slice(s * IN_SLAB, (s + 1) * IN_SLAB)
            wacc[:, cols] += _mm_tn(h1, dz_ref[:, cols])

        @pl.when(i == nt - 1)
        def _():
            wstage[...] = wacc[...].astype(BF)
            out32 = pltpu.make_async_copy(wacc, gw32_ref, osem.at[0])
            out16 = pltpu.make_async_copy(wstage, gw16_ref, osem.at[1])
            out32.start()
            out16.start()
            out32.wait()
            out16.wait()

    def acc(width, rows=1):
        return pl.BlockSpec((rows, width), lambda i: (0, 0))

    return pl.pallas_call(
        body, name="bwd_mix", grid=(nt,),
        in_specs=[_rows(tm, D_MODEL, nt), _rows(tm, 1536, nt), _halo(tm, 1536, nt),
                  _rows(tm, 1536, nt), _rows(tm, CONV_W, nt), _rows(tm, ATTN_W, nt), _rows(tm, ATTN_W, nt),
                  _rows(tm, ATTN_W, nt),
                  _const((3, CONV_W)), _const((1, CONV_W)), _const((1, CONV_W)), _const((1, D_MODEL)),
                  _const((ATTN_W, ATTN_W)), _const((1, ATTN_W)), _const((1, ATTN_W))],
        out_specs=[_rows(tm, 3072, nt),
                   acc(CONV_W, 3), acc(CONV_W), acc(CONV_W), acc(ATTN_W), acc(ATTN_W),
                   pl.BlockSpec(memory_space=pl.ANY), pl.BlockSpec(memory_space=pl.ANY)],
        out_shape=[jax.ShapeDtypeStruct((t, 3072), BF), jax.ShapeDtypeStruct((3, CONV_W), F32),
                   jax.ShapeDtypeStruct((1, CONV_W), F32), jax.ShapeDtypeStruct((1, CONV_W), F32),
                   jax.ShapeDtypeStruct((1, ATTN_W), F32), jax.ShapeDtypeStruct((1, ATTN_W), F32),
                   jax.ShapeDtypeStruct((D_MODEL, 3072), F32), jax.ShapeDtypeStruct((D_MODEL, 3072), BF)],
        scratch_shapes=[pltpu.VMEM((tm + 8, CONV_W), F32), pltpu.VMEM((tm + 8, CONV_W), F32),
                        pltpu.VMEM((D_MODEL, 3072), F32), pltpu.VMEM((D_MODEL, 3072), BF),
                        pltpu.SemaphoreType.DMA((2,))],
        compiler_params=_params(56),
    )(x, zbcx, zbcx, qkv, dycn, dq, dk, dv, conv_w, conv_b, g_oc, g_mix, gm, gq8, gk8)


def _bwd_in(x, dx1, dz, win4, g_mix, tm, riders=()):
    t = x.shape[0]
    nt = t // tm
    nr = len(riders)

    def body(*refs):
        x_ref, dx1_ref, dz_ref, w_ref, g_ref = refs[0:5]
        gx_ref, dg_ref = refs[5 + nr:7 + nr]
        i = pl.program_id(0)
        if nr:
            _ride_scatter(i == 0, i == nt - 1, riders, refs[5:5 + nr], refs[7 + nr:7 + 2 * nr],
                          *refs[7 + 2 * nr:9 + 2 * nr])

        @pl.when(i == 0)
        def _():
            dg_ref[...] = jnp.zeros_like(dg_ref)

        dh1 = jnp.zeros((tm, D_MODEL), F32)
        for s in range(N_CHIPS):
            dh1 = dh1 + _mm_nt(dz_ref[:, s * IN_SLAB:(s + 1) * IN_SLAB], w_ref[s])
        xt = x_ref[...]
        r1 = _rstd(xt)
        xh = xt * r1
        dg_ref[...] += _colsum(dh1 * xh)
        gx_ref[...] = dx1_ref[...] + _norm_bwd(dh1, xh, r1, g_ref[...])

    r_in, r_out, r_sems = _rider_specs(riders)
    res = pl.pallas_call(
        body, name="bwd_in", grid=(nt,),
        in_specs=[_rows(tm, D_MODEL), _rows(tm, D_MODEL), _rows(tm, 3072), _const((N_CHIPS, D_MODEL, IN_SLAB)),
                  _const((1, D_MODEL))] + r_in,
        out_specs=[_rows(tm, D_MODEL), pl.BlockSpec((1, D_MODEL), lambda i: (0, 0))] + r_in,
        out_shape=[jax.ShapeDtypeStruct((t, D_MODEL), F32), jax.ShapeDtypeStruct((1, D_MODEL), F32)] + r_out,
        scratch_shapes=r_sems,
        compiler_params=_params(48),
    )(x, dx1, dz, win4, g_mix, *[g for g, _ in riders])
    return res[0], res[1], list(res[2:])


def _wgrad(a, b, tn, tt, name):
    t, k = a.shape
    n = b.shape[1]
    nt = t // tt

    def body(a_ref, b_ref, o_ref, ob_ref):
        @pl.when(pl.program_id(1) == 0)
        def _():
            o_ref[...] = jnp.zeros_like(o_ref)

        o_ref[...] += _mm_tn(a_ref[...].astype(BF), b_ref[...].astype(BF))

        @pl.when(pl.program_id(1) == nt - 1)
        def _():
            ob_ref[...] = o_ref[...].astype(BF)

    spec = pl.BlockSpec((k, tn), lambda j, i: (0, j))
    return pl.pallas_call(
        body, name=name, grid=(n // tn, nt),
        in_specs=[pl.BlockSpec((tt, k), lambda j, i: (i, 0)), pl.BlockSpec((tt, tn), lambda j, i: (i, j))],
        out_specs=[spec, spec],
        out_shape=[jax.ShapeDtypeStruct((k, n), F32), jax.ShapeDtypeStruct((k, n), BF)],
        compiler_params=_params(58, 2),
    )(a, b)


def _gather_weights(shards, pack):
    nw = len(shards)

    def body(*refs):
        ins = refs[:nw]
        pack_ref = refs[nw]
        outs = refs[nw + 1:2 * nw + 1]
        pack_out = refs[2 * nw + 1]
        send_sems, recv_sems, local_sems = refs[2 * nw + 2:]
        x, y, c = _place()
        me = 2 * x + y
        local, remote = [], []

        def sem(w, j):
            return w * 6 + j

        def push(src, dst, w, j, to):
            return pltpu.make_async_remote_copy(src_ref=src, dst_ref=dst, send_sem=send_sems.at[sem(w, j)],
                                                recv_sem=recv_sems.at[sem(w, j)], device_id=to, device_id_type=MESH)

        def half_rows(w, h):
            half = ins[w].shape[0] // 2
            return pl.ds(pl.multiple_of(h * half, 16), half)

        for w in range(nw):
            local.append(pltpu.make_async_copy(ins[w], outs[w].at[me], local_sems.at[w]))
            for k in (1, 2, 3):
                px, py = _chip_peer(x, y, k)
                mine = half_rows(w, c)
                remote.append(push(ins[w].at[mine], outs[w].at[me, mine], w, k - 1, (px, py, c)))
        local.append(pltpu.make_async_copy(pack_ref, pack_out.at[me], local_sems.at[nw]))
        for k in (1, 2, 3):
            px, py = _chip_peer(x, y, k)
            remote.append(push(pack_ref, pack_out.at[me], nw, k - 1, (px, py, c)))
        for cp in local + remote:
            cp.start()
        for w in range(nw):
            for k in (1, 2, 3):
                landed = outs[w].at[me ^ k, half_rows(w, c)]
                push(landed, landed, w, k - 1, (x, y, c)).wait_recv()
                fw = push(landed, landed, w, 2 + k, (x, y, 1 - c))
                fw.start()
                remote.append(fw)
        for k in (1, 2, 3):
            landed = pack_out.at[me ^ k]
            push(landed, landed, nw, k - 1, (x, y, c)).wait_recv()
        for w in range(nw):
            for k in (1, 2, 3):
                landed = outs[w].at[me ^ k, half_rows(w, 1 - c)]
                push(landed, landed, w, 2 + k, (x, y, c)).wait_recv()
        for cp in remote:
            cp.wait_send()
        for cp in local:
            cp.wait()

    any_spec = pl.BlockSpec(memory_space=pl.ANY)
    out_shape = [jax.ShapeDtypeStruct((N_CHIPS,) + s.shape, s.dtype) for s in shards]
    out_shape.append(jax.ShapeDtypeStruct((N_CHIPS,) + pack.shape, pack.dtype))
    return pl.pallas_call(
        body, name="gather_weights",
        in_specs=[any_spec] * (nw + 1), out_specs=[any_spec] * (nw + 1), out_shape=out_shape,
        scratch_shapes=[pltpu.SemaphoreType.DMA(((nw + 1) * 6,)), pltpu.SemaphoreType.DMA(((nw + 1) * 6,)),
                        pltpu.SemaphoreType.DMA((nw + 1,))],
    )(*shards, pack)


def _adamw(w, g, m, v):
    m = ADAM_B1 * m + (1.0 - ADAM_B1) * g
    v = ADAM_B2 * v + (1.0 - ADAM_B2) * (g * g)
    m_hat = m / (1.0 - ADAM_B1 ** ADAM_STEP)
    v_hat = v / (1.0 - ADAM_B2 ** ADAM_STEP)
    delta = -ADAM_LR * (m_hat / (jnp.sqrt(v_hat) + ADAM_EPS) + ADAM_WD * w)
    return delta, m, v


def _finish_reduce(grad, slots, col_sharded, name):
    r, cw = _piece_shape(grad.shape, col_sharded)
    chunk = 32
    assert r % chunk == 0

    def body(g_hbm, slots_ref, full, own, lsem, c_send, c_recv):
        x, y, c = _place()
        cp = pltpu.make_async_copy(g_hbm.at[_piece_window(col_sharded, r, cw, 2 * x + y, c)], own, lsem)
        cp.start()
        cp.wait()
        mine = pl.multiple_of(c * r, 8)

        def add(j, carry):
            rows = pl.ds(pl.multiple_of(j * chunk, 8), chunk)
            tot = own[rows, :]
            for k in range(7):
                tot = tot + slots_ref[k, rows, :].astype(F32)
            full[pl.ds(mine + pl.multiple_of(j * chunk, 8), chunk), :] = tot
            return carry

        lax.fori_loop(0, r // chunk, add, 0)
        half = full.at[pl.ds(mine, r), :]
        swap = pltpu.make_async_remote_copy(src_ref=half, dst_ref=half, send_sem=c_send, recv_sem=c_recv,
                                            device_id=(x, y, 1 - c), device_id_type=MESH)
        swap.start()
        swap.wait()

    vmem = pl.BlockSpec(memory_space=pltpu.VMEM)
    return pl.pallas_call(
        body, name=name, in_specs=[pl.BlockSpec(memory_space=pltpu.HBM), vmem], out_specs=vmem,
        out_shape=jax.ShapeDtypeStruct((2 * r, cw), F32),
        scratch_shapes=[pltpu.VMEM((r, cw), F32), pltpu.SemaphoreType.DMA, pltpu.SemaphoreType.DMA,
                        pltpu.SemaphoreType.DMA],
        compiler_params=pltpu.CompilerParams(vmem_limit_bytes=32 * MIB),
    )(grad, slots)


def _adamw_big(g, w, m, v, name):
    vr, vc = w.shape
    assert g.shape == w.shape
    rows = 64

    def body(g_ref, w_ref, m_ref, v_ref, go_ref, do_ref, mo_ref, vo_ref):
        gg = g_ref[...]
        delta, mn, vn = _adamw(w_ref[...], gg, m_ref[...], v_ref[...])
        go_ref[...] = gg
        do_ref[...] = delta
        mo_ref[...] = mn
        vo_ref[...] = vn

    blk = pl.BlockSpec((rows, vc), lambda i: (i, 0))
    shard = jax.ShapeDtypeStruct((vr, vc), F32)
    return pl.pallas_call(
        body, name=name, grid=(vr // rows,),
        in_specs=[blk, blk, blk, blk], out_specs=[blk] * 4,
        out_shape=[shard] * 4, compiler_params=_params(32),
    )(g, w, m, v)


def _allreduce_small(pack):
    rows = pack.shape[0]

    def body(p_ref, o_ref, slots, send_sems, recv_sems):
        x, y, c = _place()
        me = 4 * x + 2 * y + c
        slots[me] = p_ref[...]
        sends = []
        for k in range(1, 8):
            cp = pltpu.make_async_remote_copy(
                src_ref=p_ref, dst_ref=slots.at[me], send_sem=send_sems.at[k - 1], recv_sem=recv_sems.at[k - 1],
                device_id=(x ^ (k >> 2), y ^ ((k >> 1) & 1), c ^ (k & 1)), device_id_type=MESH)
            cp.start()
            sends.append(cp)
        for cp in sends:
            cp.wait()
        tot = slots[0]
        for j in range(1, 8):
            tot = tot + slots[j]
        o_ref[...] = tot

    vmem = pl.BlockSpec(memory_space=pltpu.VMEM)
    return pl.pallas_call(
        body, name="allreduce_small", in_specs=[vmem], out_specs=vmem,
        out_shape=jax.ShapeDtypeStruct(pack.shape, F32),
        scratch_shapes=[pltpu.VMEM((8, rows, D_MODEL), F32), pltpu.SemaphoreType.DMA((7,)),
                        pltpu.SemaphoreType.DMA((7,))],
    )(pack)


def _adamw_small(ws, gs, ms, vs):
    n = len(ws)

    def body(*refs):
        w_refs, g_refs, m_refs, v_refs = refs[0:n], refs[n:2 * n], refs[2 * n:3 * n], refs[3 * n:4 * n]
        d_refs, mo_refs, vo_refs = refs[4 * n:5 * n], refs[5 * n:6 * n], refs[6 * n:7 * n]
        for j in range(n):
            delta, mn, vn = _adamw(w_refs[j][...], g_refs[j][...], m_refs[j][...], v_refs[j][...])
            d_refs[j][...] = delta
            mo_refs[j][...] = mn
            vo_refs[j][...] = vn

    vmem = pl.BlockSpec(memory_space=pltpu.VMEM)
    shapes = [jax.ShapeDtypeStruct(w.shape, F32) for w in ws]
    outs = pl.pallas_call(
        body, name="adamw_small", in_specs=[vmem] * (4 * n), out_specs=[vmem] * (3 * n), out_shape=shapes * 3,
    )(*ws, *gs, *ms, *vs)
    return outs[0:n], outs[n:2 * n], outs[2 * n:3 * n]


def _local_step(x, p, target, wts, late=None):
    (win4, wout, wg4, wu4, wd4, wpg, wpp4, conv_w, fcw, g_mix, conv_b, gq, gk, g_oc, g_oa, g_ffn, fcb, g_ple) = wts
    comm = late is not None
    gm = jnp.kron(jnp.eye(N_HEADS, dtype=F32), jnp.full((HEAD_DIM, HEAD_DIM), 1.0 / HEAD_DIM, F32)).astype(BF)
    gq8, gk8 = jnp.tile(gq, (1, N_HEADS)), jnp.tile(gk, (1, N_HEADS))
    mb = _mask_table()
    zbcx, qkv, ycn, qkn = _fwd_mix(x, g_mix, win4, conv_w, conv_b, g_oc, gm, gq8, gk8, 512)
    ya, lse, gathered = _attn_fwd(qkn, qkv, mb, late[0:3] if comm else ())
    if comm:
        wout, wg4, wu4 = (g.reshape(-1, D_MODEL) for g in gathered)
    x1, gp, up, gate, act, ycat, h2, gathered = _fwd_ffn(x, ycn, ya, wout, wg4, wu4, g_oa, g_ffn, fcw, fcb, 256,
                                                    late[3:6] if comm else ())
    if comm:
        wd4, wpg, wpp4 = gathered
        wd4, wpg = wd4.reshape(D_FF, D_MODEL), wpg.reshape(D_MODEL, D_MODEL)
    dx2, h3, ds, dpp, dg_ple, loss = _fwd_tail(x1, act, p, target, wd4, wpg, wpp4, g_ple, 512)
    big, big16, slots = {}, {}, {}

    def wgrad(name, a, b, tn):
        big[name], big16[name] = _wgrad(a, b, tn, 1024, "wgrad_" + name)
        return (big16[name], _COL_SHARDED[name])

    riders = [wgrad("w_down", act, dx2, 1024), wgrad("w_ple_gate", h3, ds, 1024), wgrad("w_ple_proj", p, dpp, 1024)]
    dgp, dup, dfcw, dfcb, got = _bwd_ffn_a(dx2, gate, gp, up, wd4, fcw, 512, riders if comm else ())
    slots.update(zip(("w_down", "w_ple_gate", "w_ple_proj"), got))
    riders = [wgrad("w_gate", dgp, h2, 1024), wgrad("w_up", dup, h2, 1024)]
    dx1, dycn, dya, dg_ffn, dg_oa = _bwd_ffn_b(dgp, dup, dx2, x1, ya, wg4, wu4, wout, g_ffn, g_oa, 512)
    riders.append(wgrad("w_out", ycat, dx1, 1024))
    dq, dk, dv, got = _attn_bwd(qkn, qkv, ya, lse, dya, mb, riders if comm else ())
    slots.update(zip(("w_gate", "w_up", "w_out"), got))
    dz, dcw, dcb, dg_oc, dgq8, dgk8, big["w_in"], big16["w_in"] = _bwd_mix(
        x, zbcx, qkv, dycn, dq, dk, dv, conv_w, conv_b, g_oc, g_mix, gm, gq8, gk8, 512)
    riders = [(big16["w_in"], _COL_SHARDED["w_in"])]
    grad_x, dg_mix, got = _bwd_in(x, dx1, dz, win4, g_mix, 512, riders if comm else ())
    slots.update(zip(("w_in",), got))
    dgq = dgq8.reshape(N_HEADS, HEAD_DIM).sum(axis=0, keepdims=True)
    dgk = dgk8.reshape(N_HEADS, HEAD_DIM).sum(axis=0, keepdims=True)
    small = dict(g_mix=dg_mix, conv_w=dcw, conv_b=dcb, q_norm_g=dgq, k_norm_g=dgk, g_out_conv=dg_oc,
                 g_out_attn=dg_oa, g_ffn=dg_ffn, ffn_conv_w=dfcw, ffn_conv_b=dfcb, g_ple=dg_ple)
    return loss[0, 0], grad_x, big, slots, small


_SMALL_ROWS = 24


def _pack_small(s, loss):
    z64 = jnp.zeros((1, 1024 - 512 - 128), F32)
    rows = [s["g_mix"], s["g_ffn"], s["g_ple"],
            jnp.concatenate([s["conv_b"], s["g_out_conv"]], axis=1),
            jnp.concatenate([s["g_out_attn"], s["q_norm_g"], s["k_norm_g"], z64], axis=1),
            jnp.pad(s["conv_w"], ((0, 0), (0, 512))),
            jnp.pad(s["ffn_conv_b"], ((0, 0), (0, 3072 - D_FF))).reshape(3, 1024),
            jnp.pad(s["ffn_conv_w"], ((0, 0), (0, 3072 - D_FF))).reshape(9, 1024),
            jnp.pad(loss.reshape(1, 1), ((0, 0), (0, 1023))),
            jnp.zeros((_SMALL_ROWS - 21, 1024), F32)]
    return jnp.concatenate(rows, axis=0)


def _unpack_small(t):
    return dict(g_mix=t[0:1], g_ffn=t[1:2], g_ple=t[2:3], conv_b=t[3:4, 0:512], g_out_conv=t[3:4, 512:1024],
                g_out_attn=t[4:5, 0:512], q_norm_g=t[4:5, 512:576], k_norm_g=t[4:5, 576:640],
                conv_w=t[5:8, 0:512], ffn_conv_b=t[8:11].reshape(1, 3072)[:, :D_FF],
                ffn_conv_w=t[11:20].reshape(3, 3072)[:, :D_FF], loss=t[20, 0])


_BIG = ("w_in", "w_out", "w_gate", "w_up", "w_down", "w_ple_gate", "w_ple_proj")
_COL_SHARDED = dict(w_in=True, w_out=False, w_gate=False, w_up=False, w_down=False, w_ple_gate=False, w_ple_proj=True)
_TRANSPOSED = ("w_gate", "w_up")
_WEIGHTS = ("g_mix", "w_in", "conv_w", "conv_b", "q_norm_g", "k_norm_g", "g_out_conv", "g_out_attn", "w_out",
            "g_ffn", "w_gate", "w_up", "ffn_conv_w", "ffn_conv_b", "w_down", "g_ple", "w_ple_gate", "w_ple_proj")


def kernel(x, p, g_mix, w_in, conv_w, conv_b, q_norm_g, k_norm_g, g_out_conv, g_out_attn, w_out, g_ffn, w_gate, w_up, ffn_conv_w, ffn_conv_b, w_down, g_ple, w_ple_gate, w_ple_proj, loss_target, m_g_mix, m_w_in, m_conv_w, m_conv_b, m_q_norm_g, m_k_norm_g, m_g_out_conv, m_g_out_attn, m_w_out, m_g_ffn, m_w_gate, m_w_up, m_ffn_conv_w, m_ffn_conv_b, m_w_down, m_g_ple, m_w_ple_gate, m_w_ple_proj, v_g_mix, v_w_in, v_conv_w, v_conv_b, v_q_norm_g, v_k_norm_g, v_g_out_conv, v_g_out_attn, v_w_out, v_g_ffn, v_w_gate, v_w_up, v_ffn_conv_w, v_ffn_conv_b, v_w_down, v_g_ple, v_w_ple_gate, v_w_ple_proj):
    w = dict(g_mix=g_mix, w_in=w_in, conv_w=conv_w, conv_b=conv_b, q_norm_g=q_norm_g, k_norm_g=k_norm_g,
             g_out_conv=g_out_conv, g_out_attn=g_out_attn, w_out=w_out, g_ffn=g_ffn, w_gate=w_gate, w_up=w_up,
             ffn_conv_w=ffn_conv_w, ffn_conv_b=ffn_conv_b, w_down=w_down, g_ple=g_ple, w_ple_gate=w_ple_gate,
             w_ple_proj=w_ple_proj)
    m = dict(g_mix=m_g_mix, w_in=m_w_in, conv_w=m_conv_w, conv_b=m_conv_b, q_norm_g=m_q_norm_g, k_norm_g=m_k_norm_g,
             g_out_conv=m_g_out_conv, g_out_attn=m_g_out_attn, w_out=m_w_out, g_ffn=m_g_ffn, w_gate=m_w_gate,
             w_up=m_w_up, ffn_conv_w=m_ffn_conv_w, ffn_conv_b=m_ffn_conv_b, w_down=m_w_down, g_ple=m_g_ple,
             w_ple_gate=m_w_ple_gate, w_ple_proj=m_w_ple_proj)
    v = dict(g_mix=v_g_mix, w_in=v_w_in, conv_w=v_conv_w, conv_b=v_conv_b, q_norm_g=v_q_norm_g, k_norm_g=v_k_norm_g,
             g_out_conv=v_g_out_conv, g_out_attn=v_g_out_attn, w_out=v_w_out, g_ffn=v_g_ffn, w_gate=v_w_gate,
             w_up=v_w_up, ffn_conv_w=v_ffn_conv_w, ffn_conv_b=v_ffn_conv_b, w_down=v_w_down, g_ple=v_g_ple,
             w_ple_gate=v_w_ple_gate, w_ple_proj=v_w_ple_proj)
    mats = [k for k, a in w.items() if a.ndim == 3]
    w = {k: (a[0] if k in mats else a) for k, a in w.items()}
    m = {k: (a[0] if k in mats else a) for k, a in m.items()}
    v = {k: (a[0] if k in mats else a) for k, a in v.items()}
    for n in _TRANSPOSED:
        w[n], m[n], v[n] = w[n].T, m[n].T, v[n].T
    chip = 2 * lax.axis_index("x") + lax.axis_index("y")

    late = [w[n].astype(BF) for n in ("w_out", "w_gate", "w_up", "w_down", "w_ple_gate", "w_ple_proj")]
    pack = jnp.pad(jnp.concatenate([w["conv_w"], w["ffn_conv_w"]], axis=1), ((0, 5), (0, 1024 - 128 - D_FF_SHARD)))
    win4, pack4 = _gather_weights([w["w_in"].astype(BF)], pack)
    conv_w_full = pack4[:, 0:3, 0:128].transpose(1, 0, 2).reshape(3, CONV_W)
    fcw_full = pack4[:, 0:3, 128:128 + D_FF_SHARD].transpose(1, 0, 2).reshape(3, D_FF)
    wts = (win4, None, None, None, None, None, None, conv_w_full, fcw_full, w["g_mix"], w["conv_b"], w["q_norm_g"],
           w["k_norm_g"], w["g_out_conv"], w["g_out_attn"], w["g_ffn"], w["ffn_conv_b"], w["g_ple"])

    loss, grad_x, big, slots, small = _local_step(x[0], p[0, 0], loss_target[0], wts, late)

    grads, deltas, new_m, new_v = {}, {}, {}, {}
    for name in _BIG:
        total = _finish_reduce(big[name], slots[name], _COL_SHARDED[name], "finish_" + name)
        grads[name], deltas[name], new_m[name], new_v[name] = _adamw_big(total, w[name], m[name], v[name],
                                                                         "adamw_" + name)
    tot = _unpack_small(_allreduce_small(_pack_small(small, loss)))
    loss = tot.pop("loss")
    tot["conv_w"] = lax.dynamic_slice_in_dim(tot["conv_w"], chip * 128, 128, axis=1)
    tot["ffn_conv_w"] = lax.dynamic_slice_in_dim(tot["ffn_conv_w"], chip * D_FF_SHARD, D_FF_SHARD, axis=1)
    names = [n for n in _WEIGHTS if n not in _BIG]
    d_s, m_s, v_s = _adamw_small([w[n] for n in names], [tot[n] for n in names], [m[n] for n in names],
                                 [v[n] for n in names])
    for j, n in enumerate(names):
        grads[n], deltas[n], new_m[n], new_v[n] = tot[n], d_s[j], m_s[j], v_s[j]

    out = [loss, grad_x[None]]
    for group in (grads, deltas, new_m, new_v):
        for n in _TRANSPOSED:
            group[n] = group[n].T
        out += [group[n][None] if n in mats else group[n] for n in _WEIGHTS]
    return tuple(out)
```

```python
import jax
import jax.numpy as jnp
from jax import lax
from jax.experimental import pallas as pl
from jax.experimental.pallas import tpu as pltpu

D_MODEL = 1024
CONV_W = 512
N_HEADS = 8
HEAD_DIM = 64
ATTN_W = 512
D_FF = 2816
D_FF_SHARD = 704
FF_SLABS = ((0, 1408), (1408, 2816))
IN_SLAB = 768
PLE_DIM = 256
N_CHIPS = 4
QBLK = 128
DILATIONS = (1, 4, 16)
EPS = 1e-6
NEG = -1e30
MESH = pl.DeviceIdType.MESH

ADAM_LR = 0.001
ADAM_B1 = 0.9
ADAM_B2 = 0.999
ADAM_EPS = 1e-08
ADAM_WD = 0.01
ADAM_STEP = 10

BF = jnp.bfloat16
F32 = jnp.float32
MIB = 1024 * 1024


def _mm(a, b):
    return jnp.dot(a, b, preferred_element_type=F32)


def _mm_nt(a, b):
    return lax.dot_general(a, b, (((1,), (1,)), ((), ())), preferred_element_type=F32)


def _mm_tn(a, b):
    return lax.dot_general(a, b, (((0,), (0,)), ((), ())), preferred_element_type=F32)


def _rstd(a):
    return lax.rsqrt(jnp.mean(a * a, axis=-1, keepdims=True) + EPS)


def _norm_bwd(dy, xh, r, g):
    dxh = dy * g
    return r * (dxh - xh * jnp.mean(dxh * xh, axis=-1, keepdims=True))


def _colsum(a):
    return jnp.sum(a, axis=0, keepdims=True)


def _head_mean(a, gm_ref):
    return _mm(a.astype(BF), gm_ref[...])


def _shift_down(buf, k, tm):
    return pltpu.roll(buf, k, axis=0)[8:8 + tm]


def _shift_up(buf, k, tm):
    return pltpu.roll(buf, tm + 8 - k, axis=0)[0:tm]


def _params(vmem_mib, n_grid=1):
    return pltpu.CompilerParams(dimension_semantics=("arbitrary",) * n_grid, vmem_limit_bytes=vmem_mib * MIB)


def _const(shape):
    n = len(shape)
    return pl.BlockSpec(shape, lambda *_: (0,) * n, pipeline_mode=pl.Buffered(1))


def _rows(tm, width, rev_of=None):
    if rev_of is None:
        return pl.BlockSpec((tm, width), lambda i: (i, 0))
    return pl.BlockSpec((tm, width), lambda i: (rev_of - 1 - i, 0))


def _halo(tm, width, nt):
    return pl.BlockSpec((8, width), lambda i: (jnp.maximum((nt - 1 - i) * (tm // 8) - 1, 0), 0))


def _fwd_mix(x, g_mix, win4, conv_w, conv_b, g_oc, gm, gq8, gk8, tm):
    t = x.shape[0]
    nt = t // tm

    def body(x_ref, g_ref, w_ref, cw_ref, cb_ref, goc_ref, gm_ref, gq_ref, gk_ref,
             zbcx_ref, qkv_ref, ycn_ref, qkn_ref, ubuf):
        @pl.when(pl.program_id(0) == 0)
        def _():
            ubuf[0:8, :] = jnp.zeros((8, CONV_W), F32)

        xt = x_ref[...]
        h = ((xt * _rstd(xt)) * g_ref[...]).astype(BF)
        zbcx_ref[:, 0:IN_SLAB] = _mm(h, w_ref[0])
        zbcx_ref[:, IN_SLAB:2 * IN_SLAB] = _mm(h, w_ref[1])
        qkv_ref[:, 0:IN_SLAB] = _mm(h, w_ref[2])
        qkv_ref[:, IN_SLAB:2 * IN_SLAB] = _mm(h, w_ref[3])
        u = zbcx_ref[:, 512:1024] * zbcx_ref[:, 1024:1536]
        ubuf[8:8 + tm, :] = u
        ub = ubuf[...]
        cv = (cw_ref[0:1, :] * _shift_down(ub, 2, tm) + cw_ref[1:2, :] * _shift_down(ub, 1, tm)
              + cw_ref[2:3, :] * u + cb_ref[...])
        ubuf[0:8, :] = ubuf[tm:tm + 8, :]
        yc = zbcx_ref[:, 0:512] * cv
        ycn_ref[...] = ((yc * _rstd(yc)) * goc_ref[...]).astype(BF)
        zq = qkv_ref[:, 0:512]
        zk = qkv_ref[:, 512:1024]
        rq = lax.rsqrt(_head_mean(zq * zq, gm_ref) + EPS)
        rk = lax.rsqrt(_head_mean(zk * zk, gm_ref) + EPS)
        qkn_ref[:, 0:512] = ((zq * rq) * gq_ref[...]) * (HEAD_DIM ** -0.5)
        qkn_ref[:, 512:1024] = (zk * rk) * gk_ref[...]

    return pl.pallas_call(
        body, name="fwd_mix", grid=(nt,),
        in_specs=[_rows(tm, D_MODEL), _const((1, D_MODEL)), _const((N_CHIPS, D_MODEL, IN_SLAB)),
                  _const((3, CONV_W)), _const((1, CONV_W)), _const((1, CONV_W)), _const((ATTN_W, ATTN_W)),
                  _const((1, ATTN_W)), _const((1, ATTN_W))],
        out_specs=[_rows(tm, 1536), _rows(tm, 1536), _rows(tm, CONV_W), _rows(tm, 1024)],
        out_shape=[jax.ShapeDtypeStruct((t, 1536), F32), jax.ShapeDtypeStruct((t, 1536), F32),
                   jax.ShapeDtypeStruct((t, CONV_W), BF), jax.ShapeDtypeStruct((t, 1024), F32)],
        scratch_shapes=[pltpu.VMEM((tm + 8, CONV_W), F32)],
        compiler_params=_params(48),
    )(x, g_mix, win4, conv_w, conv_b, g_oc, gm, gq8, gk8)


def _place():
    x, y, c = lax.axis_index("x"), lax.axis_index("y"), lax.axis_index("c")
    return x, y, c


def _chip_peer(x, y, k):
    return x ^ (k >> 1), y ^ (k & 1)


def _piece_shape(grad_shape, col_sharded):
    kk, nn = grad_shape
    return (kk // 2, nn // N_CHIPS) if col_sharded else (kk // (2 * N_CHIPS), nn)


def _piece_window(col_sharded, r, cw, s, h):
    if col_sharded:
        return (pl.ds(pl.multiple_of(h * r, 16), r), pl.ds(pl.multiple_of(s * cw, 128), cw))
    return (pl.ds(pl.multiple_of((2 * s + h) * r, 16), r), slice(None))


def _scatter_copies(g_ref, slots_ref, send_sems, recv_sems, base, col_sharded):
    x, y, c = _place()
    r, cw = slots_ref.shape[1:]
    copies = []
    for k in range(1, 8):
        tx, ty, tc = x ^ (k >> 2), y ^ ((k >> 1) & 1), c ^ (k & 1)
        copies.append(pltpu.make_async_remote_copy(
            src_ref=g_ref.at[_piece_window(col_sharded, r, cw, 2 * tx + ty, tc)], dst_ref=slots_ref.at[k - 1],
            send_sem=send_sems.at[base + k - 1], recv_sem=recv_sems.at[base + k - 1],
            device_id=(tx, ty, tc), device_id_type=MESH))
    return copies


def _ride_scatter(first, last, riders, g_refs, slot_refs, send_sems, recv_sems):
    def all_copies():
        out = []
        for j, (_, col_sharded) in enumerate(riders):
            out += _scatter_copies(g_refs[j], slot_refs[j], send_sems, recv_sems, 7 * j, col_sharded)
        return out

    @pl.when(first)
    def _():
        for cp in all_copies():
            cp.start()

    @pl.when(last)
    def _():
        for cp in all_copies():
            cp.wait()


def _rider_specs(riders):
    any_spec = pl.BlockSpec(memory_space=pltpu.HBM)
    shapes = [jax.ShapeDtypeStruct((7,) + _piece_shape(g.shape, cs), BF) for g, cs in riders]
    sems = [pltpu.SemaphoreType.DMA((7 * len(riders),)), pltpu.SemaphoreType.DMA((7 * len(riders),))] if riders else []
    return [any_spec] * len(riders), shapes, sems


class _Gather:
    def __init__(self, ins, outs, send_sems, recv_sems, local_sems):
        self.ins, self.outs = ins, outs
        self.send_sems, self.recv_sems, self.local_sems = send_sems, recv_sems, local_sems
        self.x, self.y, self.c = _place()
        self.me = 2 * self.x + self.y

    def _push(self, src, dst, w, j, to):
        return pltpu.make_async_remote_copy(src_ref=src, dst_ref=dst, send_sem=self.send_sems.at[6 * w + j],
                                            recv_sem=self.recv_sems.at[6 * w + j], device_id=to, device_id_type=MESH)

    def _half(self, w, h):
        half = self.ins[w].shape[0] // 2
        return pl.ds(pl.multiple_of(h * half, 16), half)

    def _local(self, w):
        return pltpu.make_async_copy(self.ins[w], self.outs[w].at[self.me], self.local_sems.at[w])

    def _ici(self, w, k):
        px, py = _chip_peer(self.x, self.y, k)
        mine = self._half(w, self.c)
        return self._push(self.ins[w].at[mine], self.outs[w].at[self.me, mine], w, k - 1, (px, py, self.c))

    def _landed(self, w, k, h):
        return self.outs[w].at[self.me ^ k, self._half(w, h)]

    def _fwd(self, w, k):
        landed = self._landed(w, k, self.c)
        return self._push(landed, landed, w, 2 + k, (self.x, self.y, 1 - self.c))

    def start(self):
        for w in range(len(self.ins)):
            self._local(w).start()
            for k in (1, 2, 3):
                self._ici(w, k).start()

    def forward(self):
        for w in range(len(self.ins)):
            for k in (1, 2, 3):
                landed = self._landed(w, k, self.c)
                self._push(landed, landed, w, k - 1, (self.x, self.y, self.c)).wait_recv()
                self._fwd(w, k).start()

    def finish(self):
        for w in range(len(self.ins)):
            for k in (1, 2, 3):
                landed = self._landed(w, k, 1 - self.c)
                self._push(landed, landed, w, 2 + k, (self.x, self.y, self.c)).wait_recv()
            for k in (1, 2, 3):
                self._ici(w, k).wait_send()
                self._fwd(w, k).wait_send()
            self._local(w).wait()


def _alibi(h):
    return 2.0 ** (-(h + 1))


CHUNK = 2048


def _mask_table():
    slopes = jnp.asarray([_alibi(h) for h in range(N_HEADS)], F32)[:, None, None]
    step = jnp.arange(QBLK)[:, None] + QBLK - jnp.arange(2 * QBLK)[None, :]
    valid = (step >= 0) & (step <= QBLK)
    tab = jnp.stack([jnp.where(valid[None], -slopes * (step * d)[None].astype(F32), NEG) for d in DILATIONS])
    return tab.reshape(3, N_HEADS // 2, 2 * QBLK, 2 * QBLK)


def _attn_fwd(qkn, qkv, mb, late=()):
    t = qkn.shape[0]
    nc = t // CHUNK
    nl = len(late)

    def body(*refs):
        qc_ref, kp_ref, kc_ref, vp_ref, vc_ref, mb_ref = refs[0:6]
        o_ref, l_ref = refs[6 + nl:8 + nl]
        ob0, ob1, ob2, lb0, lb1, lb2 = refs[8 + 2 * nl:14 + 2 * nl]
        if nl:
            gather = _Gather(refs[6:6 + nl], refs[8 + nl:8 + 2 * nl], *refs[14 + 2 * nl:17 + 2 * nl])
            step = pl.program_id(0) * nc + pl.program_id(1)
            pl.when(step == 0)(gather.start)
            pl.when(step == 2 * nc)(gather.forward)
            pl.when(step == (N_HEADS // 2) * nc - 1)(gather.finish)
        first = pl.program_id(1) == 0
        lane = lax.broadcasted_iota(jnp.int32, (QBLK, 128), 1)
        lo_half = lane < HEAD_DIM
        kj = lax.broadcasted_iota(jnp.int32, (2 * QBLK, 2 * QBLK), 1)
        no_prev = first & (kj < QBLK)
        obs, lbs = (ob0, ob1, ob2), (lb0, lb1, lb2)

        def by_head(a):
            return jnp.where(lo_half, a, 0.0).astype(BF), jnp.where(lo_half, 0.0, a).astype(BF)

        for di, d in enumerate(DILATIONS):
            span = d * QBLK
            for r in range(d):
                tail = pl.ds(CHUNK - span + r, QBLK, stride=d)
                k_prev = kp_ref[tail, :].astype(BF)
                v_prev = by_head(vp_ref[tail, :])
                for b in range(CHUNK // span):
                    rows = pl.ds(r + span * b, QBLK, stride=d)
                    q0, q1 = by_head(qc_ref[rows, :])
                    k_cur = kc_ref[rows, :].astype(BF)
                    v_cur = by_head(vc_ref[rows, :])
                    s = _mm_nt(jnp.concatenate([q0, q1], axis=0), jnp.concatenate([k_prev, k_cur], axis=0))
                    s = s + mb_ref[di, 0]
                    if b == 0:
                        s = jnp.where(no_prev, NEG, s)
                    m = jnp.max(s, axis=-1, keepdims=True)
                    e = jnp.exp(s - m)
                    den = jnp.sum(e, axis=-1, keepdims=True)
                    eb = e.astype(BF)
                    o = _mm(jnp.concatenate([eb[0:QBLK], eb[QBLK:2 * QBLK]], axis=1),
                            jnp.concatenate([v_prev[0], v_cur[0], v_prev[1], v_cur[1]], axis=0))
                    inv = 1.0 / den
                    lse = m + jnp.log(den)
                    obs[di][rows, :] = o * jnp.where(lo_half, inv[0:QBLK], inv[QBLK:2 * QBLK])
                    lbs[di][rows, :] = jnp.where(lo_half, lse[0:QBLK], lse[QBLK:2 * QBLK])
                    k_prev, v_prev = k_cur, v_cur
        for c0 in range(0, CHUNK, 256):
            rs = slice(c0, c0 + 256)
            l0, l1, l2 = lb0[rs, :], lb1[rs, :], lb2[rs, :]
            mx = jnp.maximum(jnp.maximum(l0, l1), l2)
            w0, w1, w2 = jnp.exp(l0 - mx), jnp.exp(l1 - mx), jnp.exp(l2 - mx)
            tot = w0 + w1 + w2
            o_ref[rs, :] = (ob0[rs, :] * w0 + ob1[rs, :] * w1 + ob2[rs, :] * w2) / tot
            l_ref[rs, :] = mx + jnp.log(tot)

    def cur(col):
        return pl.BlockSpec((CHUNK, 128), lambda hp, n: (n, col + hp))

    def prv(col):
        return pl.BlockSpec((CHUNK, 128), lambda hp, n: (jnp.maximum(n - 1, 0), col + hp))

    out = pl.BlockSpec((CHUNK, 128), lambda hp, n: (n, hp))
    any_spec = pl.BlockSpec(memory_space=pltpu.HBM)
    sems = [pltpu.SemaphoreType.DMA((6 * nl,)), pltpu.SemaphoreType.DMA((6 * nl,)), pltpu.SemaphoreType.DMA((nl,))]
    res = pl.pallas_call(
        body, name="attn_fwd", grid=(N_HEADS // 2, nc),
        in_specs=[cur(0), prv(4), cur(4), prv(8), cur(8),
                  pl.BlockSpec((3, 1, 2 * QBLK, 2 * QBLK), lambda hp, n: (0, hp, 0, 0))] + [any_spec] * nl,
        out_specs=[out, out] + [any_spec] * nl,
        out_shape=[jax.ShapeDtypeStruct((t, ATTN_W), F32)] * 2
        + [jax.ShapeDtypeStruct((N_CHIPS,) + w.shape, w.dtype) for w in late],
        scratch_shapes=[pltpu.VMEM((CHUNK, 128), F32)] * 6 + (sems if nl else []),
        compiler_params=_params(48, 2),
    )(qkn, qkn, qkn, qkv, qkv, mb, *late)
    return res[0], res[1], list(res[2:])


def _attn_bwd(qkn, qkv, o, lse, do, mb, riders=()):
    t = qkn.shape[0]
    nc = t // CHUNK
    nr = len(riders)

    def body(*refs):
        (qc_ref, qn_ref, kp_ref, kc_ref, vp_ref, vc_ref, oc_ref, on_ref, lc_ref, ln_ref, dc_ref, dn_ref,
         mb_ref) = refs[0:13]
        dq_ref, dk_ref, dv_ref = refs[13 + nr:16 + nr]
        if nr:
            step = pl.program_id(0) * nc + pl.program_id(1)
            _ride_scatter(step == 0, step == (N_HEADS // 2) * nc - 1, riders, refs[13:13 + nr],
                          refs[16 + nr:16 + 2 * nr], *refs[16 + 2 * nr:18 + 2 * nr])
        first = pl.program_id(1) == 0
        last = pl.program_id(1) == nc - 1
        lane = lax.broadcasted_iota(jnp.int32, (QBLK, 128), 1)
        lo_half = lane < HEAD_DIM
        kj = lax.broadcasted_iota(jnp.int32, (2 * QBLK, 2 * QBLK), 1)
        no_prev = first & (kj < QBLK)

        def by_head(a):
            return jnp.where(lo_half, a, 0.0).astype(BF), jnp.where(lo_half, 0.0, a).astype(BF)

        def query_side(q_ref, d_ref, o_ref_, l_ref_, rows):
            dvals = d_ref[rows, :]
            dd = dvals * o_ref_[rows, :]
            lv = l_ref_[rows, :]
            d0 = jnp.sum(jnp.where(lo_half, dd, 0.0), axis=-1, keepdims=True)
            d1 = jnp.sum(jnp.where(lo_half, 0.0, dd), axis=-1, keepdims=True)
            l0 = jnp.max(jnp.where(lo_half, lv, NEG), axis=-1, keepdims=True)
            l1 = jnp.max(jnp.where(lo_half, NEG, lv), axis=-1, keepdims=True)
            return (jnp.concatenate(by_head(q_ref[rows, :]), axis=0), jnp.concatenate(by_head(dvals), axis=0),
                    jnp.concatenate([l0, l1], axis=0), jnp.concatenate([d0, d1], axis=0))

        def tile(qs, dos, lcol, dcol, keys, vals, bias, dead):
            s = _mm_nt(qs, keys) + bias
            if dead is not None:
                s = jnp.where(dead, NEG, s)
            p = jnp.exp(s - lcol)
            ds = p * (_mm_nt(dos, vals) - dcol)
            return p.astype(BF), ds.astype(BF)

        def put(ref, di, rows, val):
            if di == 0:
                ref[rows, :] = val
            else:
                ref[rows, :] = ref[rows, :] + val

        for di, d in enumerate(DILATIONS):
            span = d * QBLK
            nbk = CHUNK // span
            for r in range(d):
                tail = pl.ds(CHUNK - span + r, QBLK, stride=d)
                k_prev = kp_ref[tail, :]
                kb_prev, km_prev = k_prev.astype(BF), by_head(k_prev)
                vb_prev = vp_ref[tail, :].astype(BF)
                rows_prev, dk_part, dv_part = None, None, None
                for b in range(nbk):
                    rows = pl.ds(r + span * b, QBLK, stride=d)
                    qs, dos, lcol, dcol = query_side(qc_ref, dc_ref, oc_ref, lc_ref, rows)
                    k_cur = kc_ref[rows, :]
                    kb_cur, km_cur = k_cur.astype(BF), by_head(k_cur)
                    vb_cur = vc_ref[rows, :].astype(BF)
                    p, ds = tile(qs, dos, lcol, dcol, jnp.concatenate([kb_prev, kb_cur], axis=0),
                                 jnp.concatenate([vb_prev, vb_cur], axis=0), mb_ref[di, 0],
                                 no_prev if b == 0 else None)
                    put(dq_ref, di, rows,
                        _mm(jnp.concatenate([ds[0:QBLK], ds[QBLK:2 * QBLK]], axis=1),
                            jnp.concatenate([km_prev[0], km_cur[0], km_prev[1], km_cur[1]], axis=0)))
                    dk2 = _mm_tn(ds, qs)
                    dv2 = _mm_tn(p, dos)
                    if b > 0:
                        put(dk_ref, di, rows_prev, dk_part + dk2[0:QBLK])
                        put(dv_ref, di, rows_prev, dv_part + dv2[0:QBLK])
                    rows_prev, dk_part, dv_part = rows, dk2[QBLK:2 * QBLK], dv2[QBLK:2 * QBLK]
                    kb_prev, km_prev, vb_prev = kb_cur, km_cur, vb_cur
                qs, dos, lcol, dcol = query_side(qn_ref, dn_ref, on_ref, ln_ref, pl.ds(r, QBLK, stride=d))
                p, ds = tile(qs, dos, lcol, dcol, kb_prev, vb_prev, mb_ref[di, 0, :, 0:QBLK], last)
                put(dk_ref, di, rows_prev, dk_part + _mm_tn(ds, qs))
                put(dv_ref, di, rows_prev, dv_part + _mm_tn(p, dos))

    def at(shift, col):
        return pl.BlockSpec((CHUNK, 128), lambda hp, n: (jnp.clip(n + shift, 0, nc - 1), col + hp))

    out = pl.BlockSpec((CHUNK, 128), lambda hp, n: (n, hp))
    r_in, r_out, r_sems = _rider_specs(riders)
    res = pl.pallas_call(
        body, name="attn_bwd", grid=(N_HEADS // 2, nc),
        in_specs=[at(0, 0), at(1, 0), at(-1, 4), at(0, 4), at(-1, 8), at(0, 8),
                  at(0, 0), at(1, 0), at(0, 0), at(1, 0), at(0, 0), at(1, 0),
                  pl.BlockSpec((3, 1, 2 * QBLK, 2 * QBLK), lambda hp, n: (0, hp, 0, 0))] + r_in,
        out_specs=[out, out, out] + r_in,
        out_shape=[jax.ShapeDtypeStruct((t, ATTN_W), F32)] * 3 + r_out,
        scratch_shapes=r_sems,
        compiler_params=_params(56, 2),
    )(qkn, qkn, qkn, qkn, qkv, qkv, o, o, lse, lse, do, do, mb, *[g for g, _ in riders])
    return res[0], res[1], res[2], list(res[3:])


def _fwd_ffn(x, ycn, ya, wout, wg4, wu4, g_oa, g_ffn, fcw, fcb, tm, late=()):
    t = x.shape[0]
    nt = t // tm
    nl = len(late)

    def body(*refs):
        x_ref, ycn_ref, ya_ref, wout_ref, wg_ref, wu_ref, goa_ref, gffn_ref, fcw_ref, fcb_ref = refs[0:10]
        x1_ref, gp_ref, up_ref, gate_ref, act_ref, ycat_ref, h2_ref = refs[10 + nl:17 + nl]
        cbuf = refs[17 + 2 * nl]
        if nl:
            gather = _Gather(refs[10:10 + nl], refs[17 + nl:17 + 2 * nl], *refs[18 + 2 * nl:21 + 2 * nl])
            pl.when(pl.program_id(0) == 0)(gather.start)
            pl.when(pl.program_id(0) == nt // 2)(gather.forward)
            pl.when(pl.program_id(0) == nt - 1)(gather.finish)

        @pl.when(pl.program_id(0) == 0)
        def _():
            cbuf[0:8, :] = jnp.zeros((8, D_FF), F32)

        yat = ya_ref[...]
        yan = ((yat * _rstd(yat)) * goa_ref[...]).astype(BF)
        ycn = ycn_ref[...]
        ycat_ref[:, 0:CONV_W] = ycn
        ycat_ref[:, CONV_W:D_MODEL] = yan
        x1 = x_ref[...] + _mm(ycn, wout_ref[0:CONV_W, :]) + _mm(yan, wout_ref[CONV_W:D_MODEL, :])
        x1_ref[...] = x1
        h2 = ((x1 * _rstd(x1)) * gffn_ref[...]).astype(BF)
        h2_ref[...] = h2
        for lo, hi in FF_SLABS:
            gps = _mm_nt(h2, wg_ref[lo:hi, :])
            ups = _mm_nt(h2, wu_ref[lo:hi, :])
            gp_ref[:, lo:hi] = gps.astype(BF)
            up_ref[:, lo:hi] = ups.astype(BF)
            cbuf[8:8 + tm, lo:hi] = gps
            cb = cbuf[:, lo:hi]
            gate = (fcw_ref[0:1, lo:hi] * _shift_down(cb, 2, tm) + fcw_ref[1:2, lo:hi] * _shift_down(cb, 1, tm)
                    + fcw_ref[2:3, lo:hi] * gps + fcb_ref[:, lo:hi])
            gate_ref[:, lo:hi] = gate.astype(BF)
            act_ref[:, lo:hi] = ((gate * jax.nn.sigmoid(gate)) * ups).astype(BF)
        cbuf[0:8, :] = cbuf[tm:tm + 8, :]

    any_spec = pl.BlockSpec(memory_space=pltpu.HBM)
    sems = [pltpu.SemaphoreType.DMA((6 * nl,)), pltpu.SemaphoreType.DMA((6 * nl,)), pltpu.SemaphoreType.DMA((nl,))]
    res = pl.pallas_call(
        body, name="fwd_ffn", grid=(nt,),
        in_specs=[_rows(tm, D_MODEL), _rows(tm, CONV_W), _rows(tm, ATTN_W), _const((D_MODEL, D_MODEL)),
                  _const((D_FF, D_MODEL)), _const((D_FF, D_MODEL)),
                  _const((1, ATTN_W)), _const((1, D_MODEL)), _const((3, D_FF)), _const((1, D_FF))]
        + [any_spec] * nl,
        out_specs=[_rows(tm, D_MODEL), _rows(tm, D_FF), _rows(tm, D_FF), _rows(tm, D_FF), _rows(tm, D_FF),
                   _rows(tm, D_MODEL), _rows(tm, D_MODEL)] + [any_spec] * nl,
        out_shape=[jax.ShapeDtypeStruct((t, D_MODEL), F32), jax.ShapeDtypeStruct((t, D_FF), BF),
                   jax.ShapeDtypeStruct((t, D_FF), BF), jax.ShapeDtypeStruct((t, D_FF), BF),
                   jax.ShapeDtypeStruct((t, D_FF), BF),
                   jax.ShapeDtypeStruct((t, D_MODEL), BF), jax.ShapeDtypeStruct((t, D_MODEL), BF)]
        + [jax.ShapeDtypeStruct((N_CHIPS,) + w.shape, w.dtype) for w in late],
        scratch_shapes=[pltpu.VMEM((tm + 8, D_FF), F32)] + (sems if nl else []),
        compiler_params=_params(56),
    )(x, ycn, ya, wout, wg4, wu4, g_oa, g_ffn, fcw, fcb, *late)
    return tuple(res[0:7]) + (list(res[7:]),)


def _fwd_tail(x1, act, p, target, wd4, wpg, wpp4, g_ple, tm):
    t = x1.shape[0]
    nt = t // tm

    def body(x1_ref, act_ref, p_ref, tgt_ref, wd_ref, wpg_ref, wpp_ref, g_ref,
             dx2_ref, h3_ref, ds_ref, dpp_ref, dg_ref, loss_ref, lacc):
        i = pl.program_id(0)

        @pl.when(i == 0)
        def _():
            dg_ref[...] = jnp.zeros_like(dg_ref)
            lacc[...] = jnp.zeros_like(lacc)

        x2 = x1_ref[...]
        for lo, hi in FF_SLABS:
            x2 = x2 + _mm(act_ref[:, lo:hi], wd_ref[lo:hi, :])
        r3 = _rstd(x2)
        xh = x2 * r3
        h3 = (xh * g_ref[...]).astype(BF)
        h3_ref[...] = h3
        sg = jax.nn.sigmoid(_mm(h3, wpg_ref[...]))
        pb = p_ref[...].astype(BF)
        pp = jnp.concatenate([_mm(pb, wpp_ref[s]) for s in range(N_CHIPS)], axis=1)
        err = (x2 + sg * pp) - tgt_ref[...]
        lacc[...] += _colsum(err * err)
        dx3 = err * (1.0 / D_MODEL)
        dpp_ref[...] = (dx3 * sg).astype(BF)
        dsb = ((dx3 * pp) * (sg * (1.0 - sg))).astype(BF)
        ds_ref[...] = dsb
        dh3 = _mm_nt(dsb, wpg_ref[...])
        dg_ref[...] += _colsum(dh3 * xh)
        dx2_ref[...] = dx3 + _norm_bwd(dh3, xh, r3, g_ref[...])

        @pl.when(i == nt - 1)
        def _():
            loss_ref[...] = jnp.full((1, 128), jnp.sum(lacc[...]) * (0.5 / D_MODEL), F32)

    return pl.pallas_call(
        body, name="fwd_tail", grid=(nt,),
        in_specs=[_rows(tm, D_MODEL), _rows(tm, D_FF), _rows(tm, PLE_DIM), _rows(tm, D_MODEL),
                  _const((D_FF, D_MODEL)), _const((D_MODEL, D_MODEL)),
                  _const((N_CHIPS, PLE_DIM, PLE_DIM)), _const((1, D_MODEL))],
        out_specs=[_rows(tm, D_MODEL), _rows(tm, D_MODEL), _rows(tm, D_MODEL), _rows(tm, D_MODEL),
                   pl.BlockSpec((1, D_MODEL), lambda i: (0, 0)), pl.BlockSpec((1, 128), lambda i: (0, 0))],
        out_shape=[jax.ShapeDtypeStruct((t, D_MODEL), F32), jax.ShapeDtypeStruct((t, D_MODEL), BF),
                   jax.ShapeDtypeStruct((t, D_MODEL), BF), jax.ShapeDtypeStruct((t, D_MODEL), BF),
                   jax.ShapeDtypeStruct((1, D_MODEL), F32), jax.ShapeDtypeStruct((1, 128), F32)],
        scratch_shapes=[pltpu.VMEM((1, D_MODEL), F32)],
        compiler_params=_params(48),
    )(x1, act, p, target, wd4, wpg, wpp4, g_ple)


def _bwd_ffn_a(dx2, gate, gp, up, wd4, fcw, tm, riders=()):
    t = dx2.shape[0]
    nt = t // tm
    nr = len(riders)

    def body(*refs):
        dx2_ref, gate_ref, gp_ref, up_ref, wd_ref, fcw_ref = refs[0:6]
        dgp_ref, dup_ref, dfcw_ref, dfcb_ref = refs[6 + nr:10 + nr]
        dbuf = refs[10 + 2 * nr]
        i = pl.program_id(0)
        if nr:
            _ride_scatter(i == 0, i == nt - 1, riders, refs[6:6 + nr], refs[10 + nr:10 + 2 * nr],
                          *refs[11 + 2 * nr:13 + 2 * nr])

        @pl.when(i == 0)
        def _():
            dbuf[tm:tm + 8, :] = jnp.zeros((8, D_FF), F32)
            dfcw_ref[...] = jnp.zeros_like(dfcw_ref)
            dfcb_ref[...] = jnp.zeros_like(dfcb_ref)

        dx2b = dx2_ref[...].astype(BF)
        for lo, hi in FF_SLABS:
            gate = gate_ref[:, lo:hi].astype(F32)
            gps = gp_ref[:, lo:hi].astype(F32)
            w0, w1, w2 = fcw_ref[0:1, lo:hi], fcw_ref[1:2, lo:hi], fcw_ref[2:3, lo:hi]
            sg = jax.nn.sigmoid(gate)
            dact = _mm_nt(dx2b, wd_ref[lo:hi, :])
            dup_ref[:, lo:hi] = (dact * (gate * sg)).astype(BF)
            dgate = (dact * up_ref[:, lo:hi].astype(F32)) * (sg * (1.0 + gate * (1.0 - sg)))
            dbuf[0:tm, lo:hi] = dgate
            db = dbuf[:, lo:hi]
            d1 = _shift_up(db, 1, tm)
            d2 = _shift_up(db, 2, tm)
            dfcb_ref[:, lo:hi] += _colsum(dgate)
            dfcw_ref[0:1, lo:hi] += _colsum(d2 * gps)
            dfcw_ref[1:2, lo:hi] += _colsum(d1 * gps)
            dfcw_ref[2:3, lo:hi] += _colsum(dgate * gps)
            dgp_ref[:, lo:hi] = (w2 * dgate + w1 * d1 + w0 * d2).astype(BF)
        dbuf[tm:tm + 8, :] = dbuf[0:8, :]

    r_in, r_out, r_sems = _rider_specs(riders)
    res = pl.pallas_call(
        body, name="bwd_ffn_a", grid=(nt,),
        in_specs=[_rows(tm, D_MODEL, nt), _rows(tm, D_FF, nt), _rows(tm, D_FF, nt), _rows(tm, D_FF, nt),
                  _const((D_FF, D_MODEL)), _const((3, D_FF))] + r_in,
        out_specs=[_rows(tm, D_FF, nt), _rows(tm, D_FF, nt),
                   pl.BlockSpec((3, D_FF), lambda i: (0, 0)), pl.BlockSpec((1, D_FF), lambda i: (0, 0))] + r_in,
        out_shape=[jax.ShapeDtypeStruct((t, D_FF), BF), jax.ShapeDtypeStruct((t, D_FF), BF),
                   jax.ShapeDtypeStruct((3, D_FF), F32), jax.ShapeDtypeStruct((1, D_FF), F32)] + r_out,
        scratch_shapes=[pltpu.VMEM((tm + 8, D_FF), F32)] + r_sems,
        compiler_params=_params(56),
    )(dx2, gate, gp, up, wd4, fcw, *[g for g, _ in riders])
    return res[0], res[1], res[2], res[3], list(res[4:])


def _bwd_ffn_b(dgp, dup, dx2, x1, ya, wg4, wu4, wout, g_ffn, g_oa, tm):
    t = dx2.shape[0]
    nt = t // tm

    def body(dgp_ref, dup_ref, dx2_ref, x1_ref, ya_ref, wg_ref, wu_ref, wout_ref, gffn_ref, goa_ref,
             dx1_ref, dycn_ref, dya_ref, dgffn_ref, dgoa_ref):
        @pl.when(pl.program_id(0) == 0)
        def _():
            dgffn_ref[...] = jnp.zeros_like(dgffn_ref)
            dgoa_ref[...] = jnp.zeros_like(dgoa_ref)

        dh2 = jnp.zeros((tm, D_MODEL), F32)
        for lo, hi in FF_SLABS:
            dh2 = dh2 + _mm(dgp_ref[:, lo:hi], wg_ref[lo:hi, :]) + _mm(dup_ref[:, lo:hi], wu_ref[lo:hi, :])
        x1 = x1_ref[...]
        r2 = _rstd(x1)
        xh = x1 * r2
        dgffn_ref[...] += _colsum(dh2 * xh)
        dx1 = dx2_ref[...] + _norm_bwd(dh2, xh, r2, gffn_ref[...])
        dx1_ref[...] = dx1
        dy = _mm_nt(dx1.astype(BF), wout_ref[...])
        dycn_ref[...] = dy[:, 0:CONV_W]
        dyan = dy[:, CONV_W:D_MODEL]
        yat = ya_ref[...]
        ra = _rstd(yat)
        yah = yat * ra
        dgoa_ref[...] += _colsum(dyan * yah)
        dya_ref[...] = _norm_bwd(dyan, yah, ra, goa_ref[...])

    return pl.pallas_call(
        body, name="bwd_ffn_b", grid=(nt,),
        in_specs=[_rows(tm, D_FF), _rows(tm, D_FF), _rows(tm, D_MODEL), _rows(tm, D_MODEL),
                  _rows(tm, ATTN_W), _const((D_FF, D_MODEL)), _const((D_FF, D_MODEL)),
                  _const((D_MODEL, D_MODEL)), _const((1, D_MODEL)), _const((1, ATTN_W))],
        out_specs=[_rows(tm, D_MODEL), _rows(tm, CONV_W), _rows(tm, ATTN_W),
                   pl.BlockSpec((1, D_MODEL), lambda i: (0, 0)), pl.BlockSpec((1, ATTN_W), lambda i: (0, 0))],
        out_shape=[jax.ShapeDtypeStruct((t, D_MODEL), F32), jax.ShapeDtypeStruct((t, CONV_W), F32),
                   jax.ShapeDtypeStruct((t, ATTN_W), F32),
                   jax.ShapeDtypeStruct((1, D_MODEL), F32), jax.ShapeDtypeStruct((1, ATTN_W), F32)],
        compiler_params=_params(48),
    )(dgp, dup, dx2, x1, ya, wg4, wu4, wout, g_ffn, g_oa)


def _bwd_mix(x, zbcx, qkv, dycn, dq, dk, dv, conv_w, conv_b, g_oc, g_mix, gm, gq8, gk8, tm):
    t = x.shape[0]
    nt = t // tm

    def body(x_ref, z_ref, zh_ref, qkv_ref, dycn_ref, dq_ref, dk_ref, dv_ref, cw_ref, cb_ref,
             goc_ref, g_ref, gm_ref, gq_ref, gk_ref,
             dz_ref, dcw_ref, dcb_ref, dgoc_ref, dgq_ref, dgk_ref, gw32_ref, gw16_ref,
             ubuf, dbuf, wacc, wstage, osem):
        i = pl.program_id(0)

        @pl.when(i == 0)
        def _():
            wacc[...] = jnp.zeros_like(wacc)
            dbuf[tm:tm + 8, :] = jnp.zeros((8, CONV_W), F32)
            dcw_ref[...] = jnp.zeros_like(dcw_ref)
            dcb_ref[...] = jnp.zeros_like(dcb_ref)
            dgoc_ref[...] = jnp.zeros_like(dgoc_ref)
            dgq_ref[...] = jnp.zeros_like(dgq_ref)
            dgk_ref[...] = jnp.zeros_like(dgk_ref)

        not_first_tile = i < nt - 1
        zb = z_ref[:, 0:512]
        zc = z_ref[:, 512:1024]
        zx = z_ref[:, 1024:1536]
        u = zc * zx
        ubuf[0:8, :] = jnp.where(not_first_tile, zh_ref[:, 512:1024] * zh_ref[:, 1024:1536], 0.0)
        ubuf[8:8 + tm, :] = u
        ub = ubuf[...]
        u1 = _shift_down(ub, 1, tm)
        u2 = _shift_down(ub, 2, tm)
        w0, w1, w2 = cw_ref[0:1, :], cw_ref[1:2, :], cw_ref[2:3, :]
        cv = w0 * u2 + w1 * u1 + w2 * u + cb_ref[...]
        yc = zb * cv
        rc = _rstd(yc)
        ych = yc * rc
        dycn = dycn_ref[...]
        dgoc_ref[...] += _colsum(dycn * ych)
        dyc = _norm_bwd(dycn, ych, rc, goc_ref[...])
        dcv = dyc * zb
        dcb_ref[...] += _colsum(dcv)
        dcw_ref[0:1, :] += _colsum(dcv * u2)
        dcw_ref[1:2, :] += _colsum(dcv * u1)
        dcw_ref[2:3, :] += _colsum(dcv * u)
        dbuf[0:tm, :] = dcv
        db = dbuf[...]
        du = w2 * dcv + w1 * _shift_up(db, 1, tm) + w0 * _shift_up(db, 2, tm)
        dbuf[tm:tm + 8, :] = dbuf[0:8, :]
        dz_ref[:, 0:512] = (dyc * cv).astype(BF)
        dz_ref[:, 512:1024] = (du * zx).astype(BF)
        dz_ref[:, 1024:1536] = (du * zc).astype(BF)
        for z0, d_ref, gg_ref, acc_ref, sc in ((0, dq_ref, gq_ref, dgq_ref, HEAD_DIM ** -0.5),
                                               (512, dk_ref, gk_ref, dgk_ref, 1.0)):
            z = qkv_ref[:, z0:z0 + 512]
            rr = lax.rsqrt(_head_mean(z * z, gm_ref) + EPS)
            zh = z * rr
            dn = d_ref[...] * sc
            acc_ref[...] += _colsum(dn * zh)
            dzh = dn * gg_ref[...]
            dz_ref[:, 1536 + z0:1536 + z0 + 512] = (rr * (dzh - zh * _head_mean(dzh * zh, gm_ref))).astype(BF)
        dz_ref[:, 2560:3072] = dv_ref[...].astype(BF)
        xt = x_ref[...]
        h1 = ((xt * _rstd(xt)) * g_ref[...]).astype(BF)
        for s in range(N_CHIPS):
            cols = slice(s * IN_SLAB, (s + 1) * IN_SLAB)
            wacc[:, cols] += _mm_tn(h1, dz_ref[:, cols])

        @pl.when(i == nt - 1)
        def _():
            wstage[...] = wacc[...].astype(BF)
            out32 = pltpu.make_async_copy(wacc, gw32_ref, osem.at[0])
            out16 = pltpu.make_async_copy(wstage, gw16_ref, osem.at[1])
            out32.start()
            out16.start()
            out32.wait()
            out16.wait()

    def acc(width, rows=1):
        return pl.BlockSpec((rows, width), lambda i: (0, 0))

    return pl.pallas_call(
        body, name="bwd_mix", grid=(nt,),
        in_specs=[_rows(tm, D_MODEL, nt), _rows(tm, 1536, nt), _halo(tm, 1536, nt),
                  _rows(tm, 1536, nt), _rows(tm, CONV_W, nt), _rows(tm, ATTN_W, nt), _rows(tm, ATTN_W, nt),
                  _rows(tm, ATTN_W, nt),
                  _const((3, CONV_W)), _const((1, CONV_W)), _const((1, CONV_W)), _const((1, D_MODEL)),
                  _const((ATTN_W, ATTN_W)), _const((1, ATTN_W)), _const((1, ATTN_W))],
        out_specs=[_rows(tm, 3072, nt),
                   acc(CONV_W, 3), acc(CONV_W), acc(CONV_W), acc(ATTN_W), acc(ATTN_W),
                   pl.BlockSpec(memory_space=pltpu.HBM), pl.BlockSpec(memory_space=pltpu.HBM)],
        out_shape=[jax.ShapeDtypeStruct((t, 3072), BF), jax.ShapeDtypeStruct((3, CONV_W), F32),
                   jax.ShapeDtypeStruct((1, CONV_W), F32), jax.ShapeDtypeStruct((1, CONV_W), F32),
                   jax.ShapeDtypeStruct((1, ATTN_W), F32), jax.ShapeDtypeStruct((1, ATTN_W), F32),
                   jax.ShapeDtypeStruct((D_MODEL, 3072), F32), jax.ShapeDtypeStruct((D_MODEL, 3072), BF)],
        scratch_shapes=[pltpu.VMEM((tm + 8, CONV_W), F32), pltpu.VMEM((tm + 8, CONV_W), F32),
                        pltpu.VMEM((D_MODEL, 3072), F32), pltpu.VMEM((D_MODEL, 3072), BF),
                        pltpu.SemaphoreType.DMA((2,))],
        compiler_params=_params(56),
    )(x, zbcx, zbcx, qkv, dycn, dq, dk, dv, conv_w, conv_b, g_oc, g_mix, gm, gq8, gk8)


def _bwd_in(x, dx1, dz, win4, g_mix, tm, riders=()):
    t = x.shape[0]
    nt = t // tm
    nr = len(riders)

    def body(*refs):
        x_ref, dx1_ref, dz_ref, w_ref, g_ref = refs[0:5]
        gx_ref, dg_ref = refs[5 + nr:7 + nr]
        i = pl.program_id(0)
        if nr:
            _ride_scatter(i == 0, i == nt - 1, riders, refs[5:5 + nr], refs[7 + nr:7 + 2 * nr],
                          *refs[7 + 2 * nr:9 + 2 * nr])

        @pl.when(i == 0)
        def _():
            dg_ref[...] = jnp.zeros_like(dg_ref)

        dh1 = jnp.zeros((tm, D_MODEL), F32)
        for s in range(N_CHIPS):
            dh1 = dh1 + _mm_nt(dz_ref[:, s * IN_SLAB:(s + 1) * IN_SLAB], w_ref[s])
        xt = x_ref[...]
        r1 = _rstd(xt)
        xh = xt * r1
        dg_ref[...] += _colsum(dh1 * xh)
        gx_ref[...] = dx1_ref[...] + _norm_bwd(dh1, xh, r1, g_ref[...])

    r_in, r_out, r_sems = _rider_specs(riders)
    res = pl.pallas_call(
        body, name="bwd_in", grid=(nt,),
        in_specs=[_rows(tm, D_MODEL), _rows(tm, D_MODEL), _rows(tm, 3072), _const((N_CHIPS, D_MODEL, IN_SLAB)),
                  _const((1, D_MODEL))] + r_in,
        out_specs=[_rows(tm, D_MODEL), pl.BlockSpec((1, D_MODEL), lambda i: (0, 0))] + r_in,
        out_shape=[jax.ShapeDtypeStruct((t, D_MODEL), F32), jax.ShapeDtypeStruct((1, D_MODEL), F32)] + r_out,
        scratch_shapes=r_sems,
        compiler_params=_params(48),
    )(x, dx1, dz, win4, g_mix, *[g for g, _ in riders])
    return res[0], res[1], list(res[2:])


def _wgrad(a, b, tn, tt, name):
    t, k = a.shape
    n = b.shape[1]
    nt = t // tt

    def body(a_ref, b_ref, o_ref, ob_ref):
        @pl.when(pl.program_id(1) == 0)
        def _():
            o_ref[...] = jnp.zeros_like(o_ref)

        o_ref[...] += _mm_tn(a_ref[...].astype(BF), b_ref[...].astype(BF))

        @pl.when(pl.program_id(1) == nt - 1)
        def _():
            ob_ref[...] = o_ref[...].astype(BF)

    spec = pl.BlockSpec((k, tn), lambda j, i: (0, j))
    return pl.pallas_call(
        body, name=name, grid=(n // tn, nt),
        in_specs=[pl.BlockSpec((tt, k), lambda j, i: (i, 0)), pl.BlockSpec((tt, tn), lambda j, i: (i, j))],
        out_specs=[spec, spec],
        out_shape=[jax.ShapeDtypeStruct((k, n), F32), jax.ShapeDtypeStruct((k, n), BF)],
        compiler_params=_params(58, 2),
    )(a, b)


def _gather_weights(shards, pack):
    nw = len(shards)

    def body(*refs):
        ins = refs[:nw]
        pack_ref = refs[nw]
        outs = refs[nw + 1:2 * nw + 1]
        pack_out = refs[2 * nw + 1]
        send_sems, recv_sems, local_sems = refs[2 * nw + 2:]
        x, y, c = _place()
        me = 2 * x + y
        local, remote = [], []

        def sem(w, j):
            return w * 6 + j

        def push(src, dst, w, j, to):
            return pltpu.make_async_remote_copy(src_ref=src, dst_ref=dst, send_sem=send_sems.at[sem(w, j)],
                                                recv_sem=recv_sems.at[sem(w, j)], device_id=to, device_id_type=MESH)

        def half_rows(w, h):
            half = ins[w].shape[0] // 2
            return pl.ds(pl.multiple_of(h * half, 16), half)

        for w in range(nw):
            local.append(pltpu.make_async_copy(ins[w], outs[w].at[me], local_sems.at[w]))
            for k in (1, 2, 3):
                px, py = _chip_peer(x, y, k)
                mine = half_rows(w, c)
                remote.append(push(ins[w].at[mine], outs[w].at[me, mine], w, k - 1, (px, py, c)))
        local.append(pltpu.make_async_copy(pack_ref, pack_out.at[me], local_sems.at[nw]))
        for k in (1, 2, 3):
            px, py = _chip_peer(x, y, k)
            remote.append(push(pack_ref, pack_out.at[me], nw, k - 1, (px, py, c)))
        for cp in local + remote:
            cp.start()
        for w in range(nw):
            for k in (1, 2, 3):
                landed = outs[w].at[me ^ k, half_rows(w, c)]
                push(landed, landed, w, k - 1, (x, y, c)).wait_recv()
                fw = push(landed, landed, w, 2 + k, (x, y, 1 - c))
                fw.start()
                remote.append(fw)
        for k in (1, 2, 3):
            landed = pack_out.at[me ^ k]
            push(landed, landed, nw, k - 1, (x, y, c)).wait_recv()
        for w in range(nw):
            for k in (1, 2, 3):
                landed = outs[w].at[me ^ k, half_rows(w, 1 - c)]
                push(landed, landed, w, 2 + k, (x, y, c)).wait_recv()
        for cp in remote:
            cp.wait_send()
        for cp in local:
            cp.wait()

    any_spec = pl.BlockSpec(memory_space=pltpu.HBM)
    out_shape = [jax.ShapeDtypeStruct((N_CHIPS,) + s.shape, s.dtype) for s in shards]
    out_shape.append(jax.ShapeDtypeStruct((N_CHIPS,) + pack.shape, pack.dtype))
    return pl.pallas_call(
        body, name="gather_weights",
        in_specs=[any_spec] * (nw + 1), out_specs=[any_spec] * (nw + 1), out_shape=out_shape,
        scratch_shapes=[pltpu.SemaphoreType.DMA(((nw + 1) * 6,)), pltpu.SemaphoreType.DMA(((nw + 1) * 6,)),
                        pltpu.SemaphoreType.DMA((nw + 1,))],
    )(*shards, pack)


def _adamw(w, g, m, v):
    m = ADAM_B1 * m + (1.0 - ADAM_B1) * g
    v = ADAM_B2 * v + (1.0 - ADAM_B2) * (g * g)
    m_hat = m / (1.0 - ADAM_B1 ** ADAM_STEP)
    v_hat = v / (1.0 - ADAM_B2 ** ADAM_STEP)
    delta = -ADAM_LR * (m_hat / (jnp.sqrt(v_hat) + ADAM_EPS) + ADAM_WD * w)
    return delta, m, v


def _finish_reduce(grad, slots, col_sharded, name):
    r, cw = _piece_shape(grad.shape, col_sharded)
    chunk = 32
    assert r % chunk == 0

    def body(g_hbm, slots_ref, full, own, lsem, c_send, c_recv):
        x, y, c = _place()
        cp = pltpu.make_async_copy(g_hbm.at[_piece_window(col_sharded, r, cw, 2 * x + y, c)], own, lsem)
        cp.start()
        cp.wait()
        mine = pl.multiple_of(c * r, 8)

        def add(j, carry):
            rows = pl.ds(pl.multiple_of(j * chunk, 8), chunk)
            tot = own[rows, :]
            for k in range(7):
                tot = tot + slots_ref[k, rows, :].astype(F32)
            full[pl.ds(mine + pl.multiple_of(j * chunk, 8), chunk), :] = tot
            return carry

        lax.fori_loop(0, r // chunk, add, 0)
        half = full.at[pl.ds(mine, r), :]
        swap = pltpu.make_async_remote_copy(src_ref=half, dst_ref=half, send_sem=c_send, recv_sem=c_recv,
                                            device_id=(x, y, 1 - c), device_id_type=MESH)
        swap.start()
        swap.wait()

    vmem = pl.BlockSpec(memory_space=pltpu.VMEM)
    return pl.pallas_call(
        body, name=name, in_specs=[pl.BlockSpec(memory_space=pltpu.HBM), vmem], out_specs=vmem,
        out_shape=jax.ShapeDtypeStruct((2 * r, cw), F32),
        scratch_shapes=[pltpu.VMEM((r, cw), F32), pltpu.SemaphoreType.DMA, pltpu.SemaphoreType.DMA,
                        pltpu.SemaphoreType.DMA],
        compiler_params=pltpu.CompilerParams(vmem_limit_bytes=32 * MIB),
    )(grad, slots)


def _adamw_big(g, w, m, v, name):
    vr, vc = w.shape
    assert g.shape == w.shape
    rows = 64

    def body(g_ref, w_ref, m_ref, v_ref, go_ref, do_ref, mo_ref, vo_ref):
        gg = g_ref[...]
        delta, mn, vn = _adamw(w_ref[...], gg, m_ref[...], v_ref[...])
        go_ref[...] = gg
        do_ref[...] = delta
        mo_ref[...] = mn
        vo_ref[...] = vn

    blk = pl.BlockSpec((rows, vc), lambda i: (i, 0))
    shard = jax.ShapeDtypeStruct((vr, vc), F32)
    return pl.pallas_call(
        body, name=name, grid=(vr // rows,),
        in_specs=[blk, blk, blk, blk], out_specs=[blk] * 4,
        out_shape=[shard] * 4, compiler_params=_params(32),
    )(g, w, m, v)


def _allreduce_small(pack):
    rows = pack.shape[0]

    def body(p_ref, o_ref, slots, send_sems, recv_sems):
        x, y, c = _place()
        me = 4 * x + 2 * y + c
        slots[me] = p_ref[...]
        sends = []
        for k in range(1, 8):
            cp = pltpu.make_async_remote_copy(
                src_ref=p_ref, dst_ref=slots.at[me], send_sem=send_sems.at[k - 1], recv_sem=recv_sems.at[k - 1],
                device_id=(x ^ (k >> 2), y ^ ((k >> 1) & 1), c ^ (k & 1)), device_id_type=MESH)
            cp.start()
            sends.append(cp)
        for cp in sends:
            cp.wait()
        tot = slots[0]
        for j in range(1, 8):
            tot = tot + slots[j]
        o_ref[...] = tot

    vmem = pl.BlockSpec(memory_space=pltpu.VMEM)
    return pl.pallas_call(
        body, name="allreduce_small", in_specs=[vmem], out_specs=vmem,
        out_shape=jax.ShapeDtypeStruct(pack.shape, F32),
        scratch_shapes=[pltpu.VMEM((8, rows, D_MODEL), F32), pltpu.SemaphoreType.DMA((7,)),
                        pltpu.SemaphoreType.DMA((7,))],
    )(pack)


def _adamw_small(ws, gs, ms, vs):
    n = len(ws)

    def body(*refs):
        w_refs, g_refs, m_refs, v_refs = refs[0:n], refs[n:2 * n], refs[2 * n:3 * n], refs[3 * n:4 * n]
        d_refs, mo_refs, vo_refs = refs[4 * n:5 * n], refs[5 * n:6 * n], refs[6 * n:7 * n]
        for j in range(n):
            delta, mn, vn = _adamw(w_refs[j][...], g_refs[j][...], m_refs[j][...], v_refs[j][...])
            d_refs[j][...] = delta
            mo_refs[j][...] = mn
            vo_refs[j][...] = vn

    vmem = pl.BlockSpec(memory_space=pltpu.VMEM)
    shapes = [jax.ShapeDtypeStruct(w.shape, F32) for w in ws]
    outs = pl.pallas_call(
        body, name="adamw_small", in_specs=[vmem] * (4 * n), out_specs=[vmem] * (3 * n), out_shape=shapes * 3,
    )(*ws, *gs, *ms, *vs)
    return outs[0:n], outs[n:2 * n], outs[2 * n:3 * n]


def _local_step(x, p, target, wts, late=None):
    (win4, wout, wg4, wu4, wd4, wpg, wpp4, conv_w, fcw, g_mix, conv_b, gq, gk, g_oc, g_oa, g_ffn, fcb, g_ple) = wts
    comm = late is not None
    gm = jnp.kron(jnp.eye(N_HEADS, dtype=F32), jnp.full((HEAD_DIM, HEAD_DIM), 1.0 / HEAD_DIM, F32)).astype(BF)
    gq8, gk8 = jnp.tile(gq, (1, N_HEADS)), jnp.tile(gk, (1, N_HEADS))
    mb = _mask_table()
    zbcx, qkv, ycn, qkn = _fwd_mix(x, g_mix, win4, conv_w, conv_b, g_oc, gm, gq8, gk8, 512)
    ya, lse, gathered = _attn_fwd(qkn, qkv, mb, late[0:3] if comm else ())
    if comm:
        wout, wg4, wu4 = (g.reshape(-1, D_MODEL) for g in gathered)
    x1, gp, up, gate, act, ycat, h2, gathered = _fwd_ffn(x, ycn, ya, wout, wg4, wu4, g_oa, g_ffn, fcw, fcb, 256,
                                                    late[3:6] if comm else ())
    if comm:
        wd4, wpg, wpp4 = gathered
        wd4, wpg = wd4.reshape(D_FF, D_MODEL), wpg.reshape(D_MODEL, D_MODEL)
    dx2, h3, ds, dpp, dg_ple, loss = _fwd_tail(x1, act, p, target, wd4, wpg, wpp4, g_ple, 512)
    big, big16, slots = {}, {}, {}

    def wgrad(name, a, b, tn):
        big[name], big16[name] = _wgrad(a, b, tn, 1024, "wgrad_" + name)
        return (big16[name], _COL_SHARDED[name])

    riders = [wgrad("w_down", act, dx2, 1024), wgrad("w_ple_gate", h3, ds, 1024), wgrad("w_ple_proj", p, dpp, 1024)]
    dgp, dup, dfcw, dfcb, got = _bwd_ffn_a(dx2, gate, gp, up, wd4, fcw, 512, riders if comm else ())
    slots.update(zip(("w_down", "w_ple_gate", "w_ple_proj"), got))
    riders = [wgrad("w_gate", dgp, h2, 1024), wgrad("w_up", dup, h2, 1024)]
    dx1, dycn, dya, dg_ffn, dg_oa = _bwd_ffn_b(dgp, dup, dx2, x1, ya, wg4, wu4, wout, g_ffn, g_oa, 512)
    riders.append(wgrad("w_out", ycat, dx1, 1024))
    dq, dk, dv, got = _attn_bwd(qkn, qkv, ya, lse, dya, mb, riders if comm else ())
    slots.update(zip(("w_gate", "w_up", "w_out"), got))
    dz, dcw, dcb, dg_oc, dgq8, dgk8, big["w_in"], big16["w_in"] = _bwd_mix(
        x, zbcx, qkv, dycn, dq, dk, dv, conv_w, conv_b, g_oc, g_mix, gm, gq8, gk8, 512)
    riders = [(big16["w_in"], _COL_SHARDED["w_in"])]
    grad_x, dg_mix, got = _bwd_in(x, dx1, dz, win4, g_mix, 512, riders if comm else ())
    slots.update(zip(("w_in",), got))
    dgq = dgq8.reshape(N_HEADS, HEAD_DIM).sum(axis=0, keepdims=True)
    dgk = dgk8.reshape(N_HEADS, HEAD_DIM).sum(axis=0, keepdims=True)
    small = dict(g_mix=dg_mix, conv_w=dcw, conv_b=dcb, q_norm_g=dgq, k_norm_g=dgk, g_out_conv=dg_oc,
                 g_out_attn=dg_oa, g_ffn=dg_ffn, ffn_conv_w=dfcw, ffn_conv_b=dfcb, g_ple=dg_ple)
    return loss[0, 0], grad_x, big, slots, small


_SMALL_ROWS = 24


def _pack_small(s, loss):
    z64 = jnp.zeros((1, 1024 - 512 - 128), F32)
    rows = [s["g_mix"], s["g_ffn"], s["g_ple"],
            jnp.concatenate([s["conv_b"], s["g_out_conv"]], axis=1),
            jnp.concatenate([s["g_out_attn"], s["q_norm_g"], s["k_norm_g"], z64], axis=1),
            jnp.pad(s["conv_w"], ((0, 0), (0, 512))),
            jnp.pad(s["ffn_conv_b"], ((0, 0), (0, 3072 - D_FF))).reshape(3, 1024),
            jnp.pad(s["ffn_conv_w"], ((0, 0), (0, 3072 - D_FF))).reshape(9, 1024),
            jnp.pad(loss.reshape(1, 1), ((0, 0), (0, 1023))),
            jnp.zeros((_SMALL_ROWS - 21, 1024), F32)]
    return jnp.concatenate(rows, axis=0)


def _unpack_small(t):
    return dict(g_mix=t[0:1], g_ffn=t[1:2], g_ple=t[2:3], conv_b=t[3:4, 0:512], g_out_conv=t[3:4, 512:1024],
                g_out_attn=t[4:5, 0:512], q_norm_g=t[4:5, 512:576], k_norm_g=t[4:5, 576:640],
                conv_w=t[5:8, 0:512], ffn_conv_b=t[8:11].reshape(1, 3072)[:, :D_FF],
                ffn_conv_w=t[11:20].reshape(3, 3072)[:, :D_FF], loss=t[20, 0])


_BIG = ("w_in", "w_out", "w_gate", "w_up", "w_down", "w_ple_gate", "w_ple_proj")
_COL_SHARDED = dict(w_in=True, w_out=False, w_gate=False, w_up=False, w_down=False, w_ple_gate=False, w_ple_proj=True)
_TRANSPOSED = ("w_gate", "w_up")
_WEIGHTS = ("g_mix", "w_in", "conv_w", "conv_b", "q_norm_g", "k_norm_g", "g_out_conv", "g_out_attn", "w_out",
            "g_ffn", "w_gate", "w_up", "ffn_conv_w", "ffn_conv_b", "w_down", "g_ple", "w_ple_gate", "w_ple_proj")


def kernel(x, p, g_mix, w_in, conv_w, conv_b, q_norm_g, k_norm_g, g_out_conv, g_out_attn, w_out, g_ffn, w_gate, w_up, ffn_conv_w, ffn_conv_b, w_down, g_ple, w_ple_gate, w_ple_proj, loss_target, m_g_mix, m_w_in, m_conv_w, m_conv_b, m_q_norm_g, m_k_norm_g, m_g_out_conv, m_g_out_attn, m_w_out, m_g_ffn, m_w_gate, m_w_up, m_ffn_conv_w, m_ffn_conv_b, m_w_down, m_g_ple, m_w_ple_gate, m_w_ple_proj, v_g_mix, v_w_in, v_conv_w, v_conv_b, v_q_norm_g, v_k_norm_g, v_g_out_conv, v_g_out_attn, v_w_out, v_g_ffn, v_w_gate, v_w_up, v_ffn_conv_w, v_ffn_conv_b, v_w_down, v_g_ple, v_w_ple_gate, v_w_ple_proj):
    w = dict(g_mix=g_mix, w_in=w_in, conv_w=conv_w, conv_b=conv_b, q_norm_g=q_norm_g, k_norm_g=k_norm_g,
             g_out_conv=g_out_conv, g_out_attn=g_out_attn, w_out=w_out, g_ffn=g_ffn, w_gate=w_gate, w_up=w_up,
             ffn_conv_w=ffn_conv_w, ffn_conv_b=ffn_conv_b, w_down=w_down, g_ple=g_ple, w_ple_gate=w_ple_gate,
             w_ple_proj=w_ple_proj)
    m = dict(g_mix=m_g_mix, w_in=m_w_in, conv_w=m_conv_w, conv_b=m_conv_b, q_norm_g=m_q_norm_g, k_norm_g=m_k_norm_g,
             g_out_conv=m_g_out_conv, g_out_attn=m_g_out_attn, w_out=m_w_out, g_ffn=m_g_ffn, w_gate=m_w_gate,
             w_up=m_w_up, ffn_conv_w=m_ffn_conv_w, ffn_conv_b=m_ffn_conv_b, w_down=m_w_down, g_ple=m_g_ple,
             w_ple_gate=m_w_ple_gate, w_ple_proj=m_w_ple_proj)
    v = dict(g_mix=v_g_mix, w_in=v_w_in, conv_w=v_conv_w, conv_b=v_conv_b, q_norm_g=v_q_norm_g, k_norm_g=v_k_norm_g,
             g_out_conv=v_g_out_conv, g_out_attn=v_g_out_attn, w_out=v_w_out, g_ffn=v_g_ffn, w_gate=v_w_gate,
             w_up=v_w_up, ffn_conv_w=v_ffn_conv_w, ffn_conv_b=v_ffn_conv_b, w_down=v_w_down, g_ple=v_g_ple,
             w_ple_gate=v_w_ple_gate, w_ple_proj=v_w_ple_proj)
    mats = [k for k, a in w.items() if a.ndim == 3]
    w = {k: (a[0] if k in mats else a) for k, a in w.items()}
    m = {k: (a[0] if k in mats else a) for k, a in m.items()}
    v = {k: (a[0] if k in mats else a) for k, a in v.items()}
    for n in _TRANSPOSED:
        w[n], m[n], v[n] = w[n].T, m[n].T, v[n].T
    chip = 2 * lax.axis_index("x") + lax.axis_index("y")

    late = [w[n].astype(BF) for n in ("w_out", "w_gate", "w_up", "w_down", "w_ple_gate", "w_ple_proj")]
    pack = jnp.pad(jnp.concatenate([w["conv_w"], w["ffn_conv_w"]], axis=1), ((0, 5), (0, 1024 - 128 - D_FF_SHARD)))
    win4, pack4 = _gather_weights([w["w_in"].astype(BF)], pack)
    conv_w_full = pack4[:, 0:3, 0:128].transpose(1, 0, 2).reshape(3, CONV_W)
    fcw_full = pack4[:, 0:3, 128:128 + D_FF_SHARD].transpose(1, 0, 2).reshape(3, D_FF)
    wts = (win4, None, None, None, None, None, None, conv_w_full, fcw_full, w["g_mix"], w["conv_b"], w["q_norm_g"],
           w["k_norm_g"], w["g_out_conv"], w["g_out_attn"], w["g_ffn"], w["ffn_conv_b"], w["g_ple"])

    loss, grad_x, big, slots, small = _local_step(x[0], p[0, 0], loss_target[0], wts, late)

    grads, deltas, new_m, new_v = {}, {}, {}, {}
    for name in _BIG:
        total = _finish_reduce(big[name], slots[name], _COL_SHARDED[name], "finish_" + name)
        grads[name], deltas[name], new_m[name], new_v[name] = _adamw_big(total, w[name], m[name], v[name],
                                                                         "adamw_" + name)
    tot = _unpack_small(_allreduce_small(_pack_small(small, loss)))
    loss = tot.pop("loss")
    tot["conv_w"] = lax.dynamic_slice_in_dim(tot["conv_w"], chip * 128, 128, axis=1)
    tot["ffn_conv_w"] = lax.dynamic_slice_in_dim(tot["ffn_conv_w"], chip * D_FF_SHARD, D_FF_SHARD, axis=1)
    names = [n for n in _WEIGHTS if n not in _BIG]
    d_s, m_s, v_s = _adamw_small([w[n] for n in names], [tot[n] for n in names], [m[n] for n in names],
                                 [v[n] for n in names])
    for j, n in enumerate(names):
        grads[n], deltas[n], new_m[n], new_v[n] = tot[n], d_s[j], m_s[j], v_s[j]

    out = [loss, grad_x[None]]
    for group in (grads, deltas, new_m, new_v):
        for n in _TRANSPOSED:
            group[n] = group[n].T
        out += [group[n][None] if n in mats else group[n] for n in _WEIGHTS]
    return tuple(out)
```

```python
import jax
import jax.numpy as jnp
from jax import lax
from jax.experimental import pallas as pl
from jax.experimental.pallas import tpu as pltpu

D_MODEL = 1024
CONV_W = 512
N_HEADS = 8
HEAD_DIM = 64
ATTN_W = 512
D_FF = 2816
D_FF_SHARD = 704
FF_SLABS = ((0, 1408), (1408, 2816))
IN_SLAB = 768
PLE_DIM = 256
N_CHIPS = 4
QBLK = 128
DILATIONS = (1, 4, 16)
EPS = 1e-6
NEG = -1e30
MESH = pl.DeviceIdType.MESH

ADAM_LR = 0.001
ADAM_B1 = 0.9
ADAM_B2 = 0.999
ADAM_EPS = 1e-08
ADAM_WD = 0.01
ADAM_STEP = 10

BF = jnp.bfloat16
F32 = jnp.float32
MIB = 1024 * 1024


def _mm(a, b):
    return jnp.dot(a, b, preferred_element_type=F32)


def _mm_nt(a, b):
    return lax.dot_general(a, b, (((1,), (1,)), ((), ())), preferred_element_type=F32)


def _mm_tn(a, b):
    return lax.dot_general(a, b, (((0,), (0,)), ((), ())), preferred_element_type=F32)


def _rstd(a):
    return lax.rsqrt(jnp.mean(a * a, axis=-1, keepdims=True) + EPS)


def _norm_bwd(dy, xh, r, g):
    dxh = dy * g
    return r * (dxh - xh * jnp.mean(dxh * xh, axis=-1, keepdims=True))


def _colsum(a):
    return jnp.sum(a, axis=0, keepdims=True)


def _head_mean(a, gm_ref):
    return _mm(a.astype(BF), gm_ref[...])


def _shift_down(buf, k, tm):
    return pltpu.roll(buf, k, axis=0)[8:8 + tm]


def _shift_up(buf, k, tm):
    return pltpu.roll(buf, tm + 8 - k, axis=0)[0:tm]


def _params(vmem_mib, n_grid=1):
    return pltpu.CompilerParams(dimension_semantics=("arbitrary",) * n_grid, vmem_limit_bytes=vmem_mib * MIB)


def _const(shape):
    n = len(shape)
    return pl.BlockSpec(shape, lambda *_: (0,) * n, pipeline_mode=pl.Buffered(1))


def _rows(tm, width, rev_of=None):
    if rev_of is None:
        return pl.BlockSpec((tm, width), lambda i: (i, 0))
    return pl.BlockSpec((tm, width), lambda i: (rev_of - 1 - i, 0))


def _halo(tm, width, nt):
    return pl.BlockSpec((8, width), lambda i: (jnp.maximum((nt - 1 - i) * (tm // 8) - 1, 0), 0))


def _fwd_mix(x, g_mix, win4, conv_w, conv_b, g_oc, gm, gq8, gk8, tm):
    t = x.shape[0]
    nt = t // tm

    def body(x_ref, g_ref, w_ref, cw_ref, cb_ref, goc_ref, gm_ref, gq_ref, gk_ref,
             zbcx_ref, qkv_ref, ycn_ref, qkn_ref, ubuf):
        @pl.when(pl.program_id(0) == 0)
        def _():
            ubuf[0:8, :] = jnp.zeros((8, CONV_W), F32)

        xt = x_ref[...]
        h = ((xt * _rstd(xt)) * g_ref[...]).astype(BF)
        zbcx_ref[:, 0:IN_SLAB] = _mm(h, w_ref[0])
        zbcx_ref[:, IN_SLAB:2 * IN_SLAB] = _mm(h, w_ref[1])
        qkv_ref[:, 0:IN_SLAB] = _mm(h, w_ref[2])
        qkv_ref[:, IN_SLAB:2 * IN_SLAB] = _mm(h, w_ref[3])
        u = zbcx_ref[:, 512:1024] * zbcx_ref[:, 1024:1536]
        ubuf[8:8 + tm, :] = u
        ub = ubuf[...]
        cv = (cw_ref[0:1, :] * _shift_down(ub, 2, tm) + cw_ref[1:2, :] * _shift_down(ub, 1, tm)
              + cw_ref[2:3, :] * u + cb_ref[...])
        ubuf[0:8, :] = ubuf[tm:tm + 8, :]
        yc = zbcx_ref[:, 0:512] * cv
        ycn_ref[...] = ((yc * _rstd(yc)) * goc_ref[...]).astype(BF)
        zq = qkv_ref[:, 0:512]
        zk = qkv_ref[:, 512:1024]
        rq = lax.rsqrt(_head_mean(zq * zq, gm_ref) + EPS)
        rk = lax.rsqrt(_head_mean(zk * zk, gm_ref) + EPS)
        qkn_ref[:, 0:512] = ((zq * rq) * gq_ref[...]) * (HEAD_DIM ** -0.5)
        qkn_ref[:, 512:1024] = (zk * rk) * gk_ref[...]

    return pl.pallas_call(
        body, name="fwd_mix", grid=(nt,),
        in_specs=[_rows(tm, D_MODEL), _const((1, D_MODEL)), _const((N_CHIPS, D_MODEL, IN_SLAB)),
                  _const((3, CONV_W)), _const((1, CONV_W)), _const((1, CONV_W)), _const((ATTN_W, ATTN_W)),
                  _const((1, ATTN_W)), _const((1, ATTN_W))],
        out_specs=[_rows(tm, 1536), _rows(tm, 1536), _rows(tm, CONV_W), _rows(tm, 1024)],
        out_shape=[jax.ShapeDtypeStruct((t, 1536), F32), jax.ShapeDtypeStruct((t, 1536), F32),
                   jax.ShapeDtypeStruct((t, CONV_W), BF), jax.ShapeDtypeStruct((t, 1024), F32)],
        scratch_shapes=[pltpu.VMEM((tm + 8, CONV_W), F32)],
        compiler_params=_params(48),
    )(x, g_mix, win4, conv_w, conv_b, g_oc, gm, gq8, gk8)


def _place():
    x, y, c = lax.axis_index("x"), lax.axis_index("y"), lax.axis_index("c")
    return x, y, c


def _chip_peer(x, y, k):
    return x ^ (k >> 1), y ^ (k & 1)


def _piece_shape(grad_shape, col_sharded):
    kk, nn = grad_shape
    return (kk // 2, nn // N_CHIPS) if col_sharded else (kk // (2 * N_CHIPS), nn)


def _piece_window(col_sharded, r, cw, s, h):
    if col_sharded:
        return (pl.ds(pl.multiple_of(h * r, 16), r), pl.ds(pl.multiple_of(s * cw, 128), cw))
    return (pl.ds(pl.multiple_of((2 * s + h) * r, 16), r), slice(None))


def _scatter_copies(g_ref, slots_ref, send_sems, recv_sems, base, col_sharded):
    x, y, c = _place()
    r, cw = slots_ref.shape[1:]
    copies = []
    for k in range(1, 8):
        tx, ty, tc = x ^ (k >> 2), y ^ ((k >> 1) & 1), c ^ (k & 1)
        copies.append(pltpu.make_async_remote_copy(
            src_ref=g_ref.at[_piece_window(col_sharded, r, cw, 2 * tx + ty, tc)], dst_ref=slots_ref.at[k - 1],
            send_sem=send_sems.at[base + k - 1], recv_sem=recv_sems.at[base + k - 1],
            device_id=(tx, ty, tc), device_id_type=MESH))
    return copies


def _ride_scatter(first, last, riders, g_refs, slot_refs, send_sems, recv_sems):
    def all_copies():
        out = []
        for j, (_, col_sharded) in enumerate(riders):
            out += _scatter_copies(g_refs[j], slot_refs[j], send_sems, recv_sems, 7 * j, col_sharded)
        return out

    @pl.when(first)
    def _():
        for cp in all_copies():
            cp.start()

    @pl.when(last)
    def _():
        for cp in all_copies():
            cp.wait()


def _rider_specs(riders):
    any_spec = pl.BlockSpec(memory_space=pl.ANY)
    shapes = [jax.ShapeDtypeStruct((7,) + _piece_shape(g.shape, cs), BF) for g, cs in riders]
    sems = [pltpu.SemaphoreType.DMA((7 * len(riders),)), pltpu.SemaphoreType.DMA((7 * len(riders),))] if riders else []
    return [any_spec] * len(riders), shapes, sems


class _Gather:
    def __init__(self, ins, outs, send_sems, recv_sems, local_sems):
        self.ins, self.outs = ins, outs
        self.send_sems, self.recv_sems, self.local_sems = send_sems, recv_sems, local_sems
        self.x, self.y, self.c = _place()
        self.me = 2 * self.x + self.y

    def _push(self, src, dst, w, j, to):
        return pltpu.make_async_remote_copy(src_ref=src, dst_ref=dst, send_sem=self.send_sems.at[6 * w + j],
                                            recv_sem=self.recv_sems.at[6 * w + j], device_id=to, device_id_type=MESH)

    def _half(self, w, h):
        half = self.ins[w].shape[0] // 2
        return pl.ds(pl.multiple_of(h * half, 16), half)

    def _local(self, w):
        return pltpu.make_async_copy(self.ins[w], self.outs[w].at[self.me], self.local_sems.at[w])

    def _ici(self, w, k):
        px, py = _chip_peer(self.x, self.y, k)
        mine = self._half(w, self.c)
        return self._push(self.ins[w].at[mine], self.outs[w].at[self.me, mine], w, k - 1, (px, py, self.c))

    def _landed(self, w, k, h):
        return self.outs[w].at[self.me ^ k, self._half(w, h)]

    def _fwd(self, w, k):
        landed = self._landed(w, k, self.c)
        return self._push(landed, landed, w, 2 + k, (self.x, self.y, 1 - self.c))

    def start(self):
        for w in range(len(self.ins)):
            self._local(w).start()
            for k in (1, 2, 3):
                self._ici(w, k).start()

    def forward(self):
        for w in range(len(self.ins)):
            for k in (1, 2, 3):
                landed = self._landed(w, k, self.c)
                self._push(landed, landed, w, k - 1, (self.x, self.y, self.c)).wait_recv()
                self._fwd(w, k).start()

    def finish(self):
        for w in range(len(self.ins)):
            for k in (1, 2, 3):
                landed = self._landed(w, k, 1 - self.c)
                self._push(landed, landed, w, 2 + k, (self.x, self.y, self.c)).wait_recv()
            for k in (1, 2, 3):
                self._ici(w, k).wait_send()
                self._fwd(w, k).wait_send()
            self._local(w).wait()


def _alibi(h):
    return 2.0 ** (-(h + 1))


CHUNK = 2048


def _mask_table():
    slopes = jnp.asarray([_alibi(h) for h in range(N_HEADS)], F32)[:, None, None]
    step = jnp.arange(QBLK)[:, None] + QBLK - jnp.arange(2 * QBLK)[None, :]
    valid = (step >= 0) & (step <= QBLK)
    tab = jnp.stack([jnp.where(valid[None], -slopes * (step * d)[None].astype(F32), NEG) for d in DILATIONS])
    return tab.reshape(3, N_HEADS // 2, 2 * QBLK, 2 * QBLK)


def _attn_fwd(qkn, qkv, mb, late=()):
    t = qkn.shape[0]
    nc = t // CHUNK
    nl = len(late)

    def body(*refs):
        qc_ref, kp_ref, kc_ref, vp_ref, vc_ref, mb_ref = refs[0:6]
        o_ref, l_ref = refs[6 + nl:8 + nl]
        ob0, ob1, ob2, lb0, lb1, lb2 = refs[8 + 2 * nl:14 + 2 * nl]
        if nl:
            gather = _Gather(refs[6:6 + nl], refs[8 + nl:8 + 2 * nl], *refs[14 + 2 * nl:17 + 2 * nl])
            step = pl.program_id(0) * nc + pl.program_id(1)
            pl.when(step == 0)(gather.start)
            pl.when(step == 2 * nc)(gather.forward)
            pl.when(step == (N_HEADS // 2) * nc - 1)(gather.finish)
        first = pl.program_id(1) == 0
        lane = lax.broadcasted_iota(jnp.int32, (QBLK, 128), 1)
        lo_half = lane < HEAD_DIM
        kj = lax.broadcasted_iota(jnp.int32, (2 * QBLK, 2 * QBLK), 1)
        no_prev = first & (kj < QBLK)
        obs, lbs = (ob0, ob1, ob2), (lb0, lb1, lb2)

        def by_head(a):
            return jnp.where(lo_half, a, 0.0).astype(BF), jnp.where(lo_half, 0.0, a).astype(BF)

        for di, d in enumerate(DILATIONS):
            span = d * QBLK
            for r in range(d):
                tail = pl.ds(CHUNK - span + r, QBLK, stride=d)
                k_prev = kp_ref[tail, :].astype(BF)
                v_prev = by_head(vp_ref[tail, :])
                for b in range(CHUNK // span):
                    rows = pl.ds(r + span * b, QBLK, stride=d)
                    q0, q1 = by_head(qc_ref[rows, :])
                    k_cur = kc_ref[rows, :].astype(BF)
                    v_cur = by_head(vc_ref[rows, :])
                    s = _mm_nt(jnp.concatenate([q0, q1], axis=0), jnp.concatenate([k_prev, k_cur], axis=0))
                    s = s + mb_ref[di, 0]
                    if b == 0:
                        s = jnp.where(no_prev, NEG, s)
                    m = jnp.max(s, axis=-1, keepdims=True)
                    e = jnp.exp(s - m)
                    den = jnp.sum(e, axis=-1, keepdims=True)
                    eb = e.astype(BF)
                    o = _mm(jnp.concatenate([eb[0:QBLK], eb[QBLK:2 * QBLK]], axis=1),
                            jnp.concatenate([v_prev[0], v_cur[0], v_prev[1], v_cur[1]], axis=0))
                    inv = 1.0 / den
                    lse = m + jnp.log(den)
                    obs[di][rows, :] = o * jnp.where(lo_half, inv[0:QBLK], inv[QBLK:2 * QBLK])
                    lbs[di][rows, :] = jnp.where(lo_half, lse[0:QBLK], lse[QBLK:2 * QBLK])
                    k_prev, v_prev = k_cur, v_cur
        for c0 in range(0, CHUNK, 256):
            rs = slice(c0, c0 + 256)
            l0, l1, l2 = lb0[rs, :], lb1[rs, :], lb2[rs, :]
            mx = jnp.maximum(jnp.maximum(l0, l1), l2)
            w0, w1, w2 = jnp.exp(l0 - mx), jnp.exp(l1 - mx), jnp.exp(l2 - mx)
            tot = w0 + w1 + w2
            o_ref[rs, :] = (ob0[rs, :] * w0 + ob1[rs, :] * w1 + ob2[rs, :] * w2) / tot
            l_ref[rs, :] = mx + jnp.log(tot)

    def cur(col):
        return pl.BlockSpec((CHUNK, 128), lambda hp, n: (n, col + hp))

    def prv(col):
        return pl.BlockSpec((CHUNK, 128), lambda hp, n: (jnp.maximum(n - 1, 0), col + hp))

    out = pl.BlockSpec((CHUNK, 128), lambda hp, n: (n, hp))
    any_spec = pl.BlockSpec(memory_space=pl.ANY)
    sems = [pltpu.SemaphoreType.DMA((6 * nl,)), pltpu.SemaphoreType.DMA((6 * nl,)), pltpu.SemaphoreType.DMA((nl,))]
    res = pl.pallas_call(
        body, name="attn_fwd", grid=(N_HEADS // 2, nc),
        in_specs=[cur(0), prv(4), cur(4), prv(8), cur(8),
                  pl.BlockSpec((3, 1, 2 * QBLK, 2 * QBLK), lambda hp, n: (0, hp, 0, 0))] + [any_spec] * nl,
        out_specs=[out, out] + [any_spec] * nl,
        out_shape=[jax.ShapeDtypeStruct((t, ATTN_W), F32)] * 2
        + [jax.ShapeDtypeStruct((N_CHIPS,) + w.shape, w.dtype) for w in late],
        scratch_shapes=[pltpu.VMEM((CHUNK, 128), F32)] * 6 + (sems if nl else []),
        compiler_params=_params(48, 2),
    )(qkn, qkn, qkn, qkv, qkv, mb, *late)
    return res[0], res[1], list(res[2:])


def _attn_bwd(qkn, qkv, o, lse, do, mb, riders=()):
    t = qkn.shape[0]
    nc = t // CHUNK
    nr = len(riders)

    def body(*refs):
        (qc_ref, qn_ref, kp_ref, kc_ref, vp_ref, vc_ref, oc_ref, on_ref, lc_ref, ln_ref, dc_ref, dn_ref,
         mb_ref) = refs[0:13]
        dq_ref, dk_ref, dv_ref = refs[13 + nr:16 + nr]
        if nr:
            step = pl.program_id(0) * nc + pl.program_id(1)
            _ride_scatter(step == 0, step == (N_HEADS // 2) * nc - 1, riders, refs[13:13 + nr],
                          refs[16 + nr:16 + 2 * nr], *refs[16 + 2 * nr:18 + 2 * nr])
        first = pl.program_id(1) == 0
        last = pl.program_id(1) == nc - 1
        lane = lax.broadcasted_iota(jnp.int32, (QBLK, 128), 1)
        lo_half = lane < HEAD_DIM
        kj = lax.broadcasted_iota(jnp.int32, (2 * QBLK, 2 * QBLK), 1)
        no_prev = first & (kj < QBLK)

        def by_head(a):
            return jnp.where(lo_half, a, 0.0).astype(BF), jnp.where(lo_half, 0.0, a).astype(BF)

        def query_side(q_ref, d_ref, o_ref_, l_ref_, rows):
            dvals = d_ref[rows, :]
            dd = dvals * o_ref_[rows, :]
            lv = l_ref_[rows, :]
            d0 = jnp.sum(jnp.where(lo_half, dd, 0.0), axis=-1, keepdims=True)
            d1 = jnp.sum(jnp.where(lo_half, 0.0, dd), axis=-1, keepdims=True)
            l0 = jnp.max(jnp.where(lo_half, lv, NEG), axis=-1, keepdims=True)
            l1 = jnp.max(jnp.where(lo_half, NEG, lv), axis=-1, keepdims=True)
            return (jnp.concatenate(by_head(q_ref[rows, :]), axis=0), jnp.concatenate(by_head(dvals), axis=0),
                    jnp.concatenate([l0, l1], axis=0), jnp.concatenate([d0, d1], axis=0))

        def tile(qs, dos, lcol, dcol, keys, vals, bias, dead):
            s = _mm_nt(qs, keys) + bias
            if dead is not None:
                s = jnp.where(dead, NEG, s)
            p = jnp.exp(s - lcol)
            ds = p * (_mm_nt(dos, vals) - dcol)
            return p.astype(BF), ds.astype(BF)

        def put(ref, di, rows, val):
            if di == 0:
                ref[rows, :] = val
            else:
                ref[rows, :] = ref[rows, :] + val

        for di, d in enumerate(DILATIONS):
            span = d * QBLK
            nbk = CHUNK // span
            for r in range(d):
                tail = pl.ds(CHUNK - span + r, QBLK, stride=d)
                k_prev = kp_ref[tail, :]
                kb_prev, km_prev = k_prev.astype(BF), by_head(k_prev)
                vb_prev = vp_ref[tail, :].astype(BF)
                rows_prev, dk_part, dv_part = None, None, None
                for b in range(nbk):
                    rows = pl.ds(r + span * b, QBLK, stride=d)
                    qs, dos, lcol, dcol = query_side(qc_ref, dc_ref, oc_ref, lc_ref, rows)
                    k_cur = kc_ref[rows, :]
                    kb_cur, km_cur = k_cur.astype(BF), by_head(k_cur)
                    vb_cur = vc_ref[rows, :].astype(BF)
                    p, ds = tile(qs, dos, lcol, dcol, jnp.concatenate([kb_prev, kb_cur], axis=0),
                                 jnp.concatenate([vb_prev, vb_cur], axis=0), mb_ref[di, 0],
                                 no_prev if b == 0 else None)
                    put(dq_ref, di, rows,
                        _mm(jnp.concatenate([ds[0:QBLK], ds[QBLK:2 * QBLK]], axis=1),
                            jnp.concatenate([km_prev[0], km_cur[0], km_prev[1], km_cur[1]], axis=0)))
                    dk2 = _mm_tn(ds, qs)
                    dv2 = _mm_tn(p, dos)
                    if b > 0:
                        put(dk_ref, di, rows_prev, dk_part + dk2[0:QBLK])
                        put(dv_ref, di, rows_prev, dv_part + dv2[0:QBLK])
                    rows_prev, dk_part, dv_part = rows, dk2[QBLK:2 * QBLK], dv2[QBLK:2 * QBLK]
                    kb_prev, km_prev, vb_prev = kb_cur, km_cur, vb_cur
                qs, dos, lcol, dcol = query_side(qn_ref, dn_ref, on_ref, ln_ref, pl.ds(r, QBLK, stride=d))
                p, ds = tile(qs, dos, lcol, dcol, kb_prev, vb_prev, mb_ref[di, 0, :, 0:QBLK], last)
                put(dk_ref, di, rows_prev, dk_part + _mm_tn(ds, qs))
                put(dv_ref, di, rows_prev, dv_part + _mm_tn(p, dos))

    def at(shift, col):
        return pl.BlockSpec((CHUNK, 128), lambda hp, n: (jnp.clip(n + shift, 0, nc - 1), col + hp))

    out = pl.BlockSpec((CHUNK, 128), lambda hp, n: (n, hp))
    r_in, r_out, r_sems = _rider_specs(riders)
    res = pl.pallas_call(
        body, name="attn_bwd", grid=(N_HEADS // 2, nc),
        in_specs=[at(0, 0), at(1, 0), at(-1, 4), at(0, 4), at(-1, 8), at(0, 8),
                  at(0, 0), at(1, 0), at(0, 0), at(1, 0), at(0, 0), at(1, 0),
                  pl.BlockSpec((3, 1, 2 * QBLK, 2 * QBLK), lambda hp, n: (0, hp, 0, 0))] + r_in,
        out_specs=[out, out, out] + r_in,
        out_shape=[jax.ShapeDtypeStruct((t, ATTN_W), F32)] * 3 + r_out,
        scratch_shapes=r_sems,
        compiler_params=_params(56, 2),
    )(qkn, qkn, qkn, qkn, qkv, qkv, o, o, lse, lse, do, do, mb, *[g for g, _ in riders])
    return res[0], res[1], res[2], list(res[3:])


def _fwd_ffn(x, ycn, ya, wout, wg4, wu4, g_oa, g_ffn, fcw, fcb, tm, late=()):
    t = x.shape[0]
    nt = t // tm
    nl = len(late)

    def body(*refs):
        x_ref, ycn_ref, ya_ref, wout_ref, wg_ref, wu_ref, goa_ref, gffn_ref, fcw_ref, fcb_ref = refs[0:10]
        x1_ref, gp_ref, up_ref, gate_ref, act_ref, ycat_ref, h2_ref = refs[10 + nl:17 + nl]
        cbuf = refs[17 + 2 * nl]
        if nl:
            gather = _Gather(refs[10:10 + nl], refs[17 + nl:17 + 2 * nl], *refs[18 + 2 * nl:21 + 2 * nl])
            pl.when(pl.program_id(0) == 0)(gather.start)
            pl.when(pl.program_id(0) == nt // 2)(gather.forward)
            pl.when(pl.program_id(0) == nt - 1)(gather.finish)

        @pl.when(pl.program_id(0) == 0)
        def _():
            cbuf[0:8, :] = jnp.zeros((8, D_FF), F32)

        yat = ya_ref[...]
        yan = ((yat * _rstd(yat)) * goa_ref[...]).astype(BF)
        ycn = ycn_ref[...]
        ycat_ref[:, 0:CONV_W] = ycn
        ycat_ref[:, CONV_W:D_MODEL] = yan
        x1 = x_ref[...] + _mm(ycn, wout_ref[0:CONV_W, :]) + _mm(yan, wout_ref[CONV_W:D_MODEL, :])
        x1_ref[...] = x1
        h2 = ((x1 * _rstd(x1)) * gffn_ref[...]).astype(BF)
        h2_ref[...] = h2
        for lo, hi in FF_SLABS:
            gps = _mm_nt(h2, wg_ref[lo:hi, :])
            ups = _mm_nt(h2, wu_ref[lo:hi, :])
            gp_ref[:, lo:hi] = gps.astype(BF)
            up_ref[:, lo:hi] = ups.astype(BF)
            cbuf[8:8 + tm, lo:hi] = gps
            cb = cbuf[:, lo:hi]
            gate = (fcw_ref[0:1, lo:hi] * _shift_down(cb, 2, tm) + fcw_ref[1:2, lo:hi] * _shift_down(cb, 1, tm)
                    + fcw_ref[2:3, lo:hi] * gps + fcb_ref[:, lo:hi])
            gate_ref[:, lo:hi] = gate.astype(BF)
            act_ref[:, lo:hi] = ((gate * jax.nn.sigmoid(gate)) * ups).astype(BF)
        cbuf[0:8, :] = cbuf[tm:tm + 8, :]

    any_spec = pl.BlockSpec(memory_space=pl.ANY)
    sems = [pltpu.SemaphoreType.DMA((6 * nl,)), pltpu.SemaphoreType.DMA((6 * nl,)), pltpu.SemaphoreType.DMA((nl,))]
    res = pl.pallas_call(
        body, name="fwd_ffn", grid=(nt,),
        in_specs=[_rows(tm, D_MODEL), _rows(tm, CONV_W), _rows(tm, ATTN_W), _const((D_MODEL, D_MODEL)),
                  _const((D_FF, D_MODEL)), _const((D_FF, D_MODEL)),
                  _const((1, ATTN_W)), _const((1, D_MODEL)), _const((3, D_FF)), _const((1, D_FF))]
        + [any_spec] * nl,
        out_specs=[_rows(tm, D_MODEL), _rows(tm, D_FF), _rows(tm, D_FF), _rows(tm, D_FF), _rows(tm, D_FF),
                   _rows(tm, D_MODEL), _rows(tm, D_MODEL)] + [any_spec] * nl,
        out_shape=[jax.ShapeDtypeStruct((t, D_MODEL), F32), jax.ShapeDtypeStruct((t, D_FF), BF),
                   jax.ShapeDtypeStruct((t, D_FF), BF), jax.ShapeDtypeStruct((t, D_FF), BF),
                   jax.ShapeDtypeStruct((t, D_FF), BF),
                   jax.ShapeDtypeStruct((t, D_MODEL), BF), jax.ShapeDtypeStruct((t, D_MODEL), BF)]
        + [jax.ShapeDtypeStruct((N_CHIPS,) + w.shape, w.dtype) for w in late],
        scratch_shapes=[pltpu.VMEM((tm + 8, D_FF), F32)] + (sems if nl else []),
        compiler_params=_params(56),
    )(x, ycn, ya, wout, wg4, wu4, g_oa, g_ffn, fcw, fcb, *late)
    return tuple(res[0:7]) + (list(res[7:]),)


def _fwd_tail(x1, act, p, target, wd4, wpg, wpp4, g_ple, tm):
    t = x1.shape[0]
    nt = t // tm

    def body(x1_ref, act_ref, p_ref, tgt_ref, wd_ref, wpg_ref, wpp_ref, g_ref,
             dx2_ref, h3_ref, ds_ref, dpp_ref, dg_ref, loss_ref, lacc):
        i = pl.program_id(0)

        @pl.when(i == 0)
        def _():
            dg_ref[...] = jnp.zeros_like(dg_ref)
            lacc[...] = jnp.zeros_like(lacc)

        x2 = x1_ref[...]
        for lo, hi in FF_SLABS:
            x2 = x2 + _mm(act_ref[:, lo:hi], wd_ref[lo:hi, :])
        r3 = _rstd(x2)
        xh = x2 * r3
        h3 = (xh * g_ref[...]).astype(BF)
        h3_ref[...] = h3
        sg = jax.nn.sigmoid(_mm(h3, wpg_ref[...]))
        pb = p_ref[...].astype(BF)
        pp = jnp.concatenate([_mm(pb, wpp_ref[s]) for s in range(N_CHIPS)], axis=1)
        err = (x2 + sg * pp) - tgt_ref[...]
        lacc[...] += _colsum(err * err)
        dx3 = err * (1.0 / D_MODEL)
        dpp_ref[...] = (dx3 * sg).astype(BF)
        dsb = ((dx3 * pp) * (sg * (1.0 - sg))).astype(BF)
        ds_ref[...] = dsb
        dh3 = _mm_nt(dsb, wpg_ref[...])
        dg_ref[...] += _colsum(dh3 * xh)
        dx2_ref[...] = dx3 + _norm_bwd(dh3, xh, r3, g_ref[...])

        @pl.when(i == nt - 1)
        def _():
            loss_ref[...] = jnp.full((1, 128), jnp.sum(lacc[...]) * (0.5 / D_MODEL), F32)

    return pl.pallas_call(
        body, name="fwd_tail", grid=(nt,),
        in_specs=[_rows(tm, D_MODEL), _rows(tm, D_FF), _rows(tm, PLE_DIM), _rows(tm, D_MODEL),
                  _const((D_FF, D_MODEL)), _const((D_MODEL, D_MODEL)),
                  _const((N_CHIPS, PLE_DIM, PLE_DIM)), _const((1, D_MODEL))],
        out_specs=[_rows(tm, D_MODEL), _rows(tm, D_MODEL), _rows(tm, D_MODEL), _rows(tm, D_MODEL),
                   pl.BlockSpec((1, D_MODEL), lambda i: (0, 0)), pl.BlockSpec((1, 128), lambda i: (0, 0))],
        out_shape=[jax.ShapeDtypeStruct((t, D_MODEL), F32), jax.ShapeDtypeStruct((t, D_MODEL), BF),
                   jax.ShapeDtypeStruct((t, D_MODEL), BF), jax.ShapeDtypeStruct((t, D_MODEL), BF),
                   jax.ShapeDtypeStruct((1, D_MODEL), F32), jax.ShapeDtypeStruct((1, 128), F32)],
        scratch_shapes=[pltpu.VMEM((1, D_MODEL), F32)],
        compiler_params=_params(48),
    )(x1, act, p, target, wd4, wpg, wpp4, g_ple)


def _bwd_ffn_a(dx2, gate, gp, up, wd4, fcw, tm, riders=()):
    t = dx2.shape[0]
    nt = t // tm
    nr = len(riders)

    def body(*refs):
        dx2_ref, gate_ref, gp_ref, up_ref, wd_ref, fcw_ref = refs[0:6]
        dgp_ref, dup_ref, dfcw_ref, dfcb_ref = refs[6 + nr:10 + nr]
        dbuf = refs[10 + 2 * nr]
        i = pl.program_id(0)
        if nr:
            _ride_scatter(i == 0, i == nt - 1, riders, refs[6:6 + nr], refs[10 + nr:10 + 2 * nr],
                          *refs[11 + 2 * nr:13 + 2 * nr])

        @pl.when(i == 0)
        def _():
            dbuf[tm:tm + 8, :] = jnp.zeros((8, D_FF), F32)
            dfcw_ref[...] = jnp.zeros_like(dfcw_ref)
            dfcb_ref[...] = jnp.zeros_like(dfcb_ref)

        dx2b = dx2_ref[...].astype(BF)
        for lo, hi in FF_SLABS:
            gate = gate_ref[:, lo:hi].astype(F32)
            gps = gp_ref[:, lo:hi].astype(F32)
            w0, w1, w2 = fcw_ref[0:1, lo:hi], fcw_ref[1:2, lo:hi], fcw_ref[2:3, lo:hi]
            sg = jax.nn.sigmoid(gate)
            dact = _mm_nt(dx2b, wd_ref[lo:hi, :])
            dup_ref[:, lo:hi] = (dact * (gate * sg)).astype(BF)
            dgate = (dact * up_ref[:, lo:hi].astype(F32)) * (sg * (1.0 + gate * (1.0 - sg)))
            dbuf[0:tm, lo:hi] = dgate
            db = dbuf[:, lo:hi]
            d1 = _shift_up(db, 1, tm)
            d2 = _shift_up(db, 2, tm)
            dfcb_ref[:, lo:hi] += _colsum(dgate)
            dfcw_ref[0:1, lo:hi] += _colsum(d2 * gps)
            dfcw_ref[1:2, lo:hi] += _colsum(d1 * gps)
            dfcw_ref[2:3, lo:hi] += _colsum(dgate * gps)
            dgp_ref[:, lo:hi] = (w2 * dgate + w1 * d1 + w0 * d2).astype(BF)
        dbuf[tm:tm + 8, :] = dbuf[0:8, :]

    r_in, r_out, r_sems = _rider_specs(riders)
    res = pl.pallas_call(
        body, name="bwd_ffn_a", grid=(nt,),
        in_specs=[_rows(tm, D_MODEL, nt), _rows(tm, D_FF, nt), _rows(tm, D_FF, nt), _rows(tm, D_FF, nt),
                  _const((D_FF, D_MODEL)), _const((3, D_FF))] + r_in,
        out_specs=[_rows(tm, D_FF, nt), _rows(tm, D_FF, nt),
                   pl.BlockSpec((3, D_FF), lambda i: (0, 0)), pl.BlockSpec((1, D_FF), lambda i: (0, 0))] + r_in,
        out_shape=[jax.ShapeDtypeStruct((t, D_FF), BF), jax.ShapeDtypeStruct((t, D_FF), BF),
                   jax.ShapeDtypeStruct((3, D_FF), F32), jax.ShapeDtypeStruct((1, D_FF), F32)] + r_out,
        scratch_shapes=[pltpu.VMEM((tm + 8, D_FF), F32)] + r_sems,
        compiler_params=_params(56),
    )(dx2, gate, gp, up, wd4, fcw, *[g for g, _ in riders])
    return res[0], res[1], res[2], res[3], list(res[4:])


def _bwd_ffn_b(dgp, dup, dx2, x1, ya, wg4, wu4, wout, g_ffn, g_oa, tm):
    t = dx2.shape[0]
    nt = t // tm

    def body(dgp_ref, dup_ref, dx2_ref, x1_ref, ya_ref, wg_ref, wu_ref, wout_ref, gffn_ref, goa_ref,
             dx1_ref, dycn_ref, dya_ref, dgffn_ref, dgoa_ref):
        @pl.when(pl.program_id(0) == 0)
        def _():
            dgffn_ref[...] = jnp.zeros_like(dgffn_ref)
            dgoa_ref[...] = jnp.zeros_like(dgoa_ref)

        dh2 = jnp.zeros((tm, D_MODEL), F32)
        for lo, hi in FF_SLABS:
            dh2 = dh2 + _mm(dgp_ref[:, lo:hi], wg_ref[lo:hi, :]) + _mm(dup_ref[:, lo:hi], wu_ref[lo:hi, :])
        x1 = x1_ref[...]
        r2 = _rstd(x1)
        xh = x1 * r2
        dgffn_ref[...] += _colsum(dh2 * xh)
        dx1 = dx2_ref[...] + _norm_bwd(dh2, xh, r2, gffn_ref[...])
        dx1_ref[...] = dx1
        dy = _mm_nt(dx1.astype(BF), wout_ref[...])
        dycn_ref[...] = dy[:, 0:CONV_W]
        dyan = dy[:, CONV_W:D_MODEL]
        yat = ya_ref[...]
        ra = _rstd(yat)
        yah = yat * ra
        dgoa_ref[...] += _colsum(dyan * yah)
        dya_ref[...] = _norm_bwd(dyan, yah, ra, goa_ref[...])

    return pl.pallas_call(
        body, name="bwd_ffn_b", grid=(nt,),
        in_specs=[_rows(tm, D_FF), _rows(tm, D_FF), _rows(tm, D_MODEL), _rows(tm, D_MODEL),
                  _rows(tm, ATTN_W), _const((D_FF, D_MODEL)), _const((D_FF, D_MODEL)),
                  _const((D_MODEL, D_MODEL)), _const((1, D_MODEL)), _const((1, ATTN_W))],
        out_specs=[_rows(tm, D_MODEL), _rows(tm, CONV_W), _rows(tm, ATTN_W),
                   pl.BlockSpec((1, D_MODEL), lambda i: (0, 0)), pl.BlockSpec((1, ATTN_W), lambda i: (0, 0))],
        out_shape=[jax.ShapeDtypeStruct((t, D_MODEL), F32), jax.ShapeDtypeStruct((t, CONV_W), F32),
                   jax.ShapeDtypeStruct((t, ATTN_W), F32),
                   jax.ShapeDtypeStruct((1, D_MODEL), F32), jax.ShapeDtypeStruct((1, ATTN_W), F32)],
        compiler_params=_params(48),
    )(dgp, dup, dx2, x1, ya, wg4, wu4, wout, g_ffn, g_oa)


def _bwd_mix(x, zbcx, qkv, dycn, dq, dk, dv, conv_w, conv_b, g_oc, g_mix, gm, gq8, gk8, tm):
    t = x.shape[0]
    nt = t // tm

    def body(x_ref, z_ref, zh_ref, qkv_ref, dycn_ref, dq_ref, dk_ref, dv_ref, cw_ref, cb_ref,
             goc_ref, g_ref, gm_ref, gq_ref, gk_ref,
             dz_ref, dcw_ref, dcb_ref, dgoc_ref, dgq_ref, dgk_ref, gw32_ref, gw16_ref,
             ubuf, dbuf, wacc, wstage, osem):
        i = pl.program_id(0)

        @pl.when(i == 0)
        def _():
            wacc[...] = jnp.zeros_like(wacc)
            dbuf[tm:tm + 8, :] = jnp.zeros((8, CONV_W), F32)
            dcw_ref[...] = jnp.zeros_like(dcw_ref)
            dcb_ref[...] = jnp.zeros_like(dcb_ref)
            dgoc_ref[...] = jnp.zeros_like(dgoc_ref)
            dgq_ref[...] = jnp.zeros_like(dgq_ref)
            dgk_ref[...] = jnp.zeros_like(dgk_ref)

        not_first_tile = i < nt - 1
        zb = z_ref[:, 0:512]
        zc = z_ref[:, 512:1024]
        zx = z_ref[:, 1024:1536]
        u = zc * zx
        ubuf[0:8, :] = jnp.where(not_first_tile, zh_ref[:, 512:1024] * zh_ref[:, 1024:1536], 0.0)
        ubuf[8:8 + tm, :] = u
        ub = ubuf[...]
        u1 = _shift_down(ub, 1, tm)
        u2 = _shift_down(ub, 2, tm)
        w0, w1, w2 = cw_ref[0:1, :], cw_ref[1:2, :], cw_ref[2:3, :]
        cv = w0 * u2 + w1 * u1 + w2 * u + cb_ref[...]
        yc = zb * cv
        rc = _rstd(yc)
        ych = yc * rc
        dycn = dycn_ref[...]
        dgoc_ref[...] += _colsum(dycn * ych)
        dyc = _norm_bwd(dycn, ych, rc, goc_ref[...])
        dcv = dyc * zb
        dcb_ref[...] += _colsum(dcv)
        dcw_ref[0:1, :] += _colsum(dcv * u2)
        dcw_ref[1:2, :] += _colsum(dcv * u1)
        dcw_ref[2:3, :] += _colsum(dcv * u)
        dbuf[0:tm, :] = dcv
        db = dbuf[...]
        du = w2 * dcv + w1 * _shift_up(db, 1, tm) + w0 * _shift_up(db, 2, tm)
        dbuf[tm:tm + 8, :] = dbuf[0:8, :]
        dz_ref[:, 0:512] = (dyc * cv).astype(BF)
        dz_ref[:, 512:1024] = (du * zx).astype(BF)
        dz_ref[:, 1024:1536] = (du * zc).astype(BF)
        for z0, d_ref, gg_ref, acc_ref, sc in ((0, dq_ref, gq_ref, dgq_ref, HEAD_DIM ** -0.5),
                                               (512, dk_ref, gk_ref, dgk_ref, 1.0)):
            z = qkv_ref[:, z0:z0 + 512]
            rr = lax.rsqrt(_head_mean(z * z, gm_ref) + EPS)
            zh = z * rr
            dn = d_ref[...] * sc
            acc_ref[...] += _colsum(dn * zh)
            dzh = dn * gg_ref[...]
            dz_ref[:, 1536 + z0:1536 + z0 + 512] = (rr * (dzh - zh * _head_mean(dzh * zh, gm_ref))).astype(BF)
        dz_ref[:, 2560:3072] = dv_ref[...].astype(BF)
        xt = x_ref[...]
        h1 = ((xt * _rstd(xt)) * g_ref[...]).astype(BF)
        for s in range(N_CHIPS):
            cols = slice(s * IN_SLAB, (s + 1) * IN_SLAB)
            wacc[:, cols] += _mm_tn(h1, dz_ref[:, cols])

        @pl.when(i == nt - 1)
        def _():
            wstage[...] = wacc[...].astype(BF)
            out32 = pltpu.make_async_copy(wacc, gw32_ref, osem.at[0])
            out16 = pltpu.make_async_copy(wstage, gw16_ref, osem.at[1])
            out32.start()
            out16.start()
            out32.wait()
            out16.wait()

    def acc(width, rows=1):
        return pl.BlockSpec((rows, width), lambda i: (0, 0))

    return pl.pallas_call(
        body, name="bwd_mix", grid=(nt,),
        in_specs=[_rows(tm, D_MODEL, nt), _rows(tm, 1536, nt), _halo(tm, 1536, nt),
                  _rows(tm, 1536, nt), _rows(tm, CONV_W, nt), _rows(tm, ATTN_W, nt), _rows(tm, ATTN_W, nt),
                  _rows(tm, ATTN_W, nt),
                  _const((3, CONV_W)), _const((1, CONV_W)), _const((1, CONV_W)), _const((1, D_MODEL)),
                  _const((ATTN_W, ATTN_W)), _const((1, ATTN_W)), _const((1, ATTN_W))],
        out_specs=[_rows(tm, 3072, nt),
                   acc(CONV_W, 3), acc(CONV_W), acc(CONV_W), acc(ATTN_W), acc(ATTN_W),
                   pl.BlockSpec(memory_space=pl.ANY), pl.BlockSpec(memory_space=pl.ANY)],
        out_shape=[jax.ShapeDtypeStruct((t, 3072), BF), jax.ShapeDtypeStruct((3, CONV_W), F32),
                   jax.ShapeDtypeStruct((1, CONV_W), F32), jax.ShapeDtypeStruct((1, CONV_W), F32),
                   jax.ShapeDtypeStruct((1, ATTN_W), F32), jax.ShapeDtypeStruct((1, ATTN_W), F32),
                   jax.ShapeDtypeStruct((D_MODEL, 3072), F32), jax.ShapeDtypeStruct((D_MODEL, 3072), BF)],
        scratch_shapes=[pltpu.VMEM((tm + 8, CONV_W), F32), pltpu.VMEM((tm + 8, CONV_W), F32),
                        pltpu.VMEM((D_MODEL, 3072), F32), pltpu.VMEM((D_MODEL, 3072), BF),
                        pltpu.SemaphoreType.DMA((2,))],
        compiler_params=_params(56),
    )(x, zbcx, zbcx, qkv, dycn, dq, dk, dv, conv_w, conv_b, g_oc, g_mix, gm, gq8, gk8)


def _bwd_in(x, dx1, dz, win4, g_mix, tm, riders=()):
    t = x.shape[0]
    nt = t // tm
    nr = len(riders)

    def body(*refs):
        x_ref, dx1_ref, dz_ref, w_ref, g_ref = refs[0:5]
        gx_ref, dg_ref = refs[5 + nr:7 + nr]
        i = pl.program_id(0)
        if nr:
            _ride_scatter(i == 0, i == nt - 1, riders, refs[5:5 + nr], refs[7 + nr:7 + 2 * nr],
                          *refs[7 + 2 * nr:9 + 2 * nr])

        @pl.when(i == 0)
        def _():
            dg_ref[...] = jnp.zeros_like(dg_ref)

        dh1 = jnp.zeros((tm, D_MODEL), F32)
        for s in range(N_CHIPS):
            dh1 = dh1 + _mm_nt(dz_ref[:, s * IN_SLAB:(s + 1) * IN_SLAB], w_ref[s])
        xt = x_ref[...]
        r1 = _rstd(xt)
        xh = xt * r1
        dg_ref[...] += _colsum(dh1 * xh)
        gx_ref[...] = dx1_ref[...] + _norm_bwd(dh1, xh, r1, g_ref[...])

    r_in, r_out, r_sems = _rider_specs(riders)
    res = pl.pallas_call(
        body, name="bwd_in", grid=(nt,),
        in_specs=[_rows(tm, D_MODEL), _rows(tm, D_MODEL), _rows(tm, 3072), _const((N_CHIPS, D_MODEL, IN_SLAB)),
                  _const((1, D_MODEL))] + r_in,
        out_specs=[_rows(tm, D_MODEL), pl.BlockSpec((1, D_MODEL), lambda i: (0, 0))] + r_in,
        out_shape=[jax.ShapeDtypeStruct((t, D_MODEL), F32), jax.ShapeDtypeStruct((1, D_MODEL), F32)] + r_out,
        scratch_shapes=r_sems,
        compiler_params=_params(48),
    )(x, dx1, dz, win4, g_mix, *[g for g, _ in riders])
    return res[0], res[1], list(res[2:])


def _wgrad(a, b, tn, tt, name):
    t, k = a.shape
    n = b.shape[1]
    nt = t // tt

    def body(a_ref, b_ref, o_ref, ob_ref):
        @pl.when(pl.program_id(1) == 0)
        def _():
            o_ref[...] = jnp.zeros_like(o_ref)

        o_ref[...] += _mm_tn(a_ref[...].astype(BF), b_ref[...].astype(BF))

        @pl.when(pl.program_id(1) == nt - 1)
        def _():
            ob_ref[...] = o_ref[...].astype(BF)

    spec = pl.BlockSpec((k, tn), lambda j, i: (0, j))
    return pl.pallas_call(
        body, name=name, grid=(n // tn, nt),
        in_specs=[pl.BlockSpec((tt, k), lambda j, i: (i, 0)), pl.BlockSpec((tt, tn), lambda j, i: (i, j))],
        out_specs=[spec, spec],
        out_shape=[jax.ShapeDtypeStruct((k, n), F32), jax.ShapeDtypeStruct((k, n), BF)],
        compiler_params=_params(58, 2),
    )(a, b)


def _gather_weights(shards, pack):
    nw = len(shards)

    def body(*refs):
        ins = refs[:nw]
        pack_ref = refs[nw]
        outs = refs[nw + 1:2 * nw + 1]
        pack_out = refs[2 * nw + 1]
        send_sems, recv_sems, local_sems = refs[2 * nw + 2:]
        x, y, c = _place()
        me = 2 * x + y
        local, remote = [], []

        def sem(w, j):
            return w * 6 + j

        def push(src, dst, w, j, to):
            return pltpu.make_async_remote_copy(src_ref=src, dst_ref=dst, send_sem=send_sems.at[sem(w, j)],
                                                recv_sem=recv_sems.at[sem(w, j)], device_id=to, device_id_type=MESH)

        def half_rows(w, h):
            half = ins[w].shape[0] // 2
            return pl.ds(pl.multiple_of(h * half, 16), half)

        for w in range(nw):
            local.append(pltpu.make_async_copy(ins[w], outs[w].at[me], local_sems.at[w]))
            for k in (1, 2, 3):
                px, py = _chip_peer(x, y, k)
                mine = half_rows(w, c)
                remote.append(push(ins[w].at[mine], outs[w].at[me, mine], w, k - 1, (px, py, c)))
        local.append(pltpu.make_async_copy(pack_ref, pack_out.at[me], local_sems.at[nw]))
        for k in (1, 2, 3):
            px, py = _chip_peer(x, y, k)
            remote.append(push(pack_ref, pack_out.at[me], nw, k - 1, (px, py, c)))
        for cp in local + remote:
            cp.start()
        for w in range(nw):
            for k in (1, 2, 3):
                landed = outs[w].at[me ^ k, half_rows(w, c)]
                push(landed, landed, w, k - 1, (x, y, c)).wait_recv()
                fw = push(landed, landed, w, 2 + k, (x, y, 1 - c))
                fw.start()
                remote.append(fw)
        for k in (1, 2, 3):
            landed = pack_out.at[me ^ k]
            push(landed, landed, nw, k - 1, (x, y, c)).wait_recv()
        for w in range(nw):
            for k in (1, 2, 3):
                landed = outs[w].at[me ^ k, half_rows(w, 1 - c)]
                push(landed, landed, w, 2 + k, (x, y, c)).wait_recv()
        for cp in remote:
            cp.wait_send()
        for cp in local:
            cp.wait()

    any_spec = pl.BlockSpec(memory_space=pl.ANY)
    out_shape = [jax.ShapeDtypeStruct((N_CHIPS,) + s.shape, s.dtype) for s in shards]
    out_shape.append(jax.ShapeDtypeStruct((N_CHIPS,) + pack.shape, pack.dtype))
    return pl.pallas_call(
        body, name="gather_weights",
        in_specs=[any_spec] * (nw + 1), out_specs=[any_spec] * (nw + 1), out_shape=out_shape,
        scratch_shapes=[pltpu.SemaphoreType.DMA(((nw + 1) * 6,)), pltpu.SemaphoreType.DMA(((nw + 1) * 6,)),
                        pltpu.SemaphoreType.DMA((nw + 1,))],
    )(*shards, pack)


def _adamw(w, g, m, v):
    m = ADAM_B1 * m + (1.0 - ADAM_B1) * g
    v = ADAM_B2 * v + (1.0 - ADAM_B2) * (g * g)
    m_hat = m / (1.0 - ADAM_B1 ** ADAM_STEP)
    v_hat = v / (1.0 - ADAM_B2 ** ADAM_STEP)
    delta = -ADAM_LR * (m_hat / (jnp.sqrt(v_hat) + ADAM_EPS) + ADAM_WD * w)
    return delta, m, v


def _finish_reduce(grad, slots, col_sharded, name):
    r, cw = _piece_shape(grad.shape, col_sharded)
    chunk = 32
    assert r % chunk == 0

    def body(g_hbm, slots_ref, full, own, lsem, c_send, c_recv):
        x, y, c = _place()
        cp = pltpu.make_async_copy(g_hbm.at[_piece_window(col_sharded, r, cw, 2 * x + y, c)], own, lsem)
        cp.start()
        cp.wait()
        mine = pl.multiple_of(c * r, 8)

        def add(j, carry):
            rows = pl.ds(pl.multiple_of(j * chunk, 8), chunk)
            tot = own[rows, :]
            for k in range(7):
                tot = tot + slots_ref[k, rows, :].astype(F32)
            full[pl.ds(mine + pl.multiple_of(j * chunk, 8), chunk), :] = tot
            return carry

        lax.fori_loop(0, r // chunk, add, 0)
        half = full.at[pl.ds(mine, r), :]
        swap = pltpu.make_async_remote_copy(src_ref=half, dst_ref=half, send_sem=c_send, recv_sem=c_recv,
                                            device_id=(x, y, 1 - c), device_id_type=MESH)
        swap.start()
        swap.wait()

    vmem = pl.BlockSpec(memory_space=pltpu.VMEM)
    return pl.pallas_call(
        body, name=name, in_specs=[pl.BlockSpec(memory_space=pltpu.HBM), vmem], out_specs=vmem,
        out_shape=jax.ShapeDtypeStruct((2 * r, cw), F32),
        scratch_shapes=[pltpu.VMEM((r, cw), F32), pltpu.SemaphoreType.DMA, pltpu.SemaphoreType.DMA,
                        pltpu.SemaphoreType.DMA],
        compiler_params=pltpu.CompilerParams(vmem_limit_bytes=32 * MIB),
    )(grad, slots)


def _adamw_big(g, w, m, v, name):
    vr, vc = w.shape
    assert g.shape == w.shape
    rows = 64

    def body(g_ref, w_ref, m_ref, v_ref, go_ref, do_ref, mo_ref, vo_ref):
        gg = g_ref[...]
        delta, mn, vn = _adamw(w_ref[...], gg, m_ref[...], v_ref[...])
        go_ref[...] = gg
        do_ref[...] = delta
        mo_ref[...] = mn
        vo_ref[...] = vn

    blk = pl.BlockSpec((rows, vc), lambda i: (i, 0))
    shard = jax.ShapeDtypeStruct((vr, vc), F32)
    return pl.pallas_call(
        body, name=name, grid=(vr // rows,),
        in_specs=[blk, blk, blk, blk], out_specs=[blk] * 4,
        out_shape=[shard] * 4, compiler_params=_params(32),
    )(g, w, m, v)


def _allreduce_small(pack):
    rows = pack.shape[0]

    def body(p_ref, o_ref, slots, send_sems, recv_sems):
        x, y, c = _place()
        me = 4 * x + 2 * y + c
        slots[me] = p_ref[...]
        sends = []
        for k in range(1, 8):
            cp = pltpu.make_async_remote_copy(
                src_ref=p_ref, dst_ref=slots.at[me], send_sem=send_sems.at[k - 1], recv_sem=recv_sems.at[k - 1],
                device_id=(x ^ (k >> 2), y ^ ((k >> 1) & 1), c ^ (k & 1)), device_id_type=MESH)
            cp.start()
            sends.append(cp)
        for cp in sends:
            cp.wait()
        tot = slots[0]
        for j in range(1, 8):
            tot = tot + slots[j]
        o_ref[...] = tot

    vmem = pl.BlockSpec(memory_space=pltpu.VMEM)
    return pl.pallas_call(
        body, name="allreduce_small", in_specs=[vmem], out_specs=vmem,
        out_shape=jax.ShapeDtypeStruct(pack.shape, F32),
        scratch_shapes=[pltpu.VMEM((8, rows, D_MODEL), F32), pltpu.SemaphoreType.DMA((7,)),
                        pltpu.SemaphoreType.DMA((7,))],
    )(pack)


def _adamw_small(ws, gs, ms, vs):
    n = len(ws)

    def body(*refs):
        w_refs, g_refs, m_refs, v_refs = refs[0:n], refs[n:2 * n], refs[2 * n:3 * n], refs[3 * n:4 * n]
        d_refs, mo_refs, vo_refs = refs[4 * n:5 * n], refs[5 * n:6 * n], refs[6 * n:7 * n]
        for j in range(n):
            delta, mn, vn = _adamw(w_refs[j][...], g_refs[j][...], m_refs[j][...], v_refs[j][...])
            d_refs[j][...] = delta
            mo_refs[j][...] = mn
            vo_refs[j][...] = vn

    vmem = pl.BlockSpec(memory_space=pltpu.VMEM)
    shapes = [jax.ShapeDtypeStruct(w.shape, F32) for w in ws]
    outs = pl.pallas_call(
        body, name="adamw_small", in_specs=[vmem] * (4 * n), out_specs=[vmem] * (3 * n), out_shape=shapes * 3,
    )(*ws, *gs, *ms, *vs)
    return outs[0:n], outs[n:2 * n], outs[2 * n:3 * n]


def _local_step(x, p, target, wts, late=None):
    (win4, wout, wg4, wu4, wd4, wpg, wpp4, conv_w, fcw, g_mix, conv_b, gq, gk, g_oc, g_oa, g_ffn, fcb, g_ple) = wts
    comm = late is not None
    gm = jnp.kron(jnp.eye(N_HEADS, dtype=F32), jnp.full((HEAD_DIM, HEAD_DIM), 1.0 / HEAD_DIM, F32)).astype(BF)
    gq8, gk8 = jnp.tile(gq, (1, N_HEADS)), jnp.tile(gk, (1, N_HEADS))
    mb = _mask_table()
    zbcx, qkv, ycn, qkn = _fwd_mix(x, g_mix, win4, conv_w, conv_b, g_oc, gm, gq8, gk8, 512)
    ya, lse, gathered = _attn_fwd(qkn, qkv, mb, late[0:3] if comm else ())
    if comm:
        wout, wg4, wu4 = (g.reshape(-1, D_MODEL) for g in gathered)
    x1, gp, up, gate, act, ycat, h2, gathered = _fwd_ffn(x, ycn, ya, wout, wg4, wu4, g_oa, g_ffn, fcw, fcb, 256,
                                                    late[3:6] if comm else ())
    if comm:
        wd4, wpg, wpp4 = gathered
        wd4, wpg = wd4.reshape(D_FF, D_MODEL), wpg.reshape(D_MODEL, D_MODEL)
    dx2, h3, ds, dpp, dg_ple, loss = _fwd_tail(x1, act, p, target, wd4, wpg, wpp4, g_ple, 512)
    big, big16, slots = {}, {}, {}

    def wgrad(name, a, b, tn):
        big[name], big16[name] = _wgrad(a, b, tn, 1024 if a.shape[1] > D_MODEL else 2048, "wgrad_" + name)
        return (big16[name], _COL_SHARDED[name])

    riders = [wgrad("w_down", act, dx2, 1024), wgrad("w_ple_gate", h3, ds, 1024), wgrad("w_ple_proj", p, dpp, 1024)]
    dgp, dup, dfcw, dfcb, got = _bwd_ffn_a(dx2, gate, gp, up, wd4, fcw, 512, riders if comm else ())
    slots.update(zip(("w_down", "w_ple_gate", "w_ple_proj"), got))
    riders = [wgrad("w_gate", dgp, h2, 1024), wgrad("w_up", dup, h2, 1024)]
    dx1, dycn, dya, dg_ffn, dg_oa = _bwd_ffn_b(dgp, dup, dx2, x1, ya, wg4, wu4, wout, g_ffn, g_oa, 512)
    riders.append(wgrad("w_out", ycat, dx1, 1024))
    dq, dk, dv, got = _attn_bwd(qkn, qkv, ya, lse, dya, mb, riders if comm else ())
    slots.update(zip(("w_gate", "w_up", "w_out"), got))
    dz, dcw, dcb, dg_oc, dgq8, dgk8, big["w_in"], big16["w_in"] = _bwd_mix(
        x, zbcx, qkv, dycn, dq, dk, dv, conv_w, conv_b, g_oc, g_mix, gm, gq8, gk8, 512)
    riders = [(big16["w_in"], _COL_SHARDED["w_in"])]
    grad_x, dg_mix, got = _bwd_in(x, dx1, dz, win4, g_mix, 512, riders if comm else ())
    slots.update(zip(("w_in",), got))
    dgq = dgq8.reshape(N_HEADS, HEAD_DIM).sum(axis=0, keepdims=True)
    dgk = dgk8.reshape(N_HEADS, HEAD_DIM).sum(axis=0, keepdims=True)
    small = dict(g_mix=dg_mix, conv_w=dcw, conv_b=dcb, q_norm_g=dgq, k_norm_g=dgk, g_out_conv=dg_oc,
                 g_out_attn=dg_oa, g_ffn=dg_ffn, ffn_conv_w=dfcw, ffn_conv_b=dfcb, g_ple=dg_ple)
    return loss[0, 0], grad_x, big, slots, small


_SMALL_ROWS = 24


def _pack_small(s, loss):
    z64 = jnp.zeros((1, 1024 - 512 - 128), F32)
    rows = [s["g_mix"], s["g_ffn"], s["g_ple"],
            jnp.concatenate([s["conv_b"], s["g_out_conv"]], axis=1),
            jnp.concatenate([s["g_out_attn"], s["q_norm_g"], s["k_norm_g"], z64], axis=1),
            jnp.pad(s["conv_w"], ((0, 0), (0, 512))),
            jnp.pad(s["ffn_conv_b"], ((0, 0), (0, 3072 - D_FF))).reshape(3, 1024),
            jnp.pad(s["ffn_conv_w"], ((0, 0), (0, 3072 - D_FF))).reshape(9, 1024),
            jnp.pad(loss.reshape(1, 1), ((0, 0), (0, 1023))),
            jnp.zeros((_SMALL_ROWS - 21, 1024), F32)]
    return jnp.concatenate(rows, axis=0)


def _unpack_small(t):
    return dict(g_mix=t[0:1], g_ffn=t[1:2], g_ple=t[2:3], conv_b=t[3:4, 0:512], g_out_conv=t[3:4, 512:1024],
                g_out_attn=t[4:5, 0:512], q_norm_g=t[4:5, 512:576], k_norm_g=t[4:5, 576:640],
                conv_w=t[5:8, 0:512], ffn_conv_b=t[8:11].reshape(1, 3072)[:, :D_FF],
                ffn_conv_w=t[11:20].reshape(3, 3072)[:, :D_FF], loss=t[20, 0])


_BIG = ("w_in", "w_out", "w_gate", "w_up", "w_down", "w_ple_gate", "w_ple_proj")
_COL_SHARDED = dict(w_in=True, w_out=False, w_gate=False, w_up=False, w_down=False, w_ple_gate=False, w_ple_proj=True)
_TRANSPOSED = ("w_gate", "w_up")
_WEIGHTS = ("g_mix", "w_in", "conv_w", "conv_b", "q_norm_g", "k_norm_g", "g_out_conv", "g_out_attn", "w_out",
            "g_ffn", "w_gate", "w_up", "ffn_conv_w", "ffn_conv_b", "w_down", "g_ple", "w_ple_gate", "w_ple_proj")


def kernel(x, p, g_mix, w_in, conv_w, conv_b, q_norm_g, k_norm_g, g_out_conv, g_out_attn, w_out, g_ffn, w_gate, w_up, ffn_conv_w, ffn_conv_b, w_down, g_ple, w_ple_gate, w_ple_proj, loss_target, m_g_mix, m_w_in, m_conv_w, m_conv_b, m_q_norm_g, m_k_norm_g, m_g_out_conv, m_g_out_attn, m_w_out, m_g_ffn, m_w_gate, m_w_up, m_ffn_conv_w, m_ffn_conv_b, m_w_down, m_g_ple, m_w_ple_gate, m_w_ple_proj, v_g_mix, v_w_in, v_conv_w, v_conv_b, v_q_norm_g, v_k_norm_g, v_g_out_conv, v_g_out_attn, v_w_out, v_g_ffn, v_w_gate, v_w_up, v_ffn_conv_w, v_ffn_conv_b, v_w_down, v_g_ple, v_w_ple_gate, v_w_ple_proj):
    w = dict(g_mix=g_mix, w_in=w_in, conv_w=conv_w, conv_b=conv_b, q_norm_g=q_norm_g, k_norm_g=k_norm_g,
             g_out_conv=g_out_conv, g_out_attn=g_out_attn, w_out=w_out, g_ffn=g_ffn, w_gate=w_gate, w_up=w_up,
             ffn_conv_w=ffn_conv_w, ffn_conv_b=ffn_conv_b, w_down=w_down, g_ple=g_ple, w_ple_gate=w_ple_gate,
             w_ple_proj=w_ple_proj)
    m = dict(g_mix=m_g_mix, w_in=m_w_in, conv_w=m_conv_w, conv_b=m_conv_b, q_norm_g=m_q_norm_g, k_norm_g=m_k_norm_g,
             g_out_conv=m_g_out_conv, g_out_attn=m_g_out_attn, w_out=m_w_out, g_ffn=m_g_ffn, w_gate=m_w_gate,
             w_up=m_w_up, ffn_conv_w=m_ffn_conv_w, ffn_conv_b=m_ffn_conv_b, w_down=m_w_down, g_ple=m_g_ple,
             w_ple_gate=m_w_ple_gate, w_ple_proj=m_w_ple_proj)
    v = dict(g_mix=v_g_mix, w_in=v_w_in, conv_w=v_conv_w, conv_b=v_conv_b, q_norm_g=v_q_norm_g, k_norm_g=v_k_norm_g,
             g_out_conv=v_g_out_conv, g_out_attn=v_g_out_attn, w_out=v_w_out, g_ffn=v_g_ffn, w_gate=v_w_gate,
             w_up=v_w_up, ffn_conv_w=v_ffn_conv_w, ffn_conv_b=v_ffn_conv_b, w_down=v_w_down, g_ple=v_g_ple,
             w_ple_gate=v_w_ple_gate, w_ple_proj=v_w_ple_proj)
    mats = [k for k, a in w.items() if a.ndim == 3]
    w = {k: (a[0] if k in mats else a) for k, a in w.items()}
    m = {k: (a[0] if k in mats else a) for k, a in m.items()}
    v = {k: (a[0] if k in mats else a) for k, a in v.items()}
    for n in _TRANSPOSED:
        w[n], m[n], v[n] = w[n].T, m[n].T, v[n].T
    chip = 2 * lax.axis_index("x") + lax.axis_index("y")

    late = [w[n].astype(BF) for n in ("w_out", "w_gate", "w_up", "w_down", "w_ple_gate", "w_ple_proj")]
    pack = jnp.pad(jnp.concatenate([w["conv_w"], w["ffn_conv_w"]], axis=1), ((0, 5), (0, 1024 - 128 - D_FF_SHARD)))
    win4, pack4 = _gather_weights([w["w_in"].astype(BF)], pack)
    conv_w_full = pack4[:, 0:3, 0:128].transpose(1, 0, 2).reshape(3, CONV_W)
    fcw_full = pack4[:, 0:3, 128:128 + D_FF_SHARD].transpose(1, 0, 2).reshape(3, D_FF)
    wts = (win4, None, None, None, None, None, None, conv_w_full, fcw_full, w["g_mix"], w["conv_b"], w["q_norm_g"],
           w["k_norm_g"], w["g_out_conv"], w["g_out_attn"], w["g_ffn"], w["ffn_conv_b"], w["g_ple"])

    loss, grad_x, big, slots, small = _local_step(x[0], p[0, 0], loss_target[0], wts, late)

    grads, deltas, new_m, new_v = {}, {}, {}, {}
    for name in _BIG:
        total = _finish_reduce(big[name], slots[name], _COL_SHARDED[name], "finish_" + name)
        grads[name], deltas[name], new_m[name], new_v[name] = _adamw_big(total, w[name], m[name], v[name],
                                                                         "adamw_" + name)
    tot = _unpack_small(_allreduce_small(_pack_small(small, loss)))
    loss = tot.pop("loss")
    tot["conv_w"] = lax.dynamic_slice_in_dim(tot["conv_w"], chip * 128, 128, axis=1)
    tot["ffn_conv_w"] = lax.dynamic_slice_in_dim(tot["ffn_conv_w"], chip * D_FF_SHARD, D_FF_SHARD, axis=1)
    names = [n for n in _WEIGHTS if n not in _BIG]
    d_s, m_s, v_s = _adamw_small([w[n] for n in names], [tot[n] for n in names], [m[n] for n in names],
                                 [v[n] for n in names])
    for j, n in enumerate(names):
        grads[n], deltas[n], new_m[n], new_v[n] = tot[n], d_s[j], m_s[j], v_s[j]

    out = [loss, grad_x[None]]
    for group in (grads, deltas, new_m, new_v):
        for n in _TRANSPOSED:
            group[n] = group[n].T
        out += [group[n][None] if n in mats else group[n] for n in _WEIGHTS]
    return tuple(out)
```

```python
import jax
import jax.numpy as jnp
from jax import lax
from jax.experimental import pallas as pl
from jax.experimental.pallas import tpu as pltpu

D_MODEL = 1024
CONV_W = 512
N_HEADS = 8
HEAD_DIM = 64
ATTN_W = 512
D_FF = 2816
D_FF_SHARD = 704
FF_SLABS = ((0, 1408), (1408, 2816))
IN_SLAB = 768
PLE_DIM = 256
N_CHIPS = 4
QBLK = 128
DILATIONS = (1, 4, 16)
EPS = 1e-6
NEG = -1e30
MESH = pl.DeviceIdType.MESH

ADAM_LR = 0.001
ADAM_B1 = 0.9
ADAM_B2 = 0.999
ADAM_EPS = 1e-08
ADAM_WD = 0.01
ADAM_STEP = 10

BF = jnp.bfloat16
F32 = jnp.float32
MIB = 1024 * 1024


def _mm(a, b):
    return jnp.dot(a, b, preferred_element_type=F32)


def _mm_nt(a, b):
    return lax.dot_general(a, b, (((1,), (1,)), ((), ())), preferred_element_type=F32)


def _mm_tn(a, b):
    return lax.dot_general(a, b, (((0,), (0,)), ((), ())), preferred_element_type=F32)


def _rstd(a):
    return lax.rsqrt(jnp.mean(a * a, axis=-1, keepdims=True) + EPS)


def _norm_bwd(dy, xh, r, g):
    dxh = dy * g
    return r * (dxh - xh * jnp.mean(dxh * xh, axis=-1, keepdims=True))


def _colsum(a):
    return jnp.sum(a, axis=0, keepdims=True)


def _head_mean(a, gm_ref):
    return _mm(a.astype(BF), gm_ref[...])


def _shift_down(buf, k, tm):
    return pltpu.roll(buf, k, axis=0)[8:8 + tm]


def _shift_up(buf, k, tm):
    return pltpu.roll(buf, tm + 8 - k, axis=0)[0:tm]


def _params(vmem_mib, n_grid=1):
    return pltpu.CompilerParams(dimension_semantics=("arbitrary",) * n_grid, vmem_limit_bytes=vmem_mib * MIB)


def _const(shape):
    n = len(shape)
    return pl.BlockSpec(shape, lambda *_: (0,) * n, pipeline_mode=pl.Buffered(1))


def _rows(tm, width, rev_of=None):
    if rev_of is None:
        return pl.BlockSpec((tm, width), lambda i: (i, 0))
    return pl.BlockSpec((tm, width), lambda i: (rev_of - 1 - i, 0))


def _halo(tm, width, nt):
    return pl.BlockSpec((8, width), lambda i: (jnp.maximum((nt - 1 - i) * (tm // 8) - 1, 0), 0))


def _fwd_mix(x, g_mix, win4, conv_w, conv_b, g_oc, gm, gq8, gk8, tm):
    t = x.shape[0]
    nt = t // tm

    def body(x_ref, g_ref, w_ref, cw_ref, cb_ref, goc_ref, gm_ref, gq_ref, gk_ref,
             zbcx_ref, qkv_ref, ycn_ref, qkn_ref, ubuf):
        @pl.when(pl.program_id(0) == 0)
        def _():
            ubuf[0:8, :] = jnp.zeros((8, CONV_W), F32)

        xt = x_ref[...]
        h = ((xt * _rstd(xt)) * g_ref[...]).astype(BF)
        zbcx_ref[:, 0:IN_SLAB] = _mm(h, w_ref[0])
        zbcx_ref[:, IN_SLAB:2 * IN_SLAB] = _mm(h, w_ref[1])
        qkv_ref[:, 0:IN_SLAB] = _mm(h, w_ref[2])
        qkv_ref[:, IN_SLAB:2 * IN_SLAB] = _mm(h, w_ref[3])
        u = zbcx_ref[:, 512:1024] * zbcx_ref[:, 1024:1536]
        ubuf[8:8 + tm, :] = u
        ub = ubuf[...]
        cv = (cw_ref[0:1, :] * _shift_down(ub, 2, tm) + cw_ref[1:2, :] * _shift_down(ub, 1, tm)
              + cw_ref[2:3, :] * u + cb_ref[...])
        ubuf[0:8, :] = ubuf[tm:tm + 8, :]
        yc = zbcx_ref[:, 0:512] * cv
        ycn_ref[...] = ((yc * _rstd(yc)) * goc_ref[...]).astype(BF)
        zq = qkv_ref[:, 0:512]
        zk = qkv_ref[:, 512:1024]
        rq = lax.rsqrt(_head_mean(zq * zq, gm_ref) + EPS)
        rk = lax.rsqrt(_head_mean(zk * zk, gm_ref) + EPS)
        qkn_ref[:, 0:512] = ((zq * rq) * gq_ref[...]) * (HEAD_DIM ** -0.5)
        qkn_ref[:, 512:1024] = (zk * rk) * gk_ref[...]

    return pl.pallas_call(
        body, name="fwd_mix", grid=(nt,),
        in_specs=[_rows(tm, D_MODEL), _const((1, D_MODEL)), _const((N_CHIPS, D_MODEL, IN_SLAB)),
                  _const((3, CONV_W)), _const((1, CONV_W)), _const((1, CONV_W)), _const((ATTN_W, ATTN_W)),
                  _const((1, ATTN_W)), _const((1, ATTN_W))],
        out_specs=[_rows(tm, 1536), _rows(tm, 1536), _rows(tm, CONV_W), _rows(tm, 1024)],
        out_shape=[jax.ShapeDtypeStruct((t, 1536), F32), jax.ShapeDtypeStruct((t, 1536), F32),
                   jax.ShapeDtypeStruct((t, CONV_W), BF), jax.ShapeDtypeStruct((t, 1024), F32)],
        scratch_shapes=[pltpu.VMEM((tm + 8, CONV_W), F32)],
        compiler_params=_params(48),
    )(x, g_mix, win4, conv_w, conv_b, g_oc, gm, gq8, gk8)


def _place():
    x, y, c = lax.axis_index("x"), lax.axis_index("y"), lax.axis_index("c")
    return x, y, c


def _chip_peer(x, y, k):
    return x ^ (k >> 1), y ^ (k & 1)


def _piece_shape(grad_shape, col_sharded):
    kk, nn = grad_shape
    return (kk // 2, nn // N_CHIPS) if col_sharded else (kk // (2 * N_CHIPS), nn)


def _piece_window(col_sharded, r, cw, s, h):
    if col_sharded:
        return (pl.ds(pl.multiple_of(h * r, 16), r), pl.ds(pl.multiple_of(s * cw, 128), cw))
    return (pl.ds(pl.multiple_of((2 * s + h) * r, 16), r), slice(None))


def _scatter_copies(g_ref, slots_ref, send_sems, recv_sems, base, col_sharded):
    x, y, c = _place()
    r, cw = slots_ref.shape[1:]
    copies = []
    for k in range(1, 8):
        tx, ty, tc = x ^ (k >> 2), y ^ ((k >> 1) & 1), c ^ (k & 1)
        copies.append(pltpu.make_async_remote_copy(
            src_ref=g_ref.at[_piece_window(col_sharded, r, cw, 2 * tx + ty, tc)], dst_ref=slots_ref.at[k - 1],
            send_sem=send_sems.at[base + k - 1], recv_sem=recv_sems.at[base + k - 1],
            device_id=(tx, ty, tc), device_id_type=MESH))
    return copies


def _ride_scatter(first, last, riders, g_refs, slot_refs, send_sems, recv_sems):
    def all_copies():
        out = []
        for j, (_, col_sharded) in enumerate(riders):
            out += _scatter_copies(g_refs[j], slot_refs[j], send_sems, recv_sems, 7 * j, col_sharded)
        return out

    @pl.when(first)
    def _():
        for cp in all_copies():
            cp.start()

    @pl.when(last)
    def _():
        for cp in all_copies():
            cp.wait()


def _rider_specs(riders):
    any_spec = pl.BlockSpec(memory_space=pl.ANY)
    shapes = [jax.ShapeDtypeStruct((7,) + _piece_shape(g.shape, cs), BF) for g, cs in riders]
    sems = [pltpu.SemaphoreType.DMA((7 * len(riders),)), pltpu.SemaphoreType.DMA((7 * len(riders),))] if riders else []
    return [any_spec] * len(riders), shapes, sems


class _Gather:
    def __init__(self, ins, outs, send_sems, recv_sems, local_sems):
        self.ins, self.outs = ins, outs
        self.send_sems, self.recv_sems, self.local_sems = send_sems, recv_sems, local_sems
        self.x, self.y, self.c = _place()
        self.me = 2 * self.x + self.y

    def _push(self, src, dst, w, j, to):
        return pltpu.make_async_remote_copy(src_ref=src, dst_ref=dst, send_sem=self.send_sems.at[6 * w + j],
                                            recv_sem=self.recv_sems.at[6 * w + j], device_id=to, device_id_type=MESH)

    def _half(self, w, h):
        half = self.ins[w].shape[0] // 2
        return pl.ds(pl.multiple_of(h * half, 16), half)

    def _local(self, w):
        return pltpu.make_async_copy(self.ins[w], self.outs[w].at[self.me], self.local_sems.at[w])

    def _ici(self, w, k):
        px, py = _chip_peer(self.x, self.y, k)
        mine = self._half(w, self.c)
        return self._push(self.ins[w].at[mine], self.outs[w].at[self.me, mine], w, k - 1, (px, py, self.c))

    def _landed(self, w, k, h):
        return self.outs[w].at[self.me ^ k, self._half(w, h)]

    def _fwd(self, w, k):
        landed = self._landed(w, k, self.c)
        return self._push(landed, landed, w, 2 + k, (self.x, self.y, 1 - self.c))

    def start(self):
        for w in range(len(self.ins)):
            self._local(w).start()
            for k in (1, 2, 3):
                self._ici(w, k).start()

    def forward(self):
        for w in range(len(self.ins)):
            for k in (1, 2, 3):
                landed = self._landed(w, k, self.c)
                self._push(landed, landed, w, k - 1, (self.x, self.y, self.c)).wait_recv()
                self._fwd(w, k).start()

    def finish(self):
        for w in range(len(self.ins)):
            for k in (1, 2, 3):
                landed = self._landed(w, k, 1 - self.c)
                self._push(landed, landed, w, 2 + k, (self.x, self.y, self.c)).wait_recv()
            for k in (1, 2, 3):
                self._ici(w, k).wait_send()
                self._fwd(w, k).wait_send()
            self._local(w).wait()


def _alibi(h):
    return 2.0 ** (-(h + 1))


CHUNK = 2048


def _mask_table():
    slopes = jnp.asarray([_alibi(h) for h in range(N_HEADS)], F32)[:, None, None]
    step = jnp.arange(QBLK)[:, None] + QBLK - jnp.arange(2 * QBLK)[None, :]
    valid = (step >= 0) & (step <= QBLK)
    tab = jnp.stack([jnp.where(valid[None], -slopes * (step * d)[None].astype(F32), NEG) for d in DILATIONS])
    return tab.reshape(3, N_HEADS // 2, 2 * QBLK, 2 * QBLK)


def _attn_fwd(qkn, qkv, mb, late=()):
    t = qkn.shape[0]
    nc = t // CHUNK
    nl = len(late)

    def body(*refs):
        qc_ref, kp_ref, kc_ref, vp_ref, vc_ref, mb_ref = refs[0:6]
        o_ref, l_ref = refs[6 + nl:8 + nl]
        ob0, ob1, ob2, lb0, lb1, lb2 = refs[8 + 2 * nl:14 + 2 * nl]
        if nl:
            gather = _Gather(refs[6:6 + nl], refs[8 + nl:8 + 2 * nl], *refs[14 + 2 * nl:17 + 2 * nl])
            step = pl.program_id(0) * nc + pl.program_id(1)
            pl.when(step == 0)(gather.start)
            pl.when(step == 2 * nc)(gather.forward)
            pl.when(step == (N_HEADS // 2) * nc - 1)(gather.finish)
        first = pl.program_id(1) == 0
        lane = lax.broadcasted_iota(jnp.int32, (QBLK, 128), 1)
        lo_half = lane < HEAD_DIM
        kj = lax.broadcasted_iota(jnp.int32, (2 * QBLK, 2 * QBLK), 1)
        no_prev = first & (kj < QBLK)
        obs, lbs = (ob0, ob1, ob2), (lb0, lb1, lb2)

        def by_head(a):
            return jnp.where(lo_half, a, 0.0).astype(BF), jnp.where(lo_half, 0.0, a).astype(BF)

        for di, d in enumerate(DILATIONS):
            span = d * QBLK
            for r in range(d):
                tail = pl.ds(CHUNK - span + r, QBLK, stride=d)
                k_prev = kp_ref[tail, :].astype(BF)
                v_prev = by_head(vp_ref[tail, :])
                for b in range(CHUNK // span):
                    rows = pl.ds(r + span * b, QBLK, stride=d)
                    q0, q1 = by_head(qc_ref[rows, :])
                    k_cur = kc_ref[rows, :].astype(BF)
                    v_cur = by_head(vc_ref[rows, :])
                    s_all = _mm_nt(jnp.concatenate([q0, q1], axis=0), jnp.concatenate([k_prev, k_cur], axis=0))
                    es, invs, lses = [], [], []
                    for hh in range(2):
                        r0 = hh * QBLK
                        s = s_all[r0:r0 + QBLK] + mb_ref[di, 0, r0:r0 + QBLK, :]
                        if b == 0:
                            s = jnp.where(no_prev[0:QBLK], NEG, s)
                        m = jnp.max(s, axis=-1, keepdims=True)
                        e = jnp.exp(s - m)
                        den = jnp.sum(e, axis=-1, keepdims=True)
                        es.append(e.astype(BF))
                        invs.append(1.0 / den)
                        lses.append(m + jnp.log(den))
                    o = _mm(jnp.concatenate(es, axis=1),
                            jnp.concatenate([v_prev[0], v_cur[0], v_prev[1], v_cur[1]], axis=0))
                    obs[di][rows, :] = o * jnp.where(lo_half, invs[0], invs[1])
                    lbs[di][rows, :] = jnp.where(lo_half, lses[0], lses[1])
                    k_prev, v_prev = k_cur, v_cur
        for c0 in range(0, CHUNK, 256):
            rs = slice(c0, c0 + 256)
            l0, l1, l2 = lb0[rs, :], lb1[rs, :], lb2[rs, :]
            mx = jnp.maximum(jnp.maximum(l0, l1), l2)
            w0, w1, w2 = jnp.exp(l0 - mx), jnp.exp(l1 - mx), jnp.exp(l2 - mx)
            tot = w0 + w1 + w2
            o_ref[rs, :] = (ob0[rs, :] * w0 + ob1[rs, :] * w1 + ob2[rs, :] * w2) / tot
            l_ref[rs, :] = mx + jnp.log(tot)

    def cur(col):
        return pl.BlockSpec((CHUNK, 128), lambda hp, n: (n, col + hp))

    def prv(col):
        return pl.BlockSpec((CHUNK, 128), lambda hp, n: (jnp.maximum(n - 1, 0), col + hp))

    out = pl.BlockSpec((CHUNK, 128), lambda hp, n: (n, hp))
    any_spec = pl.BlockSpec(memory_space=pl.ANY)
    sems = [pltpu.SemaphoreType.DMA((6 * nl,)), pltpu.SemaphoreType.DMA((6 * nl,)), pltpu.SemaphoreType.DMA((nl,))]
    res = pl.pallas_call(
        body, name="attn_fwd", grid=(N_HEADS // 2, nc),
        in_specs=[cur(0), prv(4), cur(4), prv(8), cur(8),
                  pl.BlockSpec((3, 1, 2 * QBLK, 2 * QBLK), lambda hp, n: (0, hp, 0, 0))] + [any_spec] * nl,
        out_specs=[out, out] + [any_spec] * nl,
        out_shape=[jax.ShapeDtypeStruct((t, ATTN_W), F32)] * 2
        + [jax.ShapeDtypeStruct((N_CHIPS,) + w.shape, w.dtype) for w in late],
        scratch_shapes=[pltpu.VMEM((CHUNK, 128), F32)] * 6 + (sems if nl else []),
        compiler_params=_params(48, 2),
    )(qkn, qkn, qkn, qkv, qkv, mb, *late)
    return res[0], res[1], list(res[2:])


def _attn_bwd(qkn, qkv, o, lse, do, mb, riders=()):
    t = qkn.shape[0]
    nc = t // CHUNK
    nr = len(riders)

    def body(*refs):
        (qc_ref, qn_ref, kp_ref, kc_ref, vp_ref, vc_ref, oc_ref, on_ref, lc_ref, ln_ref, dc_ref, dn_ref,
         mb_ref) = refs[0:13]
        dq_ref, dk_ref, dv_ref = refs[13 + nr:16 + nr]
        if nr:
            step = pl.program_id(0) * nc + pl.program_id(1)
            _ride_scatter(step == 0, step == (N_HEADS // 2) * nc - 1, riders, refs[13:13 + nr],
                          refs[16 + nr:16 + 2 * nr], *refs[16 + 2 * nr:18 + 2 * nr])
        first = pl.program_id(1) == 0
        last = pl.program_id(1) == nc - 1
        lane = lax.broadcasted_iota(jnp.int32, (QBLK, 128), 1)
        lo_half = lane < HEAD_DIM
        kj = lax.broadcasted_iota(jnp.int32, (2 * QBLK, 2 * QBLK), 1)
        no_prev = first & (kj < QBLK)

        def by_head(a):
            return jnp.where(lo_half, a, 0.0).astype(BF), jnp.where(lo_half, 0.0, a).astype(BF)

        def query_side(q_ref, d_ref, o_ref_, l_ref_, rows):
            dvals = d_ref[rows, :]
            dd = dvals * o_ref_[rows, :]
            lv = l_ref_[rows, :]
            d0 = jnp.sum(jnp.where(lo_half, dd, 0.0), axis=-1, keepdims=True)
            d1 = jnp.sum(jnp.where(lo_half, 0.0, dd), axis=-1, keepdims=True)
            l0 = jnp.max(jnp.where(lo_half, lv, NEG), axis=-1, keepdims=True)
            l1 = jnp.max(jnp.where(lo_half, NEG, lv), axis=-1, keepdims=True)
            return (jnp.concatenate(by_head(q_ref[rows, :]), axis=0), jnp.concatenate(by_head(dvals), axis=0),
                    jnp.concatenate([l0, l1], axis=0), jnp.concatenate([d0, d1], axis=0))

        def tile(qs, dos, lcol, dcol, keys, vals, bias, dead):
            s = _mm_nt(qs, keys) + bias
            if dead is not None:
                s = jnp.where(dead, NEG, s)
            p = jnp.exp(s - lcol)
            ds = p * (_mm_nt(dos, vals) - dcol)
            return p.astype(BF), ds.astype(BF)

        def put(ref, di, rows, val):
            if di == 0:
                ref[rows, :] = val
            else:
                ref[rows, :] = ref[rows, :] + val

        for di, d in enumerate(DILATIONS):
            span = d * QBLK
            nbk = CHUNK // span
            for r in range(d):
                tail = pl.ds(CHUNK - span + r, QBLK, stride=d)
                k_prev = kp_ref[tail, :]
                kb_prev, km_prev = k_prev.astype(BF), by_head(k_prev)
                vb_prev = vp_ref[tail, :].astype(BF)
                rows_prev, dk_part, dv_part = None, None, None
                for b in range(nbk):
                    rows = pl.ds(r + span * b, QBLK, stride=d)
                    qs, dos, lcol, dcol = query_side(qc_ref, dc_ref, oc_ref, lc_ref, rows)
                    k_cur = kc_ref[rows, :]
                    kb_cur, km_cur = k_cur.astype(BF), by_head(k_cur)
                    vb_cur = vc_ref[rows, :].astype(BF)
                    p, ds = tile(qs, dos, lcol, dcol, jnp.concatenate([kb_prev, kb_cur], axis=0),
                                 jnp.concatenate([vb_prev, vb_cur], axis=0), mb_ref[di, 0],
                                 no_prev if b == 0 else None)
                    put(dq_ref, di, rows,
                        _mm(jnp.concatenate([ds[0:QBLK], ds[QBLK:2 * QBLK]], axis=1),
                            jnp.concatenate([km_prev[0], km_cur[0], km_prev[1], km_cur[1]], axis=0)))
                    dk2 = _mm_tn(ds, qs)
                    dv2 = _mm_tn(p, dos)
                    if b > 0:
                        put(dk_ref, di, rows_prev, dk_part + dk2[0:QBLK])
                        put(dv_ref, di, rows_prev, dv_part + dv2[0:QBLK])
                    rows_prev, dk_part, dv_part = rows, dk2[QBLK:2 * QBLK], dv2[QBLK:2 * QBLK]
                    kb_prev, km_prev, vb_prev = kb_cur, km_cur, vb_cur
                qs, dos, lcol, dcol = query_side(qn_ref, dn_ref, on_ref, ln_ref, pl.ds(r, QBLK, stride=d))
                p, ds = tile(qs, dos, lcol, dcol, kb_prev, vb_prev, mb_ref[di, 0, :, 0:QBLK], last)
                put(dk_ref, di, rows_prev, dk_part + _mm_tn(ds, qs))
                put(dv_ref, di, rows_prev, dv_part + _mm_tn(p, dos))

    def at(shift, col):
        return pl.BlockSpec((CHUNK, 128), lambda hp, n: (jnp.clip(n + shift, 0, nc - 1), col + hp))

    out = pl.BlockSpec((CHUNK, 128), lambda hp, n: (n, hp))
    r_in, r_out, r_sems = _rider_specs(riders)
    res = pl.pallas_call(
        body, name="attn_bwd", grid=(N_HEADS // 2, nc),
        in_specs=[at(0, 0), at(1, 0), at(-1, 4), at(0, 4), at(-1, 8), at(0, 8),
                  at(0, 0), at(1, 0), at(0, 0), at(1, 0), at(0, 0), at(1, 0),
                  pl.BlockSpec((3, 1, 2 * QBLK, 2 * QBLK), lambda hp, n: (0, hp, 0, 0))] + r_in,
        out_specs=[out, out, out] + r_in,
        out_shape=[jax.ShapeDtypeStruct((t, ATTN_W), F32)] * 3 + r_out,
        scratch_shapes=r_sems,
        compiler_params=_params(56, 2),
    )(qkn, qkn, qkn, qkn, qkv, qkv, o, o, lse, lse, do, do, mb, *[g for g, _ in riders])
    return res[0], res[1], res[2], list(res[3:])


def _fwd_ffn(x, ycn, ya, wout, wg4, wu4, g_oa, g_ffn, fcw, fcb, tm, late=()):
    t = x.shape[0]
    nt = t // tm
    nl = len(late)

    def body(*refs):
        x_ref, ycn_ref, ya_ref, wout_ref, wg_ref, wu_ref, goa_ref, gffn_ref, fcw_ref, fcb_ref = refs[0:10]
        x1_ref, gp_ref, up_ref, gate_ref, act_ref, ycat_ref, h2_ref = refs[10 + nl:17 + nl]
        cbuf = refs[17 + 2 * nl]
        if nl:
            gather = _Gather(refs[10:10 + nl], refs[17 + nl:17 + 2 * nl], *refs[18 + 2 * nl:21 + 2 * nl])
            pl.when(pl.program_id(0) == 0)(gather.start)
            pl.when(pl.program_id(0) == nt // 2)(gather.forward)
            pl.when(pl.program_id(0) == nt - 1)(gather.finish)

        @pl.when(pl.program_id(0) == 0)
        def _():
            cbuf[0:8, :] = jnp.zeros((8, D_FF), F32)

        yat = ya_ref[...]
        yan = ((yat * _rstd(yat)) * goa_ref[...]).astype(BF)
        ycn = ycn_ref[...]
        ycat_ref[:, 0:CONV_W] = ycn
        ycat_ref[:, CONV_W:D_MODEL] = yan
        x1 = x_ref[...] + _mm(ycn, wout_ref[0:CONV_W, :]) + _mm(yan, wout_ref[CONV_W:D_MODEL, :])
        x1_ref[...] = x1
        h2 = ((x1 * _rstd(x1)) * gffn_ref[...]).astype(BF)
        h2_ref[...] = h2
        for lo, hi in FF_SLABS:
            gps = _mm_nt(h2, wg_ref[lo:hi, :])
            ups = _mm_nt(h2, wu_ref[lo:hi, :])
            gp_ref[:, lo:hi] = gps.astype(BF)
            up_ref[:, lo:hi] = ups.astype(BF)
            cbuf[8:8 + tm, lo:hi] = gps
            cb = cbuf[:, lo:hi]
            gate = (fcw_ref[0:1, lo:hi] * _shift_down(cb, 2, tm) + fcw_ref[1:2, lo:hi] * _shift_down(cb, 1, tm)
                    + fcw_ref[2:3, lo:hi] * gps + fcb_ref[:, lo:hi])
            gate_ref[:, lo:hi] = gate.astype(BF)
            act_ref[:, lo:hi] = ((gate * jax.nn.sigmoid(gate)) * ups).astype(BF)
        cbuf[0:8, :] = cbuf[tm:tm + 8, :]

    any_spec = pl.BlockSpec(memory_space=pl.ANY)
    sems = [pltpu.SemaphoreType.DMA((6 * nl,)), pltpu.SemaphoreType.DMA((6 * nl,)), pltpu.SemaphoreType.DMA((nl,))]
    res = pl.pallas_call(
        body, name="fwd_ffn", grid=(nt,),
        in_specs=[_rows(tm, D_MODEL), _rows(tm, CONV_W), _rows(tm, ATTN_W), _const((D_MODEL, D_MODEL)),
                  _const((D_FF, D_MODEL)), _const((D_FF, D_MODEL)),
                  _const((1, ATTN_W)), _const((1, D_MODEL)), _const((3, D_FF)), _const((1, D_FF))]
        + [any_spec] * nl,
        out_specs=[_rows(tm, D_MODEL), _rows(tm, D_FF), _rows(tm, D_FF), _rows(tm, D_FF), _rows(tm, D_FF),
                   _rows(tm, D_MODEL), _rows(tm, D_MODEL)] + [any_spec] * nl,
        out_shape=[jax.ShapeDtypeStruct((t, D_MODEL), F32), jax.ShapeDtypeStruct((t, D_FF), BF),
                   jax.ShapeDtypeStruct((t, D_FF), BF), jax.ShapeDtypeStruct((t, D_FF), BF),
                   jax.ShapeDtypeStruct((t, D_FF), BF),
                   jax.ShapeDtypeStruct((t, D_MODEL), BF), jax.ShapeDtypeStruct((t, D_MODEL), BF)]
        + [jax.ShapeDtypeStruct((N_CHIPS,) + w.shape, w.dtype) for w in late],
        scratch_shapes=[pltpu.VMEM((tm + 8, D_FF), F32)] + (sems if nl else []),
        compiler_params=_params(56),
    )(x, ycn, ya, wout, wg4, wu4, g_oa, g_ffn, fcw, fcb, *late)
    return tuple(res[0:7]) + (list(res[7:]),)


def _fwd_tail(x1, act, p, target, wd4, wpg, wpp4, g_ple, tm):
    t = x1.shape[0]
    nt = t // tm

    def body(x1_ref, act_ref, p_ref, tgt_ref, wd_ref, wpg_ref, wpp_ref, g_ref,
             dx2_ref, h3_ref, ds_ref, dpp_ref, dg_ref, loss_ref, lacc):
        i = pl.program_id(0)

        @pl.when(i == 0)
        def _():
            dg_ref[...] = jnp.zeros_like(dg_ref)
            lacc[...] = jnp.zeros_like(lacc)

        x2 = x1_ref[...]
        for lo, hi in FF_SLABS:
            x2 = x2 + _mm(act_ref[:, lo:hi], wd_ref[lo:hi, :])
        r3 = _rstd(x2)
        xh = x2 * r3
        h3 = (xh * g_ref[...]).astype(BF)
        h3_ref[...] = h3
        sg = jax.nn.sigmoid(_mm(h3, wpg_ref[...]))
        pb = p_ref[...].astype(BF)
        pp = jnp.concatenate([_mm(pb, wpp_ref[s]) for s in range(N_CHIPS)], axis=1)
        err = (x2 + sg * pp) - tgt_ref[...]
        lacc[...] += _colsum(err * err)
        dx3 = err * (1.0 / D_MODEL)
        dpp_ref[...] = (dx3 * sg).astype(BF)
        dsb = ((dx3 * pp) * (sg * (1.0 - sg))).astype(BF)
        ds_ref[...] = dsb
        dh3 = _mm_nt(dsb, wpg_ref[...])
        dg_ref[...] += _colsum(dh3 * xh)
        dx2_ref[...] = dx3 + _norm_bwd(dh3, xh, r3, g_ref[...])

        @pl.when(i == nt - 1)
        def _():
            loss_ref[...] = jnp.full((1, 128), jnp.sum(lacc[...]) * (0.5 / D_MODEL), F32)

    return pl.pallas_call(
        body, name="fwd_tail", grid=(nt,),
        in_specs=[_rows(tm, D_MODEL), _rows(tm, D_FF), _rows(tm, PLE_DIM), _rows(tm, D_MODEL),
                  _const((D_FF, D_MODEL)), _const((D_MODEL, D_MODEL)),
                  _const((N_CHIPS, PLE_DIM, PLE_DIM)), _const((1, D_MODEL))],
        out_specs=[_rows(tm, D_MODEL), _rows(tm, D_MODEL), _rows(tm, D_MODEL), _rows(tm, D_MODEL),
                   pl.BlockSpec((1, D_MODEL), lambda i: (0, 0)), pl.BlockSpec((1, 128), lambda i: (0, 0))],
        out_shape=[jax.ShapeDtypeStruct((t, D_MODEL), F32), jax.ShapeDtypeStruct((t, D_MODEL), BF),
                   jax.ShapeDtypeStruct((t, D_MODEL), BF), jax.ShapeDtypeStruct((t, D_MODEL), BF),
                   jax.ShapeDtypeStruct((1, D_MODEL), F32), jax.ShapeDtypeStruct((1, 128), F32)],
        scratch_shapes=[pltpu.VMEM((1, D_MODEL), F32)],
        compiler_params=_params(48),
    )(x1, act, p, target, wd4, wpg, wpp4, g_ple)


def _bwd_ffn_a(dx2, gate, gp, up, wd4, fcw, tm, riders=()):
    t = dx2.shape[0]
    nt = t // tm
    nr = len(riders)

    def body(*refs):
        dx2_ref, gate_ref, gp_ref, up_ref, wd_ref, fcw_ref = refs[0:6]
        dgp_ref, dup_ref, dfcw_ref, dfcb_ref = refs[6 + nr:10 + nr]
        dbuf = refs[10 + 2 * nr]
        i = pl.program_id(0)
        if nr:
            _ride_scatter(i == 0, i == nt - 1, riders, refs[6:6 + nr], refs[10 + nr:10 + 2 * nr],
                          *refs[11 + 2 * nr:13 + 2 * nr])

        @pl.when(i == 0)
        def _():
            dbuf[tm:tm + 8, :] = jnp.zeros((8, D_FF), F32)
            dfcw_ref[...] = jnp.zeros_like(dfcw_ref)
            dfcb_ref[...] = jnp.zeros_like(dfcb_ref)

        dx2b = dx2_ref[...].astype(BF)
        for lo, hi in FF_SLABS:
            gate = gate_ref[:, lo:hi].astype(F32)
            gps = gp_ref[:, lo:hi].astype(F32)
            w0, w1, w2 = fcw_ref[0:1, lo:hi], fcw_ref[1:2, lo:hi], fcw_ref[2:3, lo:hi]
            sg = jax.nn.sigmoid(gate)
            dact = _mm_nt(dx2b, wd_ref[lo:hi, :])
            dup_ref[:, lo:hi] = (dact * (gate * sg)).astype(BF)
            dgate = (dact * up_ref[:, lo:hi].astype(F32)) * (sg * (1.0 + gate * (1.0 - sg)))
            dbuf[0:tm, lo:hi] = dgate
            db = dbuf[:, lo:hi]
            d1 = _shift_up(db, 1, tm)
            d2 = _shift_up(db, 2, tm)
            dfcb_ref[:, lo:hi] += _colsum(dgate)
            dfcw_ref[0:1, lo:hi] += _colsum(d2 * gps)
            dfcw_ref[1:2, lo:hi] += _colsum(d1 * gps)
            dfcw_ref[2:3, lo:hi] += _colsum(dgate * gps)
            dgp_ref[:, lo:hi] = (w2 * dgate + w1 * d1 + w0 * d2).astype(BF)
        dbuf[tm:tm + 8, :] = dbuf[0:8, :]

    r_in, r_out, r_sems = _rider_specs(riders)
    res = pl.pallas_call(
        body, name="bwd_ffn_a", grid=(nt,),
        in_specs=[_rows(tm, D_MODEL, nt), _rows(tm, D_FF, nt), _rows(tm, D_FF, nt), _rows(tm, D_FF, nt),
                  _const((D_FF, D_MODEL)), _const((3, D_FF))] + r_in,
        out_specs=[_rows(tm, D_FF, nt), _rows(tm, D_FF, nt),
                   pl.BlockSpec((3, D_FF), lambda i: (0, 0)), pl.BlockSpec((1, D_FF), lambda i: (0, 0))] + r_in,
        out_shape=[jax.ShapeDtypeStruct((t, D_FF), BF), jax.ShapeDtypeStruct((t, D_FF), BF),
                   jax.ShapeDtypeStruct((3, D_FF), F32), jax.ShapeDtypeStruct((1, D_FF), F32)] + r_out,
        scratch_shapes=[pltpu.VMEM((tm + 8, D_FF), F32)] + r_sems,
        compiler_params=_params(56),
    )(dx2, gate, gp, up, wd4, fcw, *[g for g, _ in riders])
    return res[0], res[1], res[2], res[3], list(res[4:])


def _bwd_ffn_b(dgp, dup, dx2, x1, ya, wg4, wu4, wout, g_ffn, g_oa, tm):
    t = dx2.shape[0]
    nt = t // tm

    def body(dgp_ref, dup_ref, dx2_ref, x1_ref, ya_ref, wg_ref, wu_ref, wout_ref, gffn_ref, goa_ref,
             dx1_ref, dycn_ref, dya_ref, dgffn_ref, dgoa_ref):
        @pl.when(pl.program_id(0) == 0)
        def _():
            dgffn_ref[...] = jnp.zeros_like(dgffn_ref)
            dgoa_ref[...] = jnp.zeros_like(dgoa_ref)

        dh2 = jnp.zeros((tm, D_MODEL), F32)
        for lo, hi in FF_SLABS:
            dh2 = dh2 + _mm(dgp_ref[:, lo:hi], wg_ref[lo:hi, :]) + _mm(dup_ref[:, lo:hi], wu_ref[lo:hi, :])
        x1 = x1_ref[...]
        r2 = _rstd(x1)
        xh = x1 * r2
        dgffn_ref[...] += _colsum(dh2 * xh)
        dx1 = dx2_ref[...] + _norm_bwd(dh2, xh, r2, gffn_ref[...])
        dx1_ref[...] = dx1
        dy = _mm_nt(dx1.astype(BF), wout_ref[...])
        dycn_ref[...] = dy[:, 0:CONV_W]
        dyan = dy[:, CONV_W:D_MODEL]
        yat = ya_ref[...]
        ra = _rstd(yat)
        yah = yat * ra
        dgoa_ref[...] += _colsum(dyan * yah)
        dya_ref[...] = _norm_bwd(dyan, yah, ra, goa_ref[...])

    return pl.pallas_call(
        body, name="bwd_ffn_b", grid=(nt,),
        in_specs=[_rows(tm, D_FF), _rows(tm, D_FF), _rows(tm, D_MODEL), _rows(tm, D_MODEL),
                  _rows(tm, ATTN_W), _const((D_FF, D_MODEL)), _const((D_FF, D_MODEL)),
                  _const((D_MODEL, D_MODEL)), _const((1, D_MODEL)), _const((1, ATTN_W))],
        out_specs=[_rows(tm, D_MODEL), _rows(tm, CONV_W), _rows(tm, ATTN_W),
                   pl.BlockSpec((1, D_MODEL), lambda i: (0, 0)), pl.BlockSpec((1, ATTN_W), lambda i: (0, 0))],
        out_shape=[jax.ShapeDtypeStruct((t, D_MODEL), F32), jax.ShapeDtypeStruct((t, CONV_W), F32),
                   jax.ShapeDtypeStruct((t, ATTN_W), F32),
                   jax.ShapeDtypeStruct((1, D_MODEL), F32), jax.ShapeDtypeStruct((1, ATTN_W), F32)],
        compiler_params=_params(48),
    )(dgp, dup, dx2, x1, ya, wg4, wu4, wout, g_ffn, g_oa)


def _bwd_mix(x, zbcx, qkv, dycn, dq, dk, dv, conv_w, conv_b, g_oc, g_mix, gm, gq8, gk8, tm):
    t = x.shape[0]
    nt = t // tm

    def body(x_ref, z_ref, zh_ref, qkv_ref, dycn_ref, dq_ref, dk_ref, dv_ref, cw_ref, cb_ref,
             goc_ref, g_ref, gm_ref, gq_ref, gk_ref,
             dz_ref, dcw_ref, dcb_ref, dgoc_ref, dgq_ref, dgk_ref, gw32_ref, gw16_ref,
             ubuf, dbuf, wacc, wstage, osem):
        i = pl.program_id(0)

        @pl.when(i == 0)
        def _():
            wacc[...] = jnp.zeros_like(wacc)
            dbuf[tm:tm + 8, :] = jnp.zeros((8, CONV_W), F32)
            dcw_ref[...] = jnp.zeros_like(dcw_ref)
            dcb_ref[...] = jnp.zeros_like(dcb_ref)
            dgoc_ref[...] = jnp.zeros_like(dgoc_ref)
            dgq_ref[...] = jnp.zeros_like(dgq_ref)
            dgk_ref[...] = jnp.zeros_like(dgk_ref)

        not_first_tile = i < nt - 1
        zb = z_ref[:, 0:512]
        zc = z_ref[:, 512:1024]
        zx = z_ref[:, 1024:1536]
        u = zc * zx
        ubuf[0:8, :] = jnp.where(not_first_tile, zh_ref[:, 512:1024] * zh_ref[:, 1024:1536], 0.0)
        ubuf[8:8 + tm, :] = u
        ub = ubuf[...]
        u1 = _shift_down(ub, 1, tm)
        u2 = _shift_down(ub, 2, tm)
        w0, w1, w2 = cw_ref[0:1, :], cw_ref[1:2, :], cw_ref[2:3, :]
        cv = w0 * u2 + w1 * u1 + w2 * u + cb_ref[...]
        yc = zb * cv
        rc = _rstd(yc)
        ych = yc * rc
        dycn = dycn_ref[...]
        dgoc_ref[...] += _colsum(dycn * ych)
        dyc = _norm_bwd(dycn, ych, rc, goc_ref[...])
        dcv = dyc * zb
        dcb_ref[...] += _colsum(dcv)
        dcw_ref[0:1, :] += _colsum(dcv * u2)
        dcw_ref[1:2, :] += _colsum(dcv * u1)
        dcw_ref[2:3, :] += _colsum(dcv * u)
        dbuf[0:tm, :] = dcv
        db = dbuf[...]
        du = w2 * dcv + w1 * _shift_up(db, 1, tm) + w0 * _shift_up(db, 2, tm)
        dbuf[tm:tm + 8, :] = dbuf[0:8, :]
        dz_ref[:, 0:512] = (dyc * cv).astype(BF)
        dz_ref[:, 512:1024] = (du * zx).astype(BF)
        dz_ref[:, 1024:1536] = (du * zc).astype(BF)
        for z0, d_ref, gg_ref, acc_ref, sc in ((0, dq_ref, gq_ref, dgq_ref, HEAD_DIM ** -0.5),
                                               (512, dk_ref, gk_ref, dgk_ref, 1.0)):
            z = qkv_ref[:, z0:z0 + 512]
            rr = lax.rsqrt(_head_mean(z * z, gm_ref) + EPS)
            zh = z * rr
            dn = d_ref[...] * sc
            acc_ref[...] += _colsum(dn * zh)
            dzh = dn * gg_ref[...]
            dz_ref[:, 1536 + z0:1536 + z0 + 512] = (rr * (dzh - zh * _head_mean(dzh * zh, gm_ref))).astype(BF)
        dz_ref[:, 2560:3072] = dv_ref[...].astype(BF)
        xt = x_ref[...]
        h1 = ((xt * _rstd(xt)) * g_ref[...]).astype(BF)
        for s in range(N_CHIPS):
            cols = slice(s * IN_SLAB, (s + 1) * IN_SLAB)
            wacc[:, cols] += _mm_tn(h1, dz_ref[:, cols])

        @pl.when(i == nt - 1)
        def _():
            wstage[...] = wacc[...].astype(BF)
            out32 = pltpu.make_async_copy(wacc, gw32_ref, osem.at[0])
            out16 = pltpu.make_async_copy(wstage, gw16_ref, osem.at[1])
            out32.start()
            out16.start()
            out32.wait()
            out16.wait()

    def acc(width, rows=1):
        return pl.BlockSpec((rows, width), lambda i: (0, 0))

    return pl.pallas_call(
        body, name="bwd_mix", grid=(nt,),
        in_specs=[_rows(tm, D_MODEL, nt), _rows(tm, 1536, nt), _halo(tm, 1536, nt),
                  _rows(tm, 1536, nt), _rows(tm, CONV_W, nt), _rows(tm, ATTN_W, nt), _rows(tm, ATTN_W, nt),
                  _rows(tm, ATTN_W, nt),
                  _const((3, CONV_W)), _const((1, CONV_W)), _const((1, CONV_W)), _const((1, D_MODEL)),
                  _const((ATTN_W, ATTN_W)), _const((1, ATTN_W)), _const((1, ATTN_W))],
        out_specs=[_rows(tm, 3072, nt),
                   acc(CONV_W, 3), acc(CONV_W), acc(CONV_W), acc(ATTN_W), acc(ATTN_W),
                   pl.BlockSpec(memory_space=pl.ANY), pl.BlockSpec(memory_space=pl.ANY)],
        out_shape=[jax.ShapeDtypeStruct((t, 3072), BF), jax.ShapeDtypeStruct((3, CONV_W), F32),
                   jax.ShapeDtypeStruct((1, CONV_W), F32), jax.ShapeDtypeStruct((1, CONV_W), F32),
                   jax.ShapeDtypeStruct((1, ATTN_W), F32), jax.ShapeDtypeStruct((1, ATTN_W), F32),
                   jax.ShapeDtypeStruct((D_MODEL, 3072), F32), jax.ShapeDtypeStruct((D_MODEL, 3072), BF)],
        scratch_shapes=[pltpu.VMEM((tm + 8, CONV_W), F32), pltpu.VMEM((tm + 8, CONV_W), F32),
                        pltpu.VMEM((D_MODEL, 3072), F32), pltpu.VMEM((D_MODEL, 3072), BF),
                        pltpu.SemaphoreType.DMA((2,))],
        compiler_params=_params(56),
    )(x, zbcx, zbcx, qkv, dycn, dq, dk, dv, conv_w, conv_b, g_oc, g_mix, gm, gq8, gk8)


def _bwd_in(x, dx1, dz, win4, g_mix, tm, riders=()):
    t = x.shape[0]
    nt = t // tm
    nr = len(riders)

    def body(*refs):
        x_ref, dx1_ref, dz_ref, w_ref, g_ref = refs[0:5]
        gx_ref, dg_ref = refs[5 + nr:7 + nr]
        i = pl.program_id(0)
        if nr:
            _ride_scatter(i == 0, i == nt - 1, riders, refs[5:5 + nr], refs[7 + nr:7 + 2 * nr],
                          *refs[7 + 2 * nr:9 + 2 * nr])

        @pl.when(i == 0)
        def _():
            dg_ref[...] = jnp.zeros_like(dg_ref)

        dh1 = jnp.zeros((tm, D_MODEL), F32)
        for s in range(N_CHIPS):
            dh1 = dh1 + _mm_nt(dz_ref[:, s * IN_SLAB:(s + 1) * IN_SLAB], w_ref[s])
        xt = x_ref[...]
        r1 = _rstd(xt)
        xh = xt * r1
        dg_ref[...] += _colsum(dh1 * xh)
        gx_ref[...] = dx1_ref[...] + _norm_bwd(dh1, xh, r1, g_ref[...])

    r_in, r_out, r_sems = _rider_specs(riders)
    res = pl.pallas_call(
        body, name="bwd_in", grid=(nt,),
        in_specs=[_rows(tm, D_MODEL), _rows(tm, D_MODEL), _rows(tm, 3072), _const((N_CHIPS, D_MODEL, IN_SLAB)),
                  _const((1, D_MODEL))] + r_in,
        out_specs=[_rows(tm, D_MODEL), pl.BlockSpec((1, D_MODEL), lambda i: (0, 0))] + r_in,
        out_shape=[jax.ShapeDtypeStruct((t, D_MODEL), F32), jax.ShapeDtypeStruct((1, D_MODEL), F32)] + r_out,
        scratch_shapes=r_sems,
        compiler_params=_params(48),
    )(x, dx1, dz, win4, g_mix, *[g for g, _ in riders])
    return res[0], res[1], list(res[2:])


def _wgrad(a, b, tn, tt, name):
    t, k = a.shape
    n = b.shape[1]
    nt = t // tt

    def body(a_ref, b_ref, o_ref, ob_ref):
        @pl.when(pl.program_id(1) == 0)
        def _():
            o_ref[...] = jnp.zeros_like(o_ref)

        o_ref[...] += _mm_tn(a_ref[...].astype(BF), b_ref[...].astype(BF))

        @pl.when(pl.program_id(1) == nt - 1)
        def _():
            ob_ref[...] = o_ref[...].astype(BF)

    spec = pl.BlockSpec((k, tn), lambda j, i: (0, j))
    return pl.pallas_call(
        body, name=name, grid=(n // tn, nt),
        in_specs=[pl.BlockSpec((tt, k), lambda j, i: (i, 0)), pl.BlockSpec((tt, tn), lambda j, i: (i, j))],
        out_specs=[spec, spec],
        out_shape=[jax.ShapeDtypeStruct((k, n), F32), jax.ShapeDtypeStruct((k, n), BF)],
        compiler_params=_params(58, 2),
    )(a, b)


def _gather_weights(shards, pack):
    nw = len(shards)

    def body(*refs):
        ins = refs[:nw]
        pack_ref = refs[nw]
        outs = refs[nw + 1:2 * nw + 1]
        pack_out = refs[2 * nw + 1]
        send_sems, recv_sems, local_sems = refs[2 * nw + 2:]
        x, y, c = _place()
        me = 2 * x + y
        local, remote = [], []

        def sem(w, j):
            return w * 6 + j

        def push(src, dst, w, j, to):
            return pltpu.make_async_remote_copy(src_ref=src, dst_ref=dst, send_sem=send_sems.at[sem(w, j)],
                                                recv_sem=recv_sems.at[sem(w, j)], device_id=to, device_id_type=MESH)

        def half_rows(w, h):
            half = ins[w].shape[0] // 2
            return pl.ds(pl.multiple_of(h * half, 16), half)

        for w in range(nw):
            local.append(pltpu.make_async_copy(ins[w], outs[w].at[me], local_sems.at[w]))
            for k in (1, 2, 3):
                px, py = _chip_peer(x, y, k)
                mine = half_rows(w, c)
                remote.append(push(ins[w].at[mine], outs[w].at[me, mine], w, k - 1, (px, py, c)))
        local.append(pltpu.make_async_copy(pack_ref, pack_out.at[me], local_sems.at[nw]))
        for k in (1, 2, 3):
            px, py = _chip_peer(x, y, k)
            remote.append(push(pack_ref, pack_out.at[me], nw, k - 1, (px, py, c)))
        for cp in local + remote:
            cp.start()
        for w in range(nw):
            for k in (1, 2, 3):
                landed = outs[w].at[me ^ k, half_rows(w, c)]
                push(landed, landed, w, k - 1, (x, y, c)).wait_recv()
                fw = push(landed, landed, w, 2 + k, (x, y, 1 - c))
                fw.start()
                remote.append(fw)
        for k in (1, 2, 3):
            landed = pack_out.at[me ^ k]
            push(landed, landed, nw, k - 1, (x, y, c)).wait_recv()
        for w in range(nw):
            for k in (1, 2, 3):
                landed = outs[w].at[me ^ k, half_rows(w, 1 - c)]
                push(landed, landed, w, 2 + k, (x, y, c)).wait_recv()
        for cp in remote:
            cp.wait_send()
        for cp in local:
            cp.wait()

    any_spec = pl.BlockSpec(memory_space=pl.ANY)
    out_shape = [jax.ShapeDtypeStruct((N_CHIPS,) + s.shape, s.dtype) for s in shards]
    out_shape.append(jax.ShapeDtypeStruct((N_CHIPS,) + pack.shape, pack.dtype))
    return pl.pallas_call(
        body, name="gather_weights",
        in_specs=[any_spec] * (nw + 1), out_specs=[any_spec] * (nw + 1), out_shape=out_shape,
        scratch_shapes=[pltpu.SemaphoreType.DMA(((nw + 1) * 6,)), pltpu.SemaphoreType.DMA(((nw + 1) * 6,)),
                        pltpu.SemaphoreType.DMA((nw + 1,))],
    )(*shards, pack)


def _adamw(w, g, m, v):
    m = ADAM_B1 * m + (1.0 - ADAM_B1) * g
    v = ADAM_B2 * v + (1.0 - ADAM_B2) * (g * g)
    m_hat = m / (1.0 - ADAM_B1 ** ADAM_STEP)
    v_hat = v / (1.0 - ADAM_B2 ** ADAM_STEP)
    delta = -ADAM_LR * (m_hat / (jnp.sqrt(v_hat) + ADAM_EPS) + ADAM_WD * w)
    return delta, m, v


def _finish_reduce(grad, slots, col_sharded, name):
    r, cw = _piece_shape(grad.shape, col_sharded)
    chunk = 32
    assert r % chunk == 0

    def body(g_hbm, slots_ref, full, own, lsem, c_send, c_recv):
        x, y, c = _place()
        cp = pltpu.make_async_copy(g_hbm.at[_piece_window(col_sharded, r, cw, 2 * x + y, c)], own, lsem)
        cp.start()
        cp.wait()
        mine = pl.multiple_of(c * r, 8)

        def add(j, carry):
            rows = pl.ds(pl.multiple_of(j * chunk, 8), chunk)
            tot = own[rows, :]
            for k in range(7):
                tot = tot + slots_ref[k, rows, :].astype(F32)
            full[pl.ds(mine + pl.multiple_of(j * chunk, 8), chunk), :] = tot
            return carry

        lax.fori_loop(0, r // chunk, add, 0)
        half = full.at[pl.ds(mine, r), :]
        swap = pltpu.make_async_remote_copy(src_ref=half, dst_ref=half, send_sem=c_send, recv_sem=c_recv,
                                            device_id=(x, y, 1 - c), device_id_type=MESH)
        swap.start()
        swap.wait()

    vmem = pl.BlockSpec(memory_space=pltpu.VMEM)
    return pl.pallas_call(
        body, name=name, in_specs=[pl.BlockSpec(memory_space=pltpu.HBM), vmem], out_specs=vmem,
        out_shape=jax.ShapeDtypeStruct((2 * r, cw), F32),
        scratch_shapes=[pltpu.VMEM((r, cw), F32), pltpu.SemaphoreType.DMA, pltpu.SemaphoreType.DMA,
                        pltpu.SemaphoreType.DMA],
        compiler_params=pltpu.CompilerParams(vmem_limit_bytes=32 * MIB),
    )(grad, slots)


def _adamw_big(g, w, m, v, name):
    vr, vc = w.shape
    assert g.shape == w.shape
    rows = 64

    def body(g_ref, w_ref, m_ref, v_ref, go_ref, do_ref, mo_ref, vo_ref):
        gg = g_ref[...]
        delta, mn, vn = _adamw(w_ref[...], gg, m_ref[...], v_ref[...])
        go_ref[...] = gg
        do_ref[...] = delta
        mo_ref[...] = mn
        vo_ref[...] = vn

    blk = pl.BlockSpec((rows, vc), lambda i: (i, 0))
    shard = jax.ShapeDtypeStruct((vr, vc), F32)
    return pl.pallas_call(
        body, name=name, grid=(vr // rows,),
        in_specs=[blk, blk, blk, blk], out_specs=[blk] * 4,
        out_shape=[shard] * 4, compiler_params=_params(32),
    )(g, w, m, v)


def _allreduce_small(pack):
    rows = pack.shape[0]

    def body(p_ref, o_ref, slots, send_sems, recv_sems):
        x, y, c = _place()
        me = 4 * x + 2 * y + c
        slots[me] = p_ref[...]
        sends = []
        for k in range(1, 8):
            cp = pltpu.make_async_remote_copy(
                src_ref=p_ref, dst_ref=slots.at[me], send_sem=send_sems.at[k - 1], recv_sem=recv_sems.at[k - 1],
                device_id=(x ^ (k >> 2), y ^ ((k >> 1) & 1), c ^ (k & 1)), device_id_type=MESH)
            cp.start()
            sends.append(cp)
        for cp in sends:
            cp.wait()
        tot = slots[0]
        for j in range(1, 8):
            tot = tot + slots[j]
        o_ref[...] = tot

    vmem = pl.BlockSpec(memory_space=pltpu.VMEM)
    return pl.pallas_call(
        body, name="allreduce_small", in_specs=[vmem], out_specs=vmem,
        out_shape=jax.ShapeDtypeStruct(pack.shape, F32),
        scratch_shapes=[pltpu.VMEM((8, rows, D_MODEL), F32), pltpu.SemaphoreType.DMA((7,)),
                        pltpu.SemaphoreType.DMA((7,))],
    )(pack)


def _adamw_small(ws, gs, ms, vs):
    n = len(ws)

    def body(*refs):
        w_refs, g_refs, m_refs, v_refs = refs[0:n], refs[n:2 * n], refs[2 * n:3 * n], refs[3 * n:4 * n]
        d_refs, mo_refs, vo_refs = refs[4 * n:5 * n], refs[5 * n:6 * n], refs[6 * n:7 * n]
        for j in range(n):
            delta, mn, vn = _adamw(w_refs[j][...], g_refs[j][...], m_refs[j][...], v_refs[j][...])
            d_refs[j][...] = delta
            mo_refs[j][...] = mn
            vo_refs[j][...] = vn

    vmem = pl.BlockSpec(memory_space=pltpu.VMEM)
    shapes = [jax.ShapeDtypeStruct(w.shape, F32) for w in ws]
    outs = pl.pallas_call(
        body, name="adamw_small", in_specs=[vmem] * (4 * n), out_specs=[vmem] * (3 * n), out_shape=shapes * 3,
    )(*ws, *gs, *ms, *vs)
    return outs[0:n], outs[n:2 * n], outs[2 * n:3 * n]


def _local_step(x, p, target, wts, late=None):
    (win4, wout, wg4, wu4, wd4, wpg, wpp4, conv_w, fcw, g_mix, conv_b, gq, gk, g_oc, g_oa, g_ffn, fcb, g_ple) = wts
    comm = late is not None
    gm = jnp.kron(jnp.eye(N_HEADS, dtype=F32), jnp.full((HEAD_DIM, HEAD_DIM), 1.0 / HEAD_DIM, F32)).astype(BF)
    gq8, gk8 = jnp.tile(gq, (1, N_HEADS)), jnp.tile(gk, (1, N_HEADS))
    mb = _mask_table()
    zbcx, qkv, ycn, qkn = _fwd_mix(x, g_mix, win4, conv_w, conv_b, g_oc, gm, gq8, gk8, 512)
    ya, lse, gathered = _attn_fwd(qkn, qkv, mb, late[0:3] if comm else ())
    if comm:
        wout, wg4, wu4 = (g.reshape(-1, D_MODEL) for g in gathered)
    x1, gp, up, gate, act, ycat, h2, gathered = _fwd_ffn(x, ycn, ya, wout, wg4, wu4, g_oa, g_ffn, fcw, fcb, 256,
                                                    late[3:6] if comm else ())
    if comm:
        wd4, wpg, wpp4 = gathered
        wd4, wpg = wd4.reshape(D_FF, D_MODEL), wpg.reshape(D_MODEL, D_MODEL)
    dx2, h3, ds, dpp, dg_ple, loss = _fwd_tail(x1, act, p, target, wd4, wpg, wpp4, g_ple, 512)
    big, big16, slots = {}, {}, {}

    def wgrad(name, a, b, tn):
        big[name], big16[name] = _wgrad(a, b, tn, 1024, "wgrad_" + name)
        return (big16[name], _COL_SHARDED[name])

    riders = [wgrad("w_down", act, dx2, 1024), wgrad("w_ple_gate", h3, ds, 1024), wgrad("w_ple_proj", p, dpp, 1024)]
    dgp, dup, dfcw, dfcb, got = _bwd_ffn_a(dx2, gate, gp, up, wd4, fcw, 512, riders if comm else ())
    slots.update(zip(("w_down", "w_ple_gate", "w_ple_proj"), got))
    riders = [wgrad("w_gate", dgp, h2, 1024), wgrad("w_up", dup, h2, 1024)]
    dx1, dycn, dya, dg_ffn, dg_oa = _bwd_ffn_b(dgp, dup, dx2, x1, ya, wg4, wu4, wout, g_ffn, g_oa, 512)
    riders.append(wgrad("w_out", ycat, dx1, 1024))
    dq, dk, dv, got = _attn_bwd(qkn, qkv, ya, lse, dya, mb, riders if comm else ())
    slots.update(zip(("w_gate", "w_up", "w_out"), got))
    dz, dcw, dcb, dg_oc, dgq8, dgk8, big["w_in"], big16["w_in"] = _bwd_mix(
        x, zbcx, qkv, dycn, dq, dk, dv, conv_w, conv_b, g_oc, g_mix, gm, gq8, gk8, 512)
    riders = [(big16["w_in"], _COL_SHARDED["w_in"])]
    grad_x, dg_mix, got = _bwd_in(x, dx1, dz, win4, g_mix, 512, riders if comm else ())
    slots.update(zip(("w_in",), got))
    dgq = dgq8.reshape(N_HEADS, HEAD_DIM).sum(axis=0, keepdims=True)
    dgk = dgk8.reshape(N_HEADS, HEAD_DIM).sum(axis=0, keepdims=True)
    small = dict(g_mix=dg_mix, conv_w=dcw, conv_b=dcb, q_norm_g=dgq, k_norm_g=dgk, g_out_conv=dg_oc,
                 g_out_attn=dg_oa, g_ffn=dg_ffn, ffn_conv_w=dfcw, ffn_conv_b=dfcb, g_ple=dg_ple)
    return loss[0, 0], grad_x, big, slots, small


_SMALL_ROWS = 24


def _pack_small(s, loss):
    z64 = jnp.zeros((1, 1024 - 512 - 128), F32)
    rows = [s["g_mix"], s["g_ffn"], s["g_ple"],
            jnp.concatenate([s["conv_b"], s["g_out_conv"]], axis=1),
            jnp.concatenate([s["g_out_attn"], s["q_norm_g"], s["k_norm_g"], z64], axis=1),
            jnp.pad(s["conv_w"], ((0, 0), (0, 512))),
            jnp.pad(s["ffn_conv_b"], ((0, 0), (0, 3072 - D_FF))).reshape(3, 1024),
            jnp.pad(s["ffn_conv_w"], ((0, 0), (0, 3072 - D_FF))).reshape(9, 1024),
            jnp.pad(loss.reshape(1, 1), ((0, 0), (0, 1023))),
            jnp.zeros((_SMALL_ROWS - 21, 1024), F32)]
    return jnp.concatenate(rows, axis=0)


def _unpack_small(t):
    return dict(g_mix=t[0:1], g_ffn=t[1:2], g_ple=t[2:3], conv_b=t[3:4, 0:512], g_out_conv=t[3:4, 512:1024],
                g_out_attn=t[4:5, 0:512], q_norm_g=t[4:5, 512:576], k_norm_g=t[4:5, 576:640],
                conv_w=t[5:8, 0:512], ffn_conv_b=t[8:11].reshape(1, 3072)[:, :D_FF],
                ffn_conv_w=t[11:20].reshape(3, 3072)[:, :D_FF], loss=t[20, 0])


_BIG = ("w_in", "w_out", "w_gate", "w_up", "w_down", "w_ple_gate", "w_ple_proj")
_COL_SHARDED = dict(w_in=True, w_out=False, w_gate=False, w_up=False, w_down=False, w_ple_gate=False, w_ple_proj=True)
_TRANSPOSED = ("w_gate", "w_up")
_WEIGHTS = ("g_mix", "w_in", "conv_w", "conv_b", "q_norm_g", "k_norm_g", "g_out_conv", "g_out_attn", "w_out",
            "g_ffn", "w_gate", "w_up", "ffn_conv_w", "ffn_conv_b", "w_down", "g_ple", "w_ple_gate", "w_ple_proj")


def kernel(x, p, g_mix, w_in, conv_w, conv_b, q_norm_g, k_norm_g, g_out_conv, g_out_attn, w_out, g_ffn, w_gate, w_up, ffn_conv_w, ffn_conv_b, w_down, g_ple, w_ple_gate, w_ple_proj, loss_target, m_g_mix, m_w_in, m_conv_w, m_conv_b, m_q_norm_g, m_k_norm_g, m_g_out_conv, m_g_out_attn, m_w_out, m_g_ffn, m_w_gate, m_w_up, m_ffn_conv_w, m_ffn_conv_b, m_w_down, m_g_ple, m_w_ple_gate, m_w_ple_proj, v_g_mix, v_w_in, v_conv_w, v_conv_b, v_q_norm_g, v_k_norm_g, v_g_out_conv, v_g_out_attn, v_w_out, v_g_ffn, v_w_gate, v_w_up, v_ffn_conv_w, v_ffn_conv_b, v_w_down, v_g_ple, v_w_ple_gate, v_w_ple_proj):
    w = dict(g_mix=g_mix, w_in=w_in, conv_w=conv_w, conv_b=conv_b, q_norm_g=q_norm_g, k_norm_g=k_norm_g,
             g_out_conv=g_out_conv, g_out_attn=g_out_attn, w_out=w_out, g_ffn=g_ffn, w_gate=w_gate, w_up=w_up,
             ffn_conv_w=ffn_conv_w, ffn_conv_b=ffn_conv_b, w_down=w_down, g_ple=g_ple, w_ple_gate=w_ple_gate,
             w_ple_proj=w_ple_proj)
    m = dict(g_mix=m_g_mix, w_in=m_w_in, conv_w=m_conv_w, conv_b=m_conv_b, q_norm_g=m_q_norm_g, k_norm_g=m_k_norm_g,
             g_out_conv=m_g_out_conv, g_out_attn=m_g_out_attn, w_out=m_w_out, g_ffn=m_g_ffn, w_gate=m_w_gate,
             w_up=m_w_up, ffn_conv_w=m_ffn_conv_w, ffn_conv_b=m_ffn_conv_b, w_down=m_w_down, g_ple=m_g_ple,
             w_ple_gate=m_w_ple_gate, w_ple_proj=m_w_ple_proj)
    v = dict(g_mix=v_g_mix, w_in=v_w_in, conv_w=v_conv_w, conv_b=v_conv_b, q_norm_g=v_q_norm_g, k_norm_g=v_k_norm_g,
             g_out_conv=v_g_out_conv, g_out_attn=v_g_out_attn, w_out=v_w_out, g_ffn=v_g_ffn, w_gate=v_w_gate,
             w_up=v_w_up, ffn_conv_w=v_ffn_conv_w, ffn_conv_b=v_ffn_conv_b, w_down=v_w_down, g_ple=v_g_ple,
             w_ple_gate=v_w_ple_gate, w_ple_proj=v_w_ple_proj)
    mats = [k for k, a in w.items() if a.ndim == 3]
    w = {k: (a[0] if k in mats else a) for k, a in w.items()}
    m = {k: (a[0] if k in mats else a) for k, a in m.items()}
    v = {k: (a[0] if k in mats else a) for k, a in v.items()}
    for n in _TRANSPOSED:
        w[n], m[n], v[n] = w[n].T, m[n].T, v[n].T
    chip = 2 * lax.axis_index("x") + lax.axis_index("y")

    late = [w[n].astype(BF) for n in ("w_out", "w_gate", "w_up", "w_down", "w_ple_gate", "w_ple_proj")]
    pack = jnp.pad(jnp.concatenate([w["conv_w"], w["ffn_conv_w"]], axis=1), ((0, 5), (0, 1024 - 128 - D_FF_SHARD)))
    win4, pack4 = _gather_weights([w["w_in"].astype(BF)], pack)
    conv_w_full = pack4[:, 0:3, 0:128].transpose(1, 0, 2).reshape(3, CONV_W)
    fcw_full = pack4[:, 0:3, 128:128 + D_FF_SHARD].transpose(1, 0, 2).reshape(3, D_FF)
    wts = (win4, None, None, None, None, None, None, conv_w_full, fcw_full, w["g_mix"], w["conv_b"], w["q_norm_g"],
           w["k_norm_g"], w["g_out_conv"], w["g_out_attn"], w["g_ffn"], w["ffn_conv_b"], w["g_ple"])

    loss, grad_x, big, slots, small = _local_step(x[0], p[0, 0], loss_target[0], wts, late)

    grads, deltas, new_m, new_v = {}, {}, {}, {}
    for name in _BIG:
        total = _finish_reduce(big[name], slots[name], _COL_SHARDED[name], "finish_" + name)
        grads[name], deltas[name], new_m[name], new_v[name] = _adamw_big(total, w[name], m[name], v[name],
                                                                         "adamw_" + name)
    tot = _unpack_small(_allreduce_small(_pack_small(small, loss)))
    loss = tot.pop("loss")
    tot["conv_w"] = lax.dynamic_slice_in_dim(tot["conv_w"], chip * 128, 128, axis=1)
    tot["ffn_conv_w"] = lax.dynamic_slice_in_dim(tot["ffn_conv_w"], chip * D_FF_SHARD, D_FF_SHARD, axis=1)
    names = [n for n in _WEIGHTS if n not in _BIG]
    d_s, m_s, v_s = _adamw_small([w[n] for n in names], [tot[n] for n in names], [m[n] for n in names],
                                 [v[n] for n in names])
    for j, n in enumerate(names):
        grads[n], deltas[n], new_m[n], new_v[n] = tot[n], d_s[j], m_s[j], v_s[j]

    out = [loss, grad_x[None]]
    for group in (grads, deltas, new_m, new_v):
        for n in _TRANSPOSED:
            group[n] = group[n].T
        out += [group[n][None] if n in mats else group[n] for n in _WEIGHTS]
    return tuple(out)
```
